```python
import jax, jax.numpy as jnp
from jax import lax
import numpy as np

D_MODEL = 1024
BATCH = 8
SEQ = 8192
DEPTH = 1

MIX_WIDTH = D_MODEL
CONV_CH = MIX_WIDTH // 2
POOL_WIDTH = MIX_WIDTH - CONV_CH
CONV_K = 3
POOL_WINDOWS = (2, 4, 8, 16)
N_POOL_GROUPS = len(POOL_WINDOWS)
POOL_GROUP_DIM = POOL_WIDTH // N_POOL_GROUPS
IN_PROJ_WIDTH = 3 * CONV_CH + POOL_WIDTH
D_FF = 4 * D_MODEL
N_MOD = 6
LN_EPS = 1e-5
DEEPNORM_ALPHA = (2.0 * DEPTH) ** 0.25
DEEPNORM_BETA = (8.0 * DEPTH) ** -0.25
ADA_INIT = 0.25

kernel_name = "hybrid_conv_pool_sqrelu_deepnorm_adaln"


def _layer_norm(x, g, b):
    xf = x.astype(jnp.float32)
    mu = jnp.mean(xf, axis=-1, keepdims=True)
    var = jnp.mean(jnp.square(xf - mu), axis=-1, keepdims=True)
    y = (xf - mu) * lax.rsqrt(var + LN_EPS) * g.astype(jnp.float32) + b.astype(jnp.float32)
    return y.astype(x.dtype)


def _short_conv(u, w):
    s = u.shape[1]
    up = jnp.pad(u, ((0, 0), (CONV_K - 1, 0), (0, 0)))
    y = up[:, 0:s] * w[0]
    for k in range(1, CONV_K):
        y = y + up[:, k:k + s] * w[k]
    return y


def _multiscale_pool(u, w_pool, pool_scale):
    b, s, _ = u.shape
    uf = u.astype(jnp.float32)
    cs = jnp.cumsum(uf, axis=1)
    pos = jnp.arange(1, s + 1, dtype=jnp.float32)[None, :, None]
    outs = []
    for gi, win in enumerate(POOL_WINDOWS):
        sl = slice(gi * POOL_GROUP_DIM, (gi + 1) * POOL_GROUP_DIM)
        cs_g = cs[..., sl]
        prev = jnp.pad(cs_g, ((0, 0), (win, 0), (0, 0)))[:, :s]
        mean = (cs_g - prev) / jnp.minimum(pos, float(win))
        outs.append(mean - uf[..., sl])
    p = jnp.stack(outs, axis=2)
    p = jnp.einsum("bsgc,gcd->bsgd", p, w_pool.astype(jnp.float32))
    p = p.reshape(b, s, POOL_WIDTH) * pool_scale.astype(jnp.float32)
    return p.astype(u.dtype)


def _fwd_setup_inputs(seed: int = 0) -> dict:
    key = jax.random.key(seed)
    ks = jax.random.split(key, 16)
    f32 = jnp.float32
    x = jax.random.normal(ks[0], (BATCH, SEQ, D_MODEL), f32)
    c = jax.random.normal(ks[1], (BATCH, D_MODEL), f32)
    w_ada = jax.random.normal(ks[2], (DEPTH, D_MODEL, N_MOD * D_MODEL), f32) * (ADA_INIT * D_MODEL ** -0.5)
    b_ada = 0.02 * jax.random.normal(ks[3], (DEPTH, N_MOD * D_MODEL), f32)
    w_in = jax.random.normal(ks[4], (DEPTH, D_MODEL, IN_PROJ_WIDTH), f32) * D_MODEL ** -0.5
    conv_w = jax.random.normal(ks[5], (DEPTH, CONV_K, CONV_CH), f32) * CONV_K ** -0.5
    w_pool = jax.random.normal(ks[6], (DEPTH, N_POOL_GROUPS, POOL_GROUP_DIM, POOL_GROUP_DIM), f32) * POOL_GROUP_DIM ** -0.5
    pool_scale = 1.0 + 0.1 * jax.random.normal(ks[7], (DEPTH, POOL_WIDTH), f32)
    w_out = jax.random.normal(ks[8], (DEPTH, MIX_WIDTH, D_MODEL), f32) * (DEEPNORM_BETA * MIX_WIDTH ** -0.5)
    ln1_g = 1.0 + 0.02 * jax.random.normal(ks[9], (DEPTH, D_MODEL), f32)
    ln1_b = 0.02 * jax.random.normal(ks[10], (DEPTH, D_MODEL), f32)
    w_mlp_in = jax.random.normal(ks[11], (DEPTH, D_MODEL, D_FF), f32) * D_MODEL ** -0.5
    w_mlp_out = jax.random.normal(ks[12], (DEPTH, D_FF, D_MODEL), f32) * (DEEPNORM_BETA * D_FF ** -0.5)
    ln2_g = 1.0 + 0.02 * jax.random.normal(ks[13], (DEPTH, D_MODEL), f32)
    ln2_b = 0.02 * jax.random.normal(ks[14], (DEPTH, D_MODEL), f32)
    return {"x": x, "c": c, "w_ada": w_ada, "b_ada": b_ada, "w_in": w_in, "conv_w": conv_w,
            "w_pool": w_pool, "pool_scale": pool_scale, "w_out": w_out, "ln1_g": ln1_g,
            "ln1_b": ln1_b, "w_mlp_in": w_mlp_in, "w_mlp_out": w_mlp_out, "ln2_g": ln2_g,
            "ln2_b": ln2_b}


def _fwd_reference(x, c, w_ada, b_ada, w_in, conv_w, w_pool, pool_scale, w_out, ln1_g, ln1_b,
              w_mlp_in, w_mlp_out, ln2_g, ln2_b):
    cond = jax.nn.silu(c)
    for l in range(DEPTH):
        mod = (cond @ w_ada[l] + b_ada[l])[:, None, :]
        sh1, sc1, g1, sh2, sc2, g2 = jnp.split(mod, N_MOD, axis=-1)

        h = x * (1.0 + sc1) + sh1
        z = h @ w_in[l]
        gate_b, gate_c, v_conv, v_pool = jnp.split(
            z, [CONV_CH, 2 * CONV_CH, 3 * CONV_CH], axis=-1)
        y_conv = gate_b * _short_conv(gate_c * v_conv, conv_w[l])
        y_pool = _multiscale_pool(v_pool, w_pool[l], pool_scale[l])
        mix = jnp.concatenate([y_conv, y_pool], axis=-1) @ w_out[l]
        x = _layer_norm(DEEPNORM_ALPHA * x + (1.0 + g1) * mix, ln1_g[l], ln1_b[l])

        h = x * (1.0 + sc2) + sh2
        f = jnp.square(jax.nn.relu(h @ w_mlp_in[l])) @ w_mlp_out[l]
        x = _layer_norm(DEEPNORM_ALPHA * x + (1.0 + g2) * f, ln2_g[l], ln2_b[l])
    return x


import jax as _jax
import jax.numpy as _jnp

TWIN_FORMAT = 'train_step'
FWD_PARAMS = ['x', 'c', 'w_ada', 'b_ada', 'w_in', 'conv_w', 'w_pool', 'pool_scale', 'w_out', 'ln1_g', 'ln1_b', 'w_mlp_in', 'w_mlp_out', 'ln2_g', 'ln2_b']
TWIN_WEIGHTS = ['w_ada', 'b_ada', 'w_in', 'conv_w', 'w_pool', 'pool_scale', 'w_out', 'ln1_g', 'ln1_b', 'w_mlp_in', 'w_mlp_out', 'ln2_g', 'ln2_b']
TWIN_DIFF_INPUT = 'x'
TWIN_INPUTS = ['x', 'c', 'w_ada', 'b_ada', 'w_in', 'conv_w', 'w_pool', 'pool_scale', 'w_out', 'ln1_g', 'ln1_b', 'w_mlp_in', 'w_mlp_out', 'ln2_g', 'ln2_b', 'loss_target', 'm_w_ada', 'm_b_ada', 'm_w_in', 'm_conv_w', 'm_w_pool', 'm_pool_scale', 'm_w_out', 'm_ln1_g', 'm_ln1_b', 'm_w_mlp_in', 'm_w_mlp_out', 'm_ln2_g', 'm_ln2_b', 'v_w_ada', 'v_b_ada', 'v_w_in', 'v_conv_w', 'v_w_pool', 'v_pool_scale', 'v_w_out', 'v_ln1_g', 'v_ln1_b', 'v_w_mlp_in', 'v_w_mlp_out', 'v_ln2_g', 'v_ln2_b']
TWIN_OUTPUTS = ['loss', 'grad_x', 'grad_w_ada', 'grad_b_ada', 'grad_w_in', 'grad_conv_w', 'grad_w_pool', 'grad_pool_scale', 'grad_w_out', 'grad_ln1_g', 'grad_ln1_b', 'grad_w_mlp_in', 'grad_w_mlp_out', 'grad_ln2_g', 'grad_ln2_b', 'delta_w_ada', 'delta_b_ada', 'delta_w_in', 'delta_conv_w', 'delta_w_pool', 'delta_pool_scale', 'delta_w_out', 'delta_ln1_g', 'delta_ln1_b', 'delta_w_mlp_in', 'delta_w_mlp_out', 'delta_ln2_g', 'delta_ln2_b', 'new_m_w_ada', 'new_m_b_ada', 'new_m_w_in', 'new_m_conv_w', 'new_m_w_pool', 'new_m_pool_scale', 'new_m_w_out', 'new_m_ln1_g', 'new_m_ln1_b', 'new_m_w_mlp_in', 'new_m_w_mlp_out', 'new_m_ln2_g', 'new_m_ln2_b', 'new_v_w_ada', 'new_v_b_ada', 'new_v_w_in', 'new_v_conv_w', 'new_v_w_pool', 'new_v_pool_scale', 'new_v_w_out', 'new_v_ln1_g', 'new_v_ln1_b', 'new_v_w_mlp_in', 'new_v_w_mlp_out', 'new_v_ln2_g', 'new_v_ln2_b']
TWIN_LEAF_KINDS = {'loss': 'loss', 'grad_x': 'grad_x', 'grad_w_ada': 'grad_w', 'grad_b_ada': 'grad_w', 'grad_w_in': 'grad_w', 'grad_conv_w': 'grad_w', 'grad_w_pool': 'grad_w', 'grad_pool_scale': 'grad_w', 'grad_w_out': 'grad_w', 'grad_ln1_g': 'grad_w', 'grad_ln1_b': 'grad_w', 'grad_w_mlp_in': 'grad_w', 'grad_w_mlp_out': 'grad_w', 'grad_ln2_g': 'grad_w', 'grad_ln2_b': 'grad_w', 'delta_w_ada': 'delta_w', 'delta_b_ada': 'delta_w', 'delta_w_in': 'delta_w', 'delta_conv_w': 'delta_w', 'delta_w_pool': 'delta_w', 'delta_pool_scale': 'delta_w', 'delta_w_out': 'delta_w', 'delta_ln1_g': 'delta_w', 'delta_ln1_b': 'delta_w', 'delta_w_mlp_in': 'delta_w', 'delta_w_mlp_out': 'delta_w', 'delta_ln2_g': 'delta_w', 'delta_ln2_b': 'delta_w', 'new_m_w_ada': 'new_m', 'new_m_b_ada': 'new_m', 'new_m_w_in': 'new_m', 'new_m_conv_w': 'new_m', 'new_m_w_pool': 'new_m', 'new_m_pool_scale': 'new_m', 'new_m_w_out': 'new_m', 'new_m_ln1_g': 'new_m', 'new_m_ln1_b': 'new_m', 'new_m_w_mlp_in': 'new_m', 'new_m_w_mlp_out': 'new_m', 'new_m_ln2_g': 'new_m', 'new_m_ln2_b': 'new_m', 'new_v_w_ada': 'new_v', 'new_v_b_ada': 'new_v', 'new_v_w_in': 'new_v', 'new_v_conv_w': 'new_v', 'new_v_w_pool': 'new_v', 'new_v_pool_scale': 'new_v', 'new_v_w_out': 'new_v', 'new_v_ln1_g': 'new_v', 'new_v_ln1_b': 'new_v', 'new_v_w_mlp_in': 'new_v', 'new_v_w_mlp_out': 'new_v', 'new_v_ln2_g': 'new_v', 'new_v_ln2_b': 'new_v'}


def _forward(args):
    return _fwd_reference(*[args[k] for k in FWD_PARAMS])


def _output_shape():
    def fwd():
        inp = _fwd_setup_inputs(0)
        return _fwd_reference(*[inp[k] for k in FWD_PARAMS])
    out = _jax.eval_shape(fwd)
    return out.shape, out.dtype

N_MICROBATCH = 1
ADAM_LR = 0.001
ADAM_B1 = 0.9
ADAM_B2 = 0.999
ADAM_EPS = 1e-08
ADAM_WD = 0.01
ADAM_STEP = 10
PER_EXAMPLE_BATCH_AXIS = {'x': 0, 'c': 0, 'loss_target': 0}
SHARED_INPUTS = []
_WEIGHT_DTYPES = {'w_ada': _jnp.float32, 'b_ada': _jnp.float32, 'w_in': _jnp.float32, 'conv_w': _jnp.float32, 'w_pool': _jnp.float32, 'pool_scale': _jnp.float32, 'w_out': _jnp.float32, 'ln1_g': _jnp.float32, 'ln1_b': _jnp.float32, 'w_mlp_in': _jnp.float32, 'w_mlp_out': _jnp.float32, 'ln2_g': _jnp.float32, 'ln2_b': _jnp.float32}
MOMENT_SCALE = {'w_ada': 1.127921e-01, 'b_ada': 3.000034e-01, 'w_in': 1.233678e-01, 'conv_w': 1.305627e-01, 'w_pool': 1.067680e-01, 'pool_scale': 1.172907e-01, 'w_out': 1.994422e-01, 'ln1_g': 1.618598e+00, 'ln1_b': 9.439407e-01, 'w_mlp_in': 7.979776e-02, 'w_mlp_out': 3.925598e-01, 'ln2_g': 6.420851e+01, 'ln2_b': 1.429843e+01}


def _to_microbatches(a, axis):
    t = _jnp.moveaxis(a, axis, 0)
    t = t.reshape((N_MICROBATCH, t.shape[0] // N_MICROBATCH) + t.shape[1:])
    return _jnp.moveaxis(t, 1, axis + 1)


def setup_inputs(seed: int = 0) -> dict:
    inp = _fwd_setup_inputs(seed)
    key = _jax.random.fold_in(_jax.random.key(seed), 7919)
    shape, _ = _output_shape()
    out = dict(inp)
    out["loss_target"] = _jax.random.normal(_jax.random.fold_in(key, 0), shape, _jnp.float32)
    for i, name in enumerate(TWIN_WEIGHTS):
        w = inp[name].astype(_jnp.float32)
        if MOMENT_SCALE is None:
            s = _jnp.sqrt(_jnp.mean(_jnp.square(w)) + 1e-30)
        else:
            s = MOMENT_SCALE[name]
        km, kv = _jax.random.split(_jax.random.fold_in(key, i + 1))
        out[name] = w
        out["m_" + name] = s * _jax.random.normal(km, w.shape, _jnp.float32)
        out["v_" + name] = (s * s) * _jax.random.uniform(kv, w.shape, _jnp.float32, 0.5, 1.5)
    if N_MICROBATCH > 1:
        for name, axis in PER_EXAMPLE_BATCH_AXIS.items():
            out[name] = _to_microbatches(out[name], axis)
    return {'x': out['x'], 'c': out['c'], 'w_ada': out['w_ada'], 'b_ada': out['b_ada'], 'w_in': out['w_in'], 'conv_w': out['conv_w'], 'w_pool': out['w_pool'], 'pool_scale': out['pool_scale'], 'w_out': out['w_out'], 'ln1_g': out['ln1_g'], 'ln1_b': out['ln1_b'], 'w_mlp_in': out['w_mlp_in'], 'w_mlp_out': out['w_mlp_out'], 'ln2_g': out['ln2_g'], 'ln2_b': out['ln2_b'], 'loss_target': out['loss_target'], 'm_w_ada': out['m_w_ada'], 'm_b_ada': out['m_b_ada'], 'm_w_in': out['m_w_in'], 'm_conv_w': out['m_conv_w'], 'm_w_pool': out['m_w_pool'], 'm_pool_scale': out['m_pool_scale'], 'm_w_out': out['m_w_out'], 'm_ln1_g': out['m_ln1_g'], 'm_ln1_b': out['m_ln1_b'], 'm_w_mlp_in': out['m_w_mlp_in'], 'm_w_mlp_out': out['m_w_mlp_out'], 'm_ln2_g': out['m_ln2_g'], 'm_ln2_b': out['m_ln2_b'], 'v_w_ada': out['v_w_ada'], 'v_b_ada': out['v_b_ada'], 'v_w_in': out['v_w_in'], 'v_conv_w': out['v_conv_w'], 'v_w_pool': out['v_w_pool'], 'v_pool_scale': out['v_pool_scale'], 'v_w_out': out['v_w_out'], 'v_ln1_g': out['v_ln1_g'], 'v_ln1_b': out['v_ln1_b'], 'v_w_mlp_in': out['v_w_mlp_in'], 'v_w_mlp_out': out['v_w_mlp_out'], 'v_ln2_g': out['v_ln2_g'], 'v_ln2_b': out['v_ln2_b']}


def _loss(weights, diff, rest, loss_target):
    with _jax.named_scope("forward"):
        args = {**rest, TWIN_DIFF_INPUT: diff, **{k: w.astype(_WEIGHT_DTYPES[k]) for k, w in weights.items()}}
        y = _forward(args)
    with _jax.named_scope("loss_head"):
        err = _jnp.square(y.astype(_jnp.float32) - loss_target)
        return 0.5 * _jnp.sum(_jnp.mean(err, axis=-1)) if err.ndim else 0.5 * err


def _adamw(w, g, m, v):
    m = ADAM_B1 * m + (1.0 - ADAM_B1) * g
    v = ADAM_B2 * v + (1.0 - ADAM_B2) * _jnp.square(g)
    m_hat = m / (1.0 - ADAM_B1 ** ADAM_STEP)
    v_hat = v / (1.0 - ADAM_B2 ** ADAM_STEP)
    delta = -ADAM_LR * (m_hat / (_jnp.sqrt(v_hat) + ADAM_EPS) + ADAM_WD * w)
    return delta, m, v


def reference(x, c, w_ada, b_ada, w_in, conv_w, w_pool, pool_scale, w_out, ln1_g, ln1_b, w_mlp_in, w_mlp_out, ln2_g, ln2_b, loss_target, m_w_ada, m_b_ada, m_w_in, m_conv_w, m_w_pool, m_pool_scale, m_w_out, m_ln1_g, m_ln1_b, m_w_mlp_in, m_w_mlp_out, m_ln2_g, m_ln2_b, v_w_ada, v_b_ada, v_w_in, v_conv_w, v_w_pool, v_pool_scale, v_w_out, v_ln1_g, v_ln1_b, v_w_mlp_in, v_w_mlp_out, v_ln2_g, v_ln2_b):
    given = dict(x=x, c=c, w_ada=w_ada, b_ada=b_ada, w_in=w_in, conv_w=conv_w, w_pool=w_pool, pool_scale=pool_scale, w_out=w_out, ln1_g=ln1_g, ln1_b=ln1_b, w_mlp_in=w_mlp_in, w_mlp_out=w_mlp_out, ln2_g=ln2_g, ln2_b=ln2_b, loss_target=loss_target, m_w_ada=m_w_ada, m_b_ada=m_b_ada, m_w_in=m_w_in, m_conv_w=m_conv_w, m_w_pool=m_w_pool, m_pool_scale=m_pool_scale, m_w_out=m_w_out, m_ln1_g=m_ln1_g, m_ln1_b=m_ln1_b, m_w_mlp_in=m_w_mlp_in, m_w_mlp_out=m_w_mlp_out, m_ln2_g=m_ln2_g, m_ln2_b=m_ln2_b, v_w_ada=v_w_ada, v_b_ada=v_b_ada, v_w_in=v_w_in, v_conv_w=v_conv_w, v_w_pool=v_w_pool, v_pool_scale=v_pool_scale, v_w_out=v_w_out, v_ln1_g=v_ln1_g, v_ln1_b=v_ln1_b, v_w_mlp_in=v_w_mlp_in, v_w_mlp_out=v_w_mlp_out, v_ln2_g=v_ln2_g, v_ln2_b=v_ln2_b)
    weights = {n: given[n] for n in TWIN_WEIGHTS}
    shared = {n: given[n] for n in SHARED_INPUTS}
    per_example = {n: given[n] for n in ['x', 'c']}
    grad_fn = _jax.value_and_grad(_loss, argnums=(0, 1))

    def one_microbatch(ex, loss_target):
        ex = dict(ex)
        diff = ex.pop(TWIN_DIFF_INPUT)
        return grad_fn(weights, diff, {**shared, **ex}, loss_target)

    if N_MICROBATCH == 1:
        loss, (grad_w, grad_x) = one_microbatch(per_example, given["loss_target"])
    else:
        def body(carry, xs):
            loss_sum, grad_sum = carry
            l_k, (gw_k, gx_k) = one_microbatch(xs[0], xs[1])
            with _jax.named_scope("update"):
                return (loss_sum + l_k, _jax.tree.map(_jnp.add, grad_sum, gw_k)), gx_k

        init = (_jnp.zeros((), _jnp.float32), _jax.tree.map(_jnp.zeros_like, weights))
        (loss, grad_w), grad_x = _jax.lax.scan(body, init, (per_example, given["loss_target"]))
    with _jax.named_scope("update"):
        delta_w, new_m, new_v = {}, {}, {}
        for n in TWIN_WEIGHTS:
            delta_w[n], new_m[n], new_v[n] = _adamw(weights[n], grad_w[n], given["m_" + n], given["v_" + n])
    return (loss, grad_x, *[grad_w[n] for n in TWIN_WEIGHTS], *[delta_w[n] for n in TWIN_WEIGHTS],
            *[new_m[n] for n in TWIN_WEIGHTS], *[new_v[n] for n in TWIN_WEIGHTS])
```

```python
import functools

import jax
import jax.numpy as jnp
from jax import lax
from jax.experimental import pallas as pl
from jax.experimental.pallas import tpu as pltpu

F32 = jnp.float32
BF16 = jnp.bfloat16
MESH = pl.DeviceIdType.MESH
N_DEV = 8

LN_EPS = 1e-5
DEPTH = 1
DEEPNORM_ALPHA = (2.0 * DEPTH) ** 0.25
POOL_WINDOWS = (2, 4, 8, 16)
HALO = 16

ADAM_LR = 0.001
ADAM_B1 = 0.9
ADAM_B2 = 0.999
ADAM_EPS = 1e-08
ADAM_WD = 0.01
ADAM_STEP = 10

VMEM_LIMIT = 56 * 1024 * 1024

VMEM_SPEC = pl.BlockSpec(memory_space=pltpu.VMEM)
ANY_SPEC = pl.BlockSpec(memory_space=pl.ANY)

NT = (((1,), (1,)), ((), ()))
TN = (((0,), (0,)), ((), ()))


def _my_place():
    return lax.axis_index("x"), lax.axis_index("y"), lax.axis_index("c")


def _slot(x, y, c):
    return 4 * x + 2 * y + c


def _allgather(shards, name, in_vmem):
    n = len(shards)

    def body(*refs):
        ins, outs = refs[:n], refs[n : 2 * n]
        send_sems, recv_sems, local_sems = refs[2 * n :]
        x, y, c = _my_place()
        me, sibling = (x, y, c), (x, y, 1 - c)
        chips = [(1 - x, y), (x, 1 - y), (1 - x, 1 - y)]

        def copy(a, k, block, to, src=None):
            dst = outs[a].at[_slot(*block)]
            return pltpu.make_async_remote_copy(
                src_ref=dst if src is None else src,
                dst_ref=dst,
                send_sem=send_sems.at[7 * a + k],
                recv_sem=recv_sems.at[7 * a + k],
                device_id=to,
                device_id_type=MESH,
            )

        started = []
        for a in range(n):
            mine = pltpu.make_async_copy(ins[a], outs[a].at[_slot(*me)], local_sems.at[a])
            mine.start()
            started.append(mine)
        first = []
        for a in range(n):
            first.append(copy(a, 0, me, sibling, src=ins[a]))
            first += [copy(a, 1 + j, me, (*chip, c), src=ins[a]) for j, chip in enumerate(chips)]
        for cp in first:
            cp.start()
        passed = []
        for j, chip in enumerate(chips):
            for a in range(n):
                copy(a, 1 + j, (*chip, c), me).wait_recv()
                fwd = copy(a, 4 + j, (*chip, c), sibling)
                fwd.start()
                passed.append(fwd)
        for a in range(n):
            copy(a, 0, sibling, me).wait_recv()
            for j, chip in enumerate(chips):
                copy(a, 4 + j, (*chip, 1 - c), me).wait_recv()
        for cp in first + passed:
            cp.wait_send()
        for mine in started:
            mine.wait()

    spec = VMEM_SPEC if in_vmem else ANY_SPEC
    return pl.pallas_call(
        body,
        name=name,
        out_shape=[jax.ShapeDtypeStruct((N_DEV, *s.shape), s.dtype) for s in shards],
        in_specs=[spec] * n,
        out_specs=[spec] * n,
        scratch_shapes=[
            pltpu.SemaphoreType.DMA((7 * n,)),
            pltpu.SemaphoreType.DMA((7 * n,)),
            pltpu.SemaphoreType.DMA((n,)),
        ],
        compiler_params=pltpu.CompilerParams(vmem_limit_bytes=VMEM_LIMIT),
    )(*shards)


def _adamw_math(w, g, m, v):
    m = ADAM_B1 * m + (1.0 - ADAM_B1) * g
    v = ADAM_B2 * v + (1.0 - ADAM_B2) * (g * g)
    m_hat = m / (1.0 - ADAM_B1**ADAM_STEP)
    v_hat = v / (1.0 - ADAM_B2**ADAM_STEP)
    delta = -ADAM_LR * (m_hat / (jnp.sqrt(v_hat) + ADAM_EPS) + ADAM_WD * w)
    return delta, m, v


ROW_CHUNK = 64


def _reduce_adamw(gparts, w, m, v, name):
    _, rows, cols = gparts.shape
    n_chunks = rows // ROW_CHUNK

    def body(g_hbm, w_ref, m_ref, v_ref, grad_ref, delta_ref, nm_ref, nv_ref, recv, send_sems, recv_sems, local_sem):
        x, y, c = _my_place()
        me = _slot(x, y, c)
        own = pltpu.make_async_copy(g_hbm.at[me], recv.at[me], local_sem)
        own.start()

        def copy(k):
            px, py, pc = x ^ (k >> 2), y ^ ((k >> 1) & 1), c ^ (k & 1)
            return pltpu.make_async_remote_copy(
                src_ref=g_hbm.at[_slot(px, py, pc)],
                dst_ref=recv.at[me],
                send_sem=send_sems.at[k - 1],
                recv_sem=recv_sems.at[k - 1],
                device_id=(px, py, pc),
                device_id_type=MESH,
            )

        copies = [copy(k) for k in range(1, N_DEV)]
        for cp in copies:
            cp.start()
        own.wait()
        for cp in copies:
            cp.wait_recv()

        def step(r, carry):
            rs = pl.ds(pl.multiple_of(r * ROW_CHUNK, ROW_CHUNK), ROW_CHUNK)
            g = recv[0, rs, :].astype(F32)
            for k in range(1, N_DEV):
                g = g + recv[k, rs, :].astype(F32)
            delta, nm, nv = _adamw_math(w_ref[0, rs, :], g, m_ref[0, rs, :], v_ref[0, rs, :])
            grad_ref[0, rs, :] = g
            delta_ref[0, rs, :] = delta
            nm_ref[0, rs, :] = nm
            nv_ref[0, rs, :] = nv
            return carry

        lax.fori_loop(0, n_chunks, step, 0)
        for cp in copies:
            cp.wait_send()

    out = jax.ShapeDtypeStruct(w.shape, F32)
    return pl.pallas_call(
        body,
        name=name,
        out_shape=[out] * 4,
        in_specs=[ANY_SPEC, VMEM_SPEC, VMEM_SPEC, VMEM_SPEC],
        out_specs=[VMEM_SPEC] * 4,
        scratch_shapes=[
            pltpu.VMEM((N_DEV, rows, cols), BF16),
            pltpu.SemaphoreType.DMA((N_DEV - 1,)),
            pltpu.SemaphoreType.DMA((N_DEV - 1,)),
            pltpu.SemaphoreType.DMA,
        ],
        compiler_params=pltpu.CompilerParams(vmem_limit_bytes=VMEM_LIMIT),
    )(gparts, w, m, v)


def _adamw_multi(items, name):
    n = len(items)

    def body(*refs):
        ins, outs = refs[: 4 * n], refs[4 * n :]
        for a in range(n):
            w_ref, g_ref, m_ref, v_ref = ins[4 * a : 4 * a + 4]
            d_ref, nm_ref, nv_ref = outs[3 * a : 3 * a + 3]
            shape = w_ref.shape
            if len(shape) == 3 and shape[1] % ROW_CHUNK == 0 and shape[1] > ROW_CHUNK:

                def step(r, carry, w_ref=w_ref, g_ref=g_ref, m_ref=m_ref, v_ref=v_ref, d_ref=d_ref, nm_ref=nm_ref, nv_ref=nv_ref):
                    rs = pl.ds(pl.multiple_of(r * ROW_CHUNK, ROW_CHUNK), ROW_CHUNK)
                    delta, nm, nv = _adamw_math(w_ref[0, rs, :], g_ref[0, rs, :], m_ref[0, rs, :], v_ref[0, rs, :])
                    d_ref[0, rs, :] = delta
                    nm_ref[0, rs, :] = nm
                    nv_ref[0, rs, :] = nv
                    return carry

                lax.fori_loop(0, shape[1] // ROW_CHUNK, step, 0)
            else:
                delta, nm, nv = _adamw_math(w_ref[...], g_ref[...], m_ref[...], v_ref[...])
                d_ref[...] = delta
                nm_ref[...] = nm
                nv_ref[...] = nv

    flat = [a for it in items for a in it]
    out_shape = [jax.ShapeDtypeStruct(it[0].shape, F32) for it in items for _ in range(3)]
    outs = pl.pallas_call(
        body,
        name=name,
        out_shape=out_shape,
        in_specs=[VMEM_SPEC] * (4 * n),
        out_specs=[VMEM_SPEC] * (3 * n),
        compiler_params=pltpu.CompilerParams(vmem_limit_bytes=VMEM_LIMIT),
    )(*flat)
    return [tuple(outs[3 * a : 3 * a + 3]) for a in range(n)]


def _ada_fwd(c_all, w_ada, b_mine):
    def body(c_ref, w_ref, b_ref, o_ref):
        cv = c_ref[...]
        cond = cv * jax.nn.sigmoid(cv)
        o_ref[...] = jnp.dot(cond, w_ref[0], precision=lax.Precision.HIGHEST, preferred_element_type=F32) + b_ref[...]

    return pl.pallas_call(
        body,
        name="ada_fwd",
        out_shape=jax.ShapeDtypeStruct((N_DEV, w_ada.shape[2]), F32),
        in_specs=[VMEM_SPEC] * 3,
        out_specs=VMEM_SPEC,
        compiler_params=pltpu.CompilerParams(vmem_limit_bytes=VMEM_LIMIT),
    )(c_all, w_ada, b_mine)


def _ln_fwd(r):
    mu = jnp.mean(r, axis=-1, keepdims=True)
    d = r - mu
    var = jnp.mean(d * d, axis=-1, keepdims=True)
    rstd = lax.rsqrt(var + LN_EPS)
    return d * rstd, rstd


def _ln_bwd(dxh, xhat, rstd):
    m1 = jnp.mean(dxh, axis=-1, keepdims=True)
    m2 = jnp.mean(dxh * xhat, axis=-1, keepdims=True)
    return rstd * (dxh - m1 - xhat * m2)


def _colsum(a):
    return jnp.sum(a, axis=0, keepdims=True)


def _pool_features(vp, vp_s, row, tm):
    feats, inv_cnts = [], []
    for g, win in enumerate(POOL_WINDOWS):
        cols = slice(128 * g, 128 * g + 128)
        s = vp[:, cols]
        for j in range(1, win):
            s = s + vp_s[HALO - j : HALO - j + tm, cols]
        inv_cnt = 1.0 / jnp.minimum(row + 1, win).astype(F32)
        feats.append(s * inv_cnt - vp[:, cols])
        inv_cnts.append(inv_cnt)
    return feats, inv_cnts


def _f1(x, mod, w_in, conv_w, w_pool, pool_scale, w_out, tm):
    T, D = x.shape
    ZW = w_in.shape[1]
    CC = ZW // 4
    nt = T // tm

    def body(x_ref, mod_ref, win_ref, cw_ref, wp_ref, ps_ref, wout_ref, z_ref, h_ref, xhat_ref, rstd_ref, mix_ref, cv_s, vp_s):
        i = pl.program_id(0)

        @pl.when(i == 0)
        def _():
            cv_s[0:HALO, :] = jnp.zeros((HALO, CC), F32)
            vp_s[0:HALO, :] = jnp.zeros((HALO, CC), F32)

        xv = x_ref[...]
        sh1, sc1, g1 = mod_ref[0:1, :], mod_ref[1:2, :], mod_ref[2:3, :]
        h = (xv * (1.0 + sc1) + sh1).astype(BF16)
        h_ref[...] = h
        z = jnp.dot(h, win_ref[...], preferred_element_type=F32)
        z_ref[...] = z.astype(BF16)
        gb, gc, vc, vp = z[:, 0:CC], z[:, CC : 2 * CC], z[:, 2 * CC : 3 * CC], z[:, 3 * CC : 4 * CC]
        cv = gc * vc
        cv_s[HALO : HALO + tm, :] = cv
        vp_s[HALO : HALO + tm, :] = vp
        conv = cw_ref[0:1, :] * cv_s[HALO - 2 : HALO - 2 + tm, :] + cw_ref[1:2, :] * cv_s[HALO - 1 : HALO - 1 + tm, :] + cw_ref[2:3, :] * cv
        parts = [gb * conv]
        row = i * tm + lax.broadcasted_iota(jnp.int32, (tm, 1), 0)
        feats, _ = _pool_features(vp, vp_s, row, tm)
        for g in range(len(POOL_WINDOWS)):
            pw = jnp.dot(feats[g].astype(BF16), wp_ref[g], preferred_element_type=F32)
            parts.append(pw * ps_ref[0:1, 128 * g : 128 * g + 128])
        cv_s[0:HALO, :] = cv_s[tm : tm + HALO, :]
        vp_s[0:HALO, :] = vp_s[tm : tm + HALO, :]
        ycat = jnp.concatenate(parts, axis=1).astype(BF16)
        mix = jnp.dot(ycat, wout_ref[...], preferred_element_type=F32)
        mix_ref[...] = mix
        xhat, rstd = _ln_fwd(DEEPNORM_ALPHA * xv + (1.0 + g1) * mix)
        xhat_ref[...] = xhat
        rstd_ref[...] = rstd

    tile = lambda w: pl.BlockSpec((tm, w), lambda i: (i, 0))
    return pl.pallas_call(
        body,
        name="f1",
        grid=(nt,),
        out_shape=[
            jax.ShapeDtypeStruct((T, ZW), BF16),
            jax.ShapeDtypeStruct((T, D), BF16),
            jax.ShapeDtypeStruct((T, D), F32),
            jax.ShapeDtypeStruct((T, 1), F32),
            jax.ShapeDtypeStruct((T, D), F32),
        ],
        in_specs=[tile(D)] + [VMEM_SPEC] * 6,
        out_specs=[tile(ZW), tile(D), tile(D), tile(1), tile(D)],
        scratch_shapes=[pltpu.VMEM((HALO + tm, CC), F32), pltpu.VMEM((HALO + tm, CC), F32)],
        compiler_params=pltpu.CompilerParams(dimension_semantics=("arbitrary",), vmem_limit_bytes=VMEM_LIMIT),
    )(x, mod, w_in, conv_w, w_pool, pool_scale, w_out)


def _fb2(xhat1, target, mod, ln, w_mi, w_mo, tm):
    T, D = xhat1.shape
    H = w_mi.shape[1]
    nt = T // tm
    hc = min(1024, H)

    def body(xh1_ref, t_ref, mod_ref, ln_ref, wmi_ref, wmo_ref, dx1_ref, h2_ref, a_ref, du_ref, df_ref, acc_ref):
        i = pl.program_id(0)

        @pl.when(i == 0)
        def _():
            acc_ref[...] = jnp.zeros((8, D), F32)

        sh2, sc2, g2 = mod_ref[3:4, :], mod_ref[4:5, :], mod_ref[5:6, :]
        x1 = xh1_ref[...] * ln_ref[0:1, :] + ln_ref[1:2, :]
        h2 = (x1 * (1.0 + sc2) + sh2).astype(BF16)
        h2_ref[...] = h2
        f = jnp.zeros((tm, D), F32)
        for k in range(H // hc):
            ks = slice(k * hc, (k + 1) * hc)
            r = jnp.maximum(jnp.dot(h2, wmi_ref[:, ks], preferred_element_type=F32), 0.0)
            du_ref[:, ks] = r.astype(BF16)
            a = (r * r).astype(BF16)
            a_ref[:, ks] = a
            f = f + jnp.dot(a, wmo_ref[ks, :], preferred_element_type=F32)
        xhat2, rstd2 = _ln_fwd(DEEPNORM_ALPHA * x1 + (1.0 + g2) * f)
        ln2_g = ln_ref[2:3, :]
        d = xhat2 * ln2_g + ln_ref[3:4, :] - t_ref[...]
        dy = d * (1.0 / D)
        dr2 = _ln_bwd(dy * ln2_g, xhat2, rstd2)
        df = ((1.0 + g2) * dr2).astype(BF16)
        df_ref[...] = df
        dh2 = jnp.zeros((tm, D), F32)
        for k in range(H // hc):
            ks = slice(k * hc, (k + 1) * hc)
            da = lax.dot_general(df, wmo_ref[ks, :], NT, preferred_element_type=F32)
            du = (da * (2.0 * du_ref[:, ks].astype(F32))).astype(BF16)
            du_ref[:, ks] = du
            dh2 = dh2 + lax.dot_general(du, wmi_ref[:, ks], NT, preferred_element_type=F32)
        dx1_ref[...] = DEEPNORM_ALPHA * dr2 + dh2 * (1.0 + sc2)
        acc_ref[0:1, :] += _colsum(dy * xhat2)
        acc_ref[1:2, :] += _colsum(dy)
        acc_ref[2:3, :] += _colsum(dh2)
        acc_ref[3:4, :] += _colsum(dh2 * x1)
        acc_ref[4:5, :] += _colsum(dr2 * f)
        acc_ref[5:6, :] += jnp.zeros((1, D), F32) + (0.5 / D) * jnp.sum(d * d)

    tile = lambda w: pl.BlockSpec((tm, w), lambda i: (i, 0))
    return pl.pallas_call(
        body,
        name="fb2",
        grid=(nt,),
        out_shape=[
            jax.ShapeDtypeStruct((T, D), F32),
            jax.ShapeDtypeStruct((T, D), BF16),
            jax.ShapeDtypeStruct((T, H), BF16),
            jax.ShapeDtypeStruct((T, H), BF16),
            jax.ShapeDtypeStruct((T, D), BF16),
            jax.ShapeDtypeStruct((8, D), F32),
        ],
        in_specs=[tile(D), tile(D)] + [VMEM_SPEC] * 4,
        out_specs=[tile(D), tile(D), tile(H), tile(H), tile(D), pl.BlockSpec((8, D), lambda i: (0, 0))],
        compiler_params=pltpu.CompilerParams(dimension_semantics=("arbitrary",), vmem_limit_bytes=VMEM_LIMIT),
    )(xhat1, target, mod, ln, w_mi, w_mo)


def _b1(dx1, xhat1, rstd1, x, mix, z, mod, ln, w_out, w_in, conv_w, w_pool, pool_scale, tm):
    T, D = x.shape
    ZW = w_in.shape[1]
    CC = ZW // 4
    nt = T // tm
    hb = tm // HALO

    def body(dx1_ref, xh1_ref, rstd_ref, x_ref, mix_ref, z_ref, zh_ref, mod_ref, ln_ref, wout_ref, win_ref, cw_ref, wp_ref, ps_ref,
             dx_ref, dmix_ref, ycat_ref, dz_ref, acc_ref, gcw_ref, gwp_ref, cv_s, vp_s, e_s, q_s):
        i = pl.program_id(0)
        j = nt - 1 - i

        @pl.when(i == 0)
        def _():
            acc_ref[...] = jnp.zeros((8, D), F32)
            gcw_ref[...] = jnp.zeros((8, CC), F32)
            gwp_ref[...] = jnp.zeros(gwp_ref.shape, F32)
            e_s[tm : tm + HALO, :] = jnp.zeros((HALO, CC), F32)
            q_s[tm : tm + HALO, :] = jnp.zeros((HALO, CC), F32)

        sh1, sc1, g1 = mod_ref[0:1, :], mod_ref[1:2, :], mod_ref[2:3, :]
        dx1 = dx1_ref[...]
        xhat1 = xh1_ref[...]
        acc_ref[0:1, :] += _colsum(dx1 * xhat1)
        acc_ref[1:2, :] += _colsum(dx1)
        dr1 = _ln_bwd(dx1 * ln_ref[0:1, :], xhat1, rstd_ref[...])
        acc_ref[4:5, :] += _colsum(dr1 * mix_ref[...])
        dmix = ((1.0 + g1) * dr1).astype(BF16)
        dmix_ref[...] = dmix
        dycat = lax.dot_general(dmix, wout_ref[...], NT, preferred_element_type=F32)

        z = z_ref[...].astype(F32)
        zh = zh_ref[...].astype(F32) * jnp.where(j > 0, 1.0, 0.0)
        gb, gc, vc, vp = z[:, 0:CC], z[:, CC : 2 * CC], z[:, 2 * CC : 3 * CC], z[:, 3 * CC : 4 * CC]
        cv = gc * vc
        cv_s[0:HALO, :] = zh[:, CC : 2 * CC] * zh[:, 2 * CC : 3 * CC]
        cv_s[HALO : HALO + tm, :] = cv
        vp_s[0:HALO, :] = zh[:, 3 * CC : 4 * CC]
        vp_s[HALO : HALO + tm, :] = vp
        cv_m2 = cv_s[HALO - 2 : HALO - 2 + tm, :]
        cv_m1 = cv_s[HALO - 1 : HALO - 1 + tm, :]
        w0, w1, w2 = cw_ref[0:1, :], cw_ref[1:2, :], cw_ref[2:3, :]
        conv = w0 * cv_m2 + w1 * cv_m1 + w2 * cv
        dyc = dycat[:, 0:CC]
        e = dyc * gb
        e_s[0:tm, :] = e
        dcv = w2 * e + w1 * e_s[1 : 1 + tm, :] + w0 * e_s[2 : 2 + tm, :]
        gcw_ref[0:1, :] += _colsum(e * cv_m2)
        gcw_ref[1:2, :] += _colsum(e * cv_m1)
        gcw_ref[2:3, :] += _colsum(e * cv)
        y_parts = [gb * conv]
        dz_parts = [dyc * conv, dcv * vc, dcv * gc]

        row = j * tm + lax.broadcasted_iota(jnp.int32, (tm, 1), 0)
        feats, inv_cnts = _pool_features(vp, vp_s, row, tm)
        gps_parts = []
        for g, win in enumerate(POOL_WINDOWS):
            cols = slice(128 * g, 128 * g + 128)
            p = feats[g].astype(BF16)
            scale = ps_ref[0:1, cols]
            pw = jnp.dot(p, wp_ref[g], preferred_element_type=F32)
            y_parts.append(pw * scale)
            dyp = dycat[:, CC + 128 * g : CC + 128 * g + 128]
            gps_parts.append(_colsum(dyp * pw))
            dpw = (dyp * scale).astype(BF16)
            gwp_ref[g] += lax.dot_general(p, dpw, TN, preferred_element_type=F32)
            dp = lax.dot_general(dpw, wp_ref[g], NT, preferred_element_type=F32)
            q = dp * inv_cnts[g]
            q_s[0:tm, cols] = q
            sq = q
            for jj in range(1, win):
                sq = sq + q_s[jj : jj + tm, cols]
            dz_parts.append(sq - dp)
        gcw_ref[3:4, :] += jnp.concatenate(gps_parts, axis=1)
        ycat_ref[...] = jnp.concatenate(y_parts, axis=1).astype(BF16)
        dz = jnp.concatenate(dz_parts, axis=1).astype(BF16)
        dz_ref[...] = dz
        dh = lax.dot_general(dz, win_ref[...], NT, preferred_element_type=F32)
        acc_ref[2:3, :] += _colsum(dh)
        acc_ref[3:4, :] += _colsum(dh * x_ref[...])
        dx_ref[...] = DEEPNORM_ALPHA * dr1 + dh * (1.0 + sc1)
        e_s[tm : tm + HALO, :] = e_s[0:HALO, :]
        q_s[tm : tm + HALO, :] = q_s[0:HALO, :]

    tile = lambda w: pl.BlockSpec((tm, w), lambda i: (nt - 1 - i, 0))
    halo = pl.BlockSpec((HALO, ZW), lambda i: (jnp.maximum((nt - 1 - i) * hb - 1, 0), 0))
    fixed = lambda shape: pl.BlockSpec(shape, lambda i: (0,) * len(shape))
    return pl.pallas_call(
        body,
        name="b1",
        grid=(nt,),
        out_shape=[
            jax.ShapeDtypeStruct((T, D), F32),
            jax.ShapeDtypeStruct((T, D), BF16),
            jax.ShapeDtypeStruct((T, D), BF16),
            jax.ShapeDtypeStruct((T, ZW), BF16),
            jax.ShapeDtypeStruct((8, D), F32),
            jax.ShapeDtypeStruct((8, CC), F32),
            jax.ShapeDtypeStruct(w_pool.shape, F32),
        ],
        in_specs=[tile(D), tile(D), tile(1), tile(D), tile(D), tile(ZW), halo] + [VMEM_SPEC] * 7,
        out_specs=[tile(D), tile(D), tile(D), tile(ZW), fixed((8, D)), fixed((8, CC)), fixed(w_pool.shape)],
        scratch_shapes=[
            pltpu.VMEM((HALO + tm, CC), F32),
            pltpu.VMEM((HALO + tm, CC), F32),
            pltpu.VMEM((tm + HALO, CC), F32),
            pltpu.VMEM((tm + HALO, CC), F32),
        ],
        compiler_params=pltpu.CompilerParams(dimension_semantics=("arbitrary",), vmem_limit_bytes=VMEM_LIMIT),
    )(dx1, xhat1, rstd1, x, mix, z, z, mod, ln, w_out, w_in, conv_w, w_pool, pool_scale)


def _wgrad(a, b, bk, bn, bt, name, by_column_block):
    T, K = a.shape
    N = b.shape[1]
    nt = T // bt

    def body(a_ref, b_ref, o_ref, acc):
        t = pl.program_id(2)

        @pl.when(t == 0)
        def _():
            acc[...] = jnp.zeros(acc.shape, F32)

        acc[...] += lax.dot_general(a_ref[...], b_ref[...], TN, preferred_element_type=F32)

        @pl.when(t == nt - 1)
        def _():
            o_ref[...] = acc[...].astype(o_ref.dtype)

    if by_column_block:
        assert bk == K
        out_shape = jax.ShapeDtypeStruct((N // bn, K, bn), BF16)
        out_spec = pl.BlockSpec((None, K, bn), lambda k, n, t: (n, 0, 0))
    else:
        out_shape = jax.ShapeDtypeStruct((K, N), BF16)
        out_spec = pl.BlockSpec((bk, bn), lambda k, n, t: (k, n))
    return pl.pallas_call(
        body,
        name=name,
        grid=(K // bk, N // bn, nt),
        out_shape=out_shape,
        in_specs=[pl.BlockSpec((bt, bk), lambda k, n, t: (t, k)), pl.BlockSpec((bt, bn), lambda k, n, t: (t, n))],
        out_specs=out_spec,
        scratch_shapes=[pltpu.VMEM((bk, bn), F32)],
        compiler_params=pltpu.CompilerParams(dimension_semantics=("parallel", "parallel", "arbitrary"), vmem_limit_bytes=VMEM_LIMIT),
    )(a, b)


def _small_grads(acc1_t, acc2_t, gcw_t, gwp_all, cond_t, my_slot, w_cols):
    D = acc1_t.shape[2]
    n_chunk = D // 128
    q_mine = w_cols // 128

    def total(ref, r):
        s = ref[r, 0:1, :]
        for k in range(1, N_DEV):
            s = s + ref[r, k : k + 1, :]
        return s

    def body(slot_ref, a1_ref, a2_ref, gcw_ref, gwp_ref, ct_ref, gb_ref, gw_ref, gln_ref, gcwo_ref, gwpo_ref, dm_s):
        for s, (ref, r) in enumerate([(a1_ref, 2), (a1_ref, 3), (a1_ref, 4), (a2_ref, 2), (a2_ref, 3), (a2_ref, 4)]):
            gb_ref[0:1, s * D : (s + 1) * D] = total(ref, r)
            for qq in range(n_chunk):
                dm_s[s * n_chunk + qq] = ref[r, :, 128 * qq : 128 * qq + 128]
        gln_ref[0:1, :] = total(a1_ref, 0)
        gln_ref[1:2, :] = total(a1_ref, 1)
        gln_ref[2:3, :] = total(a2_ref, 0)
        gln_ref[3:4, :] = total(a2_ref, 1)
        gcwo_ref[...] = jnp.zeros(gcwo_ref.shape, F32)
        for r in range(4):
            gcwo_ref[r : r + 1, :] = total(gcw_ref, r)
        wp = gwp_ref[0]
        for k in range(1, N_DEV):
            wp = wp + gwp_ref[k]
        gwpo_ref[0] = wp
        ct = ct_ref[...]
        cond_t = ct * jax.nn.sigmoid(ct)
        q0 = slot_ref[0] * q_mine
        for q in range(q_mine):
            dm = dm_s[q0 + q]
            out = cond_t[:, 0:1] * dm[0:1, :]
            for k in range(1, N_DEV):
                out = out + cond_t[:, k : k + 1] * dm[k : k + 1, :]
            gw_ref[0, :, 128 * q : 128 * q + 128] = out

    CC = gcw_t.shape[2]
    return pl.pallas_call(
        body,
        name="small_grads",
        out_shape=[
            jax.ShapeDtypeStruct((1, 6 * D), F32),
            jax.ShapeDtypeStruct((1, D, w_cols), F32),
            jax.ShapeDtypeStruct((4, D), F32),
            jax.ShapeDtypeStruct((8, CC), F32),
            jax.ShapeDtypeStruct((1, *gwp_all.shape[1:]), F32),
        ],
        in_specs=[pl.BlockSpec(memory_space=pltpu.SMEM)] + [VMEM_SPEC] * 5,
        out_specs=[VMEM_SPEC] * 5,
        scratch_shapes=[pltpu.VMEM((6 * n_chunk, N_DEV, 128), F32)],
        compiler_params=pltpu.CompilerParams(vmem_limit_bytes=VMEM_LIMIT),
    )(my_slot, acc1_t, acc2_t, gcw_t, gwp_all, cond_t)


def kernel(x, c, w_ada, b_ada, w_in, conv_w, w_pool, pool_scale, w_out, ln1_g, ln1_b, w_mlp_in, w_mlp_out, ln2_g, ln2_b, loss_target, m_w_ada, m_b_ada, m_w_in, m_conv_w, m_w_pool, m_pool_scale, m_w_out, m_ln1_g, m_ln1_b, m_w_mlp_in, m_w_mlp_out, m_ln2_g, m_ln2_b, v_w_ada, v_b_ada, v_w_in, v_conv_w, v_w_pool, v_pool_scale, v_w_out, v_ln1_g, v_ln1_b, v_w_mlp_in, v_w_mlp_out, v_ln2_g, v_ln2_b):
    T, D = x.shape[1], x.shape[2]
    H = w_mlp_out.shape[1] * N_DEV
    ZW = w_in.shape[2] * N_DEV
    CC = ZW // 4
    tm = min(512, T // 2)
    ax, ay, ac = _my_place()
    me = _slot(ax, ay, ac)

    w_in_g, w_out_g, w_mi_g, w_mo_g, cw_g, c_g = _allgather(
        [w_in[0].astype(BF16), w_out[0].astype(BF16), w_mlp_in[0].astype(BF16), w_mlp_out[0].astype(BF16), conv_w[0], c],
        "gather_weights", in_vmem=False)
    w_in_f = jnp.transpose(w_in_g, (1, 0, 2)).reshape(D, ZW)
    w_mi_f = jnp.transpose(w_mi_g, (1, 0, 2)).reshape(D, H)
    w_out_f = w_out_g.reshape(D, D)
    w_mo_f = w_mo_g.reshape(H, D)
    conv_w_f = jnp.transpose(cw_g, (1, 0, 2)).reshape(conv_w.shape[1], CC)
    c_all = c_g.reshape(N_DEV, D)

    w_cols = w_ada.shape[2]
    b_mine = lax.dynamic_slice(b_ada, (0, me * w_cols), (1, w_cols))
    mod_part = _ada_fwd(c_all, w_ada, b_mine)
    (mod_g,) = _allgather([mod_part], "gather_mod", in_vmem=True)
    mod = lax.dynamic_index_in_dim(mod_g, me, axis=1, keepdims=False).reshape(6, D)

    ln = jnp.concatenate([ln1_g, ln1_b, ln2_g, ln2_b], axis=0)
    w_pool16 = w_pool[0].astype(BF16)
    xs, target = x[0], loss_target[0]

    z, h, xhat1, rstd1, mix = _f1(xs, mod, w_in_f, conv_w_f, w_pool16, pool_scale, w_out_f, tm)
    dx1, h2, a, du, df, acc2 = _fb2(xhat1, target, mod, ln, w_mi_f, w_mo_f, tm // 2)
    grad_x, dmix, ycat, dz, acc1, gcw, gwp = _b1(dx1, xhat1, rstd1, xs, mix, z, mod, ln, w_out_f, w_in_f, conv_w_f, w_pool16, pool_scale, tm)
    loss = lax.psum(acc2[5, 0], ("x", "y", "c"))

    bt = min(512, T)
    gp_in = _wgrad(h, dz, D, ZW // N_DEV, bt, "wgrad_in", True)
    gp_mi = _wgrad(h2, du, D, H // N_DEV, bt, "wgrad_mlp_in", True)
    gp_out = _wgrad(ycat, dmix, D, D, bt, "wgrad_out", False).reshape(N_DEV, D // N_DEV, D)
    gp_mo = _wgrad(a, df, D, D, bt, "wgrad_mlp_out", False).reshape(N_DEV, H // N_DEV, D)

    acc1_g, acc2_g, gcw_g, gwp_g = _allgather([acc1, acc2, gcw, gwp], "gather_small", in_vmem=True)
    g_b_ada, g_w_ada, g_ln, g_cw, g_w_pool = _small_grads(
        jnp.transpose(acc1_g, (1, 0, 2)), jnp.transpose(acc2_g, (1, 0, 2)), jnp.transpose(gcw_g, (1, 0, 2)), gwp_g,
        c_all.T, jnp.reshape(me, (1,)).astype(jnp.int32), w_cols)
    cc_mine = conv_w.shape[2]
    g_conv_w = lax.dynamic_slice(g_cw, (0, me * cc_mine), (conv_w.shape[1], cc_mine))[None]
    g_pool_scale = g_cw[3:4, :]
    g_ln1_g, g_ln1_b, g_ln2_g, g_ln2_b = g_ln[0:1], g_ln[1:2], g_ln[2:3], g_ln[3:4]

    small = _adamw_multi(
        [
            (b_ada, g_b_ada, m_b_ada, v_b_ada),
            (conv_w, g_conv_w, m_conv_w, v_conv_w),
            (w_pool, g_w_pool, m_w_pool, v_w_pool),
            (pool_scale, g_pool_scale, m_pool_scale, v_pool_scale),
            (ln1_g, g_ln1_g, m_ln1_g, v_ln1_g),
            (ln1_b, g_ln1_b, m_ln1_b, v_ln1_b),
            (ln2_g, g_ln2_g, m_ln2_g, v_ln2_g),
            (ln2_b, g_ln2_b, m_ln2_b, v_ln2_b),
        ],
        "adamw_small")
    (u_w_ada,) = _adamw_multi([(w_ada, g_w_ada, m_w_ada, v_w_ada)], "adamw_w_ada")
    u_b_ada, u_conv_w, u_w_pool, u_pool_scale, u_ln1_g, u_ln1_b, u_ln2_g, u_ln2_b = small

    g_w_in, *u_w_in = _reduce_adamw(gp_in, w_in, m_w_in, v_w_in, "reduce_w_in")
    g_w_out, *u_w_out = _reduce_adamw(gp_out, w_out, m_w_out, v_w_out, "reduce_w_out")
    g_w_mi, *u_w_mi = _reduce_adamw(gp_mi, w_mlp_in, m_w_mlp_in, v_w_mlp_in, "reduce_w_mlp_in")
    g_w_mo, *u_w_mo = _reduce_adamw(gp_mo, w_mlp_out, m_w_mlp_out, v_w_mlp_out, "reduce_w_mlp_out")

    grads = [g_w_ada, g_b_ada, g_w_in, g_conv_w, g_w_pool, g_pool_scale, g_w_out, g_ln1_g, g_ln1_b, g_w_mi, g_w_mo, g_ln2_g, g_ln2_b]
    updates = [u_w_ada, u_b_ada, u_w_in, u_conv_w, u_w_pool, u_pool_scale, u_w_out, u_ln1_g, u_ln1_b, u_w_mi, u_w_mo, u_ln2_g, u_ln2_b]
    deltas = [u[0] for u in updates]
    new_m = [u[1] for u in updates]
    new_v = [u[2] for u in updates]
    return (loss, grad_x[None], *grads, *deltas, *new_m, *new_v)
```

```python
import jax
import jax.numpy as jnp
from jax import lax
from jax.experimental import pallas as pl
from jax.experimental.pallas import tpu as pltpu

F32 = jnp.float32
BF16 = jnp.bfloat16
MESH = pl.DeviceIdType.MESH
N_DEV = 8

LN_EPS = 1e-5
DEPTH = 1
DEEPNORM_ALPHA = (2.0 * DEPTH) ** 0.25
POOL_WINDOWS = (2, 4, 8, 16)
HALO = 16

ADAM_LR = 0.001
ADAM_B1 = 0.9
ADAM_B2 = 0.999
ADAM_EPS = 1e-08
ADAM_WD = 0.01
ADAM_STEP = 10

VMEM_LIMIT = 56 * 1024 * 1024

VMEM_SPEC = pl.BlockSpec(memory_space=pltpu.VMEM)
ANY_SPEC = pl.BlockSpec(memory_space=pl.ANY)

NT = (((1,), (1,)), ((), ()))
TN = (((0,), (0,)), ((), ()))


def _my_place():
    return lax.axis_index("x"), lax.axis_index("y"), lax.axis_index("c")


def _slot(x, y, c):
    return 4 * x + 2 * y + c


def _gather_copy(outs, sems, a, k, block, to, src=None):
    send_sems, recv_sems, _ = sems
    dst = outs[a].at[_slot(*block)]
    return pltpu.make_async_remote_copy(
        src_ref=dst if src is None else src,
        dst_ref=dst,
        send_sem=send_sems.at[7 * a + k],
        recv_sem=recv_sems.at[7 * a + k],
        device_id=to,
        device_id_type=MESH,
    )


def _gather_peers():
    x, y, c = _my_place()
    return (x, y, c), (x, y, 1 - c), [(1 - x, y), (x, 1 - y), (1 - x, 1 - y)]


def _gather_first(ins, outs, sems):
    me, sibling, chips = _gather_peers()
    first = []
    for a in range(len(ins)):
        first.append(_gather_copy(outs, sems, a, 0, me, sibling, src=ins[a]))
        first += [_gather_copy(outs, sems, a, 1 + j, me, (*chip, me[2]), src=ins[a]) for j, chip in enumerate(chips)]
    return first


def _gather_start(ins, outs, sems):
    me, _, _ = _gather_peers()
    for a in range(len(ins)):
        pltpu.make_async_copy(ins[a], outs[a].at[_slot(*me)], sems[2].at[a]).start()
    for cp in _gather_first(ins, outs, sems):
        cp.start()


def _gather_forward(ins, outs, sems, j):
    me, sibling, chips = _gather_peers()
    for a in range(len(ins)):
        _gather_copy(outs, sems, a, 1 + j, (*chips[j], me[2]), me).wait_recv()
        _gather_copy(outs, sems, a, 4 + j, (*chips[j], me[2]), sibling).start()


def _gather_finish(ins, outs, sems):
    me, sibling, chips = _gather_peers()
    for a in range(len(ins)):
        _gather_copy(outs, sems, a, 0, sibling, me).wait_recv()
        for j, chip in enumerate(chips):
            _gather_copy(outs, sems, a, 4 + j, (*chip, 1 - me[2]), me).wait_recv()
    for cp in _gather_first(ins, outs, sems):
        cp.wait_send()
    for a in range(len(ins)):
        for j, chip in enumerate(chips):
            _gather_copy(outs, sems, a, 4 + j, (*chip, me[2]), sibling).wait_send()
        pltpu.make_async_copy(ins[a], outs[a].at[_slot(*me)], sems[2].at[a]).wait()


def _gather_scratch(n):
    return [pltpu.SemaphoreType.DMA((7 * n,)), pltpu.SemaphoreType.DMA((7 * n,)), pltpu.SemaphoreType.DMA((n,))]


def _gather_out_shape(shards):
    return [jax.ShapeDtypeStruct((N_DEV, *s.shape), s.dtype) for s in shards]


def _scatter_copy(g_hbm, recv, sems, k):
    x, y, c = _my_place()
    px, py, pc = x ^ (k >> 2), y ^ ((k >> 1) & 1), c ^ (k & 1)
    return pltpu.make_async_remote_copy(
        src_ref=g_hbm.at[_slot(px, py, pc)],
        dst_ref=recv.at[_slot(x, y, c)],
        send_sem=sems[0].at[k - 1],
        recv_sem=sems[1].at[k - 1],
        device_id=(px, py, pc),
        device_id_type=MESH,
    )


def _scatter_start(g_hbm, recv, sems):
    me = _slot(*_my_place())
    pltpu.make_async_copy(g_hbm.at[me], recv.at[me], sems[2]).start()
    for k in range(1, N_DEV):
        _scatter_copy(g_hbm, recv, sems, k).start()


def _scatter_finish(g_hbm, recv, sems):
    me = _slot(*_my_place())
    pltpu.make_async_copy(g_hbm.at[me], recv.at[me], sems[2]).wait()
    for k in range(1, N_DEV):
        _scatter_copy(g_hbm, recv, sems, k).wait_recv()
    for k in range(1, N_DEV):
        _scatter_copy(g_hbm, recv, sems, k).wait_send()


def _scatter_scratch():
    return [pltpu.SemaphoreType.DMA((N_DEV - 1,)), pltpu.SemaphoreType.DMA((N_DEV - 1,)), pltpu.SemaphoreType.DMA]


def _allgather(shards, name, in_vmem, scatter=None):
    n = len(shards)
    ns = 0 if scatter is None else 1

    def body(*refs):
        ins, outs = refs[:n], refs[n + ns : 2 * n + ns]
        rest = refs[2 * n + ns :]
        if ns:
            g_hbm, recv, sems, s_sems = refs[n], rest[0], rest[1:4], rest[4:7]
            _scatter_start(g_hbm, recv, s_sems)
        else:
            sems = rest[0:3]
        _gather_start(ins, outs, sems)
        for j in range(3):
            _gather_forward(ins, outs, sems, j)
        _gather_finish(ins, outs, sems)
        if ns:
            _scatter_finish(g_hbm, recv, s_sems)

    spec = VMEM_SPEC if in_vmem else ANY_SPEC
    extra_in, extra_out, extra_spec, extra_scratch = [], [], [], []
    if ns:
        extra_in, extra_spec = [scatter], [ANY_SPEC]
        extra_out = [jax.ShapeDtypeStruct(scatter.shape, scatter.dtype)]
        extra_scratch = _scatter_scratch()
    return pl.pallas_call(
        body,
        name=name,
        out_shape=_gather_out_shape(shards) + extra_out,
        in_specs=[spec] * n + extra_spec,
        out_specs=[spec] * n + extra_spec,
        scratch_shapes=_gather_scratch(n) + extra_scratch,
        compiler_params=pltpu.CompilerParams(vmem_limit_bytes=VMEM_LIMIT),
    )(*shards, *extra_in)


def _adamw_math(w, g, m, v):
    m = ADAM_B1 * m + (1.0 - ADAM_B1) * g
    v = ADAM_B2 * v + (1.0 - ADAM_B2) * (g * g)
    m_hat = m / (1.0 - ADAM_B1**ADAM_STEP)
    v_hat = v / (1.0 - ADAM_B2**ADAM_STEP)
    delta = -ADAM_LR * (m_hat / (jnp.sqrt(v_hat) + ADAM_EPS) + ADAM_WD * w)
    return delta, m, v


ROW_CHUNK = 64


def _sum_adamw(recv, w, m, v, name):
    _, rows, cols = recv.shape

    def body(r_ref, w_ref, m_ref, v_ref, grad_ref, delta_ref, nm_ref, nv_ref):
        def step(r, carry):
            rs = pl.ds(pl.multiple_of(r * ROW_CHUNK, ROW_CHUNK), ROW_CHUNK)
            g = r_ref[0, rs, :].astype(F32)
            for k in range(1, N_DEV):
                g = g + r_ref[k, rs, :].astype(F32)
            delta, nm, nv = _adamw_math(w_ref[0, rs, :], g, m_ref[0, rs, :], v_ref[0, rs, :])
            grad_ref[0, rs, :] = g
            delta_ref[0, rs, :] = delta
            nm_ref[0, rs, :] = nm
            nv_ref[0, rs, :] = nv
            return carry

        lax.fori_loop(0, rows // ROW_CHUNK, step, 0)

    out = jax.ShapeDtypeStruct(w.shape, F32)
    return pl.pallas_call(
        body,
        name=name,
        out_shape=[out] * 4,
        in_specs=[VMEM_SPEC] * 4,
        out_specs=[VMEM_SPEC] * 4,
        compiler_params=pltpu.CompilerParams(vmem_limit_bytes=VMEM_LIMIT),
    )(recv, w, m, v)


def _adamw_multi(items, name):
    n = len(items)

    def body(*refs):
        ins, outs = refs[: 4 * n], refs[4 * n :]
        for a in range(n):
            w_ref, g_ref, m_ref, v_ref = ins[4 * a : 4 * a + 4]
            d_ref, nm_ref, nv_ref = outs[3 * a : 3 * a + 3]
            shape = w_ref.shape
            if len(shape) == 3 and shape[1] % ROW_CHUNK == 0 and shape[1] > ROW_CHUNK:

                def step(r, carry, w_ref=w_ref, g_ref=g_ref, m_ref=m_ref, v_ref=v_ref, d_ref=d_ref, nm_ref=nm_ref, nv_ref=nv_ref):
                    rs = pl.ds(pl.multiple_of(r * ROW_CHUNK, ROW_CHUNK), ROW_CHUNK)
                    delta, nm, nv = _adamw_math(w_ref[0, rs, :], g_ref[0, rs, :], m_ref[0, rs, :], v_ref[0, rs, :])
                    d_ref[0, rs, :] = delta
                    nm_ref[0, rs, :] = nm
                    nv_ref[0, rs, :] = nv
                    return carry

                lax.fori_loop(0, shape[1] // ROW_CHUNK, step, 0)
            else:
                delta, nm, nv = _adamw_math(w_ref[...], g_ref[...], m_ref[...], v_ref[...])
                d_ref[...] = delta
                nm_ref[...] = nm
                nv_ref[...] = nv

    flat = [a for it in items for a in it]
    out_shape = [jax.ShapeDtypeStruct(it[0].shape, F32) for it in items for _ in range(3)]
    outs = pl.pallas_call(
        body,
        name=name,
        out_shape=out_shape,
        in_specs=[VMEM_SPEC] * (4 * n),
        out_specs=[VMEM_SPEC] * (3 * n),
        compiler_params=pltpu.CompilerParams(vmem_limit_bytes=VMEM_LIMIT),
    )(*flat)
    return [tuple(outs[3 * a : 3 * a + 3]) for a in range(n)]


def _ada_fwd(c_all, w_ada, b_mine):
    def body(c_ref, w_ref, b_ref, o_ref):
        cv = c_ref[...]
        cond = cv * jax.nn.sigmoid(cv)
        o_ref[...] = jnp.dot(cond, w_ref[0], precision=lax.Precision.HIGHEST, preferred_element_type=F32) + b_ref[...]

    return pl.pallas_call(
        body,
        name="ada_fwd",
        out_shape=jax.ShapeDtypeStruct((N_DEV, w_ada.shape[2]), F32),
        in_specs=[VMEM_SPEC] * 3,
        out_specs=VMEM_SPEC,
        compiler_params=pltpu.CompilerParams(vmem_limit_bytes=VMEM_LIMIT),
    )(c_all, w_ada, b_mine)


def _ln_fwd(r):
    mu = jnp.mean(r, axis=-1, keepdims=True)
    d = r - mu
    var = jnp.mean(d * d, axis=-1, keepdims=True)
    rstd = lax.rsqrt(var + LN_EPS)
    return d * rstd, rstd


def _ln_bwd(dxh, xhat, rstd):
    m1 = jnp.mean(dxh, axis=-1, keepdims=True)
    m2 = jnp.mean(dxh * xhat, axis=-1, keepdims=True)
    return rstd * (dxh - m1 - xhat * m2)


def _colsum(a):
    return jnp.sum(a, axis=0, keepdims=True)


def _pool_features(vp, vp_s, row, tm):
    feats, inv_cnts = [], []
    for g, win in enumerate(POOL_WINDOWS):
        cols = slice(128 * g, 128 * g + 128)
        s = vp[:, cols]
        for j in range(1, win):
            s = s + vp_s[HALO - j : HALO - j + tm, cols]
        inv_cnt = 1.0 / jnp.minimum(row + 1, win).astype(F32)
        feats.append(s * inv_cnt - vp[:, cols])
        inv_cnts.append(inv_cnt)
    return feats, inv_cnts


def _f1(x, mod, w_in, conv_w, w_pool, pool_scale, w_out, tm, gather):
    T, D = x.shape
    nb, _, wb = w_in.shape
    ZW = nb * wb
    CC = ZW // 4
    nt = T // tm
    ng = len(gather)
    fwd_steps = [(j + 1) * nt // 4 for j in range(3)]

    def body(*refs):
        x_ref, mod_ref, win_ref, cw_ref, wp_ref, ps_ref, wout_ref = refs[:7]
        g_ins = refs[7 : 7 + ng]
        z_ref, h_ref, xhat_ref, rstd_ref, mix_ref = refs[7 + ng : 12 + ng]
        g_outs = refs[12 + ng : 12 + 2 * ng]
        cv_s, vp_s = refs[12 + 2 * ng : 14 + 2 * ng]
        g_sems = refs[14 + 2 * ng :]
        i = pl.program_id(0)

        @pl.when(i == 0)
        def _():
            _gather_start(g_ins, g_outs, g_sems)
            cv_s[0:HALO, :] = jnp.zeros((HALO, CC), F32)
            vp_s[0:HALO, :] = jnp.zeros((HALO, CC), F32)

        xv = x_ref[...]
        sh1, sc1, g1 = mod_ref[0:1, :], mod_ref[1:2, :], mod_ref[2:3, :]
        h = (xv * (1.0 + sc1) + sh1).astype(BF16)
        h_ref[...] = h
        z = jnp.concatenate([jnp.dot(h, win_ref[k], preferred_element_type=F32) for k in range(nb)], axis=1)
        z_ref[...] = z.astype(BF16)
        gb, gc, vc, vp = z[:, 0:CC], z[:, CC : 2 * CC], z[:, 2 * CC : 3 * CC], z[:, 3 * CC : 4 * CC]
        cv = gc * vc
        cv_s[HALO : HALO + tm, :] = cv
        vp_s[HALO : HALO + tm, :] = vp
        conv = cw_ref[0:1, :] * cv_s[HALO - 2 : HALO - 2 + tm, :] + cw_ref[1:2, :] * cv_s[HALO - 1 : HALO - 1 + tm, :] + cw_ref[2:3, :] * cv
        parts = [gb * conv]
        row = i * tm + lax.broadcasted_iota(jnp.int32, (tm, 1), 0)
        feats, _ = _pool_features(vp, vp_s, row, tm)
        for g in range(len(POOL_WINDOWS)):
            pw = jnp.dot(feats[g].astype(BF16), wp_ref[g], preferred_element_type=F32)
            parts.append(pw * ps_ref[0:1, 128 * g : 128 * g + 128])
        cv_s[0:HALO, :] = cv_s[tm : tm + HALO, :]
        vp_s[0:HALO, :] = vp_s[tm : tm + HALO, :]
        ycat = jnp.concatenate(parts, axis=1).astype(BF16)
        mix = jnp.dot(ycat, wout_ref[...], preferred_element_type=F32)
        mix_ref[...] = mix
        xhat, rstd = _ln_fwd(DEEPNORM_ALPHA * xv + (1.0 + g1) * mix)
        xhat_ref[...] = xhat
        rstd_ref[...] = rstd

        for j in range(3):

            @pl.when(i == fwd_steps[j])
            def _(j=j):
                _gather_forward(g_ins, g_outs, g_sems, j)

        @pl.when(i == nt - 1)
        def _():
            _gather_finish(g_ins, g_outs, g_sems)

    tile = lambda w: pl.BlockSpec((tm, w), lambda i: (i, 0))
    return pl.pallas_call(
        body,
        name="f1",
        grid=(nt,),
        out_shape=[
            jax.ShapeDtypeStruct((T, ZW), BF16),
            jax.ShapeDtypeStruct((T, D), BF16),
            jax.ShapeDtypeStruct((T, D), F32),
            jax.ShapeDtypeStruct((T, 1), F32),
            jax.ShapeDtypeStruct((T, D), F32),
        ]
        + _gather_out_shape(gather),
        in_specs=[tile(D)] + [VMEM_SPEC] * 6 + [ANY_SPEC] * ng,
        out_specs=[tile(ZW), tile(D), tile(D), tile(1), tile(D)] + [ANY_SPEC] * ng,
        scratch_shapes=[pltpu.VMEM((HALO + tm, CC), F32), pltpu.VMEM((HALO + tm, CC), F32)] + _gather_scratch(ng),
        compiler_params=pltpu.CompilerParams(dimension_semantics=("arbitrary",), vmem_limit_bytes=VMEM_LIMIT),
    )(x, mod, w_in, conv_w, w_pool, pool_scale, w_out, *gather)


def _fb2(xhat1, target, mod, ln, w_mi, w_mo, tm):
    T, D = xhat1.shape
    nb, _, hc = w_mi.shape
    H = nb * hc
    nt = T // tm

    def body(xh1_ref, t_ref, mod_ref, ln_ref, wmi_ref, wmo_ref, dx1_ref, h2_ref, a_ref, du_ref, df_ref, acc_ref):
        i = pl.program_id(0)

        @pl.when(i == 0)
        def _():
            acc_ref[...] = jnp.zeros((8, D), F32)

        sh2, sc2, g2 = mod_ref[3:4, :], mod_ref[4:5, :], mod_ref[5:6, :]
        x1 = xh1_ref[...] * ln_ref[0:1, :] + ln_ref[1:2, :]
        h2 = (x1 * (1.0 + sc2) + sh2).astype(BF16)
        h2_ref[...] = h2
        f = jnp.zeros((tm, D), F32)
        for k in range(nb):
            ks = slice(k * hc, (k + 1) * hc)
            r = jnp.maximum(jnp.dot(h2, wmi_ref[k], preferred_element_type=F32), 0.0)
            du_ref[:, ks] = r.astype(BF16)
            a = (r * r).astype(BF16)
            a_ref[:, ks] = a
            f = f + jnp.dot(a, wmo_ref[k], preferred_element_type=F32)
        xhat2, rstd2 = _ln_fwd(DEEPNORM_ALPHA * x1 + (1.0 + g2) * f)
        ln2_g = ln_ref[2:3, :]
        d = xhat2 * ln2_g + ln_ref[3:4, :] - t_ref[...]
        dy = d * (1.0 / D)
        dr2 = _ln_bwd(dy * ln2_g, xhat2, rstd2)
        df = ((1.0 + g2) * dr2).astype(BF16)
        df_ref[...] = df
        dh2 = jnp.zeros((tm, D), F32)
        for k in range(nb):
            ks = slice(k * hc, (k + 1) * hc)
            da = lax.dot_general(df, wmo_ref[k], NT, preferred_element_type=F32)
            du = (da * (2.0 * du_ref[:, ks].astype(F32))).astype(BF16)
            du_ref[:, ks] = du
            dh2 = dh2 + lax.dot_general(du, wmi_ref[k], NT, preferred_element_type=F32)
        dx1_ref[...] = DEEPNORM_ALPHA * dr2 + dh2 * (1.0 + sc2)
        acc_ref[0:1, :] += _colsum(dy * xhat2)
        acc_ref[1:2, :] += _colsum(dy)
        acc_ref[2:3, :] += _colsum(dh2)
        acc_ref[3:4, :] += _colsum(dh2 * x1)
        acc_ref[4:5, :] += _colsum(dr2 * f)
        acc_ref[5:6, :] += jnp.zeros((1, D), F32) + (0.5 / D) * jnp.sum(d * d)

    tile = lambda w: pl.BlockSpec((tm, w), lambda i: (i, 0))
    return pl.pallas_call(
        body,
        name="fb2",
        grid=(nt,),
        out_shape=[
            jax.ShapeDtypeStruct((T, D), F32),
            jax.ShapeDtypeStruct((T, D), BF16),
            jax.ShapeDtypeStruct((T, H), BF16),
            jax.ShapeDtypeStruct((T, H), BF16),
            jax.ShapeDtypeStruct((T, D), BF16),
            jax.ShapeDtypeStruct((8, D), F32),
        ],
        in_specs=[tile(D), tile(D)] + [VMEM_SPEC] * 4,
        out_specs=[tile(D), tile(D), tile(H), tile(H), tile(D), pl.BlockSpec((8, D), lambda i: (0, 0))],
        compiler_params=pltpu.CompilerParams(dimension_semantics=("arbitrary",), vmem_limit_bytes=VMEM_LIMIT),
    )(xhat1, target, mod, ln, w_mi, w_mo)


def _b1(dx1, xhat1, rstd1, x, mix, z, mod, ln, w_out, w_in, conv_w, w_pool, pool_scale, tm, scatter):
    T, D = x.shape
    nb, _, wb = w_in.shape
    ZW = nb * wb
    CC = ZW // 4
    nt = T // tm
    hb = tm // HALO

    def body(dx1_ref, xh1_ref, rstd_ref, x_ref, mix_ref, z_ref, zh_ref, mod_ref, ln_ref, wout_ref, win_ref, cw_ref, wp_ref, ps_ref, s_hbm,
             dx_ref, dmix_ref, ycat_ref, dz_ref, acc_ref, gcw_ref, gwp_ref, s_recv, cv_s, vp_s, e_s, q_s, *s_sems):
        i = pl.program_id(0)
        j = nt - 1 - i

        @pl.when(i == 0)
        def _():
            _scatter_start(s_hbm, s_recv, s_sems)
            acc_ref[...] = jnp.zeros((8, D), F32)
            gcw_ref[...] = jnp.zeros((8, CC), F32)
            gwp_ref[...] = jnp.zeros(gwp_ref.shape, F32)
            e_s[tm : tm + HALO, :] = jnp.zeros((HALO, CC), F32)
            q_s[tm : tm + HALO, :] = jnp.zeros((HALO, CC), F32)

        sh1, sc1, g1 = mod_ref[0:1, :], mod_ref[1:2, :], mod_ref[2:3, :]
        dx1 = dx1_ref[...]
        xhat1 = xh1_ref[...]
        acc_ref[0:1, :] += _colsum(dx1 * xhat1)
        acc_ref[1:2, :] += _colsum(dx1)
        dr1 = _ln_bwd(dx1 * ln_ref[0:1, :], xhat1, rstd_ref[...])
        acc_ref[4:5, :] += _colsum(dr1 * mix_ref[...])
        dmix = ((1.0 + g1) * dr1).astype(BF16)
        dmix_ref[...] = dmix
        dycat = lax.dot_general(dmix, wout_ref[...], NT, preferred_element_type=F32)

        z = z_ref[...].astype(F32)
        zh = zh_ref[...].astype(F32) * jnp.where(j > 0, 1.0, 0.0)
        gb, gc, vc, vp = z[:, 0:CC], z[:, CC : 2 * CC], z[:, 2 * CC : 3 * CC], z[:, 3 * CC : 4 * CC]
        cv = gc * vc
        cv_s[0:HALO, :] = zh[:, CC : 2 * CC] * zh[:, 2 * CC : 3 * CC]
        cv_s[HALO : HALO + tm, :] = cv
        vp_s[0:HALO, :] = zh[:, 3 * CC : 4 * CC]
        vp_s[HALO : HALO + tm, :] = vp
        cv_m2 = cv_s[HALO - 2 : HALO - 2 + tm, :]
        cv_m1 = cv_s[HALO - 1 : HALO - 1 + tm, :]
        w0, w1, w2 = cw_ref[0:1, :], cw_ref[1:2, :], cw_ref[2:3, :]
        conv = w0 * cv_m2 + w1 * cv_m1 + w2 * cv
        dyc = dycat[:, 0:CC]
        e = dyc * gb
        e_s[0:tm, :] = e
        dcv = w2 * e + w1 * e_s[1 : 1 + tm, :] + w0 * e_s[2 : 2 + tm, :]
        gcw_ref[0:1, :] += _colsum(e * cv_m2)
        gcw_ref[1:2, :] += _colsum(e * cv_m1)
        gcw_ref[2:3, :] += _colsum(e * cv)
        y_parts = [gb * conv]
        dz_parts = [dyc * conv, dcv * vc, dcv * gc]

        row = j * tm + lax.broadcasted_iota(jnp.int32, (tm, 1), 0)
        feats, inv_cnts = _pool_features(vp, vp_s, row, tm)
        gps_parts = []
        for g, win in enumerate(POOL_WINDOWS):
            cols = slice(128 * g, 128 * g + 128)
            p = feats[g].astype(BF16)
            scale = ps_ref[0:1, cols]
            pw = jnp.dot(p, wp_ref[g], preferred_element_type=F32)
            y_parts.append(pw * scale)
            dyp = dycat[:, CC + 128 * g : CC + 128 * g + 128]
            gps_parts.append(_colsum(dyp * pw))
            dpw = (dyp * scale).astype(BF16)
            gwp_ref[g] += lax.dot_general(p, dpw, TN, preferred_element_type=F32)
            dp = lax.dot_general(dpw, wp_ref[g], NT, preferred_element_type=F32)
            q = dp * inv_cnts[g]
            q_s[0:tm, cols] = q
            sq = q
            for jj in range(1, win):
                sq = sq + q_s[jj : jj + tm, cols]
            dz_parts.append(sq - dp)
        gcw_ref[3:4, :] += jnp.concatenate(gps_parts, axis=1)
        ycat_ref[...] = jnp.concatenate(y_parts, axis=1).astype(BF16)
        dz = jnp.concatenate(dz_parts, axis=1).astype(BF16)
        dz_ref[...] = dz
        dh = jnp.zeros((tm, D), F32)
        for k in range(nb):
            dh = dh + lax.dot_general(dz[:, k * wb : (k + 1) * wb], win_ref[k], NT, preferred_element_type=F32)
        acc_ref[2:3, :] += _colsum(dh)
        acc_ref[3:4, :] += _colsum(dh * x_ref[...])
        dx_ref[...] = DEEPNORM_ALPHA * dr1 + dh * (1.0 + sc1)
        e_s[tm : tm + HALO, :] = e_s[0:HALO, :]
        q_s[tm : tm + HALO, :] = q_s[0:HALO, :]

        @pl.when(i == nt - 1)
        def _():
            _scatter_finish(s_hbm, s_recv, s_sems)

    tile = lambda w: pl.BlockSpec((tm, w), lambda i: (nt - 1 - i, 0))
    halo = pl.BlockSpec((HALO, ZW), lambda i: (jnp.maximum((nt - 1 - i) * hb - 1, 0), 0))
    fixed = lambda shape: pl.BlockSpec(shape, lambda i: (0,) * len(shape))
    return pl.pallas_call(
        body,
        name="b1",
        grid=(nt,),
        out_shape=[
            jax.ShapeDtypeStruct((T, D), F32),
            jax.ShapeDtypeStruct((T, D), BF16),
            jax.ShapeDtypeStruct((T, D), BF16),
            jax.ShapeDtypeStruct((T, ZW), BF16),
            jax.ShapeDtypeStruct((8, D), F32),
            jax.ShapeDtypeStruct((8, CC), F32),
            jax.ShapeDtypeStruct(w_pool.shape, F32),
            jax.ShapeDtypeStruct(scatter.shape, scatter.dtype),
        ],
        in_specs=[tile(D), tile(D), tile(1), tile(D), tile(D), tile(ZW), halo] + [VMEM_SPEC] * 7 + [ANY_SPEC],
        out_specs=[tile(D), tile(D), tile(D), tile(ZW), fixed((8, D)), fixed((8, CC)), fixed(w_pool.shape), ANY_SPEC],
        scratch_shapes=[
            pltpu.VMEM((HALO + tm, CC), F32),
            pltpu.VMEM((HALO + tm, CC), F32),
            pltpu.VMEM((tm + HALO, CC), F32),
            pltpu.VMEM((tm + HALO, CC), F32),
        ]
        + _scatter_scratch(),
        compiler_params=pltpu.CompilerParams(dimension_semantics=("arbitrary",), vmem_limit_bytes=VMEM_LIMIT),
    )(dx1, xhat1, rstd1, x, mix, z, z, mod, ln, w_out, w_in, conv_w, w_pool, pool_scale, scatter)


def _wgrad(a, b, bk, n_groups, bt, name, owners=None, scatter=None):
    T, K = a.shape
    N = b.shape[1]
    nk, nt, ng = K // bk, T // bt, N // n_groups
    nc = min(512, ng)
    ns = 0 if scatter is None else 1

    def body(*refs):
        a_ref, b_ref = refs[0], refs[1]
        o_ref = refs[2 + ns]
        acc = refs[3 + 2 * ns]
        if ns:
            s_hbm, s_recv, s_sems = refs[2], refs[4], refs[6:9]
        kk, gg, t = pl.program_id(0), pl.program_id(1), pl.program_id(2)

        if ns:

            @pl.when((kk == 0) & (gg == 0) & (t == 0))
            def _():
                _scatter_start(s_hbm, s_recv, s_sems)

        @pl.when(t == 0)
        def _():
            acc[...] = jnp.zeros(acc.shape, F32)

        at = a_ref[...].T
        for c in range(ng // nc):
            cs = slice(c * nc, (c + 1) * nc)
            acc[:, cs] += jnp.dot(at, b_ref[:, cs], preferred_element_type=F32)

        @pl.when(t == nt - 1)
        def _():
            if owners is None:
                o_ref[...] = acc[...].astype(BF16)
            else:
                per = N // owners
                for o in range(ng // per):
                    o_ref[o] = acc[:, o * per : (o + 1) * per].astype(BF16)

        if ns:

            @pl.when((kk == nk - 1) & (gg == n_groups - 1) & (t == nt - 1))
            def _():
                _scatter_finish(s_hbm, s_recv, s_sems)

    if owners is None:
        out_shape = [jax.ShapeDtypeStruct((K, N), BF16)]
        out_specs = [pl.BlockSpec((bk, ng), lambda k, g, t: (k, g))]
    else:
        assert bk == K
        per = N // owners
        out_shape = [jax.ShapeDtypeStruct((owners, K, per), BF16)]
        out_specs = [pl.BlockSpec((ng // per, K, per), lambda k, g, t: (g, 0, 0))]
    ins, in_specs = [a, b], [pl.BlockSpec((bt, bk), lambda k, g, t: (t, k)), pl.BlockSpec((bt, ng), lambda k, g, t: (t, g))]
    scratch = [pltpu.VMEM((bk, ng), F32)]
    if ns:
        ins.append(scatter)
        in_specs.append(ANY_SPEC)
        out_shape.append(jax.ShapeDtypeStruct(scatter.shape, scatter.dtype))
        out_specs.append(ANY_SPEC)
        scratch += _scatter_scratch()
    outs = pl.pallas_call(
        body,
        name=name,
        grid=(nk, n_groups, nt),
        out_shape=out_shape,
        in_specs=in_specs,
        out_specs=out_specs,
        scratch_shapes=scratch,
        compiler_params=pltpu.CompilerParams(dimension_semantics=("arbitrary", "arbitrary", "arbitrary"), vmem_limit_bytes=VMEM_LIMIT),
    )(*ins)
    return outs if ns else outs[0]


def _small_grads(acc1_t, acc2_t, gcw_t, gwp_all, cond_t, my_slot, w_cols):
    D = acc1_t.shape[2]
    n_chunk = D // 128
    q_mine = w_cols // 128

    def total(ref, r):
        s = ref[r, 0:1, :]
        for k in range(1, N_DEV):
            s = s + ref[r, k : k + 1, :]
        return s

    def body(slot_ref, a1_ref, a2_ref, gcw_ref, gwp_ref, ct_ref, gb_ref, gw_ref, gln_ref, gcwo_ref, gwpo_ref, dm_s):
        for s, (ref, r) in enumerate([(a1_ref, 2), (a1_ref, 3), (a1_ref, 4), (a2_ref, 2), (a2_ref, 3), (a2_ref, 4)]):
            gb_ref[0:1, s * D : (s + 1) * D] = total(ref, r)
            for qq in range(n_chunk):
                dm_s[s * n_chunk + qq] = ref[r, :, 128 * qq : 128 * qq + 128]
        gln_ref[0:1, :] = total(a1_ref, 0)
        gln_ref[1:2, :] = total(a1_ref, 1)
        gln_ref[2:3, :] = total(a2_ref, 0)
        gln_ref[3:4, :] = total(a2_ref, 1)
        gcwo_ref[...] = jnp.zeros(gcwo_ref.shape, F32)
        for r in range(4):
            gcwo_ref[r : r + 1, :] = total(gcw_ref, r)
        wp = gwp_ref[0]
        for k in range(1, N_DEV):
            wp = wp + gwp_ref[k]
        gwpo_ref[0] = wp
        ct = ct_ref[...]
        cond_t = ct * jax.nn.sigmoid(ct)
        q0 = slot_ref[0] * q_mine
        for q in range(q_mine):
            dm = dm_s[q0 + q]
            out = cond_t[:, 0:1] * dm[0:1, :]
            for k in range(1, N_DEV):
                out = out + cond_t[:, k : k + 1] * dm[k : k + 1, :]
            gw_ref[0, :, 128 * q : 128 * q + 128] = out

    CC = gcw_t.shape[2]
    return pl.pallas_call(
        body,
        name="small_grads",
        out_shape=[
            jax.ShapeDtypeStruct((1, 6 * D), F32),
            jax.ShapeDtypeStruct((1, D, w_cols), F32),
            jax.ShapeDtypeStruct((4, D), F32),
            jax.ShapeDtypeStruct((8, CC), F32),
            jax.ShapeDtypeStruct((1, *gwp_all.shape[1:]), F32),
        ],
        in_specs=[pl.BlockSpec(memory_space=pltpu.SMEM)] + [VMEM_SPEC] * 5,
        out_specs=[VMEM_SPEC] * 5,
        scratch_shapes=[pltpu.VMEM((6 * n_chunk, N_DEV, 128), F32)],
        compiler_params=pltpu.CompilerParams(vmem_limit_bytes=VMEM_LIMIT),
    )(my_slot, acc1_t, acc2_t, gcw_t, gwp_all, cond_t)


def kernel(x, c, w_ada, b_ada, w_in, conv_w, w_pool, pool_scale, w_out, ln1_g, ln1_b, w_mlp_in, w_mlp_out, ln2_g, ln2_b, loss_target, m_w_ada, m_b_ada, m_w_in, m_conv_w, m_w_pool, m_pool_scale, m_w_out, m_ln1_g, m_ln1_b, m_w_mlp_in, m_w_mlp_out, m_ln2_g, m_ln2_b, v_w_ada, v_b_ada, v_w_in, v_conv_w, v_w_pool, v_pool_scale, v_w_out, v_ln1_g, v_ln1_b, v_w_mlp_in, v_w_mlp_out, v_ln2_g, v_ln2_b):
    T, D = x.shape[1], x.shape[2]
    H = w_mlp_out.shape[1] * N_DEV
    ZW = w_in.shape[2] * N_DEV
    CC = ZW // 4
    tm = min(512, T // 2)
    bt = min(1024, T)
    ax, ay, ac = _my_place()
    me = _slot(ax, ay, ac)

    w_in_g, w_out_g, cw_g, c_g = _allgather(
        [w_in[0].astype(BF16), w_out[0].astype(BF16), conv_w[0], c], "gather_weights", in_vmem=False)
    w_out_f = w_out_g.reshape(D, D)
    conv_w_f = jnp.transpose(cw_g, (1, 0, 2)).reshape(conv_w.shape[1], CC)
    c_all = c_g.reshape(N_DEV, D)

    w_cols = w_ada.shape[2]
    b_mine = lax.dynamic_slice(b_ada, (0, me * w_cols), (1, w_cols))
    mod_part = _ada_fwd(c_all, w_ada, b_mine)
    (mod_g,) = _allgather([mod_part], "gather_mod", in_vmem=True)
    mod = lax.dynamic_index_in_dim(mod_g, me, axis=1, keepdims=False).reshape(6, D)

    ln = jnp.concatenate([ln1_g, ln1_b, ln2_g, ln2_b], axis=0)
    w_pool16 = w_pool[0].astype(BF16)
    xs, target = x[0], loss_target[0]

    z, h, xhat1, rstd1, mix, w_mi_g, w_mo_g = _f1(
        xs, mod, w_in_g, conv_w_f, w_pool16, pool_scale, w_out_f, tm, [w_mlp_in[0].astype(BF16), w_mlp_out[0].astype(BF16)])
    dx1, h2, a, du, df, acc2 = _fb2(xhat1, target, mod, ln, w_mi_g, w_mo_g, tm // 2)
    loss = lax.psum(acc2[5, 0], ("x", "y", "c"))

    gp_mo = _wgrad(a, df, D, 1, bt, "wgrad_mlp_out").reshape(N_DEV, H // N_DEV, D)
    gp_mi, rv_mo = _wgrad(h2, du, D, 2, bt, "wgrad_mlp_in", owners=N_DEV, scatter=gp_mo)
    grad_x, dmix, ycat, dz, acc1, gcw, gwp, rv_mi = _b1(
        dx1, xhat1, rstd1, xs, mix, z, mod, ln, w_out_f, w_in_g, conv_w_f, w_pool16, pool_scale, tm, gp_mi)
    gp_in = _wgrad(h, dz, D, 1, bt, "wgrad_in", owners=N_DEV)
    gp_out, rv_in = _wgrad(ycat, dmix, D, 1, bt, "wgrad_out", scatter=gp_in)
    gp_out = gp_out.reshape(N_DEV, D // N_DEV, D)

    acc1_g, acc2_g, gcw_g, gwp_g, rv_out = _allgather([acc1, acc2, gcw, gwp], "gather_small", in_vmem=True, scatter=gp_out)
    g_b_ada, g_w_ada, g_ln, g_cw, g_w_pool = _small_grads(
        jnp.transpose(acc1_g, (1, 0, 2)), jnp.transpose(acc2_g, (1, 0, 2)), jnp.transpose(gcw_g, (1, 0, 2)), gwp_g,
        c_all.T, jnp.reshape(me, (1,)).astype(jnp.int32), w_cols)
    cc_mine = conv_w.shape[2]
    g_conv_w = lax.dynamic_slice(g_cw, (0, me * cc_mine), (conv_w.shape[1], cc_mine))[None]
    g_pool_scale = g_cw[3:4, :]
    g_ln1_g, g_ln1_b, g_ln2_g, g_ln2_b = g_ln[0:1], g_ln[1:2], g_ln[2:3], g_ln[3:4]

    small = _adamw_multi(
        [
            (b_ada, g_b_ada, m_b_ada, v_b_ada),
            (conv_w, g_conv_w, m_conv_w, v_conv_w),
            (w_pool, g_w_pool, m_w_pool, v_w_pool),
            (pool_scale, g_pool_scale, m_pool_scale, v_pool_scale),
            (ln1_g, g_ln1_g, m_ln1_g, v_ln1_g),
            (ln1_b, g_ln1_b, m_ln1_b, v_ln1_b),
            (ln2_g, g_ln2_g, m_ln2_g, v_ln2_g),
            (ln2_b, g_ln2_b, m_ln2_b, v_ln2_b),
        ],
        "adamw_small")
    (u_w_ada,) = _adamw_multi([(w_ada, g_w_ada, m_w_ada, v_w_ada)], "adamw_w_ada")
    u_b_ada, u_conv_w, u_w_pool, u_pool_scale, u_ln1_g, u_ln1_b, u_ln2_g, u_ln2_b = small

    g_w_mo, *u_w_mo = _sum_adamw(rv_mo, w_mlp_out, m_w_mlp_out, v_w_mlp_out, "sum_w_mlp_out")
    g_w_mi, *u_w_mi = _sum_adamw(rv_mi, w_mlp_in, m_w_mlp_in, v_w_mlp_in, "sum_w_mlp_in")
    g_w_in, *u_w_in = _sum_adamw(rv_in, w_in, m_w_in, v_w_in, "sum_w_in")
    g_w_out, *u_w_out = _sum_adamw(rv_out, w_out, m_w_out, v_w_out, "sum_w_out")

    grads = [g_w_ada, g_b_ada, g_w_in, g_conv_w, g_w_pool, g_pool_scale, g_w_out, g_ln1_g, g_ln1_b, g_w_mi, g_w_mo, g_ln2_g, g_ln2_b]
    updates = [u_w_ada, u_b_ada, u_w_in, u_conv_w, u_w_pool, u_pool_scale, u_w_out, u_ln1_g, u_ln1_b, u_w_mi, u_w_mo, u_ln2_g, u_ln2_b]
    deltas = [u[0] for u in updates]
    new_m = [u[1] for u in updates]
    new_v = [u[2] for u in updates]
    return (loss, grad_x[None], *grads, *deltas, *new_m, *new_v)
```

```python
import jax
import jax.numpy as jnp
from jax import lax
from jax.experimental import pallas as pl
from jax.experimental.pallas import tpu as pltpu

F32 = jnp.float32
BF16 = jnp.bfloat16
MESH = pl.DeviceIdType.MESH
N_DEV = 8

LN_EPS = 1e-5
DEPTH = 1
DEEPNORM_ALPHA = (2.0 * DEPTH) ** 0.25
POOL_WINDOWS = (2, 4, 8, 16)
HALO = 16

ADAM_LR = 0.001
ADAM_B1 = 0.9
ADAM_B2 = 0.999
ADAM_EPS = 1e-08
ADAM_WD = 0.01
ADAM_STEP = 10

VMEM_LIMIT = 56 * 1024 * 1024

VMEM_SPEC = pl.BlockSpec(memory_space=pltpu.VMEM)
ANY_SPEC = pl.BlockSpec(memory_space=pl.ANY)

NT = (((1,), (1,)), ((), ()))
TN = (((0,), (0,)), ((), ()))


def _my_place():
    return lax.axis_index("x"), lax.axis_index("y"), lax.axis_index("c")


def _slot(x, y, c):
    return 4 * x + 2 * y + c


def _gather_place(ins, outs, a, slot):
    if len(outs[a].shape) == len(ins[a].shape):
        wb = ins[a].shape[1]
        return outs[a].at[:, pl.ds(pl.multiple_of(slot * wb, wb), wb)]
    return outs[a].at[slot]


def _gather_copy(ins, outs, sems, a, k, block, to, from_shard=False):
    send_sems, recv_sems, _ = sems
    dst = _gather_place(ins, outs, a, _slot(*block))
    return pltpu.make_async_remote_copy(
        src_ref=ins[a] if from_shard else dst,
        dst_ref=dst,
        send_sem=send_sems.at[7 * a + k],
        recv_sem=recv_sems.at[7 * a + k],
        device_id=to,
        device_id_type=MESH,
    )


def _gather_peers():
    x, y, c = _my_place()
    return (x, y, c), (x, y, 1 - c), [(1 - x, y), (x, 1 - y), (1 - x, 1 - y)]


def _gather_first(ins, outs, sems):
    me, sibling, chips = _gather_peers()
    first = []
    for a in range(len(ins)):
        first.append(_gather_copy(ins, outs, sems, a, 0, me, sibling, from_shard=True))
        first += [_gather_copy(ins, outs, sems, a, 1 + j, me, (*chip, me[2]), from_shard=True) for j, chip in enumerate(chips)]
    return first


def _gather_mine(ins, outs, sems, a):
    me, _, _ = _gather_peers()
    return pltpu.make_async_copy(ins[a], _gather_place(ins, outs, a, _slot(*me)), sems[2].at[a])


def _gather_start(ins, outs, sems):
    for a in range(len(ins)):
        _gather_mine(ins, outs, sems, a).start()
    for cp in _gather_first(ins, outs, sems):
        cp.start()


def _gather_forward(ins, outs, sems, j):
    me, sibling, chips = _gather_peers()
    for a in range(len(ins)):
        _gather_copy(ins, outs, sems, a, 1 + j, (*chips[j], me[2]), me).wait_recv()
        _gather_copy(ins, outs, sems, a, 4 + j, (*chips[j], me[2]), sibling).start()


def _gather_finish(ins, outs, sems):
    me, sibling, chips = _gather_peers()
    for a in range(len(ins)):
        _gather_copy(ins, outs, sems, a, 0, sibling, me).wait_recv()
        for j, chip in enumerate(chips):
            _gather_copy(ins, outs, sems, a, 4 + j, (*chip, 1 - me[2]), me).wait_recv()
    for cp in _gather_first(ins, outs, sems):
        cp.wait_send()
    for a in range(len(ins)):
        for j, chip in enumerate(chips):
            _gather_copy(ins, outs, sems, a, 4 + j, (*chip, me[2]), sibling).wait_send()
        _gather_mine(ins, outs, sems, a).wait()


def _gather_scratch(n):
    return [pltpu.SemaphoreType.DMA((7 * n,)), pltpu.SemaphoreType.DMA((7 * n,)), pltpu.SemaphoreType.DMA((n,))]


def _gather_out_shape(shards, by_cols):
    return [
        jax.ShapeDtypeStruct((s.shape[0], N_DEV * s.shape[1]) if cols else (N_DEV, *s.shape), s.dtype)
        for s, cols in zip(shards, by_cols)
    ]


def _scatter_copy(g_hbm, recv, sems, k):
    x, y, c = _my_place()
    px, py, pc = x ^ (k >> 2), y ^ ((k >> 1) & 1), c ^ (k & 1)
    return pltpu.make_async_remote_copy(
        src_ref=g_hbm.at[_slot(px, py, pc)],
        dst_ref=recv.at[_slot(x, y, c)],
        send_sem=sems[0].at[k - 1],
        recv_sem=sems[1].at[k - 1],
        device_id=(px, py, pc),
        device_id_type=MESH,
    )


def _scatter_start(g_hbm, recv, sems):
    me = _slot(*_my_place())
    pltpu.make_async_copy(g_hbm.at[me], recv.at[me], sems[2]).start()
    for k in range(1, N_DEV):
        _scatter_copy(g_hbm, recv, sems, k).start()


def _scatter_finish(g_hbm, recv, sems):
    me = _slot(*_my_place())
    pltpu.make_async_copy(g_hbm.at[me], recv.at[me], sems[2]).wait()
    for k in range(1, N_DEV):
        _scatter_copy(g_hbm, recv, sems, k).wait_recv()
    for k in range(1, N_DEV):
        _scatter_copy(g_hbm, recv, sems, k).wait_send()


def _scatter_scratch():
    return [pltpu.SemaphoreType.DMA((N_DEV - 1,)), pltpu.SemaphoreType.DMA((N_DEV - 1,)), pltpu.SemaphoreType.DMA]


def _allgather(shards, name, in_vmem, scatter=None, by_cols=None):
    n = len(shards)
    by_cols = by_cols or [False] * n
    ns = 0 if scatter is None else 1

    def body(*refs):
        ins, outs = refs[:n], refs[n + ns : 2 * n + ns]
        rest = refs[2 * n + ns :]
        if ns:
            g_hbm, recv, sems, s_sems = refs[n], rest[0], rest[1:4], rest[4:7]
            _scatter_start(g_hbm, recv, s_sems)
        else:
            sems = rest[0:3]
        _gather_start(ins, outs, sems)
        for j in range(3):
            _gather_forward(ins, outs, sems, j)
        _gather_finish(ins, outs, sems)
        if ns:
            _scatter_finish(g_hbm, recv, s_sems)

    spec = VMEM_SPEC if in_vmem else ANY_SPEC
    extra_in, extra_out, extra_spec, extra_scratch = [], [], [], []
    if ns:
        extra_in, extra_spec = [scatter], [ANY_SPEC]
        extra_out = [jax.ShapeDtypeStruct(scatter.shape, scatter.dtype)]
        extra_scratch = _scatter_scratch()
    return pl.pallas_call(
        body,
        name=name,
        out_shape=_gather_out_shape(shards, by_cols) + extra_out,
        in_specs=[spec] * n + extra_spec,
        out_specs=[spec] * n + extra_spec,
        scratch_shapes=_gather_scratch(n) + extra_scratch,
        compiler_params=pltpu.CompilerParams(vmem_limit_bytes=VMEM_LIMIT),
    )(*shards, *extra_in)


def _adamw_math(w, g, m, v):
    m = ADAM_B1 * m + (1.0 - ADAM_B1) * g
    v = ADAM_B2 * v + (1.0 - ADAM_B2) * (g * g)
    m_hat = m / (1.0 - ADAM_B1**ADAM_STEP)
    v_hat = v / (1.0 - ADAM_B2**ADAM_STEP)
    delta = -ADAM_LR * (m_hat / (jnp.sqrt(v_hat) + ADAM_EPS) + ADAM_WD * w)
    return delta, m, v


ROW_CHUNK = 64


def _sum_adamw(recv, w, m, v, name):
    _, rows, cols = recv.shape

    def body(r_ref, w_ref, m_ref, v_ref, grad_ref, delta_ref, nm_ref, nv_ref):
        def step(r, carry):
            rs = pl.ds(pl.multiple_of(r * ROW_CHUNK, ROW_CHUNK), ROW_CHUNK)
            g = r_ref[0, rs, :].astype(F32)
            for k in range(1, N_DEV):
                g = g + r_ref[k, rs, :].astype(F32)
            delta, nm, nv = _adamw_math(w_ref[0, rs, :], g, m_ref[0, rs, :], v_ref[0, rs, :])
            grad_ref[0, rs, :] = g
            delta_ref[0, rs, :] = delta
            nm_ref[0, rs, :] = nm
            nv_ref[0, rs, :] = nv
            return carry

        lax.fori_loop(0, rows // ROW_CHUNK, step, 0)

    out = jax.ShapeDtypeStruct(w.shape, F32)
    return pl.pallas_call(
        body,
        name=name,
        out_shape=[out] * 4,
        in_specs=[VMEM_SPEC] * 4,
        out_specs=[VMEM_SPEC] * 4,
        compiler_params=pltpu.CompilerParams(vmem_limit_bytes=VMEM_LIMIT),
    )(recv, w, m, v)


def _adamw_multi(items, name):
    n = len(items)

    def body(*refs):
        ins, outs = refs[: 4 * n], refs[4 * n :]
        for a in range(n):
            w_ref, g_ref, m_ref, v_ref = ins[4 * a : 4 * a + 4]
            d_ref, nm_ref, nv_ref = outs[3 * a : 3 * a + 3]
            shape = w_ref.shape
            if len(shape) == 3 and shape[1] % ROW_CHUNK == 0 and shape[1] > ROW_CHUNK:

                def step(r, carry, w_ref=w_ref, g_ref=g_ref, m_ref=m_ref, v_ref=v_ref, d_ref=d_ref, nm_ref=nm_ref, nv_ref=nv_ref):
                    rs = pl.ds(pl.multiple_of(r * ROW_CHUNK, ROW_CHUNK), ROW_CHUNK)
                    delta, nm, nv = _adamw_math(w_ref[0, rs, :], g_ref[0, rs, :], m_ref[0, rs, :], v_ref[0, rs, :])
                    d_ref[0, rs, :] = delta
                    nm_ref[0, rs, :] = nm
                    nv_ref[0, rs, :] = nv
                    return carry

                lax.fori_loop(0, shape[1] // ROW_CHUNK, step, 0)
            else:
                delta, nm, nv = _adamw_math(w_ref[...], g_ref[...], m_ref[...], v_ref[...])
                d_ref[...] = delta
                nm_ref[...] = nm
                nv_ref[...] = nv

    flat = [a for it in items for a in it]
    out_shape = [jax.ShapeDtypeStruct(it[0].shape, F32) for it in items for _ in range(3)]
    outs = pl.pallas_call(
        body,
        name=name,
        out_shape=out_shape,
        in_specs=[VMEM_SPEC] * (4 * n),
        out_specs=[VMEM_SPEC] * (3 * n),
        compiler_params=pltpu.CompilerParams(vmem_limit_bytes=VMEM_LIMIT),
    )(*flat)
    return [tuple(outs[3 * a : 3 * a + 3]) for a in range(n)]


def _ada_fwd(c_all, w_ada, b_mine):
    def body(c_ref, w_ref, b_ref, o_ref):
        cv = c_ref[...]
        cond = cv * jax.nn.sigmoid(cv)
        o_ref[...] = jnp.dot(cond, w_ref[0], precision=lax.Precision.HIGHEST, preferred_element_type=F32) + b_ref[...]

    return pl.pallas_call(
        body,
        name="ada_fwd",
        out_shape=jax.ShapeDtypeStruct((N_DEV, w_ada.shape[2]), F32),
        in_specs=[VMEM_SPEC] * 3,
        out_specs=VMEM_SPEC,
        compiler_params=pltpu.CompilerParams(vmem_limit_bytes=VMEM_LIMIT),
    )(c_all, w_ada, b_mine)


def _ln_fwd(r):
    mu = jnp.mean(r, axis=-1, keepdims=True)
    d = r - mu
    var = jnp.mean(d * d, axis=-1, keepdims=True)
    rstd = lax.rsqrt(var + LN_EPS)
    return d * rstd, rstd


def _ln_bwd(dxh, xhat, rstd):
    m1 = jnp.mean(dxh, axis=-1, keepdims=True)
    m2 = jnp.mean(dxh * xhat, axis=-1, keepdims=True)
    return rstd * (dxh - m1 - xhat * m2)


def _colsum(a):
    return jnp.sum(a, axis=0, keepdims=True)


def _pool_features(vp, vp_s, row, tm):
    feats, inv_cnts = [], []
    for g, win in enumerate(POOL_WINDOWS):
        cols = slice(128 * g, 128 * g + 128)
        s = vp[:, cols]
        for j in range(1, win):
            s = s + vp_s[HALO - j : HALO - j + tm, cols]
        inv_cnt = 1.0 / jnp.minimum(row + 1, win).astype(F32)
        feats.append(s * inv_cnt - vp[:, cols])
        inv_cnts.append(inv_cnt)
    return feats, inv_cnts


def _f1(x, mod, w_in, conv_w, w_pool, pool_scale, w_out, tm, gather, by_cols):
    T, D = x.shape
    ZW = w_in.shape[1]
    CC = ZW // 4
    nt = T // tm
    ng = len(gather)
    fwd_steps = [max(nt - 3 + j, 0) for j in range(3)]

    def body(*refs):
        x_ref, mod_ref, win_ref, cw_ref, wp_ref, ps_ref, wout_ref = refs[:7]
        g_ins = refs[7 : 7 + ng]
        z_ref, h_ref, xhat_ref, rstd_ref, mix_ref = refs[7 + ng : 12 + ng]
        g_outs = refs[12 + ng : 12 + 2 * ng]
        cv_s, vp_s = refs[12 + 2 * ng : 14 + 2 * ng]
        g_sems = refs[14 + 2 * ng :]
        i = pl.program_id(0)

        @pl.when(i == 0)
        def _():
            _gather_start(g_ins, g_outs, g_sems)
            cv_s[0:HALO, :] = jnp.zeros((HALO, CC), F32)
            vp_s[0:HALO, :] = jnp.zeros((HALO, CC), F32)

        xv = x_ref[...]
        sh1, sc1, g1 = mod_ref[0:1, :], mod_ref[1:2, :], mod_ref[2:3, :]
        h = (xv * (1.0 + sc1) + sh1).astype(BF16)
        h_ref[...] = h
        z = jnp.dot(h, win_ref[...], preferred_element_type=F32)
        z_ref[...] = z.astype(BF16)
        gb, gc, vc, vp = z[:, 0:CC], z[:, CC : 2 * CC], z[:, 2 * CC : 3 * CC], z[:, 3 * CC : 4 * CC]
        cv = gc * vc
        cv_s[HALO : HALO + tm, :] = cv
        vp_s[HALO : HALO + tm, :] = vp
        conv = cw_ref[0:1, :] * cv_s[HALO - 2 : HALO - 2 + tm, :] + cw_ref[1:2, :] * cv_s[HALO - 1 : HALO - 1 + tm, :] + cw_ref[2:3, :] * cv
        parts = [gb * conv]
        row = i * tm + lax.broadcasted_iota(jnp.int32, (tm, 1), 0)
        feats, _ = _pool_features(vp, vp_s, row, tm)
        for g in range(len(POOL_WINDOWS)):
            pw = jnp.dot(feats[g].astype(BF16), wp_ref[g], preferred_element_type=F32)
            parts.append(pw * ps_ref[0:1, 128 * g : 128 * g + 128])
        cv_s[0:HALO, :] = cv_s[tm : tm + HALO, :]
        vp_s[0:HALO, :] = vp_s[tm : tm + HALO, :]
        ycat = jnp.concatenate(parts, axis=1).astype(BF16)
        mix = jnp.dot(ycat, wout_ref[...], preferred_element_type=F32)
        mix_ref[...] = mix
        xhat, rstd = _ln_fwd(DEEPNORM_ALPHA * xv + (1.0 + g1) * mix)
        xhat_ref[...] = xhat
        rstd_ref[...] = rstd

        for j in range(3):

            @pl.when(i == fwd_steps[j])
            def _(j=j):
                _gather_forward(g_ins, g_outs, g_sems, j)

        @pl.when(i == nt - 1)
        def _():
            _gather_finish(g_ins, g_outs, g_sems)

    tile = lambda w: pl.BlockSpec((tm, w), lambda i: (i, 0))
    return pl.pallas_call(
        body,
        name="f1",
        grid=(nt,),
        out_shape=[
            jax.ShapeDtypeStruct((T, ZW), BF16),
            jax.ShapeDtypeStruct((T, D), BF16),
            jax.ShapeDtypeStruct((T, D), F32),
            jax.ShapeDtypeStruct((T, 1), F32),
            jax.ShapeDtypeStruct((T, D), F32),
        ]
        + _gather_out_shape(gather, by_cols),
        in_specs=[tile(D)] + [VMEM_SPEC] * 6 + [ANY_SPEC] * ng,
        out_specs=[tile(ZW), tile(D), tile(D), tile(1), tile(D)] + [ANY_SPEC] * ng,
        scratch_shapes=[pltpu.VMEM((HALO + tm, CC), F32), pltpu.VMEM((HALO + tm, CC), F32)] + _gather_scratch(ng),
        compiler_params=pltpu.CompilerParams(dimension_semantics=("arbitrary",), vmem_limit_bytes=VMEM_LIMIT),
    )(x, mod, w_in, conv_w, w_pool, pool_scale, w_out, *gather)


def _fb2(xhat1, target, mod, ln, w_mi, w_mo, tm):
    T, D = xhat1.shape
    H = w_mi.shape[1]
    hc = min(1024, H)
    nb = H // hc
    nt = T // tm

    def body(xh1_ref, t_ref, mod_ref, ln_ref, wmi_ref, wmo_ref, dx1_ref, h2_ref, a_ref, du_ref, df_ref, acc_ref):
        i = pl.program_id(0)

        @pl.when(i == 0)
        def _():
            acc_ref[...] = jnp.zeros((8, D), F32)

        sh2, sc2, g2 = mod_ref[3:4, :], mod_ref[4:5, :], mod_ref[5:6, :]
        x1 = xh1_ref[...] * ln_ref[0:1, :] + ln_ref[1:2, :]
        h2 = (x1 * (1.0 + sc2) + sh2).astype(BF16)
        h2_ref[...] = h2
        f = jnp.zeros((tm, D), F32)
        for k in range(nb):
            ks = slice(k * hc, (k + 1) * hc)
            r = jnp.maximum(jnp.dot(h2, wmi_ref[:, ks], preferred_element_type=F32), 0.0)
            du_ref[:, ks] = r.astype(BF16)
            a = (r * r).astype(BF16)
            a_ref[:, ks] = a
            f = f + jnp.dot(a, wmo_ref[ks, :], preferred_element_type=F32)
        xhat2, rstd2 = _ln_fwd(DEEPNORM_ALPHA * x1 + (1.0 + g2) * f)
        ln2_g = ln_ref[2:3, :]
        d = xhat2 * ln2_g + ln_ref[3:4, :] - t_ref[...]
        dy = d * (1.0 / D)
        dr2 = _ln_bwd(dy * ln2_g, xhat2, rstd2)
        df = ((1.0 + g2) * dr2).astype(BF16)
        df_ref[...] = df
        dh2 = jnp.zeros((tm, D), F32)
        for k in range(nb):
            ks = slice(k * hc, (k + 1) * hc)
            da = lax.dot_general(df, wmo_ref[ks, :], NT, preferred_element_type=F32)
            du = (da * (2.0 * du_ref[:, ks].astype(F32))).astype(BF16)
            du_ref[:, ks] = du
            dh2 = dh2 + lax.dot_general(du, wmi_ref[:, ks], NT, preferred_element_type=F32)
        dx1_ref[...] = DEEPNORM_ALPHA * dr2 + dh2 * (1.0 + sc2)
        acc_ref[0:1, :] += _colsum(dy * xhat2)
        acc_ref[1:2, :] += _colsum(dy)
        acc_ref[2:3, :] += _colsum(dh2)
        acc_ref[3:4, :] += _colsum(dh2 * x1)
        acc_ref[4:5, :] += _colsum(dr2 * f)
        acc_ref[5:6, :] += jnp.zeros((1, D), F32) + (0.5 / D) * jnp.sum(d * d)

    tile = lambda w: pl.BlockSpec((tm, w), lambda i: (i, 0))
    return pl.pallas_call(
        body,
        name="fb2",
        grid=(nt,),
        out_shape=[
            jax.ShapeDtypeStruct((T, D), F32),
            jax.ShapeDtypeStruct((T, D), BF16),
            jax.ShapeDtypeStruct((T, H), BF16),
            jax.ShapeDtypeStruct((T, H), BF16),
            jax.ShapeDtypeStruct((T, D), BF16),
            jax.ShapeDtypeStruct((8, D), F32),
        ],
        in_specs=[tile(D), tile(D)] + [VMEM_SPEC] * 4,
        out_specs=[tile(D), tile(D), tile(H), tile(H), tile(D), pl.BlockSpec((8, D), lambda i: (0, 0))],
        compiler_params=pltpu.CompilerParams(dimension_semantics=("arbitrary",), vmem_limit_bytes=VMEM_LIMIT),
    )(xhat1, target, mod, ln, w_mi, w_mo)


def _b1(dx1, xhat1, rstd1, x, mix, z, mod, ln, w_out, w_in, conv_w, w_pool, pool_scale, tm, scatter):
    T, D = x.shape
    ZW = w_in.shape[1]
    CC = ZW // 4
    nt = T // tm
    hb = tm // HALO

    def body(dx1_ref, xh1_ref, rstd_ref, x_ref, mix_ref, z_ref, zh_ref, mod_ref, ln_ref, wout_ref, win_ref, cw_ref, wp_ref, ps_ref, s_hbm,
             dx_ref, dmix_ref, ycat_ref, dz_ref, acc_ref, gcw_ref, gwp_ref, s_recv, cv_s, vp_s, e_s, q_s, *s_sems):
        i = pl.program_id(0)
        j = nt - 1 - i

        @pl.when(i == 0)
        def _():
            _scatter_start(s_hbm, s_recv, s_sems)
            acc_ref[...] = jnp.zeros((8, D), F32)
            gcw_ref[...] = jnp.zeros((8, CC), F32)
            gwp_ref[...] = jnp.zeros(gwp_ref.shape, F32)
            e_s[tm : tm + HALO, :] = jnp.zeros((HALO, CC), F32)
            q_s[tm : tm + HALO, :] = jnp.zeros((HALO, CC), F32)

        sh1, sc1, g1 = mod_ref[0:1, :], mod_ref[1:2, :], mod_ref[2:3, :]
        dx1 = dx1_ref[...]
        xhat1 = xh1_ref[...]
        acc_ref[0:1, :] += _colsum(dx1 * xhat1)
        acc_ref[1:2, :] += _colsum(dx1)
        dr1 = _ln_bwd(dx1 * ln_ref[0:1, :], xhat1, rstd_ref[...])
        acc_ref[4:5, :] += _colsum(dr1 * mix_ref[...])
        dmix = ((1.0 + g1) * dr1).astype(BF16)
        dmix_ref[...] = dmix
        dycat = lax.dot_general(dmix, wout_ref[...], NT, preferred_element_type=F32)

        z = z_ref[...].astype(F32)
        zh = zh_ref[...].astype(F32) * jnp.where(j > 0, 1.0, 0.0)
        gb, gc, vc, vp = z[:, 0:CC], z[:, CC : 2 * CC], z[:, 2 * CC : 3 * CC], z[:, 3 * CC : 4 * CC]
        cv = gc * vc
        cv_s[0:HALO, :] = zh[:, CC : 2 * CC] * zh[:, 2 * CC : 3 * CC]
        cv_s[HALO : HALO + tm, :] = cv
        vp_s[0:HALO, :] = zh[:, 3 * CC : 4 * CC]
        vp_s[HALO : HALO + tm, :] = vp
        cv_m2 = cv_s[HALO - 2 : HALO - 2 + tm, :]
        cv_m1 = cv_s[HALO - 1 : HALO - 1 + tm, :]
        w0, w1, w2 = cw_ref[0:1, :], cw_ref[1:2, :], cw_ref[2:3, :]
        conv = w0 * cv_m2 + w1 * cv_m1 + w2 * cv
        dyc = dycat[:, 0:CC]
        e = dyc * gb
        e_s[0:tm, :] = e
        dcv = w2 * e + w1 * e_s[1 : 1 + tm, :] + w0 * e_s[2 : 2 + tm, :]
        gcw_ref[0:1, :] += _colsum(e * cv_m2)
        gcw_ref[1:2, :] += _colsum(e * cv_m1)
        gcw_ref[2:3, :] += _colsum(e * cv)
        y_parts = [gb * conv]
        dz_parts = [dyc * conv, dcv * vc, dcv * gc]

        row = j * tm + lax.broadcasted_iota(jnp.int32, (tm, 1), 0)
        feats, inv_cnts = _pool_features(vp, vp_s, row, tm)
        gps_parts = []
        for g, win in enumerate(POOL_WINDOWS):
            cols = slice(128 * g, 128 * g + 128)
            p = feats[g].astype(BF16)
            scale = ps_ref[0:1, cols]
            pw = jnp.dot(p, wp_ref[g], preferred_element_type=F32)
            y_parts.append(pw * scale)
            dyp = dycat[:, CC + 128 * g : CC + 128 * g + 128]
            gps_parts.append(_colsum(dyp * pw))
            dpw = (dyp * scale).astype(BF16)
            gwp_ref[g] += lax.dot_general(p, dpw, TN, preferred_element_type=F32)
            dp = lax.dot_general(dpw, wp_ref[g], NT, preferred_element_type=F32)
            q = dp * inv_cnts[g]
            q_s[0:tm, cols] = q
            sq = q
            for jj in range(1, win):
                sq = sq + q_s[jj : jj + tm, cols]
            dz_parts.append(sq - dp)
        gcw_ref[3:4, :] += jnp.concatenate(gps_parts, axis=1)
        ycat_ref[...] = jnp.concatenate(y_parts, axis=1).astype(BF16)
        dz = jnp.concatenate(dz_parts, axis=1).astype(BF16)
        dz_ref[...] = dz
        dh = lax.dot_general(dz, win_ref[...], NT, preferred_element_type=F32)
        acc_ref[2:3, :] += _colsum(dh)
        acc_ref[3:4, :] += _colsum(dh * x_ref[...])
        dx_ref[...] = DEEPNORM_ALPHA * dr1 + dh * (1.0 + sc1)
        e_s[tm : tm + HALO, :] = e_s[0:HALO, :]
        q_s[tm : tm + HALO, :] = q_s[0:HALO, :]

        @pl.when(i == nt - 1)
        def _():
            _scatter_finish(s_hbm, s_recv, s_sems)

    tile = lambda w: pl.BlockSpec((tm, w), lambda i: (nt - 1 - i, 0))
    halo = pl.BlockSpec((HALO, ZW), lambda i: (jnp.maximum((nt - 1 - i) * hb - 1, 0), 0))
    fixed = lambda shape: pl.BlockSpec(shape, lambda i: (0,) * len(shape))
    return pl.pallas_call(
        body,
        name="b1",
        grid=(nt,),
        out_shape=[
            jax.ShapeDtypeStruct((T, D), F32),
            jax.ShapeDtypeStruct((T, D), BF16),
            jax.ShapeDtypeStruct((T, D), BF16),
            jax.ShapeDtypeStruct((T, ZW), BF16),
            jax.ShapeDtypeStruct((8, D), F32),
            jax.ShapeDtypeStruct((8, CC), F32),
            jax.ShapeDtypeStruct(w_pool.shape, F32),
            jax.ShapeDtypeStruct(scatter.shape, scatter.dtype),
        ],
        in_specs=[tile(D), tile(D), tile(1), tile(D), tile(D), tile(ZW), halo] + [VMEM_SPEC] * 7 + [ANY_SPEC],
        out_specs=[tile(D), tile(D), tile(D), tile(ZW), fixed((8, D)), fixed((8, CC)), fixed(w_pool.shape), ANY_SPEC],
        scratch_shapes=[
            pltpu.VMEM((HALO + tm, CC), F32),
            pltpu.VMEM((HALO + tm, CC), F32),
            pltpu.VMEM((tm + HALO, CC), F32),
            pltpu.VMEM((tm + HALO, CC), F32),
        ]
        + _scatter_scratch(),
        compiler_params=pltpu.CompilerParams(dimension_semantics=("arbitrary",), vmem_limit_bytes=VMEM_LIMIT),
    )(dx1, xhat1, rstd1, x, mix, z, z, mod, ln, w_out, w_in, conv_w, w_pool, pool_scale, scatter)


def _wgrad(a, b, bk, n_groups, bt, name, owners=None, scatter=None):
    T, K = a.shape
    N = b.shape[1]
    nk, nt, ng = K // bk, T // bt, N // n_groups
    nc = min(512, ng)
    ns = 0 if scatter is None else 1

    def body(*refs):
        a_ref, b_ref = refs[0], refs[1]
        o_ref = refs[2 + ns]
        acc = refs[3 + 2 * ns]
        if ns:
            s_hbm, s_recv, s_sems = refs[2], refs[4], refs[6:9]
        kk, gg, t = pl.program_id(0), pl.program_id(1), pl.program_id(2)

        if ns:

            @pl.when((kk == 0) & (gg == 0) & (t == 0))
            def _():
                _scatter_start(s_hbm, s_recv, s_sems)

        @pl.when(t == 0)
        def _():
            acc[...] = jnp.zeros(acc.shape, F32)

        at = a_ref[...].T
        for c in range(ng // nc):
            cs = slice(c * nc, (c + 1) * nc)
            acc[:, cs] += jnp.dot(at, b_ref[:, cs], preferred_element_type=F32)

        @pl.when(t == nt - 1)
        def _():
            if owners is None:
                o_ref[...] = acc[...].astype(BF16)
            else:
                per = N // owners
                for o in range(ng // per):
                    o_ref[o] = acc[:, o * per : (o + 1) * per].astype(BF16)

        if ns:

            @pl.when((kk == nk - 1) & (gg == n_groups - 1) & (t == nt - 1))
            def _():
                _scatter_finish(s_hbm, s_recv, s_sems)

    if owners is None:
        out_shape = [jax.ShapeDtypeStruct((K, N), BF16)]
        out_specs = [pl.BlockSpec((bk, ng), lambda k, g, t: (k, g))]
    else:
        assert bk == K
        per = N // owners
        out_shape = [jax.ShapeDtypeStruct((owners, K, per), BF16)]
        out_specs = [pl.BlockSpec((ng // per, K, per), lambda k, g, t: (g, 0, 0))]
    ins, in_specs = [a, b], [pl.BlockSpec((bt, bk), lambda k, g, t: (t, k)), pl.BlockSpec((bt, ng), lambda k, g, t: (t, g))]
    scratch = [pltpu.VMEM((bk, ng), F32)]
    if ns:
        ins.append(scatter)
        in_specs.append(ANY_SPEC)
        out_shape.append(jax.ShapeDtypeStruct(scatter.shape, scatter.dtype))
        out_specs.append(ANY_SPEC)
        scratch += _scatter_scratch()
    outs = pl.pallas_call(
        body,
        name=name,
        grid=(nk, n_groups, nt),
        out_shape=out_shape,
        in_specs=in_specs,
        out_specs=out_specs,
        scratch_shapes=scratch,
        compiler_params=pltpu.CompilerParams(dimension_semantics=("arbitrary", "arbitrary", "arbitrary"), vmem_limit_bytes=VMEM_LIMIT),
    )(*ins)
    return outs if ns else outs[0]


def _small_grads(acc1_t, acc2_t, gcw_t, gwp_all, cond_t, my_slot, w_cols):
    D = acc1_t.shape[2]
    n_chunk = D // 128
    q_mine = w_cols // 128

    def total(ref, r):
        s = ref[r, 0:1, :]
        for k in range(1, N_DEV):
            s = s + ref[r, k : k + 1, :]
        return s

    def body(slot_ref, a1_ref, a2_ref, gcw_ref, gwp_ref, ct_ref, gb_ref, gw_ref, gln_ref, gcwo_ref, gwpo_ref, dm_s):
        for s, (ref, r) in enumerate([(a1_ref, 2), (a1_ref, 3), (a1_ref, 4), (a2_ref, 2), (a2_ref, 3), (a2_ref, 4)]):
            gb_ref[0:1, s * D : (s + 1) * D] = total(ref, r)
            for qq in range(n_chunk):
                dm_s[s * n_chunk + qq] = ref[r, :, 128 * qq : 128 * qq + 128]
        gln_ref[0:1, :] = total(a1_ref, 0)
        gln_ref[1:2, :] = total(a1_ref, 1)
        gln_ref[2:3, :] = total(a2_ref, 0)
        gln_ref[3:4, :] = total(a2_ref, 1)
        gcwo_ref[...] = jnp.zeros(gcwo_ref.shape, F32)
        for r in range(4):
            gcwo_ref[r : r + 1, :] = total(gcw_ref, r)
        wp = gwp_ref[0]
        for k in range(1, N_DEV):
            wp = wp + gwp_ref[k]
        gwpo_ref[0] = wp
        ct = ct_ref[...]
        cond_t = ct * jax.nn.sigmoid(ct)
        q0 = slot_ref[0] * q_mine
        for q in range(q_mine):
            dm = dm_s[q0 + q]
            out = cond_t[:, 0:1] * dm[0:1, :]
            for k in range(1, N_DEV):
                out = out + cond_t[:, k : k + 1] * dm[k : k + 1, :]
            gw_ref[0, :, 128 * q : 128 * q + 128] = out

    CC = gcw_t.shape[2]
    return pl.pallas_call(
        body,
        name="small_grads",
        out_shape=[
            jax.ShapeDtypeStruct((1, 6 * D), F32),
            jax.ShapeDtypeStruct((1, D, w_cols), F32),
            jax.ShapeDtypeStruct((4, D), F32),
            jax.ShapeDtypeStruct((8, CC), F32),
            jax.ShapeDtypeStruct((1, *gwp_all.shape[1:]), F32),
        ],
        in_specs=[pl.BlockSpec(memory_space=pltpu.SMEM)] + [VMEM_SPEC] * 5,
        out_specs=[VMEM_SPEC] * 5,
        scratch_shapes=[pltpu.VMEM((6 * n_chunk, N_DEV, 128), F32)],
        compiler_params=pltpu.CompilerParams(vmem_limit_bytes=VMEM_LIMIT),
    )(my_slot, acc1_t, acc2_t, gcw_t, gwp_all, cond_t)


def kernel(x, c, w_ada, b_ada, w_in, conv_w, w_pool, pool_scale, w_out, ln1_g, ln1_b, w_mlp_in, w_mlp_out, ln2_g, ln2_b, loss_target, m_w_ada, m_b_ada, m_w_in, m_conv_w, m_w_pool, m_pool_scale, m_w_out, m_ln1_g, m_ln1_b, m_w_mlp_in, m_w_mlp_out, m_ln2_g, m_ln2_b, v_w_ada, v_b_ada, v_w_in, v_conv_w, v_w_pool, v_pool_scale, v_w_out, v_ln1_g, v_ln1_b, v_w_mlp_in, v_w_mlp_out, v_ln2_g, v_ln2_b):
    T, D = x.shape[1], x.shape[2]
    H = w_mlp_out.shape[1] * N_DEV
    ZW = w_in.shape[2] * N_DEV
    CC = ZW // 4
    tm = min(512, T // 2)
    bt = min(1024, T)
    ax, ay, ac = _my_place()
    me = _slot(ax, ay, ac)

    w_in_f, w_out_g, cw_g, c_g = _allgather(
        [w_in[0].astype(BF16), w_out[0].astype(BF16), conv_w[0], c], "gather_weights", in_vmem=False,
        by_cols=[True, False, False, False])
    w_out_f = w_out_g.reshape(D, D)
    conv_w_f = jnp.transpose(cw_g, (1, 0, 2)).reshape(conv_w.shape[1], CC)
    c_all = c_g.reshape(N_DEV, D)

    w_cols = w_ada.shape[2]
    b_mine = lax.dynamic_slice(b_ada, (0, me * w_cols), (1, w_cols))
    mod_part = _ada_fwd(c_all, w_ada, b_mine)
    (mod_g,) = _allgather([mod_part], "gather_mod", in_vmem=True)
    mod = lax.dynamic_index_in_dim(mod_g, me, axis=1, keepdims=False).reshape(6, D)

    ln = jnp.concatenate([ln1_g, ln1_b, ln2_g, ln2_b], axis=0)
    w_pool16 = w_pool[0].astype(BF16)
    xs, target = x[0], loss_target[0]

    z, h, xhat1, rstd1, mix, w_mi_f, w_mo_g = _f1(
        xs, mod, w_in_f, conv_w_f, w_pool16, pool_scale, w_out_f, tm,
        [w_mlp_in[0].astype(BF16), w_mlp_out[0].astype(BF16)], [True, False])
    dx1, h2, a, du, df, acc2 = _fb2(xhat1, target, mod, ln, w_mi_f, w_mo_g.reshape(H, D), tm // 2)
    loss = lax.psum(acc2[5, 0], ("x", "y", "c"))

    gp_mo = _wgrad(a, df, D, 1, bt, "wgrad_mlp_out").reshape(N_DEV, H // N_DEV, D)
    gp_mi, rv_mo = _wgrad(h2, du, D, 2, bt, "wgrad_mlp_in", owners=N_DEV, scatter=gp_mo)
    grad_x, dmix, ycat, dz, acc1, gcw, gwp, rv_mi = _b1(
        dx1, xhat1, rstd1, xs, mix, z, mod, ln, w_out_f, w_in_f, conv_w_f, w_pool16, pool_scale, tm, gp_mi)
    gp_out = _wgrad(ycat, dmix, D, 1, bt, "wgrad_out").reshape(N_DEV, D // N_DEV, D)
    gp_in, rv_out = _wgrad(h, dz, D, 1, bt, "wgrad_in", owners=N_DEV, scatter=gp_out)

    acc1_g, acc2_g, gcw_g, gwp_g, rv_in = _allgather([acc1, acc2, gcw, gwp], "gather_small", in_vmem=True, scatter=gp_in)
    g_b_ada, g_w_ada, g_ln, g_cw, g_w_pool = _small_grads(
        jnp.transpose(acc1_g, (1, 0, 2)), jnp.transpose(acc2_g, (1, 0, 2)), jnp.transpose(gcw_g, (1, 0, 2)), gwp_g,
        c_all.T, jnp.reshape(me, (1,)).astype(jnp.int32), w_cols)
    cc_mine = conv_w.shape[2]
    g_conv_w = lax.dynamic_slice(g_cw, (0, me * cc_mine), (conv_w.shape[1], cc_mine))[None]
    g_pool_scale = g_cw[3:4, :]
    g_ln1_g, g_ln1_b, g_ln2_g, g_ln2_b = g_ln[0:1], g_ln[1:2], g_ln[2:3], g_ln[3:4]

    small = _adamw_multi(
        [
            (b_ada, g_b_ada, m_b_ada, v_b_ada),
            (conv_w, g_conv_w, m_conv_w, v_conv_w),
            (w_pool, g_w_pool, m_w_pool, v_w_pool),
            (pool_scale, g_pool_scale, m_pool_scale, v_pool_scale),
            (ln1_g, g_ln1_g, m_ln1_g, v_ln1_g),
            (ln1_b, g_ln1_b, m_ln1_b, v_ln1_b),
            (ln2_g, g_ln2_g, m_ln2_g, v_ln2_g),
            (ln2_b, g_ln2_b, m_ln2_b, v_ln2_b),
        ],
        "adamw_small")
    (u_w_ada,) = _adamw_multi([(w_ada, g_w_ada, m_w_ada, v_w_ada)], "adamw_w_ada")
    u_b_ada, u_conv_w, u_w_pool, u_pool_scale, u_ln1_g, u_ln1_b, u_ln2_g, u_ln2_b = small

    g_w_mo, *u_w_mo = _sum_adamw(rv_mo, w_mlp_out, m_w_mlp_out, v_w_mlp_out, "sum_w_mlp_out")
    g_w_mi, *u_w_mi = _sum_adamw(rv_mi, w_mlp_in, m_w_mlp_in, v_w_mlp_in, "sum_w_mlp_in")
    g_w_in, *u_w_in = _sum_adamw(rv_in, w_in, m_w_in, v_w_in, "sum_w_in")
    g_w_out, *u_w_out = _sum_adamw(rv_out, w_out, m_w_out, v_w_out, "sum_w_out")

    grads = [g_w_ada, g_b_ada, g_w_in, g_conv_w, g_w_pool, g_pool_scale, g_w_out, g_ln1_g, g_ln1_b, g_w_mi, g_w_mo, g_ln2_g, g_ln2_b]
    updates = [u_w_ada, u_b_ada, u_w_in, u_conv_w, u_w_pool, u_pool_scale, u_w_out, u_ln1_g, u_ln1_b, u_w_mi, u_w_mo, u_ln2_g, u_ln2_b]
    deltas = [u[0] for u in updates]
    new_m = [u[1] for u in updates]
    new_v = [u[2] for u in updates]
    return (loss, grad_x[None], *grads, *deltas, *new_m, *new_v)
```

```python
import functools

import jax
import jax.numpy as jnp
from jax import lax
from jax.experimental import pallas as pl
from jax.experimental.pallas import tpu as pltpu

F32 = jnp.float32
BF16 = jnp.bfloat16
MESH = pl.DeviceIdType.MESH
N_DEV = 8

LN_EPS = 1e-5
DEPTH = 1
DEEPNORM_ALPHA = (2.0 * DEPTH) ** 0.25
POOL_WINDOWS = (2, 4, 8, 16)
HALO = 16

ADAM_LR = 0.001
ADAM_B1 = 0.9
ADAM_B2 = 0.999
ADAM_EPS = 1e-08
ADAM_WD = 0.01
ADAM_STEP = 10

VMEM_LIMIT = 56 * 1024 * 1024

VMEM_SPEC = pl.BlockSpec(memory_space=pltpu.VMEM)
ANY_SPEC = pl.BlockSpec(memory_space=pl.ANY)

NT = (((1,), (1,)), ((), ()))
TN = (((0,), (0,)), ((), ()))


def _my_place():
    return lax.axis_index("x"), lax.axis_index("y"), lax.axis_index("c")


def _slot(x, y, c):
    return 4 * x + 2 * y + c


def _gather_place(ins, outs, a, slot):
    if len(outs[a].shape) == len(ins[a].shape):
        wb = ins[a].shape[1]
        return outs[a].at[:, pl.ds(pl.multiple_of(slot * wb, wb), wb)]
    return outs[a].at[slot]


def _gather_copy(ins, outs, sems, a, k, block, to, from_shard=False):
    send_sems, recv_sems, _ = sems
    dst = _gather_place(ins, outs, a, _slot(*block))
    return pltpu.make_async_remote_copy(
        src_ref=ins[a] if from_shard else dst,
        dst_ref=dst,
        send_sem=send_sems.at[7 * a + k],
        recv_sem=recv_sems.at[7 * a + k],
        device_id=to,
        device_id_type=MESH,
    )


def _gather_peers():
    x, y, c = _my_place()
    return (x, y, c), (x, y, 1 - c), [(1 - x, y), (x, 1 - y), (1 - x, 1 - y)]


def _gather_first(ins, outs, sems):
    me, sibling, chips = _gather_peers()
    first = []
    for a in range(len(ins)):
        first.append(_gather_copy(ins, outs, sems, a, 0, me, sibling, from_shard=True))
        first += [_gather_copy(ins, outs, sems, a, 1 + j, me, (*chip, me[2]), from_shard=True) for j, chip in enumerate(chips)]
    return first


def _gather_mine(ins, outs, sems, a):
    me, _, _ = _gather_peers()
    return pltpu.make_async_copy(ins[a], _gather_place(ins, outs, a, _slot(*me)), sems[2].at[a])


def _gather_start(ins, outs, sems):
    for a in range(len(ins)):
        _gather_mine(ins, outs, sems, a).start()
    for cp in _gather_first(ins, outs, sems):
        cp.start()


def _gather_forward(ins, outs, sems, j):
    me, sibling, chips = _gather_peers()
    for a in range(len(ins)):
        _gather_copy(ins, outs, sems, a, 1 + j, (*chips[j], me[2]), me).wait_recv()
        _gather_copy(ins, outs, sems, a, 4 + j, (*chips[j], me[2]), sibling).start()


def _gather_finish(ins, outs, sems):
    me, sibling, chips = _gather_peers()
    for a in range(len(ins)):
        _gather_copy(ins, outs, sems, a, 0, sibling, me).wait_recv()
        for j, chip in enumerate(chips):
            _gather_copy(ins, outs, sems, a, 4 + j, (*chip, 1 - me[2]), me).wait_recv()
    for cp in _gather_first(ins, outs, sems):
        cp.wait_send()
    for a in range(len(ins)):
        for j, chip in enumerate(chips):
            _gather_copy(ins, outs, sems, a, 4 + j, (*chip, me[2]), sibling).wait_send()
        _gather_mine(ins, outs, sems, a).wait()


def _gather_scratch(n):
    return [pltpu.SemaphoreType.DMA((7 * n,)), pltpu.SemaphoreType.DMA((7 * n,)), pltpu.SemaphoreType.DMA((n,))]


def _gather_out_shape(shards, by_cols):
    return [
        jax.ShapeDtypeStruct((s.shape[0], N_DEV * s.shape[1]) if cols else (N_DEV, *s.shape), s.dtype)
        for s, cols in zip(shards, by_cols)
    ]


def _scatter_copy(g_hbm, recv, sems, k):
    x, y, c = _my_place()
    px, py, pc = x ^ (k >> 2), y ^ ((k >> 1) & 1), c ^ (k & 1)
    return pltpu.make_async_remote_copy(
        src_ref=g_hbm.at[_slot(px, py, pc)],
        dst_ref=recv.at[_slot(x, y, c)],
        send_sem=sems[0].at[k - 1],
        recv_sem=sems[1].at[k - 1],
        device_id=(px, py, pc),
        device_id_type=MESH,
    )


def _scatter_start(g_hbm, recv, sems):
    me = _slot(*_my_place())
    pltpu.make_async_copy(g_hbm.at[me], recv.at[me], sems[2]).start()
    for k in range(1, N_DEV):
        _scatter_copy(g_hbm, recv, sems, k).start()


def _scatter_finish(g_hbm, recv, sems):
    me = _slot(*_my_place())
    pltpu.make_async_copy(g_hbm.at[me], recv.at[me], sems[2]).wait()
    for k in range(1, N_DEV):
        _scatter_copy(g_hbm, recv, sems, k).wait_recv()
    for k in range(1, N_DEV):
        _scatter_copy(g_hbm, recv, sems, k).wait_send()


def _scatter_scratch():
    return [pltpu.SemaphoreType.DMA((N_DEV - 1,)), pltpu.SemaphoreType.DMA((N_DEV - 1,)), pltpu.SemaphoreType.DMA]


def _adamw_math(w, g, m, v):
    m = ADAM_B1 * m + (1.0 - ADAM_B1) * g
    v = ADAM_B2 * v + (1.0 - ADAM_B2) * (g * g)
    m_hat = m / (1.0 - ADAM_B1**ADAM_STEP)
    v_hat = v / (1.0 - ADAM_B2**ADAM_STEP)
    delta = -ADAM_LR * (m_hat / (jnp.sqrt(v_hat) + ADAM_EPS) + ADAM_WD * w)
    return delta, m, v


ROW_CHUNK = 64


def _sum_adamw_rows(r_ref, w_ref, m_ref, v_ref, grad_ref, delta_ref, nm_ref, nv_ref):
    def step(r, carry):
        rs = pl.ds(pl.multiple_of(r * ROW_CHUNK, ROW_CHUNK), ROW_CHUNK)
        g = r_ref[0, rs, :].astype(F32)
        for k in range(1, N_DEV):
            g = g + r_ref[k, rs, :].astype(F32)
        delta, nm, nv = _adamw_math(w_ref[0, rs, :], g, m_ref[0, rs, :], v_ref[0, rs, :])
        grad_ref[0, rs, :] = g
        delta_ref[0, rs, :] = delta
        nm_ref[0, rs, :] = nm
        nv_ref[0, rs, :] = nv
        return carry

    lax.fori_loop(0, r_ref.shape[1] // ROW_CHUNK, step, 0)


def _allgather(shards, name, in_vmem, scatter=None, by_cols=None):
    n = len(shards)
    by_cols = by_cols or [False] * n
    ns = 0 if scatter is None else 1

    def body(*refs):
        ins, outs = refs[:n], refs[n + ns : 2 * n + ns]
        rest = refs[2 * n + ns :]
        if ns:
            g_hbm, recv, sems, s_sems = refs[n], rest[0], rest[1:4], rest[4:7]
            _scatter_start(g_hbm, recv, s_sems)
        else:
            sems = rest[0:3]
        _gather_start(ins, outs, sems)
        for j in range(3):
            _gather_forward(ins, outs, sems, j)
        _gather_finish(ins, outs, sems)
        if ns:
            _scatter_finish(g_hbm, recv, s_sems)

    spec = VMEM_SPEC if in_vmem else ANY_SPEC
    extra_in, extra_out, extra_spec, extra_scratch = [], [], [], []
    if ns:
        extra_in, extra_spec = [scatter], [ANY_SPEC]
        extra_out = [jax.ShapeDtypeStruct(scatter.shape, scatter.dtype)]
        extra_scratch = _scatter_scratch()
    return pl.pallas_call(
        body,
        name=name,
        out_shape=_gather_out_shape(shards, by_cols) + extra_out,
        in_specs=[spec] * n + extra_spec,
        out_specs=[spec] * n + extra_spec,
        scratch_shapes=_gather_scratch(n) + extra_scratch,
        compiler_params=pltpu.CompilerParams(vmem_limit_bytes=VMEM_LIMIT),
    )(*shards, *extra_in)


def _sum_adamw(recv, w, m, v, name):
    out = jax.ShapeDtypeStruct(w.shape, F32)
    return pl.pallas_call(
        functools.partial(_sum_adamw_rows),
        name=name,
        out_shape=[out] * 4,
        in_specs=[VMEM_SPEC] * 4,
        out_specs=[VMEM_SPEC] * 4,
        compiler_params=pltpu.CompilerParams(vmem_limit_bytes=VMEM_LIMIT),
    )(recv, w, m, v)


def _adamw_multi(items, name):
    n = len(items)

    def body(*refs):
        ins, outs = refs[: 4 * n], refs[4 * n :]
        for a in range(n):
            w_ref, g_ref, m_ref, v_ref = ins[4 * a : 4 * a + 4]
            d_ref, nm_ref, nv_ref = outs[3 * a : 3 * a + 3]
            shape = w_ref.shape
            if len(shape) == 3 and shape[1] % ROW_CHUNK == 0 and shape[1] > ROW_CHUNK:

                def step(r, carry, w_ref=w_ref, g_ref=g_ref, m_ref=m_ref, v_ref=v_ref, d_ref=d_ref, nm_ref=nm_ref, nv_ref=nv_ref):
                    rs = pl.ds(pl.multiple_of(r * ROW_CHUNK, ROW_CHUNK), ROW_CHUNK)
                    delta, nm, nv = _adamw_math(w_ref[0, rs, :], g_ref[0, rs, :], m_ref[0, rs, :], v_ref[0, rs, :])
                    d_ref[0, rs, :] = delta
                    nm_ref[0, rs, :] = nm
                    nv_ref[0, rs, :] = nv
                    return carry

                lax.fori_loop(0, shape[1] // ROW_CHUNK, step, 0)
            else:
                delta, nm, nv = _adamw_math(w_ref[...], g_ref[...], m_ref[...], v_ref[...])
                d_ref[...] = delta
                nm_ref[...] = nm
                nv_ref[...] = nv

    flat = [a for it in items for a in it]
    out_shape = [jax.ShapeDtypeStruct(it[0].shape, F32) for it in items for _ in range(3)]
    outs = pl.pallas_call(
        body,
        name=name,
        out_shape=out_shape,
        in_specs=[VMEM_SPEC] * (4 * n),
        out_specs=[VMEM_SPEC] * (3 * n),
        compiler_params=pltpu.CompilerParams(vmem_limit_bytes=VMEM_LIMIT),
    )(*flat)
    return [tuple(outs[3 * a : 3 * a + 3]) for a in range(n)]


def _prologue(w_in16, w_out16, conv_w, c, w_ada, b_mine):
    D = c.shape[1]
    wc = w_ada.shape[2]

    def gather_now(ins, outs, sems):
        _gather_start(ins, outs, sems)
        for j in range(3):
            _gather_forward(ins, outs, sems, j)
        _gather_finish(ins, outs, sems)

    def body(win_ref, wout_ref, cw_ref, c_ref, wada_ref, b_ref, win_g, wout_g, cw_g, c_g, mod_g, c_s, mp_s, *sems):
        w_ins, w_outs, w_sems = (win_ref, wout_ref, cw_ref), (win_g, wout_g, cw_g), sems[0:3]
        _gather_start(w_ins, w_outs, w_sems)
        gather_now((c_ref,), (c_g,), sems[3:6])
        for k in range(N_DEV):
            c_s[k : k + 1, :] = c_g[k]
        cv = c_s[...]
        cond = cv * jax.nn.sigmoid(cv)
        mp_s[...] = jnp.dot(cond, wada_ref[0], precision=lax.Precision.HIGHEST, preferred_element_type=F32) + b_ref[...]
        gather_now((mp_s,), (mod_g,), sems[6:9])
        for j in range(3):
            _gather_forward(w_ins, w_outs, w_sems, j)
        _gather_finish(w_ins, w_outs, w_sems)

    weights = [w_in16, w_out16, conv_w]
    return pl.pallas_call(
        body,
        name="prologue",
        out_shape=_gather_out_shape(weights, [True, False, False])
        + [jax.ShapeDtypeStruct((N_DEV, 1, D), F32), jax.ShapeDtypeStruct((N_DEV, N_DEV, wc), F32)],
        in_specs=[ANY_SPEC] * 3 + [VMEM_SPEC] * 3,
        out_specs=[ANY_SPEC] * 3 + [VMEM_SPEC] * 2,
        scratch_shapes=[pltpu.VMEM((N_DEV, D), F32), pltpu.VMEM((N_DEV, wc), F32)]
        + _gather_scratch(3) + _gather_scratch(1) + _gather_scratch(1),
        compiler_params=pltpu.CompilerParams(vmem_limit_bytes=VMEM_LIMIT),
    )(w_in16, w_out16, conv_w, c, w_ada, b_mine)


def _ln_fwd(r):
    mu = jnp.mean(r, axis=-1, keepdims=True)
    d = r - mu
    var = jnp.mean(d * d, axis=-1, keepdims=True)
    rstd = lax.rsqrt(var + LN_EPS)
    return d * rstd, rstd


def _ln_bwd(dxh, xhat, rstd):
    m1 = jnp.mean(dxh, axis=-1, keepdims=True)
    m2 = jnp.mean(dxh * xhat, axis=-1, keepdims=True)
    return rstd * (dxh - m1 - xhat * m2)


def _colsum(a):
    return jnp.sum(a, axis=0, keepdims=True)


def _window_sums(ext, tm, causal):
    n = ext.shape[0]
    lo = HALO if causal else 0
    s, out = ext, []
    for p in range(len(POOL_WINDOWS)):
        assert POOL_WINDOWS[p] == 2 ** (p + 1)
        k = 2**p
        s = s + pltpu.roll(s, k if causal else n - k, 0)
        out.append(s[lo : lo + tm, 0:128])
        if p + 1 < len(POOL_WINDOWS):
            s = s[:, 128:]
    return out


def _pool_features(vp, vp_s, row, tm):
    sums = _window_sums(vp_s[...], tm, causal=True)
    feats, inv_cnts = [], []
    for g, win in enumerate(POOL_WINDOWS):
        inv_cnt = 1.0 / jnp.minimum(row + 1, win).astype(F32)
        feats.append(sums[g] * inv_cnt - vp[:, 128 * g : 128 * g + 128])
        inv_cnts.append(inv_cnt)
    return feats, inv_cnts


def _f1(x, mod, w_in, conv_w, w_pool, pool_scale, w_out, tm, gather, by_cols):
    T, D = x.shape
    ZW = w_in.shape[1]
    CC = ZW // 4
    nt = T // tm
    ng = len(gather)
    fwd_steps = [max(nt - 3 + j, 0) for j in range(3)]

    def body(*refs):
        x_ref, mod_ref, win_ref, cw_ref, wp_ref, ps_ref, wout_ref = refs[:7]
        g_ins = refs[7 : 7 + ng]
        z_ref, h_ref, xhat_ref, rstd_ref, mix_ref = refs[7 + ng : 12 + ng]
        g_outs = refs[12 + ng : 12 + 2 * ng]
        cv_s, vp_s = refs[12 + 2 * ng : 14 + 2 * ng]
        g_sems = refs[14 + 2 * ng :]
        i = pl.program_id(0)

        @pl.when(i == 0)
        def _():
            _gather_start(g_ins, g_outs, g_sems)
            cv_s[0:HALO, :] = jnp.zeros((HALO, CC), F32)
            vp_s[0:HALO, :] = jnp.zeros((HALO, CC), F32)

        xv = x_ref[...]
        sh1, sc1, g1 = mod_ref[0:1, :], mod_ref[1:2, :], mod_ref[2:3, :]
        h = (xv * (1.0 + sc1) + sh1).astype(BF16)
        h_ref[...] = h
        z = jnp.dot(h, win_ref[...], preferred_element_type=F32)
        z_ref[...] = z.astype(BF16)
        gb, gc, vc, vp = z[:, 0:CC], z[:, CC : 2 * CC], z[:, 2 * CC : 3 * CC], z[:, 3 * CC : 4 * CC]
        cv = gc * vc
        cv_s[HALO : HALO + tm, :] = cv
        vp_s[HALO : HALO + tm, :] = vp
        conv = cw_ref[0:1, :] * cv_s[HALO - 2 : HALO - 2 + tm, :] + cw_ref[1:2, :] * cv_s[HALO - 1 : HALO - 1 + tm, :] + cw_ref[2:3, :] * cv
        parts = [gb * conv]
        row = i * tm + lax.broadcasted_iota(jnp.int32, (tm, 1), 0)
        feats, _ = _pool_features(vp, vp_s, row, tm)
        for g in range(len(POOL_WINDOWS)):
            pw = jnp.dot(feats[g].astype(BF16), wp_ref[g], preferred_element_type=F32)
            parts.append(pw * ps_ref[0:1, 128 * g : 128 * g + 128])
        cv_s[0:HALO, :] = cv_s[tm : tm + HALO, :]
        vp_s[0:HALO, :] = vp_s[tm : tm + HALO, :]
        ycat = jnp.concatenate(parts, axis=1).astype(BF16)
        mix = jnp.dot(ycat, wout_ref[...], preferred_element_type=F32)
        mix_ref[...] = mix
        xhat, rstd = _ln_fwd(DEEPNORM_ALPHA * xv + (1.0 + g1) * mix)
        xhat_ref[...] = xhat
        rstd_ref[...] = rstd

        for j in range(3):

            @pl.when(i == fwd_steps[j])
            def _(j=j):
                _gather_forward(g_ins, g_outs, g_sems, j)

        @pl.when(i == nt - 1)
        def _():
            _gather_finish(g_ins, g_outs, g_sems)

    tile = lambda w: pl.BlockSpec((tm, w), lambda i: (i, 0))
    return pl.pallas_call(
        body,
        name="f1",
        grid=(nt,),
        out_shape=[
            jax.ShapeDtypeStruct((T, ZW), BF16),
            jax.ShapeDtypeStruct((T, D), BF16),
            jax.ShapeDtypeStruct((T, D), F32),
            jax.ShapeDtypeStruct((T, 1), F32),
            jax.ShapeDtypeStruct((T, D), F32),
        ]
        + _gather_out_shape(gather, by_cols),
        in_specs=[tile(D)] + [VMEM_SPEC] * 6 + [ANY_SPEC] * ng,
        out_specs=[tile(ZW), tile(D), tile(D), tile(1), tile(D)] + [ANY_SPEC] * ng,
        scratch_shapes=[pltpu.VMEM((HALO + tm, CC), F32), pltpu.VMEM((HALO + tm, CC), F32)] + _gather_scratch(ng),
        compiler_params=pltpu.CompilerParams(dimension_semantics=("arbitrary",), vmem_limit_bytes=VMEM_LIMIT),
    )(x, mod, w_in, conv_w, w_pool, pool_scale, w_out, *gather)


def _fb2(xhat1, target, mod, ln, w_mi, w_mo, tm):
    T, D = xhat1.shape
    H = w_mi.shape[1]
    hc = min(1024, H)
    nb = H // hc
    nt = T // tm

    def body(xh1_ref, t_ref, mod_ref, ln_ref, wmi_ref, wmo_ref, dx1_ref, h2_ref, a_ref, du_ref, df_ref, acc_ref):
        i = pl.program_id(0)

        @pl.when(i == 0)
        def _():
            acc_ref[...] = jnp.zeros((8, D), F32)

        sh2, sc2, g2 = mod_ref[3:4, :], mod_ref[4:5, :], mod_ref[5:6, :]
        x1 = xh1_ref[...] * ln_ref[0:1, :] + ln_ref[1:2, :]
        h2 = (x1 * (1.0 + sc2) + sh2).astype(BF16)
        h2_ref[...] = h2
        f = jnp.zeros((tm, D), F32)
        for k in range(nb):
            ks = slice(k * hc, (k + 1) * hc)
            r = jnp.maximum(jnp.dot(h2, wmi_ref[:, ks], preferred_element_type=F32), 0.0)
            du_ref[:, ks] = r.astype(BF16)
            a = (r * r).astype(BF16)
            a_ref[:, ks] = a
            f = f + jnp.dot(a, wmo_ref[ks, :], preferred_element_type=F32)
        xhat2, rstd2 = _ln_fwd(DEEPNORM_ALPHA * x1 + (1.0 + g2) * f)
        ln2_g = ln_ref[2:3, :]
        d = xhat2 * ln2_g + ln_ref[3:4, :] - t_ref[...]
        dr2 = _ln_bwd(d * (ln2_g * (1.0 / D)), xhat2, rstd2)
        df = ((1.0 + g2) * dr2).astype(BF16)
        df_ref[...] = df
        dh2 = jnp.zeros((tm, D), F32)
        for k in range(nb):
            ks = slice(k * hc, (k + 1) * hc)
            da = lax.dot_general(df, wmo_ref[ks, :], NT, preferred_element_type=F32)
            du = (da * (2.0 * du_ref[:, ks].astype(F32))).astype(BF16)
            du_ref[:, ks] = du
            dh2 = dh2 + lax.dot_general(du, wmi_ref[:, ks], NT, preferred_element_type=F32)
        dx1_ref[...] = DEEPNORM_ALPHA * dr2 + dh2 * (1.0 + sc2)
        acc_ref[0:1, :] += _colsum(d * xhat2) * (1.0 / D)
        acc_ref[1:2, :] += _colsum(d) * (1.0 / D)
        acc_ref[2:3, :] += _colsum(dh2)
        acc_ref[3:4, :] += _colsum(dh2 * x1)
        acc_ref[4:5, :] += _colsum(dr2 * f)
        acc_ref[5:6, :] += jnp.zeros((1, D), F32) + (0.5 / D) * jnp.sum(d * d)

    tile = lambda w: pl.BlockSpec((tm, w), lambda i: (i, 0))
    return pl.pallas_call(
        body,
        name="fb2",
        grid=(nt,),
        out_shape=[
            jax.ShapeDtypeStruct((T, D), F32),
            jax.ShapeDtypeStruct((T, D), BF16),
            jax.ShapeDtypeStruct((T, H), BF16),
            jax.ShapeDtypeStruct((T, H), BF16),
            jax.ShapeDtypeStruct((T, D), BF16),
            jax.ShapeDtypeStruct((8, D), F32),
        ],
        in_specs=[tile(D), tile(D)] + [VMEM_SPEC] * 4,
        out_specs=[tile(D), tile(D), tile(H), tile(H), tile(D), pl.BlockSpec((8, D), lambda i: (0, 0))],
        compiler_params=pltpu.CompilerParams(dimension_semantics=("arbitrary",), vmem_limit_bytes=VMEM_LIMIT),
    )(xhat1, target, mod, ln, w_mi, w_mo)


def _b1(dx1, xhat1, rstd1, x, mix, z, mod, ln, w_out, w_in, conv_w, w_pool, pool_scale, tm, scatter):
    T, D = x.shape
    ZW = w_in.shape[1]
    CC = ZW // 4
    nt = T // tm
    hb = tm // HALO

    def body(dx1_ref, xh1_ref, rstd_ref, x_ref, mix_ref, z_ref, zh_ref, mod_ref, ln_ref, wout_ref, win_ref, cw_ref, wp_ref, ps_ref, s_hbm,
             dx_ref, dmix_ref, ycat_ref, dz_ref, acc_ref, gcw_ref, gwp_ref, s_recv, cv_s, vp_s, e_s, q_s, *s_sems):
        i = pl.program_id(0)
        j = nt - 1 - i

        @pl.when(i == 0)
        def _():
            _scatter_start(s_hbm, s_recv, s_sems)
            acc_ref[...] = jnp.zeros((8, D), F32)
            gcw_ref[...] = jnp.zeros((8, CC), F32)
            gwp_ref[...] = jnp.zeros(gwp_ref.shape, F32)
            e_s[tm : tm + HALO, :] = jnp.zeros((HALO, CC), F32)
            q_s[tm : tm + HALO, :] = jnp.zeros((HALO, CC), F32)

        sh1, sc1, g1 = mod_ref[0:1, :], mod_ref[1:2, :], mod_ref[2:3, :]
        dx1 = dx1_ref[...]
        xhat1 = xh1_ref[...]
        acc_ref[0:1, :] += _colsum(dx1 * xhat1)
        acc_ref[1:2, :] += _colsum(dx1)
        dr1 = _ln_bwd(dx1 * ln_ref[0:1, :], xhat1, rstd_ref[...])
        acc_ref[4:5, :] += _colsum(dr1 * mix_ref[...])
        dmix = ((1.0 + g1) * dr1).astype(BF16)
        dmix_ref[...] = dmix
        dycat = lax.dot_general(dmix, wout_ref[...], NT, preferred_element_type=F32)

        z = z_ref[...].astype(F32)
        zh = zh_ref[...].astype(F32) * jnp.where(j > 0, 1.0, 0.0)
        gb, gc, vc, vp = z[:, 0:CC], z[:, CC : 2 * CC], z[:, 2 * CC : 3 * CC], z[:, 3 * CC : 4 * CC]
        cv = gc * vc
        cv_s[0:HALO, :] = zh[:, CC : 2 * CC] * zh[:, 2 * CC : 3 * CC]
        cv_s[HALO : HALO + tm, :] = cv
        vp_s[0:HALO, :] = zh[:, 3 * CC : 4 * CC]
        vp_s[HALO : HALO + tm, :] = vp
        cv_m2 = cv_s[HALO - 2 : HALO - 2 + tm, :]
        cv_m1 = cv_s[HALO - 1 : HALO - 1 + tm, :]
        w0, w1, w2 = cw_ref[0:1, :], cw_ref[1:2, :], cw_ref[2:3, :]
        conv = w0 * cv_m2 + w1 * cv_m1 + w2 * cv
        dyc = dycat[:, 0:CC]
        e = dyc * gb
        e_s[0:tm, :] = e
        dcv = w2 * e + w1 * e_s[1 : 1 + tm, :] + w0 * e_s[2 : 2 + tm, :]
        gcw_ref[0:1, :] += _colsum(e * cv_m2)
        gcw_ref[1:2, :] += _colsum(e * cv_m1)
        gcw_ref[2:3, :] += _colsum(e * cv)
        y_parts = [gb * conv]
        dz_parts = [dyc * conv, dcv * vc, dcv * gc]

        row = j * tm + lax.broadcasted_iota(jnp.int32, (tm, 1), 0)
        feats, inv_cnts = _pool_features(vp, vp_s, row, tm)
        gps_parts, dps = [], []
        for g in range(len(POOL_WINDOWS)):
            cols = slice(128 * g, 128 * g + 128)
            p = feats[g].astype(BF16)
            scale = ps_ref[0:1, cols]
            pw = jnp.dot(p, wp_ref[g], preferred_element_type=F32)
            y_parts.append(pw * scale)
            dyp = dycat[:, CC + 128 * g : CC + 128 * g + 128]
            gps_parts.append(_colsum(dyp * pw))
            dpw = (dyp * scale).astype(BF16)
            gwp_ref[g] += lax.dot_general(p, dpw, TN, preferred_element_type=F32)
            dp = lax.dot_general(dpw, wp_ref[g], NT, preferred_element_type=F32)
            q_s[0:tm, cols] = dp * inv_cnts[g]
            dps.append(dp)
        sq = _window_sums(q_s[...], tm, causal=False)
        dz_parts += [sq[g] - dps[g] for g in range(len(POOL_WINDOWS))]
        gcw_ref[3:4, :] += jnp.concatenate(gps_parts, axis=1)
        ycat_ref[...] = jnp.concatenate(y_parts, axis=1).astype(BF16)
        dz = jnp.concatenate(dz_parts, axis=1).astype(BF16)
        dz_ref[...] = dz
        dh = lax.dot_general(dz, win_ref[...], NT, preferred_element_type=F32)
        acc_ref[2:3, :] += _colsum(dh)
        acc_ref[3:4, :] += _colsum(dh * x_ref[...])
        dx_ref[...] = DEEPNORM_ALPHA * dr1 + dh * (1.0 + sc1)
        e_s[tm : tm + HALO, :] = e_s[0:HALO, :]
        q_s[tm : tm + HALO, :] = q_s[0:HALO, :]

        @pl.when(i == nt - 1)
        def _():
            _scatter_finish(s_hbm, s_recv, s_sems)

    tile = lambda w: pl.BlockSpec((tm, w), lambda i: (nt - 1 - i, 0))
    halo = pl.BlockSpec((HALO, ZW), lambda i: (jnp.maximum((nt - 1 - i) * hb - 1, 0), 0))
    fixed = lambda shape: pl.BlockSpec(shape, lambda i: (0,) * len(shape))
    return pl.pallas_call(
        body,
        name="b1",
        grid=(nt,),
        out_shape=[
            jax.ShapeDtypeStruct((T, D), F32),
            jax.ShapeDtypeStruct((T, D), BF16),
            jax.ShapeDtypeStruct((T, D), BF16),
            jax.ShapeDtypeStruct((T, ZW), BF16),
            jax.ShapeDtypeStruct((8, D), F32),
            jax.ShapeDtypeStruct((8, CC), F32),
            jax.ShapeDtypeStruct(w_pool.shape, F32),
            jax.ShapeDtypeStruct(scatter.shape, scatter.dtype),
        ],
        in_specs=[tile(D), tile(D), tile(1), tile(D), tile(D), tile(ZW), halo] + [VMEM_SPEC] * 7 + [ANY_SPEC],
        out_specs=[tile(D), tile(D), tile(D), tile(ZW), fixed((8, D)), fixed((8, CC)), fixed(w_pool.shape), ANY_SPEC],
        scratch_shapes=[
            pltpu.VMEM((HALO + tm, CC), F32),
            pltpu.VMEM((HALO + tm, CC), F32),
            pltpu.VMEM((tm + HALO, CC), F32),
            pltpu.VMEM((tm + HALO, CC), F32),
        ]
        + _scatter_scratch(),
        compiler_params=pltpu.CompilerParams(dimension_semantics=("arbitrary",), vmem_limit_bytes=VMEM_LIMIT),
    )(dx1, xhat1, rstd1, x, mix, z, z, mod, ln, w_out, w_in, conv_w, w_pool, pool_scale, scatter)


def _wgrad(a, b, bk, n_groups, bt, name, owners=None, scatter=None):
    T, K = a.shape
    N = b.shape[1]
    nk, nt, ng = K // bk, T // bt, N // n_groups
    nc = min(512, ng)
    ns = 0 if scatter is None else 1

    def body(*refs):
        a_ref, b_ref = refs[0], refs[1]
        o_ref = refs[2 + ns]
        acc = refs[3 + 2 * ns]
        if ns:
            s_hbm, s_recv, s_sems = refs[2], refs[4], refs[6:9]
        kk, gg, t = pl.program_id(0), pl.program_id(1), pl.program_id(2)

        if ns:

            @pl.when((kk == 0) & (gg == 0) & (t == 0))
            def _():
                _scatter_start(s_hbm, s_recv, s_sems)

        @pl.when(t == 0)
        def _():
            acc[...] = jnp.zeros(acc.shape, F32)

        at = a_ref[...].T
        for c in range(ng // nc):
            cs = slice(c * nc, (c + 1) * nc)
            acc[:, cs] += jnp.dot(at, b_ref[:, cs], preferred_element_type=F32)

        @pl.when(t == nt - 1)
        def _():
            if owners is None:
                o_ref[...] = acc[...].astype(BF16)
            else:
                per = N // owners
                for o in range(ng // per):
                    o_ref[o] = acc[:, o * per : (o + 1) * per].astype(BF16)

        if ns:

            @pl.when((kk == nk - 1) & (gg == n_groups - 1) & (t == nt - 1))
            def _():
                _scatter_finish(s_hbm, s_recv, s_sems)

    if owners is None:
        out_shape = [jax.ShapeDtypeStruct((K, N), BF16)]
        out_specs = [pl.BlockSpec((bk, ng), lambda k, g, t: (k, g))]
    else:
        assert bk == K
        per = N // owners
        out_shape = [jax.ShapeDtypeStruct((owners, K, per), BF16)]
        out_specs = [pl.BlockSpec((ng // per, K, per), lambda k, g, t: (g, 0, 0))]
    ins, in_specs = [a, b], [pl.BlockSpec((bt, bk), lambda k, g, t: (t, k)), pl.BlockSpec((bt, ng), lambda k, g, t: (t, g))]
    scratch = [pltpu.VMEM((bk, ng), F32)]
    if ns:
        ins.append(scatter)
        in_specs.append(ANY_SPEC)
        out_shape.append(jax.ShapeDtypeStruct(scatter.shape, scatter.dtype))
        out_specs.append(ANY_SPEC)
        scratch += _scatter_scratch()
    outs = pl.pallas_call(
        body,
        name=name,
        grid=(nk, n_groups, nt),
        out_shape=out_shape,
        in_specs=in_specs,
        out_specs=out_specs,
        scratch_shapes=scratch,
        compiler_params=pltpu.CompilerParams(dimension_semantics=("arbitrary", "arbitrary", "arbitrary"), vmem_limit_bytes=VMEM_LIMIT),
    )(*ins)
    return outs if ns else outs[0]


def _small_grads(acc1_t, acc2_t, gcw_t, gwp_all, cond_t, my_slot, w_cols):
    D = acc1_t.shape[2]
    n_chunk = D // 128
    q_mine = w_cols // 128

    def total(ref, r):
        s = ref[r, 0:1, :]
        for k in range(1, N_DEV):
            s = s + ref[r, k : k + 1, :]
        return s

    def body(slot_ref, a1_ref, a2_ref, gcw_ref, gwp_ref, ct_ref, gb_ref, gw_ref, gln_ref, gcwo_ref, gwpo_ref, loss_ref, dm_s):
        loss_ref[...] = total(a2_ref, 5)
        for s, (ref, r) in enumerate([(a1_ref, 2), (a1_ref, 3), (a1_ref, 4), (a2_ref, 2), (a2_ref, 3), (a2_ref, 4)]):
            gb_ref[0:1, s * D : (s + 1) * D] = total(ref, r)
            for qq in range(n_chunk):
                dm_s[s * n_chunk + qq] = ref[r, :, 128 * qq : 128 * qq + 128]
        gln_ref[0:1, :] = total(a1_ref, 0)
        gln_ref[1:2, :] = total(a1_ref, 1)
        gln_ref[2:3, :] = total(a2_ref, 0)
        gln_ref[3:4, :] = total(a2_ref, 1)
        gcwo_ref[...] = jnp.zeros(gcwo_ref.shape, F32)
        for r in range(4):
            gcwo_ref[r : r + 1, :] = total(gcw_ref, r)
        wp = gwp_ref[0]
        for k in range(1, N_DEV):
            wp = wp + gwp_ref[k]
        gwpo_ref[0] = wp
        ct = ct_ref[...]
        cond_t = ct * jax.nn.sigmoid(ct)
        q0 = slot_ref[0] * q_mine
        for q in range(q_mine):
            dm = dm_s[q0 + q]
            out = cond_t[:, 0:1] * dm[0:1, :]
            for k in range(1, N_DEV):
                out = out + cond_t[:, k : k + 1] * dm[k : k + 1, :]
            gw_ref[0, :, 128 * q : 128 * q + 128] = out

    CC = gcw_t.shape[2]
    return pl.pallas_call(
        body,
        name="small_grads",
        out_shape=[
            jax.ShapeDtypeStruct((1, 6 * D), F32),
            jax.ShapeDtypeStruct((1, D, w_cols), F32),
            jax.ShapeDtypeStruct((4, D), F32),
            jax.ShapeDtypeStruct((8, CC), F32),
            jax.ShapeDtypeStruct((1, *gwp_all.shape[1:]), F32),
            jax.ShapeDtypeStruct((1, D), F32),
        ],
        in_specs=[pl.BlockSpec(memory_space=pltpu.SMEM)] + [VMEM_SPEC] * 5,
        out_specs=[VMEM_SPEC] * 6,
        scratch_shapes=[pltpu.VMEM((6 * n_chunk, N_DEV, 128), F32)],
        compiler_params=pltpu.CompilerParams(vmem_limit_bytes=VMEM_LIMIT),
    )(my_slot, acc1_t, acc2_t, gcw_t, gwp_all, cond_t)


def kernel(x, c, w_ada, b_ada, w_in, conv_w, w_pool, pool_scale, w_out, ln1_g, ln1_b, w_mlp_in, w_mlp_out, ln2_g, ln2_b, loss_target, m_w_ada, m_b_ada, m_w_in, m_conv_w, m_w_pool, m_pool_scale, m_w_out, m_ln1_g, m_ln1_b, m_w_mlp_in, m_w_mlp_out, m_ln2_g, m_ln2_b, v_w_ada, v_b_ada, v_w_in, v_conv_w, v_w_pool, v_pool_scale, v_w_out, v_ln1_g, v_ln1_b, v_w_mlp_in, v_w_mlp_out, v_ln2_g, v_ln2_b):
    T, D = x.shape[1], x.shape[2]
    H = w_mlp_out.shape[1] * N_DEV
    ZW = w_in.shape[2] * N_DEV
    CC = ZW // 4
    tm = min(512, T // 2)
    bt = min(1024, T)
    ax, ay, ac = _my_place()
    me = _slot(ax, ay, ac)

    w_cols = w_ada.shape[2]
    b_mine = lax.dynamic_slice(b_ada, (0, me * w_cols), (1, w_cols))
    w_in_f, w_out_g, cw_g, c_g, mod_g = _prologue(w_in[0].astype(BF16), w_out[0].astype(BF16), conv_w[0], c, w_ada, b_mine)
    w_out_f = w_out_g.reshape(D, D)
    conv_w_f = jnp.transpose(cw_g, (1, 0, 2)).reshape(conv_w.shape[1], CC)
    c_all = c_g.reshape(N_DEV, D)
    mod = lax.dynamic_index_in_dim(mod_g, me, axis=1, keepdims=False).reshape(6, D)

    ln = jnp.concatenate([ln1_g, ln1_b, ln2_g, ln2_b], axis=0)
    w_pool16 = w_pool[0].astype(BF16)
    xs, target = x[0], loss_target[0]

    z, h, xhat1, rstd1, mix, w_mi_f, w_mo_g = _f1(
        xs, mod, w_in_f, conv_w_f, w_pool16, pool_scale, w_out_f, tm,
        [w_mlp_in[0].astype(BF16), w_mlp_out[0].astype(BF16)], [True, False])
    dx1, h2, a, du, df, acc2 = _fb2(xhat1, target, mod, ln, w_mi_f, w_mo_g.reshape(H, D), tm // 2)

    gp_mo = _wgrad(a, df, D, 1, bt, "wgrad_mlp_out").reshape(N_DEV, H // N_DEV, D)
    gp_mi, rv_mo = _wgrad(h2, du, D, 2, bt, "wgrad_mlp_in", owners=N_DEV, scatter=gp_mo)
    grad_x, dmix, ycat, dz, acc1, gcw, gwp, rv_mi = _b1(
        dx1, xhat1, rstd1, xs, mix, z, mod, ln, w_out_f, w_in_f, conv_w_f, w_pool16, pool_scale, tm, gp_mi)
    gp_out = _wgrad(ycat, dmix, D, 1, bt, "wgrad_out").reshape(N_DEV, D // N_DEV, D)
    gp_in, rv_out = _wgrad(h, dz, D, 1, bt, "wgrad_in", owners=N_DEV, scatter=gp_out)

    acc1_g, acc2_g, gcw_g, gwp_g, rv_in = _allgather([acc1, acc2, gcw, gwp], "gather_small", in_vmem=True, scatter=gp_in)
    g_b_ada, g_w_ada, g_ln, g_cw, g_w_pool, loss_row = _small_grads(
        jnp.transpose(acc1_g, (1, 0, 2)), jnp.transpose(acc2_g, (1, 0, 2)), jnp.transpose(gcw_g, (1, 0, 2)), gwp_g,
        c_all.T, jnp.reshape(me, (1,)).astype(jnp.int32), w_cols)
    cc_mine = conv_w.shape[2]
    g_conv_w = lax.dynamic_slice(g_cw, (0, me * cc_mine), (conv_w.shape[1], cc_mine))[None]
    g_pool_scale = g_cw[3:4, :]
    g_ln1_g, g_ln1_b, g_ln2_g, g_ln2_b = g_ln[0:1], g_ln[1:2], g_ln[2:3], g_ln[3:4]

    small = _adamw_multi(
        [
            (b_ada, g_b_ada, m_b_ada, v_b_ada),
            (conv_w, g_conv_w, m_conv_w, v_conv_w),
            (w_pool, g_w_pool, m_w_pool, v_w_pool),
            (pool_scale, g_pool_scale, m_pool_scale, v_pool_scale),
            (ln1_g, g_ln1_g, m_ln1_g, v_ln1_g),
            (ln1_b, g_ln1_b, m_ln1_b, v_ln1_b),
            (ln2_g, g_ln2_g, m_ln2_g, v_ln2_g),
            (ln2_b, g_ln2_b, m_ln2_b, v_ln2_b),
        ],
        "adamw_small")
    (u_w_ada,) = _adamw_multi([(w_ada, g_w_ada, m_w_ada, v_w_ada)], "adamw_w_ada")
    u_b_ada, u_conv_w, u_w_pool, u_pool_scale, u_ln1_g, u_ln1_b, u_ln2_g, u_ln2_b = small

    g_w_mo, *u_w_mo = _sum_adamw(rv_mo, w_mlp_out, m_w_mlp_out, v_w_mlp_out, "sum_w_mlp_out")
    g_w_mi, *u_w_mi = _sum_adamw(rv_mi, w_mlp_in, m_w_mlp_in, v_w_mlp_in, "sum_w_mlp_in")
    g_w_in, *u_w_in = _sum_adamw(rv_in, w_in, m_w_in, v_w_in, "sum_w_in")
    g_w_out, *u_w_out = _sum_adamw(rv_out, w_out, m_w_out, v_w_out, "sum_w_out")

    grads = [g_w_ada, g_b_ada, g_w_in, g_conv_w, g_w_pool, g_pool_scale, g_w_out, g_ln1_g, g_ln1_b, g_w_mi, g_w_mo, g_ln2_g, g_ln2_b]
    updates = [u_w_ada, u_b_ada, u_w_in, u_conv_w, u_w_pool, u_pool_scale, u_w_out, u_ln1_g, u_ln1_b, u_w_mi, u_w_mo, u_ln2_g, u_ln2_b]
    deltas = [u[0] for u in updates]
    new_m = [u[1] for u in updates]
    new_v = [u[2] for u in updates]
    return (loss_row[0, 0], grad_x[None], *grads, *deltas, *new_m, *new_v)
```

```python
import jax
import jax.numpy as jnp
from jax import lax
from jax.experimental import pallas as pl
from jax.experimental.pallas import tpu as pltpu

F32 = jnp.float32
BF16 = jnp.bfloat16
MESH = pl.DeviceIdType.MESH
N_DEV = 8

LN_EPS = 1e-5
DEPTH = 1
DEEPNORM_ALPHA = (2.0 * DEPTH) ** 0.25
POOL_WINDOWS = (2, 4, 8, 16)
HALO = 16

ADAM_LR = 0.001
ADAM_B1 = 0.9
ADAM_B2 = 0.999
ADAM_EPS = 1e-08
ADAM_WD = 0.01
ADAM_STEP = 10

VMEM_LIMIT = 56 * 1024 * 1024

VMEM_SPEC = pl.BlockSpec(memory_space=pltpu.VMEM)
ANY_SPEC = pl.BlockSpec(memory_space=pl.ANY)

NT = (((1,), (1,)), ((), ()))
TN = (((0,), (0,)), ((), ()))


def _my_place():
    return lax.axis_index("x"), lax.axis_index("y"), lax.axis_index("c")


def _slot(x, y, c):
    return 4 * x + 2 * y + c


def _gather_place(ins, outs, a, slot):
    if len(outs[a].shape) == len(ins[a].shape):
        wb = ins[a].shape[1]
        return outs[a].at[:, pl.ds(pl.multiple_of(slot * wb, wb), wb)]
    return outs[a].at[slot]


def _gather_copy(ins, outs, sems, a, k, block, to, from_shard=False):
    send_sems, recv_sems, _ = sems
    dst = _gather_place(ins, outs, a, _slot(*block))
    return pltpu.make_async_remote_copy(
        src_ref=ins[a] if from_shard else dst,
        dst_ref=dst,
        send_sem=send_sems.at[7 * a + k],
        recv_sem=recv_sems.at[7 * a + k],
        device_id=to,
        device_id_type=MESH,
    )


def _gather_peers():
    x, y, c = _my_place()
    return (x, y, c), (x, y, 1 - c), [(1 - x, y), (x, 1 - y), (1 - x, 1 - y)]


def _gather_first(ins, outs, sems):
    me, sibling, chips = _gather_peers()
    first = []
    for a in range(len(ins)):
        first.append(_gather_copy(ins, outs, sems, a, 0, me, sibling, from_shard=True))
        first += [_gather_copy(ins, outs, sems, a, 1 + j, me, (*chip, me[2]), from_shard=True) for j, chip in enumerate(chips)]
    return first


def _gather_mine(ins, outs, sems, a):
    me, _, _ = _gather_peers()
    return pltpu.make_async_copy(ins[a], _gather_place(ins, outs, a, _slot(*me)), sems[2].at[a])


def _gather_start(ins, outs, sems):
    for a in range(len(ins)):
        _gather_mine(ins, outs, sems, a).start()
    for cp in _gather_first(ins, outs, sems):
        cp.start()


def _gather_forward(ins, outs, sems, j):
    me, sibling, chips = _gather_peers()
    for a in range(len(ins)):
        _gather_copy(ins, outs, sems, a, 1 + j, (*chips[j], me[2]), me).wait_recv()
        _gather_copy(ins, outs, sems, a, 4 + j, (*chips[j], me[2]), sibling).start()


def _gather_finish(ins, outs, sems):
    me, sibling, chips = _gather_peers()
    for a in range(len(ins)):
        _gather_copy(ins, outs, sems, a, 0, sibling, me).wait_recv()
        for j, chip in enumerate(chips):
            _gather_copy(ins, outs, sems, a, 4 + j, (*chip, 1 - me[2]), me).wait_recv()
    for cp in _gather_first(ins, outs, sems):
        cp.wait_send()
    for a in range(len(ins)):
        for j, chip in enumerate(chips):
            _gather_copy(ins, outs, sems, a, 4 + j, (*chip, me[2]), sibling).wait_send()
        _gather_mine(ins, outs, sems, a).wait()


def _gather_scratch(n):
    return [pltpu.SemaphoreType.DMA((7 * n,)), pltpu.SemaphoreType.DMA((7 * n,)), pltpu.SemaphoreType.DMA((n,))]


def _gather_out_shape(shards, by_cols):
    return [
        jax.ShapeDtypeStruct((s.shape[0], N_DEV * s.shape[1]) if cols else (N_DEV, *s.shape), s.dtype)
        for s, cols in zip(shards, by_cols)
    ]


N_CHIP = 4


def _scatter_scratch(rows, cols):
    block = pltpu.VMEM((N_CHIP, rows, cols), BF16)
    dma = pltpu.SemaphoreType.DMA
    return [block, block, block, dma((N_CHIP,)), dma((N_CHIP,)), dma((N_CHIP,)), dma((N_CHIP - 1,)), dma((N_CHIP - 1,)), dma]


def _scatter_pair_copies(g_hbm, scr):
    x, y, c = _my_place()
    mine, theirs, _, a_send, a_recv, load_sem = scr[:6]
    to_sibling = [
        pltpu.make_async_remote_copy(
            src_ref=g_hbm.at[2 * q + (1 - c)], dst_ref=theirs.at[q], send_sem=a_send.at[q], recv_sem=a_recv.at[q],
            device_id=(x, y, 1 - c), device_id_type=MESH)
        for q in range(N_CHIP)
    ]
    loads = [pltpu.make_async_copy(g_hbm.at[2 * q + c], mine.at[q], load_sem.at[q]) for q in range(N_CHIP)]
    return to_sibling, loads


def _scatter_sum_copies(recv, scr):
    x, y, c = _my_place()
    sums, b_send, b_recv, own_sem = scr[2], scr[6], scr[7], scr[8]
    q_me = 2 * x + y
    to_owner = [
        pltpu.make_async_remote_copy(
            src_ref=sums.at[2 * px + py], dst_ref=recv.at[q_me], send_sem=b_send.at[j], recv_sem=b_recv.at[j],
            device_id=(px, py, c), device_id_type=MESH)
        for j, (px, py) in enumerate([(1 - x, y), (x, 1 - y), (1 - x, 1 - y)])
    ]
    return to_owner, pltpu.make_async_copy(sums.at[q_me], recv.at[q_me], own_sem)


def _scatter_start(g_hbm, scr):
    to_sibling, loads = _scatter_pair_copies(g_hbm, scr)
    for cp in to_sibling + loads:
        cp.start()


def _scatter_middle(g_hbm, recv, scr):
    to_sibling, loads = _scatter_pair_copies(g_hbm, scr)
    for cp in to_sibling:
        cp.wait_recv()
    for cp in loads:
        cp.wait()
    mine, theirs, sums = scr[:3]

    def step(r, carry):
        rs = pl.ds(pl.multiple_of(r * ROW_CHUNK, ROW_CHUNK), ROW_CHUNK)
        for q in range(N_CHIP):
            sums[q, rs, :] = (mine[q, rs, :].astype(F32) + theirs[q, rs, :].astype(F32)).astype(BF16)
        return carry

    lax.fori_loop(0, mine.shape[1] // ROW_CHUNK, step, 0)
    to_owner, own = _scatter_sum_copies(recv, scr)
    for cp in to_owner + [own]:
        cp.start()


def _scatter_finish(g_hbm, recv, scr):
    to_sibling, _ = _scatter_pair_copies(g_hbm, scr)
    to_owner, own = _scatter_sum_copies(recv, scr)
    for cp in to_owner:
        cp.wait_recv()
    for cp in to_sibling + to_owner:
        cp.wait_send()
    own.wait()


def _scatter_out_shape(gparts):
    return jax.ShapeDtypeStruct((N_CHIP, *gparts.shape[1:]), gparts.dtype)


def _adamw_math(w, g, m, v):
    m = ADAM_B1 * m + (1.0 - ADAM_B1) * g
    v = ADAM_B2 * v + (1.0 - ADAM_B2) * (g * g)
    m_hat = m / (1.0 - ADAM_B1**ADAM_STEP)
    v_hat = v / (1.0 - ADAM_B2**ADAM_STEP)
    delta = -ADAM_LR * (m_hat / (jnp.sqrt(v_hat) + ADAM_EPS) + ADAM_WD * w)
    return delta, m, v


ROW_CHUNK = 64


ELEMS_PER_STEP = 64 * 1024


def _allgather(shards, name, in_vmem, scatter=None, by_cols=None):
    n = len(shards)
    by_cols = by_cols or [False] * n
    ns = 0 if scatter is None else 1

    def body(*refs):
        ins, outs = refs[:n], refs[n + ns : 2 * n + ns]
        rest = refs[2 * n + ns :]
        if ns:
            g_hbm, recv, sems, scr = refs[n], rest[0], rest[1:4], rest[4:]
            _scatter_start(g_hbm, scr)
        else:
            sems = rest[0:3]
        _gather_start(ins, outs, sems)
        if ns:
            _scatter_middle(g_hbm, recv, scr)
        for j in range(3):
            _gather_forward(ins, outs, sems, j)
        _gather_finish(ins, outs, sems)
        if ns:
            _scatter_finish(g_hbm, recv, scr)

    spec = VMEM_SPEC if in_vmem else ANY_SPEC
    extra_in, extra_out, extra_spec, extra_scratch = [], [], [], []
    if ns:
        extra_in, extra_spec = [scatter], [ANY_SPEC]
        extra_out = [_scatter_out_shape(scatter)]
        extra_scratch = _scatter_scratch(*scatter.shape[1:])
    return pl.pallas_call(
        body,
        name=name,
        out_shape=_gather_out_shape(shards, by_cols) + extra_out,
        in_specs=[spec] * n + extra_spec,
        out_specs=[spec] * n + extra_spec,
        scratch_shapes=_gather_scratch(n) + extra_scratch,
        compiler_params=pltpu.CompilerParams(vmem_limit_bytes=VMEM_LIMIT),
    )(*shards, *extra_in)


def _sum_adamw(parts, w, m, v, name):
    _, rows, cols = w.shape
    rb = rows
    while rb * cols > ELEMS_PER_STEP and rb % 16 == 0:
        rb //= 2

    def body(p_ref, w_ref, m_ref, v_ref, grad_ref, delta_ref, nm_ref, nv_ref):
        g = p_ref[0].astype(F32)
        for k in range(1, p_ref.shape[0]):
            g = g + p_ref[k].astype(F32)
        delta, nm, nv = _adamw_math(w_ref[0], g, m_ref[0], v_ref[0])
        grad_ref[0] = g
        delta_ref[0] = delta
        nm_ref[0] = nm
        nv_ref[0] = nv

    block = lambda lead: pl.BlockSpec((lead, rb, cols), lambda i: (0, i, 0))
    out = jax.ShapeDtypeStruct(w.shape, F32)
    return pl.pallas_call(
        body,
        name=name,
        grid=(rows // rb,),
        out_shape=[out] * 4,
        in_specs=[block(parts.shape[0])] + [block(1)] * 3,
        out_specs=[block(1)] * 4,
        compiler_params=pltpu.CompilerParams(dimension_semantics=("arbitrary",), vmem_limit_bytes=VMEM_LIMIT),
    )(parts, w, m, v)


def _adamw_multi(items, name):
    n = len(items)

    def body(*refs):
        ins, outs = refs[: 4 * n], refs[4 * n :]
        for a in range(n):
            w_ref, g_ref, m_ref, v_ref = ins[4 * a : 4 * a + 4]
            d_ref, nm_ref, nv_ref = outs[3 * a : 3 * a + 3]
            delta, nm, nv = _adamw_math(w_ref[...], g_ref[...], m_ref[...], v_ref[...])
            d_ref[...] = delta
            nm_ref[...] = nm
            nv_ref[...] = nv

    flat = [a for it in items for a in it]
    out_shape = [jax.ShapeDtypeStruct(it[0].shape, F32) for it in items for _ in range(3)]
    outs = pl.pallas_call(
        body,
        name=name,
        out_shape=out_shape,
        in_specs=[VMEM_SPEC] * (4 * n),
        out_specs=[VMEM_SPEC] * (3 * n),
        compiler_params=pltpu.CompilerParams(vmem_limit_bytes=VMEM_LIMIT),
    )(*flat)
    return [tuple(outs[3 * a : 3 * a + 3]) for a in range(n)]


def _prologue(w_in16, w_out16, conv_w, c, w_ada, b_mine):
    D = c.shape[1]
    wc = w_ada.shape[2]

    def gather_now(ins, outs, sems):
        _gather_start(ins, outs, sems)
        for j in range(3):
            _gather_forward(ins, outs, sems, j)
        _gather_finish(ins, outs, sems)

    def body(win_ref, wout_ref, cw_ref, c_ref, wada_ref, b_ref, win_g, wout_g, cw_g, c_g, mod_g, c_s, mp_s, *sems):
        w_ins, w_outs, w_sems = (win_ref, wout_ref, cw_ref), (win_g, wout_g, cw_g), sems[0:3]
        _gather_start(w_ins, w_outs, w_sems)
        gather_now((c_ref,), (c_g,), sems[3:6])
        for k in range(N_DEV):
            c_s[k : k + 1, :] = c_g[k]
        cv = c_s[...]
        cond = cv * jax.nn.sigmoid(cv)
        mp_s[...] = jnp.dot(cond, wada_ref[0], precision=lax.Precision.HIGHEST, preferred_element_type=F32) + b_ref[...]
        gather_now((mp_s,), (mod_g,), sems[6:9])
        for j in range(3):
            _gather_forward(w_ins, w_outs, w_sems, j)
        _gather_finish(w_ins, w_outs, w_sems)

    weights = [w_in16, w_out16, conv_w]
    return pl.pallas_call(
        body,
        name="prologue",
        out_shape=_gather_out_shape(weights, [True, False, False])
        + [jax.ShapeDtypeStruct((N_DEV, 1, D), F32), jax.ShapeDtypeStruct((N_DEV, N_DEV, wc), F32)],
        in_specs=[ANY_SPEC] * 3 + [VMEM_SPEC] * 3,
        out_specs=[ANY_SPEC] * 3 + [VMEM_SPEC] * 2,
        scratch_shapes=[pltpu.VMEM((N_DEV, D), F32), pltpu.VMEM((N_DEV, wc), F32)]
        + _gather_scratch(3) + _gather_scratch(1) + _gather_scratch(1),
        compiler_params=pltpu.CompilerParams(vmem_limit_bytes=VMEM_LIMIT),
    )(w_in16, w_out16, conv_w, c, w_ada, b_mine)


def _ln_fwd(r):
    mu = jnp.mean(r, axis=-1, keepdims=True)
    d = r - mu
    var = jnp.mean(d * d, axis=-1, keepdims=True)
    rstd = lax.rsqrt(var + LN_EPS)
    return d * rstd, rstd


def _ln_bwd(dxh, xhat, rstd):
    m1 = jnp.mean(dxh, axis=-1, keepdims=True)
    m2 = jnp.mean(dxh * xhat, axis=-1, keepdims=True)
    return rstd * (dxh - m1 - xhat * m2)


def _colsum(a):
    return jnp.sum(a, axis=0, keepdims=True)


def _window_sums(ext, tm, causal):
    n = ext.shape[0]
    lo = HALO if causal else 0
    s, out = ext, []
    for p in range(len(POOL_WINDOWS)):
        assert POOL_WINDOWS[p] == 2 ** (p + 1)
        k = 2**p
        s = s + pltpu.roll(s, k if causal else n - k, 0)
        out.append(s[lo : lo + tm, 0:128])
        if p + 1 < len(POOL_WINDOWS):
            s = s[:, 128:]
    return out


def _pool_features(vp, vp_s, row, tm):
    sums = _window_sums(vp_s[...], tm, causal=True)
    feats, inv_cnts = [], []
    for g, win in enumerate(POOL_WINDOWS):
        inv_cnt = 1.0 / jnp.minimum(row + 1, win).astype(F32)
        feats.append(sums[g] * inv_cnt - vp[:, 128 * g : 128 * g + 128])
        inv_cnts.append(inv_cnt)
    return feats, inv_cnts


def _f1(x, mod, w_in, conv_w, w_pool, pool_scale, w_out, tm, gather, by_cols):
    T, D = x.shape
    ZW = w_in.shape[1]
    CC = ZW // 4
    nt = T // tm
    ng = len(gather)
    fwd_steps = [max(nt - 3 + j, 0) for j in range(3)]

    def body(*refs):
        x_ref, mod_ref, win_ref, cw_ref, wp_ref, ps_ref, wout_ref = refs[:7]
        g_ins = refs[7 : 7 + ng]
        z_ref, h_ref, xhat_ref, rstd_ref, mix_ref = refs[7 + ng : 12 + ng]
        g_outs = refs[12 + ng : 12 + 2 * ng]
        cv_s, vp_s = refs[12 + 2 * ng : 14 + 2 * ng]
        g_sems = refs[14 + 2 * ng :]
        i = pl.program_id(0)

        @pl.when(i == 0)
        def _():
            _gather_start(g_ins, g_outs, g_sems)
            cv_s[0:HALO, :] = jnp.zeros((HALO, CC), F32)
            vp_s[0:HALO, :] = jnp.zeros((HALO, CC), F32)

        xv = x_ref[...]
        sh1, sc1, g1 = mod_ref[0:1, :], mod_ref[1:2, :], mod_ref[2:3, :]
        h = (xv * (1.0 + sc1) + sh1).astype(BF16)
        h_ref[...] = h
        z = jnp.dot(h, win_ref[...], preferred_element_type=F32)
        z_ref[...] = z.astype(BF16)
        gb, gc, vc, vp = z[:, 0:CC], z[:, CC : 2 * CC], z[:, 2 * CC : 3 * CC], z[:, 3 * CC : 4 * CC]
        cv = gc * vc
        cv_s[HALO : HALO + tm, :] = cv
        vp_s[HALO : HALO + tm, :] = vp
        conv = cw_ref[0:1, :] * cv_s[HALO - 2 : HALO - 2 + tm, :] + cw_ref[1:2, :] * cv_s[HALO - 1 : HALO - 1 + tm, :] + cw_ref[2:3, :] * cv
        parts = [gb * conv]
        row = i * tm + lax.broadcasted_iota(jnp.int32, (tm, 1), 0)
        feats, _ = _pool_features(vp, vp_s, row, tm)
        for g in range(len(POOL_WINDOWS)):
            pw = jnp.dot(feats[g].astype(BF16), wp_ref[g], preferred_element_type=F32)
            parts.append(pw * ps_ref[0:1, 128 * g : 128 * g + 128])
        cv_s[0:HALO, :] = cv_s[tm : tm + HALO, :]
        vp_s[0:HALO, :] = vp_s[tm : tm + HALO, :]
        ycat = jnp.concatenate(parts, axis=1).astype(BF16)
        mix = jnp.dot(ycat, wout_ref[...], preferred_element_type=F32)
        mix_ref[...] = mix
        xhat, rstd = _ln_fwd(DEEPNORM_ALPHA * xv + (1.0 + g1) * mix)
        xhat_ref[...] = xhat
        rstd_ref[...] = rstd

        for j in range(3):

            @pl.when(i == fwd_steps[j])
            def _(j=j):
                _gather_forward(g_ins, g_outs, g_sems, j)

        @pl.when(i == nt - 1)
        def _():
            _gather_finish(g_ins, g_outs, g_sems)

    tile = lambda w: pl.BlockSpec((tm, w), lambda i: (i, 0))
    return pl.pallas_call(
        body,
        name="f1",
        grid=(nt,),
        out_shape=[
            jax.ShapeDtypeStruct((T, ZW), BF16),
            jax.ShapeDtypeStruct((T, D), BF16),
            jax.ShapeDtypeStruct((T, D), F32),
            jax.ShapeDtypeStruct((T, 1), F32),
            jax.ShapeDtypeStruct((T, D), F32),
        ]
        + _gather_out_shape(gather, by_cols),
        in_specs=[tile(D)] + [VMEM_SPEC] * 6 + [ANY_SPEC] * ng,
        out_specs=[tile(ZW), tile(D), tile(D), tile(1), tile(D)] + [ANY_SPEC] * ng,
        scratch_shapes=[pltpu.VMEM((HALO + tm, CC), F32), pltpu.VMEM((HALO + tm, CC), F32)] + _gather_scratch(ng),
        compiler_params=pltpu.CompilerParams(dimension_semantics=("arbitrary",), vmem_limit_bytes=VMEM_LIMIT),
    )(x, mod, w_in, conv_w, w_pool, pool_scale, w_out, *gather)


def _fb2(xhat1, target, mod, ln, w_mi, w_mo, tm):
    T, D = xhat1.shape
    H = w_mi.shape[1]
    hc = min(1024, H)
    nb = H // hc
    nt = T // tm

    def body(xh1_ref, t_ref, mod_ref, ln_ref, wmi_ref, wmo_ref, dx1_ref, h2_ref, a_ref, du_ref, df_ref, acc_ref):
        i = pl.program_id(0)

        @pl.when(i == 0)
        def _():
            acc_ref[...] = jnp.zeros((8, D), F32)

        sh2, sc2, g2 = mod_ref[3:4, :], mod_ref[4:5, :], mod_ref[5:6, :]
        x1 = xh1_ref[...] * ln_ref[0:1, :] + ln_ref[1:2, :]
        h2 = (x1 * (1.0 + sc2) + sh2).astype(BF16)
        h2_ref[...] = h2
        f = jnp.zeros((tm, D), F32)
        for k in range(nb):
            ks = slice(k * hc, (k + 1) * hc)
            r = jnp.maximum(jnp.dot(h2, wmi_ref[:, ks], preferred_element_type=F32), 0.0)
            du_ref[:, ks] = r.astype(BF16)
            a = (r * r).astype(BF16)
            a_ref[:, ks] = a
            f = f + jnp.dot(a, wmo_ref[ks, :], preferred_element_type=F32)
        xhat2, rstd2 = _ln_fwd(DEEPNORM_ALPHA * x1 + (1.0 + g2) * f)
        ln2_g = ln_ref[2:3, :]
        d = xhat2 * ln2_g + ln_ref[3:4, :] - t_ref[...]
        dr2 = _ln_bwd(d * (ln2_g * (1.0 / D)), xhat2, rstd2)
        df = ((1.0 + g2) * dr2).astype(BF16)
        df_ref[...] = df
        dh2 = jnp.zeros((tm, D), F32)
        for k in range(nb):
            ks = slice(k * hc, (k + 1) * hc)
            da = lax.dot_general(df, wmo_ref[ks, :], NT, preferred_element_type=F32)
            du = (da * (2.0 * du_ref[:, ks].astype(F32))).astype(BF16)
            du_ref[:, ks] = du
            dh2 = dh2 + lax.dot_general(du, wmi_ref[:, ks], NT, preferred_element_type=F32)
        dx1_ref[...] = DEEPNORM_ALPHA * dr2 + dh2 * (1.0 + sc2)
        acc_ref[0:1, :] += _colsum(d * xhat2) * (1.0 / D)
        acc_ref[1:2, :] += _colsum(d) * (1.0 / D)
        acc_ref[2:3, :] += _colsum(dh2)
        acc_ref[3:4, :] += _colsum(dh2 * x1)
        acc_ref[4:5, :] += _colsum(dr2 * f)
        acc_ref[5:6, :] += jnp.zeros((1, D), F32) + (0.5 / D) * jnp.sum(d * d)

    tile = lambda w: pl.BlockSpec((tm, w), lambda i: (i, 0))
    return pl.pallas_call(
        body,
        name="fb2",
        grid=(nt,),
        out_shape=[
            jax.ShapeDtypeStruct((T, D), F32),
            jax.ShapeDtypeStruct((T, D), BF16),
            jax.ShapeDtypeStruct((T, H), BF16),
            jax.ShapeDtypeStruct((T, H), BF16),
            jax.ShapeDtypeStruct((T, D), BF16),
            jax.ShapeDtypeStruct((8, D), F32),
        ],
        in_specs=[tile(D), tile(D)] + [VMEM_SPEC] * 4,
        out_specs=[tile(D), tile(D), tile(H), tile(H), tile(D), pl.BlockSpec((8, D), lambda i: (0, 0))],
        compiler_params=pltpu.CompilerParams(dimension_semantics=("arbitrary",), vmem_limit_bytes=VMEM_LIMIT),
    )(xhat1, target, mod, ln, w_mi, w_mo)


def _b1(dx1, xhat1, rstd1, x, mix, z, mod, ln, w_out, w_in, conv_w, w_pool, pool_scale, tm):
    T, D = x.shape
    ZW = w_in.shape[1]
    CC = ZW // 4
    nt = T // tm
    hb = tm // HALO

    def body(dx1_ref, xh1_ref, rstd_ref, x_ref, mix_ref, z_ref, zh_ref, mod_ref, ln_ref, wout_ref, win_ref, cw_ref, wp_ref, ps_ref,
             dx_ref, dmix_ref, ycat_ref, dz_ref, acc_ref, gcw_ref, gwp_ref, cv_s, vp_s, e_s, q_s):
        i = pl.program_id(0)
        j = nt - 1 - i

        @pl.when(i == 0)
        def _():
            acc_ref[...] = jnp.zeros((8, D), F32)
            gcw_ref[...] = jnp.zeros((8, CC), F32)
            gwp_ref[...] = jnp.zeros(gwp_ref.shape, F32)
            e_s[tm : tm + HALO, :] = jnp.zeros((HALO, CC), F32)
            q_s[tm : tm + HALO, :] = jnp.zeros((HALO, CC), F32)

        sh1, sc1, g1 = mod_ref[0:1, :], mod_ref[1:2, :], mod_ref[2:3, :]
        dx1 = dx1_ref[...]
        xhat1 = xh1_ref[...]
        acc_ref[0:1, :] += _colsum(dx1 * xhat1)
        acc_ref[1:2, :] += _colsum(dx1)
        dr1 = _ln_bwd(dx1 * ln_ref[0:1, :], xhat1, rstd_ref[...])
        acc_ref[4:5, :] += _colsum(dr1 * mix_ref[...])
        dmix = ((1.0 + g1) * dr1).astype(BF16)
        dmix_ref[...] = dmix
        dycat = lax.dot_general(dmix, wout_ref[...], NT, preferred_element_type=F32)

        z = z_ref[...].astype(F32)
        zh = zh_ref[...].astype(F32) * jnp.where(j > 0, 1.0, 0.0)
        gb, gc, vc, vp = z[:, 0:CC], z[:, CC : 2 * CC], z[:, 2 * CC : 3 * CC], z[:, 3 * CC : 4 * CC]
        cv = gc * vc
        cv_s[0:HALO, :] = zh[:, CC : 2 * CC] * zh[:, 2 * CC : 3 * CC]
        cv_s[HALO : HALO + tm, :] = cv
        vp_s[0:HALO, :] = zh[:, 3 * CC : 4 * CC]
        vp_s[HALO : HALO + tm, :] = vp
        cv_m2 = cv_s[HALO - 2 : HALO - 2 + tm, :]
        cv_m1 = cv_s[HALO - 1 : HALO - 1 + tm, :]
        w0, w1, w2 = cw_ref[0:1, :], cw_ref[1:2, :], cw_ref[2:3, :]
        conv = w0 * cv_m2 + w1 * cv_m1 + w2 * cv
        dyc = dycat[:, 0:CC]
        e = dyc * gb
        e_s[0:tm, :] = e
        dcv = w2 * e + w1 * e_s[1 : 1 + tm, :] + w0 * e_s[2 : 2 + tm, :]
        gcw_ref[0:1, :] += _colsum(e * cv_m2)
        gcw_ref[1:2, :] += _colsum(e * cv_m1)
        gcw_ref[2:3, :] += _colsum(e * cv)
        y_parts = [gb * conv]
        dz_parts = [dyc * conv, dcv * vc, dcv * gc]

        row = j * tm + lax.broadcasted_iota(jnp.int32, (tm, 1), 0)
        feats, inv_cnts = _pool_features(vp, vp_s, row, tm)
        gps_parts, dps = [], []
        for g in range(len(POOL_WINDOWS)):
            cols = slice(128 * g, 128 * g + 128)
            p = feats[g].astype(BF16)
            scale = ps_ref[0:1, cols]
            pw = jnp.dot(p, wp_ref[g], preferred_element_type=F32)
            y_parts.append(pw * scale)
            dyp = dycat[:, CC + 128 * g : CC + 128 * g + 128]
            gps_parts.append(_colsum(dyp * pw))
            dpw = (dyp * scale).astype(BF16)
            gwp_ref[g] += lax.dot_general(p, dpw, TN, preferred_element_type=F32)
            dp = lax.dot_general(dpw, wp_ref[g], NT, preferred_element_type=F32)
            q_s[0:tm, cols] = dp * inv_cnts[g]
            dps.append(dp)
        sq = _window_sums(q_s[...], tm, causal=False)
        dz_parts += [sq[g] - dps[g] for g in range(len(POOL_WINDOWS))]
        gcw_ref[3:4, :] += jnp.concatenate(gps_parts, axis=1)
        ycat_ref[...] = jnp.concatenate(y_parts, axis=1).astype(BF16)
        dz = jnp.concatenate(dz_parts, axis=1).astype(BF16)
        dz_ref[...] = dz
        dh = lax.dot_general(dz, win_ref[...], NT, preferred_element_type=F32)
        acc_ref[2:3, :] += _colsum(dh)
        acc_ref[3:4, :] += _colsum(dh * x_ref[...])
        dx_ref[...] = DEEPNORM_ALPHA * dr1 + dh * (1.0 + sc1)
        e_s[tm : tm + HALO, :] = e_s[0:HALO, :]
        q_s[tm : tm + HALO, :] = q_s[0:HALO, :]

    tile = lambda w: pl.BlockSpec((tm, w), lambda i: (nt - 1 - i, 0))
    halo = pl.BlockSpec((HALO, ZW), lambda i: (jnp.maximum((nt - 1 - i) * hb - 1, 0), 0))
    fixed = lambda shape: pl.BlockSpec(shape, lambda i: (0,) * len(shape))
    return pl.pallas_call(
        body,
        name="b1",
        grid=(nt,),
        out_shape=[
            jax.ShapeDtypeStruct((T, D), F32),
            jax.ShapeDtypeStruct((T, D), BF16),
            jax.ShapeDtypeStruct((T, D), BF16),
            jax.ShapeDtypeStruct((T, ZW), BF16),
            jax.ShapeDtypeStruct((8, D), F32),
            jax.ShapeDtypeStruct((8, CC), F32),
            jax.ShapeDtypeStruct(w_pool.shape, F32),
        ],
        in_specs=[tile(D), tile(D), tile(1), tile(D), tile(D), tile(ZW), halo] + [VMEM_SPEC] * 7,
        out_specs=[tile(D), tile(D), tile(D), tile(ZW), fixed((8, D)), fixed((8, CC)), fixed(w_pool.shape)],
        scratch_shapes=[
            pltpu.VMEM((HALO + tm, CC), F32),
            pltpu.VMEM((HALO + tm, CC), F32),
            pltpu.VMEM((tm + HALO, CC), F32),
            pltpu.VMEM((tm + HALO, CC), F32),
        ],
        compiler_params=pltpu.CompilerParams(dimension_semantics=("arbitrary",), vmem_limit_bytes=VMEM_LIMIT),
    )(dx1, xhat1, rstd1, x, mix, z, z, mod, ln, w_out, w_in, conv_w, w_pool, pool_scale)


def _wgrad(a, b, bk, n_groups, bt, name, owners=None, scatter=None):
    T, K = a.shape
    N = b.shape[1]
    nk, nt, ng = K // bk, T // bt, N // n_groups
    nc = min(512, ng)
    ns = 0 if scatter is None else 1
    n_steps = nk * n_groups * nt
    mid_step = min(2, n_steps - 1)

    def body(*refs):
        a_ref, b_ref = refs[0], refs[1]
        o_ref = refs[2 + ns]
        acc = refs[3 + 2 * ns]
        if ns:
            s_hbm, s_recv, s_scr = refs[2], refs[4], refs[6:]
        kk, gg, t = pl.program_id(0), pl.program_id(1), pl.program_id(2)
        step = (kk * n_groups + gg) * nt + t

        if ns:

            @pl.when(step == 0)
            def _():
                _scatter_start(s_hbm, s_scr)

            @pl.when(step == mid_step)
            def _():
                _scatter_middle(s_hbm, s_recv, s_scr)

        @pl.when(t == 0)
        def _():
            acc[...] = jnp.zeros(acc.shape, F32)

        at = a_ref[...].T
        for c in range(ng // nc):
            cs = slice(c * nc, (c + 1) * nc)
            acc[:, cs] += jnp.dot(at, b_ref[:, cs], preferred_element_type=F32)

        @pl.when(t == nt - 1)
        def _():
            if owners is None:
                o_ref[...] = acc[...].astype(BF16)
            else:
                per = N // owners
                for o in range(ng // per):
                    o_ref[o] = acc[:, o * per : (o + 1) * per].astype(BF16)

        if ns:

            @pl.when(step == n_steps - 1)
            def _():
                _scatter_finish(s_hbm, s_recv, s_scr)

    if owners is None:
        out_shape = [jax.ShapeDtypeStruct((K, N), BF16)]
        out_specs = [pl.BlockSpec((bk, ng), lambda k, g, t: (k, g))]
    else:
        assert bk == K
        per = N // owners
        out_shape = [jax.ShapeDtypeStruct((owners, K, per), BF16)]
        out_specs = [pl.BlockSpec((ng // per, K, per), lambda k, g, t: (g, 0, 0))]
    ins, in_specs = [a, b], [pl.BlockSpec((bt, bk), lambda k, g, t: (t, k)), pl.BlockSpec((bt, ng), lambda k, g, t: (t, g))]
    scratch = [pltpu.VMEM((bk, ng), F32)]
    if ns:
        ins.append(scatter)
        in_specs.append(ANY_SPEC)
        out_shape.append(_scatter_out_shape(scatter))
        out_specs.append(ANY_SPEC)
        scratch += _scatter_scratch(*scatter.shape[1:])
    outs = pl.pallas_call(
        body,
        name=name,
        grid=(nk, n_groups, nt),
        out_shape=out_shape,
        in_specs=in_specs,
        out_specs=out_specs,
        scratch_shapes=scratch,
        compiler_params=pltpu.CompilerParams(dimension_semantics=("arbitrary", "arbitrary", "arbitrary"), vmem_limit_bytes=VMEM_LIMIT),
    )(*ins)
    return outs if ns else outs[0]


def _small_grads(acc1_t, acc2_t, gcw_t, gwp_all, cond_t, my_slot, w_cols):
    D = acc1_t.shape[2]
    n_chunk = D // 128
    q_mine = w_cols // 128

    def total(ref, r):
        s = ref[r, 0:1, :]
        for k in range(1, N_DEV):
            s = s + ref[r, k : k + 1, :]
        return s

    def body(slot_ref, a1_ref, a2_ref, gcw_ref, gwp_ref, ct_ref, gb_ref, gw_ref, gln_ref, gcwo_ref, gwpo_ref, loss_ref, dm_s):
        loss_ref[...] = total(a2_ref, 5)
        for s, (ref, r) in enumerate([(a1_ref, 2), (a1_ref, 3), (a1_ref, 4), (a2_ref, 2), (a2_ref, 3), (a2_ref, 4)]):
            gb_ref[0:1, s * D : (s + 1) * D] = total(ref, r)
            for qq in range(n_chunk):
                dm_s[s * n_chunk + qq] = ref[r, :, 128 * qq : 128 * qq + 128]
        gln_ref[0:1, :] = total(a1_ref, 0)
        gln_ref[1:2, :] = total(a1_ref, 1)
        gln_ref[2:3, :] = total(a2_ref, 0)
        gln_ref[3:4, :] = total(a2_ref, 1)
        gcwo_ref[...] = jnp.zeros(gcwo_ref.shape, F32)
        for r in range(4):
            gcwo_ref[r : r + 1, :] = total(gcw_ref, r)
        wp = gwp_ref[0]
        for k in range(1, N_DEV):
            wp = wp + gwp_ref[k]
        gwpo_ref[0] = wp
        ct = ct_ref[...]
        cond_t = ct * jax.nn.sigmoid(ct)
        q0 = slot_ref[0] * q_mine
        for q in range(q_mine):
            dm = dm_s[q0 + q]
            out = cond_t[:, 0:1] * dm[0:1, :]
            for k in range(1, N_DEV):
                out = out + cond_t[:, k : k + 1] * dm[k : k + 1, :]
            gw_ref[0, :, 128 * q : 128 * q + 128] = out

    CC = gcw_t.shape[2]
    return pl.pallas_call(
        body,
        name="small_grads",
        out_shape=[
            jax.ShapeDtypeStruct((1, 6 * D), F32),
            jax.ShapeDtypeStruct((1, D, w_cols), F32),
            jax.ShapeDtypeStruct((4, D), F32),
            jax.ShapeDtypeStruct((8, CC), F32),
            jax.ShapeDtypeStruct((1, *gwp_all.shape[1:]), F32),
            jax.ShapeDtypeStruct((1, D), F32),
        ],
        in_specs=[pl.BlockSpec(memory_space=pltpu.SMEM)] + [VMEM_SPEC] * 5,
        out_specs=[VMEM_SPEC] * 6,
        scratch_shapes=[pltpu.VMEM((6 * n_chunk, N_DEV, 128), F32)],
        compiler_params=pltpu.CompilerParams(vmem_limit_bytes=VMEM_LIMIT),
    )(my_slot, acc1_t, acc2_t, gcw_t, gwp_all, cond_t)


def kernel(x, c, w_ada, b_ada, w_in, conv_w, w_pool, pool_scale, w_out, ln1_g, ln1_b, w_mlp_in, w_mlp_out, ln2_g, ln2_b, loss_target, m_w_ada, m_b_ada, m_w_in, m_conv_w, m_w_pool, m_pool_scale, m_w_out, m_ln1_g, m_ln1_b, m_w_mlp_in, m_w_mlp_out, m_ln2_g, m_ln2_b, v_w_ada, v_b_ada, v_w_in, v_conv_w, v_w_pool, v_pool_scale, v_w_out, v_ln1_g, v_ln1_b, v_w_mlp_in, v_w_mlp_out, v_ln2_g, v_ln2_b):
    T, D = x.shape[1], x.shape[2]
    H = w_mlp_out.shape[1] * N_DEV
    ZW = w_in.shape[2] * N_DEV
    CC = ZW // 4
    tm = min(512, T // 2)
    bt = min(1024, T)
    ax, ay, ac = _my_place()
    me = _slot(ax, ay, ac)

    w_cols = w_ada.shape[2]
    b_mine = lax.dynamic_slice(b_ada, (0, me * w_cols), (1, w_cols))
    w_in_f, w_out_g, cw_g, c_g, mod_g = _prologue(w_in[0].astype(BF16), w_out[0].astype(BF16), conv_w[0], c, w_ada, b_mine)
    w_out_f = w_out_g.reshape(D, D)
    conv_w_f = jnp.transpose(cw_g, (1, 0, 2)).reshape(conv_w.shape[1], CC)
    c_all = c_g.reshape(N_DEV, D)
    mod = lax.dynamic_index_in_dim(mod_g, me, axis=1, keepdims=False).reshape(6, D)

    ln = jnp.concatenate([ln1_g, ln1_b, ln2_g, ln2_b], axis=0)
    w_pool16 = w_pool[0].astype(BF16)
    xs, target = x[0], loss_target[0]

    z, h, xhat1, rstd1, mix, w_mi_f, w_mo_g = _f1(
        xs, mod, w_in_f, conv_w_f, w_pool16, pool_scale, w_out_f, tm,
        [w_mlp_in[0].astype(BF16), w_mlp_out[0].astype(BF16)], [True, False])
    dx1, h2, a, du, df, acc2 = _fb2(xhat1, target, mod, ln, w_mi_f, w_mo_g.reshape(H, D), tm // 2)

    grad_x, dmix, ycat, dz, acc1, gcw, gwp = _b1(
        dx1, xhat1, rstd1, xs, mix, z, mod, ln, w_out_f, w_in_f, conv_w_f, w_pool16, pool_scale, tm)

    gp_mo = _wgrad(a, df, D, 1, bt, "wgrad_mlp_out").reshape(N_DEV, H // N_DEV, D)
    gp_mi, rv_mo = _wgrad(h2, du, D, 2, bt, "wgrad_mlp_in", owners=N_DEV, scatter=gp_mo)
    gp_in, rv_mi = _wgrad(h, dz, D, 1, bt, "wgrad_in", owners=N_DEV, scatter=gp_mi)
    gp_out, rv_in = _wgrad(ycat, dmix, D, 1, bt, "wgrad_out", scatter=gp_in)
    gp_out = gp_out.reshape(N_DEV, D // N_DEV, D)

    acc1_g, acc2_g, gcw_g, gwp_g, rv_out = _allgather([acc1, acc2, gcw, gwp], "gather_small", in_vmem=True, scatter=gp_out)
    g_b_ada, g_w_ada, g_ln, g_cw, g_w_pool, loss_row = _small_grads(
        jnp.transpose(acc1_g, (1, 0, 2)), jnp.transpose(acc2_g, (1, 0, 2)), jnp.transpose(gcw_g, (1, 0, 2)), gwp_g,
        c_all.T, jnp.reshape(me, (1,)).astype(jnp.int32), w_cols)
    cc_mine = conv_w.shape[2]
    g_conv_w = lax.dynamic_slice(g_cw, (0, me * cc_mine), (conv_w.shape[1], cc_mine))[None]
    g_pool_scale = g_cw[3:4, :]
    g_ln1_g, g_ln1_b, g_ln2_g, g_ln2_b = g_ln[0:1], g_ln[1:2], g_ln[2:3], g_ln[3:4]

    small = _adamw_multi(
        [
            (b_ada, g_b_ada, m_b_ada, v_b_ada),
            (conv_w, g_conv_w, m_conv_w, v_conv_w),
            (w_pool, g_w_pool, m_w_pool, v_w_pool),
            (pool_scale, g_pool_scale, m_pool_scale, v_pool_scale),
            (ln1_g, g_ln1_g, m_ln1_g, v_ln1_g),
            (ln1_b, g_ln1_b, m_ln1_b, v_ln1_b),
            (ln2_g, g_ln2_g, m_ln2_g, v_ln2_g),
            (ln2_b, g_ln2_b, m_ln2_b, v_ln2_b),
        ],
        "adamw_small")
    u_b_ada, u_conv_w, u_w_pool, u_pool_scale, u_ln1_g, u_ln1_b, u_ln2_g, u_ln2_b = small

    g_w_ada, *u_w_ada = _sum_adamw(g_w_ada, w_ada, m_w_ada, v_w_ada, "adamw_w_ada")
    g_w_mo, *u_w_mo = _sum_adamw(rv_mo, w_mlp_out, m_w_mlp_out, v_w_mlp_out, "sum_w_mlp_out")
    g_w_mi, *u_w_mi = _sum_adamw(rv_mi, w_mlp_in, m_w_mlp_in, v_w_mlp_in, "sum_w_mlp_in")
    g_w_in, *u_w_in = _sum_adamw(rv_in, w_in, m_w_in, v_w_in, "sum_w_in")
    g_w_out, *u_w_out = _sum_adamw(rv_out, w_out, m_w_out, v_w_out, "sum_w_out")

    grads = [g_w_ada, g_b_ada, g_w_in, g_conv_w, g_w_pool, g_pool_scale, g_w_out, g_ln1_g, g_ln1_b, g_w_mi, g_w_mo, g_ln2_g, g_ln2_b]
    updates = [u_w_ada, u_b_ada, u_w_in, u_conv_w, u_w_pool, u_pool_scale, u_w_out, u_ln1_g, u_ln1_b, u_w_mi, u_w_mo, u_ln2_g, u_ln2_b]
    deltas = [u[0] for u in updates]
    new_m = [u[1] for u in updates]
    new_v = [u[2] for u in updates]
    return (loss_row[0, 0], grad_x[None], *grads, *deltas, *new_m, *new_v)
```

```python
import jax
import jax.numpy as jnp
from jax import lax
from jax.experimental import pallas as pl
from jax.experimental.pallas import tpu as pltpu

F32 = jnp.float32
BF16 = jnp.bfloat16
MESH = pl.DeviceIdType.MESH
N_DEV = 8

LN_EPS = 1e-5
DEPTH = 1
DEEPNORM_ALPHA = (2.0 * DEPTH) ** 0.25
POOL_WINDOWS = (2, 4, 8, 16)
HALO = 16

ADAM_LR = 0.001
ADAM_B1 = 0.9
ADAM_B2 = 0.999
ADAM_EPS = 1e-08
ADAM_WD = 0.01
ADAM_STEP = 10

VMEM_LIMIT = 56 * 1024 * 1024

VMEM_SPEC = pl.BlockSpec(memory_space=pltpu.VMEM)
ANY_SPEC = pl.BlockSpec(memory_space=pl.ANY)

NT = (((1,), (1,)), ((), ()))
TN = (((0,), (0,)), ((), ()))


def _my_place():
    return lax.axis_index("x"), lax.axis_index("y"), lax.axis_index("c")


def _slot(x, y, c):
    return 4 * x + 2 * y + c


def _gather_place(ins, outs, a, slot):
    if len(outs[a].shape) == len(ins[a].shape):
        wb = ins[a].shape[1]
        return outs[a].at[:, pl.ds(pl.multiple_of(slot * wb, wb), wb)]
    return outs[a].at[slot]


def _gather_copy(ins, outs, sems, a, k, block, to, from_shard=False):
    send_sems, recv_sems, _ = sems
    dst = _gather_place(ins, outs, a, _slot(*block))
    return pltpu.make_async_remote_copy(
        src_ref=ins[a] if from_shard else dst,
        dst_ref=dst,
        send_sem=send_sems.at[7 * a + k],
        recv_sem=recv_sems.at[7 * a + k],
        device_id=to,
        device_id_type=MESH,
    )


def _gather_peers():
    x, y, c = _my_place()
    return (x, y, c), (x, y, 1 - c), [(1 - x, y), (x, 1 - y), (1 - x, 1 - y)]


def _gather_first(ins, outs, sems):
    me, sibling, chips = _gather_peers()
    first = []
    for a in range(len(ins)):
        first.append(_gather_copy(ins, outs, sems, a, 0, me, sibling, from_shard=True))
        first += [_gather_copy(ins, outs, sems, a, 1 + j, me, (*chip, me[2]), from_shard=True) for j, chip in enumerate(chips)]
    return first


def _gather_mine(ins, outs, sems, a):
    me, _, _ = _gather_peers()
    return pltpu.make_async_copy(ins[a], _gather_place(ins, outs, a, _slot(*me)), sems[2].at[a])


def _gather_start(ins, outs, sems):
    for a in range(len(ins)):
        _gather_mine(ins, outs, sems, a).start()
    for cp in _gather_first(ins, outs, sems):
        cp.start()


def _gather_forward(ins, outs, sems, j):
    me, sibling, chips = _gather_peers()
    for a in range(len(ins)):
        _gather_copy(ins, outs, sems, a, 1 + j, (*chips[j], me[2]), me).wait_recv()
        _gather_copy(ins, outs, sems, a, 4 + j, (*chips[j], me[2]), sibling).start()


def _gather_finish(ins, outs, sems):
    me, sibling, chips = _gather_peers()
    for a in range(len(ins)):
        _gather_copy(ins, outs, sems, a, 0, sibling, me).wait_recv()
        for j, chip in enumerate(chips):
            _gather_copy(ins, outs, sems, a, 4 + j, (*chip, 1 - me[2]), me).wait_recv()
    for cp in _gather_first(ins, outs, sems):
        cp.wait_send()
    for a in range(len(ins)):
        for j, chip in enumerate(chips):
            _gather_copy(ins, outs, sems, a, 4 + j, (*chip, me[2]), sibling).wait_send()
        _gather_mine(ins, outs, sems, a).wait()


def _gather_scratch(n):
    return [pltpu.SemaphoreType.DMA((7 * n,)), pltpu.SemaphoreType.DMA((7 * n,)), pltpu.SemaphoreType.DMA((n,))]


def _gather_out_shape(shards, by_cols):
    return [
        jax.ShapeDtypeStruct((s.shape[0], N_DEV * s.shape[1]) if cols else (N_DEV, *s.shape), s.dtype)
        for s, cols in zip(shards, by_cols)
    ]


N_CHIP = 4


def _scatter_scratch(rows, cols):
    block = pltpu.VMEM((N_CHIP, rows, cols), BF16)
    dma = pltpu.SemaphoreType.DMA
    return [block, block, block, dma((N_CHIP,)), dma((N_CHIP,)), dma((N_CHIP,)), dma((N_CHIP - 1,)), dma((N_CHIP - 1,)), dma]


def _scatter_pair_copies(g_hbm, scr):
    x, y, c = _my_place()
    mine, theirs, _, a_send, a_recv, load_sem = scr[:6]
    to_sibling = [
        pltpu.make_async_remote_copy(
            src_ref=g_hbm.at[2 * q + (1 - c)], dst_ref=theirs.at[q], send_sem=a_send.at[q], recv_sem=a_recv.at[q],
            device_id=(x, y, 1 - c), device_id_type=MESH)
        for q in range(N_CHIP)
    ]
    loads = [pltpu.make_async_copy(g_hbm.at[2 * q + c], mine.at[q], load_sem.at[q]) for q in range(N_CHIP)]
    return to_sibling, loads


def _scatter_sum_copies(recv, scr):
    x, y, c = _my_place()
    sums, b_send, b_recv, own_sem = scr[2], scr[6], scr[7], scr[8]
    q_me = 2 * x + y
    to_owner = [
        pltpu.make_async_remote_copy(
            src_ref=sums.at[2 * px + py], dst_ref=recv.at[q_me], send_sem=b_send.at[j], recv_sem=b_recv.at[j],
            device_id=(px, py, c), device_id_type=MESH)
        for j, (px, py) in enumerate([(1 - x, y), (x, 1 - y), (1 - x, 1 - y)])
    ]
    return to_owner, pltpu.make_async_copy(sums.at[q_me], recv.at[q_me], own_sem)


def _scatter_start(g_hbm, scr):
    to_sibling, loads = _scatter_pair_copies(g_hbm, scr)
    for cp in to_sibling + loads:
        cp.start()


def _scatter_middle(g_hbm, recv, scr):
    to_sibling, loads = _scatter_pair_copies(g_hbm, scr)
    for cp in to_sibling:
        cp.wait_recv()
    for cp in loads:
        cp.wait()
    mine, theirs, sums = scr[:3]

    def step(r, carry):
        rs = pl.ds(pl.multiple_of(r * ROW_CHUNK, ROW_CHUNK), ROW_CHUNK)
        for q in range(N_CHIP):
            sums[q, rs, :] = (mine[q, rs, :].astype(F32) + theirs[q, rs, :].astype(F32)).astype(BF16)
        return carry

    lax.fori_loop(0, mine.shape[1] // ROW_CHUNK, step, 0)
    to_owner, own = _scatter_sum_copies(recv, scr)
    for cp in to_owner + [own]:
        cp.start()


def _scatter_finish(g_hbm, recv, scr):
    to_sibling, _ = _scatter_pair_copies(g_hbm, scr)
    to_owner, own = _scatter_sum_copies(recv, scr)
    for cp in to_owner:
        cp.wait_recv()
    for cp in to_sibling + to_owner:
        cp.wait_send()
    own.wait()


def _scatter_out_shape(gparts):
    return jax.ShapeDtypeStruct((N_CHIP, *gparts.shape[1:]), gparts.dtype)


def _adamw_math(w, g, m, v):
    m = ADAM_B1 * m + (1.0 - ADAM_B1) * g
    v = ADAM_B2 * v + (1.0 - ADAM_B2) * (g * g)
    m_hat = m / (1.0 - ADAM_B1**ADAM_STEP)
    v_hat = v / (1.0 - ADAM_B2**ADAM_STEP)
    delta = -ADAM_LR * (m_hat / (jnp.sqrt(v_hat) + ADAM_EPS) + ADAM_WD * w)
    return delta, m, v


ROW_CHUNK = 64


ELEMS_PER_STEP = 128 * 1024


def _allgather(shards, name, in_vmem, scatter=None, by_cols=None):
    n = len(shards)
    by_cols = by_cols or [False] * n
    ns = 0 if scatter is None else 1

    def body(*refs):
        ins, outs = refs[:n], refs[n + ns : 2 * n + ns]
        rest = refs[2 * n + ns :]
        if ns:
            g_hbm, recv, sems, scr = refs[n], rest[0], rest[1:4], rest[4:]
            _scatter_start(g_hbm, scr)
        else:
            sems = rest[0:3]
        _gather_start(ins, outs, sems)
        if ns:
            _scatter_middle(g_hbm, recv, scr)
        for j in range(3):
            _gather_forward(ins, outs, sems, j)
        _gather_finish(ins, outs, sems)
        if ns:
            _scatter_finish(g_hbm, recv, scr)

    spec = VMEM_SPEC if in_vmem else ANY_SPEC
    extra_in, extra_out, extra_spec, extra_scratch = [], [], [], []
    if ns:
        extra_in, extra_spec = [scatter], [ANY_SPEC]
        extra_out = [_scatter_out_shape(scatter)]
        extra_scratch = _scatter_scratch(*scatter.shape[1:])
    return pl.pallas_call(
        body,
        name=name,
        out_shape=_gather_out_shape(shards, by_cols) + extra_out,
        in_specs=[spec] * n + extra_spec,
        out_specs=[spec] * n + extra_spec,
        scratch_shapes=_gather_scratch(n) + extra_scratch,
        compiler_params=pltpu.CompilerParams(vmem_limit_bytes=VMEM_LIMIT),
    )(*shards, *extra_in)


def _sum_adamw(parts, w, m, v, name):
    _, rows, cols = w.shape
    rb = rows
    while rb * cols > ELEMS_PER_STEP and rb % 16 == 0:
        rb //= 2

    def body(p_ref, w_ref, m_ref, v_ref, grad_ref, delta_ref, nm_ref, nv_ref):
        g = p_ref[0].astype(F32)
        for k in range(1, p_ref.shape[0]):
            g = g + p_ref[k].astype(F32)
        delta, nm, nv = _adamw_math(w_ref[0], g, m_ref[0], v_ref[0])
        grad_ref[0] = g
        delta_ref[0] = delta
        nm_ref[0] = nm
        nv_ref[0] = nv

    block = lambda lead: pl.BlockSpec((lead, rb, cols), lambda i: (0, i, 0))
    out = jax.ShapeDtypeStruct(w.shape, F32)
    return pl.pallas_call(
        body,
        name=name,
        grid=(rows // rb,),
        out_shape=[out] * 4,
        in_specs=[block(parts.shape[0])] + [block(1)] * 3,
        out_specs=[block(1)] * 4,
        compiler_params=pltpu.CompilerParams(dimension_semantics=("arbitrary",), vmem_limit_bytes=VMEM_LIMIT),
    )(parts, w, m, v)


def _adamw_multi(items, name):
    n = len(items)

    def body(*refs):
        ins, outs = refs[: 4 * n], refs[4 * n :]
        for a in range(n):
            w_ref, g_ref, m_ref, v_ref = ins[4 * a : 4 * a + 4]
            d_ref, nm_ref, nv_ref = outs[3 * a : 3 * a + 3]
            delta, nm, nv = _adamw_math(w_ref[...], g_ref[...], m_ref[...], v_ref[...])
            d_ref[...] = delta
            nm_ref[...] = nm
            nv_ref[...] = nv

    flat = [a for it in items for a in it]
    out_shape = [jax.ShapeDtypeStruct(it[0].shape, F32) for it in items for _ in range(3)]
    outs = pl.pallas_call(
        body,
        name=name,
        out_shape=out_shape,
        in_specs=[VMEM_SPEC] * (4 * n),
        out_specs=[VMEM_SPEC] * (3 * n),
        compiler_params=pltpu.CompilerParams(vmem_limit_bytes=VMEM_LIMIT),
    )(*flat)
    return [tuple(outs[3 * a : 3 * a + 3]) for a in range(n)]


def _prologue(w_in, w_out, conv_w, c, w_ada, b_mine):
    D = c.shape[1]
    wc = w_ada.shape[2]
    shards16 = [jax.ShapeDtypeStruct(w_in.shape[1:], BF16), jax.ShapeDtypeStruct(w_out.shape[1:], BF16)]

    def gather_now(ins, outs, sems):
        _gather_start(ins, outs, sems)
        for j in range(3):
            _gather_forward(ins, outs, sems, j)
        _gather_finish(ins, outs, sems)

    def body(win_ref, wout_ref, cw_ref, c_ref, wada_ref, b_ref, win_g, wout_g, cw_g, c_g, mod_g, win16, wout16, c_s, mp_s, *sems):
        win16[...] = win_ref[0].astype(BF16)
        wout16[...] = wout_ref[0].astype(BF16)
        w_ins, w_outs, w_sems = (win16, wout16, cw_ref), (win_g, wout_g, cw_g), sems[0:3]
        _gather_start(w_ins, w_outs, w_sems)
        gather_now((c_ref,), (c_g,), sems[3:6])
        for k in range(N_DEV):
            c_s[k : k + 1, :] = c_g[k]
        cv = c_s[...]
        cond = cv * jax.nn.sigmoid(cv)
        mp_s[...] = jnp.dot(cond, wada_ref[0], precision=lax.Precision.HIGHEST, preferred_element_type=F32) + b_ref[...]
        gather_now((mp_s,), (mod_g,), sems[6:9])
        for j in range(3):
            _gather_forward(w_ins, w_outs, w_sems, j)
        _gather_finish(w_ins, w_outs, w_sems)

    return pl.pallas_call(
        body,
        name="prologue",
        out_shape=_gather_out_shape(shards16 + [conv_w], [True, False, False])
        + [jax.ShapeDtypeStruct((N_DEV, 1, D), F32), jax.ShapeDtypeStruct((N_DEV, N_DEV, wc), F32)],
        in_specs=[VMEM_SPEC, VMEM_SPEC, ANY_SPEC] + [VMEM_SPEC] * 3,
        out_specs=[ANY_SPEC] * 3 + [VMEM_SPEC] * 2,
        scratch_shapes=[pltpu.VMEM(s.shape, BF16) for s in shards16]
        + [pltpu.VMEM((N_DEV, D), F32), pltpu.VMEM((N_DEV, wc), F32)]
        + _gather_scratch(3) + _gather_scratch(1) + _gather_scratch(1),
        compiler_params=pltpu.CompilerParams(vmem_limit_bytes=VMEM_LIMIT),
    )(w_in, w_out, conv_w, c, w_ada, b_mine)


def _ln_fwd(r):
    mu = jnp.mean(r, axis=-1, keepdims=True)
    d = r - mu
    var = jnp.mean(d * d, axis=-1, keepdims=True)
    rstd = lax.rsqrt(var + LN_EPS)
    return d * rstd, rstd


def _ln_bwd(dxh, xhat, rstd):
    m1 = jnp.mean(dxh, axis=-1, keepdims=True)
    m2 = jnp.mean(dxh * xhat, axis=-1, keepdims=True)
    return rstd * (dxh - m1 - xhat * m2)


def _colsum(a):
    return jnp.sum(a, axis=0, keepdims=True)


def _window_sums(ext, tm, causal):
    n = ext.shape[0]
    lo = HALO if causal else 0
    s, out = ext, []
    for p in range(len(POOL_WINDOWS)):
        assert POOL_WINDOWS[p] == 2 ** (p + 1)
        k = 2**p
        s = s + pltpu.roll(s, k if causal else n - k, 0)
        out.append(s[lo : lo + tm, 0:128])
        if p + 1 < len(POOL_WINDOWS):
            s = s[:, 128:]
    return out


def _pool_features(vp, vp_s, row, tm):
    sums = _window_sums(vp_s[...], tm, causal=True)
    feats, inv_cnts = [], []
    for g, win in enumerate(POOL_WINDOWS):
        inv_cnt = 1.0 / jnp.minimum(row + 1, win).astype(F32)
        feats.append(sums[g] * inv_cnt - vp[:, 128 * g : 128 * g + 128])
        inv_cnts.append(inv_cnt)
    return feats, inv_cnts


def _f1(x, mod, w_in, conv_w, w_pool, pool_scale, w_out, tm, gather, by_cols):
    T, D = x.shape
    ZW = w_in.shape[1]
    CC = ZW // 4
    nt = T // tm
    ng = len(gather)
    fwd_steps = [max(nt - 3 + j, 0) for j in range(3)]

    shards16 = [jax.ShapeDtypeStruct(s.shape[1:], BF16) for s in gather]

    def body(*refs):
        x_ref, mod_ref, win_ref, cw_ref, wp_ref, ps_ref, wout_ref = refs[:7]
        g_f32 = refs[7 : 7 + ng]
        z_ref, h_ref, xhat_ref, rstd_ref, mix_ref = refs[7 + ng : 12 + ng]
        g_outs = refs[12 + ng : 12 + 2 * ng]
        cv_s, vp_s = refs[12 + 2 * ng : 14 + 2 * ng]
        g_ins = refs[14 + 2 * ng : 14 + 3 * ng]
        g_sems = refs[14 + 3 * ng :]
        i = pl.program_id(0)

        @pl.when(i == 0)
        def _():
            for src, dst in zip(g_f32, g_ins):
                dst[...] = src[0].astype(BF16)
            _gather_start(g_ins, g_outs, g_sems)
            cv_s[0:HALO, :] = jnp.zeros((HALO, CC), F32)
            vp_s[0:HALO, :] = jnp.zeros((HALO, CC), F32)

        xv = x_ref[...]
        sh1, sc1, g1 = mod_ref[0:1, :], mod_ref[1:2, :], mod_ref[2:3, :]
        h = (xv * (1.0 + sc1) + sh1).astype(BF16)
        h_ref[...] = h
        z = jnp.dot(h, win_ref[...], preferred_element_type=F32)
        z_ref[...] = z.astype(BF16)
        gb, gc, vc, vp = z[:, 0:CC], z[:, CC : 2 * CC], z[:, 2 * CC : 3 * CC], z[:, 3 * CC : 4 * CC]
        cv = gc * vc
        cv_s[HALO : HALO + tm, :] = cv
        vp_s[HALO : HALO + tm, :] = vp
        conv = cw_ref[0:1, :] * cv_s[HALO - 2 : HALO - 2 + tm, :] + cw_ref[1:2, :] * cv_s[HALO - 1 : HALO - 1 + tm, :] + cw_ref[2:3, :] * cv
        parts = [gb * conv]
        row = i * tm + lax.broadcasted_iota(jnp.int32, (tm, 1), 0)
        feats, _ = _pool_features(vp, vp_s, row, tm)
        for g in range(len(POOL_WINDOWS)):
            pw = jnp.dot(feats[g].astype(BF16), wp_ref[g].astype(BF16), preferred_element_type=F32)
            parts.append(pw * ps_ref[0:1, 128 * g : 128 * g + 128])
        cv_s[0:HALO, :] = cv_s[tm : tm + HALO, :]
        vp_s[0:HALO, :] = vp_s[tm : tm + HALO, :]
        ycat = jnp.concatenate(parts, axis=1).astype(BF16)
        mix = jnp.dot(ycat, wout_ref[...], preferred_element_type=F32)
        mix_ref[...] = mix
        xhat, rstd = _ln_fwd(DEEPNORM_ALPHA * xv + (1.0 + g1) * mix)
        xhat_ref[...] = xhat
        rstd_ref[...] = rstd

        for j in range(3):

            @pl.when(i == fwd_steps[j])
            def _(j=j):
                _gather_forward(g_ins, g_outs, g_sems, j)

        @pl.when(i == nt - 1)
        def _():
            _gather_finish(g_ins, g_outs, g_sems)

    tile = lambda w: pl.BlockSpec((tm, w), lambda i: (i, 0))
    return pl.pallas_call(
        body,
        name="f1",
        grid=(nt,),
        out_shape=[
            jax.ShapeDtypeStruct((T, ZW), BF16),
            jax.ShapeDtypeStruct((T, D), BF16),
            jax.ShapeDtypeStruct((T, D), F32),
            jax.ShapeDtypeStruct((T, 1), F32),
            jax.ShapeDtypeStruct((T, D), F32),
        ]
        + _gather_out_shape(shards16, by_cols),
        in_specs=[tile(D)] + [VMEM_SPEC] * (6 + ng),
        out_specs=[tile(ZW), tile(D), tile(D), tile(1), tile(D)] + [ANY_SPEC] * ng,
        scratch_shapes=[pltpu.VMEM((HALO + tm, CC), F32), pltpu.VMEM((HALO + tm, CC), F32)]
        + [pltpu.VMEM(s.shape, BF16) for s in shards16]
        + _gather_scratch(ng),
        compiler_params=pltpu.CompilerParams(dimension_semantics=("arbitrary",), vmem_limit_bytes=VMEM_LIMIT),
    )(x, mod, w_in, conv_w, w_pool, pool_scale, w_out, *gather)


def _fb2(xhat1, target, mod, ln, w_mi, w_mo, tm):
    T, D = xhat1.shape
    H = w_mi.shape[1]
    hc = min(1024, H)
    nb = H // hc
    nt = T // tm

    def body(xh1_ref, t_ref, mod_ref, ln_ref, wmi_ref, wmo_ref, dx1_ref, h2_ref, a_ref, du_ref, df_ref, acc_ref):
        i = pl.program_id(0)

        @pl.when(i == 0)
        def _():
            acc_ref[...] = jnp.zeros((8, D), F32)

        sh2, sc2, g2 = mod_ref[3:4, :], mod_ref[4:5, :], mod_ref[5:6, :]
        x1 = xh1_ref[...] * ln_ref[0:1, :] + ln_ref[1:2, :]
        h2 = (x1 * (1.0 + sc2) + sh2).astype(BF16)
        h2_ref[...] = h2
        f = jnp.zeros((tm, D), F32)
        for k in range(nb):
            ks = slice(k * hc, (k + 1) * hc)
            r = jnp.maximum(jnp.dot(h2, wmi_ref[:, ks], preferred_element_type=F32), 0.0)
            du_ref[:, ks] = r.astype(BF16)
            a = (r * r).astype(BF16)
            a_ref[:, ks] = a
            f = f + jnp.dot(a, wmo_ref[ks, :], preferred_element_type=F32)
        xhat2, rstd2 = _ln_fwd(DEEPNORM_ALPHA * x1 + (1.0 + g2) * f)
        ln2_g = ln_ref[2:3, :]
        d = xhat2 * ln2_g + ln_ref[3:4, :] - t_ref[...]
        dr2 = _ln_bwd(d * (ln2_g * (1.0 / D)), xhat2, rstd2)
        df = ((1.0 + g2) * dr2).astype(BF16)
        df_ref[...] = df
        dh2 = jnp.zeros((tm, D), F32)
        for k in range(nb):
            ks = slice(k * hc, (k + 1) * hc)
            da = lax.dot_general(df, wmo_ref[ks, :], NT, preferred_element_type=F32)
            du = (da * (2.0 * du_ref[:, ks].astype(F32))).astype(BF16)
            du_ref[:, ks] = du
            dh2 = dh2 + lax.dot_general(du, wmi_ref[:, ks], NT, preferred_element_type=F32)
        dx1_ref[...] = DEEPNORM_ALPHA * dr2 + dh2 * (1.0 + sc2)
        acc_ref[0:1, :] += _colsum(d * xhat2) * (1.0 / D)
        acc_ref[1:2, :] += _colsum(d) * (1.0 / D)
        acc_ref[2:3, :] += _colsum(dh2)
        acc_ref[3:4, :] += _colsum(dh2 * x1)
        acc_ref[4:5, :] += _colsum(dr2 * f)
        acc_ref[5:6, :] += jnp.zeros((1, D), F32) + (0.5 / D) * jnp.sum(d * d)

    tile = lambda w: pl.BlockSpec((tm, w), lambda i: (i, 0))
    return pl.pallas_call(
        body,
        name="fb2",
        grid=(nt,),
        out_shape=[
            jax.ShapeDtypeStruct((T, D), F32),
            jax.ShapeDtypeStruct((T, D), BF16),
            jax.ShapeDtypeStruct((T, H), BF16),
            jax.ShapeDtypeStruct((T, H), BF16),
            jax.ShapeDtypeStruct((T, D), BF16),
            jax.ShapeDtypeStruct((8, D), F32),
        ],
        in_specs=[tile(D), tile(D)] + [VMEM_SPEC] * 4,
        out_specs=[tile(D), tile(D), tile(H), tile(H), tile(D), pl.BlockSpec((8, D), lambda i: (0, 0))],
        compiler_params=pltpu.CompilerParams(dimension_semantics=("arbitrary",), vmem_limit_bytes=VMEM_LIMIT),
    )(xhat1, target, mod, ln, w_mi, w_mo)


def _b1(dx1, xhat1, rstd1, x, mix, z, mod, ln, w_out, w_in, conv_w, w_pool, pool_scale, tm):
    T, D = x.shape
    ZW = w_in.shape[1]
    CC = ZW // 4
    nt = T // tm
    hb = tm // HALO

    def body(dx1_ref, xh1_ref, rstd_ref, x_ref, mix_ref, z_ref, zh_ref, mod_ref, ln_ref, wout_ref, win_ref, cw_ref, wp_ref, ps_ref,
             dx_ref, dmix_ref, ycat_ref, dz_ref, acc_ref, gcw_ref, gwp_ref, cv_s, vp_s, e_s, q_s):
        i = pl.program_id(0)
        j = nt - 1 - i

        @pl.when(i == 0)
        def _():
            acc_ref[...] = jnp.zeros((8, D), F32)
            gcw_ref[...] = jnp.zeros((8, CC), F32)
            gwp_ref[...] = jnp.zeros(gwp_ref.shape, F32)
            e_s[tm : tm + HALO, :] = jnp.zeros((HALO, CC), F32)
            q_s[tm : tm + HALO, :] = jnp.zeros((HALO, CC), F32)

        sh1, sc1, g1 = mod_ref[0:1, :], mod_ref[1:2, :], mod_ref[2:3, :]
        dx1 = dx1_ref[...]
        xhat1 = xh1_ref[...]
        acc_ref[0:1, :] += _colsum(dx1 * xhat1)
        acc_ref[1:2, :] += _colsum(dx1)
        dr1 = _ln_bwd(dx1 * ln_ref[0:1, :], xhat1, rstd_ref[...])
        acc_ref[4:5, :] += _colsum(dr1 * mix_ref[...])
        dmix = ((1.0 + g1) * dr1).astype(BF16)
        dmix_ref[...] = dmix
        dycat = lax.dot_general(dmix, wout_ref[...], NT, preferred_element_type=F32)

        z = z_ref[...].astype(F32)
        zh = zh_ref[...].astype(F32) * jnp.where(j > 0, 1.0, 0.0)
        gb, gc, vc, vp = z[:, 0:CC], z[:, CC : 2 * CC], z[:, 2 * CC : 3 * CC], z[:, 3 * CC : 4 * CC]
        cv = gc * vc
        cv_s[0:HALO, :] = zh[:, CC : 2 * CC] * zh[:, 2 * CC : 3 * CC]
        cv_s[HALO : HALO + tm, :] = cv
        vp_s[0:HALO, :] = zh[:, 3 * CC : 4 * CC]
        vp_s[HALO : HALO + tm, :] = vp
        cv_m2 = cv_s[HALO - 2 : HALO - 2 + tm, :]
        cv_m1 = cv_s[HALO - 1 : HALO - 1 + tm, :]
        w0, w1, w2 = cw_ref[0:1, :], cw_ref[1:2, :], cw_ref[2:3, :]
        conv = w0 * cv_m2 + w1 * cv_m1 + w2 * cv
        dyc = dycat[:, 0:CC]
        e = dyc * gb
        e_s[0:tm, :] = e
        dcv = w2 * e + w1 * e_s[1 : 1 + tm, :] + w0 * e_s[2 : 2 + tm, :]
        gcw_ref[0:1, :] += _colsum(e * cv_m2)
        gcw_ref[1:2, :] += _colsum(e * cv_m1)
        gcw_ref[2:3, :] += _colsum(e * cv)
        y_parts = [gb * conv]
        dz_parts = [dyc * conv, dcv * vc, dcv * gc]

        row = j * tm + lax.broadcasted_iota(jnp.int32, (tm, 1), 0)
        feats, inv_cnts = _pool_features(vp, vp_s, row, tm)
        gps_parts, dps = [], []
        for g in range(len(POOL_WINDOWS)):
            cols = slice(128 * g, 128 * g + 128)
            p = feats[g].astype(BF16)
            scale = ps_ref[0:1, cols]
            wp = wp_ref[g].astype(BF16)
            pw = jnp.dot(p, wp, preferred_element_type=F32)
            y_parts.append(pw * scale)
            dyp = dycat[:, CC + 128 * g : CC + 128 * g + 128]
            gps_parts.append(_colsum(dyp * pw))
            dpw = (dyp * scale).astype(BF16)
            gwp_ref[g] += lax.dot_general(p, dpw, TN, preferred_element_type=F32)
            dp = lax.dot_general(dpw, wp, NT, preferred_element_type=F32)
            q_s[0:tm, cols] = dp * inv_cnts[g]
            dps.append(dp)
        sq = _window_sums(q_s[...], tm, causal=False)
        dz_parts += [sq[g] - dps[g] for g in range(len(POOL_WINDOWS))]
        gcw_ref[3:4, :] += jnp.concatenate(gps_parts, axis=1)
        ycat_ref[...] = jnp.concatenate(y_parts, axis=1).astype(BF16)
        dz = jnp.concatenate(dz_parts, axis=1).astype(BF16)
        dz_ref[...] = dz
        dh = lax.dot_general(dz, win_ref[...], NT, preferred_element_type=F32)
        acc_ref[2:3, :] += _colsum(dh)
        acc_ref[3:4, :] += _colsum(dh * x_ref[...])
        dx_ref[...] = DEEPNORM_ALPHA * dr1 + dh * (1.0 + sc1)
        e_s[tm : tm + HALO, :] = e_s[0:HALO, :]
        q_s[tm : tm + HALO, :] = q_s[0:HALO, :]

    tile = lambda w: pl.BlockSpec((tm, w), lambda i: (nt - 1 - i, 0))
    halo = pl.BlockSpec((HALO, ZW), lambda i: (jnp.maximum((nt - 1 - i) * hb - 1, 0), 0))
    fixed = lambda shape: pl.BlockSpec(shape, lambda i: (0,) * len(shape))
    return pl.pallas_call(
        body,
        name="b1",
        grid=(nt,),
        out_shape=[
            jax.ShapeDtypeStruct((T, D), F32),
            jax.ShapeDtypeStruct((T, D), BF16),
            jax.ShapeDtypeStruct((T, D), BF16),
            jax.ShapeDtypeStruct((T, ZW), BF16),
            jax.ShapeDtypeStruct((8, D), F32),
            jax.ShapeDtypeStruct((8, CC), F32),
            jax.ShapeDtypeStruct(w_pool.shape, F32),
        ],
        in_specs=[tile(D), tile(D), tile(1), tile(D), tile(D), tile(ZW), halo] + [VMEM_SPEC] * 7,
        out_specs=[tile(D), tile(D), tile(D), tile(ZW), fixed((8, D)), fixed((8, CC)), fixed(w_pool.shape)],
        scratch_shapes=[
            pltpu.VMEM((HALO + tm, CC), F32),
            pltpu.VMEM((HALO + tm, CC), F32),
            pltpu.VMEM((tm + HALO, CC), F32),
            pltpu.VMEM((tm + HALO, CC), F32),
        ],
        compiler_params=pltpu.CompilerParams(dimension_semantics=("arbitrary",), vmem_limit_bytes=VMEM_LIMIT),
    )(dx1, xhat1, rstd1, x, mix, z, z, mod, ln, w_out, w_in, conv_w, w_pool, pool_scale)


def _wgrad(a, b, bk, n_groups, bt, name, owners=None, scatter=None):
    T, K = a.shape
    N = b.shape[1]
    nk, nt, ng = K // bk, T // bt, N // n_groups
    nc = min(512, ng)
    ns = 0 if scatter is None else 1
    n_steps = nk * n_groups * nt
    mid_step = min(1, n_steps - 1)

    def body(*refs):
        a_ref, b_ref = refs[0], refs[1]
        o_ref = refs[2 + ns]
        acc = refs[3 + 2 * ns]
        if ns:
            s_hbm, s_recv, s_scr = refs[2], refs[4], refs[6:]
        kk, gg, t = pl.program_id(0), pl.program_id(1), pl.program_id(2)
        step = (kk * n_groups + gg) * nt + t

        if ns:

            @pl.when(step == 0)
            def _():
                _scatter_start(s_hbm, s_scr)

            @pl.when(step == mid_step)
            def _():
                _scatter_middle(s_hbm, s_recv, s_scr)

        @pl.when(t == 0)
        def _():
            acc[...] = jnp.zeros(acc.shape, F32)

        at = a_ref[...].T
        for c in range(ng // nc):
            cs = slice(c * nc, (c + 1) * nc)
            acc[:, cs] += jnp.dot(at, b_ref[:, cs], preferred_element_type=F32)

        @pl.when(t == nt - 1)
        def _():
            if owners is None:
                o_ref[...] = acc[...].astype(BF16)
            else:
                per = N // owners
                for o in range(ng // per):
                    o_ref[o] = acc[:, o * per : (o + 1) * per].astype(BF16)

        if ns:

            @pl.when(step == n_steps - 1)
            def _():
                _scatter_finish(s_hbm, s_recv, s_scr)

    if owners is None:
        out_shape = [jax.ShapeDtypeStruct((K, N), BF16)]
        out_specs = [pl.BlockSpec((bk, ng), lambda k, g, t: (k, g))]
    else:
        assert bk == K
        per = N // owners
        out_shape = [jax.ShapeDtypeStruct((owners, K, per), BF16)]
        out_specs = [pl.BlockSpec((ng // per, K, per), lambda k, g, t: (g, 0, 0))]
    ins, in_specs = [a, b], [pl.BlockSpec((bt, bk), lambda k, g, t: (t, k)), pl.BlockSpec((bt, ng), lambda k, g, t: (t, g))]
    scratch = [pltpu.VMEM((bk, ng), F32)]
    if ns:
        ins.append(scatter)
        in_specs.append(ANY_SPEC)
        out_shape.append(_scatter_out_shape(scatter))
        out_specs.append(ANY_SPEC)
        scratch += _scatter_scratch(*scatter.shape[1:])
    outs = pl.pallas_call(
        body,
        name=name,
        grid=(nk, n_groups, nt),
        out_shape=out_shape,
        in_specs=in_specs,
        out_specs=out_specs,
        scratch_shapes=scratch,
        compiler_params=pltpu.CompilerParams(dimension_semantics=("arbitrary", "arbitrary", "arbitrary"), vmem_limit_bytes=VMEM_LIMIT),
    )(*ins)
    return outs if ns else outs[0]


def _small_grads(acc1_t, acc2_t, gcw_t, gwp_all, cond_t, my_slot, w_cols):
    D = acc1_t.shape[2]
    n_chunk = D // 128
    q_mine = w_cols // 128

    def total(ref, r):
        s = ref[r, 0:1, :]
        for k in range(1, N_DEV):
            s = s + ref[r, k : k + 1, :]
        return s

    def body(slot_ref, a1_ref, a2_ref, gcw_ref, gwp_ref, ct_ref, gb_ref, gw_ref, gln_ref, gcwo_ref, gwpo_ref, loss_ref, dm_s):
        loss_ref[...] = total(a2_ref, 5)
        for s, (ref, r) in enumerate([(a1_ref, 2), (a1_ref, 3), (a1_ref, 4), (a2_ref, 2), (a2_ref, 3), (a2_ref, 4)]):
            gb_ref[0:1, s * D : (s + 1) * D] = total(ref, r)
            for qq in range(n_chunk):
                dm_s[s * n_chunk + qq] = ref[r, :, 128 * qq : 128 * qq + 128]
        gln_ref[0:1, :] = total(a1_ref, 0)
        gln_ref[1:2, :] = total(a1_ref, 1)
        gln_ref[2:3, :] = total(a2_ref, 0)
        gln_ref[3:4, :] = total(a2_ref, 1)
        gcwo_ref[...] = jnp.zeros(gcwo_ref.shape, F32)
        for r in range(4):
            gcwo_ref[r : r + 1, :] = total(gcw_ref, r)
        wp = gwp_ref[0]
        for k in range(1, N_DEV):
            wp = wp + gwp_ref[k]
        gwpo_ref[0] = wp
        ct = ct_ref[...]
        cond_t = ct * jax.nn.sigmoid(ct)
        q0 = slot_ref[0] * q_mine
        for q in range(q_mine):
            dm = dm_s[q0 + q]
            out = cond_t[:, 0:1] * dm[0:1, :]
            for k in range(1, N_DEV):
                out = out + cond_t[:, k : k + 1] * dm[k : k + 1, :]
            gw_ref[0, :, 128 * q : 128 * q + 128] = out

    CC = gcw_t.shape[2]
    return pl.pallas_call(
        body,
        name="small_grads",
        out_shape=[
            jax.ShapeDtypeStruct((1, 6 * D), F32),
            jax.ShapeDtypeStruct((1, D, w_cols), F32),
            jax.ShapeDtypeStruct((4, D), F32),
            jax.ShapeDtypeStruct((8, CC), F32),
            jax.ShapeDtypeStruct((1, *gwp_all.shape[1:]), F32),
            jax.ShapeDtypeStruct((1, D), F32),
        ],
        in_specs=[pl.BlockSpec(memory_space=pltpu.SMEM)] + [VMEM_SPEC] * 5,
        out_specs=[VMEM_SPEC] * 6,
        scratch_shapes=[pltpu.VMEM((6 * n_chunk, N_DEV, 128), F32)],
        compiler_params=pltpu.CompilerParams(vmem_limit_bytes=VMEM_LIMIT),
    )(my_slot, acc1_t, acc2_t, gcw_t, gwp_all, cond_t)


def kernel(x, c, w_ada, b_ada, w_in, conv_w, w_pool, pool_scale, w_out, ln1_g, ln1_b, w_mlp_in, w_mlp_out, ln2_g, ln2_b, loss_target, m_w_ada, m_b_ada, m_w_in, m_conv_w, m_w_pool, m_pool_scale, m_w_out, m_ln1_g, m_ln1_b, m_w_mlp_in, m_w_mlp_out, m_ln2_g, m_ln2_b, v_w_ada, v_b_ada, v_w_in, v_conv_w, v_w_pool, v_pool_scale, v_w_out, v_ln1_g, v_ln1_b, v_w_mlp_in, v_w_mlp_out, v_ln2_g, v_ln2_b):
    T, D = x.shape[1], x.shape[2]
    H = w_mlp_out.shape[1] * N_DEV
    ZW = w_in.shape[2] * N_DEV
    CC = ZW // 4
    tm = min(512, T // 2)
    bt = min(1024, T)
    ax, ay, ac = _my_place()
    me = _slot(ax, ay, ac)

    w_cols = w_ada.shape[2]
    b_mine = lax.dynamic_slice(b_ada, (0, me * w_cols), (1, w_cols))
    w_in_f, w_out_g, cw_g, c_g, mod_g = _prologue(w_in, w_out, conv_w[0], c, w_ada, b_mine)
    w_out_f = w_out_g.reshape(D, D)
    conv_w_f = jnp.transpose(cw_g, (1, 0, 2)).reshape(conv_w.shape[1], CC)
    c_all = c_g.reshape(N_DEV, D)
    mod = lax.dynamic_index_in_dim(mod_g, me, axis=1, keepdims=False).reshape(6, D)

    ln = jnp.concatenate([ln1_g, ln1_b, ln2_g, ln2_b], axis=0)
    xs, target = x[0], loss_target[0]

    z, h, xhat1, rstd1, mix, w_mi_f, w_mo_g = _f1(
        xs, mod, w_in_f, conv_w_f, w_pool[0], pool_scale, w_out_f, tm,
        [w_mlp_in, w_mlp_out], [True, False])
    dx1, h2, a, du, df, acc2 = _fb2(xhat1, target, mod, ln, w_mi_f, w_mo_g.reshape(H, D), tm // 2)

    grad_x, dmix, ycat, dz, acc1, gcw, gwp = _b1(
        dx1, xhat1, rstd1, xs, mix, z, mod, ln, w_out_f, w_in_f, conv_w_f, w_pool[0], pool_scale, tm)

    gp_mo = _wgrad(a, df, D, 1, bt, "wgrad_mlp_out").reshape(N_DEV, H // N_DEV, D)
    gp_mi, rv_mo = _wgrad(h2, du, D, 2, bt, "wgrad_mlp_in", owners=N_DEV, scatter=gp_mo)
    gp_in, rv_mi = _wgrad(h, dz, D, 1, bt, "wgrad_in", owners=N_DEV, scatter=gp_mi)
    gp_out, rv_in = _wgrad(ycat, dmix, D, 1, bt, "wgrad_out", scatter=gp_in)
    gp_out = gp_out.reshape(N_DEV, D // N_DEV, D)

    acc1_g, acc2_g, gcw_g, gwp_g, rv_out = _allgather([acc1, acc2, gcw, gwp], "gather_small", in_vmem=True, scatter=gp_out)
    g_b_ada, g_w_ada, g_ln, g_cw, g_w_pool, loss_row = _small_grads(
        jnp.transpose(acc1_g, (1, 0, 2)), jnp.transpose(acc2_g, (1, 0, 2)), jnp.transpose(gcw_g, (1, 0, 2)), gwp_g,
        c_all.T, jnp.reshape(me, (1,)).astype(jnp.int32), w_cols)
    cc_mine = conv_w.shape[2]
    g_conv_w = lax.dynamic_slice(g_cw, (0, me * cc_mine), (conv_w.shape[1], cc_mine))[None]
    g_pool_scale = g_cw[3:4, :]
    g_ln1_g, g_ln1_b, g_ln2_g, g_ln2_b = g_ln[0:1], g_ln[1:2], g_ln[2:3], g_ln[3:4]

    small = _adamw_multi(
        [
            (b_ada, g_b_ada, m_b_ada, v_b_ada),
            (conv_w, g_conv_w, m_conv_w, v_conv_w),
            (w_pool, g_w_pool, m_w_pool, v_w_pool),
            (pool_scale, g_pool_scale, m_pool_scale, v_pool_scale),
            (ln1_g, g_ln1_g, m_ln1_g, v_ln1_g),
            (ln1_b, g_ln1_b, m_ln1_b, v_ln1_b),
            (ln2_g, g_ln2_g, m_ln2_g, v_ln2_g),
            (ln2_b, g_ln2_b, m_ln2_b, v_ln2_b),
        ],
        "adamw_small")
    u_b_ada, u_conv_w, u_w_pool, u_pool_scale, u_ln1_g, u_ln1_b, u_ln2_g, u_ln2_b = small

    g_w_ada, *u_w_ada = _sum_adamw(g_w_ada, w_ada, m_w_ada, v_w_ada, "adamw_w_ada")
    g_w_mo, *u_w_mo = _sum_adamw(rv_mo, w_mlp_out, m_w_mlp_out, v_w_mlp_out, "sum_w_mlp_out")
    g_w_mi, *u_w_mi = _sum_adamw(rv_mi, w_mlp_in, m_w_mlp_in, v_w_mlp_in, "sum_w_mlp_in")
    g_w_in, *u_w_in = _sum_adamw(rv_in, w_in, m_w_in, v_w_in, "sum_w_in")
    g_w_out, *u_w_out = _sum_adamw(rv_out, w_out, m_w_out, v_w_out, "sum_w_out")

    grads = [g_w_ada, g_b_ada, g_w_in, g_conv_w, g_w_pool, g_pool_scale, g_w_out, g_ln1_g, g_ln1_b, g_w_mi, g_w_mo, g_ln2_g, g_ln2_b]
    updates = [u_w_ada, u_b_ada, u_w_in, u_conv_w, u_w_pool, u_pool_scale, u_w_out, u_ln1_g, u_ln1_b, u_w_mi, u_w_mo, u_ln2_g, u_ln2_b]
    deltas = [u[0] for u in updates]
    new_m = [u[1] for u in updates]
    new_v = [u[2] for u in updates]
    return (loss_row[0, 0], grad_x[None], *grads, *deltas, *new_m, *new_v)
```

```python
import jax
import jax.numpy as jnp
from jax import lax
from jax.experimental import pallas as pl
from jax.experimental.pallas import tpu as pltpu

F32 = jnp.float32
BF16 = jnp.bfloat16
MESH = pl.DeviceIdType.MESH
N_DEV = 8

LN_EPS = 1e-5
DEPTH = 1
DEEPNORM_ALPHA = (2.0 * DEPTH) ** 0.25
POOL_WINDOWS = (2, 4, 8, 16)
HALO = 16

ADAM_LR = 0.001
ADAM_B1 = 0.9
ADAM_B2 = 0.999
ADAM_EPS = 1e-08
ADAM_WD = 0.01
ADAM_STEP = 10

VMEM_LIMIT = 60 * 1024 * 1024

VMEM_SPEC = pl.BlockSpec(memory_space=pltpu.VMEM)
ANY_SPEC = pl.BlockSpec(memory_space=pl.ANY)

NT = (((1,), (1,)), ((), ()))
TN = (((0,), (0,)), ((), ()))


def _my_place():
    return lax.axis_index("x"), lax.axis_index("y"), lax.axis_index("c")


def _slot(x, y, c):
    return 4 * x + 2 * y + c


def _gather_place(ins, outs, a, slot):
    if len(outs[a].shape) == len(ins[a].shape):
        wb = ins[a].shape[1]
        return outs[a].at[:, pl.ds(pl.multiple_of(slot * wb, wb), wb)]
    return outs[a].at[slot]


def _gather_copy(ins, outs, sems, a, k, block, to, from_shard=False):
    send_sems, recv_sems, _ = sems
    dst = _gather_place(ins, outs, a, _slot(*block))
    return pltpu.make_async_remote_copy(
        src_ref=ins[a] if from_shard else dst,
        dst_ref=dst,
        send_sem=send_sems.at[7 * a + k],
        recv_sem=recv_sems.at[7 * a + k],
        device_id=to,
        device_id_type=MESH,
    )


def _gather_peers():
    x, y, c = _my_place()
    return (x, y, c), (x, y, 1 - c), [(1 - x, y), (x, 1 - y), (1 - x, 1 - y)]


def _gather_first(ins, outs, sems):
    me, sibling, chips = _gather_peers()
    first = []
    for a in range(len(ins)):
        first.append(_gather_copy(ins, outs, sems, a, 0, me, sibling, from_shard=True))
        first += [_gather_copy(ins, outs, sems, a, 1 + j, me, (*chip, me[2]), from_shard=True) for j, chip in enumerate(chips)]
    return first


def _gather_mine(ins, outs, sems, a):
    me, _, _ = _gather_peers()
    return pltpu.make_async_copy(ins[a], _gather_place(ins, outs, a, _slot(*me)), sems[2].at[a])


def _gather_start(ins, outs, sems):
    for a in range(len(ins)):
        _gather_mine(ins, outs, sems, a).start()
    for cp in _gather_first(ins, outs, sems):
        cp.start()


def _gather_forward(ins, outs, sems, j):
    me, sibling, chips = _gather_peers()
    for a in range(len(ins)):
        _gather_copy(ins, outs, sems, a, 1 + j, (*chips[j], me[2]), me).wait_recv()
        _gather_copy(ins, outs, sems, a, 4 + j, (*chips[j], me[2]), sibling).start()


def _gather_finish(ins, outs, sems):
    me, sibling, chips = _gather_peers()
    for a in range(len(ins)):
        _gather_copy(ins, outs, sems, a, 0, sibling, me).wait_recv()
        for j, chip in enumerate(chips):
            _gather_copy(ins, outs, sems, a, 4 + j, (*chip, 1 - me[2]), me).wait_recv()
    for cp in _gather_first(ins, outs, sems):
        cp.wait_send()
    for a in range(len(ins)):
        for j, chip in enumerate(chips):
            _gather_copy(ins, outs, sems, a, 4 + j, (*chip, me[2]), sibling).wait_send()
        _gather_mine(ins, outs, sems, a).wait()


def _gather_scratch(n):
    return [pltpu.SemaphoreType.DMA((7 * n,)), pltpu.SemaphoreType.DMA((7 * n,)), pltpu.SemaphoreType.DMA((n,))]


def _gather_out_shape(shards, by_cols):
    return [
        jax.ShapeDtypeStruct((s.shape[0], N_DEV * s.shape[1]) if cols else (N_DEV, *s.shape), s.dtype)
        for s, cols in zip(shards, by_cols)
    ]


N_CHIP = 4


def _scatter_scratch(rows, cols):
    block = pltpu.VMEM((N_CHIP, rows, cols), BF16)
    dma = pltpu.SemaphoreType.DMA
    return [block, block, block, dma((N_CHIP,)), dma((N_CHIP,)), dma((N_CHIP,)), dma((N_CHIP - 1,)), dma((N_CHIP - 1,)), dma]


def _scatter_pair_copies(g_hbm, scr):
    x, y, c = _my_place()
    mine, theirs, _, a_send, a_recv, load_sem = scr[:6]
    to_sibling = [
        pltpu.make_async_remote_copy(
            src_ref=g_hbm.at[2 * q + (1 - c)], dst_ref=theirs.at[q], send_sem=a_send.at[q], recv_sem=a_recv.at[q],
            device_id=(x, y, 1 - c), device_id_type=MESH)
        for q in range(N_CHIP)
    ]
    loads = [pltpu.make_async_copy(g_hbm.at[2 * q + c], mine.at[q], load_sem.at[q]) for q in range(N_CHIP)]
    return to_sibling, loads


def _scatter_sum_copies(recv, scr):
    x, y, c = _my_place()
    sums, b_send, b_recv, own_sem = scr[2], scr[6], scr[7], scr[8]
    q_me = 2 * x + y
    to_owner = [
        pltpu.make_async_remote_copy(
            src_ref=sums.at[2 * px + py], dst_ref=recv.at[q_me], send_sem=b_send.at[j], recv_sem=b_recv.at[j],
            device_id=(px, py, c), device_id_type=MESH)
        for j, (px, py) in enumerate([(1 - x, y), (x, 1 - y), (1 - x, 1 - y)])
    ]
    return to_owner, pltpu.make_async_copy(sums.at[q_me], recv.at[q_me], own_sem)


def _scatter_start(g_hbm, scr):
    to_sibling, loads = _scatter_pair_copies(g_hbm, scr)
    for cp in to_sibling + loads:
        cp.start()


def _scatter_middle(g_hbm, recv, scr):
    to_sibling, loads = _scatter_pair_copies(g_hbm, scr)
    for cp in to_sibling:
        cp.wait_recv()
    for cp in loads:
        cp.wait()
    mine, theirs, sums = scr[:3]

    def step(r, carry):
        rs = pl.ds(pl.multiple_of(r * ROW_CHUNK, ROW_CHUNK), ROW_CHUNK)
        for q in range(N_CHIP):
            sums[q, rs, :] = (mine[q, rs, :].astype(F32) + theirs[q, rs, :].astype(F32)).astype(BF16)
        return carry

    lax.fori_loop(0, mine.shape[1] // ROW_CHUNK, step, 0)
    to_owner, own = _scatter_sum_copies(recv, scr)
    for cp in to_owner + [own]:
        cp.start()


def _scatter_finish(g_hbm, recv, scr):
    to_sibling, _ = _scatter_pair_copies(g_hbm, scr)
    to_owner, own = _scatter_sum_copies(recv, scr)
    for cp in to_owner:
        cp.wait_recv()
    for cp in to_sibling + to_owner:
        cp.wait_send()
    own.wait()


def _scatter_out_shape(gparts):
    return jax.ShapeDtypeStruct((N_CHIP, *gparts.shape[1:]), gparts.dtype)


def _adamw_math(w, g, m, v):
    m = ADAM_B1 * m + (1.0 - ADAM_B1) * g
    v = ADAM_B2 * v + (1.0 - ADAM_B2) * (g * g)
    m_hat = m / (1.0 - ADAM_B1**ADAM_STEP)
    v_hat = v / (1.0 - ADAM_B2**ADAM_STEP)
    delta = -ADAM_LR * (m_hat / (jnp.sqrt(v_hat) + ADAM_EPS) + ADAM_WD * w)
    return delta, m, v


ROW_CHUNK = 64


ELEMS_PER_STEP = 128 * 1024


def _allgather(shards, name, in_vmem, scatter=None, by_cols=None):
    n = len(shards)
    by_cols = by_cols or [False] * n
    ns = 0 if scatter is None else 1

    def body(*refs):
        ins, outs = refs[:n], refs[n + ns : 2 * n + ns]
        rest = refs[2 * n + ns :]
        if ns:
            g_hbm, recv, sems, scr = refs[n], rest[0], rest[1:4], rest[4:]
            _scatter_start(g_hbm, scr)
        else:
            sems = rest[0:3]
        _gather_start(ins, outs, sems)
        if ns:
            _scatter_middle(g_hbm, recv, scr)
        for j in range(3):
            _gather_forward(ins, outs, sems, j)
        _gather_finish(ins, outs, sems)
        if ns:
            _scatter_finish(g_hbm, recv, scr)

    spec = VMEM_SPEC if in_vmem else ANY_SPEC
    extra_in, extra_out, extra_spec, extra_scratch = [], [], [], []
    if ns:
        extra_in, extra_spec = [scatter], [ANY_SPEC]
        extra_out = [_scatter_out_shape(scatter)]
        extra_scratch = _scatter_scratch(*scatter.shape[1:])
    return pl.pallas_call(
        body,
        name=name,
        out_shape=_gather_out_shape(shards, by_cols) + extra_out,
        in_specs=[spec] * n + extra_spec,
        out_specs=[spec] * n + extra_spec,
        scratch_shapes=_gather_scratch(n) + extra_scratch,
        compiler_params=pltpu.CompilerParams(vmem_limit_bytes=VMEM_LIMIT),
    )(*shards, *extra_in)


def _sum_adamw(parts, w, m, v, name):
    _, rows, cols = w.shape
    rb = rows
    while rb * cols > ELEMS_PER_STEP and rb % 16 == 0:
        rb //= 2

    def body(p_ref, w_ref, m_ref, v_ref, grad_ref, delta_ref, nm_ref, nv_ref):
        g = p_ref[0].astype(F32)
        for k in range(1, p_ref.shape[0]):
            g = g + p_ref[k].astype(F32)
        delta, nm, nv = _adamw_math(w_ref[0], g, m_ref[0], v_ref[0])
        grad_ref[0] = g
        delta_ref[0] = delta
        nm_ref[0] = nm
        nv_ref[0] = nv

    block = lambda lead: pl.BlockSpec((lead, rb, cols), lambda i: (0, i, 0))
    out = jax.ShapeDtypeStruct(w.shape, F32)
    return pl.pallas_call(
        body,
        name=name,
        grid=(rows // rb,),
        out_shape=[out] * 4,
        in_specs=[block(parts.shape[0])] + [block(1)] * 3,
        out_specs=[block(1)] * 4,
        compiler_params=pltpu.CompilerParams(dimension_semantics=("arbitrary",), vmem_limit_bytes=VMEM_LIMIT),
    )(parts, w, m, v)


def _adamw_multi(items, name):
    n = len(items)

    def body(*refs):
        ins, outs = refs[: 4 * n], refs[4 * n :]
        for a in range(n):
            w_ref, g_ref, m_ref, v_ref = ins[4 * a : 4 * a + 4]
            d_ref, nm_ref, nv_ref = outs[3 * a : 3 * a + 3]
            delta, nm, nv = _adamw_math(w_ref[...], g_ref[...], m_ref[...], v_ref[...])
            d_ref[...] = delta
            nm_ref[...] = nm
            nv_ref[...] = nv

    flat = [a for it in items for a in it]
    out_shape = [jax.ShapeDtypeStruct(it[0].shape, F32) for it in items for _ in range(3)]
    outs = pl.pallas_call(
        body,
        name=name,
        out_shape=out_shape,
        in_specs=[VMEM_SPEC] * (4 * n),
        out_specs=[VMEM_SPEC] * (3 * n),
        compiler_params=pltpu.CompilerParams(vmem_limit_bytes=VMEM_LIMIT),
    )(*flat)
    return [tuple(outs[3 * a : 3 * a + 3]) for a in range(n)]


def _prologue(w_in, w_out, conv_w, c, w_ada, b_mine):
    D = c.shape[1]
    wc = w_ada.shape[2]
    shards16 = [jax.ShapeDtypeStruct(w_in.shape[1:], BF16), jax.ShapeDtypeStruct(w_out.shape[1:], BF16)]

    def to_all(src, out, sems):
        x, y, c = _my_place()
        me = _slot(x, y, c)
        copies = [
            pltpu.make_async_remote_copy(
                src_ref=src, dst_ref=out.at[me], send_sem=sems[0].at[k - 1], recv_sem=sems[1].at[k - 1],
                device_id=(x ^ (k >> 2), y ^ ((k >> 1) & 1), c ^ (k & 1)), device_id_type=MESH)
            for k in range(1, N_DEV)
        ]
        return copies, pltpu.make_async_copy(src, out.at[me], sems[2].at[0])

    def start(copies, own):
        for cp in copies + [own]:
            cp.start()

    def finish(copies, own):
        for cp in copies:
            cp.wait()
        own.wait()

    def body(win_ref, wout_ref, cw_ref, c_ref, wada_ref, b_ref, win_g, wout_g, cw_g, c_g, mod_g, win16, wout16, c_s, mp_s, *sems):
        c_copies = to_all(c_ref, c_g, sems[3:6])
        start(*c_copies)
        win16[...] = win_ref[0].astype(BF16)
        wout16[...] = wout_ref[0].astype(BF16)
        w_ins, w_outs, w_sems = (win16, wout16, cw_ref), (win_g, wout_g, cw_g), sems[0:3]
        _gather_start(w_ins, w_outs, w_sems)
        finish(*c_copies)
        for k in range(N_DEV):
            c_s[k : k + 1, :] = c_g[k]
        cv = c_s[...]
        cond = cv * jax.nn.sigmoid(cv)
        mp_s[...] = jnp.dot(cond, wada_ref[0], precision=lax.Precision.HIGHEST, preferred_element_type=F32) + b_ref[...]
        m_copies = to_all(mp_s, mod_g, sems[6:9])
        start(*m_copies)
        for j in range(3):
            _gather_forward(w_ins, w_outs, w_sems, j)
        _gather_finish(w_ins, w_outs, w_sems)
        finish(*m_copies)

    return pl.pallas_call(
        body,
        name="prologue",
        out_shape=_gather_out_shape(shards16 + [conv_w], [True, False, False])
        + [jax.ShapeDtypeStruct((N_DEV, 1, D), F32), jax.ShapeDtypeStruct((N_DEV, N_DEV, wc), F32)],
        in_specs=[VMEM_SPEC, VMEM_SPEC, ANY_SPEC] + [VMEM_SPEC] * 3,
        out_specs=[ANY_SPEC] * 3 + [VMEM_SPEC] * 2,
        scratch_shapes=[pltpu.VMEM(s.shape, BF16) for s in shards16]
        + [pltpu.VMEM((N_DEV, D), F32), pltpu.VMEM((N_DEV, wc), F32)]
        + _gather_scratch(3) + _gather_scratch(1) + _gather_scratch(1),
        compiler_params=pltpu.CompilerParams(vmem_limit_bytes=VMEM_LIMIT),
    )(w_in, w_out, conv_w, c, w_ada, b_mine)


def _ln_fwd(r):
    mu = jnp.mean(r, axis=-1, keepdims=True)
    d = r - mu
    var = jnp.mean(d * d, axis=-1, keepdims=True)
    rstd = lax.rsqrt(var + LN_EPS)
    return d * rstd, rstd


def _ln_bwd(dxh, xhat, rstd):
    m1 = jnp.mean(dxh, axis=-1, keepdims=True)
    m2 = jnp.mean(dxh * xhat, axis=-1, keepdims=True)
    return rstd * (dxh - m1 - xhat * m2)


def _colsum(a):
    return jnp.sum(a, axis=0, keepdims=True)


def _window_sums(ext, tm, causal):
    n = ext.shape[0]
    lo = HALO if causal else 0
    s, out = ext, []
    for p in range(len(POOL_WINDOWS)):
        assert POOL_WINDOWS[p] == 2 ** (p + 1)
        k = 2**p
        s = s + pltpu.roll(s, k if causal else n - k, 0)
        out.append(s[lo : lo + tm, 0:128])
        if p + 1 < len(POOL_WINDOWS):
            s = s[:, 128:]
    return out


def _pool_features(vp, vp_s, row, tm):
    sums = _window_sums(vp_s[...], tm, causal=True)
    feats, inv_cnts = [], []
    for g, win in enumerate(POOL_WINDOWS):
        inv_cnt = 1.0 / jnp.minimum(row + 1, win).astype(F32)
        feats.append(sums[g] * inv_cnt - vp[:, 128 * g : 128 * g + 128])
        inv_cnts.append(inv_cnt)
    return feats, inv_cnts


def _f1(x, mod, w_in, conv_w, w_pool, pool_scale, w_out, tm, gather, by_cols):
    T, D = x.shape
    ZW = w_in.shape[1]
    CC = ZW // 4
    nt = T // tm
    ng = len(gather)
    fwd_steps = [max(nt - 3 + j, 0) for j in range(3)]

    shards16 = [jax.ShapeDtypeStruct(s.shape[1:], BF16) for s in gather]

    def body(*refs):
        x_ref, mod_ref, win_ref, cw_ref, wp_ref, ps_ref, wout_ref = refs[:7]
        g_f32 = refs[7 : 7 + ng]
        z_ref, h_ref, xhat_ref, rstd_ref, mix_ref = refs[7 + ng : 12 + ng]
        g_outs = refs[12 + ng : 12 + 2 * ng]
        cv_s, vp_s = refs[12 + 2 * ng : 14 + 2 * ng]
        g_ins = refs[14 + 2 * ng : 14 + 3 * ng]
        g_sems = refs[14 + 3 * ng :]
        i = pl.program_id(0)

        @pl.when(i == 0)
        def _():
            for src, dst in zip(g_f32, g_ins):
                dst[...] = src[0].astype(BF16)
            _gather_start(g_ins, g_outs, g_sems)
            cv_s[0:HALO, :] = jnp.zeros((HALO, CC), F32)
            vp_s[0:HALO, :] = jnp.zeros((HALO, CC), F32)

        xv = x_ref[...]
        sh1, sc1, g1 = mod_ref[0:1, :], mod_ref[1:2, :], mod_ref[2:3, :]
        h = (xv * (1.0 + sc1) + sh1).astype(BF16)
        h_ref[...] = h
        z = jnp.dot(h, win_ref[...], preferred_element_type=F32)
        z_ref[...] = z.astype(BF16)
        gb, gc, vc, vp = z[:, 0:CC], z[:, CC : 2 * CC], z[:, 2 * CC : 3 * CC], z[:, 3 * CC : 4 * CC]
        cv = gc * vc
        cv_s[HALO : HALO + tm, :] = cv
        vp_s[HALO : HALO + tm, :] = vp
        conv = cw_ref[0:1, :] * cv_s[HALO - 2 : HALO - 2 + tm, :] + cw_ref[1:2, :] * cv_s[HALO - 1 : HALO - 1 + tm, :] + cw_ref[2:3, :] * cv
        parts = [gb * conv]
        row = i * tm + lax.broadcasted_iota(jnp.int32, (tm, 1), 0)
        feats, _ = _pool_features(vp, vp_s, row, tm)
        for g in range(len(POOL_WINDOWS)):
            pw = jnp.dot(feats[g].astype(BF16), wp_ref[g].astype(BF16), preferred_element_type=F32)
            parts.append(pw * ps_ref[0:1, 128 * g : 128 * g + 128])
        cv_s[0:HALO, :] = cv_s[tm : tm + HALO, :]
        vp_s[0:HALO, :] = vp_s[tm : tm + HALO, :]
        ycat = jnp.concatenate(parts, axis=1).astype(BF16)
        mix = jnp.dot(ycat, wout_ref[...], preferred_element_type=F32)
        mix_ref[...] = mix
        xhat, rstd = _ln_fwd(DEEPNORM_ALPHA * xv + (1.0 + g1) * mix)
        xhat_ref[...] = xhat
        rstd_ref[...] = rstd

        for j in range(3):

            @pl.when(i == fwd_steps[j])
            def _(j=j):
                _gather_forward(g_ins, g_outs, g_sems, j)

        @pl.when(i == nt - 1)
        def _():
            _gather_finish(g_ins, g_outs, g_sems)

    tile = lambda w: pl.BlockSpec((tm, w), lambda i: (i, 0))
    return pl.pallas_call(
        body,
        name="f1",
        grid=(nt,),
        out_shape=[
            jax.ShapeDtypeStruct((T, ZW), BF16),
            jax.ShapeDtypeStruct((T, D), BF16),
            jax.ShapeDtypeStruct((T, D), F32),
            jax.ShapeDtypeStruct((T, 1), F32),
            jax.ShapeDtypeStruct((T, D), F32),
        ]
        + _gather_out_shape(shards16, by_cols),
        in_specs=[tile(D)] + [VMEM_SPEC] * (6 + ng),
        out_specs=[tile(ZW), tile(D), tile(D), tile(1), tile(D)] + [ANY_SPEC] * ng,
        scratch_shapes=[pltpu.VMEM((HALO + tm, CC), F32), pltpu.VMEM((HALO + tm, CC), F32)]
        + [pltpu.VMEM(s.shape, BF16) for s in shards16]
        + _gather_scratch(ng),
        compiler_params=pltpu.CompilerParams(dimension_semantics=("arbitrary",), vmem_limit_bytes=VMEM_LIMIT),
    )(x, mod, w_in, conv_w, w_pool, pool_scale, w_out, *gather)


def _fb2(xhat1, target, mod, ln, w_mi, w_mo, tm):
    T, D = xhat1.shape
    H = w_mi.shape[1]
    hc = min(1024, H)
    nb = H // hc
    nt = T // tm

    def body(xh1_ref, t_ref, mod_ref, ln_ref, wmi_ref, wmo_ref, dx1_ref, h2_ref, a_ref, du_ref, df_ref, acc_ref):
        i = pl.program_id(0)

        @pl.when(i == 0)
        def _():
            acc_ref[...] = jnp.zeros((8, D), F32)

        sh2, sc2, g2 = mod_ref[3:4, :], mod_ref[4:5, :], mod_ref[5:6, :]
        x1 = xh1_ref[...] * ln_ref[0:1, :] + ln_ref[1:2, :]
        h2 = (x1 * (1.0 + sc2) + sh2).astype(BF16)
        h2_ref[...] = h2
        f = jnp.zeros((tm, D), F32)
        for k in range(nb):
            ks = slice(k * hc, (k + 1) * hc)
            r = jnp.maximum(jnp.dot(h2, wmi_ref[:, ks], preferred_element_type=F32), 0.0)
            du_ref[:, ks] = r.astype(BF16)
            a = (r * r).astype(BF16)
            a_ref[:, ks] = a
            f = f + jnp.dot(a, wmo_ref[ks, :], preferred_element_type=F32)
        xhat2, rstd2 = _ln_fwd(DEEPNORM_ALPHA * x1 + (1.0 + g2) * f)
        ln2_g = ln_ref[2:3, :]
        d = xhat2 * ln2_g + ln_ref[3:4, :] - t_ref[...]
        dr2 = _ln_bwd(d * (ln2_g * (1.0 / D)), xhat2, rstd2)
        df = ((1.0 + g2) * dr2).astype(BF16)
        df_ref[...] = df
        dh2 = jnp.zeros((tm, D), F32)
        for k in range(nb):
            ks = slice(k * hc, (k + 1) * hc)
            da = lax.dot_general(df, wmo_ref[ks, :], NT, preferred_element_type=F32)
            du = (da * (2.0 * du_ref[:, ks].astype(F32))).astype(BF16)
            du_ref[:, ks] = du
            dh2 = dh2 + lax.dot_general(du, wmi_ref[:, ks], NT, preferred_element_type=F32)
        dx1_ref[...] = DEEPNORM_ALPHA * dr2 + dh2 * (1.0 + sc2)
        acc_ref[0:1, :] += _colsum(d * xhat2) * (1.0 / D)
        acc_ref[1:2, :] += _colsum(d) * (1.0 / D)
        acc_ref[2:3, :] += _colsum(dh2)
        acc_ref[3:4, :] += _colsum(dh2 * x1)
        acc_ref[4:5, :] += _colsum(dr2 * f)
        acc_ref[5:6, :] += jnp.zeros((1, D), F32) + (0.5 / D) * jnp.sum(d * d)

    tile = lambda w: pl.BlockSpec((tm, w), lambda i: (i, 0))
    return pl.pallas_call(
        body,
        name="fb2",
        grid=(nt,),
        out_shape=[
            jax.ShapeDtypeStruct((T, D), F32),
            jax.ShapeDtypeStruct((T, D), BF16),
            jax.ShapeDtypeStruct((T, H), BF16),
            jax.ShapeDtypeStruct((T, H), BF16),
            jax.ShapeDtypeStruct((T, D), BF16),
            jax.ShapeDtypeStruct((8, D), F32),
        ],
        in_specs=[tile(D), tile(D)] + [VMEM_SPEC] * 4,
        out_specs=[tile(D), tile(D), tile(H), tile(H), tile(D), pl.BlockSpec((8, D), lambda i: (0, 0))],
        compiler_params=pltpu.CompilerParams(dimension_semantics=("arbitrary",), vmem_limit_bytes=VMEM_LIMIT),
    )(xhat1, target, mod, ln, w_mi, w_mo)


def _b1(dx1, xhat1, rstd1, x, mix, z, mod, ln, w_out, w_in, conv_w, w_pool, pool_scale, tm):
    T, D = x.shape
    ZW = w_in.shape[1]
    CC = ZW // 4
    nt = T // tm
    hb = tm // HALO

    def body(dx1_ref, xh1_ref, rstd_ref, x_ref, mix_ref, z_ref, zh_ref, mod_ref, ln_ref, wout_ref, win_ref, cw_ref, wp_ref, ps_ref,
             dx_ref, dmix_ref, ycat_ref, dz_ref, acc_ref, gcw_ref, gwp_ref, cv_s, vp_s, e_s, q_s):
        i = pl.program_id(0)
        j = nt - 1 - i

        @pl.when(i == 0)
        def _():
            acc_ref[...] = jnp.zeros((8, D), F32)
            gcw_ref[...] = jnp.zeros((8, CC), F32)
            gwp_ref[...] = jnp.zeros(gwp_ref.shape, F32)
            e_s[tm : tm + HALO, :] = jnp.zeros((HALO, CC), F32)
            q_s[tm : tm + HALO, :] = jnp.zeros((HALO, CC), F32)

        sh1, sc1, g1 = mod_ref[0:1, :], mod_ref[1:2, :], mod_ref[2:3, :]
        dx1 = dx1_ref[...]
        xhat1 = xh1_ref[...]
        acc_ref[0:1, :] += _colsum(dx1 * xhat1)
        acc_ref[1:2, :] += _colsum(dx1)
        dr1 = _ln_bwd(dx1 * ln_ref[0:1, :], xhat1, rstd_ref[...])
        acc_ref[4:5, :] += _colsum(dr1 * mix_ref[...])
        dmix = ((1.0 + g1) * dr1).astype(BF16)
        dmix_ref[...] = dmix
        dycat = lax.dot_general(dmix, wout_ref[...], NT, preferred_element_type=F32)

        z = z_ref[...].astype(F32)
        zh = zh_ref[...].astype(F32) * jnp.where(j > 0, 1.0, 0.0)
        gb, gc, vc, vp = z[:, 0:CC], z[:, CC : 2 * CC], z[:, 2 * CC : 3 * CC], z[:, 3 * CC : 4 * CC]
        cv = gc * vc
        cv_s[0:HALO, :] = zh[:, CC : 2 * CC] * zh[:, 2 * CC : 3 * CC]
        cv_s[HALO : HALO + tm, :] = cv
        vp_s[0:HALO, :] = zh[:, 3 * CC : 4 * CC]
        vp_s[HALO : HALO + tm, :] = vp
        cv_m2 = cv_s[HALO - 2 : HALO - 2 + tm, :]
        cv_m1 = cv_s[HALO - 1 : HALO - 1 + tm, :]
        w0, w1, w2 = cw_ref[0:1, :], cw_ref[1:2, :], cw_ref[2:3, :]
        conv = w0 * cv_m2 + w1 * cv_m1 + w2 * cv
        dyc = dycat[:, 0:CC]
        e = dyc * gb
        e_s[0:tm, :] = e
        dcv = w2 * e + w1 * e_s[1 : 1 + tm, :] + w0 * e_s[2 : 2 + tm, :]
        gcw_ref[0:1, :] += _colsum(e * cv_m2)
        gcw_ref[1:2, :] += _colsum(e * cv_m1)
        gcw_ref[2:3, :] += _colsum(e * cv)
        y_parts = [gb * conv]
        dz_parts = [dyc * conv, dcv * vc, dcv * gc]

        row = j * tm + lax.broadcasted_iota(jnp.int32, (tm, 1), 0)
        feats, inv_cnts = _pool_features(vp, vp_s, row, tm)
        gps_parts, dps = [], []
        for g in range(len(POOL_WINDOWS)):
            cols = slice(128 * g, 128 * g + 128)
            p = feats[g].astype(BF16)
            scale = ps_ref[0:1, cols]
            wp = wp_ref[g].astype(BF16)
            pw = jnp.dot(p, wp, preferred_element_type=F32)
            y_parts.append(pw * scale)
            dyp = dycat[:, CC + 128 * g : CC + 128 * g + 128]
            gps_parts.append(_colsum(dyp * pw))
            dpw = (dyp * scale).astype(BF16)
            gwp_ref[g] += lax.dot_general(p, dpw, TN, preferred_element_type=F32)
            dp = lax.dot_general(dpw, wp, NT, preferred_element_type=F32)
            q_s[0:tm, cols] = dp * inv_cnts[g]
            dps.append(dp)
        sq = _window_sums(q_s[...], tm, causal=False)
        dz_parts += [sq[g] - dps[g] for g in range(len(POOL_WINDOWS))]
        gcw_ref[3:4, :] += jnp.concatenate(gps_parts, axis=1)
        ycat_ref[...] = jnp.concatenate(y_parts, axis=1).astype(BF16)
        dz = jnp.concatenate(dz_parts, axis=1).astype(BF16)
        dz_ref[...] = dz
        dh = lax.dot_general(dz, win_ref[...], NT, preferred_element_type=F32)
        acc_ref[2:3, :] += _colsum(dh)
        acc_ref[3:4, :] += _colsum(dh * x_ref[...])
        dx_ref[...] = DEEPNORM_ALPHA * dr1 + dh * (1.0 + sc1)
        e_s[tm : tm + HALO, :] = e_s[0:HALO, :]
        q_s[tm : tm + HALO, :] = q_s[0:HALO, :]

    tile = lambda w: pl.BlockSpec((tm, w), lambda i: (nt - 1 - i, 0))
    halo = pl.BlockSpec((HALO, ZW), lambda i: (jnp.maximum((nt - 1 - i) * hb - 1, 0), 0))
    fixed = lambda shape: pl.BlockSpec(shape, lambda i: (0,) * len(shape))
    return pl.pallas_call(
        body,
        name="b1",
        grid=(nt,),
        out_shape=[
            jax.ShapeDtypeStruct((T, D), F32),
            jax.ShapeDtypeStruct((T, D), BF16),
            jax.ShapeDtypeStruct((T, D), BF16),
            jax.ShapeDtypeStruct((T, ZW), BF16),
            jax.ShapeDtypeStruct((8, D), F32),
            jax.ShapeDtypeStruct((8, CC), F32),
            jax.ShapeDtypeStruct(w_pool.shape, F32),
        ],
        in_specs=[tile(D), tile(D), tile(1), tile(D), tile(D), tile(ZW), halo] + [VMEM_SPEC] * 7,
        out_specs=[tile(D), tile(D), tile(D), tile(ZW), fixed((8, D)), fixed((8, CC)), fixed(w_pool.shape)],
        scratch_shapes=[
            pltpu.VMEM((HALO + tm, CC), F32),
            pltpu.VMEM((HALO + tm, CC), F32),
            pltpu.VMEM((tm + HALO, CC), F32),
            pltpu.VMEM((tm + HALO, CC), F32),
        ],
        compiler_params=pltpu.CompilerParams(dimension_semantics=("arbitrary",), vmem_limit_bytes=VMEM_LIMIT),
    )(dx1, xhat1, rstd1, x, mix, z, z, mod, ln, w_out, w_in, conv_w, w_pool, pool_scale)


def _wgrad(a, b, bk, n_groups, bt, name, owners=None, scatter=None):
    T, K = a.shape
    N = b.shape[1]
    nk, nt, ng = K // bk, T // bt, N // n_groups
    nc = min(512, ng)
    ns = 0 if scatter is None else 1
    n_steps = nk * n_groups * nt
    mid_step = min(1, n_steps - 1)

    def body(*refs):
        a_ref, b_ref = refs[0], refs[1]
        o_ref = refs[2 + ns]
        acc = refs[3 + 2 * ns]
        if ns:
            s_hbm, s_recv, s_scr = refs[2], refs[4], refs[6:]
        kk, gg, t = pl.program_id(0), pl.program_id(1), pl.program_id(2)
        step = (kk * n_groups + gg) * nt + t

        if ns:

            @pl.when(step == 0)
            def _():
                _scatter_start(s_hbm, s_scr)

            @pl.when(step == mid_step)
            def _():
                _scatter_middle(s_hbm, s_recv, s_scr)

        @pl.when(t == 0)
        def _():
            acc[...] = jnp.zeros(acc.shape, F32)

        at = a_ref[...].T
        for c in range(ng // nc):
            cs = slice(c * nc, (c + 1) * nc)
            acc[:, cs] += jnp.dot(at, b_ref[:, cs], preferred_element_type=F32)

        @pl.when(t == nt - 1)
        def _():
            if owners is None:
                o_ref[...] = acc[...].astype(BF16)
            else:
                per = N // owners
                for o in range(ng // per):
                    o_ref[o] = acc[:, o * per : (o + 1) * per].astype(BF16)

        if ns:

            @pl.when(step == n_steps - 1)
            def _():
                _scatter_finish(s_hbm, s_recv, s_scr)

    if owners is None:
        out_shape = [jax.ShapeDtypeStruct((K, N), BF16)]
        out_specs = [pl.BlockSpec((bk, ng), lambda k, g, t: (k, g))]
    else:
        assert bk == K
        per = N // owners
        out_shape = [jax.ShapeDtypeStruct((owners, K, per), BF16)]
        out_specs = [pl.BlockSpec((ng // per, K, per), lambda k, g, t: (g, 0, 0))]
    ins, in_specs = [a, b], [pl.BlockSpec((bt, bk), lambda k, g, t: (t, k)), pl.BlockSpec((bt, ng), lambda k, g, t: (t, g))]
    scratch = [pltpu.VMEM((bk, ng), F32)]
    if ns:
        ins.append(scatter)
        in_specs.append(ANY_SPEC)
        out_shape.append(_scatter_out_shape(scatter))
        out_specs.append(ANY_SPEC)
        scratch += _scatter_scratch(*scatter.shape[1:])
    outs = pl.pallas_call(
        body,
        name=name,
        grid=(nk, n_groups, nt),
        out_shape=out_shape,
        in_specs=in_specs,
        out_specs=out_specs,
        scratch_shapes=scratch,
        compiler_params=pltpu.CompilerParams(dimension_semantics=("arbitrary", "arbitrary", "arbitrary"), vmem_limit_bytes=VMEM_LIMIT),
    )(*ins)
    return outs if ns else outs[0]


def _small_grads(acc1_t, acc2_t, gcw_t, gwp_all, cond_t, my_slot, w_cols):
    D = acc1_t.shape[2]
    n_chunk = D // 128
    q_mine = w_cols // 128

    def total(ref, r):
        s = ref[r, 0:1, :]
        for k in range(1, N_DEV):
            s = s + ref[r, k : k + 1, :]
        return s

    def body(slot_ref, a1_ref, a2_ref, gcw_ref, gwp_ref, ct_ref, gb_ref, gw_ref, gln_ref, gcwo_ref, gwpo_ref, loss_ref, dm_s):
        loss_ref[...] = total(a2_ref, 5)
        for s, (ref, r) in enumerate([(a1_ref, 2), (a1_ref, 3), (a1_ref, 4), (a2_ref, 2), (a2_ref, 3), (a2_ref, 4)]):
            gb_ref[0:1, s * D : (s + 1) * D] = total(ref, r)
            for qq in range(n_chunk):
                dm_s[s * n_chunk + qq] = ref[r, :, 128 * qq : 128 * qq + 128]
        gln_ref[0:1, :] = total(a1_ref, 0)
        gln_ref[1:2, :] = total(a1_ref, 1)
        gln_ref[2:3, :] = total(a2_ref, 0)
        gln_ref[3:4, :] = total(a2_ref, 1)
        gcwo_ref[...] = jnp.zeros(gcwo_ref.shape, F32)
        for r in range(4):
            gcwo_ref[r : r + 1, :] = total(gcw_ref, r)
        wp = gwp_ref[0]
        for k in range(1, N_DEV):
            wp = wp + gwp_ref[k]
        gwpo_ref[0] = wp
        ct = ct_ref[...]
        cond_t = ct * jax.nn.sigmoid(ct)
        q0 = slot_ref[0] * q_mine
        for q in range(q_mine):
            dm = dm_s[q0 + q]
            out = cond_t[:, 0:1] * dm[0:1, :]
            for k in range(1, N_DEV):
                out = out + cond_t[:, k : k + 1] * dm[k : k + 1, :]
            gw_ref[0, :, 128 * q : 128 * q + 128] = out

    CC = gcw_t.shape[2]
    return pl.pallas_call(
        body,
        name="small_grads",
        out_shape=[
            jax.ShapeDtypeStruct((1, 6 * D), F32),
            jax.ShapeDtypeStruct((1, D, w_cols), F32),
            jax.ShapeDtypeStruct((4, D), F32),
            jax.ShapeDtypeStruct((8, CC), F32),
            jax.ShapeDtypeStruct((1, *gwp_all.shape[1:]), F32),
            jax.ShapeDtypeStruct((1, D), F32),
        ],
        in_specs=[pl.BlockSpec(memory_space=pltpu.SMEM)] + [VMEM_SPEC] * 5,
        out_specs=[VMEM_SPEC] * 6,
        scratch_shapes=[pltpu.VMEM((6 * n_chunk, N_DEV, 128), F32)],
        compiler_params=pltpu.CompilerParams(vmem_limit_bytes=VMEM_LIMIT),
    )(my_slot, acc1_t, acc2_t, gcw_t, gwp_all, cond_t)


def kernel(x, c, w_ada, b_ada, w_in, conv_w, w_pool, pool_scale, w_out, ln1_g, ln1_b, w_mlp_in, w_mlp_out, ln2_g, ln2_b, loss_target, m_w_ada, m_b_ada, m_w_in, m_conv_w, m_w_pool, m_pool_scale, m_w_out, m_ln1_g, m_ln1_b, m_w_mlp_in, m_w_mlp_out, m_ln2_g, m_ln2_b, v_w_ada, v_b_ada, v_w_in, v_conv_w, v_w_pool, v_pool_scale, v_w_out, v_ln1_g, v_ln1_b, v_w_mlp_in, v_w_mlp_out, v_ln2_g, v_ln2_b):
    T, D = x.shape[1], x.shape[2]
    H = w_mlp_out.shape[1] * N_DEV
    ZW = w_in.shape[2] * N_DEV
    CC = ZW // 4
    tm = min(512, T // 2)
    bt = min(1024, T)
    ax, ay, ac = _my_place()
    me = _slot(ax, ay, ac)

    w_cols = w_ada.shape[2]
    b_mine = lax.dynamic_slice(b_ada, (0, me * w_cols), (1, w_cols))
    w_in_f, w_out_g, cw_g, c_g, mod_g = _prologue(w_in, w_out, conv_w[0], c, w_ada, b_mine)
    w_out_f = w_out_g.reshape(D, D)
    conv_w_f = jnp.transpose(cw_g, (1, 0, 2)).reshape(conv_w.shape[1], CC)
    c_all = c_g.reshape(N_DEV, D)
    mod = lax.dynamic_index_in_dim(mod_g, me, axis=1, keepdims=False).reshape(6, D)

    ln = jnp.concatenate([ln1_g, ln1_b, ln2_g, ln2_b], axis=0)
    xs, target = x[0], loss_target[0]

    z, h, xhat1, rstd1, mix, w_mi_f, w_mo_g = _f1(
        xs, mod, w_in_f, conv_w_f, w_pool[0], pool_scale, w_out_f, tm,
        [w_mlp_in, w_mlp_out], [True, False])
    dx1, h2, a, du, df, acc2 = _fb2(xhat1, target, mod, ln, w_mi_f, w_mo_g.reshape(H, D), tm)

    grad_x, dmix, ycat, dz, acc1, gcw, gwp = _b1(
        dx1, xhat1, rstd1, xs, mix, z, mod, ln, w_out_f, w_in_f, conv_w_f, w_pool[0], pool_scale, tm)

    gp_mo = _wgrad(a, df, D, 1, bt, "wgrad_mlp_out").reshape(N_DEV, H // N_DEV, D)
    gp_mi, rv_mo = _wgrad(h2, du, D, 2, bt, "wgrad_mlp_in", owners=N_DEV, scatter=gp_mo)
    gp_in, rv_mi = _wgrad(h, dz, D, 1, bt, "wgrad_in", owners=N_DEV, scatter=gp_mi)
    gp_out, rv_in = _wgrad(ycat, dmix, D, 1, bt, "wgrad_out", scatter=gp_in)
    gp_out = gp_out.reshape(N_DEV, D // N_DEV, D)

    acc1_g, acc2_g, gcw_g, gwp_g, rv_out = _allgather([acc1, acc2, gcw, gwp], "gather_small", in_vmem=True, scatter=gp_out)
    g_b_ada, g_w_ada, g_ln, g_cw, g_w_pool, loss_row = _small_grads(
        jnp.transpose(acc1_g, (1, 0, 2)), jnp.transpose(acc2_g, (1, 0, 2)), jnp.transpose(gcw_g, (1, 0, 2)), gwp_g,
        c_all.T, jnp.reshape(me, (1,)).astype(jnp.int32), w_cols)
    cc_mine = conv_w.shape[2]
    g_conv_w = lax.dynamic_slice(g_cw, (0, me * cc_mine), (conv_w.shape[1], cc_mine))[None]
    g_pool_scale = g_cw[3:4, :]
    g_ln1_g, g_ln1_b, g_ln2_g, g_ln2_b = g_ln[0:1], g_ln[1:2], g_ln[2:3], g_ln[3:4]

    small = _adamw_multi(
        [
            (b_ada, g_b_ada, m_b_ada, v_b_ada),
            (conv_w, g_conv_w, m_conv_w, v_conv_w),
            (w_pool, g_w_pool, m_w_pool, v_w_pool),
            (pool_scale, g_pool_scale, m_pool_scale, v_pool_scale),
            (ln1_g, g_ln1_g, m_ln1_g, v_ln1_g),
            (ln1_b, g_ln1_b, m_ln1_b, v_ln1_b),
            (ln2_g, g_ln2_g, m_ln2_g, v_ln2_g),
            (ln2_b, g_ln2_b, m_ln2_b, v_ln2_b),
        ],
        "adamw_small")
    u_b_ada, u_conv_w, u_w_pool, u_pool_scale, u_ln1_g, u_ln1_b, u_ln2_g, u_ln2_b = small

    g_w_ada, *u_w_ada = _sum_adamw(g_w_ada, w_ada, m_w_ada, v_w_ada, "adamw_w_ada")
    g_w_mo, *u_w_mo = _sum_adamw(rv_mo, w_mlp_out, m_w_mlp_out, v_w_mlp_out, "sum_w_mlp_out")
    g_w_mi, *u_w_mi = _sum_adamw(rv_mi, w_mlp_in, m_w_mlp_in, v_w_mlp_in, "sum_w_mlp_in")
    g_w_in, *u_w_in = _sum_adamw(rv_in, w_in, m_w_in, v_w_in, "sum_w_in")
    g_w_out, *u_w_out = _sum_adamw(rv_out, w_out, m_w_out, v_w_out, "sum_w_out")

    grads = [g_w_ada, g_b_ada, g_w_in, g_conv_w, g_w_pool, g_pool_scale, g_w_out, g_ln1_g, g_ln1_b, g_w_mi, g_w_mo, g_ln2_g, g_ln2_b]
    updates = [u_w_ada, u_b_ada, u_w_in, u_conv_w, u_w_pool, u_pool_scale, u_w_out, u_ln1_g, u_ln1_b, u_w_mi, u_w_mo, u_ln2_g, u_ln2_b]
    deltas = [u[0] for u in updates]
    new_m = [u[1] for u in updates]
    new_v = [u[2] for u in updates]
    return (loss_row[0, 0], grad_x[None], *grads, *deltas, *new_m, *new_v)
```

```python
import jax
import jax.numpy as jnp
from jax import lax
from jax.experimental import pallas as pl
from jax.experimental.pallas import tpu as pltpu

F32 = jnp.float32
BF16 = jnp.bfloat16
MESH = pl.DeviceIdType.MESH
N_DEV = 8

LN_EPS = 1e-5
DEPTH = 1
DEEPNORM_ALPHA = (2.0 * DEPTH) ** 0.25
POOL_WINDOWS = (2, 4, 8, 16)
HALO = 16

ADAM_LR = 0.001
ADAM_B1 = 0.9
ADAM_B2 = 0.999
ADAM_EPS = 1e-08
ADAM_WD = 0.01
ADAM_STEP = 10

VMEM_LIMIT = 60 * 1024 * 1024

VMEM_SPEC = pl.BlockSpec(memory_space=pltpu.VMEM)
ANY_SPEC = pl.BlockSpec(memory_space=pl.ANY)

NT = (((1,), (1,)), ((), ()))
TN = (((0,), (0,)), ((), ()))


def _my_place():
    return lax.axis_index("x"), lax.axis_index("y"), lax.axis_index("c")


def _slot(x, y, c):
    return 4 * x + 2 * y + c


def _gather_place(ins, outs, a, slot):
    if len(outs[a].shape) == len(ins[a].shape):
        wb = ins[a].shape[1]
        return outs[a].at[:, pl.ds(pl.multiple_of(slot * wb, wb), wb)]
    return outs[a].at[slot]


def _gather_copy(ins, outs, sems, a, k, block, to, from_shard=False):
    send_sems, recv_sems, _ = sems
    dst = _gather_place(ins, outs, a, _slot(*block))
    return pltpu.make_async_remote_copy(
        src_ref=ins[a] if from_shard else dst,
        dst_ref=dst,
        send_sem=send_sems.at[7 * a + k],
        recv_sem=recv_sems.at[7 * a + k],
        device_id=to,
        device_id_type=MESH,
    )


def _gather_peers():
    x, y, c = _my_place()
    return (x, y, c), (x, y, 1 - c), [(1 - x, y), (x, 1 - y), (1 - x, 1 - y)]


def _gather_first(ins, outs, sems):
    me, sibling, chips = _gather_peers()
    first = []
    for a in range(len(ins)):
        first.append(_gather_copy(ins, outs, sems, a, 0, me, sibling, from_shard=True))
        first += [_gather_copy(ins, outs, sems, a, 1 + j, me, (*chip, me[2]), from_shard=True) for j, chip in enumerate(chips)]
    return first


def _gather_mine(ins, outs, sems, a):
    me, _, _ = _gather_peers()
    return pltpu.make_async_copy(ins[a], _gather_place(ins, outs, a, _slot(*me)), sems[2].at[a])


def _gather_start(ins, outs, sems):
    for a in range(len(ins)):
        _gather_mine(ins, outs, sems, a).start()
    for cp in _gather_first(ins, outs, sems):
        cp.start()


def _gather_forward(ins, outs, sems, j):
    me, sibling, chips = _gather_peers()
    for a in range(len(ins)):
        _gather_copy(ins, outs, sems, a, 1 + j, (*chips[j], me[2]), me).wait_recv()
        _gather_copy(ins, outs, sems, a, 4 + j, (*chips[j], me[2]), sibling).start()


def _gather_finish(ins, outs, sems):
    me, sibling, chips = _gather_peers()
    for a in range(len(ins)):
        _gather_copy(ins, outs, sems, a, 0, sibling, me).wait_recv()
        for j, chip in enumerate(chips):
            _gather_copy(ins, outs, sems, a, 4 + j, (*chip, 1 - me[2]), me).wait_recv()
    for cp in _gather_first(ins, outs, sems):
        cp.wait_send()
    for a in range(len(ins)):
        for j, chip in enumerate(chips):
            _gather_copy(ins, outs, sems, a, 4 + j, (*chip, me[2]), sibling).wait_send()
        _gather_mine(ins, outs, sems, a).wait()


def _gather_scratch(n):
    return [pltpu.SemaphoreType.DMA((7 * n,)), pltpu.SemaphoreType.DMA((7 * n,)), pltpu.SemaphoreType.DMA((n,))]


def _gather_out_shape(shards, by_cols):
    return [
        jax.ShapeDtypeStruct((s.shape[0], N_DEV * s.shape[1]) if cols else (N_DEV, *s.shape), s.dtype)
        for s, cols in zip(shards, by_cols)
    ]


N_CHIP = 4


def _scatter_scratch(rows, cols):
    block = pltpu.VMEM((N_CHIP, rows, cols), BF16)
    dma = pltpu.SemaphoreType.DMA
    return [block, block, block, dma((N_CHIP,)), dma((N_CHIP,)), dma((N_CHIP,)), dma((N_CHIP - 1,)), dma((N_CHIP - 1,)), dma]


def _scatter_pair_copies(g_hbm, scr):
    x, y, c = _my_place()
    mine, theirs, _, a_send, a_recv, load_sem = scr[:6]
    to_sibling = [
        pltpu.make_async_remote_copy(
            src_ref=g_hbm.at[2 * q + (1 - c)], dst_ref=theirs.at[q], send_sem=a_send.at[q], recv_sem=a_recv.at[q],
            device_id=(x, y, 1 - c), device_id_type=MESH)
        for q in range(N_CHIP)
    ]
    loads = [pltpu.make_async_copy(g_hbm.at[2 * q + c], mine.at[q], load_sem.at[q]) for q in range(N_CHIP)]
    return to_sibling, loads


def _scatter_sum_copies(recv, scr):
    x, y, c = _my_place()
    sums, b_send, b_recv, own_sem = scr[2], scr[6], scr[7], scr[8]
    q_me = 2 * x + y
    to_owner = [
        pltpu.make_async_remote_copy(
            src_ref=sums.at[2 * px + py], dst_ref=recv.at[q_me], send_sem=b_send.at[j], recv_sem=b_recv.at[j],
            device_id=(px, py, c), device_id_type=MESH)
        for j, (px, py) in enumerate([(1 - x, y), (x, 1 - y), (1 - x, 1 - y)])
    ]
    return to_owner, pltpu.make_async_copy(sums.at[q_me], recv.at[q_me], own_sem)


def _scatter_start(g_hbm, scr):
    to_sibling, loads = _scatter_pair_copies(g_hbm, scr)
    for cp in to_sibling + loads:
        cp.start()


def _scatter_middle(g_hbm, recv, scr):
    to_sibling, loads = _scatter_pair_copies(g_hbm, scr)
    for cp in to_sibling:
        cp.wait_recv()
    for cp in loads:
        cp.wait()
    mine, theirs, sums = scr[:3]

    def step(r, carry):
        rs = pl.ds(pl.multiple_of(r * ROW_CHUNK, ROW_CHUNK), ROW_CHUNK)
        for q in range(N_CHIP):
            sums[q, rs, :] = (mine[q, rs, :].astype(F32) + theirs[q, rs, :].astype(F32)).astype(BF16)
        return carry

    lax.fori_loop(0, mine.shape[1] // ROW_CHUNK, step, 0)
    to_owner, own = _scatter_sum_copies(recv, scr)
    for cp in to_owner + [own]:
        cp.start()


def _scatter_finish(g_hbm, recv, scr):
    to_sibling, _ = _scatter_pair_copies(g_hbm, scr)
    to_owner, own = _scatter_sum_copies(recv, scr)
    for cp in to_owner:
        cp.wait_recv()
    for cp in to_sibling + to_owner:
        cp.wait_send()
    own.wait()


def _scatter_out_shape(gparts):
    return jax.ShapeDtypeStruct((N_CHIP, *gparts.shape[1:]), gparts.dtype)


def _adamw_math(w, g, m, v):
    m = ADAM_B1 * m + (1.0 - ADAM_B1) * g
    v = ADAM_B2 * v + (1.0 - ADAM_B2) * (g * g)
    m_hat = m / (1.0 - ADAM_B1**ADAM_STEP)
    v_hat = v / (1.0 - ADAM_B2**ADAM_STEP)
    delta = -ADAM_LR * (m_hat / (jnp.sqrt(v_hat) + ADAM_EPS) + ADAM_WD * w)
    return delta, m, v


ROW_CHUNK = 64


ELEMS_PER_STEP = 128 * 1024


def _allgather(shards, name, in_vmem, scatter=None, by_cols=None):
    n = len(shards)
    by_cols = by_cols or [False] * n
    ns = 0 if scatter is None else 1

    def body(*refs):
        ins, outs = refs[:n], refs[n + ns : 2 * n + ns]
        rest = refs[2 * n + ns :]
        if ns:
            g_hbm, recv, sems, scr = refs[n], rest[0], rest[1:4], rest[4:]
            _scatter_start(g_hbm, scr)
        else:
            sems = rest[0:3]
        _gather_start(ins, outs, sems)
        if ns:
            _scatter_middle(g_hbm, recv, scr)
        for j in range(3):
            _gather_forward(ins, outs, sems, j)
        _gather_finish(ins, outs, sems)
        if ns:
            _scatter_finish(g_hbm, recv, scr)

    spec = VMEM_SPEC if in_vmem else ANY_SPEC
    extra_in, extra_out, extra_spec, extra_scratch = [], [], [], []
    if ns:
        extra_in, extra_spec = [scatter], [ANY_SPEC]
        extra_out = [_scatter_out_shape(scatter)]
        extra_scratch = _scatter_scratch(*scatter.shape[1:])
    return pl.pallas_call(
        body,
        name=name,
        out_shape=_gather_out_shape(shards, by_cols) + extra_out,
        in_specs=[spec] * n + extra_spec,
        out_specs=[spec] * n + extra_spec,
        scratch_shapes=_gather_scratch(n) + extra_scratch,
        compiler_params=pltpu.CompilerParams(vmem_limit_bytes=VMEM_LIMIT),
    )(*shards, *extra_in)


def _sum_adamw(parts, w, m, v, name):
    _, rows, cols = w.shape
    rb = rows
    while rb * cols > ELEMS_PER_STEP and rb % 16 == 0:
        rb //= 2

    def body(p_ref, w_ref, m_ref, v_ref, grad_ref, delta_ref, nm_ref, nv_ref):
        g = p_ref[0].astype(F32)
        for k in range(1, p_ref.shape[0]):
            g = g + p_ref[k].astype(F32)
        delta, nm, nv = _adamw_math(w_ref[0], g, m_ref[0], v_ref[0])
        grad_ref[0] = g
        delta_ref[0] = delta
        nm_ref[0] = nm
        nv_ref[0] = nv

    block = lambda lead: pl.BlockSpec((lead, rb, cols), lambda i: (0, i, 0))
    out = jax.ShapeDtypeStruct(w.shape, F32)
    return pl.pallas_call(
        body,
        name=name,
        grid=(rows // rb,),
        out_shape=[out] * 4,
        in_specs=[block(parts.shape[0])] + [block(1)] * 3,
        out_specs=[block(1)] * 4,
        compiler_params=pltpu.CompilerParams(dimension_semantics=("arbitrary",), vmem_limit_bytes=VMEM_LIMIT),
    )(parts, w, m, v)


def _adamw_multi(items, name):
    n = len(items)

    def body(*refs):
        ins, outs = refs[: 4 * n], refs[4 * n :]
        for a in range(n):
            w_ref, g_ref, m_ref, v_ref = ins[4 * a : 4 * a + 4]
            d_ref, nm_ref, nv_ref = outs[3 * a : 3 * a + 3]
            delta, nm, nv = _adamw_math(w_ref[...], g_ref[...], m_ref[...], v_ref[...])
            d_ref[...] = delta
            nm_ref[...] = nm
            nv_ref[...] = nv

    flat = [a for it in items for a in it]
    out_shape = [jax.ShapeDtypeStruct(it[0].shape, F32) for it in items for _ in range(3)]
    outs = pl.pallas_call(
        body,
        name=name,
        out_shape=out_shape,
        in_specs=[VMEM_SPEC] * (4 * n),
        out_specs=[VMEM_SPEC] * (3 * n),
        compiler_params=pltpu.CompilerParams(vmem_limit_bytes=VMEM_LIMIT),
    )(*flat)
    return [tuple(outs[3 * a : 3 * a + 3]) for a in range(n)]


def _prologue(w_in, w_out, conv_w, c, w_ada, b_mine):
    D = c.shape[1]
    wc = w_ada.shape[2]
    shards16 = [jax.ShapeDtypeStruct(w_in.shape[1:], BF16), jax.ShapeDtypeStruct(w_out.shape[1:], BF16)]

    def to_all(src, out, sems):
        x, y, c = _my_place()
        me = _slot(x, y, c)
        copies = [
            pltpu.make_async_remote_copy(
                src_ref=src, dst_ref=out.at[me], send_sem=sems[0].at[k - 1], recv_sem=sems[1].at[k - 1],
                device_id=(x ^ (k >> 2), y ^ ((k >> 1) & 1), c ^ (k & 1)), device_id_type=MESH)
            for k in range(1, N_DEV)
        ]
        return copies, pltpu.make_async_copy(src, out.at[me], sems[2].at[0])

    def start(copies, own):
        for cp in copies + [own]:
            cp.start()

    def finish(copies, own):
        for cp in copies:
            cp.wait()
        own.wait()

    def body(win_ref, wout_ref, cw_ref, c_ref, wada_ref, b_ref, win_g, wout_g, cw_g, c_g, mod_g, win16, wout16, c_s, mp_s, *sems):
        c_copies = to_all(c_ref, c_g, sems[3:6])
        start(*c_copies)
        win16[...] = win_ref[0].astype(BF16)
        wout16[...] = wout_ref[0].astype(BF16)
        w_ins, w_outs, w_sems = (win16, wout16, cw_ref), (win_g, wout_g, cw_g), sems[0:3]
        _gather_start(w_ins, w_outs, w_sems)
        finish(*c_copies)
        for k in range(N_DEV):
            c_s[k : k + 1, :] = c_g[k]
        cv = c_s[...]
        cond = cv * jax.nn.sigmoid(cv)
        mp_s[...] = jnp.dot(cond, wada_ref[0], precision=lax.Precision.HIGHEST, preferred_element_type=F32) + b_ref[...]
        m_copies = to_all(mp_s, mod_g, sems[6:9])
        start(*m_copies)
        for j in range(3):
            _gather_forward(w_ins, w_outs, w_sems, j)
        _gather_finish(w_ins, w_outs, w_sems)
        finish(*m_copies)

    return pl.pallas_call(
        body,
        name="prologue",
        out_shape=_gather_out_shape(shards16 + [conv_w], [True, False, False])
        + [jax.ShapeDtypeStruct((N_DEV, 1, D), F32), jax.ShapeDtypeStruct((N_DEV, N_DEV, wc), F32)],
        in_specs=[VMEM_SPEC, VMEM_SPEC, ANY_SPEC] + [VMEM_SPEC] * 3,
        out_specs=[ANY_SPEC] * 3 + [VMEM_SPEC] * 2,
        scratch_shapes=[pltpu.VMEM(s.shape, BF16) for s in shards16]
        + [pltpu.VMEM((N_DEV, D), F32), pltpu.VMEM((N_DEV, wc), F32)]
        + _gather_scratch(3) + _gather_scratch(1) + _gather_scratch(1),
        compiler_params=pltpu.CompilerParams(vmem_limit_bytes=VMEM_LIMIT),
    )(w_in, w_out, conv_w, c, w_ada, b_mine)


def _ln_fwd(r):
    mu = jnp.mean(r, axis=-1, keepdims=True)
    d = r - mu
    var = jnp.mean(d * d, axis=-1, keepdims=True)
    rstd = lax.rsqrt(var + LN_EPS)
    return d * rstd, rstd


def _ln_bwd(dxh, xhat, rstd):
    m1 = jnp.mean(dxh, axis=-1, keepdims=True)
    m2 = jnp.mean(dxh * xhat, axis=-1, keepdims=True)
    return rstd * (dxh - m1 - xhat * m2)


def _colsum(a):
    return jnp.sum(a, axis=0, keepdims=True)


def _window_sums(ext, tm, causal):
    n = ext.shape[0]
    lo = HALO if causal else 0
    s, out = ext, []
    for p in range(len(POOL_WINDOWS)):
        assert POOL_WINDOWS[p] == 2 ** (p + 1)
        k = 2**p
        s = s + pltpu.roll(s, k if causal else n - k, 0)
        out.append(s[lo : lo + tm, 0:128])
        if p + 1 < len(POOL_WINDOWS):
            s = s[:, 128:]
    return out


def _pool_features(vp, vp_s, row, tm):
    sums = _window_sums(vp_s[...], tm, causal=True)
    feats, inv_cnts = [], []
    for g, win in enumerate(POOL_WINDOWS):
        inv_cnt = 1.0 / jnp.minimum(row + 1, win).astype(F32)
        feats.append(sums[g] * inv_cnt - vp[:, 128 * g : 128 * g + 128])
        inv_cnts.append(inv_cnt)
    return feats, inv_cnts


def _f1(x, mod, w_in, conv_w, w_pool, pool_scale, w_out, tm, gather, by_cols):
    T, D = x.shape
    ZW = w_in.shape[1]
    CC = ZW // 4
    nt = T // tm
    ng = len(gather)
    fwd_steps = [max(nt - 3 + j, 0) for j in range(3)]

    shards16 = [jax.ShapeDtypeStruct(s.shape[1:], BF16) for s in gather]

    def body(*refs):
        x_ref, mod_ref, win_ref, cw_ref, wp_ref, ps_ref, wout_ref = refs[:7]
        g_f32 = refs[7 : 7 + ng]
        z_ref, h_ref, xhat_ref, rstd_ref, mix_ref = refs[7 + ng : 12 + ng]
        g_outs = refs[12 + ng : 12 + 2 * ng]
        cv_s, vp_s = refs[12 + 2 * ng : 14 + 2 * ng]
        g_ins = refs[14 + 2 * ng : 14 + 3 * ng]
        g_sems = refs[14 + 3 * ng :]
        i = pl.program_id(0)

        @pl.when(i == 0)
        def _():
            for src, dst in zip(g_f32, g_ins):
                dst[...] = src[0].astype(BF16)
            _gather_start(g_ins, g_outs, g_sems)
            cv_s[0:HALO, :] = jnp.zeros((HALO, CC), F32)
            vp_s[0:HALO, :] = jnp.zeros((HALO, CC), F32)

        xv = x_ref[...]
        sh1, sc1, g1 = mod_ref[0:1, :], mod_ref[1:2, :], mod_ref[2:3, :]
        h = (xv * (1.0 + sc1) + sh1).astype(BF16)
        h_ref[...] = h
        z = jnp.dot(h, win_ref[...], preferred_element_type=F32)
        z_ref[...] = z.astype(BF16)
        gb, gc, vc, vp = z[:, 0:CC], z[:, CC : 2 * CC], z[:, 2 * CC : 3 * CC], z[:, 3 * CC : 4 * CC]
        cv = gc * vc
        cv_s[HALO : HALO + tm, :] = cv
        vp_s[HALO : HALO + tm, :] = vp
        conv = cw_ref[0:1, :] * cv_s[HALO - 2 : HALO - 2 + tm, :] + cw_ref[1:2, :] * cv_s[HALO - 1 : HALO - 1 + tm, :] + cw_ref[2:3, :] * cv
        parts = [gb * conv]
        row = i * tm + lax.broadcasted_iota(jnp.int32, (tm, 1), 0)
        feats, _ = _pool_features(vp, vp_s, row, tm)
        for g in range(len(POOL_WINDOWS)):
            pw = jnp.dot(feats[g].astype(BF16), wp_ref[g].astype(BF16), preferred_element_type=F32)
            parts.append(pw * ps_ref[0:1, 128 * g : 128 * g + 128])
        cv_s[0:HALO, :] = cv_s[tm : tm + HALO, :]
        vp_s[0:HALO, :] = vp_s[tm : tm + HALO, :]
        ycat = jnp.concatenate(parts, axis=1).astype(BF16)
        mix = jnp.dot(ycat, wout_ref[...], preferred_element_type=F32)
        mix_ref[...] = mix
        xhat, rstd = _ln_fwd(DEEPNORM_ALPHA * xv + (1.0 + g1) * mix)
        xhat_ref[...] = xhat
        rstd_ref[...] = rstd

        for j in range(3):

            @pl.when(i == fwd_steps[j])
            def _(j=j):
                _gather_forward(g_ins, g_outs, g_sems, j)

        @pl.when(i == nt - 1)
        def _():
            _gather_finish(g_ins, g_outs, g_sems)

    tile = lambda w: pl.BlockSpec((tm, w), lambda i: (i, 0))
    return pl.pallas_call(
        body,
        name="f1",
        grid=(nt,),
        out_shape=[
            jax.ShapeDtypeStruct((T, ZW), BF16),
            jax.ShapeDtypeStruct((T, D), BF16),
            jax.ShapeDtypeStruct((T, D), F32),
            jax.ShapeDtypeStruct((T, 1), F32),
            jax.ShapeDtypeStruct((T, D), F32),
        ]
        + _gather_out_shape(shards16, by_cols),
        in_specs=[tile(D)] + [VMEM_SPEC] * (6 + ng),
        out_specs=[tile(ZW), tile(D), tile(D), tile(1), tile(D)] + [ANY_SPEC] * ng,
        scratch_shapes=[pltpu.VMEM((HALO + tm, CC), F32), pltpu.VMEM((HALO + tm, CC), F32)]
        + [pltpu.VMEM(s.shape, BF16) for s in shards16]
        + _gather_scratch(ng),
        compiler_params=pltpu.CompilerParams(dimension_semantics=("arbitrary",), vmem_limit_bytes=VMEM_LIMIT),
    )(x, mod, w_in, conv_w, w_pool, pool_scale, w_out, *gather)


def _fb2(xhat1, target, mod, ln, w_mi, w_mo, tm):
    T, D = xhat1.shape
    H = w_mi.shape[1]
    hc = min(1024, H)
    nb = H // hc
    nt = T // tm

    def body(xh1_ref, t_ref, mod_ref, ln_ref, wmi_ref, wmo_ref, dx1_ref, h2_ref, a_ref, du_ref, df_ref, acc_ref):
        i = pl.program_id(0)

        @pl.when(i == 0)
        def _():
            acc_ref[...] = jnp.zeros((8, D), F32)

        sh2, sc2, g2 = mod_ref[3:4, :], mod_ref[4:5, :], mod_ref[5:6, :]
        x1 = xh1_ref[...] * ln_ref[0:1, :] + ln_ref[1:2, :]
        h2 = (x1 * (1.0 + sc2) + sh2).astype(BF16)
        h2_ref[...] = h2
        f = jnp.zeros((tm, D), F32)
        for k in range(nb):
            ks = slice(k * hc, (k + 1) * hc)
            r = jnp.maximum(jnp.dot(h2, wmi_ref[:, ks], preferred_element_type=F32), 0.0)
            du_ref[:, ks] = r.astype(BF16)
            a = (r * r).astype(BF16)
            a_ref[:, ks] = a
            f = f + jnp.dot(a, wmo_ref[ks, :], preferred_element_type=F32)
        xhat2, rstd2 = _ln_fwd(DEEPNORM_ALPHA * x1 + (1.0 + g2) * f)
        ln2_g = ln_ref[2:3, :]
        d = xhat2 * ln2_g + ln_ref[3:4, :] - t_ref[...]
        dr2 = _ln_bwd(d * (ln2_g * (1.0 / D)), xhat2, rstd2)
        df = ((1.0 + g2) * dr2).astype(BF16)
        df_ref[...] = df
        dh2 = jnp.zeros((tm, D), F32)
        for k in range(nb):
            ks = slice(k * hc, (k + 1) * hc)
            da = lax.dot_general(df, wmo_ref[ks, :], NT, preferred_element_type=F32)
            du = (da * (2.0 * du_ref[:, ks].astype(F32))).astype(BF16)
            du_ref[:, ks] = du
            dh2 = dh2 + lax.dot_general(du, wmi_ref[:, ks], NT, preferred_element_type=F32)
        dx1_ref[...] = DEEPNORM_ALPHA * dr2 + dh2 * (1.0 + sc2)
        acc_ref[0:1, :] += _colsum(d * xhat2) * (1.0 / D)
        acc_ref[1:2, :] += _colsum(d) * (1.0 / D)
        acc_ref[2:3, :] += _colsum(dh2)
        acc_ref[3:4, :] += _colsum(dh2 * x1)
        acc_ref[4:5, :] += _colsum(dr2 * f)
        acc_ref[5:6, :] += jnp.zeros((1, D), F32) + (0.5 / D) * jnp.sum(d * d)

    tile = lambda w: pl.BlockSpec((tm, w), lambda i: (i, 0))
    return pl.pallas_call(
        body,
        name="fb2",
        grid=(nt,),
        out_shape=[
            jax.ShapeDtypeStruct((T, D), F32),
            jax.ShapeDtypeStruct((T, D), BF16),
            jax.ShapeDtypeStruct((T, H), BF16),
            jax.ShapeDtypeStruct((T, H), BF16),
            jax.ShapeDtypeStruct((T, D), BF16),
            jax.ShapeDtypeStruct((8, D), F32),
        ],
        in_specs=[tile(D), tile(D)] + [VMEM_SPEC] * 4,
        out_specs=[tile(D), tile(D), tile(H), tile(H), tile(D), pl.BlockSpec((8, D), lambda i: (0, 0))],
        compiler_params=pltpu.CompilerParams(dimension_semantics=("arbitrary",), vmem_limit_bytes=VMEM_LIMIT),
    )(xhat1, target, mod, ln, w_mi, w_mo)


def _b1(dx1, xhat1, rstd1, x, mix, z, h, mod, ln, w_out, w_in, conv_w, w_pool, pool_scale, tm):
    T, D = x.shape
    ZW = w_in.shape[1]
    CC = ZW // 4
    nt = T // tm
    hb = tm // HALO
    wb = ZW // N_DEV
    nc = 512

    def body(dx1_ref, xh1_ref, rstd_ref, x_ref, mix_ref, z_ref, zh_ref, h_ref, mod_ref, ln_ref, wout_ref, win_ref, cw_ref, wp_ref, ps_ref,
             dx_ref, acc_ref, gcw_ref, gwp_ref, gin_hbm, gout_hbm, cv_s, vp_s, e_s, q_s, gin_acc, gout_acc, gin_16, gout_16, out_sems):
        i = pl.program_id(0)
        j = nt - 1 - i

        @pl.when(i == 0)
        def _():
            acc_ref[...] = jnp.zeros((8, D), F32)
            gcw_ref[...] = jnp.zeros((8, CC), F32)
            gwp_ref[...] = jnp.zeros(gwp_ref.shape, F32)
            gin_acc[...] = jnp.zeros(gin_acc.shape, F32)
            gout_acc[...] = jnp.zeros(gout_acc.shape, F32)
            e_s[tm : tm + HALO, :] = jnp.zeros((HALO, CC), F32)
            q_s[tm : tm + HALO, :] = jnp.zeros((HALO, CC), F32)

        sh1, sc1, g1 = mod_ref[0:1, :], mod_ref[1:2, :], mod_ref[2:3, :]
        dx1 = dx1_ref[...]
        xhat1 = xh1_ref[...]
        acc_ref[0:1, :] += _colsum(dx1 * xhat1)
        acc_ref[1:2, :] += _colsum(dx1)
        dr1 = _ln_bwd(dx1 * ln_ref[0:1, :], xhat1, rstd_ref[...])
        acc_ref[4:5, :] += _colsum(dr1 * mix_ref[...])
        dmix = ((1.0 + g1) * dr1).astype(BF16)
        dycat = lax.dot_general(dmix, wout_ref[...], NT, preferred_element_type=F32)

        z = z_ref[...].astype(F32)
        zh = zh_ref[...].astype(F32) * jnp.where(j > 0, 1.0, 0.0)
        gb, gc, vc, vp = z[:, 0:CC], z[:, CC : 2 * CC], z[:, 2 * CC : 3 * CC], z[:, 3 * CC : 4 * CC]
        cv = gc * vc
        cv_s[0:HALO, :] = zh[:, CC : 2 * CC] * zh[:, 2 * CC : 3 * CC]
        cv_s[HALO : HALO + tm, :] = cv
        vp_s[0:HALO, :] = zh[:, 3 * CC : 4 * CC]
        vp_s[HALO : HALO + tm, :] = vp
        cv_m2 = cv_s[HALO - 2 : HALO - 2 + tm, :]
        cv_m1 = cv_s[HALO - 1 : HALO - 1 + tm, :]
        w0, w1, w2 = cw_ref[0:1, :], cw_ref[1:2, :], cw_ref[2:3, :]
        conv = w0 * cv_m2 + w1 * cv_m1 + w2 * cv
        dyc = dycat[:, 0:CC]
        e = dyc * gb
        e_s[0:tm, :] = e
        dcv = w2 * e + w1 * e_s[1 : 1 + tm, :] + w0 * e_s[2 : 2 + tm, :]
        gcw_ref[0:1, :] += _colsum(e * cv_m2)
        gcw_ref[1:2, :] += _colsum(e * cv_m1)
        gcw_ref[2:3, :] += _colsum(e * cv)
        y_parts = [gb * conv]
        dz_parts = [dyc * conv, dcv * vc, dcv * gc]

        row = j * tm + lax.broadcasted_iota(jnp.int32, (tm, 1), 0)
        feats, inv_cnts = _pool_features(vp, vp_s, row, tm)
        gps_parts, dps = [], []
        for g in range(len(POOL_WINDOWS)):
            cols = slice(128 * g, 128 * g + 128)
            p = feats[g].astype(BF16)
            scale = ps_ref[0:1, cols]
            wp = wp_ref[g].astype(BF16)
            pw = jnp.dot(p, wp, preferred_element_type=F32)
            y_parts.append(pw * scale)
            dyp = dycat[:, CC + 128 * g : CC + 128 * g + 128]
            gps_parts.append(_colsum(dyp * pw))
            dpw = (dyp * scale).astype(BF16)
            gwp_ref[g] += lax.dot_general(p, dpw, TN, preferred_element_type=F32)
            dp = lax.dot_general(dpw, wp, NT, preferred_element_type=F32)
            q_s[0:tm, cols] = dp * inv_cnts[g]
            dps.append(dp)
        sq = _window_sums(q_s[...], tm, causal=False)
        dz_parts += [sq[g] - dps[g] for g in range(len(POOL_WINDOWS))]
        gcw_ref[3:4, :] += jnp.concatenate(gps_parts, axis=1)
        ycat = jnp.concatenate(y_parts, axis=1).astype(BF16)
        dz = jnp.concatenate(dz_parts, axis=1).astype(BF16)
        dh = lax.dot_general(dz, win_ref[...], NT, preferred_element_type=F32)
        acc_ref[2:3, :] += _colsum(dh)
        acc_ref[3:4, :] += _colsum(dh * x_ref[...])
        dx_ref[...] = DEEPNORM_ALPHA * dr1 + dh * (1.0 + sc1)
        e_s[tm : tm + HALO, :] = e_s[0:HALO, :]
        q_s[tm : tm + HALO, :] = q_s[0:HALO, :]

        h_t = h_ref[...].T
        for k in range(ZW // nc):
            ks = slice(k * nc, (k + 1) * nc)
            gin_acc[:, ks] += jnp.dot(h_t, dz[:, ks], preferred_element_type=F32)
        ycat_t = ycat.T
        for k in range(D // nc):
            ks = slice(k * nc, (k + 1) * nc)
            gout_acc[:, ks] += jnp.dot(ycat_t, dmix[:, ks], preferred_element_type=F32)

        @pl.when(i == nt - 1)
        def _():
            for o in range(N_DEV):
                gin_16[o] = gin_acc[:, o * wb : (o + 1) * wb].astype(BF16)
            gout_16[...] = gout_acc[...].astype(BF16)
            writes = [pltpu.make_async_copy(gin_16, gin_hbm, out_sems.at[0]), pltpu.make_async_copy(gout_16, gout_hbm, out_sems.at[1])]
            for cp in writes:
                cp.start()
            for cp in writes:
                cp.wait()

    tile = lambda w: pl.BlockSpec((tm, w), lambda i: (nt - 1 - i, 0))
    halo = pl.BlockSpec((HALO, ZW), lambda i: (jnp.maximum((nt - 1 - i) * hb - 1, 0), 0))
    fixed = lambda shape: pl.BlockSpec(shape, lambda i: (0,) * len(shape))
    return pl.pallas_call(
        body,
        name="b1",
        grid=(nt,),
        out_shape=[
            jax.ShapeDtypeStruct((T, D), F32),
            jax.ShapeDtypeStruct((8, D), F32),
            jax.ShapeDtypeStruct((8, CC), F32),
            jax.ShapeDtypeStruct(w_pool.shape, F32),
            jax.ShapeDtypeStruct((N_DEV, D, wb), BF16),
            jax.ShapeDtypeStruct((D, D), BF16),
        ],
        in_specs=[tile(D), tile(D), tile(1), tile(D), tile(D), tile(ZW), halo, tile(D)] + [VMEM_SPEC] * 7,
        out_specs=[tile(D), fixed((8, D)), fixed((8, CC)), fixed(w_pool.shape), ANY_SPEC, ANY_SPEC],
        scratch_shapes=[
            pltpu.VMEM((HALO + tm, CC), F32),
            pltpu.VMEM((HALO + tm, CC), F32),
            pltpu.VMEM((tm + HALO, CC), F32),
            pltpu.VMEM((tm + HALO, CC), F32),
            pltpu.VMEM((D, ZW), F32),
            pltpu.VMEM((D, D), F32),
            pltpu.VMEM((N_DEV, D, wb), BF16),
            pltpu.VMEM((D, D), BF16),
            pltpu.SemaphoreType.DMA((2,)),
        ],
        compiler_params=pltpu.CompilerParams(dimension_semantics=("arbitrary",), vmem_limit_bytes=VMEM_LIMIT),
    )(dx1, xhat1, rstd1, x, mix, z, z, h, mod, ln, w_out, w_in, conv_w, w_pool, pool_scale)


N_SCATTER_SCRATCH = 9


def _wgrad(a, b, bk, n_groups, bt, name, owners=None, scatter=()):
    T, K = a.shape
    N = b.shape[1]
    nk, nt, ng = K // bk, T // bt, N // n_groups
    nc = min(512, ng)
    ns = len(scatter)
    n_steps = nk * n_groups * nt
    mid_step = min(1, n_steps - 1)

    def body(*refs):
        a_ref, b_ref = refs[0], refs[1]
        s_hbm = refs[2 : 2 + ns]
        o_ref = refs[2 + ns]
        s_recv = refs[3 + ns : 3 + 2 * ns]
        acc = refs[3 + 2 * ns]
        s_scr = [refs[4 + 2 * ns + N_SCATTER_SCRATCH * k : 4 + 2 * ns + N_SCATTER_SCRATCH * (k + 1)] for k in range(ns)]
        kk, gg, t = pl.program_id(0), pl.program_id(1), pl.program_id(2)
        step = (kk * n_groups + gg) * nt + t

        if ns:

            @pl.when(step == 0)
            def _():
                for k in range(ns):
                    _scatter_start(s_hbm[k], s_scr[k])

            @pl.when(step == mid_step)
            def _():
                for k in range(ns):
                    _scatter_middle(s_hbm[k], s_recv[k], s_scr[k])

        @pl.when(t == 0)
        def _():
            acc[...] = jnp.zeros(acc.shape, F32)

        at = a_ref[...].T
        for c in range(ng // nc):
            cs = slice(c * nc, (c + 1) * nc)
            acc[:, cs] += jnp.dot(at, b_ref[:, cs], preferred_element_type=F32)

        @pl.when(t == nt - 1)
        def _():
            if owners is None:
                o_ref[...] = acc[...].astype(BF16)
            else:
                per = N // owners
                for o in range(ng // per):
                    o_ref[o] = acc[:, o * per : (o + 1) * per].astype(BF16)

        if ns:

            @pl.when(step == n_steps - 1)
            def _():
                for k in range(ns):
                    _scatter_finish(s_hbm[k], s_recv[k], s_scr[k])

    if owners is None:
        out_shape = [jax.ShapeDtypeStruct((K, N), BF16)]
        out_specs = [pl.BlockSpec((bk, ng), lambda k, g, t: (k, g))]
    else:
        assert bk == K
        per = N // owners
        out_shape = [jax.ShapeDtypeStruct((owners, K, per), BF16)]
        out_specs = [pl.BlockSpec((ng // per, K, per), lambda k, g, t: (g, 0, 0))]
    ins, in_specs = [a, b], [pl.BlockSpec((bt, bk), lambda k, g, t: (t, k)), pl.BlockSpec((bt, ng), lambda k, g, t: (t, g))]
    scratch = [pltpu.VMEM((bk, ng), F32)]
    for s in scatter:
        ins.append(s)
        in_specs.append(ANY_SPEC)
        out_shape.append(_scatter_out_shape(s))
        out_specs.append(ANY_SPEC)
        scratch += _scatter_scratch(*s.shape[1:])
    outs = pl.pallas_call(
        body,
        name=name,
        grid=(nk, n_groups, nt),
        out_shape=out_shape,
        in_specs=in_specs,
        out_specs=out_specs,
        scratch_shapes=scratch,
        compiler_params=pltpu.CompilerParams(dimension_semantics=("arbitrary", "arbitrary", "arbitrary"), vmem_limit_bytes=VMEM_LIMIT),
    )(*ins)
    return outs if ns else outs[0]


def _small_grads(acc1_t, acc2_t, gcw_t, gwp_all, cond_t, my_slot, w_cols):
    D = acc1_t.shape[2]
    n_chunk = D // 128
    q_mine = w_cols // 128

    def total(ref, r):
        s = ref[r, 0:1, :]
        for k in range(1, N_DEV):
            s = s + ref[r, k : k + 1, :]
        return s

    def body(slot_ref, a1_ref, a2_ref, gcw_ref, gwp_ref, ct_ref, gb_ref, gw_ref, gln_ref, gcwo_ref, gwpo_ref, loss_ref, dm_s):
        loss_ref[...] = total(a2_ref, 5)
        for s, (ref, r) in enumerate([(a1_ref, 2), (a1_ref, 3), (a1_ref, 4), (a2_ref, 2), (a2_ref, 3), (a2_ref, 4)]):
            gb_ref[0:1, s * D : (s + 1) * D] = total(ref, r)
            for qq in range(n_chunk):
                dm_s[s * n_chunk + qq] = ref[r, :, 128 * qq : 128 * qq + 128]
        gln_ref[0:1, :] = total(a1_ref, 0)
        gln_ref[1:2, :] = total(a1_ref, 1)
        gln_ref[2:3, :] = total(a2_ref, 0)
        gln_ref[3:4, :] = total(a2_ref, 1)
        gcwo_ref[...] = jnp.zeros(gcwo_ref.shape, F32)
        for r in range(4):
            gcwo_ref[r : r + 1, :] = total(gcw_ref, r)
        wp = gwp_ref[0]
        for k in range(1, N_DEV):
            wp = wp + gwp_ref[k]
        gwpo_ref[0] = wp
        ct = ct_ref[...]
        cond_t = ct * jax.nn.sigmoid(ct)
        q0 = slot_ref[0] * q_mine
        for q in range(q_mine):
            dm = dm_s[q0 + q]
            out = cond_t[:, 0:1] * dm[0:1, :]
            for k in range(1, N_DEV):
                out = out + cond_t[:, k : k + 1] * dm[k : k + 1, :]
            gw_ref[0, :, 128 * q : 128 * q + 128] = out

    CC = gcw_t.shape[2]
    return pl.pallas_call(
        body,
        name="small_grads",
        out_shape=[
            jax.ShapeDtypeStruct((1, 6 * D), F32),
            jax.ShapeDtypeStruct((1, D, w_cols), F32),
            jax.ShapeDtypeStruct((4, D), F32),
            jax.ShapeDtypeStruct((8, CC), F32),
            jax.ShapeDtypeStruct((1, *gwp_all.shape[1:]), F32),
            jax.ShapeDtypeStruct((1, D), F32),
        ],
        in_specs=[pl.BlockSpec(memory_space=pltpu.SMEM)] + [VMEM_SPEC] * 5,
        out_specs=[VMEM_SPEC] * 6,
        scratch_shapes=[pltpu.VMEM((6 * n_chunk, N_DEV, 128), F32)],
        compiler_params=pltpu.CompilerParams(vmem_limit_bytes=VMEM_LIMIT),
    )(my_slot, acc1_t, acc2_t, gcw_t, gwp_all, cond_t)


def kernel(x, c, w_ada, b_ada, w_in, conv_w, w_pool, pool_scale, w_out, ln1_g, ln1_b, w_mlp_in, w_mlp_out, ln2_g, ln2_b, loss_target, m_w_ada, m_b_ada, m_w_in, m_conv_w, m_w_pool, m_pool_scale, m_w_out, m_ln1_g, m_ln1_b, m_w_mlp_in, m_w_mlp_out, m_ln2_g, m_ln2_b, v_w_ada, v_b_ada, v_w_in, v_conv_w, v_w_pool, v_pool_scale, v_w_out, v_ln1_g, v_ln1_b, v_w_mlp_in, v_w_mlp_out, v_ln2_g, v_ln2_b):
    T, D = x.shape[1], x.shape[2]
    H = w_mlp_out.shape[1] * N_DEV
    ZW = w_in.shape[2] * N_DEV
    CC = ZW // 4
    tm = min(512, T // 2)
    bt = min(1024, T)
    ax, ay, ac = _my_place()
    me = _slot(ax, ay, ac)

    w_cols = w_ada.shape[2]
    b_mine = lax.dynamic_slice(b_ada, (0, me * w_cols), (1, w_cols))
    w_in_f, w_out_g, cw_g, c_g, mod_g = _prologue(w_in, w_out, conv_w[0], c, w_ada, b_mine)
    w_out_f = w_out_g.reshape(D, D)
    conv_w_f = jnp.transpose(cw_g, (1, 0, 2)).reshape(conv_w.shape[1], CC)
    c_all = c_g.reshape(N_DEV, D)
    mod = lax.dynamic_index_in_dim(mod_g, me, axis=1, keepdims=False).reshape(6, D)

    ln = jnp.concatenate([ln1_g, ln1_b, ln2_g, ln2_b], axis=0)
    xs, target = x[0], loss_target[0]

    z, h, xhat1, rstd1, mix, w_mi_f, w_mo_g = _f1(
        xs, mod, w_in_f, conv_w_f, w_pool[0], pool_scale, w_out_f, tm,
        [w_mlp_in, w_mlp_out], [True, False])
    dx1, h2, a, du, df, acc2 = _fb2(xhat1, target, mod, ln, w_mi_f, w_mo_g.reshape(H, D), tm)

    grad_x, acc1, gcw, gwp, gp_in, gp_out = _b1(
        dx1, xhat1, rstd1, xs, mix, z, h, mod, ln, w_out_f, w_in_f, conv_w_f, w_pool[0], pool_scale, tm // 2)

    gp_mo, rv_in, rv_out = _wgrad(a, df, D, 1, bt, "wgrad_mlp_out", scatter=[gp_in, gp_out.reshape(N_DEV, D // N_DEV, D)])
    gp_mi, rv_mo = _wgrad(h2, du, D, 2, bt, "wgrad_mlp_in", owners=N_DEV, scatter=[gp_mo.reshape(N_DEV, H // N_DEV, D)])

    acc1_g, acc2_g, gcw_g, gwp_g, rv_mi = _allgather([acc1, acc2, gcw, gwp], "gather_small", in_vmem=True, scatter=gp_mi)
    g_b_ada, g_w_ada, g_ln, g_cw, g_w_pool, loss_row = _small_grads(
        jnp.transpose(acc1_g, (1, 0, 2)), jnp.transpose(acc2_g, (1, 0, 2)), jnp.transpose(gcw_g, (1, 0, 2)), gwp_g,
        c_all.T, jnp.reshape(me, (1,)).astype(jnp.int32), w_cols)
    cc_mine = conv_w.shape[2]
    g_conv_w = lax.dynamic_slice(g_cw, (0, me * cc_mine), (conv_w.shape[1], cc_mine))[None]
    g_pool_scale = g_cw[3:4, :]
    g_ln1_g, g_ln1_b, g_ln2_g, g_ln2_b = g_ln[0:1], g_ln[1:2], g_ln[2:3], g_ln[3:4]

    small = _adamw_multi(
        [
            (b_ada, g_b_ada, m_b_ada, v_b_ada),
            (conv_w, g_conv_w, m_conv_w, v_conv_w),
            (w_pool, g_w_pool, m_w_pool, v_w_pool),
            (pool_scale, g_pool_scale, m_pool_scale, v_pool_scale),
            (ln1_g, g_ln1_g, m_ln1_g, v_ln1_g),
            (ln1_b, g_ln1_b, m_ln1_b, v_ln1_b),
            (ln2_g, g_ln2_g, m_ln2_g, v_ln2_g),
            (ln2_b, g_ln2_b, m_ln2_b, v_ln2_b),
        ],
        "adamw_small")
    u_b_ada, u_conv_w, u_w_pool, u_pool_scale, u_ln1_g, u_ln1_b, u_ln2_g, u_ln2_b = small

    g_w_ada, *u_w_ada = _sum_adamw(g_w_ada, w_ada, m_w_ada, v_w_ada, "adamw_w_ada")
    g_w_mo, *u_w_mo = _sum_adamw(rv_mo, w_mlp_out, m_w_mlp_out, v_w_mlp_out, "sum_w_mlp_out")
    g_w_mi, *u_w_mi = _sum_adamw(rv_mi, w_mlp_in, m_w_mlp_in, v_w_mlp_in, "sum_w_mlp_in")
    g_w_in, *u_w_in = _sum_adamw(rv_in, w_in, m_w_in, v_w_in, "sum_w_in")
    g_w_out, *u_w_out = _sum_adamw(rv_out, w_out, m_w_out, v_w_out, "sum_w_out")

    grads = [g_w_ada, g_b_ada, g_w_in, g_conv_w, g_w_pool, g_pool_scale, g_w_out, g_ln1_g, g_ln1_b, g_w_mi, g_w_mo, g_ln2_g, g_ln2_b]
    updates = [u_w_ada, u_b_ada, u_w_in, u_conv_w, u_w_pool, u_pool_scale, u_w_out, u_ln1_g, u_ln1_b, u_w_mi, u_w_mo, u_ln2_g, u_ln2_b]
    deltas = [u[0] for u in updates]
    new_m = [u[1] for u in updates]
    new_v = [u[2] for u in updates]
    return (loss_row[0, 0], grad_x[None], *grads, *deltas, *new_m, *new_v)
```

```python
import jax
import jax.numpy as jnp
from jax import lax
from jax.experimental import pallas as pl
from jax.experimental.pallas import tpu as pltpu

F32 = jnp.float32
BF16 = jnp.bfloat16
MESH = pl.DeviceIdType.MESH
N_DEV = 8

LN_EPS = 1e-5
DEPTH = 1
DEEPNORM_ALPHA = (2.0 * DEPTH) ** 0.25
POOL_WINDOWS = (2, 4, 8, 16)
HALO = 16

ADAM_LR = 0.001
ADAM_B1 = 0.9
ADAM_B2 = 0.999
ADAM_EPS = 1e-08
ADAM_WD = 0.01
ADAM_STEP = 10

VMEM_LIMIT = 60 * 1024 * 1024

VMEM_SPEC = pl.BlockSpec(memory_space=pltpu.VMEM)
ANY_SPEC = pl.BlockSpec(memory_space=pl.ANY)

NT = (((1,), (1,)), ((), ()))
TN = (((0,), (0,)), ((), ()))


def _my_place():
    return lax.axis_index("x"), lax.axis_index("y"), lax.axis_index("c")


def _slot(x, y, c):
    return 4 * x + 2 * y + c


def _gather_place(ins, outs, a, slot):
    if len(outs[a].shape) == len(ins[a].shape):
        wb = ins[a].shape[1]
        return outs[a].at[:, pl.ds(pl.multiple_of(slot * wb, wb), wb)]
    return outs[a].at[slot]


def _gather_copy(ins, outs, sems, a, k, block, to, from_shard=False):
    send_sems, recv_sems, _ = sems
    dst = _gather_place(ins, outs, a, _slot(*block))
    return pltpu.make_async_remote_copy(
        src_ref=ins[a] if from_shard else dst,
        dst_ref=dst,
        send_sem=send_sems.at[7 * a + k],
        recv_sem=recv_sems.at[7 * a + k],
        device_id=to,
        device_id_type=MESH,
    )


def _gather_peers():
    x, y, c = _my_place()
    return (x, y, c), (x, y, 1 - c), [(1 - x, y), (x, 1 - y), (1 - x, 1 - y)]


def _gather_first(ins, outs, sems):
    me, sibling, chips = _gather_peers()
    first = []
    for a in range(len(ins)):
        first.append(_gather_copy(ins, outs, sems, a, 0, me, sibling, from_shard=True))
        first += [_gather_copy(ins, outs, sems, a, 1 + j, me, (*chip, me[2]), from_shard=True) for j, chip in enumerate(chips)]
    return first


def _gather_mine(ins, outs, sems, a):
    me, _, _ = _gather_peers()
    return pltpu.make_async_copy(ins[a], _gather_place(ins, outs, a, _slot(*me)), sems[2].at[a])


def _gather_start(ins, outs, sems):
    for a in range(len(ins)):
        _gather_mine(ins, outs, sems, a).start()
    for cp in _gather_first(ins, outs, sems):
        cp.start()


def _gather_forward(ins, outs, sems, j):
    me, sibling, chips = _gather_peers()
    for a in range(len(ins)):
        _gather_copy(ins, outs, sems, a, 1 + j, (*chips[j], me[2]), me).wait_recv()
        _gather_copy(ins, outs, sems, a, 4 + j, (*chips[j], me[2]), sibling).start()


def _gather_finish(ins, outs, sems):
    me, sibling, chips = _gather_peers()
    for a in range(len(ins)):
        _gather_copy(ins, outs, sems, a, 0, sibling, me).wait_recv()
        for j, chip in enumerate(chips):
            _gather_copy(ins, outs, sems, a, 4 + j, (*chip, 1 - me[2]), me).wait_recv()
    for cp in _gather_first(ins, outs, sems):
        cp.wait_send()
    for a in range(len(ins)):
        for j, chip in enumerate(chips):
            _gather_copy(ins, outs, sems, a, 4 + j, (*chip, me[2]), sibling).wait_send()
        _gather_mine(ins, outs, sems, a).wait()


def _gather_scratch(n):
    return [pltpu.SemaphoreType.DMA((7 * n,)), pltpu.SemaphoreType.DMA((7 * n,)), pltpu.SemaphoreType.DMA((n,))]


def _gather_out_shape(shards, by_cols):
    return [
        jax.ShapeDtypeStruct((s.shape[0], N_DEV * s.shape[1]) if cols else (N_DEV, *s.shape), s.dtype)
        for s, cols in zip(shards, by_cols)
    ]


N_CHIP = 4


def _scatter_scratch(rows, cols):
    block = pltpu.VMEM((N_CHIP, rows, cols), BF16)
    dma = pltpu.SemaphoreType.DMA
    return [block, block, block, dma((N_CHIP,)), dma((N_CHIP,)), dma((N_CHIP,)), dma((N_CHIP - 1,)), dma((N_CHIP - 1,)), dma]


def _scatter_pair_copies(g_hbm, scr):
    x, y, c = _my_place()
    mine, theirs, _, a_send, a_recv, load_sem = scr[:6]
    to_sibling = [
        pltpu.make_async_remote_copy(
            src_ref=g_hbm.at[2 * q + (1 - c)], dst_ref=theirs.at[q], send_sem=a_send.at[q], recv_sem=a_recv.at[q],
            device_id=(x, y, 1 - c), device_id_type=MESH)
        for q in range(N_CHIP)
    ]
    loads = [pltpu.make_async_copy(g_hbm.at[2 * q + c], mine.at[q], load_sem.at[q]) for q in range(N_CHIP)]
    return to_sibling, loads


def _scatter_sum_copies(recv, scr):
    x, y, c = _my_place()
    sums, b_send, b_recv, own_sem = scr[2], scr[6], scr[7], scr[8]
    q_me = 2 * x + y
    to_owner = [
        pltpu.make_async_remote_copy(
            src_ref=sums.at[2 * px + py], dst_ref=recv.at[q_me], send_sem=b_send.at[j], recv_sem=b_recv.at[j],
            device_id=(px, py, c), device_id_type=MESH)
        for j, (px, py) in enumerate([(1 - x, y), (x, 1 - y), (1 - x, 1 - y)])
    ]
    return to_owner, pltpu.make_async_copy(sums.at[q_me], recv.at[q_me], own_sem)


def _scatter_start(g_hbm, scr):
    to_sibling, loads = _scatter_pair_copies(g_hbm, scr)
    for cp in to_sibling + loads:
        cp.start()


def _scatter_middle(g_hbm, recv, scr):
    to_sibling, loads = _scatter_pair_copies(g_hbm, scr)
    for cp in to_sibling:
        cp.wait_recv()
    for cp in loads:
        cp.wait()
    mine, theirs, sums = scr[:3]

    def step(r, carry):
        rs = pl.ds(pl.multiple_of(r * ROW_CHUNK, ROW_CHUNK), ROW_CHUNK)
        for q in range(N_CHIP):
            sums[q, rs, :] = (mine[q, rs, :].astype(F32) + theirs[q, rs, :].astype(F32)).astype(BF16)
        return carry

    lax.fori_loop(0, mine.shape[1] // ROW_CHUNK, step, 0)
    to_owner, own = _scatter_sum_copies(recv, scr)
    for cp in to_owner + [own]:
        cp.start()


def _scatter_finish(g_hbm, recv, scr):
    to_sibling, _ = _scatter_pair_copies(g_hbm, scr)
    to_owner, own = _scatter_sum_copies(recv, scr)
    for cp in to_owner:
        cp.wait_recv()
    for cp in to_sibling + to_owner:
        cp.wait_send()
    own.wait()


def _scatter_out_shape(gparts):
    return jax.ShapeDtypeStruct((N_CHIP, *gparts.shape[1:]), gparts.dtype)


def _adamw_math(w, g, m, v):
    m = ADAM_B1 * m + (1.0 - ADAM_B1) * g
    v = ADAM_B2 * v + (1.0 - ADAM_B2) * (g * g)
    m_hat = m / (1.0 - ADAM_B1**ADAM_STEP)
    v_hat = v / (1.0 - ADAM_B2**ADAM_STEP)
    delta = -ADAM_LR * (m_hat / (jnp.sqrt(v_hat) + ADAM_EPS) + ADAM_WD * w)
    return delta, m, v


ROW_CHUNK = 64


ELEMS_PER_STEP = 128 * 1024


def _reduce_scatter(gparts, name):
    def body(g_hbm, recv, *scr):
        _scatter_start(g_hbm, scr)
        _scatter_middle(g_hbm, recv, scr)
        _scatter_finish(g_hbm, recv, scr)

    return pl.pallas_call(
        body,
        name=name,
        out_shape=_scatter_out_shape(gparts),
        in_specs=[ANY_SPEC],
        out_specs=ANY_SPEC,
        scratch_shapes=_scatter_scratch(*gparts.shape[1:]),
        compiler_params=pltpu.CompilerParams(vmem_limit_bytes=VMEM_LIMIT),
    )(gparts)


def _sum_adamw(parts, w, m, v, name):
    _, rows, cols = w.shape
    rb = rows
    while rb * cols > ELEMS_PER_STEP and rb % 16 == 0:
        rb //= 2

    def body(p_ref, w_ref, m_ref, v_ref, grad_ref, delta_ref, nm_ref, nv_ref):
        g = p_ref[0].astype(F32)
        for k in range(1, p_ref.shape[0]):
            g = g + p_ref[k].astype(F32)
        delta, nm, nv = _adamw_math(w_ref[0], g, m_ref[0], v_ref[0])
        grad_ref[0] = g
        delta_ref[0] = delta
        nm_ref[0] = nm
        nv_ref[0] = nv

    block = lambda lead: pl.BlockSpec((lead, rb, cols), lambda i: (0, i, 0))
    out = jax.ShapeDtypeStruct(w.shape, F32)
    return pl.pallas_call(
        body,
        name=name,
        grid=(rows // rb,),
        out_shape=[out] * 4,
        in_specs=[block(parts.shape[0])] + [block(1)] * 3,
        out_specs=[block(1)] * 4,
        compiler_params=pltpu.CompilerParams(dimension_semantics=("arbitrary",), vmem_limit_bytes=VMEM_LIMIT),
    )(parts, w, m, v)


def _adamw_multi(items, name):
    n = len(items)

    def body(*refs):
        ins, outs = refs[: 4 * n], refs[4 * n :]
        for a in range(n):
            w_ref, g_ref, m_ref, v_ref = ins[4 * a : 4 * a + 4]
            d_ref, nm_ref, nv_ref = outs[3 * a : 3 * a + 3]
            delta, nm, nv = _adamw_math(w_ref[...], g_ref[...], m_ref[...], v_ref[...])
            d_ref[...] = delta
            nm_ref[...] = nm
            nv_ref[...] = nv

    flat = [a for it in items for a in it]
    out_shape = [jax.ShapeDtypeStruct(it[0].shape, F32) for it in items for _ in range(3)]
    outs = pl.pallas_call(
        body,
        name=name,
        out_shape=out_shape,
        in_specs=[VMEM_SPEC] * (4 * n),
        out_specs=[VMEM_SPEC] * (3 * n),
        compiler_params=pltpu.CompilerParams(vmem_limit_bytes=VMEM_LIMIT),
    )(*flat)
    return [tuple(outs[3 * a : 3 * a + 3]) for a in range(n)]


def _prologue(w_in, w_out, conv_w, c, w_ada, b_mine):
    D = c.shape[1]
    wc = w_ada.shape[2]
    shards16 = [jax.ShapeDtypeStruct(w_in.shape[1:], BF16), jax.ShapeDtypeStruct(w_out.shape[1:], BF16)]

    def to_all(src, out, sems):
        x, y, c = _my_place()
        me = _slot(x, y, c)
        copies = [
            pltpu.make_async_remote_copy(
                src_ref=src, dst_ref=out.at[me], send_sem=sems[0].at[k - 1], recv_sem=sems[1].at[k - 1],
                device_id=(x ^ (k >> 2), y ^ ((k >> 1) & 1), c ^ (k & 1)), device_id_type=MESH)
            for k in range(1, N_DEV)
        ]
        return copies, pltpu.make_async_copy(src, out.at[me], sems[2].at[0])

    def start(copies, own):
        for cp in copies + [own]:
            cp.start()

    def finish(copies, own):
        for cp in copies:
            cp.wait()
        own.wait()

    def body(win_ref, wout_ref, cw_ref, c_ref, wada_ref, b_ref, win_g, wout_g, cw_g, c_g, mod_g, win16, wout16, c_s, mp_s, *sems):
        c_copies = to_all(c_ref, c_g, sems[3:6])
        start(*c_copies)
        win16[...] = win_ref[0].astype(BF16)
        wout16[...] = wout_ref[0].astype(BF16)
        w_ins, w_outs, w_sems = (win16, wout16, cw_ref), (win_g, wout_g, cw_g), sems[0:3]
        _gather_start(w_ins, w_outs, w_sems)
        finish(*c_copies)
        for k in range(N_DEV):
            c_s[k : k + 1, :] = c_g[k]
        cv = c_s[...]
        cond = cv * jax.nn.sigmoid(cv)
        mp_s[...] = jnp.dot(cond, wada_ref[0], precision=lax.Precision.HIGHEST, preferred_element_type=F32) + b_ref[...]
        m_copies = to_all(mp_s, mod_g, sems[6:9])
        start(*m_copies)
        for j in range(3):
            _gather_forward(w_ins, w_outs, w_sems, j)
        _gather_finish(w_ins, w_outs, w_sems)
        finish(*m_copies)

    return pl.pallas_call(
        body,
        name="prologue",
        out_shape=_gather_out_shape(shards16 + [conv_w], [True, False, False])
        + [jax.ShapeDtypeStruct((N_DEV, 1, D), F32), jax.ShapeDtypeStruct((N_DEV, N_DEV, wc), F32)],
        in_specs=[VMEM_SPEC, VMEM_SPEC, ANY_SPEC] + [VMEM_SPEC] * 3,
        out_specs=[ANY_SPEC] * 3 + [VMEM_SPEC] * 2,
        scratch_shapes=[pltpu.VMEM(s.shape, BF16) for s in shards16]
        + [pltpu.VMEM((N_DEV, D), F32), pltpu.VMEM((N_DEV, wc), F32)]
        + _gather_scratch(3) + _gather_scratch(1) + _gather_scratch(1),
        compiler_params=pltpu.CompilerParams(vmem_limit_bytes=VMEM_LIMIT),
    )(w_in, w_out, conv_w, c, w_ada, b_mine)


def _ln_fwd(r):
    mu = jnp.mean(r, axis=-1, keepdims=True)
    d = r - mu
    var = jnp.mean(d * d, axis=-1, keepdims=True)
    rstd = lax.rsqrt(var + LN_EPS)
    return d * rstd, rstd


def _ln_bwd(dxh, xhat, rstd):
    m1 = jnp.mean(dxh, axis=-1, keepdims=True)
    m2 = jnp.mean(dxh * xhat, axis=-1, keepdims=True)
    return rstd * (dxh - m1 - xhat * m2)


def _colsum(a):
    return jnp.sum(a, axis=0, keepdims=True)


def _window_sums(ext, tm, causal):
    n = ext.shape[0]
    lo = HALO if causal else 0
    s, out = ext, []
    for p in range(len(POOL_WINDOWS)):
        assert POOL_WINDOWS[p] == 2 ** (p + 1)
        k = 2**p
        s = s + pltpu.roll(s, k if causal else n - k, 0)
        out.append(s[lo : lo + tm, 0:128])
        if p + 1 < len(POOL_WINDOWS):
            s = s[:, 128:]
    return out


def _pool_features(vp, vp_s, row, tm):
    sums = _window_sums(vp_s[...], tm, causal=True)
    feats, inv_cnts = [], []
    for g, win in enumerate(POOL_WINDOWS):
        inv_cnt = 1.0 / jnp.minimum(row + 1, win).astype(F32)
        feats.append(sums[g] * inv_cnt - vp[:, 128 * g : 128 * g + 128])
        inv_cnts.append(inv_cnt)
    return feats, inv_cnts


def _f1(x, mod, w_in, conv_w, w_pool, pool_scale, w_out, tm, gather, by_cols):
    T, D = x.shape
    ZW = w_in.shape[1]
    CC = ZW // 4
    nt = T // tm
    ng = len(gather)
    fwd_steps = [max(nt - 3 + j, 0) for j in range(3)]

    shards16 = [jax.ShapeDtypeStruct(s.shape[1:], BF16) for s in gather]

    def body(*refs):
        x_ref, mod_ref, win_ref, cw_ref, wp_ref, ps_ref, wout_ref = refs[:7]
        g_f32 = refs[7 : 7 + ng]
        z_ref, h_ref, xhat_ref, rstd_ref, mix_ref = refs[7 + ng : 12 + ng]
        g_outs = refs[12 + ng : 12 + 2 * ng]
        cv_s, vp_s = refs[12 + 2 * ng : 14 + 2 * ng]
        g_ins = refs[14 + 2 * ng : 14 + 3 * ng]
        g_sems = refs[14 + 3 * ng :]
        i = pl.program_id(0)

        @pl.when(i == 0)
        def _():
            for src, dst in zip(g_f32, g_ins):
                dst[...] = src[0].astype(BF16)
            _gather_start(g_ins, g_outs, g_sems)
            cv_s[0:HALO, :] = jnp.zeros((HALO, CC), F32)
            vp_s[0:HALO, :] = jnp.zeros((HALO, CC), F32)

        xv = x_ref[...]
        sh1, sc1, g1 = mod_ref[0:1, :], mod_ref[1:2, :], mod_ref[2:3, :]
        h = (xv * (1.0 + sc1) + sh1).astype(BF16)
        h_ref[...] = h
        z = jnp.dot(h, win_ref[...], preferred_element_type=F32)
        z_ref[...] = z.astype(BF16)
        gb, gc, vc, vp = z[:, 0:CC], z[:, CC : 2 * CC], z[:, 2 * CC : 3 * CC], z[:, 3 * CC : 4 * CC]
        cv = gc * vc
        cv_s[HALO : HALO + tm, :] = cv
        vp_s[HALO : HALO + tm, :] = vp
        conv = cw_ref[0:1, :] * cv_s[HALO - 2 : HALO - 2 + tm, :] + cw_ref[1:2, :] * cv_s[HALO - 1 : HALO - 1 + tm, :] + cw_ref[2:3, :] * cv
        parts = [gb * conv]
        row = i * tm + lax.broadcasted_iota(jnp.int32, (tm, 1), 0)
        feats, _ = _pool_features(vp, vp_s, row, tm)
        for g in range(len(POOL_WINDOWS)):
            pw = jnp.dot(feats[g].astype(BF16), wp_ref[g].astype(BF16), preferred_element_type=F32)
            parts.append(pw * ps_ref[0:1, 128 * g : 128 * g + 128])
        cv_s[0:HALO, :] = cv_s[tm : tm + HALO, :]
        vp_s[0:HALO, :] = vp_s[tm : tm + HALO, :]
        ycat = jnp.concatenate(parts, axis=1).astype(BF16)
        mix = jnp.dot(ycat, wout_ref[...], preferred_element_type=F32)
        mix_ref[...] = mix
        xhat, rstd = _ln_fwd(DEEPNORM_ALPHA * xv + (1.0 + g1) * mix)
        xhat_ref[...] = xhat
        rstd_ref[...] = rstd

        for j in range(3):

            @pl.when(i == fwd_steps[j])
            def _(j=j):
                _gather_forward(g_ins, g_outs, g_sems, j)

        @pl.when(i == nt - 1)
        def _():
            _gather_finish(g_ins, g_outs, g_sems)

    tile = lambda w: pl.BlockSpec((tm, w), lambda i: (i, 0))
    return pl.pallas_call(
        body,
        name="f1",
        grid=(nt,),
        out_shape=[
            jax.ShapeDtypeStruct((T, ZW), BF16),
            jax.ShapeDtypeStruct((T, D), BF16),
            jax.ShapeDtypeStruct((T, D), F32),
            jax.ShapeDtypeStruct((T, 1), F32),
            jax.ShapeDtypeStruct((T, D), F32),
        ]
        + _gather_out_shape(shards16, by_cols),
        in_specs=[tile(D)] + [VMEM_SPEC] * (6 + ng),
        out_specs=[tile(ZW), tile(D), tile(D), tile(1), tile(D)] + [ANY_SPEC] * ng,
        scratch_shapes=[pltpu.VMEM((HALO + tm, CC), F32), pltpu.VMEM((HALO + tm, CC), F32)]
        + [pltpu.VMEM(s.shape, BF16) for s in shards16]
        + _gather_scratch(ng),
        compiler_params=pltpu.CompilerParams(dimension_semantics=("arbitrary",), vmem_limit_bytes=VMEM_LIMIT),
    )(x, mod, w_in, conv_w, w_pool, pool_scale, w_out, *gather)


def _fb2(xhat1, target, mod, ln, w_mi, w_mo, tm):
    T, D = xhat1.shape
    H = w_mi.shape[1]
    hc = min(1024, H)
    nb = H // hc
    nt = T // tm

    def body(xh1_ref, t_ref, mod_ref, ln_ref, wmi_ref, wmo_ref, dx1_ref, h2_ref, a_ref, du_ref, df_ref, acc_ref):
        i = pl.program_id(0)

        @pl.when(i == 0)
        def _():
            acc_ref[...] = jnp.zeros((8, D), F32)

        sh2, sc2, g2 = mod_ref[3:4, :], mod_ref[4:5, :], mod_ref[5:6, :]
        x1 = xh1_ref[...] * ln_ref[0:1, :] + ln_ref[1:2, :]
        h2 = (x1 * (1.0 + sc2) + sh2).astype(BF16)
        h2_ref[...] = h2
        f = jnp.zeros((tm, D), F32)
        for k in range(nb):
            ks = slice(k * hc, (k + 1) * hc)
            r = jnp.maximum(jnp.dot(h2, wmi_ref[:, ks], preferred_element_type=F32), 0.0)
            du_ref[:, ks] = r.astype(BF16)
            a = (r * r).astype(BF16)
            a_ref[:, ks] = a
            f = f + jnp.dot(a, wmo_ref[ks, :], preferred_element_type=F32)
        xhat2, rstd2 = _ln_fwd(DEEPNORM_ALPHA * x1 + (1.0 + g2) * f)
        ln2_g = ln_ref[2:3, :]
        d = xhat2 * ln2_g + ln_ref[3:4, :] - t_ref[...]
        dr2 = _ln_bwd(d * (ln2_g * (1.0 / D)), xhat2, rstd2)
        df = ((1.0 + g2) * dr2).astype(BF16)
        df_ref[...] = df
        dh2 = jnp.zeros((tm, D), F32)
        for k in range(nb):
            ks = slice(k * hc, (k + 1) * hc)
            da = lax.dot_general(df, wmo_ref[ks, :], NT, preferred_element_type=F32)
            du = (da * (2.0 * du_ref[:, ks].astype(F32))).astype(BF16)
            du_ref[:, ks] = du
            dh2 = dh2 + lax.dot_general(du, wmi_ref[:, ks], NT, preferred_element_type=F32)
        dx1_ref[...] = DEEPNORM_ALPHA * dr2 + dh2 * (1.0 + sc2)
        acc_ref[0:1, :] += _colsum(d * xhat2) * (1.0 / D)
        acc_ref[1:2, :] += _colsum(d) * (1.0 / D)
        acc_ref[2:3, :] += _colsum(dh2)
        acc_ref[3:4, :] += _colsum(dh2 * x1)
        acc_ref[4:5, :] += _colsum(dr2 * f)
        acc_ref[5:6, :] += jnp.zeros((1, D), F32) + (0.5 / D) * jnp.sum(d * d)

    tile = lambda w: pl.BlockSpec((tm, w), lambda i: (i, 0))
    return pl.pallas_call(
        body,
        name="fb2",
        grid=(nt,),
        out_shape=[
            jax.ShapeDtypeStruct((T, D), F32),
            jax.ShapeDtypeStruct((T, D), BF16),
            jax.ShapeDtypeStruct((T, H), BF16),
            jax.ShapeDtypeStruct((T, H), BF16),
            jax.ShapeDtypeStruct((T, D), BF16),
            jax.ShapeDtypeStruct((8, D), F32),
        ],
        in_specs=[tile(D), tile(D)] + [VMEM_SPEC] * 4,
        out_specs=[tile(D), tile(D), tile(H), tile(H), tile(D), pl.BlockSpec((8, D), lambda i: (0, 0))],
        compiler_params=pltpu.CompilerParams(dimension_semantics=("arbitrary",), vmem_limit_bytes=VMEM_LIMIT),
    )(xhat1, target, mod, ln, w_mi, w_mo)


def _b1(dx1, xhat1, rstd1, x, mix, z, mod, ln, w_out, w_in, conv_w, w_pool, pool_scale, tm):
    T, D = x.shape
    ZW = w_in.shape[1]
    CC = ZW // 4
    nt = T // tm
    hb = tm // HALO

    def body(dx1_ref, xh1_ref, rstd_ref, x_ref, mix_ref, z_ref, zh_ref, mod_ref, ln_ref, wout_ref, win_ref, cw_ref, wp_ref, ps_ref,
             dx_ref, dmix_ref, ycat_ref, dz_ref, acc_ref, gcw_ref, gwp_ref, cv_s, vp_s, e_s, q_s):
        i = pl.program_id(0)
        j = nt - 1 - i

        @pl.when(i == 0)
        def _():
            acc_ref[...] = jnp.zeros((8, D), F32)
            gcw_ref[...] = jnp.zeros((8, CC), F32)
            gwp_ref[...] = jnp.zeros(gwp_ref.shape, F32)
            e_s[tm : tm + HALO, :] = jnp.zeros((HALO, CC), F32)
            q_s[tm : tm + HALO, :] = jnp.zeros((HALO, CC), F32)

        sh1, sc1, g1 = mod_ref[0:1, :], mod_ref[1:2, :], mod_ref[2:3, :]
        dx1 = dx1_ref[...]
        xhat1 = xh1_ref[...]
        acc_ref[0:1, :] += _colsum(dx1 * xhat1)
        acc_ref[1:2, :] += _colsum(dx1)
        dr1 = _ln_bwd(dx1 * ln_ref[0:1, :], xhat1, rstd_ref[...])
        acc_ref[4:5, :] += _colsum(dr1 * mix_ref[...])
        dmix = ((1.0 + g1) * dr1).astype(BF16)
        dmix_ref[...] = dmix
        dycat = lax.dot_general(dmix, wout_ref[...], NT, preferred_element_type=F32)

        z = z_ref[...].astype(F32)
        zh = zh_ref[...].astype(F32) * jnp.where(j > 0, 1.0, 0.0)
        gb, gc, vc, vp = z[:, 0:CC], z[:, CC : 2 * CC], z[:, 2 * CC : 3 * CC], z[:, 3 * CC : 4 * CC]
        cv = gc * vc
        cv_s[0:HALO, :] = zh[:, CC : 2 * CC] * zh[:, 2 * CC : 3 * CC]
        cv_s[HALO : HALO + tm, :] = cv
        vp_s[0:HALO, :] = zh[:, 3 * CC : 4 * CC]
        vp_s[HALO : HALO + tm, :] = vp
        cv_m2 = cv_s[HALO - 2 : HALO - 2 + tm, :]
        cv_m1 = cv_s[HALO - 1 : HALO - 1 + tm, :]
        w0, w1, w2 = cw_ref[0:1, :], cw_ref[1:2, :], cw_ref[2:3, :]
        conv = w0 * cv_m2 + w1 * cv_m1 + w2 * cv
        dyc = dycat[:, 0:CC]
        e = dyc * gb
        e_s[0:tm, :] = e
        dcv = w2 * e + w1 * e_s[1 : 1 + tm, :] + w0 * e_s[2 : 2 + tm, :]
        gcw_ref[0:1, :] += _colsum(e * cv_m2)
        gcw_ref[1:2, :] += _colsum(e * cv_m1)
        gcw_ref[2:3, :] += _colsum(e * cv)
        y_parts = [gb * conv]
        dz_parts = [dyc * conv, dcv * vc, dcv * gc]

        row = j * tm + lax.broadcasted_iota(jnp.int32, (tm, 1), 0)
        feats, inv_cnts = _pool_features(vp, vp_s, row, tm)
        gps_parts, dps = [], []
        for g in range(len(POOL_WINDOWS)):
            cols = slice(128 * g, 128 * g + 128)
            p = feats[g].astype(BF16)
            scale = ps_ref[0:1, cols]
            wp = wp_ref[g].astype(BF16)
            pw = jnp.dot(p, wp, preferred_element_type=F32)
            y_parts.append(pw * scale)
            dyp = dycat[:, CC + 128 * g : CC + 128 * g + 128]
            gps_parts.append(_colsum(dyp * pw))
            dpw = (dyp * scale).astype(BF16)
            gwp_ref[g] += lax.dot_general(p, dpw, TN, preferred_element_type=F32)
            dp = lax.dot_general(dpw, wp, NT, preferred_element_type=F32)
            q_s[0:tm, cols] = dp * inv_cnts[g]
            dps.append(dp)
        sq = _window_sums(q_s[...], tm, causal=False)
        dz_parts += [sq[g] - dps[g] for g in range(len(POOL_WINDOWS))]
        gcw_ref[3:4, :] += jnp.concatenate(gps_parts, axis=1)
        ycat_ref[...] = jnp.concatenate(y_parts, axis=1).astype(BF16)
        dz = jnp.concatenate(dz_parts, axis=1).astype(BF16)
        dz_ref[...] = dz
        dh = lax.dot_general(dz, win_ref[...], NT, preferred_element_type=F32)
        acc_ref[2:3, :] += _colsum(dh)
        acc_ref[3:4, :] += _colsum(dh * x_ref[...])
        dx_ref[...] = DEEPNORM_ALPHA * dr1 + dh * (1.0 + sc1)
        e_s[tm : tm + HALO, :] = e_s[0:HALO, :]
        q_s[tm : tm + HALO, :] = q_s[0:HALO, :]

    tile = lambda w: pl.BlockSpec((tm, w), lambda i: (nt - 1 - i, 0))
    halo = pl.BlockSpec((HALO, ZW), lambda i: (jnp.maximum((nt - 1 - i) * hb - 1, 0), 0))
    fixed = lambda shape: pl.BlockSpec(shape, lambda i: (0,) * len(shape))
    return pl.pallas_call(
        body,
        name="b1",
        grid=(nt,),
        out_shape=[
            jax.ShapeDtypeStruct((T, D), F32),
            jax.ShapeDtypeStruct((T, D), BF16),
            jax.ShapeDtypeStruct((T, D), BF16),
            jax.ShapeDtypeStruct((T, ZW), BF16),
            jax.ShapeDtypeStruct((8, D), F32),
            jax.ShapeDtypeStruct((8, CC), F32),
            jax.ShapeDtypeStruct(w_pool.shape, F32),
        ],
        in_specs=[tile(D), tile(D), tile(1), tile(D), tile(D), tile(ZW), halo] + [VMEM_SPEC] * 7,
        out_specs=[tile(D), tile(D), tile(D), tile(ZW), fixed((8, D)), fixed((8, CC)), fixed(w_pool.shape)],
        scratch_shapes=[
            pltpu.VMEM((HALO + tm, CC), F32),
            pltpu.VMEM((HALO + tm, CC), F32),
            pltpu.VMEM((tm + HALO, CC), F32),
            pltpu.VMEM((tm + HALO, CC), F32),
        ],
        compiler_params=pltpu.CompilerParams(dimension_semantics=("arbitrary",), vmem_limit_bytes=VMEM_LIMIT),
    )(dx1, xhat1, rstd1, x, mix, z, z, mod, ln, w_out, w_in, conv_w, w_pool, pool_scale)


N_SCATTER_SCRATCH = 9


def _wgrad(a, b, bk, n_groups, bt, name, owners=None, scatter=(), gather=()):
    T, K = a.shape
    N = b.shape[1]
    nk, nt, ng = K // bk, T // bt, N // n_groups
    nc = min(512, ng)
    ns, ngat = len(scatter), len(gather)
    n_steps = nk * n_groups * nt
    mid_step = min(1, n_steps - 1)
    fwd_steps = [min(2 * (j + 1), n_steps - 1) for j in range(3)]

    def body(*refs):
        a_ref, b_ref = refs[0], refs[1]
        s_hbm = refs[2 : 2 + ns]
        g_ins = refs[2 + ns : 2 + ns + ngat]
        outs = refs[2 + ns + ngat :]
        o_ref, s_recv, g_outs = outs[0], outs[1 : 1 + ns], outs[1 + ns : 1 + ns + ngat]
        scr = outs[1 + ns + ngat :]
        acc = scr[0]
        s_scr = [scr[1 + N_SCATTER_SCRATCH * k : 1 + N_SCATTER_SCRATCH * (k + 1)] for k in range(ns)]
        g_sems = scr[1 + N_SCATTER_SCRATCH * ns :]
        kk, gg, t = pl.program_id(0), pl.program_id(1), pl.program_id(2)
        step = (kk * n_groups + gg) * nt + t

        if ngat:

            @pl.when(step == 0)
            def _():
                _gather_start(g_ins, g_outs, g_sems)

            for j in range(3):

                @pl.when(step == fwd_steps[j])
                def _(j=j):
                    _gather_forward(g_ins, g_outs, g_sems, j)

        if ns:

            @pl.when(step == 0)
            def _():
                for k in range(ns):
                    _scatter_start(s_hbm[k], s_scr[k])

            @pl.when(step == mid_step)
            def _():
                for k in range(ns):
                    _scatter_middle(s_hbm[k], s_recv[k], s_scr[k])

        @pl.when(t == 0)
        def _():
            acc[...] = jnp.zeros(acc.shape, F32)

        at = a_ref[...].T
        for c in range(ng // nc):
            cs = slice(c * nc, (c + 1) * nc)
            acc[:, cs] += jnp.dot(at, b_ref[:, cs], preferred_element_type=F32)

        @pl.when(t == nt - 1)
        def _():
            if owners is None:
                o_ref[...] = acc[...].astype(BF16)
            else:
                per = N // owners
                for o in range(ng // per):
                    o_ref[o] = acc[:, o * per : (o + 1) * per].astype(BF16)

        if ns:

            @pl.when(step == n_steps - 1)
            def _():
                for k in range(ns):
                    _scatter_finish(s_hbm[k], s_recv[k], s_scr[k])

        if ngat:

            @pl.when(step == n_steps - 1)
            def _():
                _gather_finish(g_ins, g_outs, g_sems)

    if owners is None:
        out_shape = [jax.ShapeDtypeStruct((K, N), BF16)]
        out_specs = [pl.BlockSpec((bk, ng), lambda k, g, t: (k, g))]
    else:
        assert bk == K
        per = N // owners
        out_shape = [jax.ShapeDtypeStruct((owners, K, per), BF16)]
        out_specs = [pl.BlockSpec((ng // per, K, per), lambda k, g, t: (g, 0, 0))]
    ins, in_specs = [a, b], [pl.BlockSpec((bt, bk), lambda k, g, t: (t, k)), pl.BlockSpec((bt, ng), lambda k, g, t: (t, g))]
    scratch = [pltpu.VMEM((bk, ng), F32)]
    for s in scatter:
        ins.append(s)
        in_specs.append(ANY_SPEC)
        out_shape.append(_scatter_out_shape(s))
        out_specs.append(ANY_SPEC)
        scratch += _scatter_scratch(*s.shape[1:])
    if ngat:
        ins += list(gather)
        in_specs += [ANY_SPEC] * ngat
        out_shape += _gather_out_shape(gather, [False] * ngat)
        out_specs += [ANY_SPEC] * ngat
        scratch += _gather_scratch(ngat)
    outs = pl.pallas_call(
        body,
        name=name,
        grid=(nk, n_groups, nt),
        out_shape=out_shape,
        in_specs=in_specs,
        out_specs=out_specs,
        scratch_shapes=scratch,
        compiler_params=pltpu.CompilerParams(dimension_semantics=("arbitrary", "arbitrary", "arbitrary"), vmem_limit_bytes=VMEM_LIMIT),
    )(*ins)
    return outs if ns + ngat else outs[0]


def _small_grads(acc1_t, acc2_t, gcw_t, gwp_all, cond_t, my_slot, w_cols):
    D = acc1_t.shape[2]
    n_chunk = D // 128
    q_mine = w_cols // 128

    def total(ref, r):
        s = ref[r, 0:1, :]
        for k in range(1, N_DEV):
            s = s + ref[r, k : k + 1, :]
        return s

    def body(slot_ref, a1_ref, a2_ref, gcw_ref, gwp_ref, ct_ref, gb_ref, gw_ref, gln_ref, gcwo_ref, gwpo_ref, loss_ref, dm_s):
        loss_ref[...] = total(a2_ref, 5)
        for s, (ref, r) in enumerate([(a1_ref, 2), (a1_ref, 3), (a1_ref, 4), (a2_ref, 2), (a2_ref, 3), (a2_ref, 4)]):
            gb_ref[0:1, s * D : (s + 1) * D] = total(ref, r)
            for qq in range(n_chunk):
                dm_s[s * n_chunk + qq] = ref[r, :, 128 * qq : 128 * qq + 128]
        gln_ref[0:1, :] = total(a1_ref, 0)
        gln_ref[1:2, :] = total(a1_ref, 1)
        gln_ref[2:3, :] = total(a2_ref, 0)
        gln_ref[3:4, :] = total(a2_ref, 1)
        gcwo_ref[...] = jnp.zeros(gcwo_ref.shape, F32)
        for r in range(4):
            gcwo_ref[r : r + 1, :] = total(gcw_ref, r)
        wp = gwp_ref[0]
        for k in range(1, N_DEV):
            wp = wp + gwp_ref[k]
        gwpo_ref[0] = wp
        ct = ct_ref[...]
        cond_t = ct * jax.nn.sigmoid(ct)
        q0 = slot_ref[0] * q_mine
        for q in range(q_mine):
            dm = dm_s[q0 + q]
            out = cond_t[:, 0:1] * dm[0:1, :]
            for k in range(1, N_DEV):
                out = out + cond_t[:, k : k + 1] * dm[k : k + 1, :]
            gw_ref[0, :, 128 * q : 128 * q + 128] = out

    CC = gcw_t.shape[2]
    return pl.pallas_call(
        body,
        name="small_grads",
        out_shape=[
            jax.ShapeDtypeStruct((1, 6 * D), F32),
            jax.ShapeDtypeStruct((1, D, w_cols), F32),
            jax.ShapeDtypeStruct((4, D), F32),
            jax.ShapeDtypeStruct((8, CC), F32),
            jax.ShapeDtypeStruct((1, *gwp_all.shape[1:]), F32),
            jax.ShapeDtypeStruct((1, D), F32),
        ],
        in_specs=[pl.BlockSpec(memory_space=pltpu.SMEM)] + [VMEM_SPEC] * 5,
        out_specs=[VMEM_SPEC] * 6,
        scratch_shapes=[pltpu.VMEM((6 * n_chunk, N_DEV, 128), F32)],
        compiler_params=pltpu.CompilerParams(vmem_limit_bytes=VMEM_LIMIT),
    )(my_slot, acc1_t, acc2_t, gcw_t, gwp_all, cond_t)


def kernel(x, c, w_ada, b_ada, w_in, conv_w, w_pool, pool_scale, w_out, ln1_g, ln1_b, w_mlp_in, w_mlp_out, ln2_g, ln2_b, loss_target, m_w_ada, m_b_ada, m_w_in, m_conv_w, m_w_pool, m_pool_scale, m_w_out, m_ln1_g, m_ln1_b, m_w_mlp_in, m_w_mlp_out, m_ln2_g, m_ln2_b, v_w_ada, v_b_ada, v_w_in, v_conv_w, v_w_pool, v_pool_scale, v_w_out, v_ln1_g, v_ln1_b, v_w_mlp_in, v_w_mlp_out, v_ln2_g, v_ln2_b):
    T, D = x.shape[1], x.shape[2]
    H = w_mlp_out.shape[1] * N_DEV
    ZW = w_in.shape[2] * N_DEV
    CC = ZW // 4
    tm = min(512, T // 2)
    bt = min(1024, T)
    ax, ay, ac = _my_place()
    me = _slot(ax, ay, ac)

    w_cols = w_ada.shape[2]
    b_mine = lax.dynamic_slice(b_ada, (0, me * w_cols), (1, w_cols))
    w_in_f, w_out_g, cw_g, c_g, mod_g = _prologue(w_in, w_out, conv_w[0], c, w_ada, b_mine)
    w_out_f = w_out_g.reshape(D, D)
    conv_w_f = jnp.transpose(cw_g, (1, 0, 2)).reshape(conv_w.shape[1], CC)
    c_all = c_g.reshape(N_DEV, D)
    mod = lax.dynamic_index_in_dim(mod_g, me, axis=1, keepdims=False).reshape(6, D)

    ln = jnp.concatenate([ln1_g, ln1_b, ln2_g, ln2_b], axis=0)
    xs, target = x[0], loss_target[0]

    z, h, xhat1, rstd1, mix, w_mi_f, w_mo_g = _f1(
        xs, mod, w_in_f, conv_w_f, w_pool[0], pool_scale, w_out_f, tm,
        [w_mlp_in, w_mlp_out], [True, False])
    dx1, h2, a, du, df, acc2 = _fb2(xhat1, target, mod, ln, w_mi_f, w_mo_g.reshape(H, D), tm)

    grad_x, dmix, ycat, dz, acc1, gcw, gwp = _b1(
        dx1, xhat1, rstd1, xs, mix, z, mod, ln, w_out_f, w_in_f, conv_w_f, w_pool[0], pool_scale, tm)

    gp_mo, acc1_g, acc2_g, gcw_g, gwp_g = _wgrad(a, df, D, 1, bt, "wgrad_mlp_out", gather=[acc1, acc2, gcw, gwp])
    gp_mi, rv_mo = _wgrad(h2, du, D, 2, bt, "wgrad_mlp_in", owners=N_DEV, scatter=[gp_mo.reshape(N_DEV, H // N_DEV, D)])
    gp_in, rv_mi = _wgrad(h, dz, D, 1, bt, "wgrad_in", owners=N_DEV, scatter=[gp_mi])
    gp_out, rv_in = _wgrad(ycat, dmix, D, 1, bt, "wgrad_out", scatter=[gp_in])
    rv_out = _reduce_scatter(gp_out.reshape(N_DEV, D // N_DEV, D), "scatter_w_out")

    g_b_ada, g_w_ada, g_ln, g_cw, g_w_pool, loss_row = _small_grads(
        jnp.transpose(acc1_g, (1, 0, 2)), jnp.transpose(acc2_g, (1, 0, 2)), jnp.transpose(gcw_g, (1, 0, 2)), gwp_g,
        c_all.T, jnp.reshape(me, (1,)).astype(jnp.int32), w_cols)
    cc_mine = conv_w.shape[2]
    g_conv_w = lax.dynamic_slice(g_cw, (0, me * cc_mine), (conv_w.shape[1], cc_mine))[None]
    g_pool_scale = g_cw[3:4, :]
    g_ln1_g, g_ln1_b, g_ln2_g, g_ln2_b = g_ln[0:1], g_ln[1:2], g_ln[2:3], g_ln[3:4]

    small = _adamw_multi(
        [
            (b_ada, g_b_ada, m_b_ada, v_b_ada),
            (conv_w, g_conv_w, m_conv_w, v_conv_w),
            (w_pool, g_w_pool, m_w_pool, v_w_pool),
            (pool_scale, g_pool_scale, m_pool_scale, v_pool_scale),
            (ln1_g, g_ln1_g, m_ln1_g, v_ln1_g),
            (ln1_b, g_ln1_b, m_ln1_b, v_ln1_b),
            (ln2_g, g_ln2_g, m_ln2_g, v_ln2_g),
            (ln2_b, g_ln2_b, m_ln2_b, v_ln2_b),
        ],
        "adamw_small")
    u_b_ada, u_conv_w, u_w_pool, u_pool_scale, u_ln1_g, u_ln1_b, u_ln2_g, u_ln2_b = small

    g_w_ada, *u_w_ada = _sum_adamw(g_w_ada, w_ada, m_w_ada, v_w_ada, "adamw_w_ada")
    g_w_mo, *u_w_mo = _sum_adamw(rv_mo, w_mlp_out, m_w_mlp_out, v_w_mlp_out, "sum_w_mlp_out")
    g_w_mi, *u_w_mi = _sum_adamw(rv_mi, w_mlp_in, m_w_mlp_in, v_w_mlp_in, "sum_w_mlp_in")
    g_w_in, *u_w_in = _sum_adamw(rv_in, w_in, m_w_in, v_w_in, "sum_w_in")
    g_w_out, *u_w_out = _sum_adamw(rv_out, w_out, m_w_out, v_w_out, "sum_w_out")

    grads = [g_w_ada, g_b_ada, g_w_in, g_conv_w, g_w_pool, g_pool_scale, g_w_out, g_ln1_g, g_ln1_b, g_w_mi, g_w_mo, g_ln2_g, g_ln2_b]
    updates = [u_w_ada, u_b_ada, u_w_in, u_conv_w, u_w_pool, u_pool_scale, u_w_out, u_ln1_g, u_ln1_b, u_w_mi, u_w_mo, u_ln2_g, u_ln2_b]
    deltas = [u[0] for u in updates]
    new_m = [u[1] for u in updates]
    new_v = [u[2] for u in updates]
    return (loss_row[0, 0], grad_x[None], *grads, *deltas, *new_m, *new_v)
```

```python
import jax
import jax.numpy as jnp
from jax import lax
from jax.experimental import pallas as pl
from jax.experimental.pallas import tpu as pltpu

F32 = jnp.float32
BF16 = jnp.bfloat16
MESH = pl.DeviceIdType.MESH
N_DEV = 8

LN_EPS = 1e-5
DEPTH = 1
DEEPNORM_ALPHA = (2.0 * DEPTH) ** 0.25
POOL_WINDOWS = (2, 4, 8, 16)
HALO = 16

ADAM_LR = 0.001
ADAM_B1 = 0.9
ADAM_B2 = 0.999
ADAM_EPS = 1e-08
ADAM_WD = 0.01
ADAM_STEP = 10

VMEM_LIMIT = 60 * 1024 * 1024

VMEM_SPEC = pl.BlockSpec(memory_space=pltpu.VMEM)
ANY_SPEC = pl.BlockSpec(memory_space=pl.ANY)

NT = (((1,), (1,)), ((), ()))
TN = (((0,), (0,)), ((), ()))


def _my_place():
    return lax.axis_index("x"), lax.axis_index("y"), lax.axis_index("c")


def _slot(x, y, c):
    return 4 * x + 2 * y + c


def _gather_place(ins, outs, a, slot):
    if len(outs[a].shape) == len(ins[a].shape):
        wb = ins[a].shape[1]
        return outs[a].at[:, pl.ds(pl.multiple_of(slot * wb, wb), wb)]
    return outs[a].at[slot]


def _gather_copy(ins, outs, sems, a, k, block, to, from_shard=False):
    send_sems, recv_sems, _ = sems
    dst = _gather_place(ins, outs, a, _slot(*block))
    return pltpu.make_async_remote_copy(
        src_ref=ins[a] if from_shard else dst,
        dst_ref=dst,
        send_sem=send_sems.at[7 * a + k],
        recv_sem=recv_sems.at[7 * a + k],
        device_id=to,
        device_id_type=MESH,
    )


def _gather_peers():
    x, y, c = _my_place()
    return (x, y, c), (x, y, 1 - c), [(1 - x, y), (x, 1 - y), (1 - x, 1 - y)]


def _gather_first(ins, outs, sems):
    me, sibling, chips = _gather_peers()
    first = []
    for a in range(len(ins)):
        first.append(_gather_copy(ins, outs, sems, a, 0, me, sibling, from_shard=True))
        first += [_gather_copy(ins, outs, sems, a, 1 + j, me, (*chip, me[2]), from_shard=True) for j, chip in enumerate(chips)]
    return first


def _gather_mine(ins, outs, sems, a):
    me, _, _ = _gather_peers()
    return pltpu.make_async_copy(ins[a], _gather_place(ins, outs, a, _slot(*me)), sems[2].at[a])


def _gather_start(ins, outs, sems):
    for a in range(len(ins)):
        _gather_mine(ins, outs, sems, a).start()
    for cp in _gather_first(ins, outs, sems):
        cp.start()


def _gather_forward(ins, outs, sems, j):
    me, sibling, chips = _gather_peers()
    for a in range(len(ins)):
        _gather_copy(ins, outs, sems, a, 1 + j, (*chips[j], me[2]), me).wait_recv()
        _gather_copy(ins, outs, sems, a, 4 + j, (*chips[j], me[2]), sibling).start()


def _gather_finish(ins, outs, sems):
    me, sibling, chips = _gather_peers()
    for a in range(len(ins)):
        _gather_copy(ins, outs, sems, a, 0, sibling, me).wait_recv()
        for j, chip in enumerate(chips):
            _gather_copy(ins, outs, sems, a, 4 + j, (*chip, 1 - me[2]), me).wait_recv()
    for cp in _gather_first(ins, outs, sems):
        cp.wait_send()
    for a in range(len(ins)):
        for j, chip in enumerate(chips):
            _gather_copy(ins, outs, sems, a, 4 + j, (*chip, me[2]), sibling).wait_send()
        _gather_mine(ins, outs, sems, a).wait()


def _gather_scratch(n):
    return [pltpu.SemaphoreType.DMA((7 * n,)), pltpu.SemaphoreType.DMA((7 * n,)), pltpu.SemaphoreType.DMA((n,))]


def _gather_out_shape(shards, by_cols):
    return [
        jax.ShapeDtypeStruct((s.shape[0], N_DEV * s.shape[1]) if cols else (N_DEV, *s.shape), s.dtype)
        for s, cols in zip(shards, by_cols)
    ]


N_CHIP = 4


def _scatter_scratch(rows, cols):
    block = pltpu.VMEM((N_CHIP, rows, cols), BF16)
    dma = pltpu.SemaphoreType.DMA
    return [block, block, block, dma((N_CHIP,)), dma((N_CHIP,)), dma((N_CHIP,)), dma((N_CHIP - 1,)), dma((N_CHIP - 1,)), dma]


def _scatter_pair_copies(g_hbm, scr):
    x, y, c = _my_place()
    mine, theirs, _, a_send, a_recv, load_sem = scr[:6]
    to_sibling = [
        pltpu.make_async_remote_copy(
            src_ref=g_hbm.at[2 * q + (1 - c)], dst_ref=theirs.at[q], send_sem=a_send.at[q], recv_sem=a_recv.at[q],
            device_id=(x, y, 1 - c), device_id_type=MESH)
        for q in range(N_CHIP)
    ]
    loads = [pltpu.make_async_copy(g_hbm.at[2 * q + c], mine.at[q], load_sem.at[q]) for q in range(N_CHIP)]
    return to_sibling, loads


def _scatter_sum_copies(recv, scr):
    x, y, c = _my_place()
    sums, b_send, b_recv, own_sem = scr[2], scr[6], scr[7], scr[8]
    q_me = 2 * x + y
    to_owner = [
        pltpu.make_async_remote_copy(
            src_ref=sums.at[2 * px + py], dst_ref=recv.at[q_me], send_sem=b_send.at[j], recv_sem=b_recv.at[j],
            device_id=(px, py, c), device_id_type=MESH)
        for j, (px, py) in enumerate([(1 - x, y), (x, 1 - y), (1 - x, 1 - y)])
    ]
    return to_owner, pltpu.make_async_copy(sums.at[q_me], recv.at[q_me], own_sem)


def _scatter_start(g_hbm, scr):
    to_sibling, loads = _scatter_pair_copies(g_hbm, scr)
    for cp in to_sibling + loads:
        cp.start()


def _scatter_middle(g_hbm, recv, scr):
    to_sibling, loads = _scatter_pair_copies(g_hbm, scr)
    for cp in to_sibling:
        cp.wait_recv()
    for cp in loads:
        cp.wait()
    mine, theirs, sums = scr[:3]

    def step(r, carry):
        rs = pl.ds(pl.multiple_of(r * ROW_CHUNK, ROW_CHUNK), ROW_CHUNK)
        for q in range(N_CHIP):
            sums[q, rs, :] = (mine[q, rs, :].astype(F32) + theirs[q, rs, :].astype(F32)).astype(BF16)
        return carry

    lax.fori_loop(0, mine.shape[1] // ROW_CHUNK, step, 0)
    to_owner, own = _scatter_sum_copies(recv, scr)
    for cp in to_owner + [own]:
        cp.start()


def _scatter_finish(g_hbm, recv, scr):
    to_sibling, _ = _scatter_pair_copies(g_hbm, scr)
    to_owner, own = _scatter_sum_copies(recv, scr)
    for cp in to_owner:
        cp.wait_recv()
    for cp in to_sibling + to_owner:
        cp.wait_send()
    own.wait()


def _scatter_out_shape(gparts):
    return jax.ShapeDtypeStruct((N_CHIP, *gparts.shape[1:]), gparts.dtype)


def _adamw_math(w, g, m, v):
    m = ADAM_B1 * m + (1.0 - ADAM_B1) * g
    v = ADAM_B2 * v + (1.0 - ADAM_B2) * (g * g)
    m_hat = m / (1.0 - ADAM_B1**ADAM_STEP)
    v_hat = v / (1.0 - ADAM_B2**ADAM_STEP)
    delta = -ADAM_LR * (m_hat / (jnp.sqrt(v_hat) + ADAM_EPS) + ADAM_WD * w)
    return delta, m, v


ROW_CHUNK = 64


ELEMS_PER_STEP = 128 * 1024


def _gather_and_scatter(shards, gparts, name):
    n = len(shards)

    def body(*refs):
        ins, g_hbm, outs, recv = refs[:n], refs[n], refs[n + 1 : 2 * n + 1], refs[2 * n + 1]
        sems, scr = refs[2 * n + 2 : 2 * n + 5], refs[2 * n + 5 :]
        _scatter_start(g_hbm, scr)
        _gather_start(ins, outs, sems)
        _scatter_middle(g_hbm, recv, scr)
        for j in range(3):
            _gather_forward(ins, outs, sems, j)
        _gather_finish(ins, outs, sems)
        _scatter_finish(g_hbm, recv, scr)

    return pl.pallas_call(
        body,
        name=name,
        out_shape=_gather_out_shape(shards, [False] * n) + [_scatter_out_shape(gparts)],
        in_specs=[VMEM_SPEC] * n + [ANY_SPEC],
        out_specs=[VMEM_SPEC] * n + [ANY_SPEC],
        scratch_shapes=_gather_scratch(n) + _scatter_scratch(*gparts.shape[1:]),
        compiler_params=pltpu.CompilerParams(vmem_limit_bytes=VMEM_LIMIT),
    )(*shards, gparts)


def _sum_adamw(parts, w, m, v, name):
    _, rows, cols = w.shape
    n_parts = len(parts)
    pc = cols // n_parts
    rb = rows
    while rb * pc > ELEMS_PER_STEP and rb % 16 == 0:
        rb //= 2

    def body(*refs):
        p_refs = refs[:n_parts]
        w_ref, m_ref, v_ref, grad_ref, delta_ref, nm_ref, nv_ref = refs[n_parts:]
        for c, p_ref in enumerate(p_refs):

            @pl.when(pl.program_id(0) == c)
            def _(p_ref=p_ref):
                g = p_ref[0].astype(F32)
                for k in range(1, p_ref.shape[0]):
                    g = g + p_ref[k].astype(F32)
                delta, nm, nv = _adamw_math(w_ref[0], g, m_ref[0], v_ref[0])
                grad_ref[0] = g
                delta_ref[0] = delta
                nm_ref[0] = nm
                nv_ref[0] = nv

    part = lambda c, lead: pl.BlockSpec((lead, rb, pc), lambda j, i: (0, jnp.where(j == c, i, 0), 0))
    block = pl.BlockSpec((1, rb, pc), lambda j, i: (0, i, j))
    out = jax.ShapeDtypeStruct(w.shape, F32)
    return pl.pallas_call(
        body,
        name=name,
        grid=(n_parts, rows // rb),
        out_shape=[out] * 4,
        in_specs=[part(c, p.shape[0]) for c, p in enumerate(parts)] + [block] * 3,
        out_specs=[block] * 4,
        compiler_params=pltpu.CompilerParams(dimension_semantics=("arbitrary", "arbitrary"), vmem_limit_bytes=VMEM_LIMIT),
    )(*parts, w, m, v)


def _adamw_multi(items, name):
    n = len(items)

    def body(*refs):
        ins, outs = refs[: 4 * n], refs[4 * n :]
        for a in range(n):
            w_ref, g_ref, m_ref, v_ref = ins[4 * a : 4 * a + 4]
            d_ref, nm_ref, nv_ref = outs[3 * a : 3 * a + 3]
            delta, nm, nv = _adamw_math(w_ref[...], g_ref[...], m_ref[...], v_ref[...])
            d_ref[...] = delta
            nm_ref[...] = nm
            nv_ref[...] = nv

    flat = [a for it in items for a in it]
    out_shape = [jax.ShapeDtypeStruct(it[0].shape, F32) for it in items for _ in range(3)]
    outs = pl.pallas_call(
        body,
        name=name,
        out_shape=out_shape,
        in_specs=[VMEM_SPEC] * (4 * n),
        out_specs=[VMEM_SPEC] * (3 * n),
        compiler_params=pltpu.CompilerParams(vmem_limit_bytes=VMEM_LIMIT),
    )(*flat)
    return [tuple(outs[3 * a : 3 * a + 3]) for a in range(n)]


def _prologue(w_in, w_out, conv_w, c, w_ada, b_mine):
    D = c.shape[1]
    wc = w_ada.shape[2]
    shards16 = [jax.ShapeDtypeStruct(w_in.shape[1:], BF16), jax.ShapeDtypeStruct(w_out.shape[1:], BF16)]

    def to_all(src, out, sems):
        x, y, c = _my_place()
        me = _slot(x, y, c)
        copies = [
            pltpu.make_async_remote_copy(
                src_ref=src, dst_ref=out.at[me], send_sem=sems[0].at[k - 1], recv_sem=sems[1].at[k - 1],
                device_id=(x ^ (k >> 2), y ^ ((k >> 1) & 1), c ^ (k & 1)), device_id_type=MESH)
            for k in range(1, N_DEV)
        ]
        return copies, pltpu.make_async_copy(src, out.at[me], sems[2].at[0])

    def start(copies, own):
        for cp in copies + [own]:
            cp.start()

    def finish(copies, own):
        for cp in copies:
            cp.wait()
        own.wait()

    def body(win_ref, wout_ref, cw_ref, c_ref, wada_ref, b_ref, win_g, wout_g, cw_g, c_g, mod_g, win16, wout16, c_s, mp_s, *sems):
        c_copies = to_all(c_ref, c_g, sems[3:6])
        start(*c_copies)
        win16[...] = win_ref[0].astype(BF16)
        wout16[...] = wout_ref[0].astype(BF16)
        w_ins, w_outs, w_sems = (win16, wout16, cw_ref), (win_g, wout_g, cw_g), sems[0:3]
        _gather_start(w_ins, w_outs, w_sems)
        finish(*c_copies)
        for k in range(N_DEV):
            c_s[k : k + 1, :] = c_g[k]
        cv = c_s[...]
        cond = cv * jax.nn.sigmoid(cv)
        mp_s[...] = jnp.dot(cond, wada_ref[0], precision=lax.Precision.HIGHEST, preferred_element_type=F32) + b_ref[...]
        m_copies = to_all(mp_s, mod_g, sems[6:9])
        start(*m_copies)
        for j in range(3):
            _gather_forward(w_ins, w_outs, w_sems, j)
        _gather_finish(w_ins, w_outs, w_sems)
        finish(*m_copies)

    return pl.pallas_call(
        body,
        name="prologue",
        out_shape=_gather_out_shape(shards16 + [conv_w], [True, False, False])
        + [jax.ShapeDtypeStruct((N_DEV, 1, D), F32), jax.ShapeDtypeStruct((N_DEV, N_DEV, wc), F32)],
        in_specs=[VMEM_SPEC, VMEM_SPEC, ANY_SPEC] + [VMEM_SPEC] * 3,
        out_specs=[ANY_SPEC] * 3 + [VMEM_SPEC] * 2,
        scratch_shapes=[pltpu.VMEM(s.shape, BF16) for s in shards16]
        + [pltpu.VMEM((N_DEV, D), F32), pltpu.VMEM((N_DEV, wc), F32)]
        + _gather_scratch(3) + _gather_scratch(1) + _gather_scratch(1),
        compiler_params=pltpu.CompilerParams(vmem_limit_bytes=VMEM_LIMIT),
    )(w_in, w_out, conv_w, c, w_ada, b_mine)


def _ln_fwd(r):
    mu = jnp.mean(r, axis=-1, keepdims=True)
    d = r - mu
    var = jnp.mean(d * d, axis=-1, keepdims=True)
    rstd = lax.rsqrt(var + LN_EPS)
    return d * rstd, rstd


def _ln_bwd(dxh, xhat, rstd):
    m1 = jnp.mean(dxh, axis=-1, keepdims=True)
    m2 = jnp.mean(dxh * xhat, axis=-1, keepdims=True)
    return rstd * (dxh - m1 - xhat * m2)


def _colsum(a):
    return jnp.sum(a, axis=0, keepdims=True)


def _window_sums(ext, tm, causal):
    n = ext.shape[0]
    lo = HALO if causal else 0
    s, out = ext, []
    for p in range(len(POOL_WINDOWS)):
        assert POOL_WINDOWS[p] == 2 ** (p + 1)
        k = 2**p
        s = s + pltpu.roll(s, k if causal else n - k, 0)
        out.append(s[lo : lo + tm, 0:128])
        if p + 1 < len(POOL_WINDOWS):
            s = s[:, 128:]
    return out


def _pool_features(vp, vp_s, row, tm):
    sums = _window_sums(vp_s[...], tm, causal=True)
    feats, inv_cnts = [], []
    for g, win in enumerate(POOL_WINDOWS):
        inv_cnt = 1.0 / jnp.minimum(row + 1, win).astype(F32)
        feats.append(sums[g] * inv_cnt - vp[:, 128 * g : 128 * g + 128])
        inv_cnts.append(inv_cnt)
    return feats, inv_cnts


def _f1(x, mod, w_in, conv_w, w_pool, pool_scale, w_out, tm, gather, by_cols):
    T, D = x.shape
    ZW = w_in.shape[1]
    CC = ZW // 4
    nt = T // tm
    ng = len(gather)
    fwd_steps = [max(nt - 3 + j, 0) for j in range(3)]

    shards16 = [jax.ShapeDtypeStruct(s.shape[1:], BF16) for s in gather]

    def body(*refs):
        x_ref, mod_ref, win_ref, cw_ref, wp_ref, ps_ref, wout_ref = refs[:7]
        g_f32 = refs[7 : 7 + ng]
        z_ref, h_ref, xhat_ref, rstd_ref, mix_ref = refs[7 + ng : 12 + ng]
        g_outs = refs[12 + ng : 12 + 2 * ng]
        cv_s, vp_s = refs[12 + 2 * ng : 14 + 2 * ng]
        g_ins = refs[14 + 2 * ng : 14 + 3 * ng]
        g_sems = refs[14 + 3 * ng :]
        i = pl.program_id(0)

        @pl.when(i == 0)
        def _():
            for src, dst in zip(g_f32, g_ins):
                dst[...] = src[0].astype(BF16)
            _gather_start(g_ins, g_outs, g_sems)
            cv_s[0:HALO, :] = jnp.zeros((HALO, CC), F32)
            vp_s[0:HALO, :] = jnp.zeros((HALO, CC), F32)

        xv = x_ref[...]
        sh1, sc1, g1 = mod_ref[0:1, :], mod_ref[1:2, :], mod_ref[2:3, :]
        h = (xv * (1.0 + sc1) + sh1).astype(BF16)
        h_ref[...] = h
        z = jnp.dot(h, win_ref[...], preferred_element_type=F32)
        z_ref[...] = z.astype(BF16)
        gb, gc, vc, vp = z[:, 0:CC], z[:, CC : 2 * CC], z[:, 2 * CC : 3 * CC], z[:, 3 * CC : 4 * CC]
        cv = gc * vc
        cv_s[HALO : HALO + tm, :] = cv
        vp_s[HALO : HALO + tm, :] = vp
        conv = cw_ref[0:1, :] * cv_s[HALO - 2 : HALO - 2 + tm, :] + cw_ref[1:2, :] * cv_s[HALO - 1 : HALO - 1 + tm, :] + cw_ref[2:3, :] * cv
        parts = [gb * conv]
        row = i * tm + lax.broadcasted_iota(jnp.int32, (tm, 1), 0)
        feats, _ = _pool_features(vp, vp_s, row, tm)
        for g in range(len(POOL_WINDOWS)):
            pw = jnp.dot(feats[g].astype(BF16), wp_ref[g].astype(BF16), preferred_element_type=F32)
            parts.append(pw * ps_ref[0:1, 128 * g : 128 * g + 128])
        cv_s[0:HALO, :] = cv_s[tm : tm + HALO, :]
        vp_s[0:HALO, :] = vp_s[tm : tm + HALO, :]
        ycat = jnp.concatenate(parts, axis=1).astype(BF16)
        mix = jnp.dot(ycat, wout_ref[...], preferred_element_type=F32)
        mix_ref[...] = mix
        xhat, rstd = _ln_fwd(DEEPNORM_ALPHA * xv + (1.0 + g1) * mix)
        xhat_ref[...] = xhat
        rstd_ref[...] = rstd

        for j in range(3):

            @pl.when(i == fwd_steps[j])
            def _(j=j):
                _gather_forward(g_ins, g_outs, g_sems, j)

        @pl.when(i == nt - 1)
        def _():
            _gather_finish(g_ins, g_outs, g_sems)

    tile = lambda w: pl.BlockSpec((tm, w), lambda i: (i, 0))
    return pl.pallas_call(
        body,
        name="f1",
        grid=(nt,),
        out_shape=[
            jax.ShapeDtypeStruct((T, ZW), BF16),
            jax.ShapeDtypeStruct((T, D), BF16),
            jax.ShapeDtypeStruct((T, D), F32),
            jax.ShapeDtypeStruct((T, 1), F32),
            jax.ShapeDtypeStruct((T, D), F32),
        ]
        + _gather_out_shape(shards16, by_cols),
        in_specs=[tile(D)] + [VMEM_SPEC] * (6 + ng),
        out_specs=[tile(ZW), tile(D), tile(D), tile(1), tile(D)] + [ANY_SPEC] * ng,
        scratch_shapes=[pltpu.VMEM((HALO + tm, CC), F32), pltpu.VMEM((HALO + tm, CC), F32)]
        + [pltpu.VMEM(s.shape, BF16) for s in shards16]
        + _gather_scratch(ng),
        compiler_params=pltpu.CompilerParams(dimension_semantics=("arbitrary",), vmem_limit_bytes=VMEM_LIMIT),
    )(x, mod, w_in, conv_w, w_pool, pool_scale, w_out, *gather)


def _fb2(xhat1, target, mod, ln, w_mi, w_mo, tm):
    T, D = xhat1.shape
    H = w_mi.shape[1]
    hc = min(1024, H)
    nb = H // hc
    nt = T // tm

    def body(xh1_ref, t_ref, mod_ref, ln_ref, wmi_ref, wmo_ref, dx1_ref, h2_ref, a_ref, du_ref, df_ref, acc_ref):
        i = pl.program_id(0)

        @pl.when(i == 0)
        def _():
            acc_ref[...] = jnp.zeros((8, D), F32)

        sh2, sc2, g2 = mod_ref[3:4, :], mod_ref[4:5, :], mod_ref[5:6, :]
        x1 = xh1_ref[...] * ln_ref[0:1, :] + ln_ref[1:2, :]
        h2 = (x1 * (1.0 + sc2) + sh2).astype(BF16)
        h2_ref[...] = h2
        f = jnp.zeros((tm, D), F32)
        for k in range(nb):
            ks = slice(k * hc, (k + 1) * hc)
            r = jnp.maximum(jnp.dot(h2, wmi_ref[:, ks], preferred_element_type=F32), 0.0)
            du_ref[:, ks] = r.astype(BF16)
            a = (r * r).astype(BF16)
            a_ref[:, ks] = a
            f = f + jnp.dot(a, wmo_ref[ks, :], preferred_element_type=F32)
        xhat2, rstd2 = _ln_fwd(DEEPNORM_ALPHA * x1 + (1.0 + g2) * f)
        ln2_g = ln_ref[2:3, :]
        d = xhat2 * ln2_g + ln_ref[3:4, :] - t_ref[...]
        dr2 = _ln_bwd(d * (ln2_g * (1.0 / D)), xhat2, rstd2)
        df = ((1.0 + g2) * dr2).astype(BF16)
        df_ref[...] = df
        dh2 = jnp.zeros((tm, D), F32)
        for k in range(nb):
            ks = slice(k * hc, (k + 1) * hc)
            da = lax.dot_general(df, wmo_ref[ks, :], NT, preferred_element_type=F32)
            du = (da * (2.0 * du_ref[:, ks].astype(F32))).astype(BF16)
            du_ref[:, ks] = du
            dh2 = dh2 + lax.dot_general(du, wmi_ref[:, ks], NT, preferred_element_type=F32)
        dx1_ref[...] = DEEPNORM_ALPHA * dr2 + dh2 * (1.0 + sc2)
        acc_ref[0:1, :] += _colsum(d * xhat2) * (1.0 / D)
        acc_ref[1:2, :] += _colsum(d) * (1.0 / D)
        acc_ref[2:3, :] += _colsum(dh2)
        acc_ref[3:4, :] += _colsum(dh2 * x1)
        acc_ref[4:5, :] += _colsum(dr2 * f)
        acc_ref[5:6, :] += jnp.zeros((1, D), F32) + (0.5 / D) * jnp.sum(d * d)

    tile = lambda w: pl.BlockSpec((tm, w), lambda i: (i, 0))
    return pl.pallas_call(
        body,
        name="fb2",
        grid=(nt,),
        out_shape=[
            jax.ShapeDtypeStruct((T, D), F32),
            jax.ShapeDtypeStruct((T, D), BF16),
            jax.ShapeDtypeStruct((T, H), BF16),
            jax.ShapeDtypeStruct((T, H), BF16),
            jax.ShapeDtypeStruct((T, D), BF16),
            jax.ShapeDtypeStruct((8, D), F32),
        ],
        in_specs=[tile(D), tile(D)] + [VMEM_SPEC] * 4,
        out_specs=[tile(D), tile(D), tile(H), tile(H), tile(D), pl.BlockSpec((8, D), lambda i: (0, 0))],
        compiler_params=pltpu.CompilerParams(dimension_semantics=("arbitrary",), vmem_limit_bytes=VMEM_LIMIT),
    )(xhat1, target, mod, ln, w_mi, w_mo)


def _b1(dx1, xhat1, rstd1, x, mix, z, mod, ln, w_out, w_in, conv_w, w_pool, pool_scale, tm):
    T, D = x.shape
    ZW = w_in.shape[1]
    CC = ZW // 4
    nt = T // tm
    hb = tm // HALO

    def body(dx1_ref, xh1_ref, rstd_ref, x_ref, mix_ref, z_ref, zh_ref, mod_ref, ln_ref, wout_ref, win_ref, cw_ref, wp_ref, ps_ref,
             dx_ref, dmix_ref, ycat_ref, dz_ref, acc_ref, gcw_ref, gwp_ref, cv_s, vp_s, e_s, q_s):
        i = pl.program_id(0)
        j = nt - 1 - i

        @pl.when(i == 0)
        def _():
            acc_ref[...] = jnp.zeros((8, D), F32)
            gcw_ref[...] = jnp.zeros((8, CC), F32)
            gwp_ref[...] = jnp.zeros(gwp_ref.shape, F32)
            e_s[tm : tm + HALO, :] = jnp.zeros((HALO, CC), F32)
            q_s[tm : tm + HALO, :] = jnp.zeros((HALO, CC), F32)

        sh1, sc1, g1 = mod_ref[0:1, :], mod_ref[1:2, :], mod_ref[2:3, :]
        dx1 = dx1_ref[...]
        xhat1 = xh1_ref[...]
        acc_ref[0:1, :] += _colsum(dx1 * xhat1)
        acc_ref[1:2, :] += _colsum(dx1)
        dr1 = _ln_bwd(dx1 * ln_ref[0:1, :], xhat1, rstd_ref[...])
        acc_ref[4:5, :] += _colsum(dr1 * mix_ref[...])
        dmix = ((1.0 + g1) * dr1).astype(BF16)
        dmix_ref[...] = dmix
        dycat = lax.dot_general(dmix, wout_ref[...], NT, preferred_element_type=F32)

        z = z_ref[...].astype(F32)
        zh = zh_ref[...].astype(F32) * jnp.where(j > 0, 1.0, 0.0)
        gb, gc, vc, vp = z[:, 0:CC], z[:, CC : 2 * CC], z[:, 2 * CC : 3 * CC], z[:, 3 * CC : 4 * CC]
        cv = gc * vc
        cv_s[0:HALO, :] = zh[:, CC : 2 * CC] * zh[:, 2 * CC : 3 * CC]
        cv_s[HALO : HALO + tm, :] = cv
        vp_s[0:HALO, :] = zh[:, 3 * CC : 4 * CC]
        vp_s[HALO : HALO + tm, :] = vp
        cv_m2 = cv_s[HALO - 2 : HALO - 2 + tm, :]
        cv_m1 = cv_s[HALO - 1 : HALO - 1 + tm, :]
        w0, w1, w2 = cw_ref[0:1, :], cw_ref[1:2, :], cw_ref[2:3, :]
        conv = w0 * cv_m2 + w1 * cv_m1 + w2 * cv
        dyc = dycat[:, 0:CC]
        e = dyc * gb
        e_s[0:tm, :] = e
        dcv = w2 * e + w1 * e_s[1 : 1 + tm, :] + w0 * e_s[2 : 2 + tm, :]
        gcw_ref[0:1, :] += _colsum(e * cv_m2)
        gcw_ref[1:2, :] += _colsum(e * cv_m1)
        gcw_ref[2:3, :] += _colsum(e * cv)
        y_parts = [gb * conv]
        dz_parts = [dyc * conv, dcv * vc, dcv * gc]

        row = j * tm + lax.broadcasted_iota(jnp.int32, (tm, 1), 0)
        feats, inv_cnts = _pool_features(vp, vp_s, row, tm)
        gps_parts, dps = [], []
        for g in range(len(POOL_WINDOWS)):
            cols = slice(128 * g, 128 * g + 128)
            p = feats[g].astype(BF16)
            scale = ps_ref[0:1, cols]
            wp = wp_ref[g].astype(BF16)
            pw = jnp.dot(p, wp, preferred_element_type=F32)
            y_parts.append(pw * scale)
            dyp = dycat[:, CC + 128 * g : CC + 128 * g + 128]
            gps_parts.append(_colsum(dyp * pw))
            dpw = (dyp * scale).astype(BF16)
            gwp_ref[g] += lax.dot_general(p, dpw, TN, preferred_element_type=F32)
            dp = lax.dot_general(dpw, wp, NT, preferred_element_type=F32)
            q_s[0:tm, cols] = dp * inv_cnts[g]
            dps.append(dp)
        sq = _window_sums(q_s[...], tm, causal=False)
        dz_parts += [sq[g] - dps[g] for g in range(len(POOL_WINDOWS))]
        gcw_ref[3:4, :] += jnp.concatenate(gps_parts, axis=1)
        ycat_ref[...] = jnp.concatenate(y_parts, axis=1).astype(BF16)
        dz = jnp.concatenate(dz_parts, axis=1).astype(BF16)
        dz_ref[...] = dz
        dh = lax.dot_general(dz, win_ref[...], NT, preferred_element_type=F32)
        acc_ref[2:3, :] += _colsum(dh)
        acc_ref[3:4, :] += _colsum(dh * x_ref[...])
        dx_ref[...] = DEEPNORM_ALPHA * dr1 + dh * (1.0 + sc1)
        e_s[tm : tm + HALO, :] = e_s[0:HALO, :]
        q_s[tm : tm + HALO, :] = q_s[0:HALO, :]

    tile = lambda w: pl.BlockSpec((tm, w), lambda i: (nt - 1 - i, 0))
    halo = pl.BlockSpec((HALO, ZW), lambda i: (jnp.maximum((nt - 1 - i) * hb - 1, 0), 0))
    fixed = lambda shape: pl.BlockSpec(shape, lambda i: (0,) * len(shape))
    return pl.pallas_call(
        body,
        name="b1",
        grid=(nt,),
        out_shape=[
            jax.ShapeDtypeStruct((T, D), F32),
            jax.ShapeDtypeStruct((T, D), BF16),
            jax.ShapeDtypeStruct((T, D), BF16),
            jax.ShapeDtypeStruct((T, ZW), BF16),
            jax.ShapeDtypeStruct((8, D), F32),
            jax.ShapeDtypeStruct((8, CC), F32),
            jax.ShapeDtypeStruct(w_pool.shape, F32),
        ],
        in_specs=[tile(D), tile(D), tile(1), tile(D), tile(D), tile(ZW), halo] + [VMEM_SPEC] * 7,
        out_specs=[tile(D), tile(D), tile(D), tile(ZW), fixed((8, D)), fixed((8, CC)), fixed(w_pool.shape)],
        scratch_shapes=[
            pltpu.VMEM((HALO + tm, CC), F32),
            pltpu.VMEM((HALO + tm, CC), F32),
            pltpu.VMEM((tm + HALO, CC), F32),
            pltpu.VMEM((tm + HALO, CC), F32),
        ],
        compiler_params=pltpu.CompilerParams(dimension_semantics=("arbitrary",), vmem_limit_bytes=VMEM_LIMIT),
    )(dx1, xhat1, rstd1, x, mix, z, z, mod, ln, w_out, w_in, conv_w, w_pool, pool_scale)


N_SCATTER_SCRATCH = 9


def _wgrad(a, b, bk, n_groups, bt, name, owners=None, scatter=(), groups=None):
    T, K = a.shape
    N = b.shape[1]
    nk, nt, ng = K // bk, T // bt, N // n_groups
    g0, n_groups = groups or (0, n_groups)
    assert owners is None or groups is None
    N = n_groups * ng
    nc = min(512, ng)
    ns = len(scatter)
    n_steps = nk * n_groups * nt
    mid_step = min(1, n_steps - 1)

    def body(*refs):
        a_ref, b_ref = refs[0], refs[1]
        s_hbm = refs[2 : 2 + ns]
        outs = refs[2 + ns :]
        o_ref, s_recv = outs[0], outs[1 : 1 + ns]
        scr = outs[1 + ns :]
        acc = scr[0]
        s_scr = [scr[1 + N_SCATTER_SCRATCH * k : 1 + N_SCATTER_SCRATCH * (k + 1)] for k in range(ns)]
        kk, gg, t = pl.program_id(0), pl.program_id(1), pl.program_id(2)
        step = (kk * n_groups + gg) * nt + t

        if ns:

            @pl.when(step == 0)
            def _():
                for k in range(ns):
                    _scatter_start(s_hbm[k], s_scr[k])

            @pl.when(step == mid_step)
            def _():
                for k in range(ns):
                    _scatter_middle(s_hbm[k], s_recv[k], s_scr[k])

        @pl.when(t == 0)
        def _():
            acc[...] = jnp.zeros(acc.shape, F32)

        at = a_ref[...].T
        for c in range(ng // nc):
            cs = slice(c * nc, (c + 1) * nc)
            acc[:, cs] += jnp.dot(at, b_ref[:, cs], preferred_element_type=F32)

        @pl.when(t == nt - 1)
        def _():
            if owners is None:
                o_ref[...] = acc[...].astype(BF16)
            else:
                per = N // owners
                for o in range(ng // per):
                    o_ref[o] = acc[:, o * per : (o + 1) * per].astype(BF16)

        if ns:

            @pl.when(step == n_steps - 1)
            def _():
                for k in range(ns):
                    _scatter_finish(s_hbm[k], s_recv[k], s_scr[k])

    if owners is None:
        out_shape = [jax.ShapeDtypeStruct((K, N), BF16)]
        out_specs = [pl.BlockSpec((bk, ng), lambda k, g, t: (k, g))]
    else:
        assert bk == K
        per = N // owners
        out_shape = [jax.ShapeDtypeStruct((owners, K, per), BF16)]
        out_specs = [pl.BlockSpec((ng // per, K, per), lambda k, g, t: (g, 0, 0))]
    ins, in_specs = [a, b], [pl.BlockSpec((bt, bk), lambda k, g, t: (t, k)), pl.BlockSpec((bt, ng), lambda k, g, t: (t, g + g0))]
    scratch = [pltpu.VMEM((bk, ng), F32)]
    for s in scatter:
        ins.append(s)
        in_specs.append(ANY_SPEC)
        out_shape.append(_scatter_out_shape(s))
        out_specs.append(ANY_SPEC)
        scratch += _scatter_scratch(*s.shape[1:])
    outs = pl.pallas_call(
        body,
        name=name,
        grid=(nk, n_groups, nt),
        out_shape=out_shape,
        in_specs=in_specs,
        out_specs=out_specs,
        scratch_shapes=scratch,
        compiler_params=pltpu.CompilerParams(dimension_semantics=("arbitrary", "arbitrary", "arbitrary"), vmem_limit_bytes=VMEM_LIMIT),
    )(*ins)
    return outs if ns else outs[0]


def _small_grads(acc1_t, acc2_t, gcw_t, gwp_all, cond_t, my_slot, w_cols):
    D = acc1_t.shape[2]
    n_chunk = D // 128
    q_mine = w_cols // 128

    def total(ref, r):
        s = ref[r, 0:1, :]
        for k in range(1, N_DEV):
            s = s + ref[r, k : k + 1, :]
        return s

    def body(slot_ref, a1_ref, a2_ref, gcw_ref, gwp_ref, ct_ref, gb_ref, gw_ref, gln_ref, gcwo_ref, gwpo_ref, loss_ref, dm_s):
        loss_ref[...] = total(a2_ref, 5)
        for s, (ref, r) in enumerate([(a1_ref, 2), (a1_ref, 3), (a1_ref, 4), (a2_ref, 2), (a2_ref, 3), (a2_ref, 4)]):
            gb_ref[0:1, s * D : (s + 1) * D] = total(ref, r)
            for qq in range(n_chunk):
                dm_s[s * n_chunk + qq] = ref[r, :, 128 * qq : 128 * qq + 128]
        gln_ref[0:1, :] = total(a1_ref, 0)
        gln_ref[1:2, :] = total(a1_ref, 1)
        gln_ref[2:3, :] = total(a2_ref, 0)
        gln_ref[3:4, :] = total(a2_ref, 1)
        gcwo_ref[...] = jnp.zeros(gcwo_ref.shape, F32)
        for r in range(4):
            gcwo_ref[r : r + 1, :] = total(gcw_ref, r)
        wp = gwp_ref[0]
        for k in range(1, N_DEV):
            wp = wp + gwp_ref[k]
        gwpo_ref[0] = wp
        ct = ct_ref[...]
        cond_t = ct * jax.nn.sigmoid(ct)
        q0 = slot_ref[0] * q_mine
        for q in range(q_mine):
            dm = dm_s[q0 + q]
            out = cond_t[:, 0:1] * dm[0:1, :]
            for k in range(1, N_DEV):
                out = out + cond_t[:, k : k + 1] * dm[k : k + 1, :]
            gw_ref[0, :, 128 * q : 128 * q + 128] = out

    CC = gcw_t.shape[2]
    return pl.pallas_call(
        body,
        name="small_grads",
        out_shape=[
            jax.ShapeDtypeStruct((1, 6 * D), F32),
            jax.ShapeDtypeStruct((1, D, w_cols), F32),
            jax.ShapeDtypeStruct((4, D), F32),
            jax.ShapeDtypeStruct((8, CC), F32),
            jax.ShapeDtypeStruct((1, *gwp_all.shape[1:]), F32),
            jax.ShapeDtypeStruct((1, D), F32),
        ],
        in_specs=[pl.BlockSpec(memory_space=pltpu.SMEM)] + [VMEM_SPEC] * 5,
        out_specs=[VMEM_SPEC] * 6,
        scratch_shapes=[pltpu.VMEM((6 * n_chunk, N_DEV, 128), F32)],
        compiler_params=pltpu.CompilerParams(vmem_limit_bytes=VMEM_LIMIT),
    )(my_slot, acc1_t, acc2_t, gcw_t, gwp_all, cond_t)


def kernel(x, c, w_ada, b_ada, w_in, conv_w, w_pool, pool_scale, w_out, ln1_g, ln1_b, w_mlp_in, w_mlp_out, ln2_g, ln2_b, loss_target, m_w_ada, m_b_ada, m_w_in, m_conv_w, m_w_pool, m_pool_scale, m_w_out, m_ln1_g, m_ln1_b, m_w_mlp_in, m_w_mlp_out, m_ln2_g, m_ln2_b, v_w_ada, v_b_ada, v_w_in, v_conv_w, v_w_pool, v_pool_scale, v_w_out, v_ln1_g, v_ln1_b, v_w_mlp_in, v_w_mlp_out, v_ln2_g, v_ln2_b):
    T, D = x.shape[1], x.shape[2]
    H = w_mlp_out.shape[1] * N_DEV
    ZW = w_in.shape[2] * N_DEV
    CC = ZW // 4
    tm = min(512, T // 2)
    bt = min(1024, T)
    ax, ay, ac = _my_place()
    me = _slot(ax, ay, ac)

    w_cols = w_ada.shape[2]
    b_mine = lax.dynamic_slice(b_ada, (0, me * w_cols), (1, w_cols))
    w_in_f, w_out_g, cw_g, c_g, mod_g = _prologue(w_in, w_out, conv_w[0], c, w_ada, b_mine)
    w_out_f = w_out_g.reshape(D, D)
    conv_w_f = jnp.transpose(cw_g, (1, 0, 2)).reshape(conv_w.shape[1], CC)
    c_all = c_g.reshape(N_DEV, D)
    mod = lax.dynamic_index_in_dim(mod_g, me, axis=1, keepdims=False).reshape(6, D)

    ln = jnp.concatenate([ln1_g, ln1_b, ln2_g, ln2_b], axis=0)
    xs, target = x[0], loss_target[0]

    z, h, xhat1, rstd1, mix, w_mi_f, w_mo_g = _f1(
        xs, mod, w_in_f, conv_w_f, w_pool[0], pool_scale, w_out_f, tm,
        [w_mlp_in, w_mlp_out], [True, False])
    dx1, h2, a, du, df, acc2 = _fb2(xhat1, target, mod, ln, w_mi_f, w_mo_g.reshape(H, D), tm)

    grad_x, dmix, ycat, dz, acc1, gcw, gwp = _b1(
        dx1, xhat1, rstd1, xs, mix, z, mod, ln, w_out_f, w_in_f, conv_w_f, w_pool[0], pool_scale, tm)

    mo_blocks = lambda g: g.reshape(N_DEV, H // N_DEV, D // 2)
    gp_out = _wgrad(ycat, dmix, D, 1, bt, "wgrad_out").reshape(N_DEV, D // N_DEV, D)
    gp_in, rv_out = _wgrad(h, dz, D, 1, bt, "wgrad_in", owners=N_DEV, scatter=[gp_out])
    gp_mi, rv_in = _wgrad(h2, du, D, 2, bt, "wgrad_mlp_in", owners=N_DEV, scatter=[gp_in])
    gp_mo_a, rv_mi = _wgrad(a, df, D, 2, bt, "wgrad_mlp_out_a", groups=(0, 1), scatter=[gp_mi])
    gp_mo_b, rv_mo_a = _wgrad(a, df, D, 2, bt, "wgrad_mlp_out_b", groups=(1, 1), scatter=[mo_blocks(gp_mo_a)])
    acc1_g, acc2_g, gcw_g, gwp_g, rv_mo_b = _gather_and_scatter([acc1, acc2, gcw, gwp], mo_blocks(gp_mo_b), "gather_small")

    g_b_ada, g_w_ada, g_ln, g_cw, g_w_pool, loss_row = _small_grads(
        jnp.transpose(acc1_g, (1, 0, 2)), jnp.transpose(acc2_g, (1, 0, 2)), jnp.transpose(gcw_g, (1, 0, 2)), gwp_g,
        c_all.T, jnp.reshape(me, (1,)).astype(jnp.int32), w_cols)
    cc_mine = conv_w.shape[2]
    g_conv_w = lax.dynamic_slice(g_cw, (0, me * cc_mine), (conv_w.shape[1], cc_mine))[None]
    g_pool_scale = g_cw[3:4, :]
    g_ln1_g, g_ln1_b, g_ln2_g, g_ln2_b = g_ln[0:1], g_ln[1:2], g_ln[2:3], g_ln[3:4]

    small = _adamw_multi(
        [
            (b_ada, g_b_ada, m_b_ada, v_b_ada),
            (conv_w, g_conv_w, m_conv_w, v_conv_w),
            (w_pool, g_w_pool, m_w_pool, v_w_pool),
            (pool_scale, g_pool_scale, m_pool_scale, v_pool_scale),
            (ln1_g, g_ln1_g, m_ln1_g, v_ln1_g),
            (ln1_b, g_ln1_b, m_ln1_b, v_ln1_b),
            (ln2_g, g_ln2_g, m_ln2_g, v_ln2_g),
            (ln2_b, g_ln2_b, m_ln2_b, v_ln2_b),
        ],
        "adamw_small")
    u_b_ada, u_conv_w, u_w_pool, u_pool_scale, u_ln1_g, u_ln1_b, u_ln2_g, u_ln2_b = small

    g_w_ada, *u_w_ada = _sum_adamw([g_w_ada], w_ada, m_w_ada, v_w_ada, "adamw_w_ada")
    g_w_mo, *u_w_mo = _sum_adamw([rv_mo_a, rv_mo_b], w_mlp_out, m_w_mlp_out, v_w_mlp_out, "sum_w_mlp_out")
    g_w_mi, *u_w_mi = _sum_adamw([rv_mi], w_mlp_in, m_w_mlp_in, v_w_mlp_in, "sum_w_mlp_in")
    g_w_in, *u_w_in = _sum_adamw([rv_in], w_in, m_w_in, v_w_in, "sum_w_in")
    g_w_out, *u_w_out = _sum_adamw([rv_out], w_out, m_w_out, v_w_out, "sum_w_out")

    grads = [g_w_ada, g_b_ada, g_w_in, g_conv_w, g_w_pool, g_pool_scale, g_w_out, g_ln1_g, g_ln1_b, g_w_mi, g_w_mo, g_ln2_g, g_ln2_b]
    updates = [u_w_ada, u_b_ada, u_w_in, u_conv_w, u_w_pool, u_pool_scale, u_w_out, u_ln1_g, u_ln1_b, u_w_mi, u_w_mo, u_ln2_g, u_ln2_b]
    deltas = [u[0] for u in updates]
    new_m = [u[1] for u in updates]
    new_v = [u[2] for u in updates]
    return (loss_row[0, 0], grad_x[None], *grads, *deltas, *new_m, *new_v)
```

```python
import jax
import jax.numpy as jnp
from jax import lax
from jax.experimental import pallas as pl
from jax.experimental.pallas import tpu as pltpu

F32 = jnp.float32
BF16 = jnp.bfloat16
MESH = pl.DeviceIdType.MESH
N_DEV = 8

LN_EPS = 1e-5
DEPTH = 1
DEEPNORM_ALPHA = (2.0 * DEPTH) ** 0.25
POOL_WINDOWS = (2, 4, 8, 16)
HALO = 16

ADAM_LR = 0.001
ADAM_B1 = 0.9
ADAM_B2 = 0.999
ADAM_EPS = 1e-08
ADAM_WD = 0.01
ADAM_STEP = 10

VMEM_LIMIT = 60 * 1024 * 1024

VMEM_SPEC = pl.BlockSpec(memory_space=pltpu.VMEM)
ANY_SPEC = pl.BlockSpec(memory_space=pl.ANY)

NT = (((1,), (1,)), ((), ()))
TN = (((0,), (0,)), ((), ()))


def _my_place():
    return lax.axis_index("x"), lax.axis_index("y"), lax.axis_index("c")


def _slot(x, y, c):
    return 4 * x + 2 * y + c


def _gather_place(ins, outs, a, slot):
    if len(outs[a].shape) == len(ins[a].shape):
        wb = ins[a].shape[1]
        return outs[a].at[:, pl.ds(pl.multiple_of(slot * wb, wb), wb)]
    return outs[a].at[slot]


def _gather_copy(ins, outs, sems, a, k, block, to, from_shard=False):
    send_sems, recv_sems, _ = sems
    dst = _gather_place(ins, outs, a, _slot(*block))
    return pltpu.make_async_remote_copy(
        src_ref=ins[a] if from_shard else dst,
        dst_ref=dst,
        send_sem=send_sems.at[7 * a + k],
        recv_sem=recv_sems.at[7 * a + k],
        device_id=to,
        device_id_type=MESH,
    )


def _gather_peers():
    x, y, c = _my_place()
    return (x, y, c), (x, y, 1 - c), [(1 - x, y), (x, 1 - y), (1 - x, 1 - y)]


def _gather_first(ins, outs, sems):
    me, sibling, chips = _gather_peers()
    first = []
    for a in range(len(ins)):
        first.append(_gather_copy(ins, outs, sems, a, 0, me, sibling, from_shard=True))
        first += [_gather_copy(ins, outs, sems, a, 1 + j, me, (*chip, me[2]), from_shard=True) for j, chip in enumerate(chips)]
    return first


def _gather_mine(ins, outs, sems, a):
    me, _, _ = _gather_peers()
    return pltpu.make_async_copy(ins[a], _gather_place(ins, outs, a, _slot(*me)), sems[2].at[a])


def _gather_start(ins, outs, sems):
    for a in range(len(ins)):
        _gather_mine(ins, outs, sems, a).start()
    for cp in _gather_first(ins, outs, sems):
        cp.start()


def _gather_forward(ins, outs, sems, j):
    me, sibling, chips = _gather_peers()
    for a in range(len(ins)):
        _gather_copy(ins, outs, sems, a, 1 + j, (*chips[j], me[2]), me).wait_recv()
        _gather_copy(ins, outs, sems, a, 4 + j, (*chips[j], me[2]), sibling).start()


def _gather_finish(ins, outs, sems):
    me, sibling, chips = _gather_peers()
    for a in range(len(ins)):
        _gather_copy(ins, outs, sems, a, 0, sibling, me).wait_recv()
        for j, chip in enumerate(chips):
            _gather_copy(ins, outs, sems, a, 4 + j, (*chip, 1 - me[2]), me).wait_recv()
    for cp in _gather_first(ins, outs, sems):
        cp.wait_send()
    for a in range(len(ins)):
        for j, chip in enumerate(chips):
            _gather_copy(ins, outs, sems, a, 4 + j, (*chip, me[2]), sibling).wait_send()
        _gather_mine(ins, outs, sems, a).wait()


def _gather_scratch(n):
    return [pltpu.SemaphoreType.DMA((7 * n,)), pltpu.SemaphoreType.DMA((7 * n,)), pltpu.SemaphoreType.DMA((n,))]


def _gather_out_shape(shards, by_cols):
    return [
        jax.ShapeDtypeStruct((s.shape[0], N_DEV * s.shape[1]) if cols else (N_DEV, *s.shape), s.dtype)
        for s, cols in zip(shards, by_cols)
    ]


N_CHIP = 4


def _scatter_scratch(rows, cols):
    block = pltpu.VMEM((N_CHIP, rows, cols), BF16)
    dma = pltpu.SemaphoreType.DMA
    return [block, block, block, dma((N_CHIP,)), dma((N_CHIP,)), dma((N_CHIP,)), dma((N_CHIP - 1,)), dma((N_CHIP - 1,)), dma]


def _scatter_pair_copies(g_hbm, scr):
    x, y, c = _my_place()
    mine, theirs, _, a_send, a_recv, load_sem = scr[:6]
    to_sibling = [
        pltpu.make_async_remote_copy(
            src_ref=g_hbm.at[2 * q + (1 - c)], dst_ref=theirs.at[q], send_sem=a_send.at[q], recv_sem=a_recv.at[q],
            device_id=(x, y, 1 - c), device_id_type=MESH)
        for q in range(N_CHIP)
    ]
    loads = [pltpu.make_async_copy(g_hbm.at[2 * q + c], mine.at[q], load_sem.at[q]) for q in range(N_CHIP)]
    return to_sibling, loads


def _scatter_sum_copies(recv, scr):
    x, y, c = _my_place()
    sums, b_send, b_recv, own_sem = scr[2], scr[6], scr[7], scr[8]
    q_me = 2 * x + y
    to_owner = [
        pltpu.make_async_remote_copy(
            src_ref=sums.at[2 * px + py], dst_ref=recv.at[q_me], send_sem=b_send.at[j], recv_sem=b_recv.at[j],
            device_id=(px, py, c), device_id_type=MESH)
        for j, (px, py) in enumerate([(1 - x, y), (x, 1 - y), (1 - x, 1 - y)])
    ]
    return to_owner, pltpu.make_async_copy(sums.at[q_me], recv.at[q_me], own_sem)


def _scatter_start(g_hbm, scr):
    to_sibling, loads = _scatter_pair_copies(g_hbm, scr)
    for cp in to_sibling + loads:
        cp.start()


def _scatter_middle(g_hbm, recv, scr):
    to_sibling, loads = _scatter_pair_copies(g_hbm, scr)
    for cp in to_sibling:
        cp.wait_recv()
    for cp in loads:
        cp.wait()
    mine, theirs, sums = scr[:3]

    def step(r, carry):
        rs = pl.ds(pl.multiple_of(r * ROW_CHUNK, ROW_CHUNK), ROW_CHUNK)
        for q in range(N_CHIP):
            sums[q, rs, :] = (mine[q, rs, :].astype(F32) + theirs[q, rs, :].astype(F32)).astype(BF16)
        return carry

    lax.fori_loop(0, mine.shape[1] // ROW_CHUNK, step, 0)
    to_owner, own = _scatter_sum_copies(recv, scr)
    for cp in to_owner + [own]:
        cp.start()


def _scatter_finish(g_hbm, recv, scr):
    to_sibling, _ = _scatter_pair_copies(g_hbm, scr)
    to_owner, own = _scatter_sum_copies(recv, scr)
    for cp in to_owner:
        cp.wait_recv()
    for cp in to_sibling + to_owner:
        cp.wait_send()
    own.wait()


def _scatter_out_shape(gparts):
    return jax.ShapeDtypeStruct((N_CHIP, *gparts.shape[1:]), gparts.dtype)


def _adamw_math(w, g, m, v):
    m = ADAM_B1 * m + (1.0 - ADAM_B1) * g
    v = ADAM_B2 * v + (1.0 - ADAM_B2) * (g * g)
    m_hat = m / (1.0 - ADAM_B1**ADAM_STEP)
    v_hat = v / (1.0 - ADAM_B2**ADAM_STEP)
    delta = -ADAM_LR * (m_hat / (jnp.sqrt(v_hat) + ADAM_EPS) + ADAM_WD * w)
    return delta, m, v


ROW_CHUNK = 64


ELEMS_PER_STEP = 128 * 1024


def _allgather(shards, name, in_vmem, scatter=None, by_cols=None):
    n = len(shards)
    by_cols = by_cols or [False] * n
    ns = 0 if scatter is None else 1

    def body(*refs):
        ins, outs = refs[:n], refs[n + ns : 2 * n + ns]
        rest = refs[2 * n + ns :]
        if ns:
            g_hbm, recv, sems, scr = refs[n], rest[0], rest[1:4], rest[4:]
            _scatter_start(g_hbm, scr)
        else:
            sems = rest[0:3]
        _gather_start(ins, outs, sems)
        if ns:
            _scatter_middle(g_hbm, recv, scr)
        for j in range(3):
            _gather_forward(ins, outs, sems, j)
        _gather_finish(ins, outs, sems)
        if ns:
            _scatter_finish(g_hbm, recv, scr)

    spec = VMEM_SPEC if in_vmem else ANY_SPEC
    extra_in, extra_out, extra_spec, extra_scratch = [], [], [], []
    if ns:
        extra_in, extra_spec = [scatter], [ANY_SPEC]
        extra_out = [_scatter_out_shape(scatter)]
        extra_scratch = _scatter_scratch(*scatter.shape[1:])
    return pl.pallas_call(
        body,
        name=name,
        out_shape=_gather_out_shape(shards, by_cols) + extra_out,
        in_specs=[spec] * n + extra_spec,
        out_specs=[spec] * n + extra_spec,
        scratch_shapes=_gather_scratch(n) + extra_scratch,
        compiler_params=pltpu.CompilerParams(vmem_limit_bytes=VMEM_LIMIT),
    )(*shards, *extra_in)


def _sum_adamw(parts, w, m, v, name):
    _, rows, cols = w.shape
    rb = rows
    while rb * cols > ELEMS_PER_STEP and rb % 16 == 0:
        rb //= 2

    def body(p_ref, w_ref, m_ref, v_ref, grad_ref, delta_ref, nm_ref, nv_ref):
        g = p_ref[0].astype(F32)
        for k in range(1, p_ref.shape[0]):
            g = g + p_ref[k].astype(F32)
        delta, nm, nv = _adamw_math(w_ref[0], g, m_ref[0], v_ref[0])
        grad_ref[0] = g
        delta_ref[0] = delta
        nm_ref[0] = nm
        nv_ref[0] = nv

    block = lambda lead: pl.BlockSpec((lead, rb, cols), lambda i: (0, i, 0))
    out = jax.ShapeDtypeStruct(w.shape, F32)
    return pl.pallas_call(
        body,
        name=name,
        grid=(rows // rb,),
        out_shape=[out] * 4,
        in_specs=[block(parts.shape[0])] + [block(1)] * 3,
        out_specs=[block(1)] * 4,
        compiler_params=pltpu.CompilerParams(dimension_semantics=("arbitrary",), vmem_limit_bytes=VMEM_LIMIT),
    )(parts, w, m, v)


def _adamw_multi(items, name):
    n = len(items)

    def body(*refs):
        ins, outs = refs[: 4 * n], refs[4 * n :]
        for a in range(n):
            w_ref, g_ref, m_ref, v_ref = ins[4 * a : 4 * a + 4]
            d_ref, nm_ref, nv_ref = outs[3 * a : 3 * a + 3]
            delta, nm, nv = _adamw_math(w_ref[...], g_ref[...], m_ref[...], v_ref[...])
            d_ref[...] = delta
            nm_ref[...] = nm
            nv_ref[...] = nv

    flat = [a for it in items for a in it]
    out_shape = [jax.ShapeDtypeStruct(it[0].shape, F32) for it in items for _ in range(3)]
    outs = pl.pallas_call(
        body,
        name=name,
        out_shape=out_shape,
        in_specs=[VMEM_SPEC] * (4 * n),
        out_specs=[VMEM_SPEC] * (3 * n),
        compiler_params=pltpu.CompilerParams(vmem_limit_bytes=VMEM_LIMIT),
    )(*flat)
    return [tuple(outs[3 * a : 3 * a + 3]) for a in range(n)]


def _prologue(w_in, w_out, conv_w, c, w_ada, b_mine):
    D = c.shape[1]
    wc = w_ada.shape[2]
    shards16 = [jax.ShapeDtypeStruct(w_in.shape[1:], BF16), jax.ShapeDtypeStruct(w_out.shape[1:], BF16)]

    def to_all(src, out, sems):
        x, y, c = _my_place()
        me = _slot(x, y, c)
        copies = [
            pltpu.make_async_remote_copy(
                src_ref=src, dst_ref=out.at[me], send_sem=sems[0].at[k - 1], recv_sem=sems[1].at[k - 1],
                device_id=(x ^ (k >> 2), y ^ ((k >> 1) & 1), c ^ (k & 1)), device_id_type=MESH)
            for k in range(1, N_DEV)
        ]
        return copies, pltpu.make_async_copy(src, out.at[me], sems[2].at[0])

    def start(copies, own):
        for cp in copies + [own]:
            cp.start()

    def finish(copies, own):
        for cp in copies:
            cp.wait()
        own.wait()

    def body(win_ref, wout_ref, cw_ref, c_ref, wada_ref, b_ref, win_g, wout_g, cw_g, c_g, mod_g, win16, wout16, c_s, mp_s, *sems):
        c_copies = to_all(c_ref, c_g, sems[3:6])
        start(*c_copies)
        win16[...] = win_ref[0].astype(BF16)
        wout16[...] = wout_ref[0].astype(BF16)
        w_ins, w_outs, w_sems = (win16, wout16, cw_ref), (win_g, wout_g, cw_g), sems[0:3]
        _gather_start(w_ins, w_outs, w_sems)
        finish(*c_copies)
        for k in range(N_DEV):
            c_s[k : k + 1, :] = c_g[k]
        cv = c_s[...]
        cond = cv * jax.nn.sigmoid(cv)
        mp_s[...] = jnp.dot(cond, wada_ref[0], precision=lax.Precision.HIGHEST, preferred_element_type=F32) + b_ref[...]
        m_copies = to_all(mp_s, mod_g, sems[6:9])
        start(*m_copies)
        for j in range(3):
            _gather_forward(w_ins, w_outs, w_sems, j)
        _gather_finish(w_ins, w_outs, w_sems)
        finish(*m_copies)

    return pl.pallas_call(
        body,
        name="prologue",
        out_shape=_gather_out_shape(shards16 + [conv_w], [True, False, False])
        + [jax.ShapeDtypeStruct((N_DEV, 1, D), F32), jax.ShapeDtypeStruct((N_DEV, N_DEV, wc), F32)],
        in_specs=[VMEM_SPEC, VMEM_SPEC, ANY_SPEC] + [VMEM_SPEC] * 3,
        out_specs=[ANY_SPEC] * 3 + [VMEM_SPEC] * 2,
        scratch_shapes=[pltpu.VMEM(s.shape, BF16) for s in shards16]
        + [pltpu.VMEM((N_DEV, D), F32), pltpu.VMEM((N_DEV, wc), F32)]
        + _gather_scratch(3) + _gather_scratch(1) + _gather_scratch(1),
        compiler_params=pltpu.CompilerParams(vmem_limit_bytes=VMEM_LIMIT),
    )(w_in, w_out, conv_w, c, w_ada, b_mine)


def _ln_fwd(r):
    mu = jnp.mean(r, axis=-1, keepdims=True)
    d = r - mu
    var = jnp.mean(d * d, axis=-1, keepdims=True)
    rstd = lax.rsqrt(var + LN_EPS)
    return d * rstd, rstd


def _ln_bwd(dxh, xhat, rstd):
    m1 = jnp.mean(dxh, axis=-1, keepdims=True)
    m2 = jnp.mean(dxh * xhat, axis=-1, keepdims=True)
    return rstd * (dxh - m1 - xhat * m2)


def _colsum(a):
    return jnp.sum(a, axis=0, keepdims=True)


def _window_sums(ext, tm, causal):
    n = ext.shape[0]
    lo = HALO if causal else 0
    s, out = ext, []
    for p in range(len(POOL_WINDOWS)):
        assert POOL_WINDOWS[p] == 2 ** (p + 1)
        k = 2**p
        s = s + pltpu.roll(s, k if causal else n - k, 0)
        out.append(s[lo : lo + tm, 0:128])
        if p + 1 < len(POOL_WINDOWS):
            s = s[:, 128:]
    return out


def _pool_features(vp, vp_s, row, tm):
    sums = _window_sums(vp_s[...], tm, causal=True)
    feats, inv_cnts = [], []
    for g, win in enumerate(POOL_WINDOWS):
        inv_cnt = 1.0 / jnp.minimum(row + 1, win).astype(F32)
        feats.append(sums[g] * inv_cnt - vp[:, 128 * g : 128 * g + 128])
        inv_cnts.append(inv_cnt)
    return feats, inv_cnts


def _f1(x, mod, w_in, conv_w, w_pool, pool_scale, w_out, tm, gather, by_cols):
    T, D = x.shape
    ZW = w_in.shape[1]
    CC = ZW // 4
    nt = T // tm
    ng = len(gather)
    fwd_steps = [max(nt - 3 + j, 0) for j in range(3)]

    shards16 = [jax.ShapeDtypeStruct(s.shape[1:], BF16) for s in gather]

    def body(*refs):
        x_ref, mod_ref, win_ref, cw_ref, wp_ref, ps_ref, wout_ref = refs[:7]
        g_f32 = refs[7 : 7 + ng]
        z_ref, h_ref, xhat_ref, rstd_ref, mix_ref = refs[7 + ng : 12 + ng]
        g_outs = refs[12 + ng : 12 + 2 * ng]
        cv_s, vp_s = refs[12 + 2 * ng : 14 + 2 * ng]
        g_ins = refs[14 + 2 * ng : 14 + 3 * ng]
        g_sems = refs[14 + 3 * ng :]
        i = pl.program_id(0)

        @pl.when(i == 0)
        def _():
            for src, dst in zip(g_f32, g_ins):
                dst[...] = src[0].astype(BF16)
            _gather_start(g_ins, g_outs, g_sems)
            cv_s[0:HALO, :] = jnp.zeros((HALO, CC), F32)
            vp_s[0:HALO, :] = jnp.zeros((HALO, CC), F32)

        xv = x_ref[...]
        sh1, sc1, g1 = mod_ref[0:1, :], mod_ref[1:2, :], mod_ref[2:3, :]
        h = (xv * (1.0 + sc1) + sh1).astype(BF16)
        h_ref[...] = h
        z = jnp.dot(h, win_ref[...], preferred_element_type=F32)
        z_ref[...] = z.astype(BF16)
        gb, gc, vc, vp = z[:, 0:CC], z[:, CC : 2 * CC], z[:, 2 * CC : 3 * CC], z[:, 3 * CC : 4 * CC]
        cv = gc * vc
        cv_s[HALO : HALO + tm, :] = cv
        vp_s[HALO : HALO + tm, :] = vp
        cv_ext = cv_s[...]
        cv_m2 = pltpu.roll(cv_ext, 2, 0)[HALO : HALO + tm, :]
        cv_m1 = pltpu.roll(cv_ext, 1, 0)[HALO : HALO + tm, :]
        conv = cw_ref[0:1, :] * cv_m2 + cw_ref[1:2, :] * cv_m1 + cw_ref[2:3, :] * cv
        parts = [gb * conv]
        row = i * tm + lax.broadcasted_iota(jnp.int32, (tm, 1), 0)
        feats, _ = _pool_features(vp, vp_s, row, tm)
        for g in range(len(POOL_WINDOWS)):
            pw = jnp.dot(feats[g].astype(BF16), wp_ref[g].astype(BF16), preferred_element_type=F32)
            parts.append(pw * ps_ref[0:1, 128 * g : 128 * g + 128])
        cv_s[0:HALO, :] = cv_s[tm : tm + HALO, :]
        vp_s[0:HALO, :] = vp_s[tm : tm + HALO, :]
        ycat = jnp.concatenate(parts, axis=1).astype(BF16)
        mix = jnp.dot(ycat, wout_ref[...], preferred_element_type=F32)
        mix_ref[...] = mix
        xhat, rstd = _ln_fwd(DEEPNORM_ALPHA * xv + (1.0 + g1) * mix)
        xhat_ref[...] = xhat
        rstd_ref[...] = rstd

        for j in range(3):

            @pl.when(i == fwd_steps[j])
            def _(j=j):
                _gather_forward(g_ins, g_outs, g_sems, j)

        @pl.when(i == nt - 1)
        def _():
            _gather_finish(g_ins, g_outs, g_sems)

    tile = lambda w: pl.BlockSpec((tm, w), lambda i: (i, 0))
    return pl.pallas_call(
        body,
        name="f1",
        grid=(nt,),
        out_shape=[
            jax.ShapeDtypeStruct((T, ZW), BF16),
            jax.ShapeDtypeStruct((T, D), BF16),
            jax.ShapeDtypeStruct((T, D), F32),
            jax.ShapeDtypeStruct((T, 1), F32),
            jax.ShapeDtypeStruct((T, D), F32),
        ]
        + _gather_out_shape(shards16, by_cols),
        in_specs=[tile(D)] + [VMEM_SPEC] * (6 + ng),
        out_specs=[tile(ZW), tile(D), tile(D), tile(1), tile(D)] + [ANY_SPEC] * ng,
        scratch_shapes=[pltpu.VMEM((HALO + tm, CC), F32), pltpu.VMEM((HALO + tm, CC), F32)]
        + [pltpu.VMEM(s.shape, BF16) for s in shards16]
        + _gather_scratch(ng),
        compiler_params=pltpu.CompilerParams(dimension_semantics=("arbitrary",), vmem_limit_bytes=VMEM_LIMIT),
    )(x, mod, w_in, conv_w, w_pool, pool_scale, w_out, *gather)


def _fb2(xhat1, target, mod, ln, w_mi, w_mo, tm):
    T, D = xhat1.shape
    H = w_mi.shape[1]
    hc = min(1024, H)
    nb = H // hc
    nt = T // tm

    def body(xh1_ref, t_ref, mod_ref, ln_ref, wmi_ref, wmo_ref, dx1_ref, h2_ref, a_ref, du_ref, df_ref, acc_ref):
        i = pl.program_id(0)

        @pl.when(i == 0)
        def _():
            acc_ref[...] = jnp.zeros((8, D), F32)

        sh2, sc2, g2 = mod_ref[3:4, :], mod_ref[4:5, :], mod_ref[5:6, :]
        x1 = xh1_ref[...] * ln_ref[0:1, :] + ln_ref[1:2, :]
        h2 = (x1 * (1.0 + sc2) + sh2).astype(BF16)
        h2_ref[...] = h2
        f = jnp.zeros((tm, D), F32)
        for k in range(nb):
            ks = slice(k * hc, (k + 1) * hc)
            r = jnp.maximum(jnp.dot(h2, wmi_ref[:, ks], preferred_element_type=F32), 0.0)
            du_ref[:, ks] = r.astype(BF16)
            a = (r * r).astype(BF16)
            a_ref[:, ks] = a
            f = f + jnp.dot(a, wmo_ref[ks, :], preferred_element_type=F32)
        xhat2, rstd2 = _ln_fwd(DEEPNORM_ALPHA * x1 + (1.0 + g2) * f)
        ln2_g = ln_ref[2:3, :]
        d = xhat2 * ln2_g + ln_ref[3:4, :] - t_ref[...]
        dr2 = _ln_bwd(d * (ln2_g * (1.0 / D)), xhat2, rstd2)
        df = ((1.0 + g2) * dr2).astype(BF16)
        df_ref[...] = df
        dh2 = jnp.zeros((tm, D), F32)
        for k in range(nb):
            ks = slice(k * hc, (k + 1) * hc)
            da = lax.dot_general(df, wmo_ref[ks, :], NT, preferred_element_type=F32)
            du = (da * (2.0 * du_ref[:, ks].astype(F32))).astype(BF16)
            du_ref[:, ks] = du
            dh2 = dh2 + lax.dot_general(du, wmi_ref[:, ks], NT, preferred_element_type=F32)
        dx1_ref[...] = DEEPNORM_ALPHA * dr2 + dh2 * (1.0 + sc2)
        acc_ref[0:1, :] += _colsum(d * xhat2) * (1.0 / D)
        acc_ref[1:2, :] += _colsum(d) * (1.0 / D)
        acc_ref[2:3, :] += _colsum(dh2)
        acc_ref[3:4, :] += _colsum(dh2 * x1)
        acc_ref[4:5, :] += _colsum(dr2 * f)
        acc_ref[5:6, :] += jnp.zeros((1, D), F32) + (0.5 / D) * jnp.sum(d * d)

    tile = lambda w: pl.BlockSpec((tm, w), lambda i: (i, 0))
    return pl.pallas_call(
        body,
        name="fb2",
        grid=(nt,),
        out_shape=[
            jax.ShapeDtypeStruct((T, D), F32),
            jax.ShapeDtypeStruct((T, D), BF16),
            jax.ShapeDtypeStruct((T, H), BF16),
            jax.ShapeDtypeStruct((T, H), BF16),
            jax.ShapeDtypeStruct((T, D), BF16),
            jax.ShapeDtypeStruct((8, D), F32),
        ],
        in_specs=[tile(D), tile(D)] + [VMEM_SPEC] * 4,
        out_specs=[tile(D), tile(D), tile(H), tile(H), tile(D), pl.BlockSpec((8, D), lambda i: (0, 0))],
        compiler_params=pltpu.CompilerParams(dimension_semantics=("arbitrary",), vmem_limit_bytes=VMEM_LIMIT),
    )(xhat1, target, mod, ln, w_mi, w_mo)


def _b1(dx1, xhat1, rstd1, x, mix, z, mod, ln, w_out, w_in, conv_w, w_pool, pool_scale, tm):
    T, D = x.shape
    ZW = w_in.shape[1]
    CC = ZW // 4
    nt = T // tm
    hb = tm // HALO

    def body(dx1_ref, xh1_ref, rstd_ref, x_ref, mix_ref, z_ref, zh_ref, mod_ref, ln_ref, wout_ref, win_ref, cw_ref, wp_ref, ps_ref,
             dx_ref, dmix_ref, ycat_ref, dz_ref, acc_ref, gcw_ref, gwp_ref, cv_s, vp_s, e_s, q_s):
        i = pl.program_id(0)
        j = nt - 1 - i

        @pl.when(i == 0)
        def _():
            acc_ref[...] = jnp.zeros((8, D), F32)
            gcw_ref[...] = jnp.zeros((8, CC), F32)
            gwp_ref[...] = jnp.zeros(gwp_ref.shape, F32)
            e_s[tm : tm + HALO, :] = jnp.zeros((HALO, CC), F32)
            q_s[tm : tm + HALO, :] = jnp.zeros((HALO, CC), F32)

        sh1, sc1, g1 = mod_ref[0:1, :], mod_ref[1:2, :], mod_ref[2:3, :]
        dx1 = dx1_ref[...]
        xhat1 = xh1_ref[...]
        acc_ref[0:1, :] += _colsum(dx1 * xhat1)
        acc_ref[1:2, :] += _colsum(dx1)
        dr1 = _ln_bwd(dx1 * ln_ref[0:1, :], xhat1, rstd_ref[...])
        acc_ref[4:5, :] += _colsum(dr1 * mix_ref[...])
        dmix = ((1.0 + g1) * dr1).astype(BF16)
        dmix_ref[...] = dmix
        dycat = lax.dot_general(dmix, wout_ref[...], NT, preferred_element_type=F32)

        z = z_ref[...].astype(F32)
        zh = zh_ref[...].astype(F32) * jnp.where(j > 0, 1.0, 0.0)
        gb, gc, vc, vp = z[:, 0:CC], z[:, CC : 2 * CC], z[:, 2 * CC : 3 * CC], z[:, 3 * CC : 4 * CC]
        cv = gc * vc
        cv_s[0:HALO, :] = zh[:, CC : 2 * CC] * zh[:, 2 * CC : 3 * CC]
        cv_s[HALO : HALO + tm, :] = cv
        vp_s[0:HALO, :] = zh[:, 3 * CC : 4 * CC]
        vp_s[HALO : HALO + tm, :] = vp
        cv_ext = cv_s[...]
        cv_m2 = pltpu.roll(cv_ext, 2, 0)[HALO : HALO + tm, :]
        cv_m1 = pltpu.roll(cv_ext, 1, 0)[HALO : HALO + tm, :]
        w0, w1, w2 = cw_ref[0:1, :], cw_ref[1:2, :], cw_ref[2:3, :]
        conv = w0 * cv_m2 + w1 * cv_m1 + w2 * cv
        dyc = dycat[:, 0:CC]
        e = dyc * gb
        e_s[0:tm, :] = e
        e_ext = e_s[...]
        dcv = w2 * e + w1 * pltpu.roll(e_ext, tm + HALO - 1, 0)[0:tm, :] + w0 * pltpu.roll(e_ext, tm + HALO - 2, 0)[0:tm, :]
        gcw_ref[0:1, :] += _colsum(e * cv_m2)
        gcw_ref[1:2, :] += _colsum(e * cv_m1)
        gcw_ref[2:3, :] += _colsum(e * cv)
        y_parts = [gb * conv]
        dz_parts = [dyc * conv, dcv * vc, dcv * gc]

        row = j * tm + lax.broadcasted_iota(jnp.int32, (tm, 1), 0)
        feats, inv_cnts = _pool_features(vp, vp_s, row, tm)
        gps_parts, dps = [], []
        for g in range(len(POOL_WINDOWS)):
            cols = slice(128 * g, 128 * g + 128)
            p = feats[g].astype(BF16)
            scale = ps_ref[0:1, cols]
            wp = wp_ref[g].astype(BF16)
            pw = jnp.dot(p, wp, preferred_element_type=F32)
            y_parts.append(pw * scale)
            dyp = dycat[:, CC + 128 * g : CC + 128 * g + 128]
            gps_parts.append(_colsum(dyp * pw))
            dpw = (dyp * scale).astype(BF16)
            gwp_ref[g] += lax.dot_general(p, dpw, TN, preferred_element_type=F32)
            dp = lax.dot_general(dpw, wp, NT, preferred_element_type=F32)
            q_s[0:tm, cols] = dp * inv_cnts[g]
            dps.append(dp)
        sq = _window_sums(q_s[...], tm, causal=False)
        dz_parts += [sq[g] - dps[g] for g in range(len(POOL_WINDOWS))]
        gcw_ref[3:4, :] += jnp.concatenate(gps_parts, axis=1)
        ycat_ref[...] = jnp.concatenate(y_parts, axis=1).astype(BF16)
        dz = jnp.concatenate(dz_parts, axis=1).astype(BF16)
        dz_ref[...] = dz
        dh = lax.dot_general(dz, win_ref[...], NT, preferred_element_type=F32)
        acc_ref[2:3, :] += _colsum(dh)
        acc_ref[3:4, :] += _colsum(dh * x_ref[...])
        dx_ref[...] = DEEPNORM_ALPHA * dr1 + dh * (1.0 + sc1)
        e_s[tm : tm + HALO, :] = e_s[0:HALO, :]
        q_s[tm : tm + HALO, :] = q_s[0:HALO, :]

    tile = lambda w: pl.BlockSpec((tm, w), lambda i: (nt - 1 - i, 0))
    halo = pl.BlockSpec((HALO, ZW), lambda i: (jnp.maximum((nt - 1 - i) * hb - 1, 0), 0))
    fixed = lambda shape: pl.BlockSpec(shape, lambda i: (0,) * len(shape))
    return pl.pallas_call(
        body,
        name="b1",
        grid=(nt,),
        out_shape=[
            jax.ShapeDtypeStruct((T, D), F32),
            jax.ShapeDtypeStruct((T, D), BF16),
            jax.ShapeDtypeStruct((T, D), BF16),
            jax.ShapeDtypeStruct((T, ZW), BF16),
            jax.ShapeDtypeStruct((8, D), F32),
            jax.ShapeDtypeStruct((8, CC), F32),
            jax.ShapeDtypeStruct(w_pool.shape, F32),
        ],
        in_specs=[tile(D), tile(D), tile(1), tile(D), tile(D), tile(ZW), halo] + [VMEM_SPEC] * 7,
        out_specs=[tile(D), tile(D), tile(D), tile(ZW), fixed((8, D)), fixed((8, CC)), fixed(w_pool.shape)],
        scratch_shapes=[
            pltpu.VMEM((HALO + tm, CC), F32),
            pltpu.VMEM((HALO + tm, CC), F32),
            pltpu.VMEM((tm + HALO, CC), F32),
            pltpu.VMEM((tm + HALO, CC), F32),
        ],
        compiler_params=pltpu.CompilerParams(dimension_semantics=("arbitrary",), vmem_limit_bytes=VMEM_LIMIT),
    )(dx1, xhat1, rstd1, x, mix, z, z, mod, ln, w_out, w_in, conv_w, w_pool, pool_scale)


def _wgrad(a, b, bk, n_groups, bt, name, owners=None, scatter=None):
    T, K = a.shape
    N = b.shape[1]
    nk, nt, ng = K // bk, T // bt, N // n_groups
    nc = min(512, ng)
    ns = 0 if scatter is None else 1
    n_steps = nk * n_groups * nt
    mid_step = min(1, n_steps - 1)

    def body(*refs):
        a_ref, b_ref = refs[0], refs[1]
        o_ref = refs[2 + ns]
        acc = refs[3 + 2 * ns]
        if ns:
            s_hbm, s_recv, s_scr = refs[2], refs[4], refs[6:]
        kk, gg, t = pl.program_id(0), pl.program_id(1), pl.program_id(2)
        step = (kk * n_groups + gg) * nt + t

        if ns:

            @pl.when(step == 0)
            def _():
                _scatter_start(s_hbm, s_scr)

            @pl.when(step == mid_step)
            def _():
                _scatter_middle(s_hbm, s_recv, s_scr)

        @pl.when(t == 0)
        def _():
            acc[...] = jnp.zeros(acc.shape, F32)

        at = a_ref[...].T
        for c in range(ng // nc):
            cs = slice(c * nc, (c + 1) * nc)
            acc[:, cs] += jnp.dot(at, b_ref[:, cs], preferred_element_type=F32)

        @pl.when(t == nt - 1)
        def _():
            if owners is None:
                o_ref[...] = acc[...].astype(BF16)
            else:
                per = N // owners
                for o in range(ng // per):
                    o_ref[o] = acc[:, o * per : (o + 1) * per].astype(BF16)

        if ns:

            @pl.when(step == n_steps - 1)
            def _():
                _scatter_finish(s_hbm, s_recv, s_scr)

    if owners is None:
        out_shape = [jax.ShapeDtypeStruct((K, N), BF16)]
        out_specs = [pl.BlockSpec((bk, ng), lambda k, g, t: (k, g))]
    else:
        assert bk == K
        per = N // owners
        out_shape = [jax.ShapeDtypeStruct((owners, K, per), BF16)]
        out_specs = [pl.BlockSpec((ng // per, K, per), lambda k, g, t: (g, 0, 0))]
    ins, in_specs = [a, b], [pl.BlockSpec((bt, bk), lambda k, g, t: (t, k)), pl.BlockSpec((bt, ng), lambda k, g, t: (t, g))]
    scratch = [pltpu.VMEM((bk, ng), F32)]
    if ns:
        ins.append(scatter)
        in_specs.append(ANY_SPEC)
        out_shape.append(_scatter_out_shape(scatter))
        out_specs.append(ANY_SPEC)
        scratch += _scatter_scratch(*scatter.shape[1:])
    outs = pl.pallas_call(
        body,
        name=name,
        grid=(nk, n_groups, nt),
        out_shape=out_shape,
        in_specs=in_specs,
        out_specs=out_specs,
        scratch_shapes=scratch,
        compiler_params=pltpu.CompilerParams(dimension_semantics=("arbitrary", "arbitrary", "arbitrary"), vmem_limit_bytes=VMEM_LIMIT),
    )(*ins)
    return outs if ns else outs[0]


def _small_grads(acc1_all, acc2_all, gcw_all, gwp_all, cond_t, my_slot, w_cols):
    D = acc1_all.shape[2]
    n_chunk = D // 128
    q_mine = w_cols // 128

    def total(ref, r):
        s = ref[0, r : r + 1, :]
        for k in range(1, N_DEV):
            s = s + ref[k, r : r + 1, :]
        return s

    def body(slot_ref, a1_ref, a2_ref, gcw_ref, gwp_ref, ct_ref, gb_ref, gw_ref, gln_ref, gcwo_ref, gwpo_ref, loss_ref, dm_s):
        loss_ref[...] = total(a2_ref, 5)
        for s, (ref, r) in enumerate([(a1_ref, 2), (a1_ref, 3), (a1_ref, 4), (a2_ref, 2), (a2_ref, 3), (a2_ref, 4)]):
            gb_ref[0:1, s * D : (s + 1) * D] = total(ref, r)
            for k in range(N_DEV):
                row = ref[k, r : r + 1, :]
                for qq in range(n_chunk):
                    dm_s[s * n_chunk + qq, k : k + 1, :] = row[:, 128 * qq : 128 * qq + 128]
        gln_ref[0:1, :] = total(a1_ref, 0)
        gln_ref[1:2, :] = total(a1_ref, 1)
        gln_ref[2:3, :] = total(a2_ref, 0)
        gln_ref[3:4, :] = total(a2_ref, 1)
        gcwo_ref[...] = jnp.zeros(gcwo_ref.shape, F32)
        for r in range(4):
            gcwo_ref[r : r + 1, :] = total(gcw_ref, r)
        wp = gwp_ref[0]
        for k in range(1, N_DEV):
            wp = wp + gwp_ref[k]
        gwpo_ref[0] = wp
        ct = ct_ref[...]
        cond_t = ct * jax.nn.sigmoid(ct)
        q0 = slot_ref[0] * q_mine
        for q in range(q_mine):
            dm = dm_s[q0 + q]
            out = cond_t[:, 0:1] * dm[0:1, :]
            for k in range(1, N_DEV):
                out = out + cond_t[:, k : k + 1] * dm[k : k + 1, :]
            gw_ref[0, :, 128 * q : 128 * q + 128] = out

    CC = gcw_all.shape[2]
    return pl.pallas_call(
        body,
        name="small_grads",
        out_shape=[
            jax.ShapeDtypeStruct((1, 6 * D), F32),
            jax.ShapeDtypeStruct((1, D, w_cols), F32),
            jax.ShapeDtypeStruct((4, D), F32),
            jax.ShapeDtypeStruct((8, CC), F32),
            jax.ShapeDtypeStruct((1, *gwp_all.shape[1:]), F32),
            jax.ShapeDtypeStruct((1, D), F32),
        ],
        in_specs=[pl.BlockSpec(memory_space=pltpu.SMEM)] + [VMEM_SPEC] * 5,
        out_specs=[VMEM_SPEC] * 6,
        scratch_shapes=[pltpu.VMEM((6 * n_chunk, N_DEV, 128), F32)],
        compiler_params=pltpu.CompilerParams(vmem_limit_bytes=VMEM_LIMIT),
    )(my_slot, acc1_all, acc2_all, gcw_all, gwp_all, cond_t)


def kernel(x, c, w_ada, b_ada, w_in, conv_w, w_pool, pool_scale, w_out, ln1_g, ln1_b, w_mlp_in, w_mlp_out, ln2_g, ln2_b, loss_target, m_w_ada, m_b_ada, m_w_in, m_conv_w, m_w_pool, m_pool_scale, m_w_out, m_ln1_g, m_ln1_b, m_w_mlp_in, m_w_mlp_out, m_ln2_g, m_ln2_b, v_w_ada, v_b_ada, v_w_in, v_conv_w, v_w_pool, v_pool_scale, v_w_out, v_ln1_g, v_ln1_b, v_w_mlp_in, v_w_mlp_out, v_ln2_g, v_ln2_b):
    T, D = x.shape[1], x.shape[2]
    H = w_mlp_out.shape[1] * N_DEV
    ZW = w_in.shape[2] * N_DEV
    CC = ZW // 4
    tm = min(512, T // 2)
    bt = min(1024, T)
    ax, ay, ac = _my_place()
    me = _slot(ax, ay, ac)

    w_cols = w_ada.shape[2]
    b_mine = lax.dynamic_slice(b_ada, (0, me * w_cols), (1, w_cols))
    w_in_f, w_out_g, cw_g, c_g, mod_g = _prologue(w_in, w_out, conv_w[0], c, w_ada, b_mine)
    w_out_f = w_out_g.reshape(D, D)
    conv_w_f = jnp.transpose(cw_g, (1, 0, 2)).reshape(conv_w.shape[1], CC)
    c_all = c_g.reshape(N_DEV, D)
    mod = lax.dynamic_index_in_dim(mod_g, me, axis=1, keepdims=False).reshape(6, D)

    ln = jnp.concatenate([ln1_g, ln1_b, ln2_g, ln2_b], axis=0)
    xs, target = x[0], loss_target[0]

    z, h, xhat1, rstd1, mix, w_mi_f, w_mo_g = _f1(
        xs, mod, w_in_f, conv_w_f, w_pool[0], pool_scale, w_out_f, tm,
        [w_mlp_in, w_mlp_out], [True, False])
    dx1, h2, a, du, df, acc2 = _fb2(xhat1, target, mod, ln, w_mi_f, w_mo_g.reshape(H, D), tm)

    grad_x, dmix, ycat, dz, acc1, gcw, gwp = _b1(
        dx1, xhat1, rstd1, xs, mix, z, mod, ln, w_out_f, w_in_f, conv_w_f, w_pool[0], pool_scale, tm)

    gp_mo = _wgrad(a, df, D, 1, min(2 * bt, T), "wgrad_mlp_out").reshape(N_DEV, H // N_DEV, D)
    gp_mi, rv_mo = _wgrad(h2, du, D, 2, min(2 * bt, T), "wgrad_mlp_in", owners=N_DEV, scatter=gp_mo)
    gp_in, rv_mi = _wgrad(h, dz, D, 1, bt, "wgrad_in", owners=N_DEV, scatter=gp_mi)
    gp_out, rv_in = _wgrad(ycat, dmix, D, 1, bt, "wgrad_out", scatter=gp_in)
    gp_out = gp_out.reshape(N_DEV, D // N_DEV, D)

    acc1_g, acc2_g, gcw_g, gwp_g, rv_out = _allgather([acc1, acc2, gcw, gwp], "gather_small", in_vmem=True, scatter=gp_out)
    g_b_ada, g_w_ada, g_ln, g_cw, g_w_pool, loss_row = _small_grads(
        acc1_g, acc2_g, gcw_g, gwp_g, c_all.T, jnp.reshape(me, (1,)).astype(jnp.int32), w_cols)
    cc_mine = conv_w.shape[2]
    g_conv_w = lax.dynamic_slice(g_cw, (0, me * cc_mine), (conv_w.shape[1], cc_mine))[None]
    g_pool_scale = g_cw[3:4, :]
    g_ln1_g, g_ln1_b, g_ln2_g, g_ln2_b = g_ln[0:1], g_ln[1:2], g_ln[2:3], g_ln[3:4]

    small = _adamw_multi(
        [
            (b_ada, g_b_ada, m_b_ada, v_b_ada),
            (conv_w, g_conv_w, m_conv_w, v_conv_w),
            (w_pool, g_w_pool, m_w_pool, v_w_pool),
            (pool_scale, g_pool_scale, m_pool_scale, v_pool_scale),
            (ln1_g, g_ln1_g, m_ln1_g, v_ln1_g),
            (ln1_b, g_ln1_b, m_ln1_b, v_ln1_b),
            (ln2_g, g_ln2_g, m_ln2_g, v_ln2_g),
            (ln2_b, g_ln2_b, m_ln2_b, v_ln2_b),
        ],
        "adamw_small")
    u_b_ada, u_conv_w, u_w_pool, u_pool_scale, u_ln1_g, u_ln1_b, u_ln2_g, u_ln2_b = small

    g_w_ada, *u_w_ada = _sum_adamw(g_w_ada, w_ada, m_w_ada, v_w_ada, "adamw_w_ada")
    g_w_mo, *u_w_mo = _sum_adamw(rv_mo, w_mlp_out, m_w_mlp_out, v_w_mlp_out, "sum_w_mlp_out")
    g_w_mi, *u_w_mi = _sum_adamw(rv_mi, w_mlp_in, m_w_mlp_in, v_w_mlp_in, "sum_w_mlp_in")
    g_w_in, *u_w_in = _sum_adamw(rv_in, w_in, m_w_in, v_w_in, "sum_w_in")
    g_w_out, *u_w_out = _sum_adamw(rv_out, w_out, m_w_out, v_w_out, "sum_w_out")

    grads = [g_w_ada, g_b_ada, g_w_in, g_conv_w, g_w_pool, g_pool_scale, g_w_out, g_ln1_g, g_ln1_b, g_w_mi, g_w_mo, g_ln2_g, g_ln2_b]
    updates = [u_w_ada, u_b_ada, u_w_in, u_conv_w, u_w_pool, u_pool_scale, u_w_out, u_ln1_g, u_ln1_b, u_w_mi, u_w_mo, u_ln2_g, u_ln2_b]
    deltas = [u[0] for u in updates]
    new_m = [u[1] for u in updates]
    new_v = [u[2] for u in updates]
    return (loss_row[0, 0], grad_x[None], *grads, *deltas, *new_m, *new_v)
```

```python
import jax
import jax.numpy as jnp
from jax import lax
from jax.experimental import pallas as pl
from jax.experimental.pallas import tpu as pltpu

F32 = jnp.float32
BF16 = jnp.bfloat16
MESH = pl.DeviceIdType.MESH
N_DEV = 8

LN_EPS = 1e-5
DEPTH = 1
DEEPNORM_ALPHA = (2.0 * DEPTH) ** 0.25
POOL_WINDOWS = (2, 4, 8, 16)
HALO = 16

ADAM_LR = 0.001
ADAM_B1 = 0.9
ADAM_B2 = 0.999
ADAM_EPS = 1e-08
ADAM_WD = 0.01
ADAM_STEP = 10

VMEM_LIMIT = 60 * 1024 * 1024

VMEM_SPEC = pl.BlockSpec(memory_space=pltpu.VMEM)
ANY_SPEC = pl.BlockSpec(memory_space=pl.ANY)

NT = (((1,), (1,)), ((), ()))
TN = (((0,), (0,)), ((), ()))


def _my_place():
    return lax.axis_index("x"), lax.axis_index("y"), lax.axis_index("c")


def _slot(x, y, c):
    return 4 * x + 2 * y + c


def _gather_place(ins, outs, a, slot):
    if len(outs[a].shape) == len(ins[a].shape):
        wb = ins[a].shape[1]
        return outs[a].at[:, pl.ds(pl.multiple_of(slot * wb, wb), wb)]
    return outs[a].at[slot]


def _gather_copy(ins, outs, sems, a, k, block, to, from_shard=False):
    send_sems, recv_sems, _ = sems
    dst = _gather_place(ins, outs, a, _slot(*block))
    return pltpu.make_async_remote_copy(
        src_ref=ins[a] if from_shard else dst,
        dst_ref=dst,
        send_sem=send_sems.at[7 * a + k],
        recv_sem=recv_sems.at[7 * a + k],
        device_id=to,
        device_id_type=MESH,
    )


def _gather_peers():
    x, y, c = _my_place()
    return (x, y, c), (x, y, 1 - c), [(1 - x, y), (x, 1 - y), (1 - x, 1 - y)]


def _gather_first(ins, outs, sems):
    me, sibling, chips = _gather_peers()
    first = []
    for a in range(len(ins)):
        first.append(_gather_copy(ins, outs, sems, a, 0, me, sibling, from_shard=True))
        first += [_gather_copy(ins, outs, sems, a, 1 + j, me, (*chip, me[2]), from_shard=True) for j, chip in enumerate(chips)]
    return first


def _gather_mine(ins, outs, sems, a):
    me, _, _ = _gather_peers()
    return pltpu.make_async_copy(ins[a], _gather_place(ins, outs, a, _slot(*me)), sems[2].at[a])


def _gather_start(ins, outs, sems):
    for a in range(len(ins)):
        _gather_mine(ins, outs, sems, a).start()
    for cp in _gather_first(ins, outs, sems):
        cp.start()


def _gather_forward(ins, outs, sems, j):
    me, sibling, chips = _gather_peers()
    for a in range(len(ins)):
        _gather_copy(ins, outs, sems, a, 1 + j, (*chips[j], me[2]), me).wait_recv()
        _gather_copy(ins, outs, sems, a, 4 + j, (*chips[j], me[2]), sibling).start()


def _gather_finish(ins, outs, sems):
    me, sibling, chips = _gather_peers()
    for a in range(len(ins)):
        _gather_copy(ins, outs, sems, a, 0, sibling, me).wait_recv()
        for j, chip in enumerate(chips):
            _gather_copy(ins, outs, sems, a, 4 + j, (*chip, 1 - me[2]), me).wait_recv()
    for cp in _gather_first(ins, outs, sems):
        cp.wait_send()
    for a in range(len(ins)):
        for j, chip in enumerate(chips):
            _gather_copy(ins, outs, sems, a, 4 + j, (*chip, me[2]), sibling).wait_send()
        _gather_mine(ins, outs, sems, a).wait()


def _gather_scratch(n):
    return [pltpu.SemaphoreType.DMA((7 * n,)), pltpu.SemaphoreType.DMA((7 * n,)), pltpu.SemaphoreType.DMA((n,))]


def _gather_out_shape(shards, by_cols):
    return [
        jax.ShapeDtypeStruct((s.shape[0], N_DEV * s.shape[1]) if cols else (N_DEV, *s.shape), s.dtype)
        for s, cols in zip(shards, by_cols)
    ]


N_CHIP = 4


def _scatter_scratch(rows, cols):
    block = pltpu.VMEM((N_CHIP, rows, cols), BF16)
    dma = pltpu.SemaphoreType.DMA
    return [block, block, block, dma((N_CHIP,)), dma((N_CHIP,)), dma((N_CHIP,)), dma((N_CHIP - 1,)), dma((N_CHIP - 1,)), dma]


def _scatter_pair_copies(g_hbm, scr):
    x, y, c = _my_place()
    mine, theirs, _, a_send, a_recv, load_sem = scr[:6]
    to_sibling = [
        pltpu.make_async_remote_copy(
            src_ref=g_hbm.at[2 * q + (1 - c)], dst_ref=theirs.at[q], send_sem=a_send.at[q], recv_sem=a_recv.at[q],
            device_id=(x, y, 1 - c), device_id_type=MESH)
        for q in range(N_CHIP)
    ]
    loads = [pltpu.make_async_copy(g_hbm.at[2 * q + c], mine.at[q], load_sem.at[q]) for q in range(N_CHIP)]
    return to_sibling, loads


def _scatter_sum_copies(recv, scr):
    x, y, c = _my_place()
    sums, b_send, b_recv, own_sem = scr[2], scr[6], scr[7], scr[8]
    q_me = 2 * x + y
    to_owner = [
        pltpu.make_async_remote_copy(
            src_ref=sums.at[2 * px + py], dst_ref=recv.at[q_me], send_sem=b_send.at[j], recv_sem=b_recv.at[j],
            device_id=(px, py, c), device_id_type=MESH)
        for j, (px, py) in enumerate([(1 - x, y), (x, 1 - y), (1 - x, 1 - y)])
    ]
    return to_owner, pltpu.make_async_copy(sums.at[q_me], recv.at[q_me], own_sem)


def _scatter_start(g_hbm, scr):
    to_sibling, loads = _scatter_pair_copies(g_hbm, scr)
    for cp in to_sibling + loads:
        cp.start()


def _scatter_middle(g_hbm, recv, scr):
    to_sibling, loads = _scatter_pair_copies(g_hbm, scr)
    for cp in to_sibling:
        cp.wait_recv()
    for cp in loads:
        cp.wait()
    mine, theirs, sums = scr[:3]

    def step(r, carry):
        rs = pl.ds(pl.multiple_of(r * ROW_CHUNK, ROW_CHUNK), ROW_CHUNK)
        for q in range(N_CHIP):
            sums[q, rs, :] = (mine[q, rs, :].astype(F32) + theirs[q, rs, :].astype(F32)).astype(BF16)
        return carry

    lax.fori_loop(0, mine.shape[1] // ROW_CHUNK, step, 0)
    to_owner, own = _scatter_sum_copies(recv, scr)
    for cp in to_owner + [own]:
        cp.start()


def _scatter_finish(g_hbm, recv, scr):
    to_sibling, _ = _scatter_pair_copies(g_hbm, scr)
    to_owner, own = _scatter_sum_copies(recv, scr)
    for cp in to_owner:
        cp.wait_recv()
    for cp in to_sibling + to_owner:
        cp.wait_send()
    own.wait()


def _scatter_out_shape(gparts):
    return jax.ShapeDtypeStruct((N_CHIP, *gparts.shape[1:]), gparts.dtype)


def _adamw_math(w, g, m, v):
    m = ADAM_B1 * m + (1.0 - ADAM_B1) * g
    v = ADAM_B2 * v + (1.0 - ADAM_B2) * (g * g)
    m_hat = m / (1.0 - ADAM_B1**ADAM_STEP)
    v_hat = v / (1.0 - ADAM_B2**ADAM_STEP)
    delta = -ADAM_LR * (m_hat / (jnp.sqrt(v_hat) + ADAM_EPS) + ADAM_WD * w)
    return delta, m, v


ROW_CHUNK = 64


ELEMS_PER_STEP = 128 * 1024


N_SCATTER_SCRATCH = 9


def _reduce_scatter(gparts, name):
    def body(g_hbm, recv, *scr):
        _scatter_start(g_hbm, scr)
        _scatter_middle(g_hbm, recv, scr)
        _scatter_finish(g_hbm, recv, scr)

    return pl.pallas_call(
        body,
        name=name,
        out_shape=_scatter_out_shape(gparts),
        in_specs=[ANY_SPEC],
        out_specs=ANY_SPEC,
        scratch_shapes=_scatter_scratch(*gparts.shape[1:]),
        compiler_params=pltpu.CompilerParams(vmem_limit_bytes=VMEM_LIMIT),
    )(gparts)


def _sum_adamw(parts, w, m, v, name):
    _, rows, cols = w.shape
    rb = rows
    while rb * cols > ELEMS_PER_STEP and rb % 16 == 0:
        rb //= 2

    def body(p_ref, w_ref, m_ref, v_ref, grad_ref, delta_ref, nm_ref, nv_ref):
        g = p_ref[0].astype(F32)
        for k in range(1, p_ref.shape[0]):
            g = g + p_ref[k].astype(F32)
        delta, nm, nv = _adamw_math(w_ref[0], g, m_ref[0], v_ref[0])
        grad_ref[0] = g
        delta_ref[0] = delta
        nm_ref[0] = nm
        nv_ref[0] = nv

    block = lambda lead: pl.BlockSpec((lead, rb, cols), lambda i: (0, i, 0))
    out = jax.ShapeDtypeStruct(w.shape, F32)
    return pl.pallas_call(
        body,
        name=name,
        grid=(rows // rb,),
        out_shape=[out] * 4,
        in_specs=[block(parts.shape[0])] + [block(1)] * 3,
        out_specs=[block(1)] * 4,
        compiler_params=pltpu.CompilerParams(dimension_semantics=("arbitrary",), vmem_limit_bytes=VMEM_LIMIT),
    )(parts, w, m, v)


def _adamw_multi(items, name):
    n = len(items)

    def body(*refs):
        ins, outs = refs[: 4 * n], refs[4 * n :]
        for a in range(n):
            w_ref, g_ref, m_ref, v_ref = ins[4 * a : 4 * a + 4]
            d_ref, nm_ref, nv_ref = outs[3 * a : 3 * a + 3]
            delta, nm, nv = _adamw_math(w_ref[...], g_ref[...], m_ref[...], v_ref[...])
            d_ref[...] = delta
            nm_ref[...] = nm
            nv_ref[...] = nv

    flat = [a for it in items for a in it]
    out_shape = [jax.ShapeDtypeStruct(it[0].shape, F32) for it in items for _ in range(3)]
    outs = pl.pallas_call(
        body,
        name=name,
        out_shape=out_shape,
        in_specs=[VMEM_SPEC] * (4 * n),
        out_specs=[VMEM_SPEC] * (3 * n),
        compiler_params=pltpu.CompilerParams(vmem_limit_bytes=VMEM_LIMIT),
    )(*flat)
    return [tuple(outs[3 * a : 3 * a + 3]) for a in range(n)]


def _prologue(w_in, w_out, conv_w, c, w_ada, b_mine):
    D = c.shape[1]
    wc = w_ada.shape[2]
    shards16 = [jax.ShapeDtypeStruct(w_in.shape[1:], BF16), jax.ShapeDtypeStruct(w_out.shape[1:], BF16)]

    def to_all(src, out, sems):
        x, y, c = _my_place()
        me = _slot(x, y, c)
        copies = [
            pltpu.make_async_remote_copy(
                src_ref=src, dst_ref=out.at[me], send_sem=sems[0].at[k - 1], recv_sem=sems[1].at[k - 1],
                device_id=(x ^ (k >> 2), y ^ ((k >> 1) & 1), c ^ (k & 1)), device_id_type=MESH)
            for k in range(1, N_DEV)
        ]
        return copies, pltpu.make_async_copy(src, out.at[me], sems[2].at[0])

    def start(copies, own):
        for cp in copies + [own]:
            cp.start()

    def finish(copies, own):
        for cp in copies:
            cp.wait()
        own.wait()

    def body(win_ref, wout_ref, cw_ref, c_ref, wada_ref, b_ref, win_g, wout_g, cw_g, c_g, mod_g, win16, wout16, c_s, mp_s, *sems):
        c_copies = to_all(c_ref, c_g, sems[3:6])
        start(*c_copies)
        win16[...] = win_ref[0].astype(BF16)
        wout16[...] = wout_ref[0].astype(BF16)
        w_ins, w_outs, w_sems = (win16, wout16, cw_ref), (win_g, wout_g, cw_g), sems[0:3]
        _gather_start(w_ins, w_outs, w_sems)
        finish(*c_copies)
        for k in range(N_DEV):
            c_s[k : k + 1, :] = c_g[k]
        cv = c_s[...]
        cond = cv * jax.nn.sigmoid(cv)
        mp_s[...] = jnp.dot(cond, wada_ref[0], precision=lax.Precision.HIGHEST, preferred_element_type=F32) + b_ref[...]
        m_copies = to_all(mp_s, mod_g, sems[6:9])
        start(*m_copies)
        for j in range(3):
            _gather_forward(w_ins, w_outs, w_sems, j)
        _gather_finish(w_ins, w_outs, w_sems)
        finish(*m_copies)

    return pl.pallas_call(
        body,
        name="prologue",
        out_shape=_gather_out_shape(shards16 + [conv_w], [True, False, False])
        + [jax.ShapeDtypeStruct((N_DEV, 1, D), F32), jax.ShapeDtypeStruct((N_DEV, N_DEV, wc), F32)],
        in_specs=[VMEM_SPEC, VMEM_SPEC, ANY_SPEC] + [VMEM_SPEC] * 3,
        out_specs=[ANY_SPEC] * 3 + [VMEM_SPEC] * 2,
        scratch_shapes=[pltpu.VMEM(s.shape, BF16) for s in shards16]
        + [pltpu.VMEM((N_DEV, D), F32), pltpu.VMEM((N_DEV, wc), F32)]
        + _gather_scratch(3) + _gather_scratch(1) + _gather_scratch(1),
        compiler_params=pltpu.CompilerParams(vmem_limit_bytes=VMEM_LIMIT),
    )(w_in, w_out, conv_w, c, w_ada, b_mine)


def _ln_fwd(r):
    mu = jnp.mean(r, axis=-1, keepdims=True)
    d = r - mu
    var = jnp.mean(d * d, axis=-1, keepdims=True)
    rstd = lax.rsqrt(var + LN_EPS)
    return d * rstd, rstd


def _ln_bwd(dxh, xhat, rstd):
    m1 = jnp.mean(dxh, axis=-1, keepdims=True)
    m2 = jnp.mean(dxh * xhat, axis=-1, keepdims=True)
    return rstd * (dxh - m1 - xhat * m2)


def _colsum(a):
    return jnp.sum(a, axis=0, keepdims=True)


def _window_sums(ext, tm, causal):
    n = ext.shape[0]
    lo = HALO if causal else 0
    s, out = ext, []
    for p in range(len(POOL_WINDOWS)):
        assert POOL_WINDOWS[p] == 2 ** (p + 1)
        k = 2**p
        s = s + pltpu.roll(s, k if causal else n - k, 0)
        out.append(s[lo : lo + tm, 0:128])
        if p + 1 < len(POOL_WINDOWS):
            s = s[:, 128:]
    return out


def _pool_features(vp, vp_s, row, tm):
    sums = _window_sums(vp_s[...], tm, causal=True)
    feats, inv_cnts = [], []
    for g, win in enumerate(POOL_WINDOWS):
        inv_cnt = 1.0 / jnp.minimum(row + 1, win).astype(F32)
        feats.append(sums[g] * inv_cnt - vp[:, 128 * g : 128 * g + 128])
        inv_cnts.append(inv_cnt)
    return feats, inv_cnts


def _f1(x, mod, w_in, conv_w, w_pool, pool_scale, w_out, tm, gather, by_cols):
    T, D = x.shape
    ZW = w_in.shape[1]
    CC = ZW // 4
    nt = T // tm
    ng = len(gather)
    fwd_steps = [max(nt - 3 + j, 0) for j in range(3)]

    shards16 = [jax.ShapeDtypeStruct(s.shape[1:], BF16) for s in gather]

    def body(*refs):
        x_ref, mod_ref, win_ref, cw_ref, wp_ref, ps_ref, wout_ref = refs[:7]
        g_f32 = refs[7 : 7 + ng]
        z_ref, h_ref, xhat_ref, rstd_ref, mix_ref = refs[7 + ng : 12 + ng]
        g_outs = refs[12 + ng : 12 + 2 * ng]
        cv_s, vp_s = refs[12 + 2 * ng : 14 + 2 * ng]
        g_ins = refs[14 + 2 * ng : 14 + 3 * ng]
        g_sems = refs[14 + 3 * ng :]
        i = pl.program_id(0)

        @pl.when(i == 0)
        def _():
            for src, dst in zip(g_f32, g_ins):
                dst[...] = src[0].astype(BF16)
            _gather_start(g_ins, g_outs, g_sems)
            cv_s[0:HALO, :] = jnp.zeros((HALO, CC), F32)
            vp_s[0:HALO, :] = jnp.zeros((HALO, CC), F32)

        xv = x_ref[...]
        sh1, sc1, g1 = mod_ref[0:1, :], mod_ref[1:2, :], mod_ref[2:3, :]
        h = (xv * (1.0 + sc1) + sh1).astype(BF16)
        h_ref[...] = h
        z = jnp.dot(h, win_ref[...], preferred_element_type=F32)
        z_ref[...] = z.astype(BF16)
        gb, gc, vc, vp = z[:, 0:CC], z[:, CC : 2 * CC], z[:, 2 * CC : 3 * CC], z[:, 3 * CC : 4 * CC]
        cv = gc * vc
        cv_s[HALO : HALO + tm, :] = cv
        vp_s[HALO : HALO + tm, :] = vp
        cv_ext = cv_s[...]
        cv_m2 = pltpu.roll(cv_ext, 2, 0)[HALO : HALO + tm, :]
        cv_m1 = pltpu.roll(cv_ext, 1, 0)[HALO : HALO + tm, :]
        conv = cw_ref[0:1, :] * cv_m2 + cw_ref[1:2, :] * cv_m1 + cw_ref[2:3, :] * cv
        parts = [gb * conv]
        row = i * tm + lax.broadcasted_iota(jnp.int32, (tm, 1), 0)
        feats, _ = _pool_features(vp, vp_s, row, tm)
        for g in range(len(POOL_WINDOWS)):
            pw = jnp.dot(feats[g].astype(BF16), wp_ref[g].astype(BF16), preferred_element_type=F32)
            parts.append(pw * ps_ref[0:1, 128 * g : 128 * g + 128])
        cv_s[0:HALO, :] = cv_s[tm : tm + HALO, :]
        vp_s[0:HALO, :] = vp_s[tm : tm + HALO, :]
        ycat = jnp.concatenate(parts, axis=1).astype(BF16)
        mix = jnp.dot(ycat, wout_ref[...], preferred_element_type=F32)
        mix_ref[...] = mix
        xhat, rstd = _ln_fwd(DEEPNORM_ALPHA * xv + (1.0 + g1) * mix)
        xhat_ref[...] = xhat
        rstd_ref[...] = rstd

        for j in range(3):

            @pl.when(i == fwd_steps[j])
            def _(j=j):
                _gather_forward(g_ins, g_outs, g_sems, j)

        @pl.when(i == nt - 1)
        def _():
            _gather_finish(g_ins, g_outs, g_sems)

    tile = lambda w: pl.BlockSpec((tm, w), lambda i: (i, 0))
    return pl.pallas_call(
        body,
        name="f1",
        grid=(nt,),
        out_shape=[
            jax.ShapeDtypeStruct((T, ZW), BF16),
            jax.ShapeDtypeStruct((T, D), BF16),
            jax.ShapeDtypeStruct((T, D), F32),
            jax.ShapeDtypeStruct((T, 1), F32),
            jax.ShapeDtypeStruct((T, D), F32),
        ]
        + _gather_out_shape(shards16, by_cols),
        in_specs=[tile(D)] + [VMEM_SPEC] * (6 + ng),
        out_specs=[tile(ZW), tile(D), tile(D), tile(1), tile(D)] + [ANY_SPEC] * ng,
        scratch_shapes=[pltpu.VMEM((HALO + tm, CC), F32), pltpu.VMEM((HALO + tm, CC), F32)]
        + [pltpu.VMEM(s.shape, BF16) for s in shards16]
        + _gather_scratch(ng),
        compiler_params=pltpu.CompilerParams(dimension_semantics=("arbitrary",), vmem_limit_bytes=VMEM_LIMIT),
    )(x, mod, w_in, conv_w, w_pool, pool_scale, w_out, *gather)


def _fb2(xhat1, target, mod, ln, w_mi, w_mo, tm):
    T, D = xhat1.shape
    H = w_mi.shape[1]
    hc = min(1024, H)
    nb = H // hc
    nt = T // tm

    def body(xh1_ref, t_ref, mod_ref, ln_ref, wmi_ref, wmo_ref, dx1_ref, h2_ref, a_ref, du_ref, df_ref, acc_ref):
        i = pl.program_id(0)

        @pl.when(i == 0)
        def _():
            acc_ref[...] = jnp.zeros((8, D), F32)

        sh2, sc2, g2 = mod_ref[3:4, :], mod_ref[4:5, :], mod_ref[5:6, :]
        x1 = xh1_ref[...] * ln_ref[0:1, :] + ln_ref[1:2, :]
        h2 = (x1 * (1.0 + sc2) + sh2).astype(BF16)
        h2_ref[...] = h2
        f = jnp.zeros((tm, D), F32)
        for k in range(nb):
            ks = slice(k * hc, (k + 1) * hc)
            r = jnp.maximum(jnp.dot(h2, wmi_ref[:, ks], preferred_element_type=F32), 0.0)
            du_ref[:, ks] = r.astype(BF16)
            a = (r * r).astype(BF16)
            a_ref[:, ks] = a
            f = f + jnp.dot(a, wmo_ref[ks, :], preferred_element_type=F32)
        xhat2, rstd2 = _ln_fwd(DEEPNORM_ALPHA * x1 + (1.0 + g2) * f)
        ln2_g = ln_ref[2:3, :]
        d = xhat2 * ln2_g + ln_ref[3:4, :] - t_ref[...]
        dr2 = _ln_bwd(d * (ln2_g * (1.0 / D)), xhat2, rstd2)
        df = ((1.0 + g2) * dr2).astype(BF16)
        df_ref[...] = df
        dh2 = jnp.zeros((tm, D), F32)
        for k in range(nb):
            ks = slice(k * hc, (k + 1) * hc)
            da = lax.dot_general(df, wmo_ref[ks, :], NT, preferred_element_type=F32)
            du = (da * (2.0 * du_ref[:, ks].astype(F32))).astype(BF16)
            du_ref[:, ks] = du
            dh2 = dh2 + lax.dot_general(du, wmi_ref[:, ks], NT, preferred_element_type=F32)
        dx1_ref[...] = DEEPNORM_ALPHA * dr2 + dh2 * (1.0 + sc2)
        acc_ref[0:1, :] += _colsum(d * xhat2) * (1.0 / D)
        acc_ref[1:2, :] += _colsum(d) * (1.0 / D)
        acc_ref[2:3, :] += _colsum(dh2)
        acc_ref[3:4, :] += _colsum(dh2 * x1)
        acc_ref[4:5, :] += _colsum(dr2 * f)
        acc_ref[5:6, :] += jnp.zeros((1, D), F32) + (0.5 / D) * jnp.sum(d * d)

    tile = lambda w: pl.BlockSpec((tm, w), lambda i: (i, 0))
    return pl.pallas_call(
        body,
        name="fb2",
        grid=(nt,),
        out_shape=[
            jax.ShapeDtypeStruct((T, D), F32),
            jax.ShapeDtypeStruct((T, D), BF16),
            jax.ShapeDtypeStruct((T, H), BF16),
            jax.ShapeDtypeStruct((T, H), BF16),
            jax.ShapeDtypeStruct((T, D), BF16),
            jax.ShapeDtypeStruct((8, D), F32),
        ],
        in_specs=[tile(D), tile(D)] + [VMEM_SPEC] * 4,
        out_specs=[tile(D), tile(D), tile(H), tile(H), tile(D), pl.BlockSpec((8, D), lambda i: (0, 0))],
        compiler_params=pltpu.CompilerParams(dimension_semantics=("arbitrary",), vmem_limit_bytes=VMEM_LIMIT),
    )(xhat1, target, mod, ln, w_mi, w_mo)


def _b1(dx1, xhat1, rstd1, x, mix, z, mod, ln, w_out, w_in, conv_w, w_pool, pool_scale, tm):
    T, D = x.shape
    ZW = w_in.shape[1]
    CC = ZW // 4
    nt = T // tm
    hb = tm // HALO

    def body(dx1_ref, xh1_ref, rstd_ref, x_ref, mix_ref, z_ref, zh_ref, mod_ref, ln_ref, wout_ref, win_ref, cw_ref, wp_ref, ps_ref,
             dx_ref, dmix_ref, ycat_ref, dz_ref, acc_ref, gcw_ref, gwp_ref, cv_s, vp_s, e_s, q_s):
        i = pl.program_id(0)
        j = nt - 1 - i

        @pl.when(i == 0)
        def _():
            acc_ref[...] = jnp.zeros((8, D), F32)
            gcw_ref[...] = jnp.zeros((8, CC), F32)
            gwp_ref[...] = jnp.zeros(gwp_ref.shape, F32)
            e_s[tm : tm + HALO, :] = jnp.zeros((HALO, CC), F32)
            q_s[tm : tm + HALO, :] = jnp.zeros((HALO, CC), F32)

        sh1, sc1, g1 = mod_ref[0:1, :], mod_ref[1:2, :], mod_ref[2:3, :]
        dx1 = dx1_ref[...]
        xhat1 = xh1_ref[...]
        acc_ref[0:1, :] += _colsum(dx1 * xhat1)
        acc_ref[1:2, :] += _colsum(dx1)
        dr1 = _ln_bwd(dx1 * ln_ref[0:1, :], xhat1, rstd_ref[...])
        acc_ref[4:5, :] += _colsum(dr1 * mix_ref[...])
        dmix = ((1.0 + g1) * dr1).astype(BF16)
        dmix_ref[...] = dmix
        dycat = lax.dot_general(dmix, wout_ref[...], NT, preferred_element_type=F32)

        z = z_ref[...].astype(F32)
        zh = zh_ref[...].astype(F32) * jnp.where(j > 0, 1.0, 0.0)
        gb, gc, vc, vp = z[:, 0:CC], z[:, CC : 2 * CC], z[:, 2 * CC : 3 * CC], z[:, 3 * CC : 4 * CC]
        cv = gc * vc
        cv_s[0:HALO, :] = zh[:, CC : 2 * CC] * zh[:, 2 * CC : 3 * CC]
        cv_s[HALO : HALO + tm, :] = cv
        vp_s[0:HALO, :] = zh[:, 3 * CC : 4 * CC]
        vp_s[HALO : HALO + tm, :] = vp
        cv_ext = cv_s[...]
        cv_m2 = pltpu.roll(cv_ext, 2, 0)[HALO : HALO + tm, :]
        cv_m1 = pltpu.roll(cv_ext, 1, 0)[HALO : HALO + tm, :]
        w0, w1, w2 = cw_ref[0:1, :], cw_ref[1:2, :], cw_ref[2:3, :]
        conv = w0 * cv_m2 + w1 * cv_m1 + w2 * cv
        dyc = dycat[:, 0:CC]
        e = dyc * gb
        e_s[0:tm, :] = e
        e_ext = e_s[...]
        dcv = w2 * e + w1 * pltpu.roll(e_ext, tm + HALO - 1, 0)[0:tm, :] + w0 * pltpu.roll(e_ext, tm + HALO - 2, 0)[0:tm, :]
        gcw_ref[0:1, :] += _colsum(e * cv_m2)
        gcw_ref[1:2, :] += _colsum(e * cv_m1)
        gcw_ref[2:3, :] += _colsum(e * cv)
        y_parts = [gb * conv]
        dz_parts = [dyc * conv, dcv * vc, dcv * gc]

        row = j * tm + lax.broadcasted_iota(jnp.int32, (tm, 1), 0)
        feats, inv_cnts = _pool_features(vp, vp_s, row, tm)
        gps_parts, dps = [], []
        for g in range(len(POOL_WINDOWS)):
            cols = slice(128 * g, 128 * g + 128)
            p = feats[g].astype(BF16)
            scale = ps_ref[0:1, cols]
            wp = wp_ref[g].astype(BF16)
            pw = jnp.dot(p, wp, preferred_element_type=F32)
            y_parts.append(pw * scale)
            dyp = dycat[:, CC + 128 * g : CC + 128 * g + 128]
            gps_parts.append(_colsum(dyp * pw))
            dpw = (dyp * scale).astype(BF16)
            gwp_ref[g] += lax.dot_general(p, dpw, TN, preferred_element_type=F32)
            dp = lax.dot_general(dpw, wp, NT, preferred_element_type=F32)
            q_s[0:tm, cols] = dp * inv_cnts[g]
            dps.append(dp)
        sq = _window_sums(q_s[...], tm, causal=False)
        dz_parts += [sq[g] - dps[g] for g in range(len(POOL_WINDOWS))]
        gcw_ref[3:4, :] += jnp.concatenate(gps_parts, axis=1)
        ycat_ref[...] = jnp.concatenate(y_parts, axis=1).astype(BF16)
        dz = jnp.concatenate(dz_parts, axis=1).astype(BF16)
        dz_ref[...] = dz
        dh = lax.dot_general(dz, win_ref[...], NT, preferred_element_type=F32)
        acc_ref[2:3, :] += _colsum(dh)
        acc_ref[3:4, :] += _colsum(dh * x_ref[...])
        dx_ref[...] = DEEPNORM_ALPHA * dr1 + dh * (1.0 + sc1)
        e_s[tm : tm + HALO, :] = e_s[0:HALO, :]
        q_s[tm : tm + HALO, :] = q_s[0:HALO, :]

    tile = lambda w: pl.BlockSpec((tm, w), lambda i: (nt - 1 - i, 0))
    halo = pl.BlockSpec((HALO, ZW), lambda i: (jnp.maximum((nt - 1 - i) * hb - 1, 0), 0))
    fixed = lambda shape: pl.BlockSpec(shape, lambda i: (0,) * len(shape))
    return pl.pallas_call(
        body,
        name="b1",
        grid=(nt,),
        out_shape=[
            jax.ShapeDtypeStruct((T, D), F32),
            jax.ShapeDtypeStruct((T, D), BF16),
            jax.ShapeDtypeStruct((T, D), BF16),
            jax.ShapeDtypeStruct((T, ZW), BF16),
            jax.ShapeDtypeStruct((8, D), F32),
            jax.ShapeDtypeStruct((8, CC), F32),
            jax.ShapeDtypeStruct(w_pool.shape, F32),
        ],
        in_specs=[tile(D), tile(D), tile(1), tile(D), tile(D), tile(ZW), halo] + [VMEM_SPEC] * 7,
        out_specs=[tile(D), tile(D), tile(D), tile(ZW), fixed((8, D)), fixed((8, CC)), fixed(w_pool.shape)],
        scratch_shapes=[
            pltpu.VMEM((HALO + tm, CC), F32),
            pltpu.VMEM((HALO + tm, CC), F32),
            pltpu.VMEM((tm + HALO, CC), F32),
            pltpu.VMEM((tm + HALO, CC), F32),
        ],
        compiler_params=pltpu.CompilerParams(dimension_semantics=("arbitrary",), vmem_limit_bytes=VMEM_LIMIT),
    )(dx1, xhat1, rstd1, x, mix, z, z, mod, ln, w_out, w_in, conv_w, w_pool, pool_scale)


def _wgrad(a, b, bk, n_groups, bt, name, owners=None, scatter=None, gather=()):
    T, K = a.shape
    N = b.shape[1]
    nk, nt, ng = K // bk, T // bt, N // n_groups
    nc = min(512, ng)
    ns, ngat = (0 if scatter is None else 1), len(gather)
    n_steps = nk * n_groups * nt
    mid_step = min(1, n_steps - 1)
    fwd_steps = [min(2 * (j + 1), n_steps - 1) for j in range(3)]

    def body(*refs):
        a_ref, b_ref = refs[0], refs[1]
        g_ins = refs[2 + ns : 2 + ns + ngat]
        outs = refs[2 + ns + ngat :]
        o_ref, g_outs = outs[0], outs[1 + ns : 1 + ns + ngat]
        scr = outs[1 + ns + ngat :]
        acc = scr[0]
        if ns:
            s_hbm, s_recv, s_scr = refs[2], outs[1], scr[1 : 1 + N_SCATTER_SCRATCH]
        g_sems = scr[1 + N_SCATTER_SCRATCH * ns :]
        kk, gg, t = pl.program_id(0), pl.program_id(1), pl.program_id(2)
        step = (kk * n_groups + gg) * nt + t

        if ngat:

            @pl.when(step == 0)
            def _():
                _gather_start(g_ins, g_outs, g_sems)

            for j in range(3):

                @pl.when(step == fwd_steps[j])
                def _(j=j):
                    _gather_forward(g_ins, g_outs, g_sems, j)

        if ns:

            @pl.when(step == 0)
            def _():
                _scatter_start(s_hbm, s_scr)

            @pl.when(step == mid_step)
            def _():
                _scatter_middle(s_hbm, s_recv, s_scr)

        @pl.when(t == 0)
        def _():
            acc[...] = jnp.zeros(acc.shape, F32)

        at = a_ref[...].T
        for c in range(ng // nc):
            cs = slice(c * nc, (c + 1) * nc)
            acc[:, cs] += jnp.dot(at, b_ref[:, cs], preferred_element_type=F32)

        @pl.when(t == nt - 1)
        def _():
            if owners is None:
                o_ref[...] = acc[...].astype(BF16)
            else:
                per = N // owners
                for o in range(ng // per):
                    o_ref[o] = acc[:, o * per : (o + 1) * per].astype(BF16)

        if ns:

            @pl.when(step == n_steps - 1)
            def _():
                _scatter_finish(s_hbm, s_recv, s_scr)

        if ngat:

            @pl.when(step == n_steps - 1)
            def _():
                _gather_finish(g_ins, g_outs, g_sems)

    if owners is None:
        out_shape = [jax.ShapeDtypeStruct((K, N), BF16)]
        out_specs = [pl.BlockSpec((bk, ng), lambda k, g, t: (k, g))]
    else:
        assert bk == K
        per = N // owners
        out_shape = [jax.ShapeDtypeStruct((owners, K, per), BF16)]
        out_specs = [pl.BlockSpec((ng // per, K, per), lambda k, g, t: (g, 0, 0))]
    ins, in_specs = [a, b], [pl.BlockSpec((bt, bk), lambda k, g, t: (t, k)), pl.BlockSpec((bt, ng), lambda k, g, t: (t, g))]
    scratch = [pltpu.VMEM((bk, ng), F32)]
    if ns:
        ins.append(scatter)
        in_specs.append(ANY_SPEC)
        out_shape.append(_scatter_out_shape(scatter))
        out_specs.append(ANY_SPEC)
        scratch += _scatter_scratch(*scatter.shape[1:])
    if ngat:
        ins += list(gather)
        in_specs += [ANY_SPEC] * ngat
        out_shape += _gather_out_shape(gather, [False] * ngat)
        out_specs += [ANY_SPEC] * ngat
        scratch += _gather_scratch(ngat)
    outs = pl.pallas_call(
        body,
        name=name,
        grid=(nk, n_groups, nt),
        out_shape=out_shape,
        in_specs=in_specs,
        out_specs=out_specs,
        scratch_shapes=scratch,
        compiler_params=pltpu.CompilerParams(dimension_semantics=("arbitrary", "arbitrary", "arbitrary"), vmem_limit_bytes=VMEM_LIMIT),
    )(*ins)
    return outs if ns + ngat else outs[0]


def _small_grads(acc1_all, acc2_all, gcw_all, gwp_all, cond_t, my_slot, w_ada, m_w_ada, v_w_ada):
    D = acc1_all.shape[2]
    w_cols = w_ada.shape[2]
    n_chunk = D // 128
    q_mine = w_cols // 128

    def total(ref, r):
        s = ref[0, r : r + 1, :]
        for k in range(1, N_DEV):
            s = s + ref[k, r : r + 1, :]
        return s

    def body(slot_ref, a1_ref, a2_ref, gcw_ref, gwp_ref, ct_ref, w_ref, m_ref, v_ref,
             gb_ref, gw_ref, gln_ref, gcwo_ref, gwpo_ref, loss_ref, dw_ref, nm_ref, nv_ref, dm_s):
        loss_ref[...] = total(a2_ref, 5)
        for s, (ref, r) in enumerate([(a1_ref, 2), (a1_ref, 3), (a1_ref, 4), (a2_ref, 2), (a2_ref, 3), (a2_ref, 4)]):
            gb_ref[0:1, s * D : (s + 1) * D] = total(ref, r)
            for k in range(N_DEV):
                row = ref[k, r : r + 1, :]
                for qq in range(n_chunk):
                    dm_s[s * n_chunk + qq, k : k + 1, :] = row[:, 128 * qq : 128 * qq + 128]
        gln_ref[0:1, :] = total(a1_ref, 0)
        gln_ref[1:2, :] = total(a1_ref, 1)
        gln_ref[2:3, :] = total(a2_ref, 0)
        gln_ref[3:4, :] = total(a2_ref, 1)
        gcwo_ref[...] = jnp.zeros(gcwo_ref.shape, F32)
        for r in range(4):
            gcwo_ref[r : r + 1, :] = total(gcw_ref, r)
        wp = gwp_ref[0]
        for k in range(1, N_DEV):
            wp = wp + gwp_ref[k]
        gwpo_ref[0] = wp
        ct = ct_ref[...]
        cond_t = ct * jax.nn.sigmoid(ct)
        q0 = slot_ref[0] * q_mine
        for q in range(q_mine):
            dm = dm_s[q0 + q]
            out = cond_t[:, 0:1] * dm[0:1, :]
            for k in range(1, N_DEV):
                out = out + cond_t[:, k : k + 1] * dm[k : k + 1, :]
            cols = slice(128 * q, 128 * q + 128)
            gw_ref[0, :, cols] = out
            delta, nm, nv = _adamw_math(w_ref[0, :, cols], out, m_ref[0, :, cols], v_ref[0, :, cols])
            dw_ref[0, :, cols] = delta
            nm_ref[0, :, cols] = nm
            nv_ref[0, :, cols] = nv

    CC = gcw_all.shape[2]
    w_like = jax.ShapeDtypeStruct(w_ada.shape, F32)
    return pl.pallas_call(
        body,
        name="small_grads",
        out_shape=[
            jax.ShapeDtypeStruct((1, 6 * D), F32),
            w_like,
            jax.ShapeDtypeStruct((4, D), F32),
            jax.ShapeDtypeStruct((8, CC), F32),
            jax.ShapeDtypeStruct((1, *gwp_all.shape[1:]), F32),
            jax.ShapeDtypeStruct((1, D), F32),
            w_like,
            w_like,
            w_like,
        ],
        in_specs=[pl.BlockSpec(memory_space=pltpu.SMEM)] + [VMEM_SPEC] * 8,
        out_specs=[VMEM_SPEC] * 9,
        scratch_shapes=[pltpu.VMEM((6 * n_chunk, N_DEV, 128), F32)],
        compiler_params=pltpu.CompilerParams(vmem_limit_bytes=VMEM_LIMIT),
    )(my_slot, acc1_all, acc2_all, gcw_all, gwp_all, cond_t, w_ada, m_w_ada, v_w_ada)


def kernel(x, c, w_ada, b_ada, w_in, conv_w, w_pool, pool_scale, w_out, ln1_g, ln1_b, w_mlp_in, w_mlp_out, ln2_g, ln2_b, loss_target, m_w_ada, m_b_ada, m_w_in, m_conv_w, m_w_pool, m_pool_scale, m_w_out, m_ln1_g, m_ln1_b, m_w_mlp_in, m_w_mlp_out, m_ln2_g, m_ln2_b, v_w_ada, v_b_ada, v_w_in, v_conv_w, v_w_pool, v_pool_scale, v_w_out, v_ln1_g, v_ln1_b, v_w_mlp_in, v_w_mlp_out, v_ln2_g, v_ln2_b):
    T, D = x.shape[1], x.shape[2]
    H = w_mlp_out.shape[1] * N_DEV
    ZW = w_in.shape[2] * N_DEV
    CC = ZW // 4
    tm = min(512, T // 2)
    bt = min(1024, T)
    ax, ay, ac = _my_place()
    me = _slot(ax, ay, ac)

    w_cols = w_ada.shape[2]
    b_mine = lax.dynamic_slice(b_ada, (0, me * w_cols), (1, w_cols))
    w_in_f, w_out_g, cw_g, c_g, mod_g = _prologue(w_in, w_out, conv_w[0], c, w_ada, b_mine)
    w_out_f = w_out_g.reshape(D, D)
    conv_w_f = jnp.transpose(cw_g, (1, 0, 2)).reshape(conv_w.shape[1], CC)
    c_all = c_g.reshape(N_DEV, D)
    mod = lax.dynamic_index_in_dim(mod_g, me, axis=1, keepdims=False).reshape(6, D)

    ln = jnp.concatenate([ln1_g, ln1_b, ln2_g, ln2_b], axis=0)
    xs, target = x[0], loss_target[0]

    z, h, xhat1, rstd1, mix, w_mi_f, w_mo_g = _f1(
        xs, mod, w_in_f, conv_w_f, w_pool[0], pool_scale, w_out_f, tm,
        [w_mlp_in, w_mlp_out], [True, False])
    dx1, h2, a, du, df, acc2 = _fb2(xhat1, target, mod, ln, w_mi_f, w_mo_g.reshape(H, D), tm)

    grad_x, dmix, ycat, dz, acc1, gcw, gwp = _b1(
        dx1, xhat1, rstd1, xs, mix, z, mod, ln, w_out_f, w_in_f, conv_w_f, w_pool[0], pool_scale, tm)

    gp_mo = _wgrad(a, df, D, 1, min(2 * bt, T), "wgrad_mlp_out").reshape(N_DEV, H // N_DEV, D)
    gp_mi, rv_mo = _wgrad(h2, du, D, 2, min(2 * bt, T), "wgrad_mlp_in", owners=N_DEV, scatter=gp_mo)
    gp_in, rv_mi = _wgrad(h, dz, D, 1, bt, "wgrad_in", owners=N_DEV, scatter=gp_mi)
    gp_out, rv_in, acc1_g, acc2_g, gcw_g, gwp_g = _wgrad(
        ycat, dmix, D, 1, bt, "wgrad_out", scatter=gp_in, gather=[acc1, acc2, gcw, gwp])
    rv_out = _reduce_scatter(gp_out.reshape(N_DEV, D // N_DEV, D), "scatter_w_out")
    g_b_ada, g_w_ada, g_ln, g_cw, g_w_pool, loss_row, *u_w_ada = _small_grads(
        acc1_g, acc2_g, gcw_g, gwp_g, c_all.T, jnp.reshape(me, (1,)).astype(jnp.int32), w_ada, m_w_ada, v_w_ada)
    cc_mine = conv_w.shape[2]
    g_conv_w = lax.dynamic_slice(g_cw, (0, me * cc_mine), (conv_w.shape[1], cc_mine))[None]
    g_pool_scale = g_cw[3:4, :]
    g_ln1_g, g_ln1_b, g_ln2_g, g_ln2_b = g_ln[0:1], g_ln[1:2], g_ln[2:3], g_ln[3:4]

    small = _adamw_multi(
        [
            (b_ada, g_b_ada, m_b_ada, v_b_ada),
            (conv_w, g_conv_w, m_conv_w, v_conv_w),
            (w_pool, g_w_pool, m_w_pool, v_w_pool),
            (pool_scale, g_pool_scale, m_pool_scale, v_pool_scale),
            (ln1_g, g_ln1_g, m_ln1_g, v_ln1_g),
            (ln1_b, g_ln1_b, m_ln1_b, v_ln1_b),
            (ln2_g, g_ln2_g, m_ln2_g, v_ln2_g),
            (ln2_b, g_ln2_b, m_ln2_b, v_ln2_b),
        ],
        "adamw_small")
    u_b_ada, u_conv_w, u_w_pool, u_pool_scale, u_ln1_g, u_ln1_b, u_ln2_g, u_ln2_b = small

    g_w_mo, *u_w_mo = _sum_adamw(rv_mo, w_mlp_out, m_w_mlp_out, v_w_mlp_out, "sum_w_mlp_out")
    g_w_mi, *u_w_mi = _sum_adamw(rv_mi, w_mlp_in, m_w_mlp_in, v_w_mlp_in, "sum_w_mlp_in")
    g_w_in, *u_w_in = _sum_adamw(rv_in, w_in, m_w_in, v_w_in, "sum_w_in")
    g_w_out, *u_w_out = _sum_adamw(rv_out, w_out, m_w_out, v_w_out, "sum_w_out")

    grads = [g_w_ada, g_b_ada, g_w_in, g_conv_w, g_w_pool, g_pool_scale, g_w_out, g_ln1_g, g_ln1_b, g_w_mi, g_w_mo, g_ln2_g, g_ln2_b]
    updates = [u_w_ada, u_b_ada, u_w_in, u_conv_w, u_w_pool, u_pool_scale, u_w_out, u_ln1_g, u_ln1_b, u_w_mi, u_w_mo, u_ln2_g, u_ln2_b]
    deltas = [u[0] for u in updates]
    new_m = [u[1] for u in updates]
    new_v = [u[2] for u in updates]
    return (loss_row[0, 0], grad_x[None], *grads, *deltas, *new_m, *new_v)
```

```python
import jax
import jax.numpy as jnp
from jax import lax
from jax.experimental import pallas as pl
from jax.experimental.pallas import tpu as pltpu

F32 = jnp.float32
BF16 = jnp.bfloat16
MESH = pl.DeviceIdType.MESH
N_DEV = 8

LN_EPS = 1e-5
DEPTH = 1
DEEPNORM_ALPHA = (2.0 * DEPTH) ** 0.25
POOL_WINDOWS = (2, 4, 8, 16)
HALO = 16

ADAM_LR = 0.001
ADAM_B1 = 0.9
ADAM_B2 = 0.999
ADAM_EPS = 1e-08
ADAM_WD = 0.01
ADAM_STEP = 10

VMEM_LIMIT = 60 * 1024 * 1024

VMEM_SPEC = pl.BlockSpec(memory_space=pltpu.VMEM)
ANY_SPEC = pl.BlockSpec(memory_space=pl.ANY)

NT = (((1,), (1,)), ((), ()))
TN = (((0,), (0,)), ((), ()))


def _my_place():
    return lax.axis_index("x"), lax.axis_index("y"), lax.axis_index("c")


def _slot(x, y, c):
    return 4 * x + 2 * y + c


def _gather_place(ins, outs, a, slot):
    if len(outs[a].shape) == len(ins[a].shape):
        wb = ins[a].shape[1]
        return outs[a].at[:, pl.ds(pl.multiple_of(slot * wb, wb), wb)]
    return outs[a].at[slot]


def _gather_copy(ins, outs, sems, a, k, block, to, from_shard=False):
    send_sems, recv_sems, _ = sems
    dst = _gather_place(ins, outs, a, _slot(*block))
    return pltpu.make_async_remote_copy(
        src_ref=ins[a] if from_shard else dst,
        dst_ref=dst,
        send_sem=send_sems.at[7 * a + k],
        recv_sem=recv_sems.at[7 * a + k],
        device_id=to,
        device_id_type=MESH,
    )


def _gather_peers():
    x, y, c = _my_place()
    return (x, y, c), (x, y, 1 - c), [(1 - x, y), (x, 1 - y), (1 - x, 1 - y)]


def _gather_first(ins, outs, sems):
    me, sibling, chips = _gather_peers()
    first = []
    for a in range(len(ins)):
        first.append(_gather_copy(ins, outs, sems, a, 0, me, sibling, from_shard=True))
        first += [_gather_copy(ins, outs, sems, a, 1 + j, me, (*chip, me[2]), from_shard=True) for j, chip in enumerate(chips)]
    return first


def _gather_mine(ins, outs, sems, a):
    me, _, _ = _gather_peers()
    return pltpu.make_async_copy(ins[a], _gather_place(ins, outs, a, _slot(*me)), sems[2].at[a])


def _gather_start(ins, outs, sems):
    for a in range(len(ins)):
        _gather_mine(ins, outs, sems, a).start()
    for cp in _gather_first(ins, outs, sems):
        cp.start()


def _gather_forward(ins, outs, sems, j):
    me, sibling, chips = _gather_peers()
    for a in range(len(ins)):
        _gather_copy(ins, outs, sems, a, 1 + j, (*chips[j], me[2]), me).wait_recv()
        _gather_copy(ins, outs, sems, a, 4 + j, (*chips[j], me[2]), sibling).start()


def _gather_finish(ins, outs, sems):
    me, sibling, chips = _gather_peers()
    for a in range(len(ins)):
        _gather_copy(ins, outs, sems, a, 0, sibling, me).wait_recv()
        for j, chip in enumerate(chips):
            _gather_copy(ins, outs, sems, a, 4 + j, (*chip, 1 - me[2]), me).wait_recv()
    for cp in _gather_first(ins, outs, sems):
        cp.wait_send()
    for a in range(len(ins)):
        for j, chip in enumerate(chips):
            _gather_copy(ins, outs, sems, a, 4 + j, (*chip, me[2]), sibling).wait_send()
        _gather_mine(ins, outs, sems, a).wait()


def _gather_scratch(n):
    return [pltpu.SemaphoreType.DMA((7 * n,)), pltpu.SemaphoreType.DMA((7 * n,)), pltpu.SemaphoreType.DMA((n,))]


def _gather_out_shape(shards, by_cols):
    return [
        jax.ShapeDtypeStruct((s.shape[0], N_DEV * s.shape[1]) if cols else (N_DEV, *s.shape), s.dtype)
        for s, cols in zip(shards, by_cols)
    ]


N_CHIP = 4


def _scatter_scratch(rows, cols):
    block = pltpu.VMEM((N_CHIP, rows, cols), BF16)
    dma = pltpu.SemaphoreType.DMA
    return [block, block, block, dma((N_CHIP,)), dma((N_CHIP,)), dma((N_CHIP,)), dma((N_CHIP - 1,)), dma((N_CHIP - 1,)), dma]


def _scatter_pair_copies(g_hbm, scr):
    x, y, c = _my_place()
    mine, theirs, _, a_send, a_recv, load_sem = scr[:6]
    to_sibling = [
        pltpu.make_async_remote_copy(
            src_ref=g_hbm.at[2 * q + (1 - c)], dst_ref=theirs.at[q], send_sem=a_send.at[q], recv_sem=a_recv.at[q],
            device_id=(x, y, 1 - c), device_id_type=MESH)
        for q in range(N_CHIP)
    ]
    loads = [pltpu.make_async_copy(g_hbm.at[2 * q + c], mine.at[q], load_sem.at[q]) for q in range(N_CHIP)]
    return to_sibling, loads


def _scatter_sum_copies(recv, scr):
    x, y, c = _my_place()
    sums, b_send, b_recv, own_sem = scr[2], scr[6], scr[7], scr[8]
    q_me = 2 * x + y
    to_owner = [
        pltpu.make_async_remote_copy(
            src_ref=sums.at[2 * px + py], dst_ref=recv.at[q_me], send_sem=b_send.at[j], recv_sem=b_recv.at[j],
            device_id=(px, py, c), device_id_type=MESH)
        for j, (px, py) in enumerate([(1 - x, y), (x, 1 - y), (1 - x, 1 - y)])
    ]
    return to_owner, pltpu.make_async_copy(sums.at[q_me], recv.at[q_me], own_sem)


def _scatter_start(g_hbm, scr):
    to_sibling, loads = _scatter_pair_copies(g_hbm, scr)
    for cp in to_sibling + loads:
        cp.start()


def _scatter_middle(g_hbm, recv, scr):
    to_sibling, loads = _scatter_pair_copies(g_hbm, scr)
    for cp in to_sibling:
        cp.wait_recv()
    for cp in loads:
        cp.wait()
    mine, theirs, sums = scr[:3]

    def step(r, carry):
        rs = pl.ds(pl.multiple_of(r * ROW_CHUNK, ROW_CHUNK), ROW_CHUNK)
        for q in range(N_CHIP):
            sums[q, rs, :] = (mine[q, rs, :].astype(F32) + theirs[q, rs, :].astype(F32)).astype(BF16)
        return carry

    lax.fori_loop(0, mine.shape[1] // ROW_CHUNK, step, 0)
    to_owner, own = _scatter_sum_copies(recv, scr)
    for cp in to_owner + [own]:
        cp.start()


def _scatter_finish(g_hbm, recv, scr):
    to_sibling, _ = _scatter_pair_copies(g_hbm, scr)
    to_owner, own = _scatter_sum_copies(recv, scr)
    for cp in to_owner:
        cp.wait_recv()
    for cp in to_sibling + to_owner:
        cp.wait_send()
    own.wait()


def _scatter_out_shape(gparts):
    return jax.ShapeDtypeStruct((N_CHIP, *gparts.shape[1:]), gparts.dtype)


def _adamw_math(w, g, m, v):
    m = ADAM_B1 * m + (1.0 - ADAM_B1) * g
    v = ADAM_B2 * v + (1.0 - ADAM_B2) * (g * g)
    m_hat = m / (1.0 - ADAM_B1**ADAM_STEP)
    v_hat = v / (1.0 - ADAM_B2**ADAM_STEP)
    delta = -ADAM_LR * (m_hat / (jnp.sqrt(v_hat) + ADAM_EPS) + ADAM_WD * w)
    return delta, m, v


ROW_CHUNK = 64


ELEMS_PER_STEP = 128 * 1024


N_SCATTER_SCRATCH = 9


def _reduce_scatter(gparts, name):
    def body(g_hbm, recv, *scr):
        _scatter_start(g_hbm, scr)
        _scatter_middle(g_hbm, recv, scr)
        _scatter_finish(g_hbm, recv, scr)

    return pl.pallas_call(
        body,
        name=name,
        out_shape=_scatter_out_shape(gparts),
        in_specs=[ANY_SPEC],
        out_specs=ANY_SPEC,
        scratch_shapes=_scatter_scratch(*gparts.shape[1:]),
        compiler_params=pltpu.CompilerParams(vmem_limit_bytes=VMEM_LIMIT),
    )(gparts)


def _sum_adamw(parts, w, m, v, name):
    _, rows, cols = w.shape
    rb = rows
    while rb * cols > ELEMS_PER_STEP and rb % 16 == 0:
        rb //= 2

    def body(p_ref, w_ref, m_ref, v_ref, grad_ref, delta_ref, nm_ref, nv_ref):
        g = p_ref[0].astype(F32)
        for k in range(1, p_ref.shape[0]):
            g = g + p_ref[k].astype(F32)
        delta, nm, nv = _adamw_math(w_ref[0], g, m_ref[0], v_ref[0])
        grad_ref[0] = g
        delta_ref[0] = delta
        nm_ref[0] = nm
        nv_ref[0] = nv

    block = lambda lead: pl.BlockSpec((lead, rb, cols), lambda i: (0, i, 0))
    out = jax.ShapeDtypeStruct(w.shape, F32)
    return pl.pallas_call(
        body,
        name=name,
        grid=(rows // rb,),
        out_shape=[out] * 4,
        in_specs=[block(parts.shape[0])] + [block(1)] * 3,
        out_specs=[block(1)] * 4,
        compiler_params=pltpu.CompilerParams(dimension_semantics=("arbitrary",), vmem_limit_bytes=VMEM_LIMIT),
    )(parts, w, m, v)


def _adamw_multi(items, name):
    n = len(items)

    def body(*refs):
        ins, outs = refs[: 4 * n], refs[4 * n :]
        for a in range(n):
            w_ref, g_ref, m_ref, v_ref = ins[4 * a : 4 * a + 4]
            d_ref, nm_ref, nv_ref = outs[3 * a : 3 * a + 3]
            delta, nm, nv = _adamw_math(w_ref[...], g_ref[...], m_ref[...], v_ref[...])
            d_ref[...] = delta
            nm_ref[...] = nm
            nv_ref[...] = nv

    flat = [a for it in items for a in it]
    out_shape = [jax.ShapeDtypeStruct(it[0].shape, F32) for it in items for _ in range(3)]
    outs = pl.pallas_call(
        body,
        name=name,
        out_shape=out_shape,
        in_specs=[VMEM_SPEC] * (4 * n),
        out_specs=[VMEM_SPEC] * (3 * n),
        compiler_params=pltpu.CompilerParams(vmem_limit_bytes=VMEM_LIMIT),
    )(*flat)
    return [tuple(outs[3 * a : 3 * a + 3]) for a in range(n)]


def _prologue(w_in, w_out, conv_w, c, w_ada, b_mine):
    D = c.shape[1]
    wc = w_ada.shape[2]
    shards16 = [jax.ShapeDtypeStruct(w_in.shape[1:], BF16), jax.ShapeDtypeStruct(w_out.shape[1:], BF16)]

    def to_all(src, out, sems):
        x, y, c = _my_place()
        me = _slot(x, y, c)
        copies = [
            pltpu.make_async_remote_copy(
                src_ref=src, dst_ref=out.at[me], send_sem=sems[0].at[k - 1], recv_sem=sems[1].at[k - 1],
                device_id=(x ^ (k >> 2), y ^ ((k >> 1) & 1), c ^ (k & 1)), device_id_type=MESH)
            for k in range(1, N_DEV)
        ]
        return copies, pltpu.make_async_copy(src, out.at[me], sems[2].at[0])

    def start(copies, own):
        for cp in copies + [own]:
            cp.start()

    def finish(copies, own):
        for cp in copies:
            cp.wait()
        own.wait()

    def body(win_ref, wout_ref, cw_ref, c_ref, wada_ref, b_ref, win_g, wout_g, cw_g, c_g, mod_g, win16, wout16, c_s, mp_s, *sems):
        c_copies = to_all(c_ref, c_g, sems[3:6])
        start(*c_copies)
        win16[...] = win_ref[0].astype(BF16)
        wout16[...] = wout_ref[0].astype(BF16)
        w_ins, w_outs, w_sems = (win16, wout16, cw_ref), (win_g, wout_g, cw_g), sems[0:3]
        _gather_start(w_ins, w_outs, w_sems)
        finish(*c_copies)
        for k in range(N_DEV):
            c_s[k : k + 1, :] = c_g[k]
        cv = c_s[...]
        cond = cv * jax.nn.sigmoid(cv)
        mp_s[...] = jnp.dot(cond, wada_ref[0], precision=lax.Precision.HIGHEST, preferred_element_type=F32) + b_ref[...]
        m_copies = to_all(mp_s, mod_g, sems[6:9])
        start(*m_copies)
        for j in range(3):
            _gather_forward(w_ins, w_outs, w_sems, j)
        _gather_finish(w_ins, w_outs, w_sems)
        finish(*m_copies)

    return pl.pallas_call(
        body,
        name="prologue",
        out_shape=_gather_out_shape(shards16 + [conv_w], [True, False, False])
        + [jax.ShapeDtypeStruct((N_DEV, 1, D), F32), jax.ShapeDtypeStruct((N_DEV, N_DEV, wc), F32)],
        in_specs=[VMEM_SPEC, VMEM_SPEC, ANY_SPEC] + [VMEM_SPEC] * 3,
        out_specs=[ANY_SPEC] * 3 + [VMEM_SPEC] * 2,
        scratch_shapes=[pltpu.VMEM(s.shape, BF16) for s in shards16]
        + [pltpu.VMEM((N_DEV, D), F32), pltpu.VMEM((N_DEV, wc), F32)]
        + _gather_scratch(3) + _gather_scratch(1) + _gather_scratch(1),
        compiler_params=pltpu.CompilerParams(vmem_limit_bytes=VMEM_LIMIT),
    )(w_in, w_out, conv_w, c, w_ada, b_mine)


def _ln_fwd(r):
    mu = jnp.mean(r, axis=-1, keepdims=True)
    d = r - mu
    var = jnp.mean(d * d, axis=-1, keepdims=True)
    rstd = lax.rsqrt(var + LN_EPS)
    return d * rstd, rstd


def _ln_bwd(dxh, xhat, rstd):
    m1 = jnp.mean(dxh, axis=-1, keepdims=True)
    m2 = jnp.mean(dxh * xhat, axis=-1, keepdims=True)
    return rstd * (dxh - m1 - xhat * m2)


def _colsum(a):
    return jnp.sum(a, axis=0, keepdims=True)


def _window_sums(ext, tm, causal):
    n = ext.shape[0]
    lo = HALO if causal else 0
    s, out = ext, []
    for p in range(len(POOL_WINDOWS)):
        assert POOL_WINDOWS[p] == 2 ** (p + 1)
        k = 2**p
        s = s + pltpu.roll(s, k if causal else n - k, 0)
        out.append(s[lo : lo + tm, 0:128])
        if p + 1 < len(POOL_WINDOWS):
            s = s[:, 128:]
    return out


def _pool_features(vp, vp_s, row, tm):
    sums = _window_sums(vp_s[...], tm, causal=True)
    feats, inv_cnts = [], []
    for g, win in enumerate(POOL_WINDOWS):
        inv_cnt = 1.0 / jnp.minimum(row + 1, win).astype(F32)
        feats.append(sums[g] * inv_cnt - vp[:, 128 * g : 128 * g + 128])
        inv_cnts.append(inv_cnt)
    return feats, inv_cnts


def _f1(x, mod, w_in, conv_w, w_pool, pool_scale, w_out, tm, gather, by_cols):
    T, D = x.shape
    ZW = w_in.shape[1]
    CC = ZW // 4
    nt = T // tm
    ng = len(gather)
    fwd_steps = [max(nt - 3 + j, 0) for j in range(3)]

    shards16 = [jax.ShapeDtypeStruct(s.shape[1:], BF16) for s in gather]

    def body(*refs):
        x_ref, mod_ref, win_ref, cw_ref, wp_ref, ps_ref, wout_ref = refs[:7]
        g_f32 = refs[7 : 7 + ng]
        z_ref, h_ref, xhat_ref, rstd_ref, mix_ref = refs[7 + ng : 12 + ng]
        g_outs = refs[12 + ng : 12 + 2 * ng]
        cv_s, vp_s = refs[12 + 2 * ng : 14 + 2 * ng]
        g_ins = refs[14 + 2 * ng : 14 + 3 * ng]
        g_sems = refs[14 + 3 * ng :]
        i = pl.program_id(0)

        @pl.when(i == 0)
        def _():
            for src, dst in zip(g_f32, g_ins):
                dst[...] = src[0].astype(BF16)
            _gather_start(g_ins, g_outs, g_sems)
            cv_s[0:HALO, :] = jnp.zeros((HALO, CC), F32)
            vp_s[0:HALO, :] = jnp.zeros((HALO, CC), F32)

        xv = x_ref[...]
        sh1, sc1, g1 = mod_ref[0:1, :], mod_ref[1:2, :], mod_ref[2:3, :]
        h = (xv * (1.0 + sc1) + sh1).astype(BF16)
        h_ref[...] = h
        z = jnp.dot(h, win_ref[...], preferred_element_type=F32)
        z_ref[...] = z.astype(BF16)
        gb, gc, vc, vp = z[:, 0:CC], z[:, CC : 2 * CC], z[:, 2 * CC : 3 * CC], z[:, 3 * CC : 4 * CC]
        cv = gc * vc
        cv_s[HALO : HALO + tm, :] = cv
        vp_s[HALO : HALO + tm, :] = vp
        cv_ext = cv_s[...]
        cv_m2 = pltpu.roll(cv_ext, 2, 0)[HALO : HALO + tm, :]
        cv_m1 = pltpu.roll(cv_ext, 1, 0)[HALO : HALO + tm, :]
        conv = cw_ref[0:1, :] * cv_m2 + cw_ref[1:2, :] * cv_m1 + cw_ref[2:3, :] * cv
        parts = [gb * conv]
        row = i * tm + lax.broadcasted_iota(jnp.int32, (tm, 1), 0)
        feats, _ = _pool_features(vp, vp_s, row, tm)
        for g in range(len(POOL_WINDOWS)):
            pw = jnp.dot(feats[g].astype(BF16), wp_ref[g].astype(BF16), preferred_element_type=F32)
            parts.append(pw * ps_ref[0:1, 128 * g : 128 * g + 128])
        cv_s[0:HALO, :] = cv_s[tm : tm + HALO, :]
        vp_s[0:HALO, :] = vp_s[tm : tm + HALO, :]
        ycat = jnp.concatenate(parts, axis=1).astype(BF16)
        mix = jnp.dot(ycat, wout_ref[...], preferred_element_type=F32)
        mix_ref[...] = mix
        xhat, rstd = _ln_fwd(DEEPNORM_ALPHA * xv + (1.0 + g1) * mix)
        xhat_ref[...] = xhat
        rstd_ref[...] = rstd

        for j in range(3):

            @pl.when(i == fwd_steps[j])
            def _(j=j):
                _gather_forward(g_ins, g_outs, g_sems, j)

        @pl.when(i == nt - 1)
        def _():
            _gather_finish(g_ins, g_outs, g_sems)

    tile = lambda w: pl.BlockSpec((tm, w), lambda i: (i, 0))
    return pl.pallas_call(
        body,
        name="f1",
        grid=(nt,),
        out_shape=[
            jax.ShapeDtypeStruct((T, ZW), BF16),
            jax.ShapeDtypeStruct((T, D), BF16),
            jax.ShapeDtypeStruct((T, D), F32),
            jax.ShapeDtypeStruct((T, 1), F32),
            jax.ShapeDtypeStruct((T, D), F32),
        ]
        + _gather_out_shape(shards16, by_cols),
        in_specs=[tile(D)] + [VMEM_SPEC] * (6 + ng),
        out_specs=[tile(ZW), tile(D), tile(D), tile(1), tile(D)] + [ANY_SPEC] * ng,
        scratch_shapes=[pltpu.VMEM((HALO + tm, CC), F32), pltpu.VMEM((HALO + tm, CC), F32)]
        + [pltpu.VMEM(s.shape, BF16) for s in shards16]
        + _gather_scratch(ng),
        compiler_params=pltpu.CompilerParams(dimension_semantics=("arbitrary",), vmem_limit_bytes=VMEM_LIMIT),
    )(x, mod, w_in, conv_w, w_pool, pool_scale, w_out, *gather)


def _fb2(xhat1, target, mod, ln, w_mi, w_mo, tm):
    T, D = xhat1.shape
    H = w_mi.shape[1]
    hc = min(1024, H)
    nb = H // hc
    nt = T // tm

    def body(xh1_ref, t_ref, mod_ref, ln_ref, wmi_ref, wmo_ref, dx1_ref, h2_ref, a_ref, du_ref, df_ref, acc_ref):
        i = pl.program_id(0)

        @pl.when(i == 0)
        def _():
            acc_ref[...] = jnp.zeros((8, D), F32)

        sh2, sc2, g2 = mod_ref[3:4, :], mod_ref[4:5, :], mod_ref[5:6, :]
        x1 = xh1_ref[...] * ln_ref[0:1, :] + ln_ref[1:2, :]
        h2 = (x1 * (1.0 + sc2) + sh2).astype(BF16)
        h2_ref[...] = h2
        f = jnp.zeros((tm, D), F32)
        for k in range(nb):
            ks = slice(k * hc, (k + 1) * hc)
            r = jnp.maximum(jnp.dot(h2, wmi_ref[:, ks], preferred_element_type=F32), 0.0)
            du_ref[:, ks] = r.astype(BF16)
            a = (r * r).astype(BF16)
            a_ref[:, ks] = a
            f = f + jnp.dot(a, wmo_ref[ks, :], preferred_element_type=F32)
        xhat2, rstd2 = _ln_fwd(DEEPNORM_ALPHA * x1 + (1.0 + g2) * f)
        ln2_g = ln_ref[2:3, :]
        d = xhat2 * ln2_g + ln_ref[3:4, :] - t_ref[...]
        dr2 = _ln_bwd(d * (ln2_g * (1.0 / D)), xhat2, rstd2)
        df = ((1.0 + g2) * dr2).astype(BF16)
        df_ref[...] = df
        dh2 = jnp.zeros((tm, D), F32)
        for k in range(nb):
            ks = slice(k * hc, (k + 1) * hc)
            da = lax.dot_general(df, wmo_ref[ks, :], NT, preferred_element_type=F32)
            du = (da * (2.0 * du_ref[:, ks].astype(F32))).astype(BF16)
            du_ref[:, ks] = du
            dh2 = dh2 + lax.dot_general(du, wmi_ref[:, ks], NT, preferred_element_type=F32)
        dx1_ref[...] = DEEPNORM_ALPHA * dr2 + dh2 * (1.0 + sc2)
        acc_ref[0:1, :] += _colsum(d * xhat2) * (1.0 / D)
        acc_ref[1:2, :] += _colsum(d) * (1.0 / D)
        acc_ref[2:3, :] += _colsum(dh2)
        acc_ref[3:4, :] += _colsum(dh2 * x1)
        acc_ref[4:5, :] += _colsum(dr2 * f)
        acc_ref[5:6, :] += jnp.zeros((1, D), F32) + (0.5 / D) * jnp.sum(d * d)

    tile = lambda w: pl.BlockSpec((tm, w), lambda i: (i, 0))
    return pl.pallas_call(
        body,
        name="fb2",
        grid=(nt,),
        out_shape=[
            jax.ShapeDtypeStruct((T, D), F32),
            jax.ShapeDtypeStruct((T, D), BF16),
            jax.ShapeDtypeStruct((T, H), BF16),
            jax.ShapeDtypeStruct((T, H), BF16),
            jax.ShapeDtypeStruct((T, D), BF16),
            jax.ShapeDtypeStruct((8, D), F32),
        ],
        in_specs=[tile(D), tile(D)] + [VMEM_SPEC] * 4,
        out_specs=[tile(D), tile(D), tile(H), tile(H), tile(D), pl.BlockSpec((8, D), lambda i: (0, 0))],
        compiler_params=pltpu.CompilerParams(dimension_semantics=("arbitrary",), vmem_limit_bytes=VMEM_LIMIT),
    )(xhat1, target, mod, ln, w_mi, w_mo)


def _b1(dx1, xhat1, rstd1, x, mix, z, mod, ln, w_out, w_in, conv_w, w_pool, pool_scale, tm):
    T, D = x.shape
    ZW = w_in.shape[1]
    CC = ZW // 4
    nt = T // tm
    hb = tm // HALO

    def body(dx1_ref, xh1_ref, rstd_ref, x_ref, mix_ref, z_ref, zh_ref, mod_ref, ln_ref, wout_ref, win_ref, cw_ref, wp_ref, ps_ref,
             dx_ref, dmix_ref, ycat_ref, dz_ref, acc_ref, gcw_ref, gwp_ref, cv_s, vp_s, e_s, q_s):
        i = pl.program_id(0)
        j = nt - 1 - i

        @pl.when(i == 0)
        def _():
            acc_ref[...] = jnp.zeros((8, D), F32)
            gcw_ref[...] = jnp.zeros((8, CC), F32)
            gwp_ref[...] = jnp.zeros(gwp_ref.shape, F32)
            e_s[tm : tm + HALO, :] = jnp.zeros((HALO, CC), F32)
            q_s[tm : tm + HALO, :] = jnp.zeros((HALO, CC), F32)

        sh1, sc1, g1 = mod_ref[0:1, :], mod_ref[1:2, :], mod_ref[2:3, :]
        dx1 = dx1_ref[...]
        xhat1 = xh1_ref[...]
        acc_ref[0:1, :] += _colsum(dx1 * xhat1)
        acc_ref[1:2, :] += _colsum(dx1)
        dr1 = _ln_bwd(dx1 * ln_ref[0:1, :], xhat1, rstd_ref[...])
        acc_ref[4:5, :] += _colsum(dr1 * mix_ref[...])
        dmix = ((1.0 + g1) * dr1).astype(BF16)
        dmix_ref[...] = dmix
        dycat = lax.dot_general(dmix, wout_ref[...], NT, preferred_element_type=F32)

        z = z_ref[...].astype(F32)
        zh = zh_ref[...].astype(F32) * jnp.where(j > 0, 1.0, 0.0)
        gb, gc, vc, vp = z[:, 0:CC], z[:, CC : 2 * CC], z[:, 2 * CC : 3 * CC], z[:, 3 * CC : 4 * CC]
        cv = gc * vc
        cv_s[0:HALO, :] = zh[:, CC : 2 * CC] * zh[:, 2 * CC : 3 * CC]
        cv_s[HALO : HALO + tm, :] = cv
        vp_s[0:HALO, :] = zh[:, 3 * CC : 4 * CC]
        vp_s[HALO : HALO + tm, :] = vp
        cv_ext = cv_s[...]
        cv_m2 = pltpu.roll(cv_ext, 2, 0)[HALO : HALO + tm, :]
        cv_m1 = pltpu.roll(cv_ext, 1, 0)[HALO : HALO + tm, :]
        w0, w1, w2 = cw_ref[0:1, :], cw_ref[1:2, :], cw_ref[2:3, :]
        conv = w0 * cv_m2 + w1 * cv_m1 + w2 * cv
        dyc = dycat[:, 0:CC]
        e = dyc * gb
        e_s[0:tm, :] = e
        e_ext = e_s[...]
        dcv = w2 * e + w1 * pltpu.roll(e_ext, tm + HALO - 1, 0)[0:tm, :] + w0 * pltpu.roll(e_ext, tm + HALO - 2, 0)[0:tm, :]
        gcw_ref[0:1, :] += _colsum(e * cv_m2)
        gcw_ref[1:2, :] += _colsum(e * cv_m1)
        gcw_ref[2:3, :] += _colsum(e * cv)
        y_parts = [gb * conv]
        dz_parts = [dyc * conv, dcv * vc, dcv * gc]

        row = j * tm + lax.broadcasted_iota(jnp.int32, (tm, 1), 0)
        feats, inv_cnts = _pool_features(vp, vp_s, row, tm)
        gps_parts, dps = [], []
        for g in range(len(POOL_WINDOWS)):
            cols = slice(128 * g, 128 * g + 128)
            p = feats[g].astype(BF16)
            scale = ps_ref[0:1, cols]
            wp = wp_ref[g].astype(BF16)
            pw = jnp.dot(p, wp, preferred_element_type=F32)
            y_parts.append(pw * scale)
            dyp = dycat[:, CC + 128 * g : CC + 128 * g + 128]
            gps_parts.append(_colsum(dyp * pw))
            dpw = (dyp * scale).astype(BF16)
            gwp_ref[g] += lax.dot_general(p, dpw, TN, preferred_element_type=F32)
            dp = lax.dot_general(dpw, wp, NT, preferred_element_type=F32)
            q_s[0:tm, cols] = dp * inv_cnts[g]
            dps.append(dp)
        sq = _window_sums(q_s[...], tm, causal=False)
        dz_parts += [sq[g] - dps[g] for g in range(len(POOL_WINDOWS))]
        gcw_ref[3:4, :] += jnp.concatenate(gps_parts, axis=1)
        ycat_ref[...] = jnp.concatenate(y_parts, axis=1).astype(BF16)
        dz = jnp.concatenate(dz_parts, axis=1).astype(BF16)
        dz_ref[...] = dz
        dh = lax.dot_general(dz, win_ref[...], NT, preferred_element_type=F32)
        acc_ref[2:3, :] += _colsum(dh)
        acc_ref[3:4, :] += _colsum(dh * x_ref[...])
        dx_ref[...] = DEEPNORM_ALPHA * dr1 + dh * (1.0 + sc1)
        e_s[tm : tm + HALO, :] = e_s[0:HALO, :]
        q_s[tm : tm + HALO, :] = q_s[0:HALO, :]

    tile = lambda w: pl.BlockSpec((tm, w), lambda i: (nt - 1 - i, 0))
    halo = pl.BlockSpec((HALO, ZW), lambda i: (jnp.maximum((nt - 1 - i) * hb - 1, 0), 0))
    fixed = lambda shape: pl.BlockSpec(shape, lambda i: (0,) * len(shape))
    return pl.pallas_call(
        body,
        name="b1",
        grid=(nt,),
        out_shape=[
            jax.ShapeDtypeStruct((T, D), F32),
            jax.ShapeDtypeStruct((T, D), BF16),
            jax.ShapeDtypeStruct((T, D), BF16),
            jax.ShapeDtypeStruct((T, ZW), BF16),
            jax.ShapeDtypeStruct((8, D), F32),
            jax.ShapeDtypeStruct((8, CC), F32),
            jax.ShapeDtypeStruct(w_pool.shape, F32),
        ],
        in_specs=[tile(D), tile(D), tile(1), tile(D), tile(D), tile(ZW), halo] + [VMEM_SPEC] * 7,
        out_specs=[tile(D), tile(D), tile(D), tile(ZW), fixed((8, D)), fixed((8, CC)), fixed(w_pool.shape)],
        scratch_shapes=[
            pltpu.VMEM((HALO + tm, CC), F32),
            pltpu.VMEM((HALO + tm, CC), F32),
            pltpu.VMEM((tm + HALO, CC), F32),
            pltpu.VMEM((tm + HALO, CC), F32),
        ],
        compiler_params=pltpu.CompilerParams(dimension_semantics=("arbitrary",), vmem_limit_bytes=VMEM_LIMIT),
    )(dx1, xhat1, rstd1, x, mix, z, z, mod, ln, w_out, w_in, conv_w, w_pool, pool_scale)


def _wgrad(a, b, bk, n_groups, bt, name, owners=None, scatter=None, gather=()):
    T, K = a.shape
    N = b.shape[1]
    nk, nt, ng = K // bk, T // bt, N // n_groups
    nc = min(512, ng)
    ns, ngat = (0 if scatter is None else 1), len(gather)
    n_steps = nk * n_groups * nt
    mid_step = min(1, n_steps - 1)
    fwd_steps = [min(2 * (j + 1), n_steps - 1) for j in range(3)]

    def body(*refs):
        a_ref, b_ref = refs[0], refs[1]
        g_ins = refs[2 + ns : 2 + ns + ngat]
        outs = refs[2 + ns + ngat :]
        o_ref, g_outs = outs[0], outs[1 + ns : 1 + ns + ngat]
        scr = outs[1 + ns + ngat :]
        acc = scr[0]
        if ns:
            s_hbm, s_recv, s_scr = refs[2], outs[1], scr[1 : 1 + N_SCATTER_SCRATCH]
        g_sems = scr[1 + N_SCATTER_SCRATCH * ns :]
        kk, gg, t = pl.program_id(0), pl.program_id(1), pl.program_id(2)
        step = (kk * n_groups + gg) * nt + t

        if ngat:

            @pl.when(step == 0)
            def _():
                _gather_start(g_ins, g_outs, g_sems)

            for j in range(3):

                @pl.when(step == fwd_steps[j])
                def _(j=j):
                    _gather_forward(g_ins, g_outs, g_sems, j)

        if ns:

            @pl.when(step == 0)
            def _():
                _scatter_start(s_hbm, s_scr)

            @pl.when(step == mid_step)
            def _():
                _scatter_middle(s_hbm, s_recv, s_scr)

        @pl.when(t == 0)
        def _():
            acc[...] = jnp.zeros(acc.shape, F32)

        at = a_ref[...].T
        for c in range(ng // nc):
            cs = slice(c * nc, (c + 1) * nc)
            acc[:, cs] += jnp.dot(at, b_ref[:, cs], preferred_element_type=F32)

        @pl.when(t == nt - 1)
        def _():
            if owners is None:
                o_ref[...] = acc[...].astype(BF16)
            else:
                per = N // owners
                for o in range(ng // per):
                    o_ref[o] = acc[:, o * per : (o + 1) * per].astype(BF16)

        if ns:

            @pl.when(step == n_steps - 1)
            def _():
                _scatter_finish(s_hbm, s_recv, s_scr)

        if ngat:

            @pl.when(step == n_steps - 1)
            def _():
                _gather_finish(g_ins, g_outs, g_sems)

    if owners is None:
        out_shape = [jax.ShapeDtypeStruct((K, N), BF16)]
        out_specs = [pl.BlockSpec((bk, ng), lambda k, g, t: (k, g))]
    else:
        assert bk == K
        per = N // owners
        out_shape = [jax.ShapeDtypeStruct((owners, K, per), BF16)]
        out_specs = [pl.BlockSpec((ng // per, K, per), lambda k, g, t: (g, 0, 0))]
    ins, in_specs = [a, b], [pl.BlockSpec((bt, bk), lambda k, g, t: (t, k)), pl.BlockSpec((bt, ng), lambda k, g, t: (t, g))]
    scratch = [pltpu.VMEM((bk, ng), F32)]
    if ns:
        ins.append(scatter)
        in_specs.append(ANY_SPEC)
        out_shape.append(_scatter_out_shape(scatter))
        out_specs.append(ANY_SPEC)
        scratch += _scatter_scratch(*scatter.shape[1:])
    if ngat:
        ins += list(gather)
        in_specs += [ANY_SPEC] * ngat
        out_shape += _gather_out_shape(gather, [False] * ngat)
        out_specs += [ANY_SPEC] * ngat
        scratch += _gather_scratch(ngat)
    outs = pl.pallas_call(
        body,
        name=name,
        grid=(nk, n_groups, nt),
        out_shape=out_shape,
        in_specs=in_specs,
        out_specs=out_specs,
        scratch_shapes=scratch,
        compiler_params=pltpu.CompilerParams(dimension_semantics=("arbitrary", "arbitrary", "arbitrary"), vmem_limit_bytes=VMEM_LIMIT),
    )(*ins)
    return outs if ns + ngat else outs[0]


def _small_grads(acc1_all, acc2_all, gcw_all, gwp_all, cond_t, my_slot, w_ada, m_w_ada, v_w_ada):
    D = acc1_all.shape[2]
    w_cols = w_ada.shape[2]
    n_chunk = D // 128
    q_mine = w_cols // 128

    def total(ref, r):
        s = ref[0, r : r + 1, :]
        for k in range(1, N_DEV):
            s = s + ref[k, r : r + 1, :]
        return s

    def body(slot_ref, a1_ref, a2_ref, gcw_ref, gwp_ref, ct_ref, w_ref, m_ref, v_ref,
             gb_ref, gw_ref, gln_ref, gcwo_ref, gwpo_ref, loss_ref, dw_ref, nm_ref, nv_ref, dm_s):
        loss_ref[...] = total(a2_ref, 5)
        for s, (ref, r) in enumerate([(a1_ref, 2), (a1_ref, 3), (a1_ref, 4), (a2_ref, 2), (a2_ref, 3), (a2_ref, 4)]):
            gb_ref[0:1, s * D : (s + 1) * D] = total(ref, r)
            for k in range(N_DEV):
                row = ref[k, r : r + 1, :]
                for qq in range(n_chunk):
                    dm_s[s * n_chunk + qq, k : k + 1, :] = row[:, 128 * qq : 128 * qq + 128]
        gln_ref[0:1, :] = total(a1_ref, 0)
        gln_ref[1:2, :] = total(a1_ref, 1)
        gln_ref[2:3, :] = total(a2_ref, 0)
        gln_ref[3:4, :] = total(a2_ref, 1)
        gcwo_ref[...] = jnp.zeros(gcwo_ref.shape, F32)
        for r in range(4):
            gcwo_ref[r : r + 1, :] = total(gcw_ref, r)
        wp = gwp_ref[0]
        for k in range(1, N_DEV):
            wp = wp + gwp_ref[k]
        gwpo_ref[0] = wp
        ct = ct_ref[...]
        cond_t = ct * jax.nn.sigmoid(ct)
        q0 = slot_ref[0] * q_mine
        for q in range(q_mine):
            dm = dm_s[q0 + q]
            out = cond_t[:, 0:1] * dm[0:1, :]
            for k in range(1, N_DEV):
                out = out + cond_t[:, k : k + 1] * dm[k : k + 1, :]
            cols = slice(128 * q, 128 * q + 128)
            gw_ref[0, :, cols] = out
            delta, nm, nv = _adamw_math(w_ref[0, :, cols], out, m_ref[0, :, cols], v_ref[0, :, cols])
            dw_ref[0, :, cols] = delta
            nm_ref[0, :, cols] = nm
            nv_ref[0, :, cols] = nv

    CC = gcw_all.shape[2]
    w_like = jax.ShapeDtypeStruct(w_ada.shape, F32)
    return pl.pallas_call(
        body,
        name="small_grads",
        out_shape=[
            jax.ShapeDtypeStruct((1, 6 * D), F32),
            w_like,
            jax.ShapeDtypeStruct((4, D), F32),
            jax.ShapeDtypeStruct((8, CC), F32),
            jax.ShapeDtypeStruct((1, *gwp_all.shape[1:]), F32),
            jax.ShapeDtypeStruct((1, D), F32),
            w_like,
            w_like,
            w_like,
        ],
        in_specs=[pl.BlockSpec(memory_space=pltpu.SMEM)] + [VMEM_SPEC] * 8,
        out_specs=[VMEM_SPEC] * 9,
        scratch_shapes=[pltpu.VMEM((6 * n_chunk, N_DEV, 128), F32)],
        compiler_params=pltpu.CompilerParams(vmem_limit_bytes=VMEM_LIMIT),
    )(my_slot, acc1_all, acc2_all, gcw_all, gwp_all, cond_t, w_ada, m_w_ada, v_w_ada)


def kernel(x, c, w_ada, b_ada, w_in, conv_w, w_pool, pool_scale, w_out, ln1_g, ln1_b, w_mlp_in, w_mlp_out, ln2_g, ln2_b, loss_target, m_w_ada, m_b_ada, m_w_in, m_conv_w, m_w_pool, m_pool_scale, m_w_out, m_ln1_g, m_ln1_b, m_w_mlp_in, m_w_mlp_out, m_ln2_g, m_ln2_b, v_w_ada, v_b_ada, v_w_in, v_conv_w, v_w_pool, v_pool_scale, v_w_out, v_ln1_g, v_ln1_b, v_w_mlp_in, v_w_mlp_out, v_ln2_g, v_ln2_b):
    T, D = x.shape[1], x.shape[2]
    H = w_mlp_out.shape[1] * N_DEV
    ZW = w_in.shape[2] * N_DEV
    CC = ZW // 4
    tm = min(512, T // 2)
    bt = min(1024, T)
    ax, ay, ac = _my_place()
    me = _slot(ax, ay, ac)

    w_cols = w_ada.shape[2]
    b_mine = lax.dynamic_slice(b_ada, (0, me * w_cols), (1, w_cols))
    w_in_f, w_out_g, cw_g, c_g, mod_g = _prologue(w_in, w_out, conv_w[0], c, w_ada, b_mine)
    w_out_f = w_out_g.reshape(D, D)
    conv_w_f = jnp.transpose(cw_g, (1, 0, 2)).reshape(conv_w.shape[1], CC)
    c_all = c_g.reshape(N_DEV, D)
    mod = lax.dynamic_index_in_dim(mod_g, me, axis=1, keepdims=False).reshape(6, D)

    ln = jnp.concatenate([ln1_g, ln1_b, ln2_g, ln2_b], axis=0)
    xs, target = x[0], loss_target[0]

    z, h, xhat1, rstd1, mix, w_mi_f, w_mo_g = _f1(
        xs, mod, w_in_f, conv_w_f, w_pool[0], pool_scale, w_out_f, tm,
        [w_mlp_in, w_mlp_out], [True, False])
    dx1, h2, a, du, df, acc2 = _fb2(xhat1, target, mod, ln, w_mi_f, w_mo_g.reshape(H, D), tm)

    grad_x, dmix, ycat, dz, acc1, gcw, gwp = _b1(
        dx1, xhat1, rstd1, xs, mix, z, mod, ln, w_out_f, w_in_f, conv_w_f, w_pool[0], pool_scale, tm)

    gp_mo = _wgrad(a, df, D, 1, min(2 * bt, T), "wgrad_mlp_out").reshape(N_DEV, H // N_DEV, D)
    gp_mi, rv_mo = _wgrad(h2, du, D, 2, min(2 * bt, T), "wgrad_mlp_in", owners=N_DEV, scatter=gp_mo)
    gp_in, rv_mi, acc1_g, acc2_g, gcw_g, gwp_g = _wgrad(
        h, dz, D, 1, bt, "wgrad_in", owners=N_DEV, scatter=gp_mi, gather=[acc1, acc2, gcw, gwp])
    gp_out, rv_in = _wgrad(ycat, dmix, D, 1, bt, "wgrad_out", scatter=gp_in)
    rv_out = _reduce_scatter(gp_out.reshape(N_DEV, D // N_DEV, D), "scatter_w_out")
    g_b_ada, g_w_ada, g_ln, g_cw, g_w_pool, loss_row, *u_w_ada = _small_grads(
        acc1_g, acc2_g, gcw_g, gwp_g, c_all.T, jnp.reshape(me, (1,)).astype(jnp.int32), w_ada, m_w_ada, v_w_ada)
    cc_mine = conv_w.shape[2]
    g_conv_w = lax.dynamic_slice(g_cw, (0, me * cc_mine), (conv_w.shape[1], cc_mine))[None]
    g_pool_scale = g_cw[3:4, :]
    g_ln1_g, g_ln1_b, g_ln2_g, g_ln2_b = g_ln[0:1], g_ln[1:2], g_ln[2:3], g_ln[3:4]

    small = _adamw_multi(
        [
            (b_ada, g_b_ada, m_b_ada, v_b_ada),
            (conv_w, g_conv_w, m_conv_w, v_conv_w),
            (w_pool, g_w_pool, m_w_pool, v_w_pool),
            (pool_scale, g_pool_scale, m_pool_scale, v_pool_scale),
            (ln1_g, g_ln1_g, m_ln1_g, v_ln1_g),
            (ln1_b, g_ln1_b, m_ln1_b, v_ln1_b),
            (ln2_g, g_ln2_g, m_ln2_g, v_ln2_g),
            (ln2_b, g_ln2_b, m_ln2_b, v_ln2_b),
        ],
        "adamw_small")
    u_b_ada, u_conv_w, u_w_pool, u_pool_scale, u_ln1_g, u_ln1_b, u_ln2_g, u_ln2_b = small

    g_w_mo, *u_w_mo = _sum_adamw(rv_mo, w_mlp_out, m_w_mlp_out, v_w_mlp_out, "sum_w_mlp_out")
    g_w_mi, *u_w_mi = _sum_adamw(rv_mi, w_mlp_in, m_w_mlp_in, v_w_mlp_in, "sum_w_mlp_in")
    g_w_in, *u_w_in = _sum_adamw(rv_in, w_in, m_w_in, v_w_in, "sum_w_in")
    g_w_out, *u_w_out = _sum_adamw(rv_out, w_out, m_w_out, v_w_out, "sum_w_out")

    grads = [g_w_ada, g_b_ada, g_w_in, g_conv_w, g_w_pool, g_pool_scale, g_w_out, g_ln1_g, g_ln1_b, g_w_mi, g_w_mo, g_ln2_g, g_ln2_b]
    updates = [u_w_ada, u_b_ada, u_w_in, u_conv_w, u_w_pool, u_pool_scale, u_w_out, u_ln1_g, u_ln1_b, u_w_mi, u_w_mo, u_ln2_g, u_ln2_b]
    deltas = [u[0] for u in updates]
    new_m = [u[1] for u in updates]
    new_v = [u[2] for u in updates]
    return (loss_row[0, 0], grad_x[None], *grads, *deltas, *new_m, *new_v)
```

```python
import jax
import jax.numpy as jnp
from jax import lax
from jax.experimental import pallas as pl
from jax.experimental.pallas import tpu as pltpu

F32 = jnp.float32
BF16 = jnp.bfloat16
MESH = pl.DeviceIdType.MESH
N_DEV = 8

LN_EPS = 1e-5
DEPTH = 1
DEEPNORM_ALPHA = (2.0 * DEPTH) ** 0.25
POOL_WINDOWS = (2, 4, 8, 16)
HALO = 16

ADAM_LR = 0.001
ADAM_B1 = 0.9
ADAM_B2 = 0.999
ADAM_EPS = 1e-08
ADAM_WD = 0.01
ADAM_STEP = 10

VMEM_LIMIT = 60 * 1024 * 1024

VMEM_SPEC = pl.BlockSpec(memory_space=pltpu.VMEM)
ANY_SPEC = pl.BlockSpec(memory_space=pl.ANY)

NT = (((1,), (1,)), ((), ()))
TN = (((0,), (0,)), ((), ()))


def _my_place():
    return lax.axis_index("x"), lax.axis_index("y"), lax.axis_index("c")


def _slot(x, y, c):
    return 4 * x + 2 * y + c


def _gather_place(ins, outs, a, slot):
    if len(outs[a].shape) == len(ins[a].shape):
        wb = ins[a].shape[1]
        return outs[a].at[:, pl.ds(pl.multiple_of(slot * wb, wb), wb)]
    return outs[a].at[slot]


def _gather_copy(ins, outs, sems, a, k, block, to, from_shard=False):
    send_sems, recv_sems, _ = sems
    dst = _gather_place(ins, outs, a, _slot(*block))
    return pltpu.make_async_remote_copy(
        src_ref=ins[a] if from_shard else dst,
        dst_ref=dst,
        send_sem=send_sems.at[7 * a + k],
        recv_sem=recv_sems.at[7 * a + k],
        device_id=to,
        device_id_type=MESH,
    )


def _gather_peers():
    x, y, c = _my_place()
    return (x, y, c), (x, y, 1 - c), [(1 - x, y), (x, 1 - y), (1 - x, 1 - y)]


def _gather_first(ins, outs, sems):
    me, sibling, chips = _gather_peers()
    first = []
    for a in range(len(ins)):
        first.append(_gather_copy(ins, outs, sems, a, 0, me, sibling, from_shard=True))
        first += [_gather_copy(ins, outs, sems, a, 1 + j, me, (*chip, me[2]), from_shard=True) for j, chip in enumerate(chips)]
    return first


def _gather_mine(ins, outs, sems, a):
    me, _, _ = _gather_peers()
    return pltpu.make_async_copy(ins[a], _gather_place(ins, outs, a, _slot(*me)), sems[2].at[a])


def _gather_start(ins, outs, sems):
    for a in range(len(ins)):
        _gather_mine(ins, outs, sems, a).start()
    for cp in _gather_first(ins, outs, sems):
        cp.start()


def _gather_forward(ins, outs, sems, j):
    me, sibling, chips = _gather_peers()
    for a in range(len(ins)):
        _gather_copy(ins, outs, sems, a, 1 + j, (*chips[j], me[2]), me).wait_recv()
        _gather_copy(ins, outs, sems, a, 4 + j, (*chips[j], me[2]), sibling).start()


def _gather_finish(ins, outs, sems):
    me, sibling, chips = _gather_peers()
    for a in range(len(ins)):
        _gather_copy(ins, outs, sems, a, 0, sibling, me).wait_recv()
        for j, chip in enumerate(chips):
            _gather_copy(ins, outs, sems, a, 4 + j, (*chip, 1 - me[2]), me).wait_recv()
    for cp in _gather_first(ins, outs, sems):
        cp.wait_send()
    for a in range(len(ins)):
        for j, chip in enumerate(chips):
            _gather_copy(ins, outs, sems, a, 4 + j, (*chip, me[2]), sibling).wait_send()
        _gather_mine(ins, outs, sems, a).wait()


def _gather_scratch(n):
    return [pltpu.SemaphoreType.DMA((7 * n,)), pltpu.SemaphoreType.DMA((7 * n,)), pltpu.SemaphoreType.DMA((n,))]


def _gather_out_shape(shards, by_cols):
    return [
        jax.ShapeDtypeStruct((s.shape[0], N_DEV * s.shape[1]) if cols else (N_DEV, *s.shape), s.dtype)
        for s, cols in zip(shards, by_cols)
    ]


N_CHIP = 4


def _scatter_scratch(rows, cols):
    block = pltpu.VMEM((N_CHIP, rows, cols), BF16)
    dma = pltpu.SemaphoreType.DMA
    return [block, block, block, dma((N_CHIP,)), dma((N_CHIP,)), dma((N_CHIP,)), dma((N_CHIP - 1,)), dma((N_CHIP - 1,)), dma]


def _scatter_pair_copies(g_hbm, scr):
    x, y, c = _my_place()
    mine, theirs, _, a_send, a_recv, load_sem = scr[:6]
    to_sibling = [
        pltpu.make_async_remote_copy(
            src_ref=g_hbm.at[2 * q + (1 - c)], dst_ref=theirs.at[q], send_sem=a_send.at[q], recv_sem=a_recv.at[q],
            device_id=(x, y, 1 - c), device_id_type=MESH)
        for q in range(N_CHIP)
    ]
    loads = [pltpu.make_async_copy(g_hbm.at[2 * q + c], mine.at[q], load_sem.at[q]) for q in range(N_CHIP)]
    return to_sibling, loads


def _scatter_sum_copies(recv, scr):
    x, y, c = _my_place()
    sums, b_send, b_recv, own_sem = scr[2], scr[6], scr[7], scr[8]
    q_me = 2 * x + y
    to_owner = [
        pltpu.make_async_remote_copy(
            src_ref=sums.at[2 * px + py], dst_ref=recv.at[q_me], send_sem=b_send.at[j], recv_sem=b_recv.at[j],
            device_id=(px, py, c), device_id_type=MESH)
        for j, (px, py) in enumerate([(1 - x, y), (x, 1 - y), (1 - x, 1 - y)])
    ]
    return to_owner, pltpu.make_async_copy(sums.at[q_me], recv.at[q_me], own_sem)


def _scatter_start(g_hbm, scr):
    to_sibling, loads = _scatter_pair_copies(g_hbm, scr)
    for cp in to_sibling + loads:
        cp.start()


def _scatter_middle(g_hbm, recv, scr):
    to_sibling, loads = _scatter_pair_copies(g_hbm, scr)
    for cp in to_sibling:
        cp.wait_recv()
    for cp in loads:
        cp.wait()
    mine, theirs, sums = scr[:3]

    def step(r, carry):
        rs = pl.ds(pl.multiple_of(r * ROW_CHUNK, ROW_CHUNK), ROW_CHUNK)
        for q in range(N_CHIP):
            sums[q, rs, :] = (mine[q, rs, :].astype(F32) + theirs[q, rs, :].astype(F32)).astype(BF16)
        return carry

    lax.fori_loop(0, mine.shape[1] // ROW_CHUNK, step, 0)
    to_owner, own = _scatter_sum_copies(recv, scr)
    for cp in to_owner + [own]:
        cp.start()


def _scatter_finish(g_hbm, recv, scr):
    to_sibling, _ = _scatter_pair_copies(g_hbm, scr)
    to_owner, own = _scatter_sum_copies(recv, scr)
    for cp in to_owner:
        cp.wait_recv()
    for cp in to_sibling + to_owner:
        cp.wait_send()
    own.wait()


def _scatter_out_shape(gparts):
    return jax.ShapeDtypeStruct((N_CHIP, *gparts.shape[1:]), gparts.dtype)


def _adamw_math(w, g, m, v):
    m = ADAM_B1 * m + (1.0 - ADAM_B1) * g
    v = ADAM_B2 * v + (1.0 - ADAM_B2) * (g * g)
    m_hat = m / (1.0 - ADAM_B1**ADAM_STEP)
    v_hat = v / (1.0 - ADAM_B2**ADAM_STEP)
    delta = -ADAM_LR * (m_hat / (jnp.sqrt(v_hat) + ADAM_EPS) + ADAM_WD * w)
    return delta, m, v


ROW_CHUNK = 64


ELEMS_PER_STEP = 128 * 1024


N_SCATTER_SCRATCH = 9


def _reduce_scatter(gparts, name):
    n = len(gparts)

    def body(*refs):
        g_hbm, recv = refs[:n], refs[n : 2 * n]
        scr = [refs[2 * n + k * N_SCATTER_SCRATCH : 2 * n + (k + 1) * N_SCATTER_SCRATCH] for k in range(n)]
        for k in range(n):
            _scatter_start(g_hbm[k], scr[k])
        for k in range(n):
            _scatter_middle(g_hbm[k], recv[k], scr[k])
        for k in range(n):
            _scatter_finish(g_hbm[k], recv[k], scr[k])

    return pl.pallas_call(
        body,
        name=name,
        out_shape=[_scatter_out_shape(g) for g in gparts],
        in_specs=[ANY_SPEC] * n,
        out_specs=[ANY_SPEC] * n,
        scratch_shapes=[s for g in gparts for s in _scatter_scratch(*g.shape[1:])],
        compiler_params=pltpu.CompilerParams(vmem_limit_bytes=VMEM_LIMIT),
    )(*gparts)


def _sum_adamw(parts, w, m, v, name):
    _, rows, cols = w.shape
    rb = rows
    while rb * cols > ELEMS_PER_STEP and rb % 16 == 0:
        rb //= 2

    def body(p_ref, w_ref, m_ref, v_ref, grad_ref, delta_ref, nm_ref, nv_ref):
        g = p_ref[0].astype(F32)
        for k in range(1, p_ref.shape[0]):
            g = g + p_ref[k].astype(F32)
        delta, nm, nv = _adamw_math(w_ref[0], g, m_ref[0], v_ref[0])
        grad_ref[0] = g
        delta_ref[0] = delta
        nm_ref[0] = nm
        nv_ref[0] = nv

    block = lambda lead: pl.BlockSpec((lead, rb, cols), lambda i: (0, i, 0))
    out = jax.ShapeDtypeStruct(w.shape, F32)
    return pl.pallas_call(
        body,
        name=name,
        grid=(rows // rb,),
        out_shape=[out] * 4,
        in_specs=[block(parts.shape[0])] + [block(1)] * 3,
        out_specs=[block(1)] * 4,
        compiler_params=pltpu.CompilerParams(dimension_semantics=("arbitrary",), vmem_limit_bytes=VMEM_LIMIT),
    )(parts, w, m, v)


def _adamw_multi(items, name):
    n = len(items)

    def body(*refs):
        ins, outs = refs[: 4 * n], refs[4 * n :]
        for a in range(n):
            w_ref, g_ref, m_ref, v_ref = ins[4 * a : 4 * a + 4]
            d_ref, nm_ref, nv_ref = outs[3 * a : 3 * a + 3]
            delta, nm, nv = _adamw_math(w_ref[...], g_ref[...], m_ref[...], v_ref[...])
            d_ref[...] = delta
            nm_ref[...] = nm
            nv_ref[...] = nv

    flat = [a for it in items for a in it]
    out_shape = [jax.ShapeDtypeStruct(it[0].shape, F32) for it in items for _ in range(3)]
    outs = pl.pallas_call(
        body,
        name=name,
        out_shape=out_shape,
        in_specs=[VMEM_SPEC] * (4 * n),
        out_specs=[VMEM_SPEC] * (3 * n),
        compiler_params=pltpu.CompilerParams(vmem_limit_bytes=VMEM_LIMIT),
    )(*flat)
    return [tuple(outs[3 * a : 3 * a + 3]) for a in range(n)]


def _prologue(w_in, w_out, conv_w, c, w_ada, b_mine):
    D = c.shape[1]
    wc = w_ada.shape[2]
    shards16 = [jax.ShapeDtypeStruct(w_in.shape[1:], BF16), jax.ShapeDtypeStruct(w_out.shape[1:], BF16)]

    def to_all(src, out, sems):
        x, y, c = _my_place()
        me = _slot(x, y, c)
        copies = [
            pltpu.make_async_remote_copy(
                src_ref=src, dst_ref=out.at[me], send_sem=sems[0].at[k - 1], recv_sem=sems[1].at[k - 1],
                device_id=(x ^ (k >> 2), y ^ ((k >> 1) & 1), c ^ (k & 1)), device_id_type=MESH)
            for k in range(1, N_DEV)
        ]
        return copies, pltpu.make_async_copy(src, out.at[me], sems[2].at[0])

    def start(copies, own):
        for cp in copies + [own]:
            cp.start()

    def finish(copies, own):
        for cp in copies:
            cp.wait()
        own.wait()

    def body(win_ref, wout_ref, cw_ref, c_ref, wada_ref, b_ref, win_g, wout_g, cw_g, c_g, mod_g, win16, wout16, c_s, mp_s, *sems):
        c_copies = to_all(c_ref, c_g, sems[3:6])
        start(*c_copies)
        win16[...] = win_ref[0].astype(BF16)
        wout16[...] = wout_ref[0].astype(BF16)
        w_ins, w_outs, w_sems = (win16, wout16, cw_ref), (win_g, wout_g, cw_g), sems[0:3]
        _gather_start(w_ins, w_outs, w_sems)
        finish(*c_copies)
        for k in range(N_DEV):
            c_s[k : k + 1, :] = c_g[k]
        cv = c_s[...]
        cond = cv * jax.nn.sigmoid(cv)
        mp_s[...] = jnp.dot(cond, wada_ref[0], precision=lax.Precision.HIGHEST, preferred_element_type=F32) + b_ref[...]
        m_copies = to_all(mp_s, mod_g, sems[6:9])
        start(*m_copies)
        for j in range(3):
            _gather_forward(w_ins, w_outs, w_sems, j)
        _gather_finish(w_ins, w_outs, w_sems)
        finish(*m_copies)

    return pl.pallas_call(
        body,
        name="prologue",
        out_shape=_gather_out_shape(shards16 + [conv_w], [True, False, False])
        + [jax.ShapeDtypeStruct((N_DEV, 1, D), F32), jax.ShapeDtypeStruct((N_DEV, N_DEV, wc), F32)],
        in_specs=[VMEM_SPEC, VMEM_SPEC, ANY_SPEC] + [VMEM_SPEC] * 3,
        out_specs=[ANY_SPEC] * 3 + [VMEM_SPEC] * 2,
        scratch_shapes=[pltpu.VMEM(s.shape, BF16) for s in shards16]
        + [pltpu.VMEM((N_DEV, D), F32), pltpu.VMEM((N_DEV, wc), F32)]
        + _gather_scratch(3) + _gather_scratch(1) + _gather_scratch(1),
        compiler_params=pltpu.CompilerParams(vmem_limit_bytes=VMEM_LIMIT),
    )(w_in, w_out, conv_w, c, w_ada, b_mine)


def _ln_fwd(r):
    mu = jnp.mean(r, axis=-1, keepdims=True)
    d = r - mu
    var = jnp.mean(d * d, axis=-1, keepdims=True)
    rstd = lax.rsqrt(var + LN_EPS)
    return d * rstd, rstd


def _ln_bwd(dxh, xhat, rstd):
    m1 = jnp.mean(dxh, axis=-1, keepdims=True)
    m2 = jnp.mean(dxh * xhat, axis=-1, keepdims=True)
    return rstd * (dxh - m1 - xhat * m2)


def _colsum(a):
    return jnp.sum(a, axis=0, keepdims=True)


def _window_sums(ext, tm, causal):
    n = ext.shape[0]
    lo = HALO if causal else 0
    s, out = ext, []
    for p in range(len(POOL_WINDOWS)):
        assert POOL_WINDOWS[p] == 2 ** (p + 1)
        k = 2**p
        s = s + pltpu.roll(s, k if causal else n - k, 0)
        out.append(s[lo : lo + tm, 0:128])
        if p + 1 < len(POOL_WINDOWS):
            s = s[:, 128:]
    return out


def _pool_features(vp, vp_s, row, tm):
    sums = _window_sums(vp_s[...], tm, causal=True)
    feats, inv_cnts = [], []
    for g, win in enumerate(POOL_WINDOWS):
        inv_cnt = 1.0 / jnp.minimum(row + 1, win).astype(F32)
        feats.append(sums[g] * inv_cnt - vp[:, 128 * g : 128 * g + 128])
        inv_cnts.append(inv_cnt)
    return feats, inv_cnts


def _f1(x, mod, w_in, conv_w, w_pool, pool_scale, w_out, tm, gather, by_cols):
    T, D = x.shape
    ZW = w_in.shape[1]
    CC = ZW // 4
    nt = T // tm
    ng = len(gather)
    fwd_steps = [max(nt - 3 + j, 0) for j in range(3)]

    shards16 = [jax.ShapeDtypeStruct(s.shape[1:], BF16) for s in gather]

    def body(*refs):
        x_ref, mod_ref, win_ref, cw_ref, wp_ref, ps_ref, wout_ref = refs[:7]
        g_f32 = refs[7 : 7 + ng]
        z_ref, h_ref, xhat_ref, rstd_ref, mix_ref = refs[7 + ng : 12 + ng]
        g_outs = refs[12 + ng : 12 + 2 * ng]
        cv_s, vp_s = refs[12 + 2 * ng : 14 + 2 * ng]
        g_ins = refs[14 + 2 * ng : 14 + 3 * ng]
        g_sems = refs[14 + 3 * ng :]
        i = pl.program_id(0)

        @pl.when(i == 0)
        def _():
            for src, dst in zip(g_f32, g_ins):
                dst[...] = src[0].astype(BF16)
            _gather_start(g_ins, g_outs, g_sems)
            cv_s[0:HALO, :] = jnp.zeros((HALO, CC), F32)
            vp_s[0:HALO, :] = jnp.zeros((HALO, CC), F32)

        xv = x_ref[...]
        sh1, sc1, g1 = mod_ref[0:1, :], mod_ref[1:2, :], mod_ref[2:3, :]
        h = (xv * (1.0 + sc1) + sh1).astype(BF16)
        h_ref[...] = h
        z = jnp.dot(h, win_ref[...], preferred_element_type=F32)
        z_ref[...] = z.astype(BF16)
        gb, gc, vc, vp = z[:, 0:CC], z[:, CC : 2 * CC], z[:, 2 * CC : 3 * CC], z[:, 3 * CC : 4 * CC]
        cv = gc * vc
        cv_s[HALO : HALO + tm, :] = cv
        vp_s[HALO : HALO + tm, :] = vp
        cv_ext = cv_s[...]
        cv_m2 = pltpu.roll(cv_ext, 2, 0)[HALO : HALO + tm, :]
        cv_m1 = pltpu.roll(cv_ext, 1, 0)[HALO : HALO + tm, :]
        conv = cw_ref[0:1, :] * cv_m2 + cw_ref[1:2, :] * cv_m1 + cw_ref[2:3, :] * cv
        parts = [gb * conv]
        row = i * tm + lax.broadcasted_iota(jnp.int32, (tm, 1), 0)
        feats, _ = _pool_features(vp, vp_s, row, tm)
        for g in range(len(POOL_WINDOWS)):
            pw = jnp.dot(feats[g].astype(BF16), wp_ref[g].astype(BF16), preferred_element_type=F32)
            parts.append(pw * ps_ref[0:1, 128 * g : 128 * g + 128])
        cv_s[0:HALO, :] = cv_s[tm : tm + HALO, :]
        vp_s[0:HALO, :] = vp_s[tm : tm + HALO, :]
        ycat = jnp.concatenate(parts, axis=1).astype(BF16)
        mix = jnp.dot(ycat, wout_ref[...], preferred_element_type=F32)
        mix_ref[...] = mix
        xhat, rstd = _ln_fwd(DEEPNORM_ALPHA * xv + (1.0 + g1) * mix)
        xhat_ref[...] = xhat
        rstd_ref[...] = rstd

        for j in range(3):

            @pl.when(i == fwd_steps[j])
            def _(j=j):
                _gather_forward(g_ins, g_outs, g_sems, j)

        @pl.when(i == nt - 1)
        def _():
            _gather_finish(g_ins, g_outs, g_sems)

    tile = lambda w: pl.BlockSpec((tm, w), lambda i: (i, 0))
    return pl.pallas_call(
        body,
        name="f1",
        grid=(nt,),
        out_shape=[
            jax.ShapeDtypeStruct((T, ZW), BF16),
            jax.ShapeDtypeStruct((T, D), BF16),
            jax.ShapeDtypeStruct((T, D), F32),
            jax.ShapeDtypeStruct((T, 1), F32),
            jax.ShapeDtypeStruct((T, D), F32),
        ]
        + _gather_out_shape(shards16, by_cols),
        in_specs=[tile(D)] + [VMEM_SPEC] * (6 + ng),
        out_specs=[tile(ZW), tile(D), tile(D), tile(1), tile(D)] + [ANY_SPEC] * ng,
        scratch_shapes=[pltpu.VMEM((HALO + tm, CC), F32), pltpu.VMEM((HALO + tm, CC), F32)]
        + [pltpu.VMEM(s.shape, BF16) for s in shards16]
        + _gather_scratch(ng),
        compiler_params=pltpu.CompilerParams(dimension_semantics=("arbitrary",), vmem_limit_bytes=VMEM_LIMIT),
    )(x, mod, w_in, conv_w, w_pool, pool_scale, w_out, *gather)


def _fb2(xhat1, target, mod, ln, w_mi, w_mo, tm):
    T, D = xhat1.shape
    H = w_mi.shape[1]
    hc = min(1024, H)
    nb = H // hc
    nt = T // tm

    def body(xh1_ref, t_ref, mod_ref, ln_ref, wmi_ref, wmo_ref, dx1_ref, h2_ref, a_ref, du_ref, df_ref, acc_ref):
        i = pl.program_id(0)

        @pl.when(i == 0)
        def _():
            acc_ref[...] = jnp.zeros((8, D), F32)

        sh2, sc2, g2 = mod_ref[3:4, :], mod_ref[4:5, :], mod_ref[5:6, :]
        x1 = xh1_ref[...] * ln_ref[0:1, :] + ln_ref[1:2, :]
        h2 = (x1 * (1.0 + sc2) + sh2).astype(BF16)
        h2_ref[...] = h2
        f = jnp.zeros((tm, D), F32)
        for k in range(nb):
            ks = slice(k * hc, (k + 1) * hc)
            r = jnp.maximum(jnp.dot(h2, wmi_ref[:, ks], preferred_element_type=F32), 0.0)
            du_ref[:, ks] = r.astype(BF16)
            a = (r * r).astype(BF16)
            a_ref[:, ks] = a
            f = f + jnp.dot(a, wmo_ref[ks, :], preferred_element_type=F32)
        xhat2, rstd2 = _ln_fwd(DEEPNORM_ALPHA * x1 + (1.0 + g2) * f)
        ln2_g = ln_ref[2:3, :]
        d = xhat2 * ln2_g + ln_ref[3:4, :] - t_ref[...]
        dr2 = _ln_bwd(d * (ln2_g * (1.0 / D)), xhat2, rstd2)
        df = ((1.0 + g2) * dr2).astype(BF16)
        df_ref[...] = df
        dh2 = jnp.zeros((tm, D), F32)
        for k in range(nb):
            ks = slice(k * hc, (k + 1) * hc)
            da = lax.dot_general(df, wmo_ref[ks, :], NT, preferred_element_type=F32)
            du = (da * (2.0 * du_ref[:, ks].astype(F32))).astype(BF16)
            du_ref[:, ks] = du
            dh2 = dh2 + lax.dot_general(du, wmi_ref[:, ks], NT, preferred_element_type=F32)
        dx1_ref[...] = DEEPNORM_ALPHA * dr2 + dh2 * (1.0 + sc2)
        acc_ref[0:1, :] += _colsum(d * xhat2) * (1.0 / D)
        acc_ref[1:2, :] += _colsum(d) * (1.0 / D)
        acc_ref[2:3, :] += _colsum(dh2)
        acc_ref[3:4, :] += _colsum(dh2 * x1)
        acc_ref[4:5, :] += _colsum(dr2 * f)
        acc_ref[5:6, :] += jnp.zeros((1, D), F32) + (0.5 / D) * jnp.sum(d * d)

    tile = lambda w: pl.BlockSpec((tm, w), lambda i: (i, 0))
    return pl.pallas_call(
        body,
        name="fb2",
        grid=(nt,),
        out_shape=[
            jax.ShapeDtypeStruct((T, D), F32),
            jax.ShapeDtypeStruct((T, D), BF16),
            jax.ShapeDtypeStruct((T, H), BF16),
            jax.ShapeDtypeStruct((T, H), BF16),
            jax.ShapeDtypeStruct((T, D), BF16),
            jax.ShapeDtypeStruct((8, D), F32),
        ],
        in_specs=[tile(D), tile(D)] + [VMEM_SPEC] * 4,
        out_specs=[tile(D), tile(D), tile(H), tile(H), tile(D), pl.BlockSpec((8, D), lambda i: (0, 0))],
        compiler_params=pltpu.CompilerParams(dimension_semantics=("arbitrary",), vmem_limit_bytes=VMEM_LIMIT),
    )(xhat1, target, mod, ln, w_mi, w_mo)


def _b1(dx1, xhat1, rstd1, x, mix, z, mod, ln, w_out, w_in, conv_w, w_pool, pool_scale, tm):
    T, D = x.shape
    ZW = w_in.shape[1]
    CC = ZW // 4
    nt = T // tm
    hb = tm // HALO

    def body(dx1_ref, xh1_ref, rstd_ref, x_ref, mix_ref, z_ref, zh_ref, mod_ref, ln_ref, wout_ref, win_ref, cw_ref, wp_ref, ps_ref,
             dx_ref, dmix_ref, ycat_ref, dz_ref, acc_ref, gcw_ref, gwp_ref, cv_s, vp_s, e_s, q_s):
        i = pl.program_id(0)
        j = nt - 1 - i

        @pl.when(i == 0)
        def _():
            acc_ref[...] = jnp.zeros((8, D), F32)
            gcw_ref[...] = jnp.zeros((8, CC), F32)
            gwp_ref[...] = jnp.zeros(gwp_ref.shape, F32)
            e_s[tm : tm + HALO, :] = jnp.zeros((HALO, CC), F32)
            q_s[tm : tm + HALO, :] = jnp.zeros((HALO, CC), F32)

        sh1, sc1, g1 = mod_ref[0:1, :], mod_ref[1:2, :], mod_ref[2:3, :]
        dx1 = dx1_ref[...]
        xhat1 = xh1_ref[...]
        acc_ref[0:1, :] += _colsum(dx1 * xhat1)
        acc_ref[1:2, :] += _colsum(dx1)
        dr1 = _ln_bwd(dx1 * ln_ref[0:1, :], xhat1, rstd_ref[...])
        acc_ref[4:5, :] += _colsum(dr1 * mix_ref[...])
        dmix = ((1.0 + g1) * dr1).astype(BF16)
        dmix_ref[...] = dmix
        dycat = lax.dot_general(dmix, wout_ref[...], NT, preferred_element_type=F32)

        z = z_ref[...].astype(F32)
        zh = zh_ref[...].astype(F32) * jnp.where(j > 0, 1.0, 0.0)
        gb, gc, vc, vp = z[:, 0:CC], z[:, CC : 2 * CC], z[:, 2 * CC : 3 * CC], z[:, 3 * CC : 4 * CC]
        cv = gc * vc
        cv_s[0:HALO, :] = zh[:, CC : 2 * CC] * zh[:, 2 * CC : 3 * CC]
        cv_s[HALO : HALO + tm, :] = cv
        vp_s[0:HALO, :] = zh[:, 3 * CC : 4 * CC]
        vp_s[HALO : HALO + tm, :] = vp
        cv_ext = cv_s[...]
        cv_m2 = pltpu.roll(cv_ext, 2, 0)[HALO : HALO + tm, :]
        cv_m1 = pltpu.roll(cv_ext, 1, 0)[HALO : HALO + tm, :]
        w0, w1, w2 = cw_ref[0:1, :], cw_ref[1:2, :], cw_ref[2:3, :]
        conv = w0 * cv_m2 + w1 * cv_m1 + w2 * cv
        dyc = dycat[:, 0:CC]
        e = dyc * gb
        e_s[0:tm, :] = e
        e_ext = e_s[...]
        dcv = w2 * e + w1 * pltpu.roll(e_ext, tm + HALO - 1, 0)[0:tm, :] + w0 * pltpu.roll(e_ext, tm + HALO - 2, 0)[0:tm, :]
        gcw_ref[0:1, :] += _colsum(e * cv_m2)
        gcw_ref[1:2, :] += _colsum(e * cv_m1)
        gcw_ref[2:3, :] += _colsum(e * cv)
        y_parts = [gb * conv]
        dz_parts = [dyc * conv, dcv * vc, dcv * gc]

        row = j * tm + lax.broadcasted_iota(jnp.int32, (tm, 1), 0)
        feats, inv_cnts = _pool_features(vp, vp_s, row, tm)
        gps_parts, dps = [], []
        for g in range(len(POOL_WINDOWS)):
            cols = slice(128 * g, 128 * g + 128)
            p = feats[g].astype(BF16)
            scale = ps_ref[0:1, cols]
            wp = wp_ref[g].astype(BF16)
            pw = jnp.dot(p, wp, preferred_element_type=F32)
            y_parts.append(pw * scale)
            dyp = dycat[:, CC + 128 * g : CC + 128 * g + 128]
            gps_parts.append(_colsum(dyp * pw))
            dpw = (dyp * scale).astype(BF16)
            gwp_ref[g] += lax.dot_general(p, dpw, TN, preferred_element_type=F32)
            dp = lax.dot_general(dpw, wp, NT, preferred_element_type=F32)
            q_s[0:tm, cols] = dp * inv_cnts[g]
            dps.append(dp)
        sq = _window_sums(q_s[...], tm, causal=False)
        dz_parts += [sq[g] - dps[g] for g in range(len(POOL_WINDOWS))]
        gcw_ref[3:4, :] += jnp.concatenate(gps_parts, axis=1)
        ycat_ref[...] = jnp.concatenate(y_parts, axis=1).astype(BF16)
        dz = jnp.concatenate(dz_parts, axis=1).astype(BF16)
        dz_ref[...] = dz
        dh = lax.dot_general(dz, win_ref[...], NT, preferred_element_type=F32)
        acc_ref[2:3, :] += _colsum(dh)
        acc_ref[3:4, :] += _colsum(dh * x_ref[...])
        dx_ref[...] = DEEPNORM_ALPHA * dr1 + dh * (1.0 + sc1)
        e_s[tm : tm + HALO, :] = e_s[0:HALO, :]
        q_s[tm : tm + HALO, :] = q_s[0:HALO, :]

    tile = lambda w: pl.BlockSpec((tm, w), lambda i: (nt - 1 - i, 0))
    halo = pl.BlockSpec((HALO, ZW), lambda i: (jnp.maximum((nt - 1 - i) * hb - 1, 0), 0))
    fixed = lambda shape: pl.BlockSpec(shape, lambda i: (0,) * len(shape))
    return pl.pallas_call(
        body,
        name="b1",
        grid=(nt,),
        out_shape=[
            jax.ShapeDtypeStruct((T, D), F32),
            jax.ShapeDtypeStruct((T, D), BF16),
            jax.ShapeDtypeStruct((T, D), BF16),
            jax.ShapeDtypeStruct((T, ZW), BF16),
            jax.ShapeDtypeStruct((8, D), F32),
            jax.ShapeDtypeStruct((8, CC), F32),
            jax.ShapeDtypeStruct(w_pool.shape, F32),
        ],
        in_specs=[tile(D), tile(D), tile(1), tile(D), tile(D), tile(ZW), halo] + [VMEM_SPEC] * 7,
        out_specs=[tile(D), tile(D), tile(D), tile(ZW), fixed((8, D)), fixed((8, CC)), fixed(w_pool.shape)],
        scratch_shapes=[
            pltpu.VMEM((HALO + tm, CC), F32),
            pltpu.VMEM((HALO + tm, CC), F32),
            pltpu.VMEM((tm + HALO, CC), F32),
            pltpu.VMEM((tm + HALO, CC), F32),
        ],
        compiler_params=pltpu.CompilerParams(dimension_semantics=("arbitrary",), vmem_limit_bytes=VMEM_LIMIT),
    )(dx1, xhat1, rstd1, x, mix, z, z, mod, ln, w_out, w_in, conv_w, w_pool, pool_scale)


def _wgrad(a, b, bk, n_groups, bt, name, owners=None, scatter=None, gather=()):
    T, K = a.shape
    N = b.shape[1]
    nk, nt, ng = K // bk, T // bt, N // n_groups
    nc = min(512, ng)
    ns, ngat = (0 if scatter is None else 1), len(gather)
    n_steps = nk * n_groups * nt
    mid_step = min(1, n_steps - 1)
    fwd_steps = [min(2 * (j + 1), n_steps - 1) for j in range(3)]

    def body(*refs):
        a_ref, b_ref = refs[0], refs[1]
        g_ins = refs[2 + ns : 2 + ns + ngat]
        outs = refs[2 + ns + ngat :]
        o_ref, g_outs = outs[0], outs[1 + ns : 1 + ns + ngat]
        scr = outs[1 + ns + ngat :]
        acc = scr[0]
        if ns:
            s_hbm, s_recv, s_scr = refs[2], outs[1], scr[1 : 1 + N_SCATTER_SCRATCH]
        g_sems = scr[1 + N_SCATTER_SCRATCH * ns :]
        kk, gg, t = pl.program_id(0), pl.program_id(1), pl.program_id(2)
        step = (kk * n_groups + gg) * nt + t

        if ngat:

            @pl.when(step == 0)
            def _():
                _gather_start(g_ins, g_outs, g_sems)

            for j in range(3):

                @pl.when(step == fwd_steps[j])
                def _(j=j):
                    _gather_forward(g_ins, g_outs, g_sems, j)

        if ns:

            @pl.when(step == 0)
            def _():
                _scatter_start(s_hbm, s_scr)

            @pl.when(step == mid_step)
            def _():
                _scatter_middle(s_hbm, s_recv, s_scr)

        @pl.when(t == 0)
        def _():
            acc[...] = jnp.zeros(acc.shape, F32)

        at = a_ref[...].T
        for c in range(ng // nc):
            cs = slice(c * nc, (c + 1) * nc)
            acc[:, cs] += jnp.dot(at, b_ref[:, cs], preferred_element_type=F32)

        @pl.when(t == nt - 1)
        def _():
            if owners is None:
                o_ref[...] = acc[...].astype(BF16)
            else:
                per = N // owners
                for o in range(ng // per):
                    o_ref[o] = acc[:, o * per : (o + 1) * per].astype(BF16)

        if ns:

            @pl.when(step == n_steps - 1)
            def _():
                _scatter_finish(s_hbm, s_recv, s_scr)

        if ngat:

            @pl.when(step == n_steps - 1)
            def _():
                _gather_finish(g_ins, g_outs, g_sems)

    if owners is None:
        out_shape = [jax.ShapeDtypeStruct((K, N), BF16)]
        out_specs = [pl.BlockSpec((bk, ng), lambda k, g, t: (k, g))]
    else:
        assert bk == K
        per = N // owners
        out_shape = [jax.ShapeDtypeStruct((owners, K, per), BF16)]
        out_specs = [pl.BlockSpec((ng // per, K, per), lambda k, g, t: (g, 0, 0))]
    ins, in_specs = [a, b], [pl.BlockSpec((bt, bk), lambda k, g, t: (t, k)), pl.BlockSpec((bt, ng), lambda k, g, t: (t, g))]
    scratch = [pltpu.VMEM((bk, ng), F32)]
    if ns:
        ins.append(scatter)
        in_specs.append(ANY_SPEC)
        out_shape.append(_scatter_out_shape(scatter))
        out_specs.append(ANY_SPEC)
        scratch += _scatter_scratch(*scatter.shape[1:])
    if ngat:
        ins += list(gather)
        in_specs += [ANY_SPEC] * ngat
        out_shape += _gather_out_shape(gather, [False] * ngat)
        out_specs += [ANY_SPEC] * ngat
        scratch += _gather_scratch(ngat)
    outs = pl.pallas_call(
        body,
        name=name,
        grid=(nk, n_groups, nt),
        out_shape=out_shape,
        in_specs=in_specs,
        out_specs=out_specs,
        scratch_shapes=scratch,
        compiler_params=pltpu.CompilerParams(dimension_semantics=("arbitrary", "arbitrary", "arbitrary"), vmem_limit_bytes=VMEM_LIMIT),
    )(*ins)
    return outs if ns + ngat else outs[0]


def _wgrad_pair(a0, b0, a1, b1, bt, name, owners, scatter, gather):
    T, K0 = a0.shape
    N0 = b0.shape[1]
    K1, N1 = a1.shape[1], b1.shape[1]
    nt = T // bt
    nc = 512
    per = N0 // owners
    ngat = len(gather)
    n_steps = 2 * nt
    fwd_steps = [min(2 * (j + 1), n_steps - 1) for j in range(3)]

    def body(*refs):
        a0_ref, b0_ref, a1_ref, b1_ref, s_hbm = refs[:5]
        g_ins = refs[5 : 5 + ngat]
        o0_ref, o1_ref, s_recv = refs[5 + ngat : 8 + ngat]
        g_outs = refs[8 + ngat : 8 + 2 * ngat]
        scr = refs[8 + 2 * ngat :]
        acc0, acc1 = scr[0], scr[1]
        s_scr, g_sems = scr[2 : 2 + N_SCATTER_SCRATCH], scr[2 + N_SCATTER_SCRATCH :]
        p, t = pl.program_id(0), pl.program_id(1)
        step = p * nt + t

        @pl.when(step == 0)
        def _():
            _scatter_start(s_hbm, s_scr)
            _gather_start(g_ins, g_outs, g_sems)

        @pl.when(step == 1)
        def _():
            _scatter_middle(s_hbm, s_recv, s_scr)

        for j in range(3):

            @pl.when(step == fwd_steps[j])
            def _(j=j):
                _gather_forward(g_ins, g_outs, g_sems, j)

        def accumulate(a_ref, b_ref, acc, n_cols):
            @pl.when(t == 0)
            def _():
                acc[...] = jnp.zeros(acc.shape, F32)

            at = a_ref[...].T
            for c in range(n_cols // nc):
                cs = slice(c * nc, (c + 1) * nc)
                acc[:, cs] += jnp.dot(at, b_ref[:, cs], preferred_element_type=F32)

        @pl.when(p == 0)
        def _():
            accumulate(a0_ref, b0_ref, acc0, N0)

            @pl.when(t == nt - 1)
            def _():
                for o in range(owners):
                    o0_ref[o] = acc0[:, o * per : (o + 1) * per].astype(BF16)

        @pl.when(p == 1)
        def _():
            accumulate(a1_ref, b1_ref, acc1, N1)

            @pl.when(t == nt - 1)
            def _():
                o1_ref[...] = acc1[...].astype(BF16)

        @pl.when(step == n_steps - 1)
        def _():
            _scatter_finish(s_hbm, s_recv, s_scr)
            _gather_finish(g_ins, g_outs, g_sems)

    first = lambda w: pl.BlockSpec((bt, w), lambda p, t: (jnp.where(p == 0, t, nt - 1), 0))
    second = lambda w: pl.BlockSpec((bt, w), lambda p, t: (jnp.where(p == 1, t, 0), 0))
    return pl.pallas_call(
        body,
        name=name,
        grid=(2, nt),
        out_shape=[jax.ShapeDtypeStruct((owners, K0, per), BF16), jax.ShapeDtypeStruct((K1, N1), BF16), _scatter_out_shape(scatter)]
        + _gather_out_shape(gather, [False] * ngat),
        in_specs=[first(K0), first(N0), second(K1), second(N1), ANY_SPEC] + [ANY_SPEC] * ngat,
        out_specs=[pl.BlockSpec((owners, K0, per), lambda p, t: (0, 0, 0)), pl.BlockSpec((K1, N1), lambda p, t: (0, 0)), ANY_SPEC]
        + [ANY_SPEC] * ngat,
        scratch_shapes=[pltpu.VMEM((K0, N0), F32), pltpu.VMEM((K1, N1), F32)]
        + _scatter_scratch(*scatter.shape[1:]) + _gather_scratch(ngat),
        compiler_params=pltpu.CompilerParams(dimension_semantics=("arbitrary", "arbitrary"), vmem_limit_bytes=VMEM_LIMIT),
    )(a0, b0, a1, b1, scatter, *gather)


def _small_grads(acc1_all, acc2_all, gcw_all, gwp_all, cond_t, my_slot, w_ada, m_w_ada, v_w_ada):
    D = acc1_all.shape[2]
    w_cols = w_ada.shape[2]
    n_chunk = D // 128
    q_mine = w_cols // 128

    def total(ref, r):
        s = ref[0, r : r + 1, :]
        for k in range(1, N_DEV):
            s = s + ref[k, r : r + 1, :]
        return s

    def body(slot_ref, a1_ref, a2_ref, gcw_ref, gwp_ref, ct_ref, w_ref, m_ref, v_ref,
             gb_ref, gw_ref, gln_ref, gcwo_ref, gwpo_ref, loss_ref, dw_ref, nm_ref, nv_ref, dm_s):
        loss_ref[...] = total(a2_ref, 5)
        for s, (ref, r) in enumerate([(a1_ref, 2), (a1_ref, 3), (a1_ref, 4), (a2_ref, 2), (a2_ref, 3), (a2_ref, 4)]):
            gb_ref[0:1, s * D : (s + 1) * D] = total(ref, r)
            for k in range(N_DEV):
                row = ref[k, r : r + 1, :]
                for qq in range(n_chunk):
                    dm_s[s * n_chunk + qq, k : k + 1, :] = row[:, 128 * qq : 128 * qq + 128]
        gln_ref[0:1, :] = total(a1_ref, 0)
        gln_ref[1:2, :] = total(a1_ref, 1)
        gln_ref[2:3, :] = total(a2_ref, 0)
        gln_ref[3:4, :] = total(a2_ref, 1)
        gcwo_ref[...] = jnp.zeros(gcwo_ref.shape, F32)
        for r in range(4):
            gcwo_ref[r : r + 1, :] = total(gcw_ref, r)
        wp = gwp_ref[0]
        for k in range(1, N_DEV):
            wp = wp + gwp_ref[k]
        gwpo_ref[0] = wp
        ct = ct_ref[...]
        cond_t = ct * jax.nn.sigmoid(ct)
        q0 = slot_ref[0] * q_mine
        for q in range(q_mine):
            dm = dm_s[q0 + q]
            out = cond_t[:, 0:1] * dm[0:1, :]
            for k in range(1, N_DEV):
                out = out + cond_t[:, k : k + 1] * dm[k : k + 1, :]
            cols = slice(128 * q, 128 * q + 128)
            gw_ref[0, :, cols] = out
            delta, nm, nv = _adamw_math(w_ref[0, :, cols], out, m_ref[0, :, cols], v_ref[0, :, cols])
            dw_ref[0, :, cols] = delta
            nm_ref[0, :, cols] = nm
            nv_ref[0, :, cols] = nv

    CC = gcw_all.shape[2]
    w_like = jax.ShapeDtypeStruct(w_ada.shape, F32)
    return pl.pallas_call(
        body,
        name="small_grads",
        out_shape=[
            jax.ShapeDtypeStruct((1, 6 * D), F32),
            w_like,
            jax.ShapeDtypeStruct((4, D), F32),
            jax.ShapeDtypeStruct((8, CC), F32),
            jax.ShapeDtypeStruct((1, *gwp_all.shape[1:]), F32),
            jax.ShapeDtypeStruct((1, D), F32),
            w_like,
            w_like,
            w_like,
        ],
        in_specs=[pl.BlockSpec(memory_space=pltpu.SMEM)] + [VMEM_SPEC] * 8,
        out_specs=[VMEM_SPEC] * 9,
        scratch_shapes=[pltpu.VMEM((6 * n_chunk, N_DEV, 128), F32)],
        compiler_params=pltpu.CompilerParams(vmem_limit_bytes=VMEM_LIMIT),
    )(my_slot, acc1_all, acc2_all, gcw_all, gwp_all, cond_t, w_ada, m_w_ada, v_w_ada)


def kernel(x, c, w_ada, b_ada, w_in, conv_w, w_pool, pool_scale, w_out, ln1_g, ln1_b, w_mlp_in, w_mlp_out, ln2_g, ln2_b, loss_target, m_w_ada, m_b_ada, m_w_in, m_conv_w, m_w_pool, m_pool_scale, m_w_out, m_ln1_g, m_ln1_b, m_w_mlp_in, m_w_mlp_out, m_ln2_g, m_ln2_b, v_w_ada, v_b_ada, v_w_in, v_conv_w, v_w_pool, v_pool_scale, v_w_out, v_ln1_g, v_ln1_b, v_w_mlp_in, v_w_mlp_out, v_ln2_g, v_ln2_b):
    T, D = x.shape[1], x.shape[2]
    H = w_mlp_out.shape[1] * N_DEV
    ZW = w_in.shape[2] * N_DEV
    CC = ZW // 4
    tm = min(512, T // 2)
    bt = min(1024, T)
    ax, ay, ac = _my_place()
    me = _slot(ax, ay, ac)

    w_cols = w_ada.shape[2]
    b_mine = lax.dynamic_slice(b_ada, (0, me * w_cols), (1, w_cols))
    w_in_f, w_out_g, cw_g, c_g, mod_g = _prologue(w_in, w_out, conv_w[0], c, w_ada, b_mine)
    w_out_f = w_out_g.reshape(D, D)
    conv_w_f = jnp.transpose(cw_g, (1, 0, 2)).reshape(conv_w.shape[1], CC)
    c_all = c_g.reshape(N_DEV, D)
    mod = lax.dynamic_index_in_dim(mod_g, me, axis=1, keepdims=False).reshape(6, D)

    ln = jnp.concatenate([ln1_g, ln1_b, ln2_g, ln2_b], axis=0)
    xs, target = x[0], loss_target[0]

    z, h, xhat1, rstd1, mix, w_mi_f, w_mo_g = _f1(
        xs, mod, w_in_f, conv_w_f, w_pool[0], pool_scale, w_out_f, tm,
        [w_mlp_in, w_mlp_out], [True, False])
    dx1, h2, a, du, df, acc2 = _fb2(xhat1, target, mod, ln, w_mi_f, w_mo_g.reshape(H, D), tm)

    grad_x, dmix, ycat, dz, acc1, gcw, gwp = _b1(
        dx1, xhat1, rstd1, xs, mix, z, mod, ln, w_out_f, w_in_f, conv_w_f, w_pool[0], pool_scale, tm)

    gp_mo = _wgrad(a, df, D, 1, min(2 * bt, T), "wgrad_mlp_out").reshape(N_DEV, H // N_DEV, D)
    gp_mi, rv_mo = _wgrad(h2, du, D, 2, min(2 * bt, T), "wgrad_mlp_in", owners=N_DEV, scatter=gp_mo)
    gp_in, gp_out, rv_mi, acc1_g, acc2_g, gcw_g, gwp_g = _wgrad_pair(
        h, dz, ycat, dmix, min(bt // 2, T), "wgrad_in_out", N_DEV, gp_mi, [acc1, acc2, gcw, gwp])
    rv_out, rv_in = _reduce_scatter([gp_out.reshape(N_DEV, D // N_DEV, D), gp_in], "scatter_w_in_out")
    g_b_ada, g_w_ada, g_ln, g_cw, g_w_pool, loss_row, *u_w_ada = _small_grads(
        acc1_g, acc2_g, gcw_g, gwp_g, c_all.T, jnp.reshape(me, (1,)).astype(jnp.int32), w_ada, m_w_ada, v_w_ada)
    cc_mine = conv_w.shape[2]
    g_conv_w = lax.dynamic_slice(g_cw, (0, me * cc_mine), (conv_w.shape[1], cc_mine))[None]
    g_pool_scale = g_cw[3:4, :]
    g_ln1_g, g_ln1_b, g_ln2_g, g_ln2_b = g_ln[0:1], g_ln[1:2], g_ln[2:3], g_ln[3:4]

    small = _adamw_multi(
        [
            (b_ada, g_b_ada, m_b_ada, v_b_ada),
            (conv_w, g_conv_w, m_conv_w, v_conv_w),
            (w_pool, g_w_pool, m_w_pool, v_w_pool),
            (pool_scale, g_pool_scale, m_pool_scale, v_pool_scale),
            (ln1_g, g_ln1_g, m_ln1_g, v_ln1_g),
            (ln1_b, g_ln1_b, m_ln1_b, v_ln1_b),
            (ln2_g, g_ln2_g, m_ln2_g, v_ln2_g),
            (ln2_b, g_ln2_b, m_ln2_b, v_ln2_b),
        ],
        "adamw_small")
    u_b_ada, u_conv_w, u_w_pool, u_pool_scale, u_ln1_g, u_ln1_b, u_ln2_g, u_ln2_b = small

    g_w_mo, *u_w_mo = _sum_adamw(rv_mo, w_mlp_out, m_w_mlp_out, v_w_mlp_out, "sum_w_mlp_out")
    g_w_mi, *u_w_mi = _sum_adamw(rv_mi, w_mlp_in, m_w_mlp_in, v_w_mlp_in, "sum_w_mlp_in")
    g_w_in, *u_w_in = _sum_adamw(rv_in, w_in, m_w_in, v_w_in, "sum_w_in")
    g_w_out, *u_w_out = _sum_adamw(rv_out, w_out, m_w_out, v_w_out, "sum_w_out")

    grads = [g_w_ada, g_b_ada, g_w_in, g_conv_w, g_w_pool, g_pool_scale, g_w_out, g_ln1_g, g_ln1_b, g_w_mi, g_w_mo, g_ln2_g, g_ln2_b]
    updates = [u_w_ada, u_b_ada, u_w_in, u_conv_w, u_w_pool, u_pool_scale, u_w_out, u_ln1_g, u_ln1_b, u_w_mi, u_w_mo, u_ln2_g, u_ln2_b]
    deltas = [u[0] for u in updates]
    new_m = [u[1] for u in updates]
    new_v = [u[2] for u in updates]
    return (loss_row[0, 0], grad_x[None], *grads, *deltas, *new_m, *new_v)
```

```python
import jax
import jax.numpy as jnp
from jax import lax
from jax.experimental import pallas as pl
from jax.experimental.pallas import tpu as pltpu

F32 = jnp.float32
BF16 = jnp.bfloat16
MESH = pl.DeviceIdType.MESH
N_DEV = 8

LN_EPS = 1e-5
DEPTH = 1
DEEPNORM_ALPHA = (2.0 * DEPTH) ** 0.25
POOL_WINDOWS = (2, 4, 8, 16)
HALO = 16

ADAM_LR = 0.001
ADAM_B1 = 0.9
ADAM_B2 = 0.999
ADAM_EPS = 1e-08
ADAM_WD = 0.01
ADAM_STEP = 10

VMEM_LIMIT = 60 * 1024 * 1024

VMEM_SPEC = pl.BlockSpec(memory_space=pltpu.VMEM)
ANY_SPEC = pl.BlockSpec(memory_space=pl.ANY)

NT = (((1,), (1,)), ((), ()))
TN = (((0,), (0,)), ((), ()))


def _my_place():
    return lax.axis_index("x"), lax.axis_index("y"), lax.axis_index("c")


def _slot(x, y, c):
    return 4 * x + 2 * y + c


def _gather_place(ins, outs, a, slot):
    if len(outs[a].shape) == len(ins[a].shape):
        wb = ins[a].shape[1]
        return outs[a].at[:, pl.ds(pl.multiple_of(slot * wb, wb), wb)]
    return outs[a].at[slot]


def _gather_copy(ins, outs, sems, a, k, block, to, from_shard=False):
    send_sems, recv_sems, _ = sems
    dst = _gather_place(ins, outs, a, _slot(*block))
    return pltpu.make_async_remote_copy(
        src_ref=ins[a] if from_shard else dst,
        dst_ref=dst,
        send_sem=send_sems.at[7 * a + k],
        recv_sem=recv_sems.at[7 * a + k],
        device_id=to,
        device_id_type=MESH,
    )


def _gather_peers():
    x, y, c = _my_place()
    return (x, y, c), (x, y, 1 - c), [(1 - x, y), (x, 1 - y), (1 - x, 1 - y)]


def _gather_first(ins, outs, sems):
    me, sibling, chips = _gather_peers()
    first = []
    for a in range(len(ins)):
        first.append(_gather_copy(ins, outs, sems, a, 0, me, sibling, from_shard=True))
        first += [_gather_copy(ins, outs, sems, a, 1 + j, me, (*chip, me[2]), from_shard=True) for j, chip in enumerate(chips)]
    return first


def _gather_mine(ins, outs, sems, a):
    me, _, _ = _gather_peers()
    return pltpu.make_async_copy(ins[a], _gather_place(ins, outs, a, _slot(*me)), sems[2].at[a])


def _gather_start(ins, outs, sems):
    for a in range(len(ins)):
        _gather_mine(ins, outs, sems, a).start()
    for cp in _gather_first(ins, outs, sems):
        cp.start()


def _gather_forward(ins, outs, sems, j):
    me, sibling, chips = _gather_peers()
    for a in range(len(ins)):
        _gather_copy(ins, outs, sems, a, 1 + j, (*chips[j], me[2]), me).wait_recv()
        _gather_copy(ins, outs, sems, a, 4 + j, (*chips[j], me[2]), sibling).start()


def _gather_finish(ins, outs, sems):
    me, sibling, chips = _gather_peers()
    for a in range(len(ins)):
        _gather_copy(ins, outs, sems, a, 0, sibling, me).wait_recv()
        for j, chip in enumerate(chips):
            _gather_copy(ins, outs, sems, a, 4 + j, (*chip, 1 - me[2]), me).wait_recv()
    for cp in _gather_first(ins, outs, sems):
        cp.wait_send()
    for a in range(len(ins)):
        for j, chip in enumerate(chips):
            _gather_copy(ins, outs, sems, a, 4 + j, (*chip, me[2]), sibling).wait_send()
        _gather_mine(ins, outs, sems, a).wait()


def _gather_scratch(n):
    return [pltpu.SemaphoreType.DMA((7 * n,)), pltpu.SemaphoreType.DMA((7 * n,)), pltpu.SemaphoreType.DMA((n,))]


def _gather_out_shape(shards, by_cols):
    return [
        jax.ShapeDtypeStruct((s.shape[0], N_DEV * s.shape[1]) if cols else (N_DEV, *s.shape), s.dtype)
        for s, cols in zip(shards, by_cols)
    ]


N_CHIP = 4


def _scatter_scratch(rows, cols):
    block = pltpu.VMEM((N_CHIP, rows, cols), BF16)
    dma = pltpu.SemaphoreType.DMA
    return [block, block, block, dma((N_CHIP,)), dma((N_CHIP,)), dma((N_CHIP,)), dma((N_CHIP - 1,)), dma((N_CHIP - 1,)), dma]


def _scatter_pair_copies(g_hbm, scr):
    x, y, c = _my_place()
    mine, theirs, _, a_send, a_recv, load_sem = scr[:6]
    to_sibling = [
        pltpu.make_async_remote_copy(
            src_ref=g_hbm.at[2 * q + (1 - c)], dst_ref=theirs.at[q], send_sem=a_send.at[q], recv_sem=a_recv.at[q],
            device_id=(x, y, 1 - c), device_id_type=MESH)
        for q in range(N_CHIP)
    ]
    loads = [pltpu.make_async_copy(g_hbm.at[2 * q + c], mine.at[q], load_sem.at[q]) for q in range(N_CHIP)]
    return to_sibling, loads


def _scatter_sum_copies(recv, scr):
    x, y, c = _my_place()
    sums, b_send, b_recv, own_sem = scr[2], scr[6], scr[7], scr[8]
    q_me = 2 * x + y
    to_owner = [
        pltpu.make_async_remote_copy(
            src_ref=sums.at[2 * px + py], dst_ref=recv.at[q_me], send_sem=b_send.at[j], recv_sem=b_recv.at[j],
            device_id=(px, py, c), device_id_type=MESH)
        for j, (px, py) in enumerate([(1 - x, y), (x, 1 - y), (1 - x, 1 - y)])
    ]
    return to_owner, pltpu.make_async_copy(sums.at[q_me], recv.at[q_me], own_sem)


def _scatter_start(g_hbm, scr):
    to_sibling, loads = _scatter_pair_copies(g_hbm, scr)
    for cp in to_sibling + loads:
        cp.start()


def _scatter_middle(g_hbm, recv, scr):
    to_sibling, loads = _scatter_pair_copies(g_hbm, scr)
    for cp in to_sibling:
        cp.wait_recv()
    for cp in loads:
        cp.wait()
    mine, theirs, sums = scr[:3]

    def step(r, carry):
        rs = pl.ds(pl.multiple_of(r * ROW_CHUNK, ROW_CHUNK), ROW_CHUNK)
        for q in range(N_CHIP):
            sums[q, rs, :] = (mine[q, rs, :].astype(F32) + theirs[q, rs, :].astype(F32)).astype(BF16)
        return carry

    lax.fori_loop(0, mine.shape[1] // ROW_CHUNK, step, 0)
    to_owner, own = _scatter_sum_copies(recv, scr)
    for cp in to_owner + [own]:
        cp.start()


def _scatter_finish(g_hbm, recv, scr):
    to_sibling, _ = _scatter_pair_copies(g_hbm, scr)
    to_owner, own = _scatter_sum_copies(recv, scr)
    for cp in to_owner:
        cp.wait_recv()
    for cp in to_sibling + to_owner:
        cp.wait_send()
    own.wait()


def _scatter_out_shape(gparts):
    return jax.ShapeDtypeStruct((N_CHIP, *gparts.shape[1:]), gparts.dtype)


def _adamw_math(w, g, m, v):
    m = ADAM_B1 * m + (1.0 - ADAM_B1) * g
    v = ADAM_B2 * v + (1.0 - ADAM_B2) * (g * g)
    m_hat = m / (1.0 - ADAM_B1**ADAM_STEP)
    v_hat = v / (1.0 - ADAM_B2**ADAM_STEP)
    delta = -ADAM_LR * (m_hat / (jnp.sqrt(v_hat) + ADAM_EPS) + ADAM_WD * w)
    return delta, m, v


ROW_CHUNK = 64


ELEMS_PER_STEP = 128 * 1024


N_SCATTER_SCRATCH = 9


def _reduce_scatter(gparts, name):
    n = len(gparts)

    def body(*refs):
        g_hbm, recv = refs[:n], refs[n : 2 * n]
        scr = [refs[2 * n + k * N_SCATTER_SCRATCH : 2 * n + (k + 1) * N_SCATTER_SCRATCH] for k in range(n)]
        for k in range(n):
            _scatter_start(g_hbm[k], scr[k])
        for k in range(n):
            _scatter_middle(g_hbm[k], recv[k], scr[k])
        for k in range(n):
            _scatter_finish(g_hbm[k], recv[k], scr[k])

    return pl.pallas_call(
        body,
        name=name,
        out_shape=[_scatter_out_shape(g) for g in gparts],
        in_specs=[ANY_SPEC] * n,
        out_specs=[ANY_SPEC] * n,
        scratch_shapes=[s for g in gparts for s in _scatter_scratch(*g.shape[1:])],
        compiler_params=pltpu.CompilerParams(vmem_limit_bytes=VMEM_LIMIT),
    )(*gparts)


def _sum_adamw(parts, w, m, v, name):
    _, rows, cols = w.shape
    rb = rows
    while rb * cols > ELEMS_PER_STEP and rb % 16 == 0:
        rb //= 2

    def body(p_ref, w_ref, m_ref, v_ref, grad_ref, delta_ref, nm_ref, nv_ref):
        g = p_ref[0].astype(F32)
        for k in range(1, p_ref.shape[0]):
            g = g + p_ref[k].astype(F32)
        delta, nm, nv = _adamw_math(w_ref[0], g, m_ref[0], v_ref[0])
        grad_ref[0] = g
        delta_ref[0] = delta
        nm_ref[0] = nm
        nv_ref[0] = nv

    block = lambda lead: pl.BlockSpec((lead, rb, cols), lambda i: (0, i, 0))
    out = jax.ShapeDtypeStruct(w.shape, F32)
    return pl.pallas_call(
        body,
        name=name,
        grid=(rows // rb,),
        out_shape=[out] * 4,
        in_specs=[block(parts.shape[0])] + [block(1)] * 3,
        out_specs=[block(1)] * 4,
        compiler_params=pltpu.CompilerParams(dimension_semantics=("arbitrary",), vmem_limit_bytes=VMEM_LIMIT),
    )(parts, w, m, v)


def _adamw_multi(items, name):
    n = len(items)

    def body(*refs):
        ins, outs = refs[: 4 * n], refs[4 * n :]
        for a in range(n):
            w_ref, g_ref, m_ref, v_ref = ins[4 * a : 4 * a + 4]
            d_ref, nm_ref, nv_ref = outs[3 * a : 3 * a + 3]
            delta, nm, nv = _adamw_math(w_ref[...], g_ref[...], m_ref[...], v_ref[...])
            d_ref[...] = delta
            nm_ref[...] = nm
            nv_ref[...] = nv

    flat = [a for it in items for a in it]
    out_shape = [jax.ShapeDtypeStruct(it[0].shape, F32) for it in items for _ in range(3)]
    outs = pl.pallas_call(
        body,
        name=name,
        out_shape=out_shape,
        in_specs=[VMEM_SPEC] * (4 * n),
        out_specs=[VMEM_SPEC] * (3 * n),
        compiler_params=pltpu.CompilerParams(vmem_limit_bytes=VMEM_LIMIT),
    )(*flat)
    return [tuple(outs[3 * a : 3 * a + 3]) for a in range(n)]


def _prologue(w_in, w_out, conv_w, c, w_ada, b_mine):
    D = c.shape[1]
    wc = w_ada.shape[2]
    shards16 = [jax.ShapeDtypeStruct(w_in.shape[1:], BF16), jax.ShapeDtypeStruct(w_out.shape[1:], BF16)]

    def to_all(src, out, sems):
        x, y, c = _my_place()
        me = _slot(x, y, c)
        copies = [
            pltpu.make_async_remote_copy(
                src_ref=src, dst_ref=out.at[me], send_sem=sems[0].at[k - 1], recv_sem=sems[1].at[k - 1],
                device_id=(x ^ (k >> 2), y ^ ((k >> 1) & 1), c ^ (k & 1)), device_id_type=MESH)
            for k in range(1, N_DEV)
        ]
        return copies, pltpu.make_async_copy(src, out.at[me], sems[2].at[0])

    def start(copies, own):
        for cp in copies + [own]:
            cp.start()

    def finish(copies, own):
        for cp in copies:
            cp.wait()
        own.wait()

    def body(win_ref, wout_ref, cw_ref, c_ref, wada_ref, b_ref, win_g, wout_g, cw_g, c_g, mod_g, win16, wout16, c_s, mp_s, *sems):
        c_copies = to_all(c_ref, c_g, sems[3:6])
        start(*c_copies)
        win16[...] = win_ref[0].astype(BF16)
        wout16[...] = wout_ref[0].astype(BF16)
        w_ins, w_outs, w_sems = (win16, wout16, cw_ref), (win_g, wout_g, cw_g), sems[0:3]
        _gather_start(w_ins, w_outs, w_sems)
        finish(*c_copies)
        for k in range(N_DEV):
            c_s[k : k + 1, :] = c_g[k]
        cv = c_s[...]
        cond = cv * jax.nn.sigmoid(cv)
        mp_s[...] = jnp.dot(cond, wada_ref[0], precision=lax.Precision.HIGHEST, preferred_element_type=F32) + b_ref[...]
        m_copies = to_all(mp_s, mod_g, sems[6:9])
        start(*m_copies)
        for j in range(3):
            _gather_forward(w_ins, w_outs, w_sems, j)
        _gather_finish(w_ins, w_outs, w_sems)
        finish(*m_copies)

    return pl.pallas_call(
        body,
        name="prologue",
        out_shape=_gather_out_shape(shards16 + [conv_w], [True, False, False])
        + [jax.ShapeDtypeStruct((N_DEV, 1, D), F32), jax.ShapeDtypeStruct((N_DEV, N_DEV, wc), F32)],
        in_specs=[VMEM_SPEC, VMEM_SPEC, ANY_SPEC] + [VMEM_SPEC] * 3,
        out_specs=[ANY_SPEC] * 3 + [VMEM_SPEC] * 2,
        scratch_shapes=[pltpu.VMEM(s.shape, BF16) for s in shards16]
        + [pltpu.VMEM((N_DEV, D), F32), pltpu.VMEM((N_DEV, wc), F32)]
        + _gather_scratch(3) + _gather_scratch(1) + _gather_scratch(1),
        compiler_params=pltpu.CompilerParams(vmem_limit_bytes=VMEM_LIMIT),
    )(w_in, w_out, conv_w, c, w_ada, b_mine)


def _ln_fwd(r):
    mu = jnp.mean(r, axis=-1, keepdims=True)
    d = r - mu
    var = jnp.mean(d * d, axis=-1, keepdims=True)
    rstd = lax.rsqrt(var + LN_EPS)
    return d * rstd, rstd


def _ln_bwd(dxh, xhat, rstd):
    m1 = jnp.mean(dxh, axis=-1, keepdims=True)
    m2 = jnp.mean(dxh * xhat, axis=-1, keepdims=True)
    return rstd * (dxh - m1 - xhat * m2)


def _colsum(a):
    return jnp.sum(a, axis=0, keepdims=True)


def _window_sums(ext, tm, causal):
    n = ext.shape[0]
    lo = HALO if causal else 0
    s, out = ext, []
    for p in range(len(POOL_WINDOWS)):
        assert POOL_WINDOWS[p] == 2 ** (p + 1)
        k = 2**p
        s = s + pltpu.roll(s, k if causal else n - k, 0)
        out.append(s[lo : lo + tm, 0:128])
        if p + 1 < len(POOL_WINDOWS):
            s = s[:, 128:]
    return out


def _pool_features(vp, vp_s, row, tm):
    sums = _window_sums(vp_s[...], tm, causal=True)
    feats, inv_cnts = [], []
    for g, win in enumerate(POOL_WINDOWS):
        inv_cnt = 1.0 / jnp.minimum(row + 1, win).astype(F32)
        feats.append(sums[g] * inv_cnt - vp[:, 128 * g : 128 * g + 128])
        inv_cnts.append(inv_cnt)
    return feats, inv_cnts


def _f1(x, mod, w_in, conv_w, w_pool, pool_scale, w_out, tm, gather, by_cols):
    T, D = x.shape
    ZW = w_in.shape[1]
    CC = ZW // 4
    nt = T // tm
    ng = len(gather)
    fwd_steps = [max(nt - 3 + j, 0) for j in range(3)]

    shards16 = [jax.ShapeDtypeStruct(s.shape[1:], BF16) for s in gather]

    def body(*refs):
        x_ref, mod_ref, win_ref, cw_ref, wp_ref, ps_ref, wout_ref = refs[:7]
        g_f32 = refs[7 : 7 + ng]
        z_ref, h_ref, xhat_ref, rstd_ref, mix_ref = refs[7 + ng : 12 + ng]
        g_outs = refs[12 + ng : 12 + 2 * ng]
        cv_s, vp_s = refs[12 + 2 * ng : 14 + 2 * ng]
        g_ins = refs[14 + 2 * ng : 14 + 3 * ng]
        g_sems = refs[14 + 3 * ng :]
        i = pl.program_id(0)

        @pl.when(i == 0)
        def _():
            for src, dst in zip(g_f32, g_ins):
                dst[...] = src[0].astype(BF16)
            _gather_start(g_ins, g_outs, g_sems)
            cv_s[0:HALO, :] = jnp.zeros((HALO, CC), F32)
            vp_s[0:HALO, :] = jnp.zeros((HALO, CC), F32)

        xv = x_ref[...]
        sh1, sc1, g1 = mod_ref[0:1, :], mod_ref[1:2, :], mod_ref[2:3, :]
        h = (xv * (1.0 + sc1) + sh1).astype(BF16)
        h_ref[...] = h
        z = jnp.dot(h, win_ref[...], preferred_element_type=F32)
        z_ref[...] = z.astype(BF16)
        gb, gc, vc, vp = z[:, 0:CC], z[:, CC : 2 * CC], z[:, 2 * CC : 3 * CC], z[:, 3 * CC : 4 * CC]
        cv = gc * vc
        cv_s[HALO : HALO + tm, :] = cv
        vp_s[HALO : HALO + tm, :] = vp
        cv_ext = cv_s[...]
        cv_m2 = pltpu.roll(cv_ext, 2, 0)[HALO : HALO + tm, :]
        cv_m1 = pltpu.roll(cv_ext, 1, 0)[HALO : HALO + tm, :]
        conv = cw_ref[0:1, :] * cv_m2 + cw_ref[1:2, :] * cv_m1 + cw_ref[2:3, :] * cv
        parts = [gb * conv]
        row = i * tm + lax.broadcasted_iota(jnp.int32, (tm, 1), 0)
        feats, _ = _pool_features(vp, vp_s, row, tm)
        for g in range(len(POOL_WINDOWS)):
            pw = jnp.dot(feats[g].astype(BF16), wp_ref[g].astype(BF16), preferred_element_type=F32)
            parts.append(pw * ps_ref[0:1, 128 * g : 128 * g + 128])
        cv_s[0:HALO, :] = cv_s[tm : tm + HALO, :]
        vp_s[0:HALO, :] = vp_s[tm : tm + HALO, :]
        ycat = jnp.concatenate(parts, axis=1).astype(BF16)
        mix = jnp.dot(ycat, wout_ref[...], preferred_element_type=F32)
        mix_ref[...] = mix
        xhat, rstd = _ln_fwd(DEEPNORM_ALPHA * xv + (1.0 + g1) * mix)
        xhat_ref[...] = xhat
        rstd_ref[...] = rstd

        for j in range(3):

            @pl.when(i == fwd_steps[j])
            def _(j=j):
                _gather_forward(g_ins, g_outs, g_sems, j)

        @pl.when(i == nt - 1)
        def _():
            _gather_finish(g_ins, g_outs, g_sems)

    tile = lambda w: pl.BlockSpec((tm, w), lambda i: (i, 0))
    return pl.pallas_call(
        body,
        name="f1",
        grid=(nt,),
        out_shape=[
            jax.ShapeDtypeStruct((T, ZW), BF16),
            jax.ShapeDtypeStruct((T, D), BF16),
            jax.ShapeDtypeStruct((T, D), F32),
            jax.ShapeDtypeStruct((T, 1), F32),
            jax.ShapeDtypeStruct((T, D), F32),
        ]
        + _gather_out_shape(shards16, by_cols),
        in_specs=[tile(D)] + [VMEM_SPEC] * (6 + ng),
        out_specs=[tile(ZW), tile(D), tile(D), tile(1), tile(D)] + [ANY_SPEC] * ng,
        scratch_shapes=[pltpu.VMEM((HALO + tm, CC), F32), pltpu.VMEM((HALO + tm, CC), F32)]
        + [pltpu.VMEM(s.shape, BF16) for s in shards16]
        + _gather_scratch(ng),
        compiler_params=pltpu.CompilerParams(dimension_semantics=("arbitrary",), vmem_limit_bytes=VMEM_LIMIT),
    )(x, mod, w_in, conv_w, w_pool, pool_scale, w_out, *gather)


def _fb2(xhat1, target, mod, ln, w_mi, w_mo, tm):
    T, D = xhat1.shape
    H = w_mi.shape[1]
    hc = min(1024, H)
    nb = H // hc
    nt = T // tm

    def body(xh1_ref, t_ref, mod_ref, ln_ref, wmi_ref, wmo_ref, dx1_ref, h2_ref, a_ref, du_ref, df_ref, acc_ref):
        i = pl.program_id(0)

        @pl.when(i == 0)
        def _():
            acc_ref[...] = jnp.zeros((8, D), F32)

        sh2, sc2, g2 = mod_ref[3:4, :], mod_ref[4:5, :], mod_ref[5:6, :]
        x1 = xh1_ref[...] * ln_ref[0:1, :] + ln_ref[1:2, :]
        h2 = (x1 * (1.0 + sc2) + sh2).astype(BF16)
        h2_ref[...] = h2
        f = jnp.zeros((tm, D), F32)
        for k in range(nb):
            ks = slice(k * hc, (k + 1) * hc)
            r = jnp.maximum(jnp.dot(h2, wmi_ref[:, ks], preferred_element_type=F32), 0.0)
            du_ref[:, ks] = r.astype(BF16)
            a = (r * r).astype(BF16)
            a_ref[:, ks] = a
            f = f + jnp.dot(a, wmo_ref[ks, :], preferred_element_type=F32)
        xhat2, rstd2 = _ln_fwd(DEEPNORM_ALPHA * x1 + (1.0 + g2) * f)
        ln2_g = ln_ref[2:3, :]
        d = xhat2 * ln2_g + ln_ref[3:4, :] - t_ref[...]
        dr2 = _ln_bwd(d * (ln2_g * (1.0 / D)), xhat2, rstd2)
        df = ((1.0 + g2) * dr2).astype(BF16)
        df_ref[...] = df
        dh2 = jnp.zeros((tm, D), F32)
        for k in range(nb):
            ks = slice(k * hc, (k + 1) * hc)
            da = lax.dot_general(df, wmo_ref[ks, :], NT, preferred_element_type=F32)
            du = (da * (2.0 * du_ref[:, ks].astype(F32))).astype(BF16)
            du_ref[:, ks] = du
            dh2 = dh2 + lax.dot_general(du, wmi_ref[:, ks], NT, preferred_element_type=F32)
        dx1_ref[...] = DEEPNORM_ALPHA * dr2 + dh2 * (1.0 + sc2)
        acc_ref[0:1, :] += _colsum(d * xhat2) * (1.0 / D)
        acc_ref[1:2, :] += _colsum(d) * (1.0 / D)
        acc_ref[2:3, :] += _colsum(dh2)
        acc_ref[3:4, :] += _colsum(dh2 * x1)
        acc_ref[4:5, :] += _colsum(dr2 * f)
        acc_ref[5:6, :] += jnp.zeros((1, D), F32) + (0.5 / D) * jnp.sum(d * d)

    tile = lambda w: pl.BlockSpec((tm, w), lambda i: (i, 0))
    return pl.pallas_call(
        body,
        name="fb2",
        grid=(nt,),
        out_shape=[
            jax.ShapeDtypeStruct((T, D), F32),
            jax.ShapeDtypeStruct((T, D), BF16),
            jax.ShapeDtypeStruct((T, H), BF16),
            jax.ShapeDtypeStruct((T, H), BF16),
            jax.ShapeDtypeStruct((T, D), BF16),
            jax.ShapeDtypeStruct((8, D), F32),
        ],
        in_specs=[tile(D), tile(D)] + [VMEM_SPEC] * 4,
        out_specs=[tile(D), tile(D), tile(H), tile(H), tile(D), pl.BlockSpec((8, D), lambda i: (0, 0))],
        compiler_params=pltpu.CompilerParams(dimension_semantics=("arbitrary",), vmem_limit_bytes=VMEM_LIMIT),
    )(xhat1, target, mod, ln, w_mi, w_mo)


def _b1(dx1, xhat1, rstd1, x, mix, z, mod, ln, w_out, w_in, conv_w, w_pool, pool_scale, tm):
    T, D = x.shape
    ZW = w_in.shape[1]
    CC = ZW // 4
    nt = T // tm
    hb = tm // HALO

    def body(dx1_ref, xh1_ref, rstd_ref, x_ref, mix_ref, z_ref, zh_ref, mod_ref, ln_ref, wout_ref, win_ref, cw_ref, wp_ref, ps_ref,
             dx_ref, dmix_ref, ycat_ref, dz_ref, acc_ref, gcw_ref, gwp_ref, cv_s, vp_s, e_s, q_s):
        i = pl.program_id(0)
        j = nt - 1 - i

        @pl.when(i == 0)
        def _():
            acc_ref[...] = jnp.zeros((8, D), F32)
            gcw_ref[...] = jnp.zeros((8, CC), F32)
            gwp_ref[...] = jnp.zeros(gwp_ref.shape, F32)
            e_s[tm : tm + HALO, :] = jnp.zeros((HALO, CC), F32)
            q_s[tm : tm + HALO, :] = jnp.zeros((HALO, CC), F32)

        sh1, sc1, g1 = mod_ref[0:1, :], mod_ref[1:2, :], mod_ref[2:3, :]
        dx1 = dx1_ref[...]
        xhat1 = xh1_ref[...]
        acc_ref[0:1, :] += _colsum(dx1 * xhat1)
        acc_ref[1:2, :] += _colsum(dx1)
        dr1 = _ln_bwd(dx1 * ln_ref[0:1, :], xhat1, rstd_ref[...])
        acc_ref[4:5, :] += _colsum(dr1 * mix_ref[...])
        dmix = ((1.0 + g1) * dr1).astype(BF16)
        dmix_ref[...] = dmix
        dycat = lax.dot_general(dmix, wout_ref[...], NT, preferred_element_type=F32)

        z = z_ref[...].astype(F32)
        zh = zh_ref[...].astype(F32) * jnp.where(j > 0, 1.0, 0.0)
        gb, gc, vc, vp = z[:, 0:CC], z[:, CC : 2 * CC], z[:, 2 * CC : 3 * CC], z[:, 3 * CC : 4 * CC]
        cv = gc * vc
        cv_s[0:HALO, :] = zh[:, CC : 2 * CC] * zh[:, 2 * CC : 3 * CC]
        cv_s[HALO : HALO + tm, :] = cv
        vp_s[0:HALO, :] = zh[:, 3 * CC : 4 * CC]
        vp_s[HALO : HALO + tm, :] = vp
        cv_ext = cv_s[...]
        cv_m2 = pltpu.roll(cv_ext, 2, 0)[HALO : HALO + tm, :]
        cv_m1 = pltpu.roll(cv_ext, 1, 0)[HALO : HALO + tm, :]
        w0, w1, w2 = cw_ref[0:1, :], cw_ref[1:2, :], cw_ref[2:3, :]
        conv = w0 * cv_m2 + w1 * cv_m1 + w2 * cv
        dyc = dycat[:, 0:CC]
        e = dyc * gb
        e_s[0:tm, :] = e
        e_ext = e_s[...]
        dcv = w2 * e + w1 * pltpu.roll(e_ext, tm + HALO - 1, 0)[0:tm, :] + w0 * pltpu.roll(e_ext, tm + HALO - 2, 0)[0:tm, :]
        gcw_ref[0:1, :] += _colsum(e * cv_m2)
        gcw_ref[1:2, :] += _colsum(e * cv_m1)
        gcw_ref[2:3, :] += _colsum(e * cv)
        y_parts = [gb * conv]
        dz_parts = [dyc * conv, dcv * vc, dcv * gc]

        row = j * tm + lax.broadcasted_iota(jnp.int32, (tm, 1), 0)
        feats, inv_cnts = _pool_features(vp, vp_s, row, tm)
        gps_parts, dps = [], []
        for g in range(len(POOL_WINDOWS)):
            cols = slice(128 * g, 128 * g + 128)
            p = feats[g].astype(BF16)
            scale = ps_ref[0:1, cols]
            wp = wp_ref[g].astype(BF16)
            pw = jnp.dot(p, wp, preferred_element_type=F32)
            y_parts.append(pw * scale)
            dyp = dycat[:, CC + 128 * g : CC + 128 * g + 128]
            gps_parts.append(_colsum(dyp * pw))
            dpw = (dyp * scale).astype(BF16)
            gwp_ref[g] += lax.dot_general(p, dpw, TN, preferred_element_type=F32)
            dp = lax.dot_general(dpw, wp, NT, preferred_element_type=F32)
            q_s[0:tm, cols] = dp * inv_cnts[g]
            dps.append(dp)
        sq = _window_sums(q_s[...], tm, causal=False)
        dz_parts += [sq[g] - dps[g] for g in range(len(POOL_WINDOWS))]
        gcw_ref[3:4, :] += jnp.concatenate(gps_parts, axis=1)
        ycat_ref[...] = jnp.concatenate(y_parts, axis=1).astype(BF16)
        dz = jnp.concatenate(dz_parts, axis=1).astype(BF16)
        dz_ref[...] = dz
        dh = lax.dot_general(dz, win_ref[...], NT, preferred_element_type=F32)
        acc_ref[2:3, :] += _colsum(dh)
        acc_ref[3:4, :] += _colsum(dh * x_ref[...])
        dx_ref[...] = DEEPNORM_ALPHA * dr1 + dh * (1.0 + sc1)
        e_s[tm : tm + HALO, :] = e_s[0:HALO, :]
        q_s[tm : tm + HALO, :] = q_s[0:HALO, :]

    tile = lambda w: pl.BlockSpec((tm, w), lambda i: (nt - 1 - i, 0))
    halo = pl.BlockSpec((HALO, ZW), lambda i: (jnp.maximum((nt - 1 - i) * hb - 1, 0), 0))
    fixed = lambda shape: pl.BlockSpec(shape, lambda i: (0,) * len(shape))
    return pl.pallas_call(
        body,
        name="b1",
        grid=(nt,),
        out_shape=[
            jax.ShapeDtypeStruct((T, D), F32),
            jax.ShapeDtypeStruct((T, D), BF16),
            jax.ShapeDtypeStruct((T, D), BF16),
            jax.ShapeDtypeStruct((T, ZW), BF16),
            jax.ShapeDtypeStruct((8, D), F32),
            jax.ShapeDtypeStruct((8, CC), F32),
            jax.ShapeDtypeStruct(w_pool.shape, F32),
        ],
        in_specs=[tile(D), tile(D), tile(1), tile(D), tile(D), tile(ZW), halo] + [VMEM_SPEC] * 7,
        out_specs=[tile(D), tile(D), tile(D), tile(ZW), fixed((8, D)), fixed((8, CC)), fixed(w_pool.shape)],
        scratch_shapes=[
            pltpu.VMEM((HALO + tm, CC), F32),
            pltpu.VMEM((HALO + tm, CC), F32),
            pltpu.VMEM((tm + HALO, CC), F32),
            pltpu.VMEM((tm + HALO, CC), F32),
        ],
        compiler_params=pltpu.CompilerParams(dimension_semantics=("arbitrary",), vmem_limit_bytes=VMEM_LIMIT),
    )(dx1, xhat1, rstd1, x, mix, z, z, mod, ln, w_out, w_in, conv_w, w_pool, pool_scale)


def _wgrad(a, b, bk, n_groups, bt, name, owners=None, scatter=None, gather=()):
    T, K = a.shape
    N = b.shape[1]
    nk, nt, ng = K // bk, T // bt, N // n_groups
    nc = min(512, ng)
    ns, ngat = (0 if scatter is None else 1), len(gather)
    n_steps = nk * n_groups * nt
    mid_step = min(1, n_steps - 1)
    fwd_steps = [min(2 * (j + 1), n_steps - 1) for j in range(3)]

    def body(*refs):
        a_ref, b_ref = refs[0], refs[1]
        g_ins = refs[2 + ns : 2 + ns + ngat]
        outs = refs[2 + ns + ngat :]
        o_ref, g_outs = outs[0], outs[1 + ns : 1 + ns + ngat]
        scr = outs[1 + ns + ngat :]
        acc = scr[0]
        if ns:
            s_hbm, s_recv, s_scr = refs[2], outs[1], scr[1 : 1 + N_SCATTER_SCRATCH]
        g_sems = scr[1 + N_SCATTER_SCRATCH * ns :]
        kk, gg, t = pl.program_id(0), pl.program_id(1), pl.program_id(2)
        step = (kk * n_groups + gg) * nt + t

        if ngat:

            @pl.when(step == 0)
            def _():
                _gather_start(g_ins, g_outs, g_sems)

            for j in range(3):

                @pl.when(step == fwd_steps[j])
                def _(j=j):
                    _gather_forward(g_ins, g_outs, g_sems, j)

        if ns:

            @pl.when(step == 0)
            def _():
                _scatter_start(s_hbm, s_scr)

            @pl.when(step == mid_step)
            def _():
                _scatter_middle(s_hbm, s_recv, s_scr)

        @pl.when(t == 0)
        def _():
            acc[...] = jnp.zeros(acc.shape, F32)

        at = a_ref[...].T
        for c in range(ng // nc):
            cs = slice(c * nc, (c + 1) * nc)
            acc[:, cs] += jnp.dot(at, b_ref[:, cs], preferred_element_type=F32)

        @pl.when(t == nt - 1)
        def _():
            if owners is None:
                o_ref[...] = acc[...].astype(BF16)
            else:
                per = N // owners
                for o in range(ng // per):
                    o_ref[o] = acc[:, o * per : (o + 1) * per].astype(BF16)

        if ns:

            @pl.when(step == n_steps - 1)
            def _():
                _scatter_finish(s_hbm, s_recv, s_scr)

        if ngat:

            @pl.when(step == n_steps - 1)
            def _():
                _gather_finish(g_ins, g_outs, g_sems)

    if owners is None:
        out_shape = [jax.ShapeDtypeStruct((K, N), BF16)]
        out_specs = [pl.BlockSpec((bk, ng), lambda k, g, t: (k, g))]
    else:
        assert bk == K
        per = N // owners
        out_shape = [jax.ShapeDtypeStruct((owners, K, per), BF16)]
        out_specs = [pl.BlockSpec((ng // per, K, per), lambda k, g, t: (g, 0, 0))]
    ins, in_specs = [a, b], [pl.BlockSpec((bt, bk), lambda k, g, t: (t, k)), pl.BlockSpec((bt, ng), lambda k, g, t: (t, g))]
    scratch = [pltpu.VMEM((bk, ng), F32)]
    if ns:
        ins.append(scatter)
        in_specs.append(ANY_SPEC)
        out_shape.append(_scatter_out_shape(scatter))
        out_specs.append(ANY_SPEC)
        scratch += _scatter_scratch(*scatter.shape[1:])
    if ngat:
        ins += list(gather)
        in_specs += [ANY_SPEC] * ngat
        out_shape += _gather_out_shape(gather, [False] * ngat)
        out_specs += [ANY_SPEC] * ngat
        scratch += _gather_scratch(ngat)
    outs = pl.pallas_call(
        body,
        name=name,
        grid=(nk, n_groups, nt),
        out_shape=out_shape,
        in_specs=in_specs,
        out_specs=out_specs,
        scratch_shapes=scratch,
        compiler_params=pltpu.CompilerParams(dimension_semantics=("arbitrary", "arbitrary", "arbitrary"), vmem_limit_bytes=VMEM_LIMIT),
    )(*ins)
    return outs if ns + ngat else outs[0]


def _wgrad_pair(a0, b0, a1, b1, bt, name, owners, scatter, gather):
    T, K0 = a0.shape
    N0 = b0.shape[1]
    K1, N1 = a1.shape[1], b1.shape[1]
    nt = T // bt
    nc = 512
    per = N0 // owners
    ngat = len(gather)
    n_steps = 2 * nt
    fwd_steps = [min(2 * (j + 1), n_steps - 1) for j in range(3)]

    def body(*refs):
        a0_ref, b0_ref, a1_ref, b1_ref, s_hbm = refs[:5]
        g_ins = refs[5 : 5 + ngat]
        o0_ref, o1_ref, s_recv = refs[5 + ngat : 8 + ngat]
        g_outs = refs[8 + ngat : 8 + 2 * ngat]
        scr = refs[8 + 2 * ngat :]
        acc0, acc1 = scr[0], scr[1]
        s_scr, g_sems = scr[2 : 2 + N_SCATTER_SCRATCH], scr[2 + N_SCATTER_SCRATCH :]
        p, t = pl.program_id(0), pl.program_id(1)
        step = p * nt + t

        @pl.when(step == 0)
        def _():
            _scatter_start(s_hbm, s_scr)
            _gather_start(g_ins, g_outs, g_sems)

        @pl.when(step == 1)
        def _():
            _scatter_middle(s_hbm, s_recv, s_scr)

        for j in range(3):

            @pl.when(step == fwd_steps[j])
            def _(j=j):
                _gather_forward(g_ins, g_outs, g_sems, j)

        def accumulate(a_ref, b_ref, acc, n_cols):
            @pl.when(t == 0)
            def _():
                acc[...] = jnp.zeros(acc.shape, F32)

            at = a_ref[...].T
            for c in range(n_cols // nc):
                cs = slice(c * nc, (c + 1) * nc)
                acc[:, cs] += jnp.dot(at, b_ref[:, cs], preferred_element_type=F32)

        @pl.when(p == 0)
        def _():
            accumulate(a0_ref, b0_ref, acc0, N0)

            @pl.when(t == nt - 1)
            def _():
                for o in range(owners):
                    o0_ref[o] = acc0[:, o * per : (o + 1) * per].astype(BF16)

        @pl.when(p == 1)
        def _():
            accumulate(a1_ref, b1_ref, acc1, N1)

            @pl.when(t == nt - 1)
            def _():
                o1_ref[...] = acc1[...].astype(BF16)

        @pl.when(step == n_steps - 1)
        def _():
            _scatter_finish(s_hbm, s_recv, s_scr)
            _gather_finish(g_ins, g_outs, g_sems)

    first = lambda w: pl.BlockSpec((bt, w), lambda p, t: (jnp.where(p == 0, t, nt - 1), 0))
    second = lambda w: pl.BlockSpec((bt, w), lambda p, t: (jnp.where(p == 1, t, 0), 0))
    return pl.pallas_call(
        body,
        name=name,
        grid=(2, nt),
        out_shape=[jax.ShapeDtypeStruct((owners, K0, per), BF16), jax.ShapeDtypeStruct((K1, N1), BF16), _scatter_out_shape(scatter)]
        + _gather_out_shape(gather, [False] * ngat),
        in_specs=[first(K0), first(N0), second(K1), second(N1), ANY_SPEC] + [ANY_SPEC] * ngat,
        out_specs=[pl.BlockSpec((owners, K0, per), lambda p, t: (0, 0, 0)), pl.BlockSpec((K1, N1), lambda p, t: (0, 0)), ANY_SPEC]
        + [ANY_SPEC] * ngat,
        scratch_shapes=[pltpu.VMEM((K0, N0), F32), pltpu.VMEM((K1, N1), F32)]
        + _scatter_scratch(*scatter.shape[1:]) + _gather_scratch(ngat),
        compiler_params=pltpu.CompilerParams(dimension_semantics=("arbitrary", "arbitrary"), vmem_limit_bytes=VMEM_LIMIT),
    )(a0, b0, a1, b1, scatter, *gather)


def _small_grads(acc1_all, acc2_all, gcw_all, gwp_all, cond_t, my_slot, w_ada, m_w_ada, v_w_ada):
    D = acc1_all.shape[2]
    w_cols = w_ada.shape[2]
    n_chunk = D // 128
    q_mine = w_cols // 128

    def total(ref, r):
        s = ref[0, r : r + 1, :]
        for k in range(1, N_DEV):
            s = s + ref[k, r : r + 1, :]
        return s

    def body(slot_ref, a1_ref, a2_ref, gcw_ref, gwp_ref, ct_ref, w_ref, m_ref, v_ref,
             gb_ref, gw_ref, gln_ref, gcwo_ref, gwpo_ref, loss_ref, dw_ref, nm_ref, nv_ref, dm_s):
        loss_ref[...] = total(a2_ref, 5)
        for s, (ref, r) in enumerate([(a1_ref, 2), (a1_ref, 3), (a1_ref, 4), (a2_ref, 2), (a2_ref, 3), (a2_ref, 4)]):
            gb_ref[0:1, s * D : (s + 1) * D] = total(ref, r)
            for k in range(N_DEV):
                row = ref[k, r : r + 1, :]
                for qq in range(n_chunk):
                    dm_s[s * n_chunk + qq, k : k + 1, :] = row[:, 128 * qq : 128 * qq + 128]
        gln_ref[0:1, :] = total(a1_ref, 0)
        gln_ref[1:2, :] = total(a1_ref, 1)
        gln_ref[2:3, :] = total(a2_ref, 0)
        gln_ref[3:4, :] = total(a2_ref, 1)
        gcwo_ref[...] = jnp.zeros(gcwo_ref.shape, F32)
        for r in range(4):
            gcwo_ref[r : r + 1, :] = total(gcw_ref, r)
        wp = gwp_ref[0]
        for k in range(1, N_DEV):
            wp = wp + gwp_ref[k]
        gwpo_ref[0] = wp
        ct = ct_ref[...]
        cond_t = ct * jax.nn.sigmoid(ct)
        q0 = slot_ref[0] * q_mine
        for q in range(q_mine):
            dm = dm_s[q0 + q]
            out = cond_t[:, 0:1] * dm[0:1, :]
            for k in range(1, N_DEV):
                out = out + cond_t[:, k : k + 1] * dm[k : k + 1, :]
            cols = slice(128 * q, 128 * q + 128)
            gw_ref[0, :, cols] = out
            delta, nm, nv = _adamw_math(w_ref[0, :, cols], out, m_ref[0, :, cols], v_ref[0, :, cols])
            dw_ref[0, :, cols] = delta
            nm_ref[0, :, cols] = nm
            nv_ref[0, :, cols] = nv

    CC = gcw_all.shape[2]
    w_like = jax.ShapeDtypeStruct(w_ada.shape, F32)
    return pl.pallas_call(
        body,
        name="small_grads",
        out_shape=[
            jax.ShapeDtypeStruct((1, 6 * D), F32),
            w_like,
            jax.ShapeDtypeStruct((4, D), F32),
            jax.ShapeDtypeStruct((8, CC), F32),
            jax.ShapeDtypeStruct((1, *gwp_all.shape[1:]), F32),
            jax.ShapeDtypeStruct((1, D), F32),
            w_like,
            w_like,
            w_like,
        ],
        in_specs=[pl.BlockSpec(memory_space=pltpu.SMEM)] + [VMEM_SPEC] * 8,
        out_specs=[VMEM_SPEC] * 9,
        scratch_shapes=[pltpu.VMEM((6 * n_chunk, N_DEV, 128), F32)],
        compiler_params=pltpu.CompilerParams(vmem_limit_bytes=VMEM_LIMIT),
    )(my_slot, acc1_all, acc2_all, gcw_all, gwp_all, cond_t, w_ada, m_w_ada, v_w_ada)


def kernel(x, c, w_ada, b_ada, w_in, conv_w, w_pool, pool_scale, w_out, ln1_g, ln1_b, w_mlp_in, w_mlp_out, ln2_g, ln2_b, loss_target, m_w_ada, m_b_ada, m_w_in, m_conv_w, m_w_pool, m_pool_scale, m_w_out, m_ln1_g, m_ln1_b, m_w_mlp_in, m_w_mlp_out, m_ln2_g, m_ln2_b, v_w_ada, v_b_ada, v_w_in, v_conv_w, v_w_pool, v_pool_scale, v_w_out, v_ln1_g, v_ln1_b, v_w_mlp_in, v_w_mlp_out, v_ln2_g, v_ln2_b):
    T, D = x.shape[1], x.shape[2]
    H = w_mlp_out.shape[1] * N_DEV
    ZW = w_in.shape[2] * N_DEV
    CC = ZW // 4
    tm = min(512, T // 2)
    bt = min(1024, T)
    ax, ay, ac = _my_place()
    me = _slot(ax, ay, ac)

    w_cols = w_ada.shape[2]
    b_mine = lax.dynamic_slice(b_ada, (0, me * w_cols), (1, w_cols))
    w_in_f, w_out_g, cw_g, c_g, mod_g = _prologue(w_in, w_out, conv_w[0], c, w_ada, b_mine)
    w_out_f = w_out_g.reshape(D, D)
    conv_w_f = jnp.transpose(cw_g, (1, 0, 2)).reshape(conv_w.shape[1], CC)
    c_all = c_g.reshape(N_DEV, D)
    mod = lax.dynamic_index_in_dim(mod_g, me, axis=1, keepdims=False).reshape(6, D)

    ln = jnp.concatenate([ln1_g, ln1_b, ln2_g, ln2_b], axis=0)
    xs, target = x[0], loss_target[0]

    z, h, xhat1, rstd1, mix, w_mi_f, w_mo_g = _f1(
        xs, mod, w_in_f, conv_w_f, w_pool[0], pool_scale, w_out_f, tm,
        [w_mlp_in, w_mlp_out], [True, False])
    dx1, h2, a, du, df, acc2 = _fb2(xhat1, target, mod, ln, w_mi_f, w_mo_g.reshape(H, D), tm)

    grad_x, dmix, ycat, dz, acc1, gcw, gwp = _b1(
        dx1, xhat1, rstd1, xs, mix, z, mod, ln, w_out_f, w_in_f, conv_w_f, w_pool[0], pool_scale, tm)

    gp_mo = _wgrad(a, df, D, 1, min(2 * bt, T), "wgrad_mlp_out").reshape(N_DEV, H // N_DEV, D)
    gp_mi, rv_mo = _wgrad(h2, du, D, 2, min(2 * bt, T), "wgrad_mlp_in", owners=N_DEV, scatter=gp_mo)
    gp_in, gp_out, rv_mi, acc1_g, acc2_g, gcw_g, gwp_g = _wgrad_pair(
        h, dz, ycat, dmix, bt, "wgrad_in_out", N_DEV, gp_mi, [acc1, acc2, gcw, gwp])
    rv_out, rv_in = _reduce_scatter([gp_out.reshape(N_DEV, D // N_DEV, D), gp_in], "scatter_w_in_out")
    g_b_ada, g_w_ada, g_ln, g_cw, g_w_pool, loss_row, *u_w_ada = _small_grads(
        acc1_g, acc2_g, gcw_g, gwp_g, c_all.T, jnp.reshape(me, (1,)).astype(jnp.int32), w_ada, m_w_ada, v_w_ada)
    cc_mine = conv_w.shape[2]
    g_conv_w = lax.dynamic_slice(g_cw, (0, me * cc_mine), (conv_w.shape[1], cc_mine))[None]
    g_pool_scale = g_cw[3:4, :]
    g_ln1_g, g_ln1_b, g_ln2_g, g_ln2_b = g_ln[0:1], g_ln[1:2], g_ln[2:3], g_ln[3:4]

    small = _adamw_multi(
        [
            (b_ada, g_b_ada, m_b_ada, v_b_ada),
            (conv_w, g_conv_w, m_conv_w, v_conv_w),
            (w_pool, g_w_pool, m_w_pool, v_w_pool),
            (pool_scale, g_pool_scale, m_pool_scale, v_pool_scale),
            (ln1_g, g_ln1_g, m_ln1_g, v_ln1_g),
            (ln1_b, g_ln1_b, m_ln1_b, v_ln1_b),
            (ln2_g, g_ln2_g, m_ln2_g, v_ln2_g),
            (ln2_b, g_ln2_b, m_ln2_b, v_ln2_b),
        ],
        "adamw_small")
    u_b_ada, u_conv_w, u_w_pool, u_pool_scale, u_ln1_g, u_ln1_b, u_ln2_g, u_ln2_b = small

    g_w_mo, *u_w_mo = _sum_adamw(rv_mo, w_mlp_out, m_w_mlp_out, v_w_mlp_out, "sum_w_mlp_out")
    g_w_mi, *u_w_mi = _sum_adamw(rv_mi, w_mlp_in, m_w_mlp_in, v_w_mlp_in, "sum_w_mlp_in")
    g_w_in, *u_w_in = _sum_adamw(rv_in, w_in, m_w_in, v_w_in, "sum_w_in")
    g_w_out, *u_w_out = _sum_adamw(rv_out, w_out, m_w_out, v_w_out, "sum_w_out")

    grads = [g_w_ada, g_b_ada, g_w_in, g_conv_w, g_w_pool, g_pool_scale, g_w_out, g_ln1_g, g_ln1_b, g_w_mi, g_w_mo, g_ln2_g, g_ln2_b]
    updates = [u_w_ada, u_b_ada, u_w_in, u_conv_w, u_w_pool, u_pool_scale, u_w_out, u_ln1_g, u_ln1_b, u_w_mi, u_w_mo, u_ln2_g, u_ln2_b]
    deltas = [u[0] for u in updates]
    new_m = [u[1] for u in updates]
    new_v = [u[2] for u in updates]
    return (loss_row[0, 0], grad_x[None], *grads, *deltas, *new_m, *new_v)
```

```python
import jax
import jax.numpy as jnp
from jax import lax
from jax.experimental import pallas as pl
from jax.experimental.pallas import tpu as pltpu

F32 = jnp.float32
BF16 = jnp.bfloat16
MESH = pl.DeviceIdType.MESH
N_DEV = 8

LN_EPS = 1e-5
DEPTH = 1
DEEPNORM_ALPHA = (2.0 * DEPTH) ** 0.25
POOL_WINDOWS = (2, 4, 8, 16)
HALO = 16

ADAM_LR = 0.001
ADAM_B1 = 0.9
ADAM_B2 = 0.999
ADAM_EPS = 1e-08
ADAM_WD = 0.01
ADAM_STEP = 10

VMEM_LIMIT = 60 * 1024 * 1024

VMEM_SPEC = pl.BlockSpec(memory_space=pltpu.VMEM)
ANY_SPEC = pl.BlockSpec(memory_space=pl.ANY)

NT = (((1,), (1,)), ((), ()))
TN = (((0,), (0,)), ((), ()))


def _my_place():
    return lax.axis_index("x"), lax.axis_index("y"), lax.axis_index("c")


def _slot(x, y, c):
    return 4 * x + 2 * y + c


def _gather_place(ins, outs, a, slot):
    if len(outs[a].shape) == len(ins[a].shape):
        wb = ins[a].shape[1]
        return outs[a].at[:, pl.ds(pl.multiple_of(slot * wb, wb), wb)]
    return outs[a].at[slot]


def _gather_copy(ins, outs, sems, a, k, block, to, from_shard=False):
    send_sems, recv_sems, _ = sems
    dst = _gather_place(ins, outs, a, _slot(*block))
    return pltpu.make_async_remote_copy(
        src_ref=ins[a] if from_shard else dst,
        dst_ref=dst,
        send_sem=send_sems.at[7 * a + k],
        recv_sem=recv_sems.at[7 * a + k],
        device_id=to,
        device_id_type=MESH,
    )


def _gather_peers():
    x, y, c = _my_place()
    return (x, y, c), (x, y, 1 - c), [(1 - x, y), (x, 1 - y), (1 - x, 1 - y)]


def _gather_first(ins, outs, sems):
    me, sibling, chips = _gather_peers()
    first = []
    for a in range(len(ins)):
        first.append(_gather_copy(ins, outs, sems, a, 0, me, sibling, from_shard=True))
        first += [_gather_copy(ins, outs, sems, a, 1 + j, me, (*chip, me[2]), from_shard=True) for j, chip in enumerate(chips)]
    return first


def _gather_mine(ins, outs, sems, a):
    me, _, _ = _gather_peers()
    return pltpu.make_async_copy(ins[a], _gather_place(ins, outs, a, _slot(*me)), sems[2].at[a])


def _gather_start(ins, outs, sems):
    for a in range(len(ins)):
        _gather_mine(ins, outs, sems, a).start()
    for cp in _gather_first(ins, outs, sems):
        cp.start()


def _gather_forward(ins, outs, sems, j):
    me, sibling, chips = _gather_peers()
    for a in range(len(ins)):
        _gather_copy(ins, outs, sems, a, 1 + j, (*chips[j], me[2]), me).wait_recv()
        _gather_copy(ins, outs, sems, a, 4 + j, (*chips[j], me[2]), sibling).start()


def _gather_finish(ins, outs, sems):
    me, sibling, chips = _gather_peers()
    for a in range(len(ins)):
        _gather_copy(ins, outs, sems, a, 0, sibling, me).wait_recv()
        for j, chip in enumerate(chips):
            _gather_copy(ins, outs, sems, a, 4 + j, (*chip, 1 - me[2]), me).wait_recv()
    for cp in _gather_first(ins, outs, sems):
        cp.wait_send()
    for a in range(len(ins)):
        for j, chip in enumerate(chips):
            _gather_copy(ins, outs, sems, a, 4 + j, (*chip, me[2]), sibling).wait_send()
        _gather_mine(ins, outs, sems, a).wait()


def _gather_scratch(n):
    return [pltpu.SemaphoreType.DMA((7 * n,)), pltpu.SemaphoreType.DMA((7 * n,)), pltpu.SemaphoreType.DMA((n,))]


def _gather_out_shape(shards, by_cols):
    return [
        jax.ShapeDtypeStruct((s.shape[0], N_DEV * s.shape[1]) if cols else (N_DEV, *s.shape), s.dtype)
        for s, cols in zip(shards, by_cols)
    ]


N_CHIP = 4


def _scatter_scratch(rows, cols):
    block = pltpu.VMEM((N_CHIP, rows, cols), BF16)
    dma = pltpu.SemaphoreType.DMA
    return [block, block, block, dma((N_CHIP,)), dma((N_CHIP,)), dma((N_CHIP,)), dma((N_CHIP - 1,)), dma((N_CHIP - 1,)), dma]


def _scatter_pair_copies(g_hbm, scr):
    x, y, c = _my_place()
    mine, theirs, _, a_send, a_recv, load_sem = scr[:6]
    to_sibling = [
        pltpu.make_async_remote_copy(
            src_ref=g_hbm.at[2 * q + (1 - c)], dst_ref=theirs.at[q], send_sem=a_send.at[q], recv_sem=a_recv.at[q],
            device_id=(x, y, 1 - c), device_id_type=MESH)
        for q in range(N_CHIP)
    ]
    loads = [pltpu.make_async_copy(g_hbm.at[2 * q + c], mine.at[q], load_sem.at[q]) for q in range(N_CHIP)]
    return to_sibling, loads


def _scatter_sum_copies(recv, scr):
    x, y, c = _my_place()
    sums, b_send, b_recv, own_sem = scr[2], scr[6], scr[7], scr[8]
    q_me = 2 * x + y
    to_owner = [
        pltpu.make_async_remote_copy(
            src_ref=sums.at[2 * px + py], dst_ref=recv.at[q_me], send_sem=b_send.at[j], recv_sem=b_recv.at[j],
            device_id=(px, py, c), device_id_type=MESH)
        for j, (px, py) in enumerate([(1 - x, y), (x, 1 - y), (1 - x, 1 - y)])
    ]
    return to_owner, pltpu.make_async_copy(sums.at[q_me], recv.at[q_me], own_sem)


def _scatter_start(g_hbm, scr):
    to_sibling, loads = _scatter_pair_copies(g_hbm, scr)
    for cp in to_sibling + loads:
        cp.start()


def _scatter_middle(g_hbm, recv, scr):
    to_sibling, loads = _scatter_pair_copies(g_hbm, scr)
    for cp in to_sibling:
        cp.wait_recv()
    for cp in loads:
        cp.wait()
    mine, theirs, sums = scr[:3]

    def step(r, carry):
        rs = pl.ds(pl.multiple_of(r * ROW_CHUNK, ROW_CHUNK), ROW_CHUNK)
        for q in range(N_CHIP):
            sums[q, rs, :] = (mine[q, rs, :].astype(F32) + theirs[q, rs, :].astype(F32)).astype(BF16)
        return carry

    lax.fori_loop(0, mine.shape[1] // ROW_CHUNK, step, 0)
    to_owner, own = _scatter_sum_copies(recv, scr)
    for cp in to_owner + [own]:
        cp.start()


def _scatter_finish(g_hbm, recv, scr):
    to_sibling, _ = _scatter_pair_copies(g_hbm, scr)
    to_owner, own = _scatter_sum_copies(recv, scr)
    for cp in to_owner:
        cp.wait_recv()
    for cp in to_sibling + to_owner:
        cp.wait_send()
    own.wait()


def _scatter_out_shape(gparts):
    return jax.ShapeDtypeStruct((N_CHIP, *gparts.shape[1:]), gparts.dtype)


def _adamw_math(w, g, m, v):
    m = ADAM_B1 * m + (1.0 - ADAM_B1) * g
    v = ADAM_B2 * v + (1.0 - ADAM_B2) * (g * g)
    m_hat = m / (1.0 - ADAM_B1**ADAM_STEP)
    v_hat = v / (1.0 - ADAM_B2**ADAM_STEP)
    delta = -ADAM_LR * (m_hat / (jnp.sqrt(v_hat) + ADAM_EPS) + ADAM_WD * w)
    return delta, m, v


ROW_CHUNK = 64


ELEMS_PER_STEP = 128 * 1024


N_SCATTER_SCRATCH = 9


def _reduce_scatter(gparts, name):
    n = len(gparts)

    def body(*refs):
        g_hbm, recv = refs[:n], refs[n : 2 * n]
        scr = [refs[2 * n + k * N_SCATTER_SCRATCH : 2 * n + (k + 1) * N_SCATTER_SCRATCH] for k in range(n)]
        for k in range(n):
            _scatter_start(g_hbm[k], scr[k])
        for k in range(n):
            _scatter_middle(g_hbm[k], recv[k], scr[k])
        for k in range(n):
            _scatter_finish(g_hbm[k], recv[k], scr[k])

    return pl.pallas_call(
        body,
        name=name,
        out_shape=[_scatter_out_shape(g) for g in gparts],
        in_specs=[ANY_SPEC] * n,
        out_specs=[ANY_SPEC] * n,
        scratch_shapes=[s for g in gparts for s in _scatter_scratch(*g.shape[1:])],
        compiler_params=pltpu.CompilerParams(vmem_limit_bytes=VMEM_LIMIT),
    )(*gparts)


def _sum_adamw(parts, w, m, v, name):
    _, rows, cols = w.shape
    rb = rows
    while rb * cols > ELEMS_PER_STEP and rb % 16 == 0:
        rb //= 2

    def body(p_ref, w_ref, m_ref, v_ref, grad_ref, delta_ref, nm_ref, nv_ref):
        g = p_ref[0].astype(F32)
        for k in range(1, p_ref.shape[0]):
            g = g + p_ref[k].astype(F32)
        delta, nm, nv = _adamw_math(w_ref[0], g, m_ref[0], v_ref[0])
        grad_ref[0] = g
        delta_ref[0] = delta
        nm_ref[0] = nm
        nv_ref[0] = nv

    block = lambda lead: pl.BlockSpec((lead, rb, cols), lambda i: (0, i, 0))
    out = jax.ShapeDtypeStruct(w.shape, F32)
    return pl.pallas_call(
        body,
        name=name,
        grid=(rows // rb,),
        out_shape=[out] * 4,
        in_specs=[block(parts.shape[0])] + [block(1)] * 3,
        out_specs=[block(1)] * 4,
        compiler_params=pltpu.CompilerParams(dimension_semantics=("arbitrary",), vmem_limit_bytes=VMEM_LIMIT),
    )(parts, w, m, v)


def _adamw_multi(items, name):
    n = len(items)

    def body(*refs):
        ins, outs = refs[: 4 * n], refs[4 * n :]
        for a in range(n):
            w_ref, g_ref, m_ref, v_ref = ins[4 * a : 4 * a + 4]
            d_ref, nm_ref, nv_ref = outs[3 * a : 3 * a + 3]
            delta, nm, nv = _adamw_math(w_ref[...], g_ref[...], m_ref[...], v_ref[...])
            d_ref[...] = delta
            nm_ref[...] = nm
            nv_ref[...] = nv

    flat = [a for it in items for a in it]
    out_shape = [jax.ShapeDtypeStruct(it[0].shape, F32) for it in items for _ in range(3)]
    outs = pl.pallas_call(
        body,
        name=name,
        out_shape=out_shape,
        in_specs=[VMEM_SPEC] * (4 * n),
        out_specs=[VMEM_SPEC] * (3 * n),
        compiler_params=pltpu.CompilerParams(vmem_limit_bytes=VMEM_LIMIT),
    )(*flat)
    return [tuple(outs[3 * a : 3 * a + 3]) for a in range(n)]


def _prologue(w_in, w_out, conv_w, c, w_ada, b_ada):
    D = c.shape[1]
    wc = w_ada.shape[2]
    shards16 = [jax.ShapeDtypeStruct(w_in.shape[1:], BF16), jax.ShapeDtypeStruct(w_out.shape[1:], BF16)]

    def to_all(src, out, sems):
        x, y, c = _my_place()
        me = _slot(x, y, c)
        copies = [
            pltpu.make_async_remote_copy(
                src_ref=src, dst_ref=out.at[me], send_sem=sems[0].at[k - 1], recv_sem=sems[1].at[k - 1],
                device_id=(x ^ (k >> 2), y ^ ((k >> 1) & 1), c ^ (k & 1)), device_id_type=MESH)
            for k in range(1, N_DEV)
        ]
        return copies, pltpu.make_async_copy(src, out.at[me], sems[2].at[0])

    def start(copies, own):
        for cp in copies + [own]:
            cp.start()

    def finish(copies, own):
        for cp in copies:
            cp.wait()
        own.wait()

    def body(win_ref, wout_ref, cw_ref, c_ref, wada_ref, b_ref, win_g, wout_g, cw_g, c_g, mod_g, win16, wout16, c_s, mp_s, *sems):
        c_copies = to_all(c_ref, c_g, sems[3:6])
        start(*c_copies)
        win16[...] = win_ref[0].astype(BF16)
        wout16[...] = wout_ref[0].astype(BF16)
        w_ins, w_outs, w_sems = (win16, wout16, cw_ref), (win_g, wout_g, cw_g), sems[0:3]
        _gather_start(w_ins, w_outs, w_sems)
        finish(*c_copies)
        for k in range(N_DEV):
            c_s[k : k + 1, :] = c_g[k]
        cv = c_s[...]
        cond = cv * jax.nn.sigmoid(cv)
        b_mine = b_ref[:, pl.ds(pl.multiple_of(_slot(*_my_place()) * wc, 128), wc)]
        mp_s[...] = jnp.dot(cond, wada_ref[0], precision=lax.Precision.HIGHEST, preferred_element_type=F32) + b_mine
        m_copies = to_all(mp_s, mod_g, sems[6:9])
        start(*m_copies)
        for j in range(3):
            _gather_forward(w_ins, w_outs, w_sems, j)
        _gather_finish(w_ins, w_outs, w_sems)
        finish(*m_copies)

    return pl.pallas_call(
        body,
        name="prologue",
        out_shape=_gather_out_shape(shards16 + [conv_w], [True, False, False])
        + [jax.ShapeDtypeStruct((N_DEV, 1, D), F32), jax.ShapeDtypeStruct((N_DEV, N_DEV, wc), F32)],
        in_specs=[VMEM_SPEC, VMEM_SPEC, ANY_SPEC] + [VMEM_SPEC] * 3,
        out_specs=[ANY_SPEC] * 3 + [VMEM_SPEC] * 2,
        scratch_shapes=[pltpu.VMEM(s.shape, BF16) for s in shards16]
        + [pltpu.VMEM((N_DEV, D), F32), pltpu.VMEM((N_DEV, wc), F32)]
        + _gather_scratch(3) + _gather_scratch(1) + _gather_scratch(1),
        compiler_params=pltpu.CompilerParams(vmem_limit_bytes=VMEM_LIMIT),
    )(w_in, w_out, conv_w, c, w_ada, b_ada)


def _ln_fwd(r):
    mu = jnp.mean(r, axis=-1, keepdims=True)
    d = r - mu
    var = jnp.mean(d * d, axis=-1, keepdims=True)
    rstd = lax.rsqrt(var + LN_EPS)
    return d * rstd, rstd


def _ln_bwd(dxh, xhat, rstd):
    m1 = jnp.mean(dxh, axis=-1, keepdims=True)
    m2 = jnp.mean(dxh * xhat, axis=-1, keepdims=True)
    return rstd * (dxh - m1 - xhat * m2)


def _colsum(a):
    return jnp.sum(a, axis=0, keepdims=True)


def _window_sums(ext, tm, causal):
    n = ext.shape[0]
    lo = HALO if causal else 0
    s, out = ext, []
    for p in range(len(POOL_WINDOWS)):
        assert POOL_WINDOWS[p] == 2 ** (p + 1)
        k = 2**p
        s = s + pltpu.roll(s, k if causal else n - k, 0)
        out.append(s[lo : lo + tm, 0:128])
        if p + 1 < len(POOL_WINDOWS):
            s = s[:, 128:]
    return out


def _pool_features(vp, vp_s, row, tm):
    sums = _window_sums(vp_s[...], tm, causal=True)
    feats, inv_cnts = [], []
    for g, win in enumerate(POOL_WINDOWS):
        inv_cnt = 1.0 / jnp.minimum(row + 1, win).astype(F32)
        feats.append(sums[g] * inv_cnt - vp[:, 128 * g : 128 * g + 128])
        inv_cnts.append(inv_cnt)
    return feats, inv_cnts


def _f1(x, mod, w_in, conv_w, w_pool, pool_scale, w_out, tm, gather, by_cols):
    T, D = x.shape
    ZW = w_in.shape[1]
    CC = ZW // 4
    nt = T // tm
    ng = len(gather)
    fwd_steps = [max(nt - 3 + j, 0) for j in range(3)]

    shards16 = [jax.ShapeDtypeStruct(s.shape[1:], BF16) for s in gather]

    def body(*refs):
        x_ref, mod_ref, win_ref, cw_ref, wp_ref, ps_ref, wout_ref = refs[:7]
        g_f32 = refs[7 : 7 + ng]
        z_ref, h_ref, xhat_ref, rstd_ref, mix_ref = refs[7 + ng : 12 + ng]
        g_outs = refs[12 + ng : 12 + 2 * ng]
        cv_s, vp_s = refs[12 + 2 * ng : 14 + 2 * ng]
        g_ins = refs[14 + 2 * ng : 14 + 3 * ng]
        g_sems = refs[14 + 3 * ng :]
        i = pl.program_id(0)

        @pl.when(i == 0)
        def _():
            for src, dst in zip(g_f32, g_ins):
                dst[...] = src[0].astype(BF16)
            _gather_start(g_ins, g_outs, g_sems)
            cv_s[0:HALO, :] = jnp.zeros((HALO, CC), F32)
            vp_s[0:HALO, :] = jnp.zeros((HALO, CC), F32)

        xv = x_ref[...]
        sh1, sc1, g1 = mod_ref[0:1, :], mod_ref[1:2, :], mod_ref[2:3, :]
        h = (xv * (1.0 + sc1) + sh1).astype(BF16)
        h_ref[...] = h
        z = jnp.dot(h, win_ref[...], preferred_element_type=F32)
        z_ref[...] = z.astype(BF16)
        gb, gc, vc, vp = z[:, 0:CC], z[:, CC : 2 * CC], z[:, 2 * CC : 3 * CC], z[:, 3 * CC : 4 * CC]
        cv = gc * vc
        cv_s[HALO : HALO + tm, :] = cv
        vp_s[HALO : HALO + tm, :] = vp
        cv_ext = cv_s[...]
        cv_m2 = pltpu.roll(cv_ext, 2, 0)[HALO : HALO + tm, :]
        cv_m1 = pltpu.roll(cv_ext, 1, 0)[HALO : HALO + tm, :]
        conv = cw_ref[0:1, :] * cv_m2 + cw_ref[1:2, :] * cv_m1 + cw_ref[2:3, :] * cv
        parts = [gb * conv]
        row = i * tm + lax.broadcasted_iota(jnp.int32, (tm, 1), 0)
        feats, _ = _pool_features(vp, vp_s, row, tm)
        for g in range(len(POOL_WINDOWS)):
            pw = jnp.dot(feats[g].astype(BF16), wp_ref[g].astype(BF16), preferred_element_type=F32)
            parts.append(pw * ps_ref[0:1, 128 * g : 128 * g + 128])
        cv_s[0:HALO, :] = cv_s[tm : tm + HALO, :]
        vp_s[0:HALO, :] = vp_s[tm : tm + HALO, :]
        ycat = jnp.concatenate(parts, axis=1).astype(BF16)
        mix = jnp.dot(ycat, wout_ref[...], preferred_element_type=F32)
        mix_ref[...] = mix
        xhat, rstd = _ln_fwd(DEEPNORM_ALPHA * xv + (1.0 + g1) * mix)
        xhat_ref[...] = xhat
        rstd_ref[...] = rstd

        for j in range(3):

            @pl.when(i == fwd_steps[j])
            def _(j=j):
                _gather_forward(g_ins, g_outs, g_sems, j)

        @pl.when(i == nt - 1)
        def _():
            _gather_finish(g_ins, g_outs, g_sems)

    tile = lambda w: pl.BlockSpec((tm, w), lambda i: (i, 0))
    return pl.pallas_call(
        body,
        name="f1",
        grid=(nt,),
        out_shape=[
            jax.ShapeDtypeStruct((T, ZW), BF16),
            jax.ShapeDtypeStruct((T, D), BF16),
            jax.ShapeDtypeStruct((T, D), F32),
            jax.ShapeDtypeStruct((T, 1), F32),
            jax.ShapeDtypeStruct((T, D), F32),
        ]
        + _gather_out_shape(shards16, by_cols),
        in_specs=[tile(D)] + [VMEM_SPEC] * (6 + ng),
        out_specs=[tile(ZW), tile(D), tile(D), tile(1), tile(D)] + [ANY_SPEC] * ng,
        scratch_shapes=[pltpu.VMEM((HALO + tm, CC), F32), pltpu.VMEM((HALO + tm, CC), F32)]
        + [pltpu.VMEM(s.shape, BF16) for s in shards16]
        + _gather_scratch(ng),
        compiler_params=pltpu.CompilerParams(dimension_semantics=("arbitrary",), vmem_limit_bytes=VMEM_LIMIT),
    )(x, mod, w_in, conv_w, w_pool, pool_scale, w_out, *gather)


def _fb2(xhat1, target, mod, ln, w_mi, w_mo, tm):
    T, D = xhat1.shape
    H = w_mi.shape[1]
    hc = min(1024, H)
    nb = H // hc
    nt = T // tm

    def body(xh1_ref, t_ref, mod_ref, ln_ref, wmi_ref, wmo_ref, dx1_ref, h2_ref, a_ref, du_ref, df_ref, acc_ref):
        i = pl.program_id(0)

        @pl.when(i == 0)
        def _():
            acc_ref[...] = jnp.zeros((8, D), F32)

        sh2, sc2, g2 = mod_ref[3:4, :], mod_ref[4:5, :], mod_ref[5:6, :]
        x1 = xh1_ref[...] * ln_ref[0:1, :] + ln_ref[1:2, :]
        h2 = (x1 * (1.0 + sc2) + sh2).astype(BF16)
        h2_ref[...] = h2
        f = jnp.zeros((tm, D), F32)
        for k in range(nb):
            ks = slice(k * hc, (k + 1) * hc)
            r = jnp.maximum(jnp.dot(h2, wmi_ref[:, ks], preferred_element_type=F32), 0.0)
            du_ref[:, ks] = r.astype(BF16)
            a = (r * r).astype(BF16)
            a_ref[:, ks] = a
            f = f + jnp.dot(a, wmo_ref[ks, :], preferred_element_type=F32)
        xhat2, rstd2 = _ln_fwd(DEEPNORM_ALPHA * x1 + (1.0 + g2) * f)
        ln2_g = ln_ref[2:3, :]
        d = xhat2 * ln2_g + ln_ref[3:4, :] - t_ref[...]
        dr2 = _ln_bwd(d * (ln2_g * (1.0 / D)), xhat2, rstd2)
        df = ((1.0 + g2) * dr2).astype(BF16)
        df_ref[...] = df
        dh2 = jnp.zeros((tm, D), F32)
        for k in range(nb):
            ks = slice(k * hc, (k + 1) * hc)
            da = lax.dot_general(df, wmo_ref[ks, :], NT, preferred_element_type=F32)
            du = (da * (2.0 * du_ref[:, ks].astype(F32))).astype(BF16)
            du_ref[:, ks] = du
            dh2 = dh2 + lax.dot_general(du, wmi_ref[:, ks], NT, preferred_element_type=F32)
        dx1_ref[...] = DEEPNORM_ALPHA * dr2 + dh2 * (1.0 + sc2)
        acc_ref[0:1, :] += _colsum(d * xhat2) * (1.0 / D)
        acc_ref[1:2, :] += _colsum(d) * (1.0 / D)
        acc_ref[2:3, :] += _colsum(dh2)
        acc_ref[3:4, :] += _colsum(dh2 * x1)
        acc_ref[4:5, :] += _colsum(dr2 * f)
        acc_ref[5:6, :] += jnp.zeros((1, D), F32) + (0.5 / D) * jnp.sum(d * d)

    tile = lambda w: pl.BlockSpec((tm, w), lambda i: (i, 0))
    return pl.pallas_call(
        body,
        name="fb2",
        grid=(nt,),
        out_shape=[
            jax.ShapeDtypeStruct((T, D), F32),
            jax.ShapeDtypeStruct((T, D), BF16),
            jax.ShapeDtypeStruct((T, H), BF16),
            jax.ShapeDtypeStruct((T, H), BF16),
            jax.ShapeDtypeStruct((T, D), BF16),
            jax.ShapeDtypeStruct((8, D), F32),
        ],
        in_specs=[tile(D), tile(D)] + [VMEM_SPEC] * 4,
        out_specs=[tile(D), tile(D), tile(H), tile(H), tile(D), pl.BlockSpec((8, D), lambda i: (0, 0))],
        compiler_params=pltpu.CompilerParams(dimension_semantics=("arbitrary",), vmem_limit_bytes=VMEM_LIMIT),
    )(xhat1, target, mod, ln, w_mi, w_mo)


def _b1(dx1, xhat1, rstd1, x, mix, z, mod, ln, w_out, w_in, conv_w, w_pool, pool_scale, tm):
    T, D = x.shape
    ZW = w_in.shape[1]
    CC = ZW // 4
    nt = T // tm
    hb = tm // HALO

    def body(dx1_ref, xh1_ref, rstd_ref, x_ref, mix_ref, z_ref, zh_ref, mod_ref, ln_ref, wout_ref, win_ref, cw_ref, wp_ref, ps_ref,
             dx_ref, dmix_ref, ycat_ref, dz_ref, acc_ref, gcw_ref, gwp_ref, cv_s, vp_s, e_s, q_s):
        i = pl.program_id(0)
        j = nt - 1 - i

        @pl.when(i == 0)
        def _():
            acc_ref[...] = jnp.zeros((8, D), F32)
            gcw_ref[...] = jnp.zeros((8, CC), F32)
            gwp_ref[...] = jnp.zeros(gwp_ref.shape, F32)
            e_s[tm : tm + HALO, :] = jnp.zeros((HALO, CC), F32)
            q_s[tm : tm + HALO, :] = jnp.zeros((HALO, CC), F32)

        sh1, sc1, g1 = mod_ref[0:1, :], mod_ref[1:2, :], mod_ref[2:3, :]
        dx1 = dx1_ref[...]
        xhat1 = xh1_ref[...]
        acc_ref[0:1, :] += _colsum(dx1 * xhat1)
        acc_ref[1:2, :] += _colsum(dx1)
        dr1 = _ln_bwd(dx1 * ln_ref[0:1, :], xhat1, rstd_ref[...])
        acc_ref[4:5, :] += _colsum(dr1 * mix_ref[...])
        dmix = ((1.0 + g1) * dr1).astype(BF16)
        dmix_ref[...] = dmix
        dycat = lax.dot_general(dmix, wout_ref[...], NT, preferred_element_type=F32)

        z = z_ref[...].astype(F32)
        zh = zh_ref[...].astype(F32) * jnp.where(j > 0, 1.0, 0.0)
        gb, gc, vc, vp = z[:, 0:CC], z[:, CC : 2 * CC], z[:, 2 * CC : 3 * CC], z[:, 3 * CC : 4 * CC]
        cv = gc * vc
        cv_s[0:HALO, :] = zh[:, CC : 2 * CC] * zh[:, 2 * CC : 3 * CC]
        cv_s[HALO : HALO + tm, :] = cv
        vp_s[0:HALO, :] = zh[:, 3 * CC : 4 * CC]
        vp_s[HALO : HALO + tm, :] = vp
        cv_ext = cv_s[...]
        cv_m2 = pltpu.roll(cv_ext, 2, 0)[HALO : HALO + tm, :]
        cv_m1 = pltpu.roll(cv_ext, 1, 0)[HALO : HALO + tm, :]
        w0, w1, w2 = cw_ref[0:1, :], cw_ref[1:2, :], cw_ref[2:3, :]
        conv = w0 * cv_m2 + w1 * cv_m1 + w2 * cv
        dyc = dycat[:, 0:CC]
        e = dyc * gb
        e_s[0:tm, :] = e
        e_ext = e_s[...]
        dcv = w2 * e + w1 * pltpu.roll(e_ext, tm + HALO - 1, 0)[0:tm, :] + w0 * pltpu.roll(e_ext, tm + HALO - 2, 0)[0:tm, :]
        gcw_ref[0:1, :] += _colsum(e * cv_m2)
        gcw_ref[1:2, :] += _colsum(e * cv_m1)
        gcw_ref[2:3, :] += _colsum(e * cv)
        y_parts = [gb * conv]
        dz_parts = [dyc * conv, dcv * vc, dcv * gc]

        row = j * tm + lax.broadcasted_iota(jnp.int32, (tm, 1), 0)
        feats, inv_cnts = _pool_features(vp, vp_s, row, tm)
        gps_parts, dps = [], []
        for g in range(len(POOL_WINDOWS)):
            cols = slice(128 * g, 128 * g + 128)
            p = feats[g].astype(BF16)
            scale = ps_ref[0:1, cols]
            wp = wp_ref[g].astype(BF16)
            pw = jnp.dot(p, wp, preferred_element_type=F32)
            y_parts.append(pw * scale)
            dyp = dycat[:, CC + 128 * g : CC + 128 * g + 128]
            gps_parts.append(_colsum(dyp * pw))
            dpw = (dyp * scale).astype(BF16)
            gwp_ref[g] += lax.dot_general(p, dpw, TN, preferred_element_type=F32)
            dp = lax.dot_general(dpw, wp, NT, preferred_element_type=F32)
            q_s[0:tm, cols] = dp * inv_cnts[g]
            dps.append(dp)
        sq = _window_sums(q_s[...], tm, causal=False)
        dz_parts += [sq[g] - dps[g] for g in range(len(POOL_WINDOWS))]
        gcw_ref[3:4, :] += jnp.concatenate(gps_parts, axis=1)
        ycat_ref[...] = jnp.concatenate(y_parts, axis=1).astype(BF16)
        dz = jnp.concatenate(dz_parts, axis=1).astype(BF16)
        dz_ref[...] = dz
        dh = lax.dot_general(dz, win_ref[...], NT, preferred_element_type=F32)
        acc_ref[2:3, :] += _colsum(dh)
        acc_ref[3:4, :] += _colsum(dh * x_ref[...])
        dx_ref[...] = DEEPNORM_ALPHA * dr1 + dh * (1.0 + sc1)
        e_s[tm : tm + HALO, :] = e_s[0:HALO, :]
        q_s[tm : tm + HALO, :] = q_s[0:HALO, :]

    tile = lambda w: pl.BlockSpec((tm, w), lambda i: (nt - 1 - i, 0))
    halo = pl.BlockSpec((HALO, ZW), lambda i: (jnp.maximum((nt - 1 - i) * hb - 1, 0), 0))
    fixed = lambda shape: pl.BlockSpec(shape, lambda i: (0,) * len(shape))
    return pl.pallas_call(
        body,
        name="b1",
        grid=(nt,),
        out_shape=[
            jax.ShapeDtypeStruct((T, D), F32),
            jax.ShapeDtypeStruct((T, D), BF16),
            jax.ShapeDtypeStruct((T, D), BF16),
            jax.ShapeDtypeStruct((T, ZW), BF16),
            jax.ShapeDtypeStruct((8, D), F32),
            jax.ShapeDtypeStruct((8, CC), F32),
            jax.ShapeDtypeStruct(w_pool.shape, F32),
        ],
        in_specs=[tile(D), tile(D), tile(1), tile(D), tile(D), tile(ZW), halo] + [VMEM_SPEC] * 7,
        out_specs=[tile(D), tile(D), tile(D), tile(ZW), fixed((8, D)), fixed((8, CC)), fixed(w_pool.shape)],
        scratch_shapes=[
            pltpu.VMEM((HALO + tm, CC), F32),
            pltpu.VMEM((HALO + tm, CC), F32),
            pltpu.VMEM((tm + HALO, CC), F32),
            pltpu.VMEM((tm + HALO, CC), F32),
        ],
        compiler_params=pltpu.CompilerParams(dimension_semantics=("arbitrary",), vmem_limit_bytes=VMEM_LIMIT),
    )(dx1, xhat1, rstd1, x, mix, z, z, mod, ln, w_out, w_in, conv_w, w_pool, pool_scale)


def _wgrad(a, b, bk, n_groups, bt, name, owners=None, scatter=None, gather=()):
    T, K = a.shape
    N = b.shape[1]
    nk, nt, ng = K // bk, T // bt, N // n_groups
    nc = min(512, ng)
    ns, ngat = (0 if scatter is None else 1), len(gather)
    n_steps = nk * n_groups * nt
    mid_step = min(1, n_steps - 1)
    fwd_steps = [min(2 * (j + 1), n_steps - 1) for j in range(3)]

    def body(*refs):
        a_ref, b_ref = refs[0], refs[1]
        g_ins = refs[2 + ns : 2 + ns + ngat]
        outs = refs[2 + ns + ngat :]
        o_ref, g_outs = outs[0], outs[1 + ns : 1 + ns + ngat]
        scr = outs[1 + ns + ngat :]
        acc = scr[0]
        if ns:
            s_hbm, s_recv, s_scr = refs[2], outs[1], scr[1 : 1 + N_SCATTER_SCRATCH]
        g_sems = scr[1 + N_SCATTER_SCRATCH * ns :]
        kk, gg, t = pl.program_id(0), pl.program_id(1), pl.program_id(2)
        step = (kk * n_groups + gg) * nt + t

        if ngat:

            @pl.when(step == 0)
            def _():
                _gather_start(g_ins, g_outs, g_sems)

            for j in range(3):

                @pl.when(step == fwd_steps[j])
                def _(j=j):
                    _gather_forward(g_ins, g_outs, g_sems, j)

        if ns:

            @pl.when(step == 0)
            def _():
                _scatter_start(s_hbm, s_scr)

            @pl.when(step == mid_step)
            def _():
                _scatter_middle(s_hbm, s_recv, s_scr)

        @pl.when(t == 0)
        def _():
            acc[...] = jnp.zeros(acc.shape, F32)

        at = a_ref[...].T
        for c in range(ng // nc):
            cs = slice(c * nc, (c + 1) * nc)
            acc[:, cs] += jnp.dot(at, b_ref[:, cs], preferred_element_type=F32)

        @pl.when(t == nt - 1)
        def _():
            if owners is None:
                o_ref[...] = acc[...].astype(BF16)
            else:
                per = N // owners
                for o in range(ng // per):
                    o_ref[o] = acc[:, o * per : (o + 1) * per].astype(BF16)

        if ns:

            @pl.when(step == n_steps - 1)
            def _():
                _scatter_finish(s_hbm, s_recv, s_scr)

        if ngat:

            @pl.when(step == n_steps - 1)
            def _():
                _gather_finish(g_ins, g_outs, g_sems)

    if owners is None:
        out_shape = [jax.ShapeDtypeStruct((K, N), BF16)]
        out_specs = [pl.BlockSpec((bk, ng), lambda k, g, t: (k, g))]
    else:
        assert bk == K
        per = N // owners
        out_shape = [jax.ShapeDtypeStruct((owners, K, per), BF16)]
        out_specs = [pl.BlockSpec((ng // per, K, per), lambda k, g, t: (g, 0, 0))]
    ins, in_specs = [a, b], [pl.BlockSpec((bt, bk), lambda k, g, t: (t, k)), pl.BlockSpec((bt, ng), lambda k, g, t: (t, g))]
    scratch = [pltpu.VMEM((bk, ng), F32)]
    if ns:
        ins.append(scatter)
        in_specs.append(ANY_SPEC)
        out_shape.append(_scatter_out_shape(scatter))
        out_specs.append(ANY_SPEC)
        scratch += _scatter_scratch(*scatter.shape[1:])
    if ngat:
        ins += list(gather)
        in_specs += [ANY_SPEC] * ngat
        out_shape += _gather_out_shape(gather, [False] * ngat)
        out_specs += [ANY_SPEC] * ngat
        scratch += _gather_scratch(ngat)
    outs = pl.pallas_call(
        body,
        name=name,
        grid=(nk, n_groups, nt),
        out_shape=out_shape,
        in_specs=in_specs,
        out_specs=out_specs,
        scratch_shapes=scratch,
        compiler_params=pltpu.CompilerParams(dimension_semantics=("arbitrary", "arbitrary", "arbitrary"), vmem_limit_bytes=VMEM_LIMIT),
    )(*ins)
    return outs if ns + ngat else outs[0]


def _wgrad_pair(a0, b0, a1, b1, bt, name, owners, scatter, gather):
    T, K0 = a0.shape
    N0 = b0.shape[1]
    K1, N1 = a1.shape[1], b1.shape[1]
    nt = T // bt
    nc = 512
    per = N0 // owners
    ngat = len(gather)
    n_steps = 2 * nt
    fwd_steps = [min(2 * (j + 1), n_steps - 1) for j in range(3)]

    def body(*refs):
        a0_ref, b0_ref, a1_ref, b1_ref, s_hbm = refs[:5]
        g_ins = refs[5 : 5 + ngat]
        o0_ref, o1_ref, s_recv = refs[5 + ngat : 8 + ngat]
        g_outs = refs[8 + ngat : 8 + 2 * ngat]
        scr = refs[8 + 2 * ngat :]
        acc0, acc1 = scr[0], scr[1]
        s_scr, g_sems = scr[2 : 2 + N_SCATTER_SCRATCH], scr[2 + N_SCATTER_SCRATCH :]
        p, t = pl.program_id(0), pl.program_id(1)
        step = p * nt + t

        @pl.when(step == 0)
        def _():
            _scatter_start(s_hbm, s_scr)
            _gather_start(g_ins, g_outs, g_sems)

        @pl.when(step == 1)
        def _():
            _scatter_middle(s_hbm, s_recv, s_scr)

        for j in range(3):

            @pl.when(step == fwd_steps[j])
            def _(j=j):
                _gather_forward(g_ins, g_outs, g_sems, j)

        def accumulate(a_ref, b_ref, acc, n_cols):
            @pl.when(t == 0)
            def _():
                acc[...] = jnp.zeros(acc.shape, F32)

            at = a_ref[...].T
            for c in range(n_cols // nc):
                cs = slice(c * nc, (c + 1) * nc)
                acc[:, cs] += jnp.dot(at, b_ref[:, cs], preferred_element_type=F32)

        @pl.when(p == 0)
        def _():
            accumulate(a0_ref, b0_ref, acc0, N0)

            @pl.when(t == nt - 1)
            def _():
                for o in range(owners):
                    o0_ref[o] = acc0[:, o * per : (o + 1) * per].astype(BF16)

        @pl.when(p == 1)
        def _():
            accumulate(a1_ref, b1_ref, acc1, N1)

            @pl.when(t == nt - 1)
            def _():
                o1_ref[...] = acc1[...].astype(BF16)

        @pl.when(step == n_steps - 1)
        def _():
            _scatter_finish(s_hbm, s_recv, s_scr)
            _gather_finish(g_ins, g_outs, g_sems)

    first = lambda w: pl.BlockSpec((bt, w), lambda p, t: (jnp.where(p == 0, t, nt - 1), 0))
    second = lambda w: pl.BlockSpec((bt, w), lambda p, t: (jnp.where(p == 1, t, 0), 0))
    return pl.pallas_call(
        body,
        name=name,
        grid=(2, nt),
        out_shape=[jax.ShapeDtypeStruct((owners, K0, per), BF16), jax.ShapeDtypeStruct((K1, N1), BF16), _scatter_out_shape(scatter)]
        + _gather_out_shape(gather, [False] * ngat),
        in_specs=[first(K0), first(N0), second(K1), second(N1), ANY_SPEC] + [ANY_SPEC] * ngat,
        out_specs=[pl.BlockSpec((owners, K0, per), lambda p, t: (0, 0, 0)), pl.BlockSpec((K1, N1), lambda p, t: (0, 0)), ANY_SPEC]
        + [ANY_SPEC] * ngat,
        scratch_shapes=[pltpu.VMEM((K0, N0), F32), pltpu.VMEM((K1, N1), F32)]
        + _scatter_scratch(*scatter.shape[1:]) + _gather_scratch(ngat),
        compiler_params=pltpu.CompilerParams(dimension_semantics=("arbitrary", "arbitrary"), vmem_limit_bytes=VMEM_LIMIT),
    )(a0, b0, a1, b1, scatter, *gather)


def _small_grads(acc1_all, acc2_all, gcw_all, gwp_all, cond_t, my_slot, w_ada, m_w_ada, v_w_ada):
    D = acc1_all.shape[2]
    w_cols = w_ada.shape[2]
    n_chunk = D // 128
    q_mine = w_cols // 128

    def total(ref, r):
        s = ref[0, r : r + 1, :]
        for k in range(1, N_DEV):
            s = s + ref[k, r : r + 1, :]
        return s

    def body(slot_ref, a1_ref, a2_ref, gcw_ref, gwp_ref, ct_ref, w_ref, m_ref, v_ref,
             gb_ref, gw_ref, gl1g_ref, gl1b_ref, gl2g_ref, gl2b_ref, gcwo_ref, gps_ref, gwpo_ref, loss_ref,
             dw_ref, nm_ref, nv_ref, dm_s):
        loss_ref[...] = total(a2_ref, 5)[:, 0:1]
        for s, (ref, r) in enumerate([(a1_ref, 2), (a1_ref, 3), (a1_ref, 4), (a2_ref, 2), (a2_ref, 3), (a2_ref, 4)]):
            gb_ref[0:1, s * D : (s + 1) * D] = total(ref, r)
            for k in range(N_DEV):
                row = ref[k, r : r + 1, :]
                for qq in range(n_chunk):
                    dm_s[s * n_chunk + qq, k : k + 1, :] = row[:, 128 * qq : 128 * qq + 128]
        gl1g_ref[...] = total(a1_ref, 0)
        gl1b_ref[...] = total(a1_ref, 1)
        gl2g_ref[...] = total(a2_ref, 0)
        gl2b_ref[...] = total(a2_ref, 1)
        gcwo_ref[...] = jnp.zeros(gcwo_ref.shape, F32)
        for r in range(3):
            gcwo_ref[r : r + 1, :] = total(gcw_ref, r)
        gps_ref[...] = total(gcw_ref, 3)
        wp = gwp_ref[0]
        for k in range(1, N_DEV):
            wp = wp + gwp_ref[k]
        gwpo_ref[0] = wp
        ct = ct_ref[...]
        cond_t = ct * jax.nn.sigmoid(ct)
        q0 = slot_ref[0] * q_mine
        for q in range(q_mine):
            dm = dm_s[q0 + q]
            out = cond_t[:, 0:1] * dm[0:1, :]
            for k in range(1, N_DEV):
                out = out + cond_t[:, k : k + 1] * dm[k : k + 1, :]
            cols = slice(128 * q, 128 * q + 128)
            gw_ref[0, :, cols] = out
            delta, nm, nv = _adamw_math(w_ref[0, :, cols], out, m_ref[0, :, cols], v_ref[0, :, cols])
            dw_ref[0, :, cols] = delta
            nm_ref[0, :, cols] = nm
            nv_ref[0, :, cols] = nv

    CC = gcw_all.shape[2]
    w_like = jax.ShapeDtypeStruct(w_ada.shape, F32)
    return pl.pallas_call(
        body,
        name="small_grads",
        out_shape=[
            jax.ShapeDtypeStruct((1, 6 * D), F32),
            w_like,
            *[jax.ShapeDtypeStruct((1, D), F32)] * 4,
            jax.ShapeDtypeStruct((8, CC), F32),
            jax.ShapeDtypeStruct((1, CC), F32),
            jax.ShapeDtypeStruct((1, *gwp_all.shape[1:]), F32),
            jax.ShapeDtypeStruct((1, 1), F32),
            w_like,
            w_like,
            w_like,
        ],
        in_specs=[pl.BlockSpec(memory_space=pltpu.SMEM)] + [VMEM_SPEC] * 8,
        out_specs=[VMEM_SPEC] * 13,
        scratch_shapes=[pltpu.VMEM((6 * n_chunk, N_DEV, 128), F32)],
        compiler_params=pltpu.CompilerParams(vmem_limit_bytes=VMEM_LIMIT),
    )(my_slot, acc1_all, acc2_all, gcw_all, gwp_all, cond_t, w_ada, m_w_ada, v_w_ada)


def kernel(x, c, w_ada, b_ada, w_in, conv_w, w_pool, pool_scale, w_out, ln1_g, ln1_b, w_mlp_in, w_mlp_out, ln2_g, ln2_b, loss_target, m_w_ada, m_b_ada, m_w_in, m_conv_w, m_w_pool, m_pool_scale, m_w_out, m_ln1_g, m_ln1_b, m_w_mlp_in, m_w_mlp_out, m_ln2_g, m_ln2_b, v_w_ada, v_b_ada, v_w_in, v_conv_w, v_w_pool, v_pool_scale, v_w_out, v_ln1_g, v_ln1_b, v_w_mlp_in, v_w_mlp_out, v_ln2_g, v_ln2_b):
    T, D = x.shape[1], x.shape[2]
    H = w_mlp_out.shape[1] * N_DEV
    ZW = w_in.shape[2] * N_DEV
    CC = ZW // 4
    tm = min(512, T // 2)
    bt = min(1024, T)
    ax, ay, ac = _my_place()
    me = _slot(ax, ay, ac)

    cw3, m_cw3, v_cw3 = (jnp.transpose(t, (1, 0, 2)) for t in (conv_w, m_conv_w, v_conv_w))
    w_in_f, w_out_g, cw_g, c_g, mod_g = _prologue(w_in, w_out, cw3, c, w_ada, b_ada)
    w_out_f = w_out_g.reshape(D, D)
    conv_w_f = jnp.transpose(cw_g[:, :, 0, :], (1, 0, 2)).reshape(conv_w.shape[1], CC)
    c_all = c_g.reshape(N_DEV, D)
    mod = lax.dynamic_index_in_dim(mod_g, me, axis=1, keepdims=False).reshape(6, D)

    ln = jnp.concatenate([ln1_g, ln1_b, ln2_g, ln2_b], axis=0)
    xs, target = x[0], loss_target[0]

    z, h, xhat1, rstd1, mix, w_mi_f, w_mo_g = _f1(
        xs, mod, w_in_f, conv_w_f, w_pool[0], pool_scale, w_out_f, tm,
        [w_mlp_in, w_mlp_out], [True, False])
    dx1, h2, a, du, df, acc2 = _fb2(xhat1, target, mod, ln, w_mi_f, w_mo_g.reshape(H, D), tm)

    grad_x, dmix, ycat, dz, acc1, gcw, gwp = _b1(
        dx1, xhat1, rstd1, xs, mix, z, mod, ln, w_out_f, w_in_f, conv_w_f, w_pool[0], pool_scale, tm)

    gp_mo = _wgrad(a, df, D, 1, min(2 * bt, T), "wgrad_mlp_out").reshape(N_DEV, H // N_DEV, D)
    gp_mi, rv_mo = _wgrad(h2, du, D, 2, min(2 * bt, T), "wgrad_mlp_in", owners=N_DEV, scatter=gp_mo)
    gp_in, gp_out, rv_mi, acc1_g, acc2_g, gcw_g, gwp_g = _wgrad_pair(
        h, dz, ycat, dmix, bt, "wgrad_in_out", N_DEV, gp_mi, [acc1, acc2, gcw, gwp])
    rv_out, rv_in = _reduce_scatter([gp_out.reshape(N_DEV, D // N_DEV, D), gp_in], "scatter_w_in_out")
    g_b_ada, g_w_ada, g_ln1_g, g_ln1_b, g_ln2_g, g_ln2_b, g_cw, g_pool_scale, g_w_pool, loss, *u_w_ada = _small_grads(
        acc1_g, acc2_g, gcw_g, gwp_g, c_all.T, jnp.reshape(me, (1,)).astype(jnp.int32), w_ada, m_w_ada, v_w_ada)
    cc_mine = conv_w.shape[2]
    g_cw3 = lax.dynamic_slice(g_cw, (0, me * cc_mine), (conv_w.shape[1], cc_mine))[:, None, :]
    g_conv_w = jnp.transpose(g_cw3, (1, 0, 2))

    small = _adamw_multi(
        [
            (b_ada, g_b_ada, m_b_ada, v_b_ada),
            (cw3, g_cw3, m_cw3, v_cw3),
            (w_pool, g_w_pool, m_w_pool, v_w_pool),
            (pool_scale, g_pool_scale, m_pool_scale, v_pool_scale),
            (ln1_g, g_ln1_g, m_ln1_g, v_ln1_g),
            (ln1_b, g_ln1_b, m_ln1_b, v_ln1_b),
            (ln2_g, g_ln2_g, m_ln2_g, v_ln2_g),
            (ln2_b, g_ln2_b, m_ln2_b, v_ln2_b),
        ],
        "adamw_small")
    u_b_ada, u_cw3, u_w_pool, u_pool_scale, u_ln1_g, u_ln1_b, u_ln2_g, u_ln2_b = small
    u_conv_w = tuple(jnp.transpose(t, (1, 0, 2)) for t in u_cw3)

    g_w_mo, *u_w_mo = _sum_adamw(rv_mo, w_mlp_out, m_w_mlp_out, v_w_mlp_out, "sum_w_mlp_out")
    g_w_mi, *u_w_mi = _sum_adamw(rv_mi, w_mlp_in, m_w_mlp_in, v_w_mlp_in, "sum_w_mlp_in")
    g_w_in, *u_w_in = _sum_adamw(rv_in, w_in, m_w_in, v_w_in, "sum_w_in")
    g_w_out, *u_w_out = _sum_adamw(rv_out, w_out, m_w_out, v_w_out, "sum_w_out")

    grads = [g_w_ada, g_b_ada, g_w_in, g_conv_w, g_w_pool, g_pool_scale, g_w_out, g_ln1_g, g_ln1_b, g_w_mi, g_w_mo, g_ln2_g, g_ln2_b]
    updates = [u_w_ada, u_b_ada, u_w_in, u_conv_w, u_w_pool, u_pool_scale, u_w_out, u_ln1_g, u_ln1_b, u_w_mi, u_w_mo, u_ln2_g, u_ln2_b]
    deltas = [u[0] for u in updates]
    new_m = [u[1] for u in updates]
    new_v = [u[2] for u in updates]
    return (loss.reshape(()), grad_x[None], *grads, *deltas, *new_m, *new_v)
```

```python
import jax
import jax.numpy as jnp
from jax import lax
from jax.experimental import pallas as pl
from jax.experimental.pallas import tpu as pltpu

F32 = jnp.float32
BF16 = jnp.bfloat16
MESH = pl.DeviceIdType.MESH
N_DEV = 8

LN_EPS = 1e-5
DEPTH = 1
DEEPNORM_ALPHA = (2.0 * DEPTH) ** 0.25
POOL_WINDOWS = (2, 4, 8, 16)
HALO = 16

ADAM_LR = 0.001
ADAM_B1 = 0.9
ADAM_B2 = 0.999
ADAM_EPS = 1e-08
ADAM_WD = 0.01
ADAM_STEP = 10

VMEM_LIMIT = 60 * 1024 * 1024

VMEM_SPEC = pl.BlockSpec(memory_space=pltpu.VMEM)
ANY_SPEC = pl.BlockSpec(memory_space=pl.ANY)

NT = (((1,), (1,)), ((), ()))
TN = (((0,), (0,)), ((), ()))


def _my_place():
    return lax.axis_index("x"), lax.axis_index("y"), lax.axis_index("c")


def _slot(x, y, c):
    return 4 * x + 2 * y + c


def _gather_place(ins, outs, a, slot):
    if len(outs[a].shape) == len(ins[a].shape):
        wb = ins[a].shape[1]
        return outs[a].at[:, pl.ds(pl.multiple_of(slot * wb, wb), wb)]
    return outs[a].at[slot]


def _gather_copy(ins, outs, sems, a, k, block, to, from_shard=False):
    send_sems, recv_sems, _ = sems
    dst = _gather_place(ins, outs, a, _slot(*block))
    return pltpu.make_async_remote_copy(
        src_ref=ins[a] if from_shard else dst,
        dst_ref=dst,
        send_sem=send_sems.at[7 * a + k],
        recv_sem=recv_sems.at[7 * a + k],
        device_id=to,
        device_id_type=MESH,
    )


def _gather_peers():
    x, y, c = _my_place()
    return (x, y, c), (x, y, 1 - c), [(1 - x, y), (x, 1 - y), (1 - x, 1 - y)]


def _gather_first(ins, outs, sems):
    me, sibling, chips = _gather_peers()
    first = []
    for a in range(len(ins)):
        first.append(_gather_copy(ins, outs, sems, a, 0, me, sibling, from_shard=True))
        first += [_gather_copy(ins, outs, sems, a, 1 + j, me, (*chip, me[2]), from_shard=True) for j, chip in enumerate(chips)]
    return first


def _gather_mine(ins, outs, sems, a):
    me, _, _ = _gather_peers()
    return pltpu.make_async_copy(ins[a], _gather_place(ins, outs, a, _slot(*me)), sems[2].at[a])


def _gather_start(ins, outs, sems):
    for a in range(len(ins)):
        _gather_mine(ins, outs, sems, a).start()
    for cp in _gather_first(ins, outs, sems):
        cp.start()


def _gather_forward(ins, outs, sems, j):
    me, sibling, chips = _gather_peers()
    for a in range(len(ins)):
        _gather_copy(ins, outs, sems, a, 1 + j, (*chips[j], me[2]), me).wait_recv()
        _gather_copy(ins, outs, sems, a, 4 + j, (*chips[j], me[2]), sibling).start()


def _gather_finish(ins, outs, sems):
    me, sibling, chips = _gather_peers()
    for a in range(len(ins)):
        _gather_copy(ins, outs, sems, a, 0, sibling, me).wait_recv()
        for j, chip in enumerate(chips):
            _gather_copy(ins, outs, sems, a, 4 + j, (*chip, 1 - me[2]), me).wait_recv()
    for cp in _gather_first(ins, outs, sems):
        cp.wait_send()
    for a in range(len(ins)):
        for j, chip in enumerate(chips):
            _gather_copy(ins, outs, sems, a, 4 + j, (*chip, me[2]), sibling).wait_send()
        _gather_mine(ins, outs, sems, a).wait()


def _gather_scratch(n):
    return [pltpu.SemaphoreType.DMA((7 * n,)), pltpu.SemaphoreType.DMA((7 * n,)), pltpu.SemaphoreType.DMA((n,))]


def _gather_out_shape(shards, by_cols):
    return [
        jax.ShapeDtypeStruct((s.shape[0], N_DEV * s.shape[1]) if cols else (N_DEV, *s.shape), s.dtype)
        for s, cols in zip(shards, by_cols)
    ]


N_CHIP = 4


def _scatter_scratch(rows, cols):
    block = pltpu.VMEM((N_CHIP, rows, cols), BF16)
    dma = pltpu.SemaphoreType.DMA
    return [block, block, block, dma((N_CHIP,)), dma((N_CHIP,)), dma((N_CHIP,)), dma((N_CHIP - 1,)), dma((N_CHIP - 1,)), dma]


def _scatter_pair_copies(g_hbm, scr):
    x, y, c = _my_place()
    mine, theirs, _, a_send, a_recv, load_sem = scr[:6]
    to_sibling = [
        pltpu.make_async_remote_copy(
            src_ref=g_hbm.at[2 * q + (1 - c)], dst_ref=theirs.at[q], send_sem=a_send.at[q], recv_sem=a_recv.at[q],
            device_id=(x, y, 1 - c), device_id_type=MESH)
        for q in range(N_CHIP)
    ]
    loads = [pltpu.make_async_copy(g_hbm.at[2 * q + c], mine.at[q], load_sem.at[q]) for q in range(N_CHIP)]
    return to_sibling, loads


def _scatter_sum_copies(recv, scr):
    x, y, c = _my_place()
    sums, b_send, b_recv, own_sem = scr[2], scr[6], scr[7], scr[8]
    q_me = 2 * x + y
    to_owner = [
        pltpu.make_async_remote_copy(
            src_ref=sums.at[2 * px + py], dst_ref=recv.at[q_me], send_sem=b_send.at[j], recv_sem=b_recv.at[j],
            device_id=(px, py, c), device_id_type=MESH)
        for j, (px, py) in enumerate([(1 - x, y), (x, 1 - y), (1 - x, 1 - y)])
    ]
    return to_owner, pltpu.make_async_copy(sums.at[q_me], recv.at[q_me], own_sem)


def _scatter_start(g_hbm, scr):
    to_sibling, loads = _scatter_pair_copies(g_hbm, scr)
    for cp in to_sibling + loads:
        cp.start()


def _scatter_middle(g_hbm, recv, scr):
    to_sibling, loads = _scatter_pair_copies(g_hbm, scr)
    for cp in to_sibling:
        cp.wait_recv()
    for cp in loads:
        cp.wait()
    mine, theirs, sums = scr[:3]

    def step(r, carry):
        rs = pl.ds(pl.multiple_of(r * ROW_CHUNK, ROW_CHUNK), ROW_CHUNK)
        for q in range(N_CHIP):
            sums[q, rs, :] = (mine[q, rs, :].astype(F32) + theirs[q, rs, :].astype(F32)).astype(BF16)
        return carry

    lax.fori_loop(0, mine.shape[1] // ROW_CHUNK, step, 0)
    to_owner, own = _scatter_sum_copies(recv, scr)
    for cp in to_owner + [own]:
        cp.start()


def _scatter_finish(g_hbm, recv, scr):
    to_sibling, _ = _scatter_pair_copies(g_hbm, scr)
    to_owner, own = _scatter_sum_copies(recv, scr)
    for cp in to_owner:
        cp.wait_recv()
    for cp in to_sibling + to_owner:
        cp.wait_send()
    own.wait()


def _scatter_out_shape(gparts):
    return jax.ShapeDtypeStruct((N_CHIP, *gparts.shape[1:]), gparts.dtype)


def _adamw_math(w, g, m, v):
    m = ADAM_B1 * m + (1.0 - ADAM_B1) * g
    v = ADAM_B2 * v + (1.0 - ADAM_B2) * (g * g)
    m_hat = m / (1.0 - ADAM_B1**ADAM_STEP)
    v_hat = v / (1.0 - ADAM_B2**ADAM_STEP)
    delta = -ADAM_LR * (m_hat / (jnp.sqrt(v_hat) + ADAM_EPS) + ADAM_WD * w)
    return delta, m, v


ROW_CHUNK = 64


ELEMS_PER_STEP = 128 * 1024


N_SCATTER_SCRATCH = 9


def _reduce_scatter(gparts, name):
    n = len(gparts)

    def body(*refs):
        g_hbm, recv = refs[:n], refs[n : 2 * n]
        scr = [refs[2 * n + k * N_SCATTER_SCRATCH : 2 * n + (k + 1) * N_SCATTER_SCRATCH] for k in range(n)]
        for k in range(n):
            _scatter_start(g_hbm[k], scr[k])
        for k in range(n):
            _scatter_middle(g_hbm[k], recv[k], scr[k])
        for k in range(n):
            _scatter_finish(g_hbm[k], recv[k], scr[k])

    return pl.pallas_call(
        body,
        name=name,
        out_shape=[_scatter_out_shape(g) for g in gparts],
        in_specs=[ANY_SPEC] * n,
        out_specs=[ANY_SPEC] * n,
        scratch_shapes=[s for g in gparts for s in _scatter_scratch(*g.shape[1:])],
        compiler_params=pltpu.CompilerParams(vmem_limit_bytes=VMEM_LIMIT),
    )(*gparts)


def _sum_adamw(parts, w, m, v, name):
    _, rows, cols = w.shape
    rb = rows
    while rb * cols > ELEMS_PER_STEP and rb % 16 == 0:
        rb //= 2

    def body(p_ref, w_ref, m_ref, v_ref, grad_ref, delta_ref, nm_ref, nv_ref):
        g = p_ref[0].astype(F32)
        for k in range(1, p_ref.shape[0]):
            g = g + p_ref[k].astype(F32)
        delta, nm, nv = _adamw_math(w_ref[0], g, m_ref[0], v_ref[0])
        grad_ref[0] = g
        delta_ref[0] = delta
        nm_ref[0] = nm
        nv_ref[0] = nv

    block = lambda lead: pl.BlockSpec((lead, rb, cols), lambda i: (0, i, 0))
    out = jax.ShapeDtypeStruct(w.shape, F32)
    return pl.pallas_call(
        body,
        name=name,
        grid=(rows // rb,),
        out_shape=[out] * 4,
        in_specs=[block(parts.shape[0])] + [block(1)] * 3,
        out_specs=[block(1)] * 4,
        compiler_params=pltpu.CompilerParams(dimension_semantics=("arbitrary",), vmem_limit_bytes=VMEM_LIMIT),
    )(parts, w, m, v)


def _adamw_multi(items, name):
    n = len(items)

    def body(*refs):
        ins, outs = refs[: 4 * n], refs[4 * n :]
        for a in range(n):
            w_ref, g_ref, m_ref, v_ref = ins[4 * a : 4 * a + 4]
            d_ref, nm_ref, nv_ref = outs[3 * a : 3 * a + 3]
            delta, nm, nv = _adamw_math(w_ref[...], g_ref[...], m_ref[...], v_ref[...])
            d_ref[...] = delta
            nm_ref[...] = nm
            nv_ref[...] = nv

    flat = [a for it in items for a in it]
    out_shape = [jax.ShapeDtypeStruct(it[0].shape, F32) for it in items for _ in range(3)]
    outs = pl.pallas_call(
        body,
        name=name,
        out_shape=out_shape,
        in_specs=[VMEM_SPEC] * (4 * n),
        out_specs=[VMEM_SPEC] * (3 * n),
        compiler_params=pltpu.CompilerParams(vmem_limit_bytes=VMEM_LIMIT),
    )(*flat)
    return [tuple(outs[3 * a : 3 * a + 3]) for a in range(n)]


def _prologue(w_in, w_out, conv_w, c, w_ada, b_ada):
    D = c.shape[1]
    wc = w_ada.shape[2]
    shards16 = [jax.ShapeDtypeStruct(w_in.shape[1:], BF16), jax.ShapeDtypeStruct(w_out.shape[1:], BF16)]

    def to_all(src, out, sems):
        x, y, c = _my_place()
        me = _slot(x, y, c)
        copies = [
            pltpu.make_async_remote_copy(
                src_ref=src, dst_ref=out.at[me], send_sem=sems[0].at[k - 1], recv_sem=sems[1].at[k - 1],
                device_id=(x ^ (k >> 2), y ^ ((k >> 1) & 1), c ^ (k & 1)), device_id_type=MESH)
            for k in range(1, N_DEV)
        ]
        return copies, pltpu.make_async_copy(src, out.at[me], sems[2].at[0])

    def start(copies, own):
        for cp in copies + [own]:
            cp.start()

    def finish(copies, own):
        for cp in copies:
            cp.wait()
        own.wait()

    def body(win_ref, wout_ref, cw_ref, c_ref, wada_ref, b_ref, win_g, wout_g, cw_g, c_g, mod_g, mod_ref, win16, wout16, c_s, mp_s, *sems):
        c_copies = to_all(c_ref, c_g, sems[3:6])
        start(*c_copies)
        win16[...] = win_ref[0].astype(BF16)
        wout16[...] = wout_ref[0].astype(BF16)
        w_ins, w_outs, w_sems = (win16, wout16, cw_ref), (win_g, wout_g, cw_g), sems[0:3]
        _gather_start(w_ins, w_outs, w_sems)
        finish(*c_copies)
        for k in range(N_DEV):
            c_s[k : k + 1, :] = c_g[k]
        cv = c_s[...]
        cond = cv * jax.nn.sigmoid(cv)
        b_mine = b_ref[:, pl.ds(pl.multiple_of(_slot(*_my_place()) * wc, 128), wc)]
        mp_s[...] = jnp.dot(cond, wada_ref[0], precision=lax.Precision.HIGHEST, preferred_element_type=F32) + b_mine
        m_copies = to_all(mp_s, mod_g, sems[6:9])
        start(*m_copies)
        for j in range(3):
            _gather_forward(w_ins, w_outs, w_sems, j)
        _gather_finish(w_ins, w_outs, w_sems)
        finish(*m_copies)
        me = _slot(*_my_place())
        for p in range(N_DEV):
            part = mod_g[p, pl.ds(me, 1), :]
            lo = 0
            while lo < wc:
                r, col = divmod(p * wc + lo, D)
                n = min(wc - lo, D - col)
                mod_ref[r : r + 1, col : col + n] = part[:, lo : lo + n]
                lo += n

    return pl.pallas_call(
        body,
        name="prologue",
        out_shape=_gather_out_shape(shards16 + [conv_w], [True, False, False])
        + [jax.ShapeDtypeStruct((N_DEV, 1, D), F32), jax.ShapeDtypeStruct((N_DEV, N_DEV, wc), F32),
           jax.ShapeDtypeStruct((N_DEV * wc // D, D), F32)],
        in_specs=[VMEM_SPEC, VMEM_SPEC, ANY_SPEC] + [VMEM_SPEC] * 3,
        out_specs=[ANY_SPEC] * 3 + [VMEM_SPEC] * 3,
        scratch_shapes=[pltpu.VMEM(s.shape, BF16) for s in shards16]
        + [pltpu.VMEM((N_DEV, D), F32), pltpu.VMEM((N_DEV, wc), F32)]
        + _gather_scratch(3) + _gather_scratch(1) + _gather_scratch(1),
        compiler_params=pltpu.CompilerParams(vmem_limit_bytes=VMEM_LIMIT),
    )(w_in, w_out, conv_w, c, w_ada, b_ada)


def _ln_fwd(r):
    mu = jnp.mean(r, axis=-1, keepdims=True)
    d = r - mu
    var = jnp.mean(d * d, axis=-1, keepdims=True)
    rstd = lax.rsqrt(var + LN_EPS)
    return d * rstd, rstd


def _ln_bwd(dxh, xhat, rstd):
    m1 = jnp.mean(dxh, axis=-1, keepdims=True)
    m2 = jnp.mean(dxh * xhat, axis=-1, keepdims=True)
    return rstd * (dxh - m1 - xhat * m2)


def _colsum(a):
    return jnp.sum(a, axis=0, keepdims=True)


def _window_sums(ext, tm, causal):
    n = ext.shape[0]
    lo = HALO if causal else 0
    s, out = ext, []
    for p in range(len(POOL_WINDOWS)):
        assert POOL_WINDOWS[p] == 2 ** (p + 1)
        k = 2**p
        s = s + pltpu.roll(s, k if causal else n - k, 0)
        out.append(s[lo : lo + tm, 0:128])
        if p + 1 < len(POOL_WINDOWS):
            s = s[:, 128:]
    return out


def _pool_features(vp, vp_s, row, tm):
    sums = _window_sums(vp_s[...], tm, causal=True)
    feats, inv_cnts = [], []
    for g, win in enumerate(POOL_WINDOWS):
        inv_cnt = 1.0 / jnp.minimum(row + 1, win).astype(F32)
        feats.append(sums[g] * inv_cnt - vp[:, 128 * g : 128 * g + 128])
        inv_cnts.append(inv_cnt)
    return feats, inv_cnts


def _f1(x, mod, w_in, conv_w, w_pool, pool_scale, w_out, tm, gather, by_cols):
    T, D = x.shape
    ZW = w_in.shape[1]
    CC = ZW // 4
    nt = T // tm
    ng = len(gather)
    fwd_steps = [max(nt - 3 + j, 0) for j in range(3)]

    shards16 = [jax.ShapeDtypeStruct(s.shape[1:], BF16) for s in gather]

    def body(*refs):
        x_ref, mod_ref, win_ref, cw_ref, wp_ref, ps_ref, wout_ref = refs[:7]
        g_f32 = refs[7 : 7 + ng]
        z_ref, h_ref, xhat_ref, rstd_ref, mix_ref = refs[7 + ng : 12 + ng]
        g_outs = refs[12 + ng : 12 + 2 * ng]
        cv_s, vp_s = refs[12 + 2 * ng : 14 + 2 * ng]
        g_ins = refs[14 + 2 * ng : 14 + 3 * ng]
        g_sems = refs[14 + 3 * ng :]
        i = pl.program_id(0)

        @pl.when(i == 0)
        def _():
            for src, dst in zip(g_f32, g_ins):
                dst[...] = src[0].astype(BF16)
            _gather_start(g_ins, g_outs, g_sems)
            cv_s[0:HALO, :] = jnp.zeros((HALO, CC), F32)
            vp_s[0:HALO, :] = jnp.zeros((HALO, CC), F32)

        xv = x_ref[...]
        sh1, sc1, g1 = mod_ref[0:1, :], mod_ref[1:2, :], mod_ref[2:3, :]
        h = (xv * (1.0 + sc1) + sh1).astype(BF16)
        h_ref[...] = h
        z = jnp.dot(h, win_ref[...], preferred_element_type=F32)
        z_ref[...] = z.astype(BF16)
        gb, gc, vc, vp = z[:, 0:CC], z[:, CC : 2 * CC], z[:, 2 * CC : 3 * CC], z[:, 3 * CC : 4 * CC]
        cv = gc * vc
        cv_s[HALO : HALO + tm, :] = cv
        vp_s[HALO : HALO + tm, :] = vp
        cv_ext = cv_s[...]
        cv_m2 = pltpu.roll(cv_ext, 2, 0)[HALO : HALO + tm, :]
        cv_m1 = pltpu.roll(cv_ext, 1, 0)[HALO : HALO + tm, :]
        conv = cw_ref[0:1, :] * cv_m2 + cw_ref[1:2, :] * cv_m1 + cw_ref[2:3, :] * cv
        parts = [gb * conv]
        row = i * tm + lax.broadcasted_iota(jnp.int32, (tm, 1), 0)
        feats, _ = _pool_features(vp, vp_s, row, tm)
        for g in range(len(POOL_WINDOWS)):
            pw = jnp.dot(feats[g].astype(BF16), wp_ref[g].astype(BF16), preferred_element_type=F32)
            parts.append(pw * ps_ref[0:1, 128 * g : 128 * g + 128])
        cv_s[0:HALO, :] = cv_s[tm : tm + HALO, :]
        vp_s[0:HALO, :] = vp_s[tm : tm + HALO, :]
        ycat = jnp.concatenate(parts, axis=1).astype(BF16)
        mix = jnp.dot(ycat, wout_ref[...], preferred_element_type=F32)
        mix_ref[...] = mix
        xhat, rstd = _ln_fwd(DEEPNORM_ALPHA * xv + (1.0 + g1) * mix)
        xhat_ref[...] = xhat
        rstd_ref[...] = rstd

        for j in range(3):

            @pl.when(i == fwd_steps[j])
            def _(j=j):
                _gather_forward(g_ins, g_outs, g_sems, j)

        @pl.when(i == nt - 1)
        def _():
            _gather_finish(g_ins, g_outs, g_sems)

    tile = lambda w: pl.BlockSpec((tm, w), lambda i: (i, 0))
    return pl.pallas_call(
        body,
        name="f1",
        grid=(nt,),
        out_shape=[
            jax.ShapeDtypeStruct((T, ZW), BF16),
            jax.ShapeDtypeStruct((T, D), BF16),
            jax.ShapeDtypeStruct((T, D), F32),
            jax.ShapeDtypeStruct((T, 1), F32),
            jax.ShapeDtypeStruct((T, D), F32),
        ]
        + _gather_out_shape(shards16, by_cols),
        in_specs=[tile(D)] + [VMEM_SPEC] * (6 + ng),
        out_specs=[tile(ZW), tile(D), tile(D), tile(1), tile(D)] + [ANY_SPEC] * ng,
        scratch_shapes=[pltpu.VMEM((HALO + tm, CC), F32), pltpu.VMEM((HALO + tm, CC), F32)]
        + [pltpu.VMEM(s.shape, BF16) for s in shards16]
        + _gather_scratch(ng),
        compiler_params=pltpu.CompilerParams(dimension_semantics=("arbitrary",), vmem_limit_bytes=VMEM_LIMIT),
    )(x, mod, w_in, conv_w, w_pool, pool_scale, w_out, *gather)


def _fb2(xhat1, target, mod, ln, w_mi, w_mo, tm):
    T, D = xhat1.shape
    H = w_mi.shape[1]
    hc = min(1024, H)
    nb = H // hc
    nt = T // tm

    def body(xh1_ref, t_ref, mod_ref, ln_ref, wmi_ref, wmo_ref, dx1_ref, h2_ref, a_ref, du_ref, df_ref, acc_ref):
        i = pl.program_id(0)

        @pl.when(i == 0)
        def _():
            acc_ref[...] = jnp.zeros((8, D), F32)

        sh2, sc2, g2 = mod_ref[3:4, :], mod_ref[4:5, :], mod_ref[5:6, :]
        x1 = xh1_ref[...] * ln_ref[0:1, :] + ln_ref[1:2, :]
        h2 = (x1 * (1.0 + sc2) + sh2).astype(BF16)
        h2_ref[...] = h2
        f = jnp.zeros((tm, D), F32)
        for k in range(nb):
            ks = slice(k * hc, (k + 1) * hc)
            r = jnp.maximum(jnp.dot(h2, wmi_ref[:, ks], preferred_element_type=F32), 0.0)
            du_ref[:, ks] = r.astype(BF16)
            a = (r * r).astype(BF16)
            a_ref[:, ks] = a
            f = f + jnp.dot(a, wmo_ref[ks, :], preferred_element_type=F32)
        xhat2, rstd2 = _ln_fwd(DEEPNORM_ALPHA * x1 + (1.0 + g2) * f)
        ln2_g = ln_ref[2:3, :]
        d = xhat2 * ln2_g + ln_ref[3:4, :] - t_ref[...]
        dr2 = _ln_bwd(d * (ln2_g * (1.0 / D)), xhat2, rstd2)
        df = ((1.0 + g2) * dr2).astype(BF16)
        df_ref[...] = df
        dh2 = jnp.zeros((tm, D), F32)
        for k in range(nb):
            ks = slice(k * hc, (k + 1) * hc)
            da = lax.dot_general(df, wmo_ref[ks, :], NT, preferred_element_type=F32)
            du = (da * (2.0 * du_ref[:, ks].astype(F32))).astype(BF16)
            du_ref[:, ks] = du
            dh2 = dh2 + lax.dot_general(du, wmi_ref[:, ks], NT, preferred_element_type=F32)
        dx1_ref[...] = DEEPNORM_ALPHA * dr2 + dh2 * (1.0 + sc2)
        acc_ref[0:1, :] += _colsum(d * xhat2) * (1.0 / D)
        acc_ref[1:2, :] += _colsum(d) * (1.0 / D)
        acc_ref[2:3, :] += _colsum(dh2)
        acc_ref[3:4, :] += _colsum(dh2 * x1)
        acc_ref[4:5, :] += _colsum(dr2 * f)
        acc_ref[5:6, :] += jnp.zeros((1, D), F32) + (0.5 / D) * jnp.sum(d * d)

    tile = lambda w: pl.BlockSpec((tm, w), lambda i: (i, 0))
    return pl.pallas_call(
        body,
        name="fb2",
        grid=(nt,),
        out_shape=[
            jax.ShapeDtypeStruct((T, D), F32),
            jax.ShapeDtypeStruct((T, D), BF16),
            jax.ShapeDtypeStruct((T, H), BF16),
            jax.ShapeDtypeStruct((T, H), BF16),
            jax.ShapeDtypeStruct((T, D), BF16),
            jax.ShapeDtypeStruct((8, D), F32),
        ],
        in_specs=[tile(D), tile(D)] + [VMEM_SPEC] * 4,
        out_specs=[tile(D), tile(D), tile(H), tile(H), tile(D), pl.BlockSpec((8, D), lambda i: (0, 0))],
        compiler_params=pltpu.CompilerParams(dimension_semantics=("arbitrary",), vmem_limit_bytes=VMEM_LIMIT),
    )(xhat1, target, mod, ln, w_mi, w_mo)


def _b1(dx1, xhat1, rstd1, x, mix, z, mod, ln, w_out, w_in, conv_w, w_pool, pool_scale, tm):
    T, D = x.shape
    ZW = w_in.shape[1]
    CC = ZW // 4
    nt = T // tm
    hb = tm // HALO

    def body(dx1_ref, xh1_ref, rstd_ref, x_ref, mix_ref, z_ref, zh_ref, mod_ref, ln_ref, wout_ref, win_ref, cw_ref, wp_ref, ps_ref,
             dx_ref, dmix_ref, ycat_ref, dz_ref, acc_ref, gcw_ref, gwp_ref, cv_s, vp_s, e_s, q_s):
        i = pl.program_id(0)
        j = nt - 1 - i

        @pl.when(i == 0)
        def _():
            acc_ref[...] = jnp.zeros((8, D), F32)
            gcw_ref[...] = jnp.zeros((8, CC), F32)
            gwp_ref[...] = jnp.zeros(gwp_ref.shape, F32)
            e_s[tm : tm + HALO, :] = jnp.zeros((HALO, CC), F32)
            q_s[tm : tm + HALO, :] = jnp.zeros((HALO, CC), F32)

        sh1, sc1, g1 = mod_ref[0:1, :], mod_ref[1:2, :], mod_ref[2:3, :]
        dx1 = dx1_ref[...]
        xhat1 = xh1_ref[...]
        acc_ref[0:1, :] += _colsum(dx1 * xhat1)
        acc_ref[1:2, :] += _colsum(dx1)
        dr1 = _ln_bwd(dx1 * ln_ref[0:1, :], xhat1, rstd_ref[...])
        acc_ref[4:5, :] += _colsum(dr1 * mix_ref[...])
        dmix = ((1.0 + g1) * dr1).astype(BF16)
        dmix_ref[...] = dmix
        dycat = lax.dot_general(dmix, wout_ref[...], NT, preferred_element_type=F32)

        z = z_ref[...].astype(F32)
        zh = zh_ref[...].astype(F32) * jnp.where(j > 0, 1.0, 0.0)
        gb, gc, vc, vp = z[:, 0:CC], z[:, CC : 2 * CC], z[:, 2 * CC : 3 * CC], z[:, 3 * CC : 4 * CC]
        cv = gc * vc
        cv_s[0:HALO, :] = zh[:, CC : 2 * CC] * zh[:, 2 * CC : 3 * CC]
        cv_s[HALO : HALO + tm, :] = cv
        vp_s[0:HALO, :] = zh[:, 3 * CC : 4 * CC]
        vp_s[HALO : HALO + tm, :] = vp
        cv_ext = cv_s[...]
        cv_m2 = pltpu.roll(cv_ext, 2, 0)[HALO : HALO + tm, :]
        cv_m1 = pltpu.roll(cv_ext, 1, 0)[HALO : HALO + tm, :]
        w0, w1, w2 = cw_ref[0:1, :], cw_ref[1:2, :], cw_ref[2:3, :]
        conv = w0 * cv_m2 + w1 * cv_m1 + w2 * cv
        dyc = dycat[:, 0:CC]
        e = dyc * gb
        e_s[0:tm, :] = e
        e_ext = e_s[...]
        dcv = w2 * e + w1 * pltpu.roll(e_ext, tm + HALO - 1, 0)[0:tm, :] + w0 * pltpu.roll(e_ext, tm + HALO - 2, 0)[0:tm, :]
        gcw_ref[0:1, :] += _colsum(e * cv_m2)
        gcw_ref[1:2, :] += _colsum(e * cv_m1)
        gcw_ref[2:3, :] += _colsum(e * cv)
        y_parts = [gb * conv]
        dz_parts = [dyc * conv, dcv * vc, dcv * gc]

        row = j * tm + lax.broadcasted_iota(jnp.int32, (tm, 1), 0)
        feats, inv_cnts = _pool_features(vp, vp_s, row, tm)
        gps_parts, dps = [], []
        for g in range(len(POOL_WINDOWS)):
            cols = slice(128 * g, 128 * g + 128)
            p = feats[g].astype(BF16)
            scale = ps_ref[0:1, cols]
            wp = wp_ref[g].astype(BF16)
            pw = jnp.dot(p, wp, preferred_element_type=F32)
            y_parts.append(pw * scale)
            dyp = dycat[:, CC + 128 * g : CC + 128 * g + 128]
            gps_parts.append(_colsum(dyp * pw))
            dpw = (dyp * scale).astype(BF16)
            gwp_ref[g] += lax.dot_general(p, dpw, TN, preferred_element_type=F32)
            dp = lax.dot_general(dpw, wp, NT, preferred_element_type=F32)
            q_s[0:tm, cols] = dp * inv_cnts[g]
            dps.append(dp)
        sq = _window_sums(q_s[...], tm, causal=False)
        dz_parts += [sq[g] - dps[g] for g in range(len(POOL_WINDOWS))]
        gcw_ref[3:4, :] += jnp.concatenate(gps_parts, axis=1)
        ycat_ref[...] = jnp.concatenate(y_parts, axis=1).astype(BF16)
        dz = jnp.concatenate(dz_parts, axis=1).astype(BF16)
        dz_ref[...] = dz
        dh = lax.dot_general(dz, win_ref[...], NT, preferred_element_type=F32)
        acc_ref[2:3, :] += _colsum(dh)
        acc_ref[3:4, :] += _colsum(dh * x_ref[...])
        dx_ref[...] = DEEPNORM_ALPHA * dr1 + dh * (1.0 + sc1)
        e_s[tm : tm + HALO, :] = e_s[0:HALO, :]
        q_s[tm : tm + HALO, :] = q_s[0:HALO, :]

    tile = lambda w: pl.BlockSpec((tm, w), lambda i: (nt - 1 - i, 0))
    halo = pl.BlockSpec((HALO, ZW), lambda i: (jnp.maximum((nt - 1 - i) * hb - 1, 0), 0))
    fixed = lambda shape: pl.BlockSpec(shape, lambda i: (0,) * len(shape))
    return pl.pallas_call(
        body,
        name="b1",
        grid=(nt,),
        out_shape=[
            jax.ShapeDtypeStruct((T, D), F32),
            jax.ShapeDtypeStruct((T, D), BF16),
            jax.ShapeDtypeStruct((T, D), BF16),
            jax.ShapeDtypeStruct((T, ZW), BF16),
            jax.ShapeDtypeStruct((8, D), F32),
            jax.ShapeDtypeStruct((8, CC), F32),
            jax.ShapeDtypeStruct(w_pool.shape, F32),
        ],
        in_specs=[tile(D), tile(D), tile(1), tile(D), tile(D), tile(ZW), halo] + [VMEM_SPEC] * 7,
        out_specs=[tile(D), tile(D), tile(D), tile(ZW), fixed((8, D)), fixed((8, CC)), fixed(w_pool.shape)],
        scratch_shapes=[
            pltpu.VMEM((HALO + tm, CC), F32),
            pltpu.VMEM((HALO + tm, CC), F32),
            pltpu.VMEM((tm + HALO, CC), F32),
            pltpu.VMEM((tm + HALO, CC), F32),
        ],
        compiler_params=pltpu.CompilerParams(dimension_semantics=("arbitrary",), vmem_limit_bytes=VMEM_LIMIT),
    )(dx1, xhat1, rstd1, x, mix, z, z, mod, ln, w_out, w_in, conv_w, w_pool, pool_scale)


def _wgrad(a, b, bk, n_groups, bt, name, owners=None, scatter=None, gather=()):
    T, K = a.shape
    N = b.shape[1]
    nk, nt, ng = K // bk, T // bt, N // n_groups
    nc = min(512, ng)
    ns, ngat = (0 if scatter is None else 1), len(gather)
    n_steps = nk * n_groups * nt
    mid_step = min(1, n_steps - 1)
    fwd_steps = [min(2 * (j + 1), n_steps - 1) for j in range(3)]

    def body(*refs):
        a_ref, b_ref = refs[0], refs[1]
        g_ins = refs[2 + ns : 2 + ns + ngat]
        outs = refs[2 + ns + ngat :]
        o_ref, g_outs = outs[0], outs[1 + ns : 1 + ns + ngat]
        scr = outs[1 + ns + ngat :]
        acc = scr[0]
        if ns:
            s_hbm, s_recv, s_scr = refs[2], outs[1], scr[1 : 1 + N_SCATTER_SCRATCH]
        g_sems = scr[1 + N_SCATTER_SCRATCH * ns :]
        kk, gg, t = pl.program_id(0), pl.program_id(1), pl.program_id(2)
        step = (kk * n_groups + gg) * nt + t

        if ngat:

            @pl.when(step == 0)
            def _():
                _gather_start(g_ins, g_outs, g_sems)

            for j in range(3):

                @pl.when(step == fwd_steps[j])
                def _(j=j):
                    _gather_forward(g_ins, g_outs, g_sems, j)

        if ns:

            @pl.when(step == 0)
            def _():
                _scatter_start(s_hbm, s_scr)

            @pl.when(step == mid_step)
            def _():
                _scatter_middle(s_hbm, s_recv, s_scr)

        @pl.when(t == 0)
        def _():
            acc[...] = jnp.zeros(acc.shape, F32)

        at = a_ref[...].T
        for c in range(ng // nc):
            cs = slice(c * nc, (c + 1) * nc)
            acc[:, cs] += jnp.dot(at, b_ref[:, cs], preferred_element_type=F32)

        @pl.when(t == nt - 1)
        def _():
            if owners is None:
                o_ref[...] = acc[...].astype(BF16)
            else:
                per = N // owners
                for o in range(ng // per):
                    o_ref[o] = acc[:, o * per : (o + 1) * per].astype(BF16)

        if ns:

            @pl.when(step == n_steps - 1)
            def _():
                _scatter_finish(s_hbm, s_recv, s_scr)

        if ngat:

            @pl.when(step == n_steps - 1)
            def _():
                _gather_finish(g_ins, g_outs, g_sems)

    if owners is None:
        out_shape = [jax.ShapeDtypeStruct((K, N), BF16)]
        out_specs = [pl.BlockSpec((bk, ng), lambda k, g, t: (k, g))]
    else:
        assert bk == K
        per = N // owners
        out_shape = [jax.ShapeDtypeStruct((owners, K, per), BF16)]
        out_specs = [pl.BlockSpec((ng // per, K, per), lambda k, g, t: (g, 0, 0))]
    ins, in_specs = [a, b], [pl.BlockSpec((bt, bk), lambda k, g, t: (t, k)), pl.BlockSpec((bt, ng), lambda k, g, t: (t, g))]
    scratch = [pltpu.VMEM((bk, ng), F32)]
    if ns:
        ins.append(scatter)
        in_specs.append(ANY_SPEC)
        out_shape.append(_scatter_out_shape(scatter))
        out_specs.append(ANY_SPEC)
        scratch += _scatter_scratch(*scatter.shape[1:])
    if ngat:
        ins += list(gather)
        in_specs += [ANY_SPEC] * ngat
        out_shape += _gather_out_shape(gather, [False] * ngat)
        out_specs += [ANY_SPEC] * ngat
        scratch += _gather_scratch(ngat)
    outs = pl.pallas_call(
        body,
        name=name,
        grid=(nk, n_groups, nt),
        out_shape=out_shape,
        in_specs=in_specs,
        out_specs=out_specs,
        scratch_shapes=scratch,
        compiler_params=pltpu.CompilerParams(dimension_semantics=("arbitrary", "arbitrary", "arbitrary"), vmem_limit_bytes=VMEM_LIMIT),
    )(*ins)
    return outs if ns + ngat else outs[0]


def _wgrad_pair(a0, b0, a1, b1, bt, name, owners, scatter, gather):
    T, K0 = a0.shape
    N0 = b0.shape[1]
    K1, N1 = a1.shape[1], b1.shape[1]
    nt = T // bt
    nc = 512
    per = N0 // owners
    ngat = len(gather)
    n_steps = 2 * nt
    fwd_steps = [min(2 * (j + 1), n_steps - 1) for j in range(3)]

    def body(*refs):
        a0_ref, b0_ref, a1_ref, b1_ref, s_hbm = refs[:5]
        g_ins = refs[5 : 5 + ngat]
        o0_ref, o1_ref, s_recv = refs[5 + ngat : 8 + ngat]
        g_outs = refs[8 + ngat : 8 + 2 * ngat]
        scr = refs[8 + 2 * ngat :]
        acc0, acc1 = scr[0], scr[1]
        s_scr, g_sems = scr[2 : 2 + N_SCATTER_SCRATCH], scr[2 + N_SCATTER_SCRATCH :]
        p, t = pl.program_id(0), pl.program_id(1)
        step = p * nt + t

        @pl.when(step == 0)
        def _():
            _scatter_start(s_hbm, s_scr)
            _gather_start(g_ins, g_outs, g_sems)

        @pl.when(step == 1)
        def _():
            _scatter_middle(s_hbm, s_recv, s_scr)

        for j in range(3):

            @pl.when(step == fwd_steps[j])
            def _(j=j):
                _gather_forward(g_ins, g_outs, g_sems, j)

        def accumulate(a_ref, b_ref, acc, n_cols):
            @pl.when(t == 0)
            def _():
                acc[...] = jnp.zeros(acc.shape, F32)

            at = a_ref[...].T
            for c in range(n_cols // nc):
                cs = slice(c * nc, (c + 1) * nc)
                acc[:, cs] += jnp.dot(at, b_ref[:, cs], preferred_element_type=F32)

        @pl.when(p == 0)
        def _():
            accumulate(a0_ref, b0_ref, acc0, N0)

            @pl.when(t == nt - 1)
            def _():
                for o in range(owners):
                    o0_ref[o] = acc0[:, o * per : (o + 1) * per].astype(BF16)

        @pl.when(p == 1)
        def _():
            accumulate(a1_ref, b1_ref, acc1, N1)

            @pl.when(t == nt - 1)
            def _():
                o1_ref[...] = acc1[...].astype(BF16)

        @pl.when(step == n_steps - 1)
        def _():
            _scatter_finish(s_hbm, s_recv, s_scr)
            _gather_finish(g_ins, g_outs, g_sems)

    first = lambda w: pl.BlockSpec((bt, w), lambda p, t: (jnp.where(p == 0, t, nt - 1), 0))
    second = lambda w: pl.BlockSpec((bt, w), lambda p, t: (jnp.where(p == 1, t, 0), 0))
    return pl.pallas_call(
        body,
        name=name,
        grid=(2, nt),
        out_shape=[jax.ShapeDtypeStruct((owners, K0, per), BF16), jax.ShapeDtypeStruct((K1, N1), BF16), _scatter_out_shape(scatter)]
        + _gather_out_shape(gather, [False] * ngat),
        in_specs=[first(K0), first(N0), second(K1), second(N1), ANY_SPEC] + [ANY_SPEC] * ngat,
        out_specs=[pl.BlockSpec((owners, K0, per), lambda p, t: (0, 0, 0)), pl.BlockSpec((K1, N1), lambda p, t: (0, 0)), ANY_SPEC]
        + [ANY_SPEC] * ngat,
        scratch_shapes=[pltpu.VMEM((K0, N0), F32), pltpu.VMEM((K1, N1), F32)]
        + _scatter_scratch(*scatter.shape[1:]) + _gather_scratch(ngat),
        compiler_params=pltpu.CompilerParams(dimension_semantics=("arbitrary", "arbitrary"), vmem_limit_bytes=VMEM_LIMIT),
    )(a0, b0, a1, b1, scatter, *gather)


def _small_grads(acc1_all, acc2_all, gcw_all, gwp_all, cond_t, my_slot, w_ada, m_w_ada, v_w_ada):
    D = acc1_all.shape[2]
    w_cols = w_ada.shape[2]
    n_chunk = D // 128
    q_mine = w_cols // 128

    def total(ref, r):
        s = ref[0, r : r + 1, :]
        for k in range(1, N_DEV):
            s = s + ref[k, r : r + 1, :]
        return s

    def body(slot_ref, a1_ref, a2_ref, gcw_ref, gwp_ref, ct_ref, w_ref, m_ref, v_ref,
             gb_ref, gw_ref, gl1g_ref, gl1b_ref, gl2g_ref, gl2b_ref, gcwo_ref, gps_ref, gwpo_ref, loss_ref,
             dw_ref, nm_ref, nv_ref, dm_s):
        loss_ref[...] = total(a2_ref, 5)[:, 0:1]
        for s, (ref, r) in enumerate([(a1_ref, 2), (a1_ref, 3), (a1_ref, 4), (a2_ref, 2), (a2_ref, 3), (a2_ref, 4)]):
            gb_ref[0:1, s * D : (s + 1) * D] = total(ref, r)
            for k in range(N_DEV):
                row = ref[k, r : r + 1, :]
                for qq in range(n_chunk):
                    dm_s[s * n_chunk + qq, k : k + 1, :] = row[:, 128 * qq : 128 * qq + 128]
        gl1g_ref[...] = total(a1_ref, 0)
        gl1b_ref[...] = total(a1_ref, 1)
        gl2g_ref[...] = total(a2_ref, 0)
        gl2b_ref[...] = total(a2_ref, 1)
        gcwo_ref[...] = jnp.zeros(gcwo_ref.shape, F32)
        for r in range(3):
            gcwo_ref[r : r + 1, :] = total(gcw_ref, r)
        gps_ref[...] = total(gcw_ref, 3)
        wp = gwp_ref[0]
        for k in range(1, N_DEV):
            wp = wp + gwp_ref[k]
        gwpo_ref[0] = wp
        ct = ct_ref[...]
        cond_t = ct * jax.nn.sigmoid(ct)
        q0 = slot_ref[0] * q_mine
        for q in range(q_mine):
            dm = dm_s[q0 + q]
            out = cond_t[:, 0:1] * dm[0:1, :]
            for k in range(1, N_DEV):
                out = out + cond_t[:, k : k + 1] * dm[k : k + 1, :]
            cols = slice(128 * q, 128 * q + 128)
            gw_ref[0, :, cols] = out
            delta, nm, nv = _adamw_math(w_ref[0, :, cols], out, m_ref[0, :, cols], v_ref[0, :, cols])
            dw_ref[0, :, cols] = delta
            nm_ref[0, :, cols] = nm
            nv_ref[0, :, cols] = nv

    CC = gcw_all.shape[2]
    w_like = jax.ShapeDtypeStruct(w_ada.shape, F32)
    return pl.pallas_call(
        body,
        name="small_grads",
        out_shape=[
            jax.ShapeDtypeStruct((1, 6 * D), F32),
            w_like,
            *[jax.ShapeDtypeStruct((1, D), F32)] * 4,
            jax.ShapeDtypeStruct((8, CC), F32),
            jax.ShapeDtypeStruct((1, CC), F32),
            jax.ShapeDtypeStruct((1, *gwp_all.shape[1:]), F32),
            jax.ShapeDtypeStruct((1, 1), F32),
            w_like,
            w_like,
            w_like,
        ],
        in_specs=[pl.BlockSpec(memory_space=pltpu.SMEM)] + [VMEM_SPEC] * 8,
        out_specs=[VMEM_SPEC] * 13,
        scratch_shapes=[pltpu.VMEM((6 * n_chunk, N_DEV, 128), F32)],
        compiler_params=pltpu.CompilerParams(vmem_limit_bytes=VMEM_LIMIT),
    )(my_slot, acc1_all, acc2_all, gcw_all, gwp_all, cond_t, w_ada, m_w_ada, v_w_ada)


def kernel(x, c, w_ada, b_ada, w_in, conv_w, w_pool, pool_scale, w_out, ln1_g, ln1_b, w_mlp_in, w_mlp_out, ln2_g, ln2_b, loss_target, m_w_ada, m_b_ada, m_w_in, m_conv_w, m_w_pool, m_pool_scale, m_w_out, m_ln1_g, m_ln1_b, m_w_mlp_in, m_w_mlp_out, m_ln2_g, m_ln2_b, v_w_ada, v_b_ada, v_w_in, v_conv_w, v_w_pool, v_pool_scale, v_w_out, v_ln1_g, v_ln1_b, v_w_mlp_in, v_w_mlp_out, v_ln2_g, v_ln2_b):
    T, D = x.shape[1], x.shape[2]
    H = w_mlp_out.shape[1] * N_DEV
    ZW = w_in.shape[2] * N_DEV
    CC = ZW // 4
    tm = min(512, T // 2)
    bt = min(1024, T)
    ax, ay, ac = _my_place()
    me = _slot(ax, ay, ac)

    cw3, m_cw3, v_cw3 = (jnp.transpose(t, (1, 0, 2)) for t in (conv_w, m_conv_w, v_conv_w))
    w_in_f, w_out_g, cw_g, c_g, _, mod = _prologue(w_in, w_out, cw3, c, w_ada, b_ada)
    w_out_f = w_out_g.reshape(D, D)
    conv_w_f = jnp.transpose(cw_g[:, :, 0, :], (1, 0, 2)).reshape(conv_w.shape[1], CC)
    c_all = c_g.reshape(N_DEV, D)

    ln = jnp.concatenate([ln1_g, ln1_b, ln2_g, ln2_b], axis=0)
    xs, target = x[0], loss_target[0]

    z, h, xhat1, rstd1, mix, w_mi_f, w_mo_g = _f1(
        xs, mod, w_in_f, conv_w_f, w_pool[0], pool_scale, w_out_f, tm,
        [w_mlp_in, w_mlp_out], [True, False])
    dx1, h2, a, du, df, acc2 = _fb2(xhat1, target, mod, ln, w_mi_f, w_mo_g.reshape(H, D), tm)

    grad_x, dmix, ycat, dz, acc1, gcw, gwp = _b1(
        dx1, xhat1, rstd1, xs, mix, z, mod, ln, w_out_f, w_in_f, conv_w_f, w_pool[0], pool_scale, tm)

    gp_mo = _wgrad(a, df, D, 1, min(4 * bt, T), "wgrad_mlp_out").reshape(N_DEV, H // N_DEV, D)
    gp_mi, rv_mo = _wgrad(h2, du, D, 2, min(2 * bt, T), "wgrad_mlp_in", owners=N_DEV, scatter=gp_mo)
    gp_in, gp_out, rv_mi, acc1_g, acc2_g, gcw_g, gwp_g = _wgrad_pair(
        h, dz, ycat, dmix, bt, "wgrad_in_out", N_DEV, gp_mi, [acc1, acc2, gcw, gwp])
    rv_out, rv_in = _reduce_scatter([gp_out.reshape(N_DEV, D // N_DEV, D), gp_in], "scatter_w_in_out")
    g_b_ada, g_w_ada, g_ln1_g, g_ln1_b, g_ln2_g, g_ln2_b, g_cw, g_pool_scale, g_w_pool, loss, *u_w_ada = _small_grads(
        acc1_g, acc2_g, gcw_g, gwp_g, c_all.T, jnp.reshape(me, (1,)).astype(jnp.int32), w_ada, m_w_ada, v_w_ada)
    cc_mine = conv_w.shape[2]
    g_cw3 = lax.dynamic_slice(g_cw, (0, me * cc_mine), (conv_w.shape[1], cc_mine))[:, None, :]
    g_conv_w = jnp.transpose(g_cw3, (1, 0, 2))

    small = _adamw_multi(
        [
            (b_ada, g_b_ada, m_b_ada, v_b_ada),
            (cw3, g_cw3, m_cw3, v_cw3),
            (w_pool, g_w_pool, m_w_pool, v_w_pool),
            (pool_scale, g_pool_scale, m_pool_scale, v_pool_scale),
            (ln1_g, g_ln1_g, m_ln1_g, v_ln1_g),
            (ln1_b, g_ln1_b, m_ln1_b, v_ln1_b),
            (ln2_g, g_ln2_g, m_ln2_g, v_ln2_g),
            (ln2_b, g_ln2_b, m_ln2_b, v_ln2_b),
        ],
        "adamw_small")
    u_b_ada, u_cw3, u_w_pool, u_pool_scale, u_ln1_g, u_ln1_b, u_ln2_g, u_ln2_b = small
    u_conv_w = tuple(jnp.transpose(t, (1, 0, 2)) for t in u_cw3)

    g_w_mo, *u_w_mo = _sum_adamw(rv_mo, w_mlp_out, m_w_mlp_out, v_w_mlp_out, "sum_w_mlp_out")
    g_w_mi, *u_w_mi = _sum_adamw(rv_mi, w_mlp_in, m_w_mlp_in, v_w_mlp_in, "sum_w_mlp_in")
    g_w_in, *u_w_in = _sum_adamw(rv_in, w_in, m_w_in, v_w_in, "sum_w_in")
    g_w_out, *u_w_out = _sum_adamw(rv_out, w_out, m_w_out, v_w_out, "sum_w_out")

    grads = [g_w_ada, g_b_ada, g_w_in, g_conv_w, g_w_pool, g_pool_scale, g_w_out, g_ln1_g, g_ln1_b, g_w_mi, g_w_mo, g_ln2_g, g_ln2_b]
    updates = [u_w_ada, u_b_ada, u_w_in, u_conv_w, u_w_pool, u_pool_scale, u_w_out, u_ln1_g, u_ln1_b, u_w_mi, u_w_mo, u_ln2_g, u_ln2_b]
    deltas = [u[0] for u in updates]
    new_m = [u[1] for u in updates]
    new_v = [u[2] for u in updates]
    return (loss.reshape(()), grad_x[None], *grads, *deltas, *new_m, *new_v)
```

```python
import jax
import jax.numpy as jnp
from jax import lax
from jax.experimental import pallas as pl
from jax.experimental.pallas import tpu as pltpu

F32 = jnp.float32
BF16 = jnp.bfloat16
MESH = pl.DeviceIdType.MESH
N_DEV = 8

LN_EPS = 1e-5
DEPTH = 1
DEEPNORM_ALPHA = (2.0 * DEPTH) ** 0.25
POOL_WINDOWS = (2, 4, 8, 16)
HALO = 16

ADAM_LR = 0.001
ADAM_B1 = 0.9
ADAM_B2 = 0.999
ADAM_EPS = 1e-08
ADAM_WD = 0.01
ADAM_STEP = 10

VMEM_LIMIT = 60 * 1024 * 1024

VMEM_SPEC = pl.BlockSpec(memory_space=pltpu.VMEM)
ANY_SPEC = pl.BlockSpec(memory_space=pl.ANY)

NT = (((1,), (1,)), ((), ()))
TN = (((0,), (0,)), ((), ()))


def _my_place():
    return lax.axis_index("x"), lax.axis_index("y"), lax.axis_index("c")


def _slot(x, y, c):
    return 4 * x + 2 * y + c


def _gather_place(ins, outs, a, slot):
    if len(outs[a].shape) == len(ins[a].shape):
        wb = ins[a].shape[1]
        return outs[a].at[:, pl.ds(pl.multiple_of(slot * wb, wb), wb)]
    return outs[a].at[slot]


def _gather_copy(ins, outs, sems, a, k, block, to, from_shard=False):
    send_sems, recv_sems, _ = sems
    dst = _gather_place(ins, outs, a, _slot(*block))
    return pltpu.make_async_remote_copy(
        src_ref=ins[a] if from_shard else dst,
        dst_ref=dst,
        send_sem=send_sems.at[7 * a + k],
        recv_sem=recv_sems.at[7 * a + k],
        device_id=to,
        device_id_type=MESH,
    )


def _gather_peers():
    x, y, c = _my_place()
    return (x, y, c), (x, y, 1 - c), [(1 - x, y), (x, 1 - y), (1 - x, 1 - y)]


def _gather_first(ins, outs, sems):
    me, sibling, chips = _gather_peers()
    first = []
    for a in range(len(ins)):
        first.append(_gather_copy(ins, outs, sems, a, 0, me, sibling, from_shard=True))
        first += [_gather_copy(ins, outs, sems, a, 1 + j, me, (*chip, me[2]), from_shard=True) for j, chip in enumerate(chips)]
    return first


def _gather_mine(ins, outs, sems, a):
    me, _, _ = _gather_peers()
    return pltpu.make_async_copy(ins[a], _gather_place(ins, outs, a, _slot(*me)), sems[2].at[a])


def _gather_start(ins, outs, sems):
    for a in range(len(ins)):
        _gather_mine(ins, outs, sems, a).start()
    for cp in _gather_first(ins, outs, sems):
        cp.start()


def _gather_forward(ins, outs, sems, j):
    me, sibling, chips = _gather_peers()
    for a in range(len(ins)):
        _gather_copy(ins, outs, sems, a, 1 + j, (*chips[j], me[2]), me).wait_recv()
        _gather_copy(ins, outs, sems, a, 4 + j, (*chips[j], me[2]), sibling).start()


def _gather_finish(ins, outs, sems):
    me, sibling, chips = _gather_peers()
    for a in range(len(ins)):
        _gather_copy(ins, outs, sems, a, 0, sibling, me).wait_recv()
        for j, chip in enumerate(chips):
            _gather_copy(ins, outs, sems, a, 4 + j, (*chip, 1 - me[2]), me).wait_recv()
    for cp in _gather_first(ins, outs, sems):
        cp.wait_send()
    for a in range(len(ins)):
        for j, chip in enumerate(chips):
            _gather_copy(ins, outs, sems, a, 4 + j, (*chip, me[2]), sibling).wait_send()
        _gather_mine(ins, outs, sems, a).wait()


def _gather_scratch(n):
    return [pltpu.SemaphoreType.DMA((7 * n,)), pltpu.SemaphoreType.DMA((7 * n,)), pltpu.SemaphoreType.DMA((n,))]


def _gather_out_shape(shards, by_cols):
    return [
        jax.ShapeDtypeStruct((s.shape[0], N_DEV * s.shape[1]) if cols else (N_DEV, *s.shape), s.dtype)
        for s, cols in zip(shards, by_cols)
    ]


N_CHIP = 4


def _scatter_scratch(rows, cols):
    block = pltpu.VMEM((N_CHIP, rows, cols), BF16)
    dma = pltpu.SemaphoreType.DMA
    return [block, block, block, dma((N_CHIP,)), dma((N_CHIP,)), dma((N_CHIP,)), dma((N_CHIP - 1,)), dma((N_CHIP - 1,)), dma]


def _scatter_pair_copies(g_hbm, scr):
    x, y, c = _my_place()
    mine, theirs, _, a_send, a_recv, load_sem = scr[:6]
    to_sibling = [
        pltpu.make_async_remote_copy(
            src_ref=g_hbm.at[2 * q + (1 - c)], dst_ref=theirs.at[q], send_sem=a_send.at[q], recv_sem=a_recv.at[q],
            device_id=(x, y, 1 - c), device_id_type=MESH)
        for q in range(N_CHIP)
    ]
    loads = [pltpu.make_async_copy(g_hbm.at[2 * q + c], mine.at[q], load_sem.at[q]) for q in range(N_CHIP)]
    return to_sibling, loads


def _scatter_sum_copies(recv, scr):
    x, y, c = _my_place()
    sums, b_send, b_recv, own_sem = scr[2], scr[6], scr[7], scr[8]
    q_me = 2 * x + y
    to_owner = [
        pltpu.make_async_remote_copy(
            src_ref=sums.at[2 * px + py], dst_ref=recv.at[q_me], send_sem=b_send.at[j], recv_sem=b_recv.at[j],
            device_id=(px, py, c), device_id_type=MESH)
        for j, (px, py) in enumerate([(1 - x, y), (x, 1 - y), (1 - x, 1 - y)])
    ]
    return to_owner, pltpu.make_async_copy(sums.at[q_me], recv.at[q_me], own_sem)


def _scatter_start(g_hbm, scr):
    to_sibling, loads = _scatter_pair_copies(g_hbm, scr)
    for cp in to_sibling + loads:
        cp.start()


def _scatter_middle(g_hbm, recv, scr):
    to_sibling, loads = _scatter_pair_copies(g_hbm, scr)
    for cp in to_sibling:
        cp.wait_recv()
    for cp in loads:
        cp.wait()
    mine, theirs, sums = scr[:3]

    def step(r, carry):
        rs = pl.ds(pl.multiple_of(r * ROW_CHUNK, ROW_CHUNK), ROW_CHUNK)
        for q in range(N_CHIP):
            sums[q, rs, :] = (mine[q, rs, :].astype(F32) + theirs[q, rs, :].astype(F32)).astype(BF16)
        return carry

    lax.fori_loop(0, mine.shape[1] // ROW_CHUNK, step, 0)
    to_owner, own = _scatter_sum_copies(recv, scr)
    for cp in to_owner + [own]:
        cp.start()


def _scatter_finish(g_hbm, recv, scr):
    to_sibling, _ = _scatter_pair_copies(g_hbm, scr)
    to_owner, own = _scatter_sum_copies(recv, scr)
    for cp in to_owner:
        cp.wait_recv()
    for cp in to_sibling + to_owner:
        cp.wait_send()
    own.wait()


def _scatter_out_shape(gparts):
    return jax.ShapeDtypeStruct((N_CHIP, *gparts.shape[1:]), gparts.dtype)


def _adamw_math(w, g, m, v):
    m = ADAM_B1 * m + (1.0 - ADAM_B1) * g
    v = ADAM_B2 * v + (1.0 - ADAM_B2) * (g * g)
    m_hat = m / (1.0 - ADAM_B1**ADAM_STEP)
    v_hat = v / (1.0 - ADAM_B2**ADAM_STEP)
    delta = -ADAM_LR * (m_hat / (jnp.sqrt(v_hat) + ADAM_EPS) + ADAM_WD * w)
    return delta, m, v


ROW_CHUNK = 64


ELEMS_PER_STEP = 128 * 1024


N_SCATTER_SCRATCH = 9


def _reduce_scatter(gparts, name):
    n = len(gparts)

    def body(*refs):
        g_hbm, recv = refs[:n], refs[n : 2 * n]
        scr = [refs[2 * n + k * N_SCATTER_SCRATCH : 2 * n + (k + 1) * N_SCATTER_SCRATCH] for k in range(n)]
        for k in range(n):
            _scatter_start(g_hbm[k], scr[k])
        for k in range(n):
            _scatter_middle(g_hbm[k], recv[k], scr[k])
        for k in range(n):
            _scatter_finish(g_hbm[k], recv[k], scr[k])

    return pl.pallas_call(
        body,
        name=name,
        out_shape=[_scatter_out_shape(g) for g in gparts],
        in_specs=[ANY_SPEC] * n,
        out_specs=[ANY_SPEC] * n,
        scratch_shapes=[s for g in gparts for s in _scatter_scratch(*g.shape[1:])],
        compiler_params=pltpu.CompilerParams(vmem_limit_bytes=VMEM_LIMIT),
    )(*gparts)


def _sum_adamw(parts, w, m, v, name):
    _, rows, cols = w.shape
    rb = rows
    while rb * cols > ELEMS_PER_STEP and rb % 16 == 0:
        rb //= 2

    def body(p_ref, w_ref, m_ref, v_ref, grad_ref, delta_ref, nm_ref, nv_ref):
        g = p_ref[0].astype(F32)
        for k in range(1, p_ref.shape[0]):
            g = g + p_ref[k].astype(F32)
        delta, nm, nv = _adamw_math(w_ref[0], g, m_ref[0], v_ref[0])
        grad_ref[0] = g
        delta_ref[0] = delta
        nm_ref[0] = nm
        nv_ref[0] = nv

    block = lambda lead: pl.BlockSpec((lead, rb, cols), lambda i: (0, i, 0))
    out = jax.ShapeDtypeStruct(w.shape, F32)
    return pl.pallas_call(
        body,
        name=name,
        grid=(rows // rb,),
        out_shape=[out] * 4,
        in_specs=[block(parts.shape[0])] + [block(1)] * 3,
        out_specs=[block(1)] * 4,
        compiler_params=pltpu.CompilerParams(dimension_semantics=("arbitrary",), vmem_limit_bytes=VMEM_LIMIT),
    )(parts, w, m, v)


def _adamw_multi(items, name):
    n = len(items)

    def body(*refs):
        ins, outs = refs[: 4 * n], refs[4 * n :]
        for a in range(n):
            w_ref, g_ref, m_ref, v_ref = ins[4 * a : 4 * a + 4]
            d_ref, nm_ref, nv_ref = outs[3 * a : 3 * a + 3]
            delta, nm, nv = _adamw_math(w_ref[...], g_ref[...], m_ref[...], v_ref[...])
            d_ref[...] = delta
            nm_ref[...] = nm
            nv_ref[...] = nv

    flat = [a for it in items for a in it]
    out_shape = [jax.ShapeDtypeStruct(it[0].shape, F32) for it in items for _ in range(3)]
    outs = pl.pallas_call(
        body,
        name=name,
        out_shape=out_shape,
        in_specs=[VMEM_SPEC] * (4 * n),
        out_specs=[VMEM_SPEC] * (3 * n),
        compiler_params=pltpu.CompilerParams(vmem_limit_bytes=VMEM_LIMIT),
    )(*flat)
    return [tuple(outs[3 * a : 3 * a + 3]) for a in range(n)]


def _prologue(w_in, w_out, conv_w, c, w_ada, b_ada):
    D = c.shape[1]
    wc = w_ada.shape[2]
    shards16 = [jax.ShapeDtypeStruct(w_in.shape[1:], BF16), jax.ShapeDtypeStruct(w_out.shape[1:], BF16)]

    def to_all(src, out, sems):
        x, y, c = _my_place()
        me = _slot(x, y, c)
        copies = [
            pltpu.make_async_remote_copy(
                src_ref=src, dst_ref=out.at[me], send_sem=sems[0].at[k - 1], recv_sem=sems[1].at[k - 1],
                device_id=(x ^ (k >> 2), y ^ ((k >> 1) & 1), c ^ (k & 1)), device_id_type=MESH)
            for k in range(1, N_DEV)
        ]
        return copies, pltpu.make_async_copy(src, out.at[me], sems[2].at[0])

    def start(copies, own):
        for cp in copies + [own]:
            cp.start()

    def finish(copies, own):
        for cp in copies:
            cp.wait()
        own.wait()

    def body(win_ref, wout_ref, cw_ref, c_ref, wada_ref, b_ref, win_g, wout_g, cw_g, c_g, mod_g, mod_ref, win16, wout16, c_s, mp_s, *sems):
        c_copies = to_all(c_ref, c_g, sems[3:6])
        start(*c_copies)
        win16[...] = win_ref[0].astype(BF16)
        wout16[...] = wout_ref[0].astype(BF16)
        w_ins, w_outs, w_sems = (win16, wout16, cw_ref), (win_g, wout_g, cw_g), sems[0:3]
        _gather_start(w_ins, w_outs, w_sems)
        finish(*c_copies)
        for k in range(N_DEV):
            c_s[k : k + 1, :] = c_g[k]
        cv = c_s[...]
        cond = cv * jax.nn.sigmoid(cv)
        b_mine = b_ref[:, pl.ds(pl.multiple_of(_slot(*_my_place()) * wc, 128), wc)]
        mp_s[...] = jnp.dot(cond, wada_ref[0], precision=lax.Precision.HIGHEST, preferred_element_type=F32) + b_mine
        m_copies = to_all(mp_s, mod_g, sems[6:9])
        start(*m_copies)
        for j in range(3):
            _gather_forward(w_ins, w_outs, w_sems, j)
        _gather_finish(w_ins, w_outs, w_sems)
        finish(*m_copies)
        me = _slot(*_my_place())
        for p in range(N_DEV):
            part = mod_g[p, pl.ds(me, 1), :]
            lo = 0
            while lo < wc:
                r, col = divmod(p * wc + lo, D)
                n = min(wc - lo, D - col)
                mod_ref[r : r + 1, col : col + n] = part[:, lo : lo + n]
                lo += n

    return pl.pallas_call(
        body,
        name="prologue",
        out_shape=_gather_out_shape(shards16 + [conv_w], [True, False, False])
        + [jax.ShapeDtypeStruct((N_DEV, 1, D), F32), jax.ShapeDtypeStruct((N_DEV, N_DEV, wc), F32),
           jax.ShapeDtypeStruct((N_DEV * wc // D, D), F32)],
        in_specs=[VMEM_SPEC, VMEM_SPEC, ANY_SPEC] + [VMEM_SPEC] * 3,
        out_specs=[ANY_SPEC] * 3 + [VMEM_SPEC] * 3,
        scratch_shapes=[pltpu.VMEM(s.shape, BF16) for s in shards16]
        + [pltpu.VMEM((N_DEV, D), F32), pltpu.VMEM((N_DEV, wc), F32)]
        + _gather_scratch(3) + _gather_scratch(1) + _gather_scratch(1),
        compiler_params=pltpu.CompilerParams(vmem_limit_bytes=VMEM_LIMIT),
    )(w_in, w_out, conv_w, c, w_ada, b_ada)


def _ln_fwd(r):
    mu = jnp.mean(r, axis=-1, keepdims=True)
    d = r - mu
    var = jnp.mean(d * d, axis=-1, keepdims=True)
    rstd = lax.rsqrt(var + LN_EPS)
    return d * rstd, rstd


def _ln_bwd(dxh, xhat, rstd):
    m1 = jnp.mean(dxh, axis=-1, keepdims=True)
    m2 = jnp.mean(dxh * xhat, axis=-1, keepdims=True)
    return rstd * (dxh - m1 - xhat * m2)


def _colsum(a):
    return jnp.sum(a, axis=0, keepdims=True)


def _window_sums(ext, tm, causal):
    n = ext.shape[0]
    lo = HALO if causal else 0
    s, out = ext, []
    for p in range(len(POOL_WINDOWS)):
        assert POOL_WINDOWS[p] == 2 ** (p + 1)
        k = 2**p
        s = s + pltpu.roll(s, k if causal else n - k, 0)
        out.append(s[lo : lo + tm, 0:128])
        if p + 1 < len(POOL_WINDOWS):
            s = s[:, 128:]
    return out


def _pool_features(vp, vp_s, row, tm):
    sums = _window_sums(vp_s[...], tm, causal=True)
    feats, inv_cnts = [], []
    for g, win in enumerate(POOL_WINDOWS):
        inv_cnt = 1.0 / jnp.minimum(row + 1, win).astype(F32)
        feats.append(sums[g] * inv_cnt - vp[:, 128 * g : 128 * g + 128])
        inv_cnts.append(inv_cnt)
    return feats, inv_cnts


def _f1(x, mod, w_in, conv_w, w_pool, pool_scale, w_out, tm, gather, by_cols):
    T, D = x.shape
    ZW = w_in.shape[1]
    CC = ZW // 4
    nt = T // tm
    ng = len(gather)
    fwd_steps = [max(nt - 3 + j, 0) for j in range(3)]

    shards16 = [jax.ShapeDtypeStruct(s.shape[1:], BF16) for s in gather]

    def body(*refs):
        x_ref, mod_ref, win_ref, cw_ref, wp_ref, ps_ref, wout_ref = refs[:7]
        g_f32 = refs[7 : 7 + ng]
        z_ref, h_ref, xhat_ref, rstd_ref, mix_ref = refs[7 + ng : 12 + ng]
        g_outs = refs[12 + ng : 12 + 2 * ng]
        cv_s, vp_s = refs[12 + 2 * ng : 14 + 2 * ng]
        g_ins = refs[14 + 2 * ng : 14 + 3 * ng]
        g_sems = refs[14 + 3 * ng :]
        i = pl.program_id(0)

        @pl.when(i == 0)
        def _():
            for src, dst in zip(g_f32, g_ins):
                dst[...] = src[0].astype(BF16)
            _gather_start(g_ins, g_outs, g_sems)
            cv_s[0:HALO, :] = jnp.zeros((HALO, CC), F32)
            vp_s[0:HALO, :] = jnp.zeros((HALO, CC), F32)

        xv = x_ref[...]
        sh1, sc1, g1 = mod_ref[0:1, :], mod_ref[1:2, :], mod_ref[2:3, :]
        h = (xv * (1.0 + sc1) + sh1).astype(BF16)
        h_ref[...] = h
        z = jnp.dot(h, win_ref[...], preferred_element_type=F32)
        z_ref[...] = z.astype(BF16)
        gb, gc, vc, vp = z[:, 0:CC], z[:, CC : 2 * CC], z[:, 2 * CC : 3 * CC], z[:, 3 * CC : 4 * CC]
        cv = gc * vc
        cv_s[HALO : HALO + tm, :] = cv
        vp_s[HALO : HALO + tm, :] = vp
        cv_ext = cv_s[...]
        cv_m2 = pltpu.roll(cv_ext, 2, 0)[HALO : HALO + tm, :]
        cv_m1 = pltpu.roll(cv_ext, 1, 0)[HALO : HALO + tm, :]
        conv = cw_ref[0:1, :] * cv_m2 + cw_ref[1:2, :] * cv_m1 + cw_ref[2:3, :] * cv
        parts = [gb * conv]
        row = i * tm + lax.broadcasted_iota(jnp.int32, (tm, 1), 0)
        feats, _ = _pool_features(vp, vp_s, row, tm)
        for g in range(len(POOL_WINDOWS)):
            pw = jnp.dot(feats[g].astype(BF16), wp_ref[g].astype(BF16), preferred_element_type=F32)
            parts.append(pw * ps_ref[0:1, 128 * g : 128 * g + 128])
        cv_s[0:HALO, :] = cv_s[tm : tm + HALO, :]
        vp_s[0:HALO, :] = vp_s[tm : tm + HALO, :]
        ycat = jnp.concatenate(parts, axis=1).astype(BF16)
        mix = jnp.dot(ycat, wout_ref[...], preferred_element_type=F32)
        mix_ref[...] = mix
        xhat, rstd = _ln_fwd(DEEPNORM_ALPHA * xv + (1.0 + g1) * mix)
        xhat_ref[...] = xhat
        rstd_ref[...] = rstd

        for j in range(3):

            @pl.when(i == fwd_steps[j])
            def _(j=j):
                _gather_forward(g_ins, g_outs, g_sems, j)

        @pl.when(i == nt - 1)
        def _():
            _gather_finish(g_ins, g_outs, g_sems)

    tile = lambda w: pl.BlockSpec((tm, w), lambda i: (i, 0))
    return pl.pallas_call(
        body,
        name="f1",
        grid=(nt,),
        out_shape=[
            jax.ShapeDtypeStruct((T, ZW), BF16),
            jax.ShapeDtypeStruct((T, D), BF16),
            jax.ShapeDtypeStruct((T, D), F32),
            jax.ShapeDtypeStruct((T, 1), F32),
            jax.ShapeDtypeStruct((T, D), F32),
        ]
        + _gather_out_shape(shards16, by_cols),
        in_specs=[tile(D)] + [VMEM_SPEC] * (6 + ng),
        out_specs=[tile(ZW), tile(D), tile(D), tile(1), tile(D)] + [ANY_SPEC] * ng,
        scratch_shapes=[pltpu.VMEM((HALO + tm, CC), F32), pltpu.VMEM((HALO + tm, CC), F32)]
        + [pltpu.VMEM(s.shape, BF16) for s in shards16]
        + _gather_scratch(ng),
        compiler_params=pltpu.CompilerParams(dimension_semantics=("arbitrary",), vmem_limit_bytes=VMEM_LIMIT),
    )(x, mod, w_in, conv_w, w_pool, pool_scale, w_out, *gather)


def _fb2(xhat1, target, mod, ln, w_mi, w_mo, tm):
    T, D = xhat1.shape
    H = w_mi.shape[1]
    hc = min(1024, H)
    nb = H // hc
    nt = T // tm

    def body(xh1_ref, t_ref, mod_ref, ln_ref, wmi_ref, wmo_ref, dx1_ref, h2_ref, a_ref, du_ref, df_ref, acc_ref):
        i = pl.program_id(0)

        @pl.when(i == 0)
        def _():
            acc_ref[...] = jnp.zeros((8, D), F32)

        sh2, sc2, g2 = mod_ref[3:4, :], mod_ref[4:5, :], mod_ref[5:6, :]
        x1 = xh1_ref[...] * ln_ref[0:1, :] + ln_ref[1:2, :]
        h2 = (x1 * (1.0 + sc2) + sh2).astype(BF16)
        h2_ref[...] = h2
        f = jnp.zeros((tm, D), F32)
        for k in range(nb):
            ks = slice(k * hc, (k + 1) * hc)
            r = jnp.maximum(jnp.dot(h2, wmi_ref[:, ks], preferred_element_type=F32), 0.0)
            du_ref[:, ks] = r.astype(BF16)
            a = (r * r).astype(BF16)
            a_ref[:, ks] = a
            f = f + jnp.dot(a, wmo_ref[ks, :], preferred_element_type=F32)
        xhat2, rstd2 = _ln_fwd(DEEPNORM_ALPHA * x1 + (1.0 + g2) * f)
        ln2_g = ln_ref[2:3, :]
        d = xhat2 * ln2_g + ln_ref[3:4, :] - t_ref[...]
        dr2 = _ln_bwd(d * (ln2_g * (1.0 / D)), xhat2, rstd2)
        df = ((1.0 + g2) * dr2).astype(BF16)
        df_ref[...] = df
        dh2 = jnp.zeros((tm, D), F32)
        for k in range(nb):
            ks = slice(k * hc, (k + 1) * hc)
            da = lax.dot_general(df, wmo_ref[ks, :], NT, preferred_element_type=F32)
            du = (da * (2.0 * du_ref[:, ks].astype(F32))).astype(BF16)
            du_ref[:, ks] = du
            dh2 = dh2 + lax.dot_general(du, wmi_ref[:, ks], NT, preferred_element_type=F32)
        dx1_ref[...] = DEEPNORM_ALPHA * dr2 + dh2 * (1.0 + sc2)
        acc_ref[0:1, :] += _colsum(d * xhat2) * (1.0 / D)
        acc_ref[1:2, :] += _colsum(d) * (1.0 / D)
        acc_ref[2:3, :] += _colsum(dh2)
        acc_ref[3:4, :] += _colsum(dh2 * x1)
        acc_ref[4:5, :] += _colsum(dr2 * f)
        acc_ref[5:6, :] += jnp.zeros((1, D), F32) + (0.5 / D) * jnp.sum(d * d)

    tile = lambda w: pl.BlockSpec((tm, w), lambda i: (i, 0))
    return pl.pallas_call(
        body,
        name="fb2",
        grid=(nt,),
        out_shape=[
            jax.ShapeDtypeStruct((T, D), F32),
            jax.ShapeDtypeStruct((T, D), BF16),
            jax.ShapeDtypeStruct((T, H), BF16),
            jax.ShapeDtypeStruct((T, H), BF16),
            jax.ShapeDtypeStruct((T, D), BF16),
            jax.ShapeDtypeStruct((8, D), F32),
        ],
        in_specs=[tile(D), tile(D)] + [VMEM_SPEC] * 4,
        out_specs=[tile(D), tile(D), tile(H), tile(H), tile(D), pl.BlockSpec((8, D), lambda i: (0, 0))],
        compiler_params=pltpu.CompilerParams(dimension_semantics=("arbitrary",), vmem_limit_bytes=VMEM_LIMIT),
    )(xhat1, target, mod, ln, w_mi, w_mo)


def _b1(dx1, xhat1, rstd1, x, mix, z, mod, ln, w_out, w_in, conv_w, w_pool, pool_scale, tm):
    T, D = x.shape
    ZW = w_in.shape[1]
    CC = ZW // 4
    nt = T // tm
    hb = tm // HALO

    def body(dx1_ref, xh1_ref, rstd_ref, x_ref, mix_ref, z_ref, zh_ref, mod_ref, ln_ref, wout_ref, win_ref, cw_ref, wp_ref, ps_ref,
             dx_ref, dmix_ref, ycat_ref, dz_ref, acc_ref, gcw_ref, gwp_ref, cv_s, vp_s, e_s, q_s):
        i = pl.program_id(0)
        j = nt - 1 - i

        @pl.when(i == 0)
        def _():
            acc_ref[...] = jnp.zeros((8, D), F32)
            gcw_ref[...] = jnp.zeros((8, CC), F32)
            gwp_ref[...] = jnp.zeros(gwp_ref.shape, F32)
            e_s[tm : tm + HALO, :] = jnp.zeros((HALO, CC), F32)
            q_s[tm : tm + HALO, :] = jnp.zeros((HALO, CC), F32)

        sh1, sc1, g1 = mod_ref[0:1, :], mod_ref[1:2, :], mod_ref[2:3, :]
        dx1 = dx1_ref[...]
        xhat1 = xh1_ref[...]
        acc_ref[0:1, :] += _colsum(dx1 * xhat1)
        acc_ref[1:2, :] += _colsum(dx1)
        dr1 = _ln_bwd(dx1 * ln_ref[0:1, :], xhat1, rstd_ref[...])
        acc_ref[4:5, :] += _colsum(dr1 * mix_ref[...])
        dmix = ((1.0 + g1) * dr1).astype(BF16)
        dmix_ref[...] = dmix
        dycat = lax.dot_general(dmix, wout_ref[...], NT, preferred_element_type=F32)

        z = z_ref[...].astype(F32)
        zh = zh_ref[...].astype(F32) * jnp.where(j > 0, 1.0, 0.0)
        gb, gc, vc, vp = z[:, 0:CC], z[:, CC : 2 * CC], z[:, 2 * CC : 3 * CC], z[:, 3 * CC : 4 * CC]
        cv = gc * vc
        cv_s[0:HALO, :] = zh[:, CC : 2 * CC] * zh[:, 2 * CC : 3 * CC]
        cv_s[HALO : HALO + tm, :] = cv
        vp_s[0:HALO, :] = zh[:, 3 * CC : 4 * CC]
        vp_s[HALO : HALO + tm, :] = vp
        cv_ext = cv_s[...]
        cv_m2 = pltpu.roll(cv_ext, 2, 0)[HALO : HALO + tm, :]
        cv_m1 = pltpu.roll(cv_ext, 1, 0)[HALO : HALO + tm, :]
        w0, w1, w2 = cw_ref[0:1, :], cw_ref[1:2, :], cw_ref[2:3, :]
        conv = w0 * cv_m2 + w1 * cv_m1 + w2 * cv
        dyc = dycat[:, 0:CC]
        e = dyc * gb
        e_s[0:tm, :] = e
        e_ext = e_s[...]
        dcv = w2 * e + w1 * pltpu.roll(e_ext, tm + HALO - 1, 0)[0:tm, :] + w0 * pltpu.roll(e_ext, tm + HALO - 2, 0)[0:tm, :]
        gcw_ref[0:1, :] += _colsum(e * cv_m2)
        gcw_ref[1:2, :] += _colsum(e * cv_m1)
        gcw_ref[2:3, :] += _colsum(e * cv)
        y_parts = [gb * conv]
        dz_parts = [dyc * conv, dcv * vc, dcv * gc]

        row = j * tm + lax.broadcasted_iota(jnp.int32, (tm, 1), 0)
        feats, inv_cnts = _pool_features(vp, vp_s, row, tm)
        gps_parts, dps = [], []
        for g in range(len(POOL_WINDOWS)):
            cols = slice(128 * g, 128 * g + 128)
            p = feats[g].astype(BF16)
            scale = ps_ref[0:1, cols]
            wp = wp_ref[g].astype(BF16)
            pw = jnp.dot(p, wp, preferred_element_type=F32)
            y_parts.append(pw * scale)
            dyp = dycat[:, CC + 128 * g : CC + 128 * g + 128]
            gps_parts.append(_colsum(dyp * pw))
            dpw = (dyp * scale).astype(BF16)
            gwp_ref[g] += lax.dot_general(p, dpw, TN, preferred_element_type=F32)
            dp = lax.dot_general(dpw, wp, NT, preferred_element_type=F32)
            q_s[0:tm, cols] = dp * inv_cnts[g]
            dps.append(dp)
        sq = _window_sums(q_s[...], tm, causal=False)
        dz_parts += [sq[g] - dps[g] for g in range(len(POOL_WINDOWS))]
        gcw_ref[3:4, :] += jnp.concatenate(gps_parts, axis=1)
        ycat_ref[...] = jnp.concatenate(y_parts, axis=1).astype(BF16)
        dz = jnp.concatenate(dz_parts, axis=1).astype(BF16)
        dz_ref[...] = dz
        dh = lax.dot_general(dz, win_ref[...], NT, preferred_element_type=F32)
        acc_ref[2:3, :] += _colsum(dh)
        acc_ref[3:4, :] += _colsum(dh * x_ref[...])
        dx_ref[...] = DEEPNORM_ALPHA * dr1 + dh * (1.0 + sc1)
        e_s[tm : tm + HALO, :] = e_s[0:HALO, :]
        q_s[tm : tm + HALO, :] = q_s[0:HALO, :]

    tile = lambda w: pl.BlockSpec((tm, w), lambda i: (nt - 1 - i, 0))
    halo = pl.BlockSpec((HALO, ZW), lambda i: (jnp.maximum((nt - 1 - i) * hb - 1, 0), 0))
    fixed = lambda shape: pl.BlockSpec(shape, lambda i: (0,) * len(shape))
    return pl.pallas_call(
        body,
        name="b1",
        grid=(nt,),
        out_shape=[
            jax.ShapeDtypeStruct((T, D), F32),
            jax.ShapeDtypeStruct((T, D), BF16),
            jax.ShapeDtypeStruct((T, D), BF16),
            jax.ShapeDtypeStruct((T, ZW), BF16),
            jax.ShapeDtypeStruct((8, D), F32),
            jax.ShapeDtypeStruct((8, CC), F32),
            jax.ShapeDtypeStruct(w_pool.shape, F32),
        ],
        in_specs=[tile(D), tile(D), tile(1), tile(D), tile(D), tile(ZW), halo] + [VMEM_SPEC] * 7,
        out_specs=[tile(D), tile(D), tile(D), tile(ZW), fixed((8, D)), fixed((8, CC)), fixed(w_pool.shape)],
        scratch_shapes=[
            pltpu.VMEM((HALO + tm, CC), F32),
            pltpu.VMEM((HALO + tm, CC), F32),
            pltpu.VMEM((tm + HALO, CC), F32),
            pltpu.VMEM((tm + HALO, CC), F32),
        ],
        compiler_params=pltpu.CompilerParams(dimension_semantics=("arbitrary",), vmem_limit_bytes=VMEM_LIMIT),
    )(dx1, xhat1, rstd1, x, mix, z, z, mod, ln, w_out, w_in, conv_w, w_pool, pool_scale)


def _wgrad(a, b, bk, n_groups, bt, name, owners=None, scatter=None, gather=()):
    T, K = a.shape
    N = b.shape[1]
    nk, nt, ng = K // bk, T // bt, N // n_groups
    nc = min(512, ng)
    ns, ngat = (0 if scatter is None else 1), len(gather)
    n_steps = nk * n_groups * nt
    mid_step = min(1, n_steps - 1)
    fwd_steps = [min(2 * (j + 1), n_steps - 1) for j in range(3)]

    def body(*refs):
        a_ref, b_ref = refs[0], refs[1]
        g_ins = refs[2 + ns : 2 + ns + ngat]
        outs = refs[2 + ns + ngat :]
        o_ref, g_outs = outs[0], outs[1 + ns : 1 + ns + ngat]
        scr = outs[1 + ns + ngat :]
        acc = scr[0]
        if ns:
            s_hbm, s_recv, s_scr = refs[2], outs[1], scr[1 : 1 + N_SCATTER_SCRATCH]
        g_sems = scr[1 + N_SCATTER_SCRATCH * ns :]
        kk, gg, t = pl.program_id(0), pl.program_id(1), pl.program_id(2)
        step = (kk * n_groups + gg) * nt + t

        if ngat:

            @pl.when(step == 0)
            def _():
                _gather_start(g_ins, g_outs, g_sems)

            for j in range(3):

                @pl.when(step == fwd_steps[j])
                def _(j=j):
                    _gather_forward(g_ins, g_outs, g_sems, j)

        if ns:

            @pl.when(step == 0)
            def _():
                _scatter_start(s_hbm, s_scr)

            @pl.when(step == mid_step)
            def _():
                _scatter_middle(s_hbm, s_recv, s_scr)

        @pl.when(t == 0)
        def _():
            acc[...] = jnp.zeros(acc.shape, F32)

        at = a_ref[...].T
        for c in range(ng // nc):
            cs = slice(c * nc, (c + 1) * nc)
            acc[:, cs] += jnp.dot(at, b_ref[:, cs], preferred_element_type=F32)

        @pl.when(t == nt - 1)
        def _():
            if owners is None:
                o_ref[...] = acc[...].astype(BF16)
            else:
                per = N // owners
                for o in range(ng // per):
                    o_ref[o] = acc[:, o * per : (o + 1) * per].astype(BF16)

        if ns:

            @pl.when(step == n_steps - 1)
            def _():
                _scatter_finish(s_hbm, s_recv, s_scr)

        if ngat:

            @pl.when(step == n_steps - 1)
            def _():
                _gather_finish(g_ins, g_outs, g_sems)

    if owners is None:
        out_shape = [jax.ShapeDtypeStruct((K, N), BF16)]
        out_specs = [pl.BlockSpec((bk, ng), lambda k, g, t: (k, g))]
    else:
        assert bk == K
        per = N // owners
        out_shape = [jax.ShapeDtypeStruct((owners, K, per), BF16)]
        out_specs = [pl.BlockSpec((ng // per, K, per), lambda k, g, t: (g, 0, 0))]
    ins, in_specs = [a, b], [pl.BlockSpec((bt, bk), lambda k, g, t: (t, k)), pl.BlockSpec((bt, ng), lambda k, g, t: (t, g))]
    scratch = [pltpu.VMEM((bk, ng), F32)]
    if ns:
        ins.append(scatter)
        in_specs.append(ANY_SPEC)
        out_shape.append(_scatter_out_shape(scatter))
        out_specs.append(ANY_SPEC)
        scratch += _scatter_scratch(*scatter.shape[1:])
    if ngat:
        ins += list(gather)
        in_specs += [ANY_SPEC] * ngat
        out_shape += _gather_out_shape(gather, [False] * ngat)
        out_specs += [ANY_SPEC] * ngat
        scratch += _gather_scratch(ngat)
    outs = pl.pallas_call(
        body,
        name=name,
        grid=(nk, n_groups, nt),
        out_shape=out_shape,
        in_specs=in_specs,
        out_specs=out_specs,
        scratch_shapes=scratch,
        compiler_params=pltpu.CompilerParams(dimension_semantics=("arbitrary", "arbitrary", "arbitrary"), vmem_limit_bytes=VMEM_LIMIT),
    )(*ins)
    return outs if ns + ngat else outs[0]


def _wgrad_pair(a0, b0, a1, b1, bt, name, owners, scatter, gather):
    T, K0 = a0.shape
    N0 = b0.shape[1]
    K1, N1 = a1.shape[1], b1.shape[1]
    nt = T // bt
    nc = 512
    per = N0 // owners
    ngat = len(gather)
    n_steps = 2 * nt
    fwd_steps = [min(2 * (j + 1), n_steps - 1) for j in range(3)]

    def body(*refs):
        a0_ref, b0_ref, a1_ref, b1_ref, s_hbm = refs[:5]
        g_ins = refs[5 : 5 + ngat]
        o0_ref, o1_ref, s_recv = refs[5 + ngat : 8 + ngat]
        g_outs = refs[8 + ngat : 8 + 2 * ngat]
        scr = refs[8 + 2 * ngat :]
        acc0, acc1 = scr[0], scr[1]
        s_scr, g_sems = scr[2 : 2 + N_SCATTER_SCRATCH], scr[2 + N_SCATTER_SCRATCH :]
        p, t = pl.program_id(0), pl.program_id(1)
        step = p * nt + t

        @pl.when(step == 0)
        def _():
            _scatter_start(s_hbm, s_scr)
            _gather_start(g_ins, g_outs, g_sems)

        @pl.when(step == 1)
        def _():
            _scatter_middle(s_hbm, s_recv, s_scr)

        for j in range(3):

            @pl.when(step == fwd_steps[j])
            def _(j=j):
                _gather_forward(g_ins, g_outs, g_sems, j)

        def accumulate(a_ref, b_ref, acc, n_cols):
            @pl.when(t == 0)
            def _():
                acc[...] = jnp.zeros(acc.shape, F32)

            at = a_ref[...].T
            for c in range(n_cols // nc):
                cs = slice(c * nc, (c + 1) * nc)
                acc[:, cs] += jnp.dot(at, b_ref[:, cs], preferred_element_type=F32)

        @pl.when(p == 0)
        def _():
            accumulate(a0_ref, b0_ref, acc0, N0)

            @pl.when(t == nt - 1)
            def _():
                for o in range(owners):
                    o0_ref[o] = acc0[:, o * per : (o + 1) * per].astype(BF16)

        @pl.when(p == 1)
        def _():
            accumulate(a1_ref, b1_ref, acc1, N1)

            @pl.when(t == nt - 1)
            def _():
                o1_ref[...] = acc1[...].astype(BF16)

        @pl.when(step == n_steps - 1)
        def _():
            _scatter_finish(s_hbm, s_recv, s_scr)
            _gather_finish(g_ins, g_outs, g_sems)

    first = lambda w: pl.BlockSpec((bt, w), lambda p, t: (jnp.where(p == 0, t, nt - 1), 0))
    second = lambda w: pl.BlockSpec((bt, w), lambda p, t: (jnp.where(p == 1, t, 0), 0))
    return pl.pallas_call(
        body,
        name=name,
        grid=(2, nt),
        out_shape=[jax.ShapeDtypeStruct((owners, K0, per), BF16), jax.ShapeDtypeStruct((K1, N1), BF16), _scatter_out_shape(scatter)]
        + _gather_out_shape(gather, [False] * ngat),
        in_specs=[first(K0), first(N0), second(K1), second(N1), ANY_SPEC] + [ANY_SPEC] * ngat,
        out_specs=[pl.BlockSpec((owners, K0, per), lambda p, t: (0, 0, 0)), pl.BlockSpec((K1, N1), lambda p, t: (0, 0)), ANY_SPEC]
        + [ANY_SPEC] * ngat,
        scratch_shapes=[pltpu.VMEM((K0, N0), F32), pltpu.VMEM((K1, N1), F32)]
        + _scatter_scratch(*scatter.shape[1:]) + _gather_scratch(ngat),
        compiler_params=pltpu.CompilerParams(dimension_semantics=("arbitrary", "arbitrary"), vmem_limit_bytes=VMEM_LIMIT),
    )(a0, b0, a1, b1, scatter, *gather)


def _small_grads(acc1_all, acc2_all, gcw_all, gwp_all, cond_t, my_slot, w_ada, m_w_ada, v_w_ada):
    D = acc1_all.shape[2]
    w_cols = w_ada.shape[2]
    n_chunk = D // 128
    q_mine = w_cols // 128

    def total(ref, r):
        s = ref[0, r : r + 1, :]
        for k in range(1, N_DEV):
            s = s + ref[k, r : r + 1, :]
        return s

    def body(slot_ref, a1_ref, a2_ref, gcw_ref, gwp_ref, ct_ref, w_ref, m_ref, v_ref,
             gb_ref, gw_ref, gl1g_ref, gl1b_ref, gl2g_ref, gl2b_ref, gcwo_ref, gps_ref, gwpo_ref, loss_ref,
             dw_ref, nm_ref, nv_ref, dm_s):
        q = pl.program_id(0)

        @pl.when(q == 0)
        def _():
            loss_ref[...] = total(a2_ref, 5)[:, 0:1]
            for s, (ref, r) in enumerate([(a1_ref, 2), (a1_ref, 3), (a1_ref, 4), (a2_ref, 2), (a2_ref, 3), (a2_ref, 4)]):
                gb_ref[0:1, s * D : (s + 1) * D] = total(ref, r)
                for k in range(N_DEV):
                    row = ref[k, r : r + 1, :]
                    for qq in range(n_chunk):
                        dm_s[s * n_chunk + qq, k : k + 1, :] = row[:, 128 * qq : 128 * qq + 128]
            gl1g_ref[...] = total(a1_ref, 0)
            gl1b_ref[...] = total(a1_ref, 1)
            gl2g_ref[...] = total(a2_ref, 0)
            gl2b_ref[...] = total(a2_ref, 1)
            gcwo_ref[...] = jnp.zeros(gcwo_ref.shape, F32)
            for r in range(3):
                gcwo_ref[r : r + 1, :] = total(gcw_ref, r)
            gps_ref[...] = total(gcw_ref, 3)
            wp = gwp_ref[0]
            for k in range(1, N_DEV):
                wp = wp + gwp_ref[k]
            gwpo_ref[0] = wp

        ct = ct_ref[...]
        cond_t = ct * jax.nn.sigmoid(ct)
        dm = dm_s[slot_ref[0] * q_mine + q]
        out = cond_t[:, 0:1] * dm[0:1, :]
        for k in range(1, N_DEV):
            out = out + cond_t[:, k : k + 1] * dm[k : k + 1, :]
        gw_ref[0] = out
        delta, nm, nv = _adamw_math(w_ref[0], out, m_ref[0], v_ref[0])
        dw_ref[0] = delta
        nm_ref[0] = nm
        nv_ref[0] = nv

    CC = gcw_all.shape[2]
    w_like = jax.ShapeDtypeStruct(w_ada.shape, F32)
    w_spec = pl.BlockSpec((1, D, 128), lambda q: (0, 0, q))
    whole = lambda shape: pl.BlockSpec(shape, lambda q: (0,) * len(shape))
    out_shape = [
        jax.ShapeDtypeStruct((1, 6 * D), F32),
        w_like,
        *[jax.ShapeDtypeStruct((1, D), F32)] * 4,
        jax.ShapeDtypeStruct((8, CC), F32),
        jax.ShapeDtypeStruct((1, CC), F32),
        jax.ShapeDtypeStruct((1, *gwp_all.shape[1:]), F32),
        jax.ShapeDtypeStruct((1, 1), F32),
        w_like,
        w_like,
        w_like,
    ]
    return pl.pallas_call(
        body,
        name="small_grads",
        grid=(q_mine,),
        out_shape=out_shape,
        in_specs=[pl.BlockSpec(memory_space=pltpu.SMEM)]
        + [whole(t.shape) for t in (acc1_all, acc2_all, gcw_all, gwp_all, cond_t)] + [w_spec] * 3,
        out_specs=[w_spec if o is w_like else whole(o.shape) for o in out_shape],
        scratch_shapes=[pltpu.VMEM((6 * n_chunk, N_DEV, 128), F32)],
        compiler_params=pltpu.CompilerParams(dimension_semantics=("arbitrary",), vmem_limit_bytes=VMEM_LIMIT),
    )(my_slot, acc1_all, acc2_all, gcw_all, gwp_all, cond_t, w_ada, m_w_ada, v_w_ada)


def kernel(x, c, w_ada, b_ada, w_in, conv_w, w_pool, pool_scale, w_out, ln1_g, ln1_b, w_mlp_in, w_mlp_out, ln2_g, ln2_b, loss_target, m_w_ada, m_b_ada, m_w_in, m_conv_w, m_w_pool, m_pool_scale, m_w_out, m_ln1_g, m_ln1_b, m_w_mlp_in, m_w_mlp_out, m_ln2_g, m_ln2_b, v_w_ada, v_b_ada, v_w_in, v_conv_w, v_w_pool, v_pool_scale, v_w_out, v_ln1_g, v_ln1_b, v_w_mlp_in, v_w_mlp_out, v_ln2_g, v_ln2_b):
    T, D = x.shape[1], x.shape[2]
    H = w_mlp_out.shape[1] * N_DEV
    ZW = w_in.shape[2] * N_DEV
    CC = ZW // 4
    tm = min(512, T // 2)
    bt = min(1024, T)
    ax, ay, ac = _my_place()
    me = _slot(ax, ay, ac)

    cw3, m_cw3, v_cw3 = (jnp.transpose(t, (1, 0, 2)) for t in (conv_w, m_conv_w, v_conv_w))
    w_in_f, w_out_g, cw_g, c_g, _, mod = _prologue(w_in, w_out, cw3, c, w_ada, b_ada)
    w_out_f = w_out_g.reshape(D, D)
    conv_w_f = jnp.transpose(cw_g[:, :, 0, :], (1, 0, 2)).reshape(conv_w.shape[1], CC)
    c_all = c_g.reshape(N_DEV, D)

    ln = jnp.concatenate([ln1_g, ln1_b, ln2_g, ln2_b], axis=0)
    xs, target = x[0], loss_target[0]

    z, h, xhat1, rstd1, mix, w_mi_f, w_mo_g = _f1(
        xs, mod, w_in_f, conv_w_f, w_pool[0], pool_scale, w_out_f, tm,
        [w_mlp_in, w_mlp_out], [True, False])
    dx1, h2, a, du, df, acc2 = _fb2(xhat1, target, mod, ln, w_mi_f, w_mo_g.reshape(H, D), tm)

    grad_x, dmix, ycat, dz, acc1, gcw, gwp = _b1(
        dx1, xhat1, rstd1, xs, mix, z, mod, ln, w_out_f, w_in_f, conv_w_f, w_pool[0], pool_scale, tm)

    gp_mo = _wgrad(a, df, D, 1, min(4 * bt, T), "wgrad_mlp_out").reshape(N_DEV, H // N_DEV, D)
    gp_mi, rv_mo = _wgrad(h2, du, D, 2, min(2 * bt, T), "wgrad_mlp_in", owners=N_DEV, scatter=gp_mo)
    gp_in, gp_out, rv_mi, acc1_g, acc2_g, gcw_g, gwp_g = _wgrad_pair(
        h, dz, ycat, dmix, bt, "wgrad_in_out", N_DEV, gp_mi, [acc1, acc2, gcw, gwp])
    rv_out, rv_in = _reduce_scatter([gp_out.reshape(N_DEV, D // N_DEV, D), gp_in], "scatter_w_in_out")
    g_b_ada, g_w_ada, g_ln1_g, g_ln1_b, g_ln2_g, g_ln2_b, g_cw, g_pool_scale, g_w_pool, loss, *u_w_ada = _small_grads(
        acc1_g, acc2_g, gcw_g, gwp_g, c_all.T, jnp.reshape(me, (1,)).astype(jnp.int32), w_ada, m_w_ada, v_w_ada)
    cc_mine = conv_w.shape[2]
    g_cw3 = lax.dynamic_slice(g_cw, (0, me * cc_mine), (conv_w.shape[1], cc_mine))[:, None, :]
    g_conv_w = jnp.transpose(g_cw3, (1, 0, 2))

    small = _adamw_multi(
        [
            (b_ada, g_b_ada, m_b_ada, v_b_ada),
            (cw3, g_cw3, m_cw3, v_cw3),
            (w_pool, g_w_pool, m_w_pool, v_w_pool),
            (pool_scale, g_pool_scale, m_pool_scale, v_pool_scale),
            (ln1_g, g_ln1_g, m_ln1_g, v_ln1_g),
            (ln1_b, g_ln1_b, m_ln1_b, v_ln1_b),
            (ln2_g, g_ln2_g, m_ln2_g, v_ln2_g),
            (ln2_b, g_ln2_b, m_ln2_b, v_ln2_b),
        ],
        "adamw_small")
    u_b_ada, u_cw3, u_w_pool, u_pool_scale, u_ln1_g, u_ln1_b, u_ln2_g, u_ln2_b = small
    u_conv_w = tuple(jnp.transpose(t, (1, 0, 2)) for t in u_cw3)

    g_w_mo, *u_w_mo = _sum_adamw(rv_mo, w_mlp_out, m_w_mlp_out, v_w_mlp_out, "sum_w_mlp_out")
    g_w_mi, *u_w_mi = _sum_adamw(rv_mi, w_mlp_in, m_w_mlp_in, v_w_mlp_in, "sum_w_mlp_in")
    g_w_in, *u_w_in = _sum_adamw(rv_in, w_in, m_w_in, v_w_in, "sum_w_in")
    g_w_out, *u_w_out = _sum_adamw(rv_out, w_out, m_w_out, v_w_out, "sum_w_out")

    grads = [g_w_ada, g_b_ada, g_w_in, g_conv_w, g_w_pool, g_pool_scale, g_w_out, g_ln1_g, g_ln1_b, g_w_mi, g_w_mo, g_ln2_g, g_ln2_b]
    updates = [u_w_ada, u_b_ada, u_w_in, u_conv_w, u_w_pool, u_pool_scale, u_w_out, u_ln1_g, u_ln1_b, u_w_mi, u_w_mo, u_ln2_g, u_ln2_b]
    deltas = [u[0] for u in updates]
    new_m = [u[1] for u in updates]
    new_v = [u[2] for u in updates]
    return (loss.reshape(()), grad_x[None], *grads, *deltas, *new_m, *new_v)
```

```python
import jax
import jax.numpy as jnp
from jax import lax
from jax.experimental import pallas as pl
from jax.experimental.pallas import tpu as pltpu

F32 = jnp.float32
BF16 = jnp.bfloat16
MESH = pl.DeviceIdType.MESH
N_DEV = 8

LN_EPS = 1e-5
DEPTH = 1
DEEPNORM_ALPHA = (2.0 * DEPTH) ** 0.25
POOL_WINDOWS = (2, 4, 8, 16)
HALO = 16

ADAM_LR = 0.001
ADAM_B1 = 0.9
ADAM_B2 = 0.999
ADAM_EPS = 1e-08
ADAM_WD = 0.01
ADAM_STEP = 10

VMEM_LIMIT = 60 * 1024 * 1024

VMEM_SPEC = pl.BlockSpec(memory_space=pltpu.VMEM)
ANY_SPEC = pl.BlockSpec(memory_space=pl.ANY)

NT = (((1,), (1,)), ((), ()))
TN = (((0,), (0,)), ((), ()))


def _my_place():
    return lax.axis_index("x"), lax.axis_index("y"), lax.axis_index("c")


def _slot(x, y, c):
    return 4 * x + 2 * y + c


def _gather_place(ins, outs, a, slot):
    if len(outs[a].shape) == len(ins[a].shape):
        wb = ins[a].shape[1]
        return outs[a].at[:, pl.ds(pl.multiple_of(slot * wb, wb), wb)]
    return outs[a].at[slot]


def _gather_copy(ins, outs, sems, a, k, block, to, from_shard=False):
    send_sems, recv_sems, _ = sems
    dst = _gather_place(ins, outs, a, _slot(*block))
    return pltpu.make_async_remote_copy(
        src_ref=ins[a] if from_shard else dst,
        dst_ref=dst,
        send_sem=send_sems.at[7 * a + k],
        recv_sem=recv_sems.at[7 * a + k],
        device_id=to,
        device_id_type=MESH,
    )


def _gather_peers():
    x, y, c = _my_place()
    return (x, y, c), (x, y, 1 - c), [(1 - x, y), (x, 1 - y), (1 - x, 1 - y)]


def _gather_first(ins, outs, sems):
    me, sibling, chips = _gather_peers()
    first = []
    for a in range(len(ins)):
        first.append(_gather_copy(ins, outs, sems, a, 0, me, sibling, from_shard=True))
        first += [_gather_copy(ins, outs, sems, a, 1 + j, me, (*chip, me[2]), from_shard=True) for j, chip in enumerate(chips)]
    return first


def _gather_mine(ins, outs, sems, a):
    me, _, _ = _gather_peers()
    return pltpu.make_async_copy(ins[a], _gather_place(ins, outs, a, _slot(*me)), sems[2].at[a])


def _gather_start(ins, outs, sems):
    for a in range(len(ins)):
        _gather_mine(ins, outs, sems, a).start()
    for cp in _gather_first(ins, outs, sems):
        cp.start()


def _gather_forward(ins, outs, sems, j):
    me, sibling, chips = _gather_peers()
    for a in range(len(ins)):
        _gather_copy(ins, outs, sems, a, 1 + j, (*chips[j], me[2]), me).wait_recv()
        _gather_copy(ins, outs, sems, a, 4 + j, (*chips[j], me[2]), sibling).start()


def _gather_finish(ins, outs, sems):
    me, sibling, chips = _gather_peers()
    for a in range(len(ins)):
        _gather_copy(ins, outs, sems, a, 0, sibling, me).wait_recv()
        for j, chip in enumerate(chips):
            _gather_copy(ins, outs, sems, a, 4 + j, (*chip, 1 - me[2]), me).wait_recv()
    for cp in _gather_first(ins, outs, sems):
        cp.wait_send()
    for a in range(len(ins)):
        for j, chip in enumerate(chips):
            _gather_copy(ins, outs, sems, a, 4 + j, (*chip, me[2]), sibling).wait_send()
        _gather_mine(ins, outs, sems, a).wait()


def _gather_scratch(n):
    return [pltpu.SemaphoreType.DMA((7 * n,)), pltpu.SemaphoreType.DMA((7 * n,)), pltpu.SemaphoreType.DMA((n,))]


def _gather_out_shape(shards, by_cols):
    return [
        jax.ShapeDtypeStruct((s.shape[0], N_DEV * s.shape[1]) if cols else (N_DEV, *s.shape), s.dtype)
        for s, cols in zip(shards, by_cols)
    ]


N_CHIP = 4


def _scatter_scratch(rows, cols):
    block = pltpu.VMEM((N_CHIP, rows, cols), BF16)
    dma = pltpu.SemaphoreType.DMA
    return [block, block, block, dma((N_CHIP,)), dma((N_CHIP,)), dma((N_CHIP,)), dma((N_CHIP - 1,)), dma((N_CHIP - 1,)), dma]


def _scatter_pair_copies(g_hbm, scr):
    x, y, c = _my_place()
    mine, theirs, _, a_send, a_recv, load_sem = scr[:6]
    to_sibling = [
        pltpu.make_async_remote_copy(
            src_ref=g_hbm.at[2 * q + (1 - c)], dst_ref=theirs.at[q], send_sem=a_send.at[q], recv_sem=a_recv.at[q],
            device_id=(x, y, 1 - c), device_id_type=MESH)
        for q in range(N_CHIP)
    ]
    loads = [pltpu.make_async_copy(g_hbm.at[2 * q + c], mine.at[q], load_sem.at[q]) for q in range(N_CHIP)]
    return to_sibling, loads


def _scatter_sum_copies(recv, scr):
    x, y, c = _my_place()
    sums, b_send, b_recv, own_sem = scr[2], scr[6], scr[7], scr[8]
    q_me = 2 * x + y
    to_owner = [
        pltpu.make_async_remote_copy(
            src_ref=sums.at[2 * px + py], dst_ref=recv.at[q_me], send_sem=b_send.at[j], recv_sem=b_recv.at[j],
            device_id=(px, py, c), device_id_type=MESH)
        for j, (px, py) in enumerate([(1 - x, y), (x, 1 - y), (1 - x, 1 - y)])
    ]
    return to_owner, pltpu.make_async_copy(sums.at[q_me], recv.at[q_me], own_sem)


def _scatter_start(g_hbm, scr):
    to_sibling, loads = _scatter_pair_copies(g_hbm, scr)
    for cp in to_sibling + loads:
        cp.start()


def _scatter_middle(g_hbm, recv, scr):
    to_sibling, loads = _scatter_pair_copies(g_hbm, scr)
    for cp in to_sibling:
        cp.wait_recv()
    for cp in loads:
        cp.wait()
    mine, theirs, sums = scr[:3]

    def step(r, carry):
        rs = pl.ds(pl.multiple_of(r * ROW_CHUNK, ROW_CHUNK), ROW_CHUNK)
        for q in range(N_CHIP):
            sums[q, rs, :] = (mine[q, rs, :].astype(F32) + theirs[q, rs, :].astype(F32)).astype(BF16)
        return carry

    lax.fori_loop(0, mine.shape[1] // ROW_CHUNK, step, 0)
    to_owner, own = _scatter_sum_copies(recv, scr)
    for cp in to_owner + [own]:
        cp.start()


def _scatter_finish(g_hbm, recv, scr):
    to_sibling, _ = _scatter_pair_copies(g_hbm, scr)
    to_owner, own = _scatter_sum_copies(recv, scr)
    for cp in to_owner:
        cp.wait_recv()
    for cp in to_sibling + to_owner:
        cp.wait_send()
    own.wait()


def _scatter_out_shape(gparts):
    return jax.ShapeDtypeStruct((N_CHIP, *gparts.shape[1:]), gparts.dtype)


def _adamw_math(w, g, m, v):
    m = ADAM_B1 * m + (1.0 - ADAM_B1) * g
    v = ADAM_B2 * v + (1.0 - ADAM_B2) * (g * g)
    m_hat = m / (1.0 - ADAM_B1**ADAM_STEP)
    v_hat = v / (1.0 - ADAM_B2**ADAM_STEP)
    delta = -ADAM_LR * (m_hat / (jnp.sqrt(v_hat) + ADAM_EPS) + ADAM_WD * w)
    return delta, m, v


ROW_CHUNK = 64


ELEMS_PER_STEP = 128 * 1024


N_SCATTER_SCRATCH = 9


def _reduce_scatter(gparts, name):
    n = len(gparts)

    def body(*refs):
        g_hbm, recv = refs[:n], refs[n : 2 * n]
        scr = [refs[2 * n + k * N_SCATTER_SCRATCH : 2 * n + (k + 1) * N_SCATTER_SCRATCH] for k in range(n)]
        for k in range(n):
            _scatter_start(g_hbm[k], scr[k])
        for k in range(n):
            _scatter_middle(g_hbm[k], recv[k], scr[k])
        for k in range(n):
            _scatter_finish(g_hbm[k], recv[k], scr[k])

    return pl.pallas_call(
        body,
        name=name,
        out_shape=[_scatter_out_shape(g) for g in gparts],
        in_specs=[ANY_SPEC] * n,
        out_specs=[ANY_SPEC] * n,
        scratch_shapes=[s for g in gparts for s in _scatter_scratch(*g.shape[1:])],
        compiler_params=pltpu.CompilerParams(vmem_limit_bytes=VMEM_LIMIT),
    )(*gparts)


def _sum_adamw(parts, w, m, v, name):
    _, rows, cols = w.shape
    rb = rows
    while rb * cols > ELEMS_PER_STEP and rb % 16 == 0:
        rb //= 2

    def body(p_ref, w_ref, m_ref, v_ref, grad_ref, delta_ref, nm_ref, nv_ref):
        g = p_ref[0].astype(F32)
        for k in range(1, p_ref.shape[0]):
            g = g + p_ref[k].astype(F32)
        delta, nm, nv = _adamw_math(w_ref[0], g, m_ref[0], v_ref[0])
        grad_ref[0] = g
        delta_ref[0] = delta
        nm_ref[0] = nm
        nv_ref[0] = nv

    block = lambda lead: pl.BlockSpec((lead, rb, cols), lambda i: (0, i, 0))
    out = jax.ShapeDtypeStruct(w.shape, F32)
    return pl.pallas_call(
        body,
        name=name,
        grid=(rows // rb,),
        out_shape=[out] * 4,
        in_specs=[block(parts.shape[0])] + [block(1)] * 3,
        out_specs=[block(1)] * 4,
        compiler_params=pltpu.CompilerParams(dimension_semantics=("arbitrary",), vmem_limit_bytes=VMEM_LIMIT),
    )(parts, w, m, v)


def _adamw_multi(items, name):
    n = len(items)

    def body(*refs):
        ins, outs = refs[: 4 * n], refs[4 * n :]
        for a in range(n):
            w_ref, g_ref, m_ref, v_ref = ins[4 * a : 4 * a + 4]
            d_ref, nm_ref, nv_ref = outs[3 * a : 3 * a + 3]
            delta, nm, nv = _adamw_math(w_ref[...], g_ref[...], m_ref[...], v_ref[...])
            d_ref[...] = delta
            nm_ref[...] = nm
            nv_ref[...] = nv

    flat = [a for it in items for a in it]
    out_shape = [jax.ShapeDtypeStruct(it[0].shape, F32) for it in items for _ in range(3)]
    outs = pl.pallas_call(
        body,
        name=name,
        out_shape=out_shape,
        in_specs=[VMEM_SPEC] * (4 * n),
        out_specs=[VMEM_SPEC] * (3 * n),
        compiler_params=pltpu.CompilerParams(vmem_limit_bytes=VMEM_LIMIT),
    )(*flat)
    return [tuple(outs[3 * a : 3 * a + 3]) for a in range(n)]


def _prologue(w_in, w_out, conv_w, c, w_ada, b_ada):
    D = c.shape[1]
    wc = w_ada.shape[2]
    shards16 = [jax.ShapeDtypeStruct(w_in.shape[1:], BF16), jax.ShapeDtypeStruct(w_out.shape[1:], BF16)]

    def to_all(src, out, sems):
        x, y, c = _my_place()
        me = _slot(x, y, c)
        copies = [
            pltpu.make_async_remote_copy(
                src_ref=src, dst_ref=out.at[me], send_sem=sems[0].at[k - 1], recv_sem=sems[1].at[k - 1],
                device_id=(x ^ (k >> 2), y ^ ((k >> 1) & 1), c ^ (k & 1)), device_id_type=MESH)
            for k in range(1, N_DEV)
        ]
        return copies, pltpu.make_async_copy(src, out.at[me], sems[2].at[0])

    def start(copies, own):
        for cp in copies + [own]:
            cp.start()

    def finish(copies, own):
        for cp in copies:
            cp.wait()
        own.wait()

    def body(win_ref, wout_ref, cw_ref, c_ref, wada_ref, b_ref, win_g, wout_g, cw_g, c_g, mod_g, mod_ref, win16, wout16, c_s, mp_s, *sems):
        c_copies = to_all(c_ref, c_g, sems[3:6])
        start(*c_copies)
        win16[...] = win_ref[0].astype(BF16)
        wout16[...] = wout_ref[0].astype(BF16)
        w_ins, w_outs, w_sems = (win16, wout16, cw_ref), (win_g, wout_g, cw_g), sems[0:3]
        _gather_start(w_ins, w_outs, w_sems)
        finish(*c_copies)
        for k in range(N_DEV):
            c_s[k : k + 1, :] = c_g[k]
        cv = c_s[...]
        cond = cv * jax.nn.sigmoid(cv)
        b_mine = b_ref[:, pl.ds(pl.multiple_of(_slot(*_my_place()) * wc, 128), wc)]
        mp_s[...] = jnp.dot(cond, wada_ref[0], precision=lax.Precision.HIGHEST, preferred_element_type=F32) + b_mine
        m_copies = to_all(mp_s, mod_g, sems[6:9])
        start(*m_copies)
        for j in range(3):
            _gather_forward(w_ins, w_outs, w_sems, j)
        _gather_finish(w_ins, w_outs, w_sems)
        finish(*m_copies)
        me = _slot(*_my_place())
        for p in range(N_DEV):
            part = mod_g[p, pl.ds(me, 1), :]
            lo = 0
            while lo < wc:
                r, col = divmod(p * wc + lo, D)
                n = min(wc - lo, D - col)
                mod_ref[r : r + 1, col : col + n] = part[:, lo : lo + n]
                lo += n

    return pl.pallas_call(
        body,
        name="prologue",
        out_shape=_gather_out_shape(shards16 + [conv_w], [True, False, False])
        + [jax.ShapeDtypeStruct((N_DEV, 1, D), F32), jax.ShapeDtypeStruct((N_DEV, N_DEV, wc), F32),
           jax.ShapeDtypeStruct((N_DEV * wc // D, D), F32)],
        in_specs=[VMEM_SPEC, VMEM_SPEC, ANY_SPEC] + [VMEM_SPEC] * 3,
        out_specs=[ANY_SPEC] * 3 + [VMEM_SPEC] * 3,
        scratch_shapes=[pltpu.VMEM(s.shape, BF16) for s in shards16]
        + [pltpu.VMEM((N_DEV, D), F32), pltpu.VMEM((N_DEV, wc), F32)]
        + _gather_scratch(3) + _gather_scratch(1) + _gather_scratch(1),
        compiler_params=pltpu.CompilerParams(vmem_limit_bytes=VMEM_LIMIT),
    )(w_in, w_out, conv_w, c, w_ada, b_ada)


def _ln_fwd(r):
    mu = jnp.mean(r, axis=-1, keepdims=True)
    d = r - mu
    var = jnp.mean(d * d, axis=-1, keepdims=True)
    rstd = lax.rsqrt(var + LN_EPS)
    return d * rstd, rstd


def _ln_bwd(dxh, xhat, rstd):
    m1 = jnp.mean(dxh, axis=-1, keepdims=True)
    m2 = jnp.mean(dxh * xhat, axis=-1, keepdims=True)
    return rstd * (dxh - m1 - xhat * m2)


def _colsum(a):
    return jnp.sum(a, axis=0, keepdims=True)


def _window_sums(ext, tm, causal):
    n = ext.shape[0]
    lo = HALO if causal else 0
    s, out = ext, []
    for p in range(len(POOL_WINDOWS)):
        assert POOL_WINDOWS[p] == 2 ** (p + 1)
        k = 2**p
        s = s + pltpu.roll(s, k if causal else n - k, 0)
        out.append(s[lo : lo + tm, 0:128])
        if p + 1 < len(POOL_WINDOWS):
            s = s[:, 128:]
    return out


def _pool_features(vp, vp_s, row, tm):
    sums = _window_sums(vp_s[...], tm, causal=True)
    feats, inv_cnts = [], []
    for g, win in enumerate(POOL_WINDOWS):
        inv_cnt = 1.0 / jnp.minimum(row + 1, win).astype(F32)
        feats.append(sums[g] * inv_cnt - vp[:, 128 * g : 128 * g + 128])
        inv_cnts.append(inv_cnt)
    return feats, inv_cnts


def _f1(x, mod, w_in, conv_w, w_pool, pool_scale, w_out, tm, gather, by_cols):
    T, D = x.shape
    ZW = w_in.shape[1]
    CC = ZW // 4
    nt = T // tm
    ng = len(gather)
    fwd_steps = [max(nt - 3 + j, 0) for j in range(3)]

    shards16 = [jax.ShapeDtypeStruct(s.shape[1:], BF16) for s in gather]

    def body(*refs):
        x_ref, mod_ref, win_ref, cw_ref, wp_ref, ps_ref, wout_ref = refs[:7]
        g_f32 = refs[7 : 7 + ng]
        z_ref, h_ref, xhat_ref, rstd_ref, mix_ref = refs[7 + ng : 12 + ng]
        g_outs = refs[12 + ng : 12 + 2 * ng]
        cv_s, vp_s = refs[12 + 2 * ng : 14 + 2 * ng]
        g_ins = refs[14 + 2 * ng : 14 + 3 * ng]
        g_sems = refs[14 + 3 * ng :]
        i = pl.program_id(0)

        @pl.when(i == 0)
        def _():
            for src, dst in zip(g_f32, g_ins):
                dst[...] = src[0].astype(BF16)
            _gather_start(g_ins, g_outs, g_sems)
            cv_s[0:HALO, :] = jnp.zeros((HALO, CC), F32)
            vp_s[0:HALO, :] = jnp.zeros((HALO, CC), F32)

        xv = x_ref[...]
        sh1, sc1, g1 = mod_ref[0:1, :], mod_ref[1:2, :], mod_ref[2:3, :]
        h = (xv * (1.0 + sc1) + sh1).astype(BF16)
        h_ref[...] = h
        z = jnp.dot(h, win_ref[...], preferred_element_type=F32)
        z_ref[...] = z.astype(BF16)
        gb, gc, vc, vp = z[:, 0:CC], z[:, CC : 2 * CC], z[:, 2 * CC : 3 * CC], z[:, 3 * CC : 4 * CC]
        cv = gc * vc
        cv_s[HALO : HALO + tm, :] = cv
        vp_s[HALO : HALO + tm, :] = vp
        cv_ext = cv_s[...]
        cv_m2 = pltpu.roll(cv_ext, 2, 0)[HALO : HALO + tm, :]
        cv_m1 = pltpu.roll(cv_ext, 1, 0)[HALO : HALO + tm, :]
        conv = cw_ref[0:1, :] * cv_m2 + cw_ref[1:2, :] * cv_m1 + cw_ref[2:3, :] * cv
        parts = [gb * conv]
        row = i * tm + lax.broadcasted_iota(jnp.int32, (tm, 1), 0)
        feats, _ = _pool_features(vp, vp_s, row, tm)
        for g in range(len(POOL_WINDOWS)):
            pw = jnp.dot(feats[g].astype(BF16), wp_ref[g].astype(BF16), preferred_element_type=F32)
            parts.append(pw * ps_ref[0:1, 128 * g : 128 * g + 128])
        cv_s[0:HALO, :] = cv_s[tm : tm + HALO, :]
        vp_s[0:HALO, :] = vp_s[tm : tm + HALO, :]
        ycat = jnp.concatenate(parts, axis=1).astype(BF16)
        mix = jnp.dot(ycat, wout_ref[...], preferred_element_type=F32)
        mix_ref[...] = mix
        xhat, rstd = _ln_fwd(DEEPNORM_ALPHA * xv + (1.0 + g1) * mix)
        xhat_ref[...] = xhat
        rstd_ref[...] = rstd

        for j in range(3):

            @pl.when(i == fwd_steps[j])
            def _(j=j):
                _gather_forward(g_ins, g_outs, g_sems, j)

        @pl.when(i == nt - 1)
        def _():
            _gather_finish(g_ins, g_outs, g_sems)

    tile = lambda w: pl.BlockSpec((tm, w), lambda i: (i, 0))
    return pl.pallas_call(
        body,
        name="f1",
        grid=(nt,),
        out_shape=[
            jax.ShapeDtypeStruct((T, ZW), BF16),
            jax.ShapeDtypeStruct((T, D), BF16),
            jax.ShapeDtypeStruct((T, D), F32),
            jax.ShapeDtypeStruct((T, 1), F32),
            jax.ShapeDtypeStruct((T, D), F32),
        ]
        + _gather_out_shape(shards16, by_cols),
        in_specs=[tile(D)] + [VMEM_SPEC] * (6 + ng),
        out_specs=[tile(ZW), tile(D), tile(D), tile(1), tile(D)] + [ANY_SPEC] * ng,
        scratch_shapes=[pltpu.VMEM((HALO + tm, CC), F32), pltpu.VMEM((HALO + tm, CC), F32)]
        + [pltpu.VMEM(s.shape, BF16) for s in shards16]
        + _gather_scratch(ng),
        compiler_params=pltpu.CompilerParams(dimension_semantics=("arbitrary",), vmem_limit_bytes=VMEM_LIMIT),
    )(x, mod, w_in, conv_w, w_pool, pool_scale, w_out, *gather)


def _fb2(xhat1, target, mod, ln, w_mi, w_mo, tm):
    T, D = xhat1.shape
    H = w_mi.shape[1]
    hc = min(1024, H)
    nb = H // hc
    nt = T // tm

    def body(xh1_ref, t_ref, mod_ref, ln_ref, wmi_ref, wmo_ref, dx1_ref, h2_ref, a_ref, du_ref, df_ref, acc_ref):
        i = pl.program_id(0)

        @pl.when(i == 0)
        def _():
            acc_ref[...] = jnp.zeros((8, D), F32)

        sh2, sc2, g2 = mod_ref[3:4, :], mod_ref[4:5, :], mod_ref[5:6, :]
        x1 = xh1_ref[...] * ln_ref[0:1, :] + ln_ref[1:2, :]
        h2 = (x1 * (1.0 + sc2) + sh2).astype(BF16)
        h2_ref[...] = h2
        f = jnp.zeros((tm, D), F32)
        for k in range(nb):
            ks = slice(k * hc, (k + 1) * hc)
            r = jnp.maximum(jnp.dot(h2, wmi_ref[:, ks], preferred_element_type=F32), 0.0)
            du_ref[:, ks] = r.astype(BF16)
            a = (r * r).astype(BF16)
            a_ref[:, ks] = a
            f = f + jnp.dot(a, wmo_ref[ks, :], preferred_element_type=F32)
        xhat2, rstd2 = _ln_fwd(DEEPNORM_ALPHA * x1 + (1.0 + g2) * f)
        ln2_g = ln_ref[2:3, :]
        d = xhat2 * ln2_g + ln_ref[3:4, :] - t_ref[...]
        dr2 = _ln_bwd(d * (ln2_g * (1.0 / D)), xhat2, rstd2)
        df = ((1.0 + g2) * dr2).astype(BF16)
        df_ref[...] = df
        dh2 = jnp.zeros((tm, D), F32)
        for k in range(nb):
            ks = slice(k * hc, (k + 1) * hc)
            da = lax.dot_general(df, wmo_ref[ks, :], NT, preferred_element_type=F32)
            du = (da * (2.0 * du_ref[:, ks].astype(F32))).astype(BF16)
            du_ref[:, ks] = du
            dh2 = dh2 + lax.dot_general(du, wmi_ref[:, ks], NT, preferred_element_type=F32)
        dx1_ref[...] = DEEPNORM_ALPHA * dr2 + dh2 * (1.0 + sc2)
        acc_ref[0:1, :] += _colsum(d * xhat2) * (1.0 / D)
        acc_ref[1:2, :] += _colsum(d) * (1.0 / D)
        acc_ref[2:3, :] += _colsum(dh2)
        acc_ref[3:4, :] += _colsum(dh2 * x1)
        acc_ref[4:5, :] += _colsum(dr2 * f)
        acc_ref[5:6, :] += jnp.zeros((1, D), F32) + (0.5 / D) * jnp.sum(d * d)

    tile = lambda w: pl.BlockSpec((tm, w), lambda i: (i, 0))
    return pl.pallas_call(
        body,
        name="fb2",
        grid=(nt,),
        out_shape=[
            jax.ShapeDtypeStruct((T, D), F32),
            jax.ShapeDtypeStruct((T, D), BF16),
            jax.ShapeDtypeStruct((T, H), BF16),
            jax.ShapeDtypeStruct((T, H), BF16),
            jax.ShapeDtypeStruct((T, D), BF16),
            jax.ShapeDtypeStruct((8, D), F32),
        ],
        in_specs=[tile(D), tile(D)] + [VMEM_SPEC] * 4,
        out_specs=[tile(D), tile(D), tile(H), tile(H), tile(D), pl.BlockSpec((8, D), lambda i: (0, 0))],
        compiler_params=pltpu.CompilerParams(dimension_semantics=("arbitrary",), vmem_limit_bytes=VMEM_LIMIT),
    )(xhat1, target, mod, ln, w_mi, w_mo)


def _b1(dx1, xhat1, rstd1, x, mix, z, mod, ln, w_out, w_in, conv_w, w_pool, pool_scale, tm):
    T, D = x.shape
    ZW = w_in.shape[1]
    CC = ZW // 4
    nt = T // tm
    hb = tm // HALO

    def body(dx1_ref, xh1_ref, rstd_ref, x_ref, mix_ref, z_ref, zh_ref, mod_ref, ln_ref, wout_ref, win_ref, cw_ref, wp_ref, ps_ref,
             dx_ref, dmix_ref, ycat_ref, dz_ref, acc_ref, gcw_ref, gwp_ref, cv_s, vp_s, e_s, q_s):
        i = pl.program_id(0)
        j = nt - 1 - i

        @pl.when(i == 0)
        def _():
            acc_ref[...] = jnp.zeros((8, D), F32)
            gcw_ref[...] = jnp.zeros((8, CC), F32)
            gwp_ref[...] = jnp.zeros(gwp_ref.shape, F32)
            e_s[tm : tm + HALO, :] = jnp.zeros((HALO, CC), F32)
            q_s[tm : tm + HALO, :] = jnp.zeros((HALO, CC), F32)

        sh1, sc1, g1 = mod_ref[0:1, :], mod_ref[1:2, :], mod_ref[2:3, :]
        dx1 = dx1_ref[...]
        xhat1 = xh1_ref[...]
        acc_ref[0:1, :] += _colsum(dx1 * xhat1)
        acc_ref[1:2, :] += _colsum(dx1)
        dr1 = _ln_bwd(dx1 * ln_ref[0:1, :], xhat1, rstd_ref[...])
        acc_ref[4:5, :] += _colsum(dr1 * mix_ref[...])
        dmix = ((1.0 + g1) * dr1).astype(BF16)
        dmix_ref[...] = dmix
        dycat = lax.dot_general(dmix, wout_ref[...], NT, preferred_element_type=F32)

        z = z_ref[...].astype(F32)
        zh = zh_ref[...].astype(F32) * jnp.where(j > 0, 1.0, 0.0)
        gb, gc, vc, vp = z[:, 0:CC], z[:, CC : 2 * CC], z[:, 2 * CC : 3 * CC], z[:, 3 * CC : 4 * CC]
        cv = gc * vc
        cv_s[0:HALO, :] = zh[:, CC : 2 * CC] * zh[:, 2 * CC : 3 * CC]
        cv_s[HALO : HALO + tm, :] = cv
        vp_s[0:HALO, :] = zh[:, 3 * CC : 4 * CC]
        vp_s[HALO : HALO + tm, :] = vp
        cv_ext = cv_s[...]
        cv_m2 = pltpu.roll(cv_ext, 2, 0)[HALO : HALO + tm, :]
        cv_m1 = pltpu.roll(cv_ext, 1, 0)[HALO : HALO + tm, :]
        w0, w1, w2 = cw_ref[0:1, :], cw_ref[1:2, :], cw_ref[2:3, :]
        conv = w0 * cv_m2 + w1 * cv_m1 + w2 * cv
        dyc = dycat[:, 0:CC]
        e = dyc * gb
        e_s[0:tm, :] = e
        e_ext = e_s[...]
        dcv = w2 * e + w1 * pltpu.roll(e_ext, tm + HALO - 1, 0)[0:tm, :] + w0 * pltpu.roll(e_ext, tm + HALO - 2, 0)[0:tm, :]
        gcw_ref[0:1, :] += _colsum(e * cv_m2)
        gcw_ref[1:2, :] += _colsum(e * cv_m1)
        gcw_ref[2:3, :] += _colsum(e * cv)
        y_parts = [gb * conv]
        dz_parts = [dyc * conv, dcv * vc, dcv * gc]

        row = j * tm + lax.broadcasted_iota(jnp.int32, (tm, 1), 0)
        feats, inv_cnts = _pool_features(vp, vp_s, row, tm)
        gps_parts, dps = [], []
        for g in range(len(POOL_WINDOWS)):
            cols = slice(128 * g, 128 * g + 128)
            p = feats[g].astype(BF16)
            scale = ps_ref[0:1, cols]
            wp = wp_ref[g].astype(BF16)
            pw = jnp.dot(p, wp, preferred_element_type=F32)
            y_parts.append(pw * scale)
            dyp = dycat[:, CC + 128 * g : CC + 128 * g + 128]
            gps_parts.append(_colsum(dyp * pw))
            dpw = (dyp * scale).astype(BF16)
            gwp_ref[g] += lax.dot_general(p, dpw, TN, preferred_element_type=F32)
            dp = lax.dot_general(dpw, wp, NT, preferred_element_type=F32)
            q_s[0:tm, cols] = dp * inv_cnts[g]
            dps.append(dp)
        sq = _window_sums(q_s[...], tm, causal=False)
        dz_parts += [sq[g] - dps[g] for g in range(len(POOL_WINDOWS))]
        gcw_ref[3:4, :] += jnp.concatenate(gps_parts, axis=1)
        ycat_ref[...] = jnp.concatenate(y_parts, axis=1).astype(BF16)
        dz = jnp.concatenate(dz_parts, axis=1).astype(BF16)
        dz_ref[...] = dz
        dh = lax.dot_general(dz, win_ref[...], NT, preferred_element_type=F32)
        acc_ref[2:3, :] += _colsum(dh)
        acc_ref[3:4, :] += _colsum(dh * x_ref[...])
        dx_ref[...] = DEEPNORM_ALPHA * dr1 + dh * (1.0 + sc1)
        e_s[tm : tm + HALO, :] = e_s[0:HALO, :]
        q_s[tm : tm + HALO, :] = q_s[0:HALO, :]

    tile = lambda w: pl.BlockSpec((tm, w), lambda i: (nt - 1 - i, 0))
    halo = pl.BlockSpec((HALO, ZW), lambda i: (jnp.maximum((nt - 1 - i) * hb - 1, 0), 0))
    fixed = lambda shape: pl.BlockSpec(shape, lambda i: (0,) * len(shape))
    return pl.pallas_call(
        body,
        name="b1",
        grid=(nt,),
        out_shape=[
            jax.ShapeDtypeStruct((T, D), F32),
            jax.ShapeDtypeStruct((T, D), BF16),
            jax.ShapeDtypeStruct((T, D), BF16),
            jax.ShapeDtypeStruct((T, ZW), BF16),
            jax.ShapeDtypeStruct((8, D), F32),
            jax.ShapeDtypeStruct((8, CC), F32),
            jax.ShapeDtypeStruct(w_pool.shape, F32),
        ],
        in_specs=[tile(D), tile(D), tile(1), tile(D), tile(D), tile(ZW), halo] + [VMEM_SPEC] * 7,
        out_specs=[tile(D), tile(D), tile(D), tile(ZW), fixed((8, D)), fixed((8, CC)), fixed(w_pool.shape)],
        scratch_shapes=[
            pltpu.VMEM((HALO + tm, CC), F32),
            pltpu.VMEM((HALO + tm, CC), F32),
            pltpu.VMEM((tm + HALO, CC), F32),
            pltpu.VMEM((tm + HALO, CC), F32),
        ],
        compiler_params=pltpu.CompilerParams(dimension_semantics=("arbitrary",), vmem_limit_bytes=VMEM_LIMIT),
    )(dx1, xhat1, rstd1, x, mix, z, z, mod, ln, w_out, w_in, conv_w, w_pool, pool_scale)


def _wgrad(a, b, bk, n_groups, bt, name, owners=None, scatter=None, gather=()):
    T, K = a.shape
    N = b.shape[1]
    nk, nt, ng = K // bk, T // bt, N // n_groups
    nc = min(512, ng)
    ns, ngat = (0 if scatter is None else 1), len(gather)
    n_steps = nk * n_groups * nt
    mid_step = min(1, n_steps - 1)
    fwd_steps = [min(2 * (j + 1), n_steps - 1) for j in range(3)]

    def body(*refs):
        a_ref, b_ref = refs[0], refs[1]
        g_ins = refs[2 + ns : 2 + ns + ngat]
        outs = refs[2 + ns + ngat :]
        o_ref, g_outs = outs[0], outs[1 + ns : 1 + ns + ngat]
        scr = outs[1 + ns + ngat :]
        acc = scr[0]
        if ns:
            s_hbm, s_recv, s_scr = refs[2], outs[1], scr[1 : 1 + N_SCATTER_SCRATCH]
        g_sems = scr[1 + N_SCATTER_SCRATCH * ns :]
        kk, gg, t = pl.program_id(0), pl.program_id(1), pl.program_id(2)
        step = (kk * n_groups + gg) * nt + t

        if ngat:

            @pl.when(step == 0)
            def _():
                _gather_start(g_ins, g_outs, g_sems)

            for j in range(3):

                @pl.when(step == fwd_steps[j])
                def _(j=j):
                    _gather_forward(g_ins, g_outs, g_sems, j)

        if ns:

            @pl.when(step == 0)
            def _():
                _scatter_start(s_hbm, s_scr)

            @pl.when(step == mid_step)
            def _():
                _scatter_middle(s_hbm, s_recv, s_scr)

        @pl.when(t == 0)
        def _():
            acc[...] = jnp.zeros(acc.shape, F32)

        at = a_ref[...].T
        for c in range(ng // nc):
            cs = slice(c * nc, (c + 1) * nc)
            acc[:, cs] += jnp.dot(at, b_ref[:, cs], preferred_element_type=F32)

        @pl.when(t == nt - 1)
        def _():
            if owners is None:
                o_ref[...] = acc[...].astype(BF16)
            else:
                per = N // owners
                for o in range(ng // per):
                    o_ref[o] = acc[:, o * per : (o + 1) * per].astype(BF16)

        if ns:

            @pl.when(step == n_steps - 1)
            def _():
                _scatter_finish(s_hbm, s_recv, s_scr)

        if ngat:

            @pl.when(step == n_steps - 1)
            def _():
                _gather_finish(g_ins, g_outs, g_sems)

    if owners is None:
        out_shape = [jax.ShapeDtypeStruct((K, N), BF16)]
        out_specs = [pl.BlockSpec((bk, ng), lambda k, g, t: (k, g))]
    else:
        assert bk == K
        per = N // owners
        out_shape = [jax.ShapeDtypeStruct((owners, K, per), BF16)]
        out_specs = [pl.BlockSpec((ng // per, K, per), lambda k, g, t: (g, 0, 0))]
    ins, in_specs = [a, b], [pl.BlockSpec((bt, bk), lambda k, g, t: (t, k)), pl.BlockSpec((bt, ng), lambda k, g, t: (t, g))]
    scratch = [pltpu.VMEM((bk, ng), F32)]
    if ns:
        ins.append(scatter)
        in_specs.append(ANY_SPEC)
        out_shape.append(_scatter_out_shape(scatter))
        out_specs.append(ANY_SPEC)
        scratch += _scatter_scratch(*scatter.shape[1:])
    if ngat:
        ins += list(gather)
        in_specs += [ANY_SPEC] * ngat
        out_shape += _gather_out_shape(gather, [False] * ngat)
        out_specs += [ANY_SPEC] * ngat
        scratch += _gather_scratch(ngat)
    outs = pl.pallas_call(
        body,
        name=name,
        grid=(nk, n_groups, nt),
        out_shape=out_shape,
        in_specs=in_specs,
        out_specs=out_specs,
        scratch_shapes=scratch,
        compiler_params=pltpu.CompilerParams(dimension_semantics=("arbitrary", "arbitrary", "arbitrary"), vmem_limit_bytes=VMEM_LIMIT),
    )(*ins)
    return outs if ns + ngat else outs[0]


def _wgrad_pair(a0, b0, a1, b1, bt, name, owners, scatter, gather):
    T, K0 = a0.shape
    N0 = b0.shape[1]
    K1, N1 = a1.shape[1], b1.shape[1]
    nt = T // bt
    nc = 512
    per = N0 // owners
    ngat = len(gather)
    n_steps = 2 * nt
    fwd_steps = [min(2 * (j + 1), n_steps - 1) for j in range(3)]

    def body(*refs):
        a0_ref, b0_ref, a1_ref, b1_ref, s_hbm = refs[:5]
        g_ins = refs[5 : 5 + ngat]
        o0_ref, o1_ref, s_recv = refs[5 + ngat : 8 + ngat]
        g_outs = refs[8 + ngat : 8 + 2 * ngat]
        scr = refs[8 + 2 * ngat :]
        acc0, acc1 = scr[0], scr[1]
        s_scr, g_sems = scr[2 : 2 + N_SCATTER_SCRATCH], scr[2 + N_SCATTER_SCRATCH :]
        p, t = pl.program_id(0), pl.program_id(1)
        step = p * nt + t

        @pl.when(step == 0)
        def _():
            _scatter_start(s_hbm, s_scr)
            _gather_start(g_ins, g_outs, g_sems)

        @pl.when(step == 1)
        def _():
            _scatter_middle(s_hbm, s_recv, s_scr)

        for j in range(3):

            @pl.when(step == fwd_steps[j])
            def _(j=j):
                _gather_forward(g_ins, g_outs, g_sems, j)

        def accumulate(a_ref, b_ref, acc, n_cols):
            @pl.when(t == 0)
            def _():
                acc[...] = jnp.zeros(acc.shape, F32)

            at = a_ref[...].T
            for c in range(n_cols // nc):
                cs = slice(c * nc, (c + 1) * nc)
                acc[:, cs] += jnp.dot(at, b_ref[:, cs], preferred_element_type=F32)

        @pl.when(p == 0)
        def _():
            accumulate(a0_ref, b0_ref, acc0, N0)

            @pl.when(t == nt - 1)
            def _():
                for o in range(owners):
                    o0_ref[o] = acc0[:, o * per : (o + 1) * per].astype(BF16)

        @pl.when(p == 1)
        def _():
            accumulate(a1_ref, b1_ref, acc1, N1)

            @pl.when(t == nt - 1)
            def _():
                o1_ref[...] = acc1[...].astype(BF16)

        @pl.when(step == n_steps - 1)
        def _():
            _scatter_finish(s_hbm, s_recv, s_scr)
            _gather_finish(g_ins, g_outs, g_sems)

    first = lambda w: pl.BlockSpec((bt, w), lambda p, t: (jnp.where(p == 0, t, nt - 1), 0))
    second = lambda w: pl.BlockSpec((bt, w), lambda p, t: (jnp.where(p == 1, t, 0), 0))
    return pl.pallas_call(
        body,
        name=name,
        grid=(2, nt),
        out_shape=[jax.ShapeDtypeStruct((owners, K0, per), BF16), jax.ShapeDtypeStruct((K1, N1), BF16), _scatter_out_shape(scatter)]
        + _gather_out_shape(gather, [False] * ngat),
        in_specs=[first(K0), first(N0), second(K1), second(N1), ANY_SPEC] + [ANY_SPEC] * ngat,
        out_specs=[pl.BlockSpec((owners, K0, per), lambda p, t: (0, 0, 0)), pl.BlockSpec((K1, N1), lambda p, t: (0, 0)), ANY_SPEC]
        + [ANY_SPEC] * ngat,
        scratch_shapes=[pltpu.VMEM((K0, N0), F32), pltpu.VMEM((K1, N1), F32)]
        + _scatter_scratch(*scatter.shape[1:]) + _gather_scratch(ngat),
        compiler_params=pltpu.CompilerParams(dimension_semantics=("arbitrary", "arbitrary"), vmem_limit_bytes=VMEM_LIMIT),
    )(a0, b0, a1, b1, scatter, *gather)


def _small_grads(acc1_all, acc2_all, gcw_all, gwp_all, cond_t, my_slot, w_ada, m_w_ada, v_w_ada):
    D = acc1_all.shape[2]
    w_cols = w_ada.shape[2]
    n_chunk = D // 128
    q_mine = w_cols // 128

    def total(ref, r):
        s = ref[0, r : r + 1, :]
        for k in range(1, N_DEV):
            s = s + ref[k, r : r + 1, :]
        return s

    def body(slot_ref, a1_ref, a2_ref, gcw_ref, gwp_ref, ct_ref, w_ref, m_ref, v_ref,
             gb_ref, gw_ref, gl1g_ref, gl1b_ref, gl2g_ref, gl2b_ref, gcwo_ref, gps_ref, gwpo_ref, loss_ref,
             dw_ref, nm_ref, nv_ref, dm_s):
        @pl.when(pl.program_id(0) == 0)
        def _():
            loss_ref[...] = total(a2_ref, 5)[:, 0:1]
            for s, (ref, r) in enumerate([(a1_ref, 2), (a1_ref, 3), (a1_ref, 4), (a2_ref, 2), (a2_ref, 3), (a2_ref, 4)]):
                gb_ref[0:1, s * D : (s + 1) * D] = total(ref, r)
                for k in range(N_DEV):
                    row = ref[k, r : r + 1, :]
                    for qq in range(n_chunk):
                        dm_s[s * n_chunk + qq, k : k + 1, :] = row[:, 128 * qq : 128 * qq + 128]
            gl1g_ref[...] = total(a1_ref, 0)
            gl1b_ref[...] = total(a1_ref, 1)
            gl2g_ref[...] = total(a2_ref, 0)
            gl2b_ref[...] = total(a2_ref, 1)
            gcwo_ref[...] = jnp.zeros(gcwo_ref.shape, F32)
            for r in range(3):
                gcwo_ref[r : r + 1, :] = total(gcw_ref, r)
            gps_ref[...] = total(gcw_ref, 3)
            wp = gwp_ref[0]
            for k in range(1, N_DEV):
                wp = wp + gwp_ref[k]
            gwpo_ref[0] = wp

        ct = ct_ref[...]
        cond_t = ct * jax.nn.sigmoid(ct)
        q0 = slot_ref[0] * q_mine
        for q in range(q_mine):
            dm = dm_s[q0 + q]
            out = cond_t[:, 0:1] * dm[0:1, :]
            for k in range(1, N_DEV):
                out = out + cond_t[:, k : k + 1] * dm[k : k + 1, :]
            cols = slice(128 * q, 128 * q + 128)
            gw_ref[0, :, cols] = out
            delta, nm, nv = _adamw_math(w_ref[0, :, cols], out, m_ref[0, :, cols], v_ref[0, :, cols])
            dw_ref[0, :, cols] = delta
            nm_ref[0, :, cols] = nm
            nv_ref[0, :, cols] = nv

    CC = gcw_all.shape[2]
    w_like = jax.ShapeDtypeStruct(w_ada.shape, F32)
    rb = D // 4
    w_spec = pl.BlockSpec((1, rb, w_cols), lambda i: (0, i, 0))
    whole = lambda shape: pl.BlockSpec(shape, lambda i: (0,) * len(shape))
    out_shape = [
        jax.ShapeDtypeStruct((1, 6 * D), F32),
        w_like,
        *[jax.ShapeDtypeStruct((1, D), F32)] * 4,
        jax.ShapeDtypeStruct((8, CC), F32),
        jax.ShapeDtypeStruct((1, CC), F32),
        jax.ShapeDtypeStruct((1, *gwp_all.shape[1:]), F32),
        jax.ShapeDtypeStruct((1, 1), F32),
        w_like,
        w_like,
        w_like,
    ]
    return pl.pallas_call(
        body,
        name="small_grads",
        grid=(D // rb,),
        out_shape=out_shape,
        in_specs=[pl.BlockSpec(memory_space=pltpu.SMEM)]
        + [whole(t.shape) for t in (acc1_all, acc2_all, gcw_all, gwp_all)]
        + [pl.BlockSpec((rb, N_DEV), lambda i: (i, 0))] + [w_spec] * 3,
        out_specs=[w_spec if o is w_like else whole(o.shape) for o in out_shape],
        scratch_shapes=[pltpu.VMEM((6 * n_chunk, N_DEV, 128), F32)],
        compiler_params=pltpu.CompilerParams(dimension_semantics=("arbitrary",), vmem_limit_bytes=VMEM_LIMIT),
    )(my_slot, acc1_all, acc2_all, gcw_all, gwp_all, cond_t, w_ada, m_w_ada, v_w_ada)


def kernel(x, c, w_ada, b_ada, w_in, conv_w, w_pool, pool_scale, w_out, ln1_g, ln1_b, w_mlp_in, w_mlp_out, ln2_g, ln2_b, loss_target, m_w_ada, m_b_ada, m_w_in, m_conv_w, m_w_pool, m_pool_scale, m_w_out, m_ln1_g, m_ln1_b, m_w_mlp_in, m_w_mlp_out, m_ln2_g, m_ln2_b, v_w_ada, v_b_ada, v_w_in, v_conv_w, v_w_pool, v_pool_scale, v_w_out, v_ln1_g, v_ln1_b, v_w_mlp_in, v_w_mlp_out, v_ln2_g, v_ln2_b):
    T, D = x.shape[1], x.shape[2]
    H = w_mlp_out.shape[1] * N_DEV
    ZW = w_in.shape[2] * N_DEV
    CC = ZW // 4
    tm = min(512, T // 2)
    bt = min(1024, T)
    ax, ay, ac = _my_place()
    me = _slot(ax, ay, ac)

    cw3, m_cw3, v_cw3 = (jnp.transpose(t, (1, 0, 2)) for t in (conv_w, m_conv_w, v_conv_w))
    w_in_f, w_out_g, cw_g, c_g, _, mod = _prologue(w_in, w_out, cw3, c, w_ada, b_ada)
    w_out_f = w_out_g.reshape(D, D)
    conv_w_f = jnp.transpose(cw_g[:, :, 0, :], (1, 0, 2)).reshape(conv_w.shape[1], CC)
    c_all = c_g.reshape(N_DEV, D)

    ln = jnp.concatenate([ln1_g, ln1_b, ln2_g, ln2_b], axis=0)
    xs, target = x[0], loss_target[0]

    z, h, xhat1, rstd1, mix, w_mi_f, w_mo_g = _f1(
        xs, mod, w_in_f, conv_w_f, w_pool[0], pool_scale, w_out_f, tm,
        [w_mlp_in, w_mlp_out], [True, False])
    dx1, h2, a, du, df, acc2 = _fb2(xhat1, target, mod, ln, w_mi_f, w_mo_g.reshape(H, D), tm)

    grad_x, dmix, ycat, dz, acc1, gcw, gwp = _b1(
        dx1, xhat1, rstd1, xs, mix, z, mod, ln, w_out_f, w_in_f, conv_w_f, w_pool[0], pool_scale, tm)

    gp_mo = _wgrad(a, df, D, 1, min(4 * bt, T), "wgrad_mlp_out").reshape(N_DEV, H // N_DEV, D)
    gp_mi, rv_mo = _wgrad(h2, du, D, 2, min(2 * bt, T), "wgrad_mlp_in", owners=N_DEV, scatter=gp_mo)
    gp_in, gp_out, rv_mi, acc1_g, acc2_g, gcw_g, gwp_g = _wgrad_pair(
        h, dz, ycat, dmix, bt, "wgrad_in_out", N_DEV, gp_mi, [acc1, acc2, gcw, gwp])
    rv_out, rv_in = _reduce_scatter([gp_out.reshape(N_DEV, D // N_DEV, D), gp_in], "scatter_w_in_out")
    g_b_ada, g_w_ada, g_ln1_g, g_ln1_b, g_ln2_g, g_ln2_b, g_cw, g_pool_scale, g_w_pool, loss, *u_w_ada = _small_grads(
        acc1_g, acc2_g, gcw_g, gwp_g, c_all.T, jnp.reshape(me, (1,)).astype(jnp.int32), w_ada, m_w_ada, v_w_ada)
    cc_mine = conv_w.shape[2]
    g_cw3 = lax.dynamic_slice(g_cw, (0, me * cc_mine), (conv_w.shape[1], cc_mine))[:, None, :]
    g_conv_w = jnp.transpose(g_cw3, (1, 0, 2))

    small = _adamw_multi(
        [
            (b_ada, g_b_ada, m_b_ada, v_b_ada),
            (cw3, g_cw3, m_cw3, v_cw3),
            (w_pool, g_w_pool, m_w_pool, v_w_pool),
            (pool_scale, g_pool_scale, m_pool_scale, v_pool_scale),
            (ln1_g, g_ln1_g, m_ln1_g, v_ln1_g),
            (ln1_b, g_ln1_b, m_ln1_b, v_ln1_b),
            (ln2_g, g_ln2_g, m_ln2_g, v_ln2_g),
            (ln2_b, g_ln2_b, m_ln2_b, v_ln2_b),
        ],
        "adamw_small")
    u_b_ada, u_cw3, u_w_pool, u_pool_scale, u_ln1_g, u_ln1_b, u_ln2_g, u_ln2_b = small
    u_conv_w = tuple(jnp.transpose(t, (1, 0, 2)) for t in u_cw3)

    g_w_mo, *u_w_mo = _sum_adamw(rv_mo, w_mlp_out, m_w_mlp_out, v_w_mlp_out, "sum_w_mlp_out")
    g_w_mi, *u_w_mi = _sum_adamw(rv_mi, w_mlp_in, m_w_mlp_in, v_w_mlp_in, "sum_w_mlp_in")
    g_w_in, *u_w_in = _sum_adamw(rv_in, w_in, m_w_in, v_w_in, "sum_w_in")
    g_w_out, *u_w_out = _sum_adamw(rv_out, w_out, m_w_out, v_w_out, "sum_w_out")

    grads = [g_w_ada, g_b_ada, g_w_in, g_conv_w, g_w_pool, g_pool_scale, g_w_out, g_ln1_g, g_ln1_b, g_w_mi, g_w_mo, g_ln2_g, g_ln2_b]
    updates = [u_w_ada, u_b_ada, u_w_in, u_conv_w, u_w_pool, u_pool_scale, u_w_out, u_ln1_g, u_ln1_b, u_w_mi, u_w_mo, u_ln2_g, u_ln2_b]
    deltas = [u[0] for u in updates]
    new_m = [u[1] for u in updates]
    new_v = [u[2] for u in updates]
    return (loss.reshape(()), grad_x[None], *grads, *deltas, *new_m, *new_v)
```

```python
import jax
import jax.numpy as jnp
from jax import lax
from jax.experimental import pallas as pl
from jax.experimental.pallas import tpu as pltpu

F32 = jnp.float32
BF16 = jnp.bfloat16
MESH = pl.DeviceIdType.MESH
N_DEV = 8

LN_EPS = 1e-5
DEPTH = 1
DEEPNORM_ALPHA = (2.0 * DEPTH) ** 0.25
POOL_WINDOWS = (2, 4, 8, 16)
HALO = 16

ADAM_LR = 0.001
ADAM_B1 = 0.9
ADAM_B2 = 0.999
ADAM_EPS = 1e-08
ADAM_WD = 0.01
ADAM_STEP = 10

VMEM_LIMIT = 60 * 1024 * 1024

VMEM_SPEC = pl.BlockSpec(memory_space=pltpu.VMEM)
ANY_SPEC = pl.BlockSpec(memory_space=pl.ANY)

NT = (((1,), (1,)), ((), ()))
TN = (((0,), (0,)), ((), ()))


def _my_place():
    return lax.axis_index("x"), lax.axis_index("y"), lax.axis_index("c")


def _slot(x, y, c):
    return 4 * x + 2 * y + c


def _gather_place(ins, outs, a, slot):
    if len(outs[a].shape) == len(ins[a].shape):
        wb = ins[a].shape[1]
        return outs[a].at[:, pl.ds(pl.multiple_of(slot * wb, wb), wb)]
    return outs[a].at[slot]


def _gather_copy(ins, outs, sems, a, k, block, to, from_shard=False):
    send_sems, recv_sems, _ = sems
    dst = _gather_place(ins, outs, a, _slot(*block))
    return pltpu.make_async_remote_copy(
        src_ref=ins[a] if from_shard else dst,
        dst_ref=dst,
        send_sem=send_sems.at[7 * a + k],
        recv_sem=recv_sems.at[7 * a + k],
        device_id=to,
        device_id_type=MESH,
    )


def _gather_peers():
    x, y, c = _my_place()
    return (x, y, c), (x, y, 1 - c), [(1 - x, y), (x, 1 - y), (1 - x, 1 - y)]


def _gather_first(ins, outs, sems):
    me, sibling, chips = _gather_peers()
    first = []
    for a in range(len(ins)):
        first.append(_gather_copy(ins, outs, sems, a, 0, me, sibling, from_shard=True))
        first += [_gather_copy(ins, outs, sems, a, 1 + j, me, (*chip, me[2]), from_shard=True) for j, chip in enumerate(chips)]
    return first


def _gather_mine(ins, outs, sems, a):
    me, _, _ = _gather_peers()
    return pltpu.make_async_copy(ins[a], _gather_place(ins, outs, a, _slot(*me)), sems[2].at[a])


def _gather_start(ins, outs, sems):
    for a in range(len(ins)):
        _gather_mine(ins, outs, sems, a).start()
    for cp in _gather_first(ins, outs, sems):
        cp.start()


def _gather_forward(ins, outs, sems, j):
    me, sibling, chips = _gather_peers()
    for a in range(len(ins)):
        _gather_copy(ins, outs, sems, a, 1 + j, (*chips[j], me[2]), me).wait_recv()
        _gather_copy(ins, outs, sems, a, 4 + j, (*chips[j], me[2]), sibling).start()


def _gather_finish(ins, outs, sems):
    me, sibling, chips = _gather_peers()
    for a in range(len(ins)):
        _gather_copy(ins, outs, sems, a, 0, sibling, me).wait_recv()
        for j, chip in enumerate(chips):
            _gather_copy(ins, outs, sems, a, 4 + j, (*chip, 1 - me[2]), me).wait_recv()
    for cp in _gather_first(ins, outs, sems):
        cp.wait_send()
    for a in range(len(ins)):
        for j, chip in enumerate(chips):
            _gather_copy(ins, outs, sems, a, 4 + j, (*chip, me[2]), sibling).wait_send()
        _gather_mine(ins, outs, sems, a).wait()


def _gather_scratch(n):
    return [pltpu.SemaphoreType.DMA((7 * n,)), pltpu.SemaphoreType.DMA((7 * n,)), pltpu.SemaphoreType.DMA((n,))]


def _gather_out_shape(shards, by_cols):
    return [
        jax.ShapeDtypeStruct((s.shape[0], N_DEV * s.shape[1]) if cols else (N_DEV, *s.shape), s.dtype)
        for s, cols in zip(shards, by_cols)
    ]


N_CHIP = 4


def _scatter_scratch(rows, cols):
    block = pltpu.VMEM((N_CHIP, rows, cols), BF16)
    dma = pltpu.SemaphoreType.DMA
    return [block, block, block, dma((N_CHIP,)), dma((N_CHIP,)), dma((N_CHIP,)), dma((N_CHIP - 1,)), dma((N_CHIP - 1,)), dma]


def _scatter_pair_copies(g_hbm, scr):
    x, y, c = _my_place()
    mine, theirs, _, a_send, a_recv, load_sem = scr[:6]
    to_sibling = [
        pltpu.make_async_remote_copy(
            src_ref=g_hbm.at[2 * q + (1 - c)], dst_ref=theirs.at[q], send_sem=a_send.at[q], recv_sem=a_recv.at[q],
            device_id=(x, y, 1 - c), device_id_type=MESH)
        for q in range(N_CHIP)
    ]
    loads = [pltpu.make_async_copy(g_hbm.at[2 * q + c], mine.at[q], load_sem.at[q]) for q in range(N_CHIP)]
    return to_sibling, loads


def _scatter_sum_copies(recv, scr):
    x, y, c = _my_place()
    sums, b_send, b_recv, own_sem = scr[2], scr[6], scr[7], scr[8]
    q_me = 2 * x + y
    to_owner = [
        pltpu.make_async_remote_copy(
            src_ref=sums.at[2 * px + py], dst_ref=recv.at[q_me], send_sem=b_send.at[j], recv_sem=b_recv.at[j],
            device_id=(px, py, c), device_id_type=MESH)
        for j, (px, py) in enumerate([(1 - x, y), (x, 1 - y), (1 - x, 1 - y)])
    ]
    return to_owner, pltpu.make_async_copy(sums.at[q_me], recv.at[q_me], own_sem)


def _scatter_start(g_hbm, scr):
    to_sibling, loads = _scatter_pair_copies(g_hbm, scr)
    for cp in to_sibling + loads:
        cp.start()


def _scatter_middle(g_hbm, recv, scr):
    to_sibling, loads = _scatter_pair_copies(g_hbm, scr)
    for cp in to_sibling:
        cp.wait_recv()
    for cp in loads:
        cp.wait()
    mine, theirs, sums = scr[:3]

    def step(r, carry):
        rs = pl.ds(pl.multiple_of(r * ROW_CHUNK, ROW_CHUNK), ROW_CHUNK)
        for q in range(N_CHIP):
            sums[q, rs, :] = (mine[q, rs, :].astype(F32) + theirs[q, rs, :].astype(F32)).astype(BF16)
        return carry

    lax.fori_loop(0, mine.shape[1] // ROW_CHUNK, step, 0)
    to_owner, own = _scatter_sum_copies(recv, scr)
    for cp in to_owner + [own]:
        cp.start()


def _scatter_finish(g_hbm, recv, scr):
    to_sibling, _ = _scatter_pair_copies(g_hbm, scr)
    to_owner, own = _scatter_sum_copies(recv, scr)
    for cp in to_owner:
        cp.wait_recv()
    for cp in to_sibling + to_owner:
        cp.wait_send()
    own.wait()


def _scatter_out_shape(gparts):
    return jax.ShapeDtypeStruct((N_CHIP, *gparts.shape[1:]), gparts.dtype)


def _adamw_math(w, g, m, v):
    m = ADAM_B1 * m + (1.0 - ADAM_B1) * g
    v = ADAM_B2 * v + (1.0 - ADAM_B2) * (g * g)
    m_hat = m / (1.0 - ADAM_B1**ADAM_STEP)
    v_hat = v / (1.0 - ADAM_B2**ADAM_STEP)
    delta = -ADAM_LR * (m_hat / (jnp.sqrt(v_hat) + ADAM_EPS) + ADAM_WD * w)
    return delta, m, v


ROW_CHUNK = 64


ELEMS_PER_STEP = 128 * 1024


N_SCATTER_SCRATCH = 9


def _reduce_scatter(gparts, name):
    n = len(gparts)

    def body(*refs):
        g_hbm, recv = refs[:n], refs[n : 2 * n]
        scr = [refs[2 * n + k * N_SCATTER_SCRATCH : 2 * n + (k + 1) * N_SCATTER_SCRATCH] for k in range(n)]
        for k in range(n):
            _scatter_start(g_hbm[k], scr[k])
        for k in range(n):
            _scatter_middle(g_hbm[k], recv[k], scr[k])
        for k in range(n):
            _scatter_finish(g_hbm[k], recv[k], scr[k])

    return pl.pallas_call(
        body,
        name=name,
        out_shape=[_scatter_out_shape(g) for g in gparts],
        in_specs=[ANY_SPEC] * n,
        out_specs=[ANY_SPEC] * n,
        scratch_shapes=[s for g in gparts for s in _scatter_scratch(*g.shape[1:])],
        compiler_params=pltpu.CompilerParams(vmem_limit_bytes=VMEM_LIMIT),
    )(*gparts)


def _sum_adamw(parts, w, m, v, name):
    _, rows, cols = w.shape
    rb = rows
    while rb * cols > ELEMS_PER_STEP and rb % 16 == 0:
        rb //= 2

    def body(p_ref, w_ref, m_ref, v_ref, grad_ref, delta_ref, nm_ref, nv_ref):
        g = p_ref[0].astype(F32)
        for k in range(1, p_ref.shape[0]):
            g = g + p_ref[k].astype(F32)
        delta, nm, nv = _adamw_math(w_ref[0], g, m_ref[0], v_ref[0])
        grad_ref[0] = g
        delta_ref[0] = delta
        nm_ref[0] = nm
        nv_ref[0] = nv

    block = lambda lead: pl.BlockSpec((lead, rb, cols), lambda i: (0, i, 0))
    out = jax.ShapeDtypeStruct(w.shape, F32)
    return pl.pallas_call(
        body,
        name=name,
        grid=(rows // rb,),
        out_shape=[out] * 4,
        in_specs=[block(parts.shape[0])] + [block(1)] * 3,
        out_specs=[block(1)] * 4,
        compiler_params=pltpu.CompilerParams(dimension_semantics=("arbitrary",), vmem_limit_bytes=VMEM_LIMIT),
    )(parts, w, m, v)


def _adamw_multi(items, name):
    n = len(items)

    def body(*refs):
        ins, outs = refs[: 4 * n], refs[4 * n :]
        for a in range(n):
            w_ref, g_ref, m_ref, v_ref = ins[4 * a : 4 * a + 4]
            d_ref, nm_ref, nv_ref = outs[3 * a : 3 * a + 3]
            delta, nm, nv = _adamw_math(w_ref[...], g_ref[...], m_ref[...], v_ref[...])
            d_ref[...] = delta
            nm_ref[...] = nm
            nv_ref[...] = nv

    flat = [a for it in items for a in it]
    out_shape = [jax.ShapeDtypeStruct(it[0].shape, F32) for it in items for _ in range(3)]
    outs = pl.pallas_call(
        body,
        name=name,
        out_shape=out_shape,
        in_specs=[VMEM_SPEC] * (4 * n),
        out_specs=[VMEM_SPEC] * (3 * n),
        compiler_params=pltpu.CompilerParams(vmem_limit_bytes=VMEM_LIMIT),
    )(*flat)
    return [tuple(outs[3 * a : 3 * a + 3]) for a in range(n)]


def _prologue(w_in, w_out, conv_w, c, w_ada, b_ada):
    D = c.shape[1]
    wc = w_ada.shape[2]
    shards16 = [jax.ShapeDtypeStruct(w_in.shape[1:], BF16), jax.ShapeDtypeStruct(w_out.shape[1:], BF16)]

    def to_all(src, out, sems):
        x, y, c = _my_place()
        me = _slot(x, y, c)
        copies = [
            pltpu.make_async_remote_copy(
                src_ref=src, dst_ref=out.at[me], send_sem=sems[0].at[k - 1], recv_sem=sems[1].at[k - 1],
                device_id=(x ^ (k >> 2), y ^ ((k >> 1) & 1), c ^ (k & 1)), device_id_type=MESH)
            for k in range(1, N_DEV)
        ]
        return copies, pltpu.make_async_copy(src, out.at[me], sems[2].at[0])

    def start(copies, own):
        for cp in copies + [own]:
            cp.start()

    def finish(copies, own):
        for cp in copies:
            cp.wait()
        own.wait()

    def body(win_ref, wout_ref, cw_ref, c_ref, wada_ref, b_ref, win_g, wout_g, cw_g, c_g, mod_g, mod_ref, ct_ref, win16, wout16, c_s, mp_s, *sems):
        c_copies = to_all(c_ref, c_g, sems[3:6])
        start(*c_copies)
        win16[...] = win_ref[0].astype(BF16)
        wout16[...] = wout_ref[0].astype(BF16)
        w_ins, w_outs, w_sems = (win16, wout16, cw_ref), (win_g, wout_g, cw_g), sems[0:3]
        _gather_start(w_ins, w_outs, w_sems)
        finish(*c_copies)
        for k in range(N_DEV):
            c_s[k : k + 1, :] = c_g[k]
        cv = c_s[...]
        ct_ref[...] = cv.T
        cond = cv * jax.nn.sigmoid(cv)
        b_mine = b_ref[:, pl.ds(pl.multiple_of(_slot(*_my_place()) * wc, 128), wc)]
        mp_s[...] = jnp.dot(cond, wada_ref[0], precision=lax.Precision.HIGHEST, preferred_element_type=F32) + b_mine
        m_copies = to_all(mp_s, mod_g, sems[6:9])
        start(*m_copies)
        for j in range(3):
            _gather_forward(w_ins, w_outs, w_sems, j)
        _gather_finish(w_ins, w_outs, w_sems)
        finish(*m_copies)
        me = _slot(*_my_place())
        for p in range(N_DEV):
            part = mod_g[p, pl.ds(me, 1), :]
            lo = 0
            while lo < wc:
                r, col = divmod(p * wc + lo, D)
                n = min(wc - lo, D - col)
                mod_ref[r : r + 1, col : col + n] = part[:, lo : lo + n]
                lo += n

    return pl.pallas_call(
        body,
        name="prologue",
        out_shape=_gather_out_shape(shards16 + [conv_w], [True, False, False])
        + [jax.ShapeDtypeStruct((N_DEV, 1, D), F32), jax.ShapeDtypeStruct((N_DEV, N_DEV, wc), F32),
           jax.ShapeDtypeStruct((N_DEV * wc // D, D), F32), jax.ShapeDtypeStruct((D, N_DEV), F32)],
        in_specs=[VMEM_SPEC, VMEM_SPEC, ANY_SPEC] + [VMEM_SPEC] * 3,
        out_specs=[ANY_SPEC] * 3 + [VMEM_SPEC] * 4,
        scratch_shapes=[pltpu.VMEM(s.shape, BF16) for s in shards16]
        + [pltpu.VMEM((N_DEV, D), F32), pltpu.VMEM((N_DEV, wc), F32)]
        + _gather_scratch(3) + _gather_scratch(1) + _gather_scratch(1),
        compiler_params=pltpu.CompilerParams(vmem_limit_bytes=VMEM_LIMIT),
    )(w_in, w_out, conv_w, c, w_ada, b_ada)


def _ln_fwd(r):
    mu = jnp.mean(r, axis=-1, keepdims=True)
    d = r - mu
    var = jnp.mean(d * d, axis=-1, keepdims=True)
    rstd = lax.rsqrt(var + LN_EPS)
    return d * rstd, rstd


def _ln_bwd(dxh, xhat, rstd):
    m1 = jnp.mean(dxh, axis=-1, keepdims=True)
    m2 = jnp.mean(dxh * xhat, axis=-1, keepdims=True)
    return rstd * (dxh - m1 - xhat * m2)


def _colsum(a):
    return jnp.sum(a, axis=0, keepdims=True)


def _window_sums(ext, tm, causal):
    n = ext.shape[0]
    lo = HALO if causal else 0
    s, out = ext, []
    for p in range(len(POOL_WINDOWS)):
        assert POOL_WINDOWS[p] == 2 ** (p + 1)
        k = 2**p
        s = s + pltpu.roll(s, k if causal else n - k, 0)
        out.append(s[lo : lo + tm, 0:128])
        if p + 1 < len(POOL_WINDOWS):
            s = s[:, 128:]
    return out


def _pool_features(vp, vp_s, row, tm):
    sums = _window_sums(vp_s[...], tm, causal=True)
    feats, inv_cnts = [], []
    for g, win in enumerate(POOL_WINDOWS):
        inv_cnt = 1.0 / jnp.minimum(row + 1, win).astype(F32)
        feats.append(sums[g] * inv_cnt - vp[:, 128 * g : 128 * g + 128])
        inv_cnts.append(inv_cnt)
    return feats, inv_cnts


def _f1(x, mod, w_in, conv_w, w_pool, pool_scale, w_out, tm, gather, by_cols):
    T, D = x.shape
    ZW = w_in.shape[1]
    CC = ZW // 4
    nt = T // tm
    ng = len(gather)
    fwd_steps = [max(nt - 3 + j, 0) for j in range(3)]

    shards16 = [jax.ShapeDtypeStruct(s.shape[1:], BF16) for s in gather]

    def body(*refs):
        x_ref, mod_ref, win_ref, cw_ref, wp_ref, ps_ref, wout_ref = refs[:7]
        g_f32 = refs[7 : 7 + ng]
        z_ref, h_ref, xhat_ref, rstd_ref, mix_ref = refs[7 + ng : 12 + ng]
        g_outs = refs[12 + ng : 12 + 2 * ng]
        cv_s, vp_s = refs[12 + 2 * ng : 14 + 2 * ng]
        g_ins = refs[14 + 2 * ng : 14 + 3 * ng]
        g_sems = refs[14 + 3 * ng :]
        i = pl.program_id(0)

        @pl.when(i == 0)
        def _():
            for src, dst in zip(g_f32, g_ins):
                dst[...] = src[0].astype(BF16)
            _gather_start(g_ins, g_outs, g_sems)
            cv_s[0:HALO, :] = jnp.zeros((HALO, CC), F32)
            vp_s[0:HALO, :] = jnp.zeros((HALO, CC), F32)

        xv = x_ref[...]
        sh1, sc1, g1 = mod_ref[0:1, :], mod_ref[1:2, :], mod_ref[2:3, :]
        h = (xv * (1.0 + sc1) + sh1).astype(BF16)
        h_ref[...] = h
        z = jnp.dot(h, win_ref[...], preferred_element_type=F32)
        z_ref[...] = z.astype(BF16)
        gb, gc, vc, vp = z[:, 0:CC], z[:, CC : 2 * CC], z[:, 2 * CC : 3 * CC], z[:, 3 * CC : 4 * CC]
        cv = gc * vc
        cv_s[HALO : HALO + tm, :] = cv
        vp_s[HALO : HALO + tm, :] = vp
        cv_ext = cv_s[...]
        cv_m2 = pltpu.roll(cv_ext, 2, 0)[HALO : HALO + tm, :]
        cv_m1 = pltpu.roll(cv_ext, 1, 0)[HALO : HALO + tm, :]
        conv = cw_ref[0:1, :] * cv_m2 + cw_ref[1:2, :] * cv_m1 + cw_ref[2:3, :] * cv
        parts = [gb * conv]
        row = i * tm + lax.broadcasted_iota(jnp.int32, (tm, 1), 0)
        feats, _ = _pool_features(vp, vp_s, row, tm)
        for g in range(len(POOL_WINDOWS)):
            pw = jnp.dot(feats[g].astype(BF16), wp_ref[g].astype(BF16), preferred_element_type=F32)
            parts.append(pw * ps_ref[0:1, 128 * g : 128 * g + 128])
        cv_s[0:HALO, :] = cv_s[tm : tm + HALO, :]
        vp_s[0:HALO, :] = vp_s[tm : tm + HALO, :]
        ycat = jnp.concatenate(parts, axis=1).astype(BF16)
        mix = jnp.dot(ycat, wout_ref[...], preferred_element_type=F32)
        mix_ref[...] = mix
        xhat, rstd = _ln_fwd(DEEPNORM_ALPHA * xv + (1.0 + g1) * mix)
        xhat_ref[...] = xhat
        rstd_ref[...] = rstd

        for j in range(3):

            @pl.when(i == fwd_steps[j])
            def _(j=j):
                _gather_forward(g_ins, g_outs, g_sems, j)

        @pl.when(i == nt - 1)
        def _():
            _gather_finish(g_ins, g_outs, g_sems)

    tile = lambda w: pl.BlockSpec((tm, w), lambda i: (i, 0))
    return pl.pallas_call(
        body,
        name="f1",
        grid=(nt,),
        out_shape=[
            jax.ShapeDtypeStruct((T, ZW), BF16),
            jax.ShapeDtypeStruct((T, D), BF16),
            jax.ShapeDtypeStruct((T, D), F32),
            jax.ShapeDtypeStruct((T, 1), F32),
            jax.ShapeDtypeStruct((T, D), F32),
        ]
        + _gather_out_shape(shards16, by_cols),
        in_specs=[tile(D)] + [VMEM_SPEC] * (6 + ng),
        out_specs=[tile(ZW), tile(D), tile(D), tile(1), tile(D)] + [ANY_SPEC] * ng,
        scratch_shapes=[pltpu.VMEM((HALO + tm, CC), F32), pltpu.VMEM((HALO + tm, CC), F32)]
        + [pltpu.VMEM(s.shape, BF16) for s in shards16]
        + _gather_scratch(ng),
        compiler_params=pltpu.CompilerParams(dimension_semantics=("arbitrary",), vmem_limit_bytes=VMEM_LIMIT),
    )(x, mod, w_in, conv_w, w_pool, pool_scale, w_out, *gather)


def _fb2(xhat1, target, mod, ln, w_mi, w_mo, tm):
    T, D = xhat1.shape
    H = w_mi.shape[1]
    hc = min(1024, H)
    nb = H // hc
    nt = T // tm

    def body(xh1_ref, t_ref, mod_ref, ln_ref, wmi_ref, wmo_ref, dx1_ref, h2_ref, a_ref, du_ref, df_ref, acc_ref):
        i = pl.program_id(0)

        @pl.when(i == 0)
        def _():
            acc_ref[...] = jnp.zeros((8, D), F32)

        sh2, sc2, g2 = mod_ref[3:4, :], mod_ref[4:5, :], mod_ref[5:6, :]
        x1 = xh1_ref[...] * ln_ref[0:1, :] + ln_ref[1:2, :]
        h2 = (x1 * (1.0 + sc2) + sh2).astype(BF16)
        h2_ref[...] = h2
        f = jnp.zeros((tm, D), F32)
        for k in range(nb):
            ks = slice(k * hc, (k + 1) * hc)
            r = jnp.maximum(jnp.dot(h2, wmi_ref[:, ks], preferred_element_type=F32), 0.0)
            du_ref[:, ks] = r.astype(BF16)
            a = (r * r).astype(BF16)
            a_ref[:, ks] = a
            f = f + jnp.dot(a, wmo_ref[ks, :], preferred_element_type=F32)
        xhat2, rstd2 = _ln_fwd(DEEPNORM_ALPHA * x1 + (1.0 + g2) * f)
        ln2_g = ln_ref[2:3, :]
        d = xhat2 * ln2_g + ln_ref[3:4, :] - t_ref[...]
        dr2 = _ln_bwd(d * (ln2_g * (1.0 / D)), xhat2, rstd2)
        df = ((1.0 + g2) * dr2).astype(BF16)
        df_ref[...] = df
        dh2 = jnp.zeros((tm, D), F32)
        for k in range(nb):
            ks = slice(k * hc, (k + 1) * hc)
            da = lax.dot_general(df, wmo_ref[ks, :], NT, preferred_element_type=F32)
            du = (da * (2.0 * du_ref[:, ks].astype(F32))).astype(BF16)
            du_ref[:, ks] = du
            dh2 = dh2 + lax.dot_general(du, wmi_ref[:, ks], NT, preferred_element_type=F32)
        dx1_ref[...] = DEEPNORM_ALPHA * dr2 + dh2 * (1.0 + sc2)
        acc_ref[0:1, :] += _colsum(d * xhat2) * (1.0 / D)
        acc_ref[1:2, :] += _colsum(d) * (1.0 / D)
        acc_ref[2:3, :] += _colsum(dh2)
        acc_ref[3:4, :] += _colsum(dh2 * x1)
        acc_ref[4:5, :] += _colsum(dr2 * f)
        acc_ref[5:6, :] += jnp.zeros((1, D), F32) + (0.5 / D) * jnp.sum(d * d)

    tile = lambda w: pl.BlockSpec((tm, w), lambda i: (i, 0))
    return pl.pallas_call(
        body,
        name="fb2",
        grid=(nt,),
        out_shape=[
            jax.ShapeDtypeStruct((T, D), F32),
            jax.ShapeDtypeStruct((T, D), BF16),
            jax.ShapeDtypeStruct((T, H), BF16),
            jax.ShapeDtypeStruct((T, H), BF16),
            jax.ShapeDtypeStruct((T, D), BF16),
            jax.ShapeDtypeStruct((8, D), F32),
        ],
        in_specs=[tile(D), tile(D)] + [VMEM_SPEC] * 4,
        out_specs=[tile(D), tile(D), tile(H), tile(H), tile(D), pl.BlockSpec((8, D), lambda i: (0, 0))],
        compiler_params=pltpu.CompilerParams(dimension_semantics=("arbitrary",), vmem_limit_bytes=VMEM_LIMIT),
    )(xhat1, target, mod, ln, w_mi, w_mo)


def _b1(dx1, xhat1, rstd1, x, mix, z, mod, ln, w_out, w_in, conv_w, w_pool, pool_scale, tm):
    T, D = x.shape
    ZW = w_in.shape[1]
    CC = ZW // 4
    nt = T // tm
    hb = tm // HALO

    def body(dx1_ref, xh1_ref, rstd_ref, x_ref, mix_ref, z_ref, zh_ref, mod_ref, ln_ref, wout_ref, win_ref, cw_ref, wp_ref, ps_ref,
             dx_ref, dmix_ref, ycat_ref, dz_ref, acc_ref, gcw_ref, gwp_ref, cv_s, vp_s, e_s, q_s):
        i = pl.program_id(0)
        j = nt - 1 - i

        @pl.when(i == 0)
        def _():
            acc_ref[...] = jnp.zeros((8, D), F32)
            gcw_ref[...] = jnp.zeros((8, CC), F32)
            gwp_ref[...] = jnp.zeros(gwp_ref.shape, F32)
            e_s[tm : tm + HALO, :] = jnp.zeros((HALO, CC), F32)
            q_s[tm : tm + HALO, :] = jnp.zeros((HALO, CC), F32)

        sh1, sc1, g1 = mod_ref[0:1, :], mod_ref[1:2, :], mod_ref[2:3, :]
        dx1 = dx1_ref[...]
        xhat1 = xh1_ref[...]
        acc_ref[0:1, :] += _colsum(dx1 * xhat1)
        acc_ref[1:2, :] += _colsum(dx1)
        dr1 = _ln_bwd(dx1 * ln_ref[0:1, :], xhat1, rstd_ref[...])
        acc_ref[4:5, :] += _colsum(dr1 * mix_ref[...])
        dmix = ((1.0 + g1) * dr1).astype(BF16)
        dmix_ref[...] = dmix
        dycat = lax.dot_general(dmix, wout_ref[...], NT, preferred_element_type=F32)

        z = z_ref[...].astype(F32)
        zh = zh_ref[...].astype(F32) * jnp.where(j > 0, 1.0, 0.0)
        gb, gc, vc, vp = z[:, 0:CC], z[:, CC : 2 * CC], z[:, 2 * CC : 3 * CC], z[:, 3 * CC : 4 * CC]
        cv = gc * vc
        cv_s[0:HALO, :] = zh[:, CC : 2 * CC] * zh[:, 2 * CC : 3 * CC]
        cv_s[HALO : HALO + tm, :] = cv
        vp_s[0:HALO, :] = zh[:, 3 * CC : 4 * CC]
        vp_s[HALO : HALO + tm, :] = vp
        cv_ext = cv_s[...]
        cv_m2 = pltpu.roll(cv_ext, 2, 0)[HALO : HALO + tm, :]
        cv_m1 = pltpu.roll(cv_ext, 1, 0)[HALO : HALO + tm, :]
        w0, w1, w2 = cw_ref[0:1, :], cw_ref[1:2, :], cw_ref[2:3, :]
        conv = w0 * cv_m2 + w1 * cv_m1 + w2 * cv
        dyc = dycat[:, 0:CC]
        e = dyc * gb
        e_s[0:tm, :] = e
        e_ext = e_s[...]
        dcv = w2 * e + w1 * pltpu.roll(e_ext, tm + HALO - 1, 0)[0:tm, :] + w0 * pltpu.roll(e_ext, tm + HALO - 2, 0)[0:tm, :]
        gcw_ref[0:1, :] += _colsum(e * cv_m2)
        gcw_ref[1:2, :] += _colsum(e * cv_m1)
        gcw_ref[2:3, :] += _colsum(e * cv)
        y_parts = [gb * conv]
        dz_parts = [dyc * conv, dcv * vc, dcv * gc]

        row = j * tm + lax.broadcasted_iota(jnp.int32, (tm, 1), 0)
        feats, inv_cnts = _pool_features(vp, vp_s, row, tm)
        gps_parts, dps = [], []
        for g in range(len(POOL_WINDOWS)):
            cols = slice(128 * g, 128 * g + 128)
            p = feats[g].astype(BF16)
            scale = ps_ref[0:1, cols]
            wp = wp_ref[g].astype(BF16)
            pw = jnp.dot(p, wp, preferred_element_type=F32)
            y_parts.append(pw * scale)
            dyp = dycat[:, CC + 128 * g : CC + 128 * g + 128]
            gps_parts.append(_colsum(dyp * pw))
            dpw = (dyp * scale).astype(BF16)
            gwp_ref[g] += lax.dot_general(p, dpw, TN, preferred_element_type=F32)
            dp = lax.dot_general(dpw, wp, NT, preferred_element_type=F32)
            q_s[0:tm, cols] = dp * inv_cnts[g]
            dps.append(dp)
        sq = _window_sums(q_s[...], tm, causal=False)
        dz_parts += [sq[g] - dps[g] for g in range(len(POOL_WINDOWS))]
        gcw_ref[3:4, :] += jnp.concatenate(gps_parts, axis=1)
        ycat_ref[...] = jnp.concatenate(y_parts, axis=1).astype(BF16)
        dz = jnp.concatenate(dz_parts, axis=1).astype(BF16)
        dz_ref[...] = dz
        dh = lax.dot_general(dz, win_ref[...], NT, preferred_element_type=F32)
        acc_ref[2:3, :] += _colsum(dh)
        acc_ref[3:4, :] += _colsum(dh * x_ref[...])
        dx_ref[...] = DEEPNORM_ALPHA * dr1 + dh * (1.0 + sc1)
        e_s[tm : tm + HALO, :] = e_s[0:HALO, :]
        q_s[tm : tm + HALO, :] = q_s[0:HALO, :]

    tile = lambda w: pl.BlockSpec((tm, w), lambda i: (nt - 1 - i, 0))
    halo = pl.BlockSpec((HALO, ZW), lambda i: (jnp.maximum((nt - 1 - i) * hb - 1, 0), 0))
    fixed = lambda shape: pl.BlockSpec(shape, lambda i: (0,) * len(shape))
    return pl.pallas_call(
        body,
        name="b1",
        grid=(nt,),
        out_shape=[
            jax.ShapeDtypeStruct((T, D), F32),
            jax.ShapeDtypeStruct((T, D), BF16),
            jax.ShapeDtypeStruct((T, D), BF16),
            jax.ShapeDtypeStruct((T, ZW), BF16),
            jax.ShapeDtypeStruct((8, D), F32),
            jax.ShapeDtypeStruct((8, CC), F32),
            jax.ShapeDtypeStruct(w_pool.shape, F32),
        ],
        in_specs=[tile(D), tile(D), tile(1), tile(D), tile(D), tile(ZW), halo] + [VMEM_SPEC] * 7,
        out_specs=[tile(D), tile(D), tile(D), tile(ZW), fixed((8, D)), fixed((8, CC)), fixed(w_pool.shape)],
        scratch_shapes=[
            pltpu.VMEM((HALO + tm, CC), F32),
            pltpu.VMEM((HALO + tm, CC), F32),
            pltpu.VMEM((tm + HALO, CC), F32),
            pltpu.VMEM((tm + HALO, CC), F32),
        ],
        compiler_params=pltpu.CompilerParams(dimension_semantics=("arbitrary",), vmem_limit_bytes=VMEM_LIMIT),
    )(dx1, xhat1, rstd1, x, mix, z, z, mod, ln, w_out, w_in, conv_w, w_pool, pool_scale)


def _wgrad(a, b, bk, n_groups, bt, name, owners=None, scatter=None, gather=()):
    T, K = a.shape
    N = b.shape[1]
    nk, nt, ng = K // bk, T // bt, N // n_groups
    nc = min(512, ng)
    ns, ngat = (0 if scatter is None else 1), len(gather)
    n_steps = nk * n_groups * nt
    mid_step = min(1, n_steps - 1)
    fwd_steps = [min(2 * (j + 1), n_steps - 1) for j in range(3)]

    def body(*refs):
        a_ref, b_ref = refs[0], refs[1]
        g_ins = refs[2 + ns : 2 + ns + ngat]
        outs = refs[2 + ns + ngat :]
        o_ref, g_outs = outs[0], outs[1 + ns : 1 + ns + ngat]
        scr = outs[1 + ns + ngat :]
        acc = scr[0]
        if ns:
            s_hbm, s_recv, s_scr = refs[2], outs[1], scr[1 : 1 + N_SCATTER_SCRATCH]
        g_sems = scr[1 + N_SCATTER_SCRATCH * ns :]
        kk, gg, t = pl.program_id(0), pl.program_id(1), pl.program_id(2)
        step = (kk * n_groups + gg) * nt + t

        if ngat:

            @pl.when(step == 0)
            def _():
                _gather_start(g_ins, g_outs, g_sems)

            for j in range(3):

                @pl.when(step == fwd_steps[j])
                def _(j=j):
                    _gather_forward(g_ins, g_outs, g_sems, j)

        if ns:

            @pl.when(step == 0)
            def _():
                _scatter_start(s_hbm, s_scr)

            @pl.when(step == mid_step)
            def _():
                _scatter_middle(s_hbm, s_recv, s_scr)

        @pl.when(t == 0)
        def _():
            acc[...] = jnp.zeros(acc.shape, F32)

        at = a_ref[...].T
        for c in range(ng // nc):
            cs = slice(c * nc, (c + 1) * nc)
            acc[:, cs] += jnp.dot(at, b_ref[:, cs], preferred_element_type=F32)

        @pl.when(t == nt - 1)
        def _():
            if owners is None:
                o_ref[...] = acc[...].astype(BF16)
            else:
                per = N // owners
                for o in range(ng // per):
                    o_ref[o] = acc[:, o * per : (o + 1) * per].astype(BF16)

        if ns:

            @pl.when(step == n_steps - 1)
            def _():
                _scatter_finish(s_hbm, s_recv, s_scr)

        if ngat:

            @pl.when(step == n_steps - 1)
            def _():
                _gather_finish(g_ins, g_outs, g_sems)

    if owners is None:
        out_shape = [jax.ShapeDtypeStruct((K, N), BF16)]
        out_specs = [pl.BlockSpec((bk, ng), lambda k, g, t: (k, g))]
    else:
        assert bk == K
        per = N // owners
        out_shape = [jax.ShapeDtypeStruct((owners, K, per), BF16)]
        out_specs = [pl.BlockSpec((ng // per, K, per), lambda k, g, t: (g, 0, 0))]
    ins, in_specs = [a, b], [pl.BlockSpec((bt, bk), lambda k, g, t: (t, k)), pl.BlockSpec((bt, ng), lambda k, g, t: (t, g))]
    scratch = [pltpu.VMEM((bk, ng), F32)]
    if ns:
        ins.append(scatter)
        in_specs.append(ANY_SPEC)
        out_shape.append(_scatter_out_shape(scatter))
        out_specs.append(ANY_SPEC)
        scratch += _scatter_scratch(*scatter.shape[1:])
    if ngat:
        ins += list(gather)
        in_specs += [ANY_SPEC] * ngat
        out_shape += _gather_out_shape(gather, [False] * ngat)
        out_specs += [ANY_SPEC] * ngat
        scratch += _gather_scratch(ngat)
    outs = pl.pallas_call(
        body,
        name=name,
        grid=(nk, n_groups, nt),
        out_shape=out_shape,
        in_specs=in_specs,
        out_specs=out_specs,
        scratch_shapes=scratch,
        compiler_params=pltpu.CompilerParams(dimension_semantics=("arbitrary", "arbitrary", "arbitrary"), vmem_limit_bytes=VMEM_LIMIT),
    )(*ins)
    return outs if ns + ngat else outs[0]


def _wgrad_pair(a0, b0, a1, b1, bt, name, owners, scatter, gather):
    T, K0 = a0.shape
    N0 = b0.shape[1]
    K1, N1 = a1.shape[1], b1.shape[1]
    nt = T // bt
    nc = 512
    per = N0 // owners
    ngat = len(gather)
    n_steps = 2 * nt
    fwd_steps = [min(2 * (j + 1), n_steps - 1) for j in range(3)]

    def body(*refs):
        a0_ref, b0_ref, a1_ref, b1_ref, s_hbm = refs[:5]
        g_ins = refs[5 : 5 + ngat]
        o0_ref, o1_ref, s_recv = refs[5 + ngat : 8 + ngat]
        g_outs = refs[8 + ngat : 8 + 2 * ngat]
        scr = refs[8 + 2 * ngat :]
        acc0, acc1 = scr[0], scr[1]
        s_scr, g_sems = scr[2 : 2 + N_SCATTER_SCRATCH], scr[2 + N_SCATTER_SCRATCH :]
        p, t = pl.program_id(0), pl.program_id(1)
        step = p * nt + t

        @pl.when(step == 0)
        def _():
            _scatter_start(s_hbm, s_scr)
            _gather_start(g_ins, g_outs, g_sems)

        @pl.when(step == 1)
        def _():
            _scatter_middle(s_hbm, s_recv, s_scr)

        for j in range(3):

            @pl.when(step == fwd_steps[j])
            def _(j=j):
                _gather_forward(g_ins, g_outs, g_sems, j)

        def accumulate(a_ref, b_ref, acc, n_cols):
            @pl.when(t == 0)
            def _():
                acc[...] = jnp.zeros(acc.shape, F32)

            at = a_ref[...].T
            for c in range(n_cols // nc):
                cs = slice(c * nc, (c + 1) * nc)
                acc[:, cs] += jnp.dot(at, b_ref[:, cs], preferred_element_type=F32)

        @pl.when(p == 0)
        def _():
            accumulate(a0_ref, b0_ref, acc0, N0)

            @pl.when(t == nt - 1)
            def _():
                for o in range(owners):
                    o0_ref[o] = acc0[:, o * per : (o + 1) * per].astype(BF16)

        @pl.when(p == 1)
        def _():
            accumulate(a1_ref, b1_ref, acc1, N1)

            @pl.when(t == nt - 1)
            def _():
                o1_ref[...] = acc1[...].astype(BF16)

        @pl.when(step == n_steps - 1)
        def _():
            _scatter_finish(s_hbm, s_recv, s_scr)
            _gather_finish(g_ins, g_outs, g_sems)

    first = lambda w: pl.BlockSpec((bt, w), lambda p, t: (jnp.where(p == 0, t, nt - 1), 0))
    second = lambda w: pl.BlockSpec((bt, w), lambda p, t: (jnp.where(p == 1, t, 0), 0))
    return pl.pallas_call(
        body,
        name=name,
        grid=(2, nt),
        out_shape=[jax.ShapeDtypeStruct((owners, K0, per), BF16), jax.ShapeDtypeStruct((K1, N1), BF16), _scatter_out_shape(scatter)]
        + _gather_out_shape(gather, [False] * ngat),
        in_specs=[first(K0), first(N0), second(K1), second(N1), ANY_SPEC] + [ANY_SPEC] * ngat,
        out_specs=[pl.BlockSpec((owners, K0, per), lambda p, t: (0, 0, 0)), pl.BlockSpec((K1, N1), lambda p, t: (0, 0)), ANY_SPEC]
        + [ANY_SPEC] * ngat,
        scratch_shapes=[pltpu.VMEM((K0, N0), F32), pltpu.VMEM((K1, N1), F32)]
        + _scatter_scratch(*scatter.shape[1:]) + _gather_scratch(ngat),
        compiler_params=pltpu.CompilerParams(dimension_semantics=("arbitrary", "arbitrary"), vmem_limit_bytes=VMEM_LIMIT),
    )(a0, b0, a1, b1, scatter, *gather)


def _small_grads(acc1_all, acc2_all, gcw_all, gwp_all, cond_t, my_slot, w_ada, m_w_ada, v_w_ada):
    D = acc1_all.shape[2]
    w_cols = w_ada.shape[2]
    n_chunk = D // 128
    q_mine = w_cols // 128

    def total(ref, r):
        s = ref[0, r : r + 1, :]
        for k in range(1, N_DEV):
            s = s + ref[k, r : r + 1, :]
        return s

    def body(slot_ref, a1_ref, a2_ref, gcw_ref, gwp_ref, ct_ref, w_ref, m_ref, v_ref,
             gb_ref, gw_ref, gl1g_ref, gl1b_ref, gl2g_ref, gl2b_ref, gcwo_ref, gps_ref, gwpo_ref, loss_ref,
             dw_ref, nm_ref, nv_ref, dm_s):
        @pl.when(pl.program_id(0) == 0)
        def _():
            loss_ref[...] = total(a2_ref, 5)[:, 0:1]
            for s, (ref, r) in enumerate([(a1_ref, 2), (a1_ref, 3), (a1_ref, 4), (a2_ref, 2), (a2_ref, 3), (a2_ref, 4)]):
                gb_ref[0:1, s * D : (s + 1) * D] = total(ref, r)
                for k in range(N_DEV):
                    row = ref[k, r : r + 1, :]
                    for qq in range(n_chunk):
                        dm_s[s * n_chunk + qq, k : k + 1, :] = row[:, 128 * qq : 128 * qq + 128]
            gl1g_ref[...] = total(a1_ref, 0)
            gl1b_ref[...] = total(a1_ref, 1)
            gl2g_ref[...] = total(a2_ref, 0)
            gl2b_ref[...] = total(a2_ref, 1)
            gcwo_ref[...] = jnp.zeros(gcwo_ref.shape, F32)
            for r in range(3):
                gcwo_ref[r : r + 1, :] = total(gcw_ref, r)
            gps_ref[...] = total(gcw_ref, 3)
            wp = gwp_ref[0]
            for k in range(1, N_DEV):
                wp = wp + gwp_ref[k]
            gwpo_ref[0] = wp

        ct = ct_ref[...]
        cond_t = ct * jax.nn.sigmoid(ct)
        q0 = slot_ref[0] * q_mine
        for q in range(q_mine):
            dm = dm_s[q0 + q]
            out = cond_t[:, 0:1] * dm[0:1, :]
            for k in range(1, N_DEV):
                out = out + cond_t[:, k : k + 1] * dm[k : k + 1, :]
            cols = slice(128 * q, 128 * q + 128)
            gw_ref[0, :, cols] = out
            delta, nm, nv = _adamw_math(w_ref[0, :, cols], out, m_ref[0, :, cols], v_ref[0, :, cols])
            dw_ref[0, :, cols] = delta
            nm_ref[0, :, cols] = nm
            nv_ref[0, :, cols] = nv

    CC = gcw_all.shape[2]
    w_like = jax.ShapeDtypeStruct(w_ada.shape, F32)
    rb = D // 4
    w_spec = pl.BlockSpec((1, rb, w_cols), lambda i: (0, i, 0))
    whole = lambda shape: pl.BlockSpec(shape, lambda i: (0,) * len(shape))
    out_shape = [
        jax.ShapeDtypeStruct((1, 6 * D), F32),
        w_like,
        *[jax.ShapeDtypeStruct((1, D), F32)] * 4,
        jax.ShapeDtypeStruct((8, CC), F32),
        jax.ShapeDtypeStruct((1, CC), F32),
        jax.ShapeDtypeStruct((1, *gwp_all.shape[1:]), F32),
        jax.ShapeDtypeStruct((1, 1), F32),
        w_like,
        w_like,
        w_like,
    ]
    return pl.pallas_call(
        body,
        name="small_grads",
        grid=(D // rb,),
        out_shape=out_shape,
        in_specs=[pl.BlockSpec(memory_space=pltpu.SMEM)]
        + [whole(t.shape) for t in (acc1_all, acc2_all, gcw_all, gwp_all)]
        + [pl.BlockSpec((rb, N_DEV), lambda i: (i, 0))] + [w_spec] * 3,
        out_specs=[w_spec if o is w_like else whole(o.shape) for o in out_shape],
        scratch_shapes=[pltpu.VMEM((6 * n_chunk, N_DEV, 128), F32)],
        compiler_params=pltpu.CompilerParams(dimension_semantics=("arbitrary",), vmem_limit_bytes=VMEM_LIMIT),
    )(my_slot, acc1_all, acc2_all, gcw_all, gwp_all, cond_t, w_ada, m_w_ada, v_w_ada)


def kernel(x, c, w_ada, b_ada, w_in, conv_w, w_pool, pool_scale, w_out, ln1_g, ln1_b, w_mlp_in, w_mlp_out, ln2_g, ln2_b, loss_target, m_w_ada, m_b_ada, m_w_in, m_conv_w, m_w_pool, m_pool_scale, m_w_out, m_ln1_g, m_ln1_b, m_w_mlp_in, m_w_mlp_out, m_ln2_g, m_ln2_b, v_w_ada, v_b_ada, v_w_in, v_conv_w, v_w_pool, v_pool_scale, v_w_out, v_ln1_g, v_ln1_b, v_w_mlp_in, v_w_mlp_out, v_ln2_g, v_ln2_b):
    T, D = x.shape[1], x.shape[2]
    H = w_mlp_out.shape[1] * N_DEV
    ZW = w_in.shape[2] * N_DEV
    CC = ZW // 4
    tm = min(512, T // 2)
    bt = min(1024, T)
    ax, ay, ac = _my_place()
    me = _slot(ax, ay, ac)

    cw3, m_cw3, v_cw3 = (jnp.transpose(t, (1, 0, 2)) for t in (conv_w, m_conv_w, v_conv_w))
    w_in_f, w_out_g, cw_g, _, _, mod, c_t = _prologue(w_in, w_out, cw3, c, w_ada, b_ada)
    w_out_f = w_out_g.reshape(D, D)
    conv_w_f = jnp.transpose(cw_g[:, :, 0, :], (1, 0, 2)).reshape(conv_w.shape[1], CC)

    ln = jnp.concatenate([ln1_g, ln1_b, ln2_g, ln2_b], axis=0)
    xs, target = x[0], loss_target[0]

    z, h, xhat1, rstd1, mix, w_mi_f, w_mo_g = _f1(
        xs, mod, w_in_f, conv_w_f, w_pool[0], pool_scale, w_out_f, tm,
        [w_mlp_in, w_mlp_out], [True, False])
    dx1, h2, a, du, df, acc2 = _fb2(xhat1, target, mod, ln, w_mi_f, w_mo_g.reshape(H, D), tm)

    grad_x, dmix, ycat, dz, acc1, gcw, gwp = _b1(
        dx1, xhat1, rstd1, xs, mix, z, mod, ln, w_out_f, w_in_f, conv_w_f, w_pool[0], pool_scale, tm)

    gp_mo = _wgrad(a, df, D, 1, min(4 * bt, T), "wgrad_mlp_out").reshape(N_DEV, H // N_DEV, D)
    gp_mi, rv_mo = _wgrad(h2, du, D, 2, min(2 * bt, T), "wgrad_mlp_in", owners=N_DEV, scatter=gp_mo)
    gp_in, gp_out, rv_mi, acc1_g, acc2_g, gcw_g, gwp_g = _wgrad_pair(
        h, dz, ycat, dmix, bt, "wgrad_in_out", N_DEV, gp_mi, [acc1, acc2, gcw, gwp])
    rv_out, rv_in = _reduce_scatter([gp_out.reshape(N_DEV, D // N_DEV, D), gp_in], "scatter_w_in_out")
    g_b_ada, g_w_ada, g_ln1_g, g_ln1_b, g_ln2_g, g_ln2_b, g_cw, g_pool_scale, g_w_pool, loss, *u_w_ada = _small_grads(
        acc1_g, acc2_g, gcw_g, gwp_g, c_t, jnp.reshape(me, (1,)).astype(jnp.int32), w_ada, m_w_ada, v_w_ada)
    cc_mine = conv_w.shape[2]
    g_cw3 = lax.dynamic_slice(g_cw, (0, me * cc_mine), (conv_w.shape[1], cc_mine))[:, None, :]
    g_conv_w = jnp.transpose(g_cw3, (1, 0, 2))

    small = _adamw_multi(
        [
            (b_ada, g_b_ada, m_b_ada, v_b_ada),
            (cw3, g_cw3, m_cw3, v_cw3),
            (w_pool, g_w_pool, m_w_pool, v_w_pool),
            (pool_scale, g_pool_scale, m_pool_scale, v_pool_scale),
            (ln1_g, g_ln1_g, m_ln1_g, v_ln1_g),
            (ln1_b, g_ln1_b, m_ln1_b, v_ln1_b),
            (ln2_g, g_ln2_g, m_ln2_g, v_ln2_g),
            (ln2_b, g_ln2_b, m_ln2_b, v_ln2_b),
        ],
        "adamw_small")
    u_b_ada, u_cw3, u_w_pool, u_pool_scale, u_ln1_g, u_ln1_b, u_ln2_g, u_ln2_b = small
    u_conv_w = tuple(jnp.transpose(t, (1, 0, 2)) for t in u_cw3)

    g_w_mo, *u_w_mo = _sum_adamw(rv_mo, w_mlp_out, m_w_mlp_out, v_w_mlp_out, "sum_w_mlp_out")
    g_w_mi, *u_w_mi = _sum_adamw(rv_mi, w_mlp_in, m_w_mlp_in, v_w_mlp_in, "sum_w_mlp_in")
    g_w_in, *u_w_in = _sum_adamw(rv_in, w_in, m_w_in, v_w_in, "sum_w_in")
    g_w_out, *u_w_out = _sum_adamw(rv_out, w_out, m_w_out, v_w_out, "sum_w_out")

    grads = [g_w_ada, g_b_ada, g_w_in, g_conv_w, g_w_pool, g_pool_scale, g_w_out, g_ln1_g, g_ln1_b, g_w_mi, g_w_mo, g_ln2_g, g_ln2_b]
    updates = [u_w_ada, u_b_ada, u_w_in, u_conv_w, u_w_pool, u_pool_scale, u_w_out, u_ln1_g, u_ln1_b, u_w_mi, u_w_mo, u_ln2_g, u_ln2_b]
    deltas = [u[0] for u in updates]
    new_m = [u[1] for u in updates]
    new_v = [u[2] for u in updates]
    return (loss.reshape(()), grad_x[None], *grads, *deltas, *new_m, *new_v)
```
